```python
import math
import jax
import jax.numpy as jnp
from jax import lax
import numpy as np

D_MODEL = 1024
BATCH = 16
SEQ = 256
DEPTH = 4
DEC_BATCH = 4
DEC_SEQ = 2048
PAST_LEN = 512

GRID_W = 64
N_EVEN = (DEPTH + 1) // 2
N_ODD = DEPTH // 2
EPS = 1e-6
NEG_INF = -1e30

DN_HEADS = 4
DN_DK = 128
DN_DV = 128
DN_QK_W = DN_HEADS * DN_DK
DN_V_W = DN_HEADS * DN_DV
CONV_K = 3
DN_CHUNK = 64

NA_HEADS = 8
NA_HD = 64
NA_W = NA_HEADS * NA_HD
NA_ROWS = 8
NA_COLS = 16
Q_BLOCK = 128

IN_COLS = 2 * DN_QK_W + 2 * DN_V_W + 4 * DN_HEADS + 3 * NA_W
MIX_W = DN_V_W + NA_W

SG_CHUNK = 128
SG_GROUPS = 8
SG_W = 2 * D_MODEL
SG_GW = SG_W // SG_GROUPS

N_EGROUPS = 4
EXP_PER_GROUP = 8
N_EXPERTS = N_EGROUPS * EXP_PER_GROUP
TOP_K = 2
D_EXPERT = D_MODEL // 2
MOE_BLOCK = 256

kernel_name = 'hybrid_deltanet_natten_sgu_hmoe_step'


def rmsnorm(x, g):
    xf = x.astype(jnp.float32)
    y = xf * lax.rsqrt(jnp.mean(xf * xf, axis=-1, keepdims=True) + EPS)
    return (y * g.astype(jnp.float32)).astype(x.dtype)


def ada_mod(cond, w, b):
    m = jax.nn.silu(cond) @ w + b
    return jnp.split(m[:, None, :], 6, axis=-1)


def l2norm(x):
    return x * lax.rsqrt(jnp.sum(x * x, axis=-1, keepdims=True) + EPS)


def short_conv(x, w):
    y = lax.conv_general_dilated(
        x, w[:, None, :].astype(x.dtype), window_strides=(1,),
        padding=[(CONV_K // 2, CONV_K // 2)],
        dimension_numbers=('NWC', 'WIO', 'NWC'), feature_group_count=x.shape[-1])
    return jax.nn.silu(y)


def gated_delta_chunked(q, k, v, g, beta, s0):
    Bn, H, T, _ = q.shape
    C = DN_CHUNK
    n = T // C

    def chunks(t):
        return t.reshape(Bn, H, n, C, *t.shape[3:])

    q, k, v, g, beta = chunks(q), chunks(k), chunks(v), chunks(g), chunks(beta)
    gc = jnp.cumsum(g, axis=-1)
    incl = jnp.tril(jnp.ones((C, C), dtype=bool))
    strict = jnp.tril(jnp.ones((C, C), dtype=bool), -1)
    diff = gc[..., :, None] - gc[..., None, :]
    decay = jnp.where(incl, jnp.exp(jnp.where(incl, diff, 0.0)), 0.0)
    kb = k * beta[..., None]
    lower = jnp.where(strict, jnp.einsum('bhnid,bhnjd->bhnij', kb, k) * decay, 0.0)
    eye = jnp.eye(C, dtype=q.dtype)
    tmat = lax.linalg.triangular_solve(eye + lower, jnp.broadcast_to(eye, lower.shape),
                                       left_side=True, lower=True, unit_diagonal=True)
    u = jnp.einsum('bhnij,bhnjd->bhnid', tmat, v * beta[..., None])
    w = jnp.einsum('bhnij,bhnjd->bhnid', tmat, kb * jnp.exp(gc)[..., None])
    attn = jnp.einsum('bhnid,bhnjd->bhnij', q, k) * decay
    qd = q * jnp.exp(gc)[..., None]
    kd = k * jnp.exp(gc[..., -1:] - gc)[..., None]
    glast = jnp.exp(gc[..., -1])

    def step(S, xs):
        a_i, u_i, w_i, qd_i, kd_i, gl_i = xs
        v_new = u_i - jnp.einsum('bhck,bhkv->bhcv', w_i, S)
        o_i = jnp.einsum('bhck,bhkv->bhcv', qd_i, S) + jnp.einsum('bhij,bhjv->bhiv', a_i, v_new)
        S = S * gl_i[..., None, None] + jnp.einsum('bhck,bhcv->bhkv', kd_i, v_new)
        return S, o_i

    xs = tuple(jnp.moveaxis(t, 2, 0) for t in (attn, u, w, qd, kd, glast))
    s_fin, o = lax.scan(step, s0, xs)
    o = jnp.moveaxis(o, 0, 2).reshape(Bn, H, T, -1)
    return o, s_fin


def delta_heads(qa, ka, va, za, alpha, beta_logit, conv_w, a_log, dt_bias, onorm_g, s0):
    Bn, T, _ = qa.shape
    qkv = short_conv(jnp.concatenate([qa, ka, va], axis=-1), conv_w).astype(jnp.float32)
    q, k, v = jnp.split(qkv, [DN_QK_W, 2 * DN_QK_W], axis=-1)

    def heads(t, d):
        return t.reshape(Bn, T, DN_HEADS, d).transpose(0, 2, 1, 3)

    def dirs(t):
        return t.astype(jnp.float32).reshape(Bn, T, 2, DN_HEADS).transpose(2, 0, 3, 1)

    q = l2norm(heads(q, DN_DK)) * DN_DK ** -0.5
    k = l2norm(heads(k, DN_DK))
    v = heads(v, DN_DV)
    a = jnp.exp(a_log.astype(jnp.float32))[:, None, :, None]
    g = -a * jax.nn.softplus(dirs(alpha) + dt_bias.astype(jnp.float32)[:, None, :, None])
    beta = jax.nn.sigmoid(dirs(beta_logit))
    s0 = s0.astype(jnp.float32)
    o_f, s_f = gated_delta_chunked(q, k, v, g[0], beta[0], s0[:, 0])

    def flip(t):
        return jnp.flip(t, axis=2)

    o_b, s_b = gated_delta_chunked(flip(q), flip(k), flip(v), flip(g[1]), flip(beta[1]), s0[:, 1])
    o = (o_f + flip(o_b)).transpose(0, 2, 1, 3)
    z = za.astype(jnp.float32).reshape(Bn, T, DN_HEADS, DN_DV)
    o = rmsnorm(o, onorm_g) * jax.nn.silu(z)
    return o.reshape(Bn, T, DN_V_W).astype(qa.dtype), jnp.stack([s_f, s_b], axis=1)


def ctx_attention(q, k, v):
    Bn, L, H, hd = q.shape
    qb = jnp.moveaxis(q.reshape(Bn, L // Q_BLOCK, Q_BLOCK, H, hd), 1, 0)

    def one_block(qi):
        s = jnp.einsum('bqhd,bkhd->bhqk', qi, k).astype(jnp.float32) * hd ** -0.5
        p = jax.nn.softmax(s, axis=-1).astype(v.dtype)
        return jnp.einsum('bhqk,bkhd->bqhd', p, v)

    o = lax.map(one_block, qb)
    return jnp.moveaxis(o, 0, 1).reshape(Bn, L, H * hd)


def na_latent(q, k, v, k_ctx, v_ctx, rpb):
    Bn, T, H, hd = q.shape
    rows = T // GRID_W
    kr = min(NA_ROWS, rows)
    r = jnp.arange(rows)
    rs = jnp.clip(r - kr // 2, 0, rows - kr)
    row_idx = rs[:, None] + jnp.arange(kr)[None, :]
    col = jnp.arange(GRID_W)
    cs = jnp.clip(col - NA_COLS // 2, 0, GRID_W - NA_COLS)
    col_ok = (col[None, :] >= cs[:, None]) & (col[None, :] < cs[:, None] + NA_COLS)
    dr = row_idx - r[:, None] + NA_ROWS - 1
    dc = jnp.clip(col[None, :] - col[:, None] + NA_COLS - 1, 0, 2 * NA_COLS - 2)
    bias = rpb.astype(jnp.float32)[:, dr[:, None, :, None], dc[None, :, None, :]]
    qg = q.reshape(Bn, rows, GRID_W, H, hd)
    kg = k.reshape(Bn, rows, GRID_W, H, hd)[:, row_idx]
    vg = v.reshape(Bn, rows, GRID_W, H, hd)[:, row_idx]
    scale = hd ** -0.5
    s_win = jnp.einsum('brqhd,brikhd->bhrqik', qg, kg).astype(jnp.float32) * scale + bias[None]
    s_win = jnp.where(col_ok[:, None, :], s_win, NEG_INF)
    s_ctx = jnp.einsum('brqhd,bchd->bhrqc', qg, k_ctx).astype(jnp.float32) * scale
    nwin = kr * GRID_W
    s_all = jnp.concatenate([s_win.reshape(Bn, H, rows, GRID_W, nwin), s_ctx], axis=-1)
    p = jax.nn.softmax(s_all, axis=-1).astype(v.dtype)
    p_win = p[..., :nwin].reshape(Bn, H, rows, GRID_W, kr, GRID_W)
    o = (jnp.einsum('bhrqik,brikhd->brqhd', p_win, vg)
         + jnp.einsum('bhrqc,bchd->brqhd', p[..., nwin:], v_ctx.astype(v.dtype)))
    return o.reshape(Bn, T, H * hd)


def even_mixer(h, w_in, w_out, conv_w, a_log, dt_bias, onorm_g, rpb, s0, k_ctx, v_ctx):
    Bn, T, _ = h.shape
    cuts = np.cumsum([DN_QK_W, DN_QK_W, DN_V_W, DN_V_W, 2 * DN_HEADS, 2 * DN_HEADS, NA_W, NA_W]).tolist()
    qa, ka, va, za, alpha, beta_logit, qb, kb, vb = jnp.split(h @ w_in, cuts, axis=-1)
    if k_ctx is None:
        s0 = jnp.zeros((Bn, 2, DN_HEADS, DN_DK, DN_DV), jnp.float32)
    o_a, s_fin = delta_heads(qa, ka, va, za, alpha, beta_logit, conv_w, a_log, dt_bias, onorm_g, s0)
    qb, kb, vb = (t.reshape(Bn, T, NA_HEADS, NA_HD) for t in (qb, kb, vb))
    if k_ctx is None:
        o_b = ctx_attention(qb, kb, vb)
    else:
        o_b = na_latent(qb, kb, vb, k_ctx, v_ctx, rpb)
    out = jnp.concatenate([o_a, o_b.astype(o_a.dtype)], axis=-1) @ w_out
    return out, s_fin, kb, vb


def sgu_mixer(h, w_in, ln_g, ln_b, w_s, b_s, w_out):
    Bn, T, _ = h.shape
    u, v = jnp.split(jax.nn.gelu(h @ w_in), 2, axis=-1)
    vf = v.astype(jnp.float32)
    mu = jnp.mean(vf, axis=-1, keepdims=True)
    var = jnp.mean(jnp.square(vf - mu), axis=-1, keepdims=True)
    v = ((vf - mu) * lax.rsqrt(var + EPS) * ln_g.astype(jnp.float32)
         + ln_b.astype(jnp.float32)).astype(h.dtype)
    v = v.reshape(Bn, T // SG_CHUNK, SG_CHUNK, SG_GROUPS, SG_GW)
    v = jnp.einsum('gts,bnsgc->bntgc', w_s, v) + b_s.T[None, None, :, :, None]
    return (u * v.reshape(Bn, T, SG_W)) @ w_out


def grouped_experts(x, e_idx, wts, w_gate, w_up, w_down):
    N, D = x.shape
    A = N * TOP_K
    flat_e = e_idx.reshape(-1)
    order = jnp.argsort(flat_e)
    sorted_e = flat_e[order]
    tok = order // TOP_K
    counts = jnp.zeros((N_EXPERTS,), jnp.int32).at[flat_e].add(1)
    padded = (counts + MOE_BLOCK - 1) // MOE_BLOCK * MOE_BLOCK
    pad_end = jnp.cumsum(padded)
    pad_start = pad_end - padded
    seg_start = jnp.cumsum(counts) - counts
    dest = pad_start[sorted_e] + jnp.arange(A) - seg_start[sorted_e]
    n_blocks = -(-(A + N_EXPERTS * (MOE_BLOCK - 1)) // MOE_BLOCK)
    P = n_blocks * MOE_BLOCK
    row_tok = jnp.full((P,), N, jnp.int32).at[dest].set(tok.astype(jnp.int32))
    x_pad = jnp.concatenate([x, jnp.zeros((1, D), x.dtype)], axis=0)[row_tok]
    x_pad = x_pad.reshape(n_blocks, MOE_BLOCK, D)
    blk_e = jnp.minimum(jnp.searchsorted(pad_end, jnp.arange(n_blocks) * MOE_BLOCK, side='right'),
                        N_EXPERTS - 1)

    def expert_block(args):
        xb, e = args
        hb = jax.nn.silu(xb @ w_gate[e]) * (xb @ w_up[e])
        return hb @ w_down[e]

    y_pad = lax.map(expert_block, (x_pad, blk_e)).reshape(P, D)
    y_sorted = y_pad[dest] * wts.reshape(-1)[order][:, None].astype(x.dtype)
    return jax.ops.segment_sum(y_sorted, tok, num_segments=N)


def hier_moe(h, w_rg, b_rg, w_re, b_re, w_gate, w_up, w_down):
    Bn, T, D = h.shape
    x = h.reshape(-1, D)
    N = x.shape[0]
    pg = jax.nn.softmax((x @ w_rg + b_rg).astype(jnp.float32), axis=-1)
    pg_top, g_idx = lax.top_k(pg, 1)
    le = (x @ w_re + b_re).astype(jnp.float32).reshape(N, N_EGROUPS, EXP_PER_GROUP)
    le = le[jnp.arange(N), g_idx[:, 0]]
    pe_top, e_loc = lax.top_k(jax.nn.softmax(le, axis=-1), TOP_K)
    wts = pg_top * pe_top / jnp.sum(pe_top, axis=-1, keepdims=True)
    e_idx = g_idx * EXP_PER_GROUP + e_loc
    y = grouped_experts(x, e_idx, wts, w_gate, w_up, w_down)
    return y.reshape(Bn, T, D)


def setup_inputs(seed: int = 0) -> dict:
    key = jax.random.key(seed)
    keys = iter(jax.random.split(key, 40))
    D = D_MODEL

    def nrm(shape, scale):
        return jax.random.normal(next(keys), shape, jnp.float32) * scale

    def gain(shape):
        return 1.0 + nrm(shape, 0.02)

    conv_ch = 2 * DN_QK_W + DN_V_W
    dt = jnp.exp(jax.random.uniform(next(keys), (N_EVEN, 2, DN_HEADS), jnp.float32,
                                    math.log(1e-3), math.log(1e-1)))
    return {
        'x_prompt': nrm((BATCH, SEQ, D), 1.0),
        'x_sample': nrm((DEC_BATCH, DEC_SEQ, D), 1.0),
        'c': nrm((DEC_BATCH, D), 1.0),
        'cache_k': nrm((DEC_BATCH, N_EVEN, PAST_LEN, NA_HEADS, NA_HD), 1.0),
        'cache_v': nrm((DEC_BATCH, N_EVEN, PAST_LEN, NA_HEADS, NA_HD), 1.0),
        'state_delta': nrm((DEC_BATCH, N_EVEN, 2, DN_HEADS, DN_DK, DN_DV), 0.1),
        'c_ctx': nrm((D,), 1.0),
        'ada_w': nrm((DEPTH, D, 6 * D), 0.5 * D ** -0.5),
        'ada_b': nrm((DEPTH, 6 * D), 0.02),
        'norm1_g': gain((DEPTH, D)),
        'norm2_g': gain((DEPTH, D)),
        'final_g': gain((D,)),
        'ev_w_in': nrm((N_EVEN, D, IN_COLS), D ** -0.5),
        'ev_w_out': nrm((N_EVEN, MIX_W, D), MIX_W ** -0.5),
        'ev_conv_w': nrm((N_EVEN, CONV_K, conv_ch), CONV_K ** -0.5),
        'ev_a_log': jnp.log(jax.random.uniform(next(keys), (N_EVEN, 2, DN_HEADS), jnp.float32, 1.0, 16.0)),
        'ev_dt_bias': dt + jnp.log(-jnp.expm1(-dt)),
        'ev_onorm_g': gain((N_EVEN, DN_DV)),
        'ev_rpb': nrm((N_EVEN, NA_HEADS, 2 * NA_ROWS - 1, 2 * NA_COLS - 1), 0.1),
        'od_w_in': nrm((N_ODD, D, 2 * SG_W), D ** -0.5),
        'od_ln_g': gain((N_ODD, SG_W)),
        'od_ln_b': nrm((N_ODD, SG_W), 0.02),
        'od_w_s': nrm((N_ODD, SG_GROUPS, SG_CHUNK, SG_CHUNK), SG_CHUNK ** -0.5),
        'od_b_s': gain((N_ODD, SG_GROUPS, SG_CHUNK)),
        'od_w_out': nrm((N_ODD, SG_W, D), SG_W ** -0.5),
        'moe_w_rg': nrm((DEPTH, D, N_EGROUPS), D ** -0.5),
        'moe_b_rg': nrm((DEPTH, N_EGROUPS), 0.01),
        'moe_w_re': nrm((DEPTH, D, N_EXPERTS), D ** -0.5),
        'moe_b_re': nrm((DEPTH, N_EXPERTS), 0.01),
        'moe_w_gate': nrm((DEPTH, N_EXPERTS, D, D_EXPERT), D ** -0.5),
        'moe_w_up': nrm((DEPTH, N_EXPERTS, D, D_EXPERT), D ** -0.5),
        'moe_w_down': nrm((DEPTH, N_EXPERTS, D_EXPERT, D), D_EXPERT ** -0.5),
    }


def reference(x_prompt, x_sample, c, cache_k, cache_v, state_delta, c_ctx, ada_w, ada_b,
              norm1_g, norm2_g, final_g, ev_w_in, ev_w_out, ev_conv_w, ev_a_log, ev_dt_bias,
              ev_onorm_g, ev_rpb, od_w_in, od_ln_g, od_ln_b, od_w_s, od_b_s, od_w_out,
              moe_w_rg, moe_b_rg, moe_w_re, moe_b_re, moe_w_gate, moe_w_up, moe_w_down):

    def run_layers(x, cond, ctx_cache):
        ks, vs, ss = [], [], []
        for l in range(DEPTH):
            shift1, scale1, gate1, shift2, scale2, gate2 = ada_mod(cond, ada_w[l], ada_b[l])
            h = rmsnorm(x, norm1_g[l]) * (1 + scale1) + shift1
            if l % 2 == 0:
                e = l // 2
                if ctx_cache is None:
                    s0, kc, vc = None, None, None
                else:
                    kc, vc, s0 = ctx_cache[0][:, e], ctx_cache[1][:, e], ctx_cache[2][:, e]
                out, s_fin, kb, vb = even_mixer(h, ev_w_in[e], ev_w_out[e], ev_conv_w[e], ev_a_log[e],
                                                ev_dt_bias[e], ev_onorm_g[e], ev_rpb[e], s0, kc, vc)
                if ctx_cache is None:
                    ks.append(kb)
                    vs.append(vb)
                    ss.append(s_fin)
            else:
                o = l // 2
                out = sgu_mixer(h, od_w_in[o], od_ln_g[o], od_ln_b[o], od_w_s[o], od_b_s[o], od_w_out[o])
            x = x + gate1 * out
            h = rmsnorm(x, norm2_g[l]) * (1 + scale2) + shift2
            x = x + gate2 * hier_moe(h, moe_w_rg[l], moe_b_rg[l], moe_w_re[l], moe_b_re[l],
                                     moe_w_gate[l], moe_w_up[l], moe_w_down[l])
        return rmsnorm(x, final_g), ks, vs, ss

    y_prompt, ks, vs, ss = run_layers(x_prompt, c_ctx[None, :], None)
    new_cache_k = jnp.stack(ks, axis=1)
    new_cache_v = jnp.stack(vs, axis=1)
    new_state_delta = jnp.stack(ss, axis=1)
    y_sample, _, _, _ = run_layers(x_sample, c, (cache_k, cache_v, state_delta))
    return (y_prompt, y_sample, new_cache_k, new_cache_v, new_state_delta)
```

```python
import functools

import jax
import jax.numpy as jnp
from jax import lax
from jax.experimental import pallas as pl
from jax.experimental.pallas import tpu as pltpu

F32 = jnp.float32
BF16 = jnp.bfloat16

D_MODEL = 1024
BATCH = 16
SEQ = 256
DEPTH = 4
DEC_BATCH = 4
DEC_SEQ = 2048
PAST_LEN = 512
GRID_W = 64
EPS = 1e-6
NEG_INF = -1e30

DN_HEADS = 4
DN_DK = 128
DN_CHUNK = 64
NA_HEADS = 8
NA_HD = 64
NA_ROWS = 8
NA_COLS = 16
SG_CHUNK = 128
SG_GROUPS = 8
SG_W = 2 * D_MODEL
SG_GW = SG_W // SG_GROUPS
N_EGROUPS = 4
EXP_PER_GROUP = 8
N_EXPERTS = 32
D_EXPERT = 512

N_CTX = BATCH * SEQ
N_LAT = DEC_BATCH * DEC_SEQ
N_TOK = N_CTX + N_LAT
N_COND = 8
PROJ_W = 4096
LANES = 128
MOE_BLK = 256
MOE_NBLK = -(-(2 * N_TOK + N_EXPERTS * (MOE_BLK - 1)) // MOE_BLK)
VMEM_LIMIT = 56 * 1024 * 1024

_QA, _KA, _VA, _ZA, _QB, _KB, _VB, _AB = 0, 4, 8, 12, 16, 20, 24, 28


def _params(sem):
    return pltpu.CompilerParams(dimension_semantics=sem, vmem_limit_bytes=VMEM_LIMIT)


def _bdot(a, b):
    return jnp.dot(a.astype(BF16), b.astype(BF16), preferred_element_type=F32)


def _bdot_nt(a, b):
    return lax.dot_general(a.astype(BF16), b.astype(BF16), (((1,), (1,)), ((), ())),
                           preferred_element_type=F32)


def _bdot_tn(a, b):
    return lax.dot_general(a.astype(BF16), b.astype(BF16), (((0,), (0,)), ((), ())),
                           preferred_element_type=F32)


def _split3(a):
    p0 = a.astype(BF16)
    r = a - p0.astype(F32)
    p1 = r.astype(BF16)
    p2 = (r - p1.astype(F32)).astype(BF16)
    return p0, p1, p2


def _dot3(a, b):
    ah = a.astype(BF16)
    al = (a - ah.astype(F32)).astype(BF16)
    bh = b.astype(BF16)
    bl = (b - bh.astype(F32)).astype(BF16)
    return (jnp.dot(ah, bh, preferred_element_type=F32) + jnp.dot(ah, bl, preferred_element_type=F32)
            + jnp.dot(al, bh, preferred_element_type=F32))


def _mask_bf16(m01):
    return jnp.where(m01, 1.0, 0.0).astype(BF16)


def _xdot(m01, a):
    m = _mask_bf16(m01)
    p0, p1, p2 = _split3(a)
    return (jnp.dot(m, p0, preferred_element_type=F32) + jnp.dot(m, p1, preferred_element_type=F32)
            + jnp.dot(m, p2, preferred_element_type=F32))


def _xdot_nt(m01, a):
    m = _mask_bf16(m01)
    dn = (((1,), (1,)), ((), ()))
    p0, p1, p2 = _split3(a)
    return (lax.dot_general(m, p0, dn, preferred_element_type=F32)
            + lax.dot_general(m, p1, dn, preferred_element_type=F32)
            + lax.dot_general(m, p2, dn, preferred_element_type=F32))


def _xdot_r(a, m01):
    m = _mask_bf16(m01)
    p0, p1, p2 = _split3(a)
    return (jnp.dot(p0, m, preferred_element_type=F32) + jnp.dot(p1, m, preferred_element_type=F32)
            + jnp.dot(p2, m, preferred_element_type=F32))


def _sigmoid(x):
    return 1.0 / (1.0 + jnp.exp(-x))


def _silu(x):
    return x * _sigmoid(x)


def _rms(x, g):
    return x * lax.rsqrt(jnp.mean(x * x, axis=-1, keepdims=True) + EPS) * g


def _cond_index(row):
    return jnp.where(row < N_CTX, 0, 1 + (row - N_CTX) // DEC_SEQ)


def _mod_spec(k, tm):
    return pl.BlockSpec((None, None, 1, D_MODEL), lambda i, *_: (_cond_index(i * tm), k, 0, 0))


def _ada_kernel(c_ref, w_ref, b_ref, o_ref):
    o_ref[...] = _bdot(_silu(c_ref[...]), w_ref[...]) + b_ref[...]


def _ada_mods(cond, ada_w, ada_b):
    tn = 1536
    out = pl.pallas_call(
        _ada_kernel, grid=(DEPTH, 6 * D_MODEL // tn),
        in_specs=[pl.BlockSpec((N_COND, D_MODEL), lambda l, j: (0, 0)),
                  pl.BlockSpec((None, D_MODEL, tn), lambda l, j: (l, 0, j)),
                  pl.BlockSpec((None, 1, tn), lambda l, j: (l, 0, j))],
        out_specs=pl.BlockSpec((None, N_COND, tn), lambda l, j: (l, 0, j)),
        out_shape=jax.ShapeDtypeStruct((DEPTH, N_COND, 6 * D_MODEL), F32),
        compiler_params=_params(("parallel", "parallel")), name="ada_mods",
    )(cond, ada_w, ada_b.reshape(DEPTH, 1, 6 * D_MODEL))
    return out.reshape(DEPTH, N_COND, 6, 1, D_MODEL)


def _gelu_tanh(x):
    return x * (0.5 * (1.0 + jnp.tanh(0.7978845608028654 * (x + 0.044715 * (x * x * x)))))


def _lin_kernel(x_ref, g_ref, sh_ref, sc_ref, w_ref, o_ref, h_scr, *, gelu):
    @pl.when(pl.program_id(1) == 0)
    def _():
        h = _rms(x_ref[...], g_ref[...]) * (1.0 + sc_ref[...]) + sh_ref[...]
        h_scr[...] = h.astype(BF16)

    y = jnp.dot(h_scr[...], w_ref[...].astype(BF16), preferred_element_type=F32)
    o_ref[...] = _gelu_tanh(y) if gelu else y


def _norm_mod_linear(x, mods, g, w, gelu):
    tm, tn = 1024, 512
    return pl.pallas_call(
        functools.partial(_lin_kernel, gelu=gelu), grid=(N_TOK // tm, PROJ_W // tn),
        in_specs=[pl.BlockSpec((tm, D_MODEL), lambda i, j: (i, 0)),
                  pl.BlockSpec((1, D_MODEL), lambda i, j: (0, 0)),
                  _mod_spec(0, tm), _mod_spec(1, tm),
                  pl.BlockSpec((D_MODEL, tn), lambda i, j: (0, j))],
        out_specs=pl.BlockSpec((tm, tn), lambda i, j: (i, j)),
        out_shape=jax.ShapeDtypeStruct((N_TOK, PROJ_W), F32),
        scratch_shapes=[pltpu.VMEM((tm, D_MODEL), BF16)],
        compiler_params=_params(("parallel", "arbitrary")), name="norm_mod_linear",
    )(x, g.reshape(1, D_MODEL), mods, mods, w)


def _dn_kernel(*refs, T, has_s0, want_state):
    it = iter(refs)
    q_ref, k_ref, v_ref, z_ref, ab_ref = (next(it) for _ in range(5))
    cwq_ref, cwk_ref, cwv_ref, alog_ref, dtb_ref, og_ref = (next(it) for _ in range(6))
    s0_ref = next(it) if has_s0 else None
    o_ref = next(it)
    sfin_ref = next(it) if want_state else None
    qc, kc, vc, gsc, bsc, osc = (next(it) for _ in range(6))

    C = DN_CHUNK
    n = T // C
    hd = pl.program_id(1)

    row = lax.broadcasted_iota(jnp.int32, (T, 1), 0)

    def conv(x_ref, cw_ref):
        x = x_ref[...]
        xp = jnp.where(row == 0, 0.0, pltpu.roll(x, 1, 0))
        xn = jnp.where(row == T - 1, 0.0, pltpu.roll(x, T - 1, 0))
        return _silu(cw_ref[0:1, :] * xp + cw_ref[1:2, :] * x + cw_ref[2:3, :] * xn)

    def l2n(x):
        return x * lax.rsqrt(jnp.sum(x * x, axis=-1, keepdims=True) + EPS)

    qc[...] = l2n(conv(q_ref, cwq_ref)) * (DN_DK ** -0.5)
    kc[...] = l2n(conv(k_ref, cwk_ref))
    vc[...] = conv(v_ref, cwv_ref)

    ab = ab_ref[...]
    sel_r = lax.broadcasted_iota(jnp.int32, (LANES, LANES), 0)
    for d in range(2):
        alpha = _xdot_r(ab, sel_r == d * DN_HEADS + hd)
        blog = _xdot_r(ab, sel_r == 2 * DN_HEADS + d * DN_HEADS + hd)
        x = alpha + dtb_ref[d, hd]
        sp = jnp.maximum(x, 0.0) + jnp.log1p(jnp.exp(-jnp.abs(x)))
        a = jnp.exp(jnp.full((1, LANES), alog_ref[d, hd], F32))
        gsc[d] = -a * sp
        bsc[d] = _sigmoid(blog)

    ri = lax.broadcasted_iota(jnp.int32, (C, C), 0)
    ci = lax.broadcasted_iota(jnp.int32, (C, C), 1)
    eye = (ri == ci).astype(F32)
    e0 = lax.broadcasted_iota(jnp.int32, (C, LANES), 1) == 0

    def chunk(d, c, S):
        sl = pl.ds(pl.multiple_of(c * C, C), C)
        q, k, v = qc[sl, :], kc[sl, :], vc[sl, :]
        g, beta = gsc[d, sl, :], bsc[d, sl, :]
        incl = (ci <= ri) if d == 0 else (ci >= ri)
        strict = (ci < ri) if d == 0 else (ci > ri)
        gc = _xdot(incl, g)
        gr = _xdot_nt(e0, gc)
        decay = jnp.where(incl, jnp.exp(jnp.where(incl, gc[:, :C] - gr, 0.0)), 0.0)
        kb = k * beta
        low = jnp.where(strict, _bdot_nt(kb, k) * decay, 0.0)
        t = eye - low
        p = low
        for _ in range(5):
            p = _dot3(p, p)
            t = t + _dot3(t, p)
        eg = jnp.exp(gc)
        uw = _bdot(t, jnp.concatenate([v * beta, kb * eg], axis=-1))
        u, w = uw[:, :LANES], uw[:, LANES:]
        attn = _bdot_nt(q, k) * decay
        last = gc[C - 1:C, :] if d == 0 else gc[0:1, :]
        qd = q * eg
        kd = k * jnp.exp(last - gc)
        v_new = u - _bdot(w, S)
        o = _bdot(qd, S) + _bdot(attn, v_new)
        S = S * jnp.exp(last) + _bdot_tn(kd, v_new)
        osc[d, sl, :] = o
        return S

    def body(i, carry):
        return chunk(0, i, carry[0]), chunk(1, n - 1 - i, carry[1])

    if has_s0:
        init = (s0_ref[0], s0_ref[1])
    else:
        init = (jnp.zeros((DN_DK, LANES), F32), jnp.zeros((DN_DK, LANES), F32))
    s_f, s_b = lax.fori_loop(0, n, body, init)
    if want_state:
        sfin_ref[0] = s_f
        sfin_ref[1] = s_b

    o = osc[0] + osc[1]
    o_ref[...] = _rms(o, og_ref[...]) * _silu(z_ref[...])


def _delta_heads(proj, conv_w, a_log, dt_bias, onorm_g, T, n_seq, row0, s0, prev_out):
    has_s0 = s0 is not None
    want_state = not has_s0

    def col(cb):
        return pl.BlockSpec((T, LANES), lambda s, h: (row0 + s, cb + h))

    def cw(cb):
        return pl.BlockSpec((3, LANES), lambda s, h: (0, cb + h))

    smem = pl.BlockSpec(memory_space=pltpu.SMEM)
    in_specs = [col(_QA), col(_KA), col(_VA), col(_ZA),
                pl.BlockSpec((T, LANES), lambda s, h: (row0 + s, _AB)),
                cw(0), cw(4), cw(8), smem, smem,
                pl.BlockSpec((1, LANES), lambda s, h: (0, 0))]
    args = [proj, proj, proj, proj, proj, conv_w, conv_w, conv_w, a_log, dt_bias,
            onorm_g.reshape(1, LANES)]
    if has_s0:
        in_specs.append(pl.BlockSpec((None, 2, None, DN_DK, LANES), lambda s, h: (s, 0, h, 0, 0)))
        args.append(s0)
    out_shape = [jax.ShapeDtypeStruct((N_TOK, DN_HEADS * LANES), F32)]
    out_specs = [pl.BlockSpec((T, LANES), lambda s, h: (row0 + s, h))]
    aliases = {}
    if prev_out is not None:
        in_specs.append(pl.BlockSpec(memory_space=pl.ANY))
        args.append(prev_out)
        aliases = {len(args) - 1: 0}
    if want_state:
        out_shape.append(jax.ShapeDtypeStruct((n_seq, 2, DN_HEADS, DN_DK, LANES), F32))
        out_specs.append(pl.BlockSpec((None, 2, None, DN_DK, LANES), lambda s, h: (s, 0, h, 0, 0)))

    def kern(*refs):
        if prev_out is not None:
            n_in = len(args)
            refs = refs[:n_in - 1] + refs[n_in:]
        _dn_kernel(*refs, T=T, has_s0=has_s0, want_state=want_state)

    res = pl.pallas_call(
        kern, grid=(n_seq, DN_HEADS), in_specs=in_specs, out_specs=out_specs, out_shape=out_shape,
        scratch_shapes=[pltpu.VMEM((T, LANES), F32)] * 3
        + [pltpu.VMEM((2, T, LANES), F32)] * 3,
        input_output_aliases=aliases,
        compiler_params=_params(("parallel", "parallel")), name="delta_heads_%d" % T,
    )(*args)
    return res if want_state else (res[0], None)


def _softmax_rows(parts):
    m = parts[0].max(axis=-1, keepdims=True)
    for s in parts[1:]:
        m = jnp.maximum(m, s.max(axis=-1, keepdims=True))
    es = [jnp.exp(s - m) for s in parts]
    den = es[0].sum(axis=-1, keepdims=True)
    for e in es[1:]:
        den = den + e.sum(axis=-1, keepdims=True)
    return [e / den for e in es]


def _ctx_attn_kernel(q_ref, k_ref, v_ref, o_ref):
    outs = []
    for hh in range(2):
        sl = slice(hh * NA_HD, (hh + 1) * NA_HD)
        s = _bdot_nt(q_ref[:, sl], k_ref[:, sl]) * (NA_HD ** -0.5)
        (p,) = _softmax_rows([s])
        outs.append(_bdot(p, v_ref[:, sl]))
    o_ref[...] = jnp.concatenate(outs, axis=-1)


def _ctx_attention(proj):
    def col(cb):
        return pl.BlockSpec((SEQ, LANES), lambda s, p: (s, cb + p))

    return pl.pallas_call(
        _ctx_attn_kernel, grid=(BATCH, NA_HEADS // 2),
        in_specs=[col(_QB), col(_KB), col(_VB)],
        out_specs=pl.BlockSpec((SEQ, LANES), lambda s, p: (s, p)),
        out_shape=jax.ShapeDtypeStruct((N_TOK, NA_HEADS * NA_HD), F32),
        compiler_params=_params(("parallel", "parallel")), name="ctx_attention",
    )(proj, proj, proj)


def _na_kernel(q_ref, k_ref, v_ref, kc_ref, vc_ref, bias_ref, prev_ref, o_ref):
    del prev_ref
    rows = DEC_SEQ // GRID_W
    win = NA_ROWS * GRID_W
    scale = NA_HD ** -0.5

    def body(r, carry):
        rs = jnp.clip(r - NA_ROWS // 2, 0, rows - NA_ROWS)
        off = r - rs
        qsl = pl.ds(pl.multiple_of(r * GRID_W, GRID_W), GRID_W)
        wsl = pl.ds(pl.multiple_of(rs * GRID_W, GRID_W), win)
        q, kw, vw = q_ref[qsl, :], k_ref[wsl, :], v_ref[wsl, :]
        outs = []
        for hh in range(2):
            sl = slice(hh * NA_HD, (hh + 1) * NA_HD)
            s_win = _bdot_nt(q[:, sl], kw[:, sl]) * scale + bias_ref[hh, off]
            s_ctx = _bdot_nt(q[:, sl], kc_ref[:, sl]) * scale
            p_win, p_ctx = _softmax_rows([s_win, s_ctx])
            outs.append(_bdot(p_win, vw[:, sl]) + _bdot(p_ctx, vc_ref[:, sl]))
        o_ref[qsl, :] = jnp.concatenate(outs, axis=-1)
        return carry

    lax.fori_loop(0, rows, body, 0)


def _na_bias_table(rpb):
    off = jnp.arange(NA_ROWS)
    dr = jnp.arange(NA_ROWS)[None, :] - off[:, None] + NA_ROWS - 1
    col = jnp.arange(GRID_W)
    cs = jnp.clip(col - NA_COLS // 2, 0, GRID_W - NA_COLS)
    col_ok = (col[None, :] >= cs[:, None]) & (col[None, :] < cs[:, None] + NA_COLS)
    dc = jnp.clip(col[None, :] - col[:, None] + NA_COLS - 1, 0, 2 * NA_COLS - 2)
    b = rpb.astype(F32)[:, dr[:, None, :, None], dc[None, :, None, :]]
    b = jnp.where(col_ok[None, None, :, None, :], b, NEG_INF)
    return b.reshape(NA_HEADS, NA_ROWS, GRID_W, NA_ROWS * GRID_W)


def _na_attention(proj, kctx, vctx, rpb, prev_out):
    blk = N_CTX // DEC_SEQ

    def col(cb):
        return pl.BlockSpec((DEC_SEQ, LANES), lambda b, p: (blk + b, cb + p))

    ctx = pl.BlockSpec((None, PAST_LEN, LANES), lambda b, p: (b, 0, p))
    return pl.pallas_call(
        _na_kernel, grid=(DEC_BATCH, NA_HEADS // 2),
        in_specs=[col(_QB), col(_KB), col(_VB), ctx, ctx,
                  pl.BlockSpec((2, NA_ROWS, GRID_W, NA_ROWS * GRID_W), lambda b, p: (p, 0, 0, 0)),
                  pl.BlockSpec(memory_space=pl.ANY)],
        out_specs=pl.BlockSpec((DEC_SEQ, LANES), lambda b, p: (blk + b, p)),
        out_shape=jax.ShapeDtypeStruct((N_TOK, NA_HEADS * NA_HD), F32),
        input_output_aliases={6: 0},
        compiler_params=_params(("parallel", "parallel")), name="na_attention",
    )(proj, proj, proj, kctx, vctx, _na_bias_table(rpb), prev_out)


def _moe_input(xnew, g2_ref, sc2_ref, sh2_ref, wrh_ref, wrl_ref, br_ref, x_out, h_out, lg_out):
    x_out[...] = xnew
    h = _rms(xnew, g2_ref[...]) * (1.0 + sc2_ref[...]) + sh2_ref[...]
    hh = h.astype(BF16)
    hl = (h - hh.astype(F32)).astype(BF16)
    h_out[...] = hh
    lg_out[...] = (jnp.dot(hh, wrh_ref[...], preferred_element_type=F32)
                   + jnp.dot(hh, wrl_ref[...], preferred_element_type=F32)
                   + jnp.dot(hl, wrh_ref[...], preferred_element_type=F32) + br_ref[...])


def _even_out_kernel(oa_ref, ob_ref, x_ref, w_ref, gate_ref, g2_ref, sc2_ref, sh2_ref, wrh_ref, wrl_ref,
                     br_ref, x_out, h_out, lg_out, w_scr):
    @pl.when(pl.program_id(0) == 0)
    def _():
        w_scr[...] = w_ref[...].astype(BF16)

    mix = jnp.concatenate([oa_ref[...].astype(BF16), ob_ref[...].astype(BF16)], axis=-1)
    out = jnp.dot(mix, w_scr[...], preferred_element_type=F32)
    _moe_input(x_ref[...] + gate_ref[...] * out, g2_ref, sc2_ref, sh2_ref, wrh_ref, wrl_ref, br_ref,
               x_out, h_out, lg_out)


def _tail_specs(tm):
    const = lambda shape: pl.BlockSpec(shape, lambda i: (0,) * len(shape))
    in_specs = [_mod_spec(2, tm), const((1, D_MODEL)), _mod_spec(4, tm), _mod_spec(3, tm),
                const((D_MODEL, LANES)), const((D_MODEL, LANES)), const((1, LANES))]
    out_specs = [pl.BlockSpec((tm, D_MODEL), lambda i: (i, 0)),
                 pl.BlockSpec((tm, D_MODEL), lambda i: (i, 0)),
                 pl.BlockSpec((tm, LANES), lambda i: (i, 0))]
    out_shape = [jax.ShapeDtypeStruct((N_TOK, D_MODEL), F32),
                 jax.ShapeDtypeStruct((N_TOK, D_MODEL), BF16),
                 jax.ShapeDtypeStruct((N_TOK, LANES), F32)]
    return in_specs, out_specs, out_shape


def _router_weights(w_rg, b_rg, w_re, b_re):
    pad = LANES - N_EGROUPS - N_EXPERTS
    w = jnp.concatenate([w_rg, w_re, jnp.zeros((D_MODEL, pad), F32)], axis=1)
    b = jnp.concatenate([b_rg, b_re, jnp.zeros((pad,), F32)]).reshape(1, LANES)
    hi = w.astype(BF16)
    lo = (w - hi.astype(F32)).astype(BF16)
    return hi, lo, b


def _even_out(o_a, o_b, x, w_out, mods, g2, router):
    tm = 512
    tail_in, out_specs, out_shape = _tail_specs(tm)
    rowblk = lambda w: pl.BlockSpec((tm, w), lambda i: (i, 0))
    return pl.pallas_call(
        _even_out_kernel, grid=(N_TOK // tm,),
        in_specs=[rowblk(DN_HEADS * LANES), rowblk(NA_HEADS * NA_HD), rowblk(D_MODEL),
                  pl.BlockSpec((D_MODEL, D_MODEL), lambda i: (0, 0))] + tail_in,
        out_specs=out_specs, out_shape=out_shape,
        scratch_shapes=[pltpu.VMEM((D_MODEL, D_MODEL), BF16)],
        compiler_params=_params(("arbitrary",)), name="even_out",
    )(o_a, o_b, x, w_out, mods, g2.reshape(1, D_MODEL), mods, mods, *router)


def _sgu_kernel(uv_ref, x_ref, lng_ref, lnb_ref, ws_ref, bst_ref, w_ref, gate_ref, g2_ref, sc2_ref, sh2_ref,
                wrh_ref, wrl_ref, br_ref, x_out, h_out, lg_out, w_scr, m_scr, *, tm):
    @pl.when(pl.program_id(0) == 0)
    def _():
        w_scr[...] = w_ref[...].astype(BF16)

    for c in range(tm // SG_CHUNK):
        rs = slice(c * SG_CHUNK, (c + 1) * SG_CHUNK)
        v = uv_ref[rs, SG_W:]
        mu = jnp.mean(v, axis=-1, keepdims=True)
        vc = v - mu
        var = jnp.mean(vc * vc, axis=-1, keepdims=True)
        vn = (vc * lax.rsqrt(var + EPS) * lng_ref[...] + lnb_ref[...]).astype(BF16)
        for g in range(SG_GROUPS):
            cs = slice(g * SG_GW, (g + 1) * SG_GW)
            sp = jnp.dot(ws_ref[g].astype(BF16), vn[:, cs], preferred_element_type=F32) + bst_ref[:, g:g + 1]
            m_scr[rs, cs] = (uv_ref[rs, cs] * sp).astype(BF16)
    out = jnp.dot(m_scr[...], w_scr[...], preferred_element_type=F32)
    _moe_input(x_ref[...] + gate_ref[...] * out, g2_ref, sc2_ref, sh2_ref, wrh_ref, wrl_ref, br_ref,
               x_out, h_out, lg_out)


def _sgu_out(uv, x, ln_g, ln_b, w_s, b_s, w_out, mods, g2, router):
    tm = 256
    tail_in, out_specs, out_shape = _tail_specs(tm)
    const = lambda shape: pl.BlockSpec(shape, lambda i: (0,) * len(shape))
    return pl.pallas_call(
        functools.partial(_sgu_kernel, tm=tm), grid=(N_TOK // tm,),
        in_specs=[pl.BlockSpec((tm, 2 * SG_W), lambda i: (i, 0)),
                  pl.BlockSpec((tm, D_MODEL), lambda i: (i, 0)),
                  const((1, SG_W)), const((1, SG_W)), const((SG_GROUPS, SG_CHUNK, SG_CHUNK)),
                  const((SG_CHUNK, SG_GROUPS)), const((SG_W, D_MODEL))] + tail_in,
        out_specs=out_specs, out_shape=out_shape,
        scratch_shapes=[pltpu.VMEM((SG_W, D_MODEL), BF16), pltpu.VMEM((tm, SG_W), BF16)],
        compiler_params=_params(("arbitrary",)), name="sgu_out",
    )(uv, x, ln_g.reshape(1, SG_W), ln_b.reshape(1, SG_W), w_s, b_s.T, w_out, mods,
      g2.reshape(1, D_MODEL), mods, mods, *router)


def _route(lg):
    n = lg.shape[0]
    pg = jax.nn.softmax(lg[:, :N_EGROUPS], axis=-1)
    pg_top, g_idx = lax.top_k(pg, 1)
    le = lg[:, N_EGROUPS:N_EGROUPS + N_EXPERTS].reshape(n, N_EGROUPS, EXP_PER_GROUP)
    le = le[jnp.arange(n), g_idx[:, 0]]
    pe_top, e_loc = lax.top_k(jax.nn.softmax(le, axis=-1), 2)
    wts = pg_top * pe_top / jnp.sum(pe_top, axis=-1, keepdims=True)
    return g_idx * EXP_PER_GROUP + e_loc, wts


def _plan(e_idx):
    flat_e = e_idx.reshape(-1)
    onehot = (flat_e[:, None] == jnp.arange(N_EXPERTS)[None, :]).astype(jnp.int32)
    csum = jnp.cumsum(onehot, axis=0)
    rank = jnp.sum(csum * onehot, axis=-1) - 1
    counts = csum[-1]
    padded = (counts + MOE_BLK - 1) // MOE_BLK * MOE_BLK
    pad_end = jnp.cumsum(padded)
    dest = (pad_end - padded)[flat_e] + rank
    blk_e = jnp.minimum(jnp.searchsorted(pad_end, jnp.arange(MOE_NBLK) * MOE_BLK, side='right'),
                        N_EXPERTS - 1).astype(jnp.int32)
    n_used = (pad_end[-1] // MOE_BLK).astype(jnp.int32).reshape(1)
    return dest.astype(jnp.int32), blk_e, n_used


def _expert_kernel(blk_e_ref, n_used_ref, x_ref, wg_ref, wu_ref, wd_ref, wt_ref, o_ref, wg_scr, wu_scr, wd_scr):
    j = pl.program_id(0)
    prev = blk_e_ref[jnp.maximum(j - 1, 0)]
    fresh = jnp.logical_or(j == 0, blk_e_ref[j] != prev)
    live = j < n_used_ref[0]

    @pl.when(jnp.logical_and(fresh, live))
    def _():
        wg_scr[...] = wg_ref[...].astype(BF16)
        wu_scr[...] = wu_ref[...].astype(BF16)
        wd_scr[...] = wd_ref[...].astype(BF16)

    @pl.when(live)
    def _():
        x = x_ref[...]
        gt = jnp.dot(x, wg_scr[...], preferred_element_type=F32)
        up = jnp.dot(x, wu_scr[...], preferred_element_type=F32)
        hb = (_silu(gt) * up).astype(BF16)
        o_ref[...] = jnp.dot(hb, wd_scr[...], preferred_element_type=F32) * wt_ref[...]


def _experts(x_pad, wt_pad, blk_e, n_used, w_gate, w_up, w_down, layer):
    grid_spec = pltpu.PrefetchScalarGridSpec(
        num_scalar_prefetch=2, grid=(MOE_NBLK,),
        in_specs=[pl.BlockSpec((MOE_BLK, D_MODEL), lambda j, be, nu: (j, 0)),
                  pl.BlockSpec((None, None, D_MODEL, D_EXPERT), lambda j, be, nu: (layer, be[j], 0, 0)),
                  pl.BlockSpec((None, None, D_MODEL, D_EXPERT), lambda j, be, nu: (layer, be[j], 0, 0)),
                  pl.BlockSpec((None, None, D_EXPERT, D_MODEL), lambda j, be, nu: (layer, be[j], 0, 0)),
                  pl.BlockSpec((MOE_BLK, 1), lambda j, be, nu: (j, 0))],
        out_specs=pl.BlockSpec((MOE_BLK, D_MODEL), lambda j, be, nu: (j, 0)),
        scratch_shapes=[pltpu.VMEM((D_MODEL, D_EXPERT), BF16), pltpu.VMEM((D_MODEL, D_EXPERT), BF16),
                        pltpu.VMEM((D_EXPERT, D_MODEL), BF16)])
    return pl.pallas_call(
        _expert_kernel, grid_spec=grid_spec,
        out_shape=jax.ShapeDtypeStruct((MOE_NBLK * MOE_BLK, D_MODEL), F32),
        compiler_params=_params(("arbitrary",)), name="experts",
    )(blk_e, n_used, x_pad, w_gate, w_up, w_down, wt_pad)


def _combine_kernel(x_ref, ya_ref, yb_ref, gate_ref, fg_ref, o_ref, *, final):
    xn = x_ref[...] + gate_ref[...] * (ya_ref[...] + yb_ref[...])
    o_ref[...] = _rms(xn, fg_ref[...]) if final else xn


def _combine(x, ya, yb, mods, final_g, final):
    tm = 512
    blk = pl.BlockSpec((tm, D_MODEL), lambda i: (i, 0))
    return pl.pallas_call(
        functools.partial(_combine_kernel, final=final), grid=(N_TOK // tm,),
        in_specs=[blk, blk, blk, _mod_spec(5, tm), pl.BlockSpec((1, D_MODEL), lambda i: (0, 0))],
        out_specs=blk, out_shape=jax.ShapeDtypeStruct((N_TOK, D_MODEL), F32),
        compiler_params=_params(("parallel",)), name="moe_combine",
    )(x, ya, yb, mods, final_g.reshape(1, D_MODEL))


def _moe(x, h, lg, mods, w_gate, w_up, w_down, layer, final_g, final):
    e_idx, wts = _route(lg)
    dest, blk_e, n_used = _plan(e_idx)
    n_pad = MOE_NBLK * MOE_BLK
    tok = jnp.arange(2 * N_TOK, dtype=jnp.int32) // 2
    row_tok = jnp.zeros((n_pad,), jnp.int32).at[dest].set(tok)
    wt_pad = jnp.zeros((n_pad,), F32).at[dest].set(wts.reshape(-1)).reshape(n_pad, 1)
    y_pad = _experts(h[row_tok], wt_pad, blk_e, n_used, w_gate, w_up, w_down, layer)
    dest2 = dest.reshape(N_TOK, 2)
    return _combine(x, y_pad[dest2[:, 0]], y_pad[dest2[:, 1]], mods, final_g, final)


def _permute_even_w(w_in):
    n_ab = 4 * DN_HEADS
    ab0 = 4 * DN_HEADS * DN_DK
    pad = PROJ_W - w_in.shape[1]
    return jnp.concatenate([w_in[:, :ab0], w_in[:, ab0 + n_ab:], w_in[:, ab0:ab0 + n_ab],
                            jnp.zeros((D_MODEL, pad), F32)], axis=1)


def kernel(x_prompt, x_sample, c, cache_k, cache_v, state_delta, c_ctx, ada_w, ada_b, norm1_g, norm2_g, final_g,
           ev_w_in, ev_w_out, ev_conv_w, ev_a_log, ev_dt_bias, ev_onorm_g, ev_rpb, od_w_in, od_ln_g, od_ln_b,
           od_w_s, od_b_s, od_w_out, moe_w_rg, moe_b_rg, moe_w_re, moe_b_re, moe_w_gate, moe_w_up, moe_w_down):
    x = jnp.concatenate([x_prompt.reshape(N_CTX, D_MODEL), x_sample.reshape(N_LAT, D_MODEL)], axis=0)
    cond = jnp.concatenate([c_ctx[None, :], c, jnp.zeros((N_COND - 1 - DEC_BATCH, D_MODEL), F32)], axis=0)
    mods_all = _ada_mods(cond, ada_w, ada_b)
    kctx_all = cache_k.reshape(DEC_BATCH, -1, PAST_LEN, NA_HEADS * NA_HD)
    vctx_all = cache_v.reshape(DEC_BATCH, -1, PAST_LEN, NA_HEADS * NA_HD)

    ks, vs, ss = [], [], []
    for l in range(DEPTH):
        mods = mods_all[l]
        router = _router_weights(moe_w_rg[l], moe_b_rg[l], moe_w_re[l], moe_b_re[l])
        if l % 2 == 0:
            e = l // 2
            proj = _norm_mod_linear(x, mods, norm1_g[l], _permute_even_w(ev_w_in[e]), gelu=False)
            o_a, s_fin = _delta_heads(proj, ev_conv_w[e], ev_a_log[e], ev_dt_bias[e], ev_onorm_g[e],
                                      SEQ, BATCH, 0, None, None)
            o_a, _ = _delta_heads(proj, ev_conv_w[e], ev_a_log[e], ev_dt_bias[e], ev_onorm_g[e],
                                  DEC_SEQ, DEC_BATCH, N_CTX // DEC_SEQ, state_delta[:, e], o_a)
            o_b = _ctx_attention(proj)
            o_b = _na_attention(proj, kctx_all[:, e], vctx_all[:, e], ev_rpb[e], o_b)
            x, h, lg = _even_out(o_a, o_b, x, ev_w_out[e], mods, norm2_g[l], router)
            kv0 = _KB * LANES
            kv1 = _VB * LANES
            ks.append(proj[:N_CTX, kv0:kv0 + NA_HEADS * NA_HD].reshape(BATCH, SEQ, NA_HEADS, NA_HD))
            vs.append(proj[:N_CTX, kv1:kv1 + NA_HEADS * NA_HD].reshape(BATCH, SEQ, NA_HEADS, NA_HD))
            ss.append(s_fin)
        else:
            o = l // 2
            uv = _norm_mod_linear(x, mods, norm1_g[l], od_w_in[o], gelu=True)
            x, h, lg = _sgu_out(uv, x, od_ln_g[o], od_ln_b[o], od_w_s[o], od_b_s[o], od_w_out[o], mods,
                                norm2_g[l], router)
        x = _moe(x, h, lg, mods, moe_w_gate, moe_w_up, moe_w_down, l, final_g, l == DEPTH - 1)

    y_prompt = x[:N_CTX].reshape(BATCH, SEQ, D_MODEL)
    y_sample = x[N_CTX:].reshape(DEC_BATCH, DEC_SEQ, D_MODEL)
    return (y_prompt, y_sample, jnp.stack(ks, axis=1), jnp.stack(vs, axis=1), jnp.stack(ss, axis=1))
```

```python
import functools

import jax
import jax.numpy as jnp
from jax import lax
from jax.experimental import pallas as pl
from jax.experimental.pallas import tpu as pltpu

F32 = jnp.float32
BF16 = jnp.bfloat16

D_MODEL = 1024
BATCH = 16
SEQ = 256
DEPTH = 4
DEC_BATCH = 4
DEC_SEQ = 2048
PAST_LEN = 512
GRID_W = 64
EPS = 1e-6
NEG_INF = -1e30

DN_HEADS = 4
DN_DK = 128
DN_CHUNK = 64
NA_HEADS = 8
NA_HD = 64
NA_ROWS = 8
NA_COLS = 16
SG_CHUNK = 128
SG_GROUPS = 8
SG_W = 2 * D_MODEL
SG_GW = SG_W // SG_GROUPS
N_EGROUPS = 4
EXP_PER_GROUP = 8
N_EXPERTS = 32
D_EXPERT = 512

N_CTX = BATCH * SEQ
N_LAT = DEC_BATCH * DEC_SEQ
N_TOK = N_CTX + N_LAT
N_COND = 8
PROJ_W = 4096
LANES = 128
MOE_BLK = 256
MOE_NBLK = -(-(2 * N_TOK + N_EXPERTS * (MOE_BLK - 1)) // MOE_BLK)
VMEM_LIMIT = 56 * 1024 * 1024

_QA, _KA, _VA, _ZA, _QB, _KB, _VB, _AB = 0, 4, 8, 12, 16, 20, 24, 28


def _params(sem):
    return pltpu.CompilerParams(dimension_semantics=sem, vmem_limit_bytes=VMEM_LIMIT)


def _bdot(a, b):
    return jnp.dot(a.astype(BF16), b.astype(BF16), preferred_element_type=F32)


def _bdot_nt(a, b):
    return lax.dot_general(a.astype(BF16), b.astype(BF16), (((1,), (1,)), ((), ())),
                           preferred_element_type=F32)


def _bdot_tn(a, b):
    return lax.dot_general(a.astype(BF16), b.astype(BF16), (((0,), (0,)), ((), ())),
                           preferred_element_type=F32)


def _split3(a):
    p0 = a.astype(BF16)
    r = a - p0.astype(F32)
    p1 = r.astype(BF16)
    p2 = (r - p1.astype(F32)).astype(BF16)
    return p0, p1, p2


def _dot3(a, b):
    ah = a.astype(BF16)
    al = (a - ah.astype(F32)).astype(BF16)
    bh = b.astype(BF16)
    bl = (b - bh.astype(F32)).astype(BF16)
    return (jnp.dot(ah, bh, preferred_element_type=F32) + jnp.dot(ah, bl, preferred_element_type=F32)
            + jnp.dot(al, bh, preferred_element_type=F32))


def _mask_bf16(m01):
    return jnp.where(m01, 1.0, 0.0).astype(BF16)


def _xdot(m01, a):
    m = _mask_bf16(m01)
    p0, p1, p2 = _split3(a)
    return (jnp.dot(m, p0, preferred_element_type=F32) + jnp.dot(m, p1, preferred_element_type=F32)
            + jnp.dot(m, p2, preferred_element_type=F32))


def _xdot_nt(m01, a):
    m = _mask_bf16(m01)
    dn = (((1,), (1,)), ((), ()))
    p0, p1, p2 = _split3(a)
    return (lax.dot_general(m, p0, dn, preferred_element_type=F32)
            + lax.dot_general(m, p1, dn, preferred_element_type=F32)
            + lax.dot_general(m, p2, dn, preferred_element_type=F32))


def _xdot_r(a, m01):
    m = _mask_bf16(m01)
    p0, p1, p2 = _split3(a)
    return (jnp.dot(p0, m, preferred_element_type=F32) + jnp.dot(p1, m, preferred_element_type=F32)
            + jnp.dot(p2, m, preferred_element_type=F32))


def _sigmoid(x):
    return 1.0 / (1.0 + jnp.exp(-x))


def _silu(x):
    return x * _sigmoid(x)


def _rms(x, g):
    return x * lax.rsqrt(jnp.mean(x * x, axis=-1, keepdims=True) + EPS) * g


def _cond_index(row):
    return jnp.where(row < N_CTX, 0, 1 + (row - N_CTX) // DEC_SEQ)


def _mod_spec(k, tm):
    return pl.BlockSpec((None, None, 1, D_MODEL), lambda i, *_: (_cond_index(i * tm), k, 0, 0))


def _ada_kernel(c_ref, w_ref, b_ref, o_ref):
    o_ref[...] = _bdot(_silu(c_ref[...]), w_ref[...]) + b_ref[...]


def _ada_mods(cond, ada_w, ada_b):
    tn = 1536
    out = pl.pallas_call(
        _ada_kernel, grid=(DEPTH, 6 * D_MODEL // tn),
        in_specs=[pl.BlockSpec((N_COND, D_MODEL), lambda l, j: (0, 0)),
                  pl.BlockSpec((None, D_MODEL, tn), lambda l, j: (l, 0, j)),
                  pl.BlockSpec((None, 1, tn), lambda l, j: (l, 0, j))],
        out_specs=pl.BlockSpec((None, N_COND, tn), lambda l, j: (l, 0, j)),
        out_shape=jax.ShapeDtypeStruct((DEPTH, N_COND, 6 * D_MODEL), F32),
        compiler_params=_params(("parallel", "parallel")), name="ada_mods",
    )(cond, ada_w, ada_b.reshape(DEPTH, 1, 6 * D_MODEL))
    return out.reshape(DEPTH, N_COND, 6, 1, D_MODEL)


def _gelu_tanh(x):
    return x * (0.5 * (1.0 + jnp.tanh(0.7978845608028654 * (x + 0.044715 * (x * x * x)))))


def _lin_kernel(x_ref, g_ref, sh_ref, sc_ref, w_ref, o_ref, h_scr, *, gelu):
    @pl.when(pl.program_id(1) == 0)
    def _():
        h = _rms(x_ref[...], g_ref[...]) * (1.0 + sc_ref[...]) + sh_ref[...]
        h_scr[...] = h.astype(BF16)

    y = jnp.dot(h_scr[...], w_ref[...].astype(BF16), preferred_element_type=F32)
    o_ref[...] = _gelu_tanh(y) if gelu else y


def _norm_mod_linear(x, mods, g, w, gelu):
    tm, tn = 1024, 512
    return pl.pallas_call(
        functools.partial(_lin_kernel, gelu=gelu), grid=(N_TOK // tm, PROJ_W // tn),
        in_specs=[pl.BlockSpec((tm, D_MODEL), lambda i, j: (i, 0)),
                  pl.BlockSpec((1, D_MODEL), lambda i, j: (0, 0)),
                  _mod_spec(0, tm), _mod_spec(1, tm),
                  pl.BlockSpec((D_MODEL, tn), lambda i, j: (0, j))],
        out_specs=pl.BlockSpec((tm, tn), lambda i, j: (i, j)),
        out_shape=jax.ShapeDtypeStruct((N_TOK, PROJ_W), F32),
        scratch_shapes=[pltpu.VMEM((tm, D_MODEL), BF16)],
        compiler_params=_params(("parallel", "arbitrary")), name="norm_mod_linear",
    )(x, g.reshape(1, D_MODEL), mods, mods, w)


_CHUNK_SHIFT = DN_CHUNK.bit_length() - 1
_CUM_ROWS = 256
_DN_PREP = 4


def _dn_kernel(*refs, T, has_s0, want_state):
    it = iter(refs)
    q_ref, k_ref, v_ref, z_ref, ab_ref = (next(it) for _ in range(5))
    cwq_ref, cwk_ref, cwv_ref, alog_ref, dtb_ref, og_ref = (next(it) for _ in range(6))
    s0_ref = next(it) if has_s0 else None
    o_ref = next(it)
    sfin_ref = next(it) if want_state else None
    qc, kc, vc, gsc, bsc, osc, u_s, wq_s, kd_s, at_s = (next(it) for _ in range(10))

    C = DN_CHUNK
    n = T // C
    hd = pl.program_id(1)

    row = lax.broadcasted_iota(jnp.int32, (T, 1), 0)

    def conv(x_ref, cw_ref):
        x = x_ref[...]
        xp = jnp.where(row == 0, 0.0, pltpu.roll(x, 1, 0))
        xn = jnp.where(row == T - 1, 0.0, pltpu.roll(x, T - 1, 0))
        return _silu(cw_ref[0:1, :] * xp + cw_ref[1:2, :] * x + cw_ref[2:3, :] * xn)

    def l2n(x):
        return x * lax.rsqrt(jnp.sum(x * x, axis=-1, keepdims=True) + EPS)

    qc[...] = l2n(conv(q_ref, cwq_ref)) * (DN_DK ** -0.5)
    kc[...] = l2n(conv(k_ref, cwk_ref))
    vc[...] = conv(v_ref, cwv_ref)

    ab = ab_ref[...]
    sel_r = lax.broadcasted_iota(jnp.int32, (LANES, LANES), 0)
    for d in range(2):
        alpha = _xdot_r(ab, sel_r == d * DN_HEADS + hd)
        blog = _xdot_r(ab, sel_r == 2 * DN_HEADS + d * DN_HEADS + hd)
        x = alpha + dtb_ref[d, hd]
        sp = jnp.maximum(x, 0.0) + jnp.log1p(jnp.exp(-jnp.abs(x)))
        a = jnp.exp(jnp.full((1, LANES), alog_ref[d, hd], F32))
        gsc[d] = -a * sp
        bsc[d] = _sigmoid(blog)

    pr = lax.broadcasted_iota(jnp.int32, (_CUM_ROWS, _CUM_ROWS), 0)
    pc = lax.broadcasted_iota(jnp.int32, (_CUM_ROWS, _CUM_ROWS), 1)
    same = lax.shift_right_logical(pr, _CHUNK_SHIFT) == lax.shift_right_logical(pc, _CHUNK_SHIFT)
    cum_mask = (jnp.logical_and(same, pc <= pr), jnp.logical_and(same, pc >= pr))

    def cum_body(i, carry):
        sl = pl.ds(pl.multiple_of(i * _CUM_ROWS, _CUM_ROWS), _CUM_ROWS)
        for d in range(2):
            gsc[d, sl, :] = _xdot(cum_mask[d], gsc[d, sl, :])
        return carry

    lax.fori_loop(0, T // _CUM_ROWS, cum_body, 0)

    ri = lax.broadcasted_iota(jnp.int32, (C, C), 0)
    ci = lax.broadcasted_iota(jnp.int32, (C, C), 1)
    eye = (ri == ci).astype(F32)

    def prepare(items):
        lows, decays = [], []
        for d, c in items:
            sl = pl.ds(pl.multiple_of(c * C, C), C)
            k, gc = kc[sl, :], gsc[d, sl, :]
            incl = (ci <= ri) if d == 0 else (ci >= ri)
            strict = (ci < ri) if d == 0 else (ci > ri)
            gr = jnp.transpose(gc)[0:1, :C]
            decay = jnp.where(incl, jnp.exp(jnp.where(incl, gc[:, :C] - gr, 0.0)), 0.0)
            lows.append(jnp.where(strict, _bdot_nt(k * bsc[d, sl, :], k) * decay, 0.0))
            decays.append(decay)
        ts = [eye - low for low in lows]
        ps = lows
        for _ in range(5):
            ps = [_dot3(p, p) for p in ps]
            ts = [t + _dot3(t, p) for t, p in zip(ts, ps)]
        for (d, c), t, decay in zip(items, ts, decays):
            sl = pl.ds(pl.multiple_of(c * C, C), C)
            q, k, gc, beta = qc[sl, :], kc[sl, :], gsc[d, sl, :], bsc[d, sl, :]
            eg = jnp.exp(gc)
            uw = _bdot(t, jnp.concatenate([vc[sl, :] * beta, k * beta * eg], axis=-1))
            last = gc[C - 1:C, :] if d == 0 else gc[0:1, :]
            u_s[d, sl, :] = uw[:, :LANES]
            wsl = pl.ds(pl.multiple_of(c * 2 * C, 2 * C), C)
            qsl = pl.ds(pl.multiple_of(c * 2 * C + C, C), C)
            wq_s[d, wsl, :] = uw[:, LANES:].astype(BF16)
            wq_s[d, qsl, :] = (q * eg).astype(BF16)
            kd_s[d, sl, :] = (k * jnp.exp(last - gc)).astype(BF16)
            at_s[d, sl, :] = (_bdot_nt(q, k) * decay).astype(BF16)

    def prep_body(i, carry):
        prepare([(d, i * _DN_PREP + j) for j in range(_DN_PREP) for d in range(2)])
        return carry

    lax.fori_loop(0, n // _DN_PREP, prep_body, 0)

    def advance(d, c, S):
        sl = pl.ds(pl.multiple_of(c * C, C), C)
        Sb = S.astype(BF16)
        ws = jnp.dot(wq_s[d, pl.ds(pl.multiple_of(c * 2 * C, 2 * C), 2 * C), :], Sb,
                     preferred_element_type=F32)
        v_new = (u_s[d, sl, :] - ws[:C]).astype(BF16)
        osc[d, sl, :] = ws[C:] + jnp.dot(at_s[d, sl, :], v_new, preferred_element_type=F32)
        last = gsc[d, pl.ds(c * C + (C - 1 if d == 0 else 0), 1), :]
        return S * jnp.exp(last) + lax.dot_general(kd_s[d, sl, :], v_new, (((0,), (0,)), ((), ())),
                                                   preferred_element_type=F32)

    def body(i, carry):
        return advance(0, i, carry[0]), advance(1, n - 1 - i, carry[1])

    if has_s0:
        init = (s0_ref[0], s0_ref[1])
    else:
        init = (jnp.zeros((DN_DK, LANES), F32), jnp.zeros((DN_DK, LANES), F32))
    s_f, s_b = lax.fori_loop(0, n, body, init)
    if want_state:
        sfin_ref[0] = s_f
        sfin_ref[1] = s_b

    o = osc[0] + osc[1]
    o_ref[...] = _rms(o, og_ref[...]) * _silu(z_ref[...])


def _delta_heads(proj, conv_w, a_log, dt_bias, onorm_g, T, n_seq, row0, s0):
    has_s0 = s0 is not None
    want_state = not has_s0

    def col(cb):
        return pl.BlockSpec((T, LANES), lambda s, h: (row0 + s, cb + h))

    def cw(cb):
        return pl.BlockSpec((3, LANES), lambda s, h: (0, cb + h))

    smem = pl.BlockSpec(memory_space=pltpu.SMEM)
    in_specs = [col(_QA), col(_KA), col(_VA), col(_ZA),
                pl.BlockSpec((T, LANES), lambda s, h: (row0 + s, _AB)),
                cw(0), cw(4), cw(8), smem, smem,
                pl.BlockSpec((1, LANES), lambda s, h: (0, 0))]
    args = [proj, proj, proj, proj, proj, conv_w, conv_w, conv_w, a_log, dt_bias,
            onorm_g.reshape(1, LANES)]
    if has_s0:
        in_specs.append(pl.BlockSpec((None, 2, None, DN_DK, LANES), lambda s, h: (s, 0, h, 0, 0)))
        args.append(s0)
    out_shape = [jax.ShapeDtypeStruct((n_seq * T, DN_HEADS * LANES), F32)]
    out_specs = [pl.BlockSpec((T, LANES), lambda s, h: (s, h))]
    if want_state:
        out_shape.append(jax.ShapeDtypeStruct((n_seq, 2, DN_HEADS, DN_DK, LANES), F32))
        out_specs.append(pl.BlockSpec((None, 2, None, DN_DK, LANES), lambda s, h: (s, 0, h, 0, 0)))
    res = pl.pallas_call(
        functools.partial(_dn_kernel, T=T, has_s0=has_s0, want_state=want_state),
        grid=(n_seq, DN_HEADS), in_specs=in_specs, out_specs=out_specs, out_shape=out_shape,
        scratch_shapes=[pltpu.VMEM((T, LANES), F32)] * 3
        + [pltpu.VMEM((2, T, LANES), F32)] * 4
        + [pltpu.VMEM((2, 2 * T, LANES), BF16), pltpu.VMEM((2, T, LANES), BF16),
           pltpu.VMEM((2, T, DN_CHUNK), BF16)],
        compiler_params=_params(("parallel", "parallel")), name="delta_heads_%d" % T,
    )(*args)
    return res if want_state else (res[0], None)


def _softmax_rows(parts):
    m = parts[0].max(axis=-1, keepdims=True)
    for s in parts[1:]:
        m = jnp.maximum(m, s.max(axis=-1, keepdims=True))
    es = [jnp.exp(s - m) for s in parts]
    den = es[0].sum(axis=-1, keepdims=True)
    for e in es[1:]:
        den = den + e.sum(axis=-1, keepdims=True)
    return [e / den for e in es]


def _ctx_attn_kernel(q_ref, k_ref, v_ref, o_ref):
    outs = []
    for hh in range(2):
        sl = slice(hh * NA_HD, (hh + 1) * NA_HD)
        s = _bdot_nt(q_ref[:, sl], k_ref[:, sl]) * (NA_HD ** -0.5)
        (p,) = _softmax_rows([s])
        outs.append(_bdot(p, v_ref[:, sl]))
    o_ref[...] = jnp.concatenate(outs, axis=-1)


def _ctx_attention(proj):
    def col(cb):
        return pl.BlockSpec((SEQ, LANES), lambda s, p: (s, cb + p))

    return pl.pallas_call(
        _ctx_attn_kernel, grid=(BATCH, NA_HEADS // 2),
        in_specs=[col(_QB), col(_KB), col(_VB)],
        out_specs=pl.BlockSpec((SEQ, LANES), lambda s, p: (s, p)),
        out_shape=jax.ShapeDtypeStruct((N_CTX, NA_HEADS * NA_HD), F32),
        compiler_params=_params(("parallel", "parallel")), name="ctx_attention",
    )(proj, proj, proj)


def _na_kernel(q_ref, k_ref, v_ref, kc_ref, vc_ref, bias_ref, o_ref):
    rows = DEC_SEQ // GRID_W
    win = NA_ROWS * GRID_W
    scale = NA_HD ** -0.5

    def body(r, carry):
        rs = jnp.clip(r - NA_ROWS // 2, 0, rows - NA_ROWS)
        off = r - rs
        qsl = pl.ds(pl.multiple_of(r * GRID_W, GRID_W), GRID_W)
        wsl = pl.ds(pl.multiple_of(rs * GRID_W, GRID_W), win)
        q, kw, vw = q_ref[qsl, :], k_ref[wsl, :], v_ref[wsl, :]
        outs = []
        for hh in range(2):
            sl = slice(hh * NA_HD, (hh + 1) * NA_HD)
            bias = jnp.concatenate([bias_ref[hh, NA_ROWS - 1 - off + 2 * j] for j in range(NA_ROWS // 2)],
                                   axis=-1)
            s_win = _bdot_nt(q[:, sl], kw[:, sl]) * scale + bias
            s_ctx = _bdot_nt(q[:, sl], kc_ref[:, sl]) * scale
            p_win, p_ctx = _softmax_rows([s_win, s_ctx])
            outs.append(_bdot(p_win, vw[:, sl]) + _bdot(p_ctx, vc_ref[:, sl]))
        o_ref[qsl, :] = jnp.concatenate(outs, axis=-1)
        return carry

    lax.fori_loop(0, rows, body, 0)


def _na_bias_table(rpb):
    col = jnp.arange(GRID_W)
    cs = jnp.clip(col - NA_COLS // 2, 0, GRID_W - NA_COLS)
    col_ok = (col[None, :] >= cs[:, None]) & (col[None, :] < cs[:, None] + NA_COLS)
    dc = jnp.clip(col[None, :] - col[:, None] + NA_COLS - 1, 0, 2 * NA_COLS - 2)
    onehot = (dc[None, :, :] == jnp.arange(2 * NA_COLS - 1)[:, None, None]).astype(F32)
    t = jnp.einsum('hrd,dqk->hrqk', rpb.astype(F32), onehot, precision=lax.Precision.HIGHEST)
    t = jnp.where(col_ok[None, None], t, NEG_INF)
    return jnp.concatenate([t[:, :-1], t[:, 1:]], axis=-1)


def _na_attention(proj, kctx, vctx, rpb):
    blk = N_CTX // DEC_SEQ

    def col(cb):
        return pl.BlockSpec((DEC_SEQ, LANES), lambda b, p: (blk + b, cb + p))

    ctx = pl.BlockSpec((None, PAST_LEN, LANES), lambda b, p: (b, 0, p))
    return pl.pallas_call(
        _na_kernel, grid=(DEC_BATCH, NA_HEADS // 2),
        in_specs=[col(_QB), col(_KB), col(_VB), ctx, ctx,
                  pl.BlockSpec((2, 2 * NA_ROWS - 2, GRID_W, 2 * GRID_W), lambda b, p: (p, 0, 0, 0))],
        out_specs=pl.BlockSpec((DEC_SEQ, LANES), lambda b, p: (b, p)),
        out_shape=jax.ShapeDtypeStruct((N_LAT, NA_HEADS * NA_HD), F32),
        compiler_params=_params(("parallel", "parallel")), name="na_attention",
    )(proj, proj, proj, kctx, vctx, _na_bias_table(rpb))


_LOGIT0 = N_EGROUPS
_R_E, _R_W, _R_RANK = 0, 2, 4


def _lane_min_where(mask, lane):
    return jnp.min(jnp.where(mask, lane, LANES), axis=-1, keepdims=True)


def _route_rows(lg, carry_ref, tri_ref):
    big = -3.0e38
    lane = lax.broadcasted_iota(jnp.int32, lg.shape, 1)
    is_g = lane < N_EGROUPS
    gmax = jnp.max(jnp.where(is_g, lg, big), axis=-1, keepdims=True)
    gsum = jnp.sum(jnp.where(is_g, jnp.exp(jnp.where(is_g, lg - gmax, 0.0)), 0.0), axis=-1, keepdims=True)
    pg_top = 1.0 / gsum
    g_idx = _lane_min_where(jnp.logical_and(is_g, lg == gmax), lane)
    in_g = jnp.logical_and(lane >= _LOGIT0, lax.shift_right_arithmetic(lane - _LOGIT0, 3) == g_idx)
    in_g = jnp.logical_and(in_g, lane < _LOGIT0 + N_EXPERTS)
    m1 = jnp.max(jnp.where(in_g, lg, big), axis=-1, keepdims=True)
    i1 = _lane_min_where(jnp.logical_and(in_g, lg == m1), lane)
    rest = jnp.logical_and(in_g, lane != i1)
    m2 = jnp.max(jnp.where(rest, lg, big), axis=-1, keepdims=True)
    i2 = _lane_min_where(jnp.logical_and(rest, lg == m2), lane)
    e2 = jnp.exp(m2 - m1)
    w1 = pg_top * (1.0 / (1.0 + e2))
    w2 = pg_top * (e2 / (1.0 + e2))
    hit1 = lane == i1
    hit2 = lane == i2
    picked = jnp.where(jnp.logical_or(hit1, hit2), 1.0, 0.0)
    before = jnp.dot(tri_ref[...], picked.astype(BF16), preferred_element_type=F32) + carry_ref[...]
    r1 = jnp.sum(jnp.where(hit1, before, 0.0), axis=-1, keepdims=True)
    r2 = jnp.sum(jnp.where(hit2, before, 0.0), axis=-1, keepdims=True)
    carry_ref[...] = carry_ref[...] + jnp.sum(picked, axis=0, keepdims=True)
    rec = jnp.zeros(lg.shape, F32)
    for ln, val in ((_R_E, (i1 - _LOGIT0).astype(F32)), (_R_E + 1, (i2 - _LOGIT0).astype(F32)),
                    (_R_W, w1), (_R_W + 1, w2), (_R_RANK, r1), (_R_RANK + 1, r2)):
        rec = jnp.where(lane == ln, val, rec)
    return rec


def _moe_input(xnew, first, tail_in, tail_out, tail_scr):
    g2_ref, sc2_ref, sh2_ref, wrh_ref, wrl_ref, br_ref = tail_in
    x_out, h_out, rec_out, cnt_out = tail_out
    tri_scr, carry_scr = tail_scr

    @pl.when(first)
    def _():
        tm = tri_scr.shape[0]
        r = lax.broadcasted_iota(jnp.int32, (tm, tm), 0)
        c = lax.broadcasted_iota(jnp.int32, (tm, tm), 1)
        tri_scr[...] = jnp.where(c < r, 1.0, 0.0).astype(BF16)
        carry_scr[...] = jnp.zeros(carry_scr.shape, F32)

    x_out[...] = xnew
    h = _rms(xnew, g2_ref[...]) * (1.0 + sc2_ref[...]) + sh2_ref[...]
    hh = h.astype(BF16)
    hl = (h - hh.astype(F32)).astype(BF16)
    h_out[...] = hh
    lg = (jnp.dot(hh, wrh_ref[...], preferred_element_type=F32)
          + jnp.dot(hh, wrl_ref[...], preferred_element_type=F32)
          + jnp.dot(hl, wrh_ref[...], preferred_element_type=F32) + br_ref[...])
    rec_out[...] = _route_rows(lg, carry_scr, tri_scr)
    cnt_out[...] = carry_scr[...]


def _even_out_kernel(oac_ref, obc_ref, oal_ref, obl_ref, x_ref, w_ref, gate_ref, *rest, ctx_tiles):
    tail_in, tail_out, (w_scr,), tail_scr = rest[:6], rest[6:10], rest[10:11], rest[11:]
    first = pl.program_id(0) == 0

    @pl.when(first)
    def _():
        w_scr[...] = w_ref[...].astype(BF16)

    is_ctx = pl.program_id(0) < ctx_tiles
    o_a = jnp.where(is_ctx, oac_ref[...], oal_ref[...])
    o_b = jnp.where(is_ctx, obc_ref[...], obl_ref[...])
    mix = jnp.concatenate([o_a.astype(BF16), o_b.astype(BF16)], axis=-1)
    out = jnp.dot(mix, w_scr[...], preferred_element_type=F32)
    _moe_input(x_ref[...] + gate_ref[...] * out, first, tail_in, tail_out, tail_scr)


def _tail_specs(tm):
    const = lambda shape: pl.BlockSpec(shape, lambda i: (0,) * len(shape))
    in_specs = [_mod_spec(2, tm), const((1, D_MODEL)), _mod_spec(4, tm), _mod_spec(3, tm),
                const((D_MODEL, LANES)), const((D_MODEL, LANES)), const((1, LANES))]
    out_specs = [pl.BlockSpec((tm, D_MODEL), lambda i: (i, 0)),
                 pl.BlockSpec((tm, D_MODEL), lambda i: (i, 0)),
                 pl.BlockSpec((tm, LANES), lambda i: (i, 0)),
                 const((1, LANES))]
    out_shape = [jax.ShapeDtypeStruct((N_TOK, D_MODEL), F32),
                 jax.ShapeDtypeStruct((N_TOK, D_MODEL), BF16),
                 jax.ShapeDtypeStruct((N_TOK, LANES), F32),
                 jax.ShapeDtypeStruct((1, LANES), F32)]
    scratch = [pltpu.VMEM((tm, tm), BF16), pltpu.VMEM((1, LANES), F32)]
    return in_specs, out_specs, out_shape, scratch


def _router_weights(w_rg, b_rg, w_re, b_re):
    pad = LANES - N_EGROUPS - N_EXPERTS
    w = jnp.concatenate([w_rg, w_re, jnp.zeros((D_MODEL, pad), F32)], axis=1)
    b = jnp.concatenate([b_rg, b_re, jnp.zeros((pad,), F32)]).reshape(1, LANES)
    hi = w.astype(BF16)
    lo = (w - hi.astype(F32)).astype(BF16)
    return hi, lo, b


def _even_out(oa_ctx, ob_ctx, oa_lat, ob_lat, x, w_out, mods, g2, router):
    tm = 512
    ctx_tiles = N_CTX // tm
    tail_in, out_specs, out_shape, tail_scr = _tail_specs(tm)
    width = DN_HEADS * LANES
    ctxblk = pl.BlockSpec((tm, width), lambda i: (jnp.minimum(i, ctx_tiles - 1), 0))
    latblk = pl.BlockSpec((tm, width), lambda i: (jnp.maximum(i - ctx_tiles, 0), 0))
    return pl.pallas_call(
        functools.partial(_even_out_kernel, ctx_tiles=ctx_tiles), grid=(N_TOK // tm,),
        in_specs=[ctxblk, ctxblk, latblk, latblk, pl.BlockSpec((tm, D_MODEL), lambda i: (i, 0)),
                  pl.BlockSpec((D_MODEL, D_MODEL), lambda i: (0, 0))] + tail_in,
        out_specs=out_specs, out_shape=out_shape,
        scratch_shapes=[pltpu.VMEM((D_MODEL, D_MODEL), BF16)] + tail_scr,
        compiler_params=_params(("arbitrary",)), name="even_out",
    )(oa_ctx, ob_ctx, oa_lat, ob_lat, x, w_out, mods, g2.reshape(1, D_MODEL), mods, mods, *router)


def _sgu_kernel(uv_ref, x_ref, lng_ref, lnb_ref, ws_ref, bst_ref, w_ref, gate_ref, *rest, tm):
    tail_in, tail_out, (w_scr, m_scr), tail_scr = rest[:6], rest[6:10], rest[10:12], rest[12:]
    first = pl.program_id(0) == 0

    @pl.when(first)
    def _():
        w_scr[...] = w_ref[...].astype(BF16)

    for c in range(tm // SG_CHUNK):
        rs = slice(c * SG_CHUNK, (c + 1) * SG_CHUNK)
        v = uv_ref[rs, SG_W:]
        mu = jnp.mean(v, axis=-1, keepdims=True)
        vc = v - mu
        var = jnp.mean(vc * vc, axis=-1, keepdims=True)
        vn = (vc * lax.rsqrt(var + EPS) * lng_ref[...] + lnb_ref[...]).astype(BF16)
        for g in range(SG_GROUPS):
            cs = slice(g * SG_GW, (g + 1) * SG_GW)
            sp = jnp.dot(ws_ref[g].astype(BF16), vn[:, cs], preferred_element_type=F32) + bst_ref[:, g:g + 1]
            m_scr[rs, cs] = (uv_ref[rs, cs] * sp).astype(BF16)
    out = jnp.dot(m_scr[...], w_scr[...], preferred_element_type=F32)
    _moe_input(x_ref[...] + gate_ref[...] * out, first, tail_in, tail_out, tail_scr)


def _sgu_out(uv, x, ln_g, ln_b, w_s, b_s, w_out, mods, g2, router):
    tm = 256
    tail_in, out_specs, out_shape, tail_scr = _tail_specs(tm)
    const = lambda shape: pl.BlockSpec(shape, lambda i: (0,) * len(shape))
    return pl.pallas_call(
        functools.partial(_sgu_kernel, tm=tm), grid=(N_TOK // tm,),
        in_specs=[pl.BlockSpec((tm, 2 * SG_W), lambda i: (i, 0)),
                  pl.BlockSpec((tm, D_MODEL), lambda i: (i, 0)),
                  const((1, SG_W)), const((1, SG_W)), const((SG_GROUPS, SG_CHUNK, SG_CHUNK)),
                  const((SG_CHUNK, SG_GROUPS)), const((SG_W, D_MODEL))] + tail_in,
        out_specs=out_specs, out_shape=out_shape,
        scratch_shapes=[pltpu.VMEM((SG_W, D_MODEL), BF16), pltpu.VMEM((tm, SG_W), BF16)] + tail_scr,
        compiler_params=_params(("arbitrary",)), name="sgu_out",
    )(uv, x, ln_g.reshape(1, SG_W), ln_b.reshape(1, SG_W), w_s, b_s.T, w_out, mods,
      g2.reshape(1, D_MODEL), mods, mods, *router)


def _plan(rec, cnt):
    e_idx = rec[:, _R_E:_R_E + 2].astype(jnp.int32)
    rank = rec[:, _R_RANK:_R_RANK + 2].astype(jnp.int32)
    counts = cnt[0, _LOGIT0:_LOGIT0 + N_EXPERTS].astype(jnp.int32)
    padded = (counts + MOE_BLK - 1) // MOE_BLK * MOE_BLK
    pad_end = jnp.cumsum(padded)
    pad_start = pad_end - padded
    hit = e_idx[:, :, None] == jnp.arange(N_EXPERTS, dtype=jnp.int32)[None, None, :]
    dest = jnp.sum(jnp.where(hit, pad_start[None, None, :], 0), axis=-1) + rank
    blk0 = jnp.arange(MOE_NBLK, dtype=jnp.int32) * MOE_BLK
    blk_e = jnp.minimum(jnp.sum((pad_end[None, :] <= blk0[:, None]).astype(jnp.int32), axis=-1),
                        N_EXPERTS - 1)
    n_used = (pad_end[-1] // MOE_BLK).astype(jnp.int32).reshape(1)
    return dest, blk_e, n_used


def _expert_kernel(blk_e_ref, n_used_ref, x_ref, wg_ref, wu_ref, wd_ref, o_ref, wg_scr, wu_scr, wd_scr):
    j = pl.program_id(0)
    prev = blk_e_ref[jnp.maximum(j - 1, 0)]
    fresh = jnp.logical_or(j == 0, blk_e_ref[j] != prev)
    live = j < n_used_ref[0]

    @pl.when(jnp.logical_and(fresh, live))
    def _():
        wg_scr[...] = wg_ref[...].astype(BF16)
        wu_scr[...] = wu_ref[...].astype(BF16)
        wd_scr[...] = wd_ref[...].astype(BF16)

    @pl.when(live)
    def _():
        x = x_ref[...]
        gt = jnp.dot(x, wg_scr[...], preferred_element_type=F32)
        up = jnp.dot(x, wu_scr[...], preferred_element_type=F32)
        hb = (_silu(gt) * up).astype(BF16)
        o_ref[...] = jnp.dot(hb, wd_scr[...], preferred_element_type=F32)


def _experts(x_pad, blk_e, n_used, w_gate, w_up, w_down, layer):
    grid_spec = pltpu.PrefetchScalarGridSpec(
        num_scalar_prefetch=2, grid=(MOE_NBLK,),
        in_specs=[pl.BlockSpec((MOE_BLK, D_MODEL), lambda j, be, nu: (j, 0)),
                  pl.BlockSpec((None, None, D_MODEL, D_EXPERT), lambda j, be, nu: (layer, be[j], 0, 0)),
                  pl.BlockSpec((None, None, D_MODEL, D_EXPERT), lambda j, be, nu: (layer, be[j], 0, 0)),
                  pl.BlockSpec((None, None, D_EXPERT, D_MODEL), lambda j, be, nu: (layer, be[j], 0, 0))],
        out_specs=pl.BlockSpec((MOE_BLK, D_MODEL), lambda j, be, nu: (j, 0)),
        scratch_shapes=[pltpu.VMEM((D_MODEL, D_EXPERT), BF16), pltpu.VMEM((D_MODEL, D_EXPERT), BF16),
                        pltpu.VMEM((D_EXPERT, D_MODEL), BF16)])
    return pl.pallas_call(
        _expert_kernel, grid_spec=grid_spec,
        out_shape=jax.ShapeDtypeStruct((MOE_NBLK * MOE_BLK, D_MODEL), F32),
        compiler_params=_params(("arbitrary",)), name="experts",
    )(blk_e, n_used, x_pad, w_gate, w_up, w_down)


def _combine_kernel(x_ref, ya_ref, yb_ref, rec_ref, gate_ref, fg_ref, o_ref, *, final):
    rec = rec_ref[...]
    y = rec[:, _R_W:_R_W + 1] * ya_ref[...] + rec[:, _R_W + 1:_R_W + 2] * yb_ref[...]
    xn = x_ref[...] + gate_ref[...] * y
    o_ref[...] = _rms(xn, fg_ref[...]) if final else xn


def _combine(x, ya, yb, rec, mods, final_g, final):
    tm = 512
    blk = pl.BlockSpec((tm, D_MODEL), lambda i: (i, 0))
    return pl.pallas_call(
        functools.partial(_combine_kernel, final=final), grid=(N_TOK // tm,),
        in_specs=[blk, blk, blk, pl.BlockSpec((tm, LANES), lambda i: (i, 0)), _mod_spec(5, tm),
                  pl.BlockSpec((1, D_MODEL), lambda i: (0, 0))],
        out_specs=blk, out_shape=jax.ShapeDtypeStruct((N_TOK, D_MODEL), F32),
        compiler_params=_params(("parallel",)), name="moe_combine",
    )(x, ya, yb, rec, mods, final_g.reshape(1, D_MODEL))


def _moe(x, h, rec, cnt, mods, w_gate, w_up, w_down, layer, final_g, final):
    dest, blk_e, n_used = _plan(rec, cnt)
    tok = jnp.arange(2 * N_TOK, dtype=jnp.int32) // 2
    row_tok = jnp.zeros((MOE_NBLK * MOE_BLK,), jnp.int32).at[dest.reshape(-1)].set(tok, unique_indices=True)
    y_pad = _experts(h[row_tok], blk_e, n_used, w_gate, w_up, w_down, layer)
    return _combine(x, y_pad[dest[:, 0]], y_pad[dest[:, 1]], rec, mods, final_g, final)


def _permute_even_w(w_in):
    n_ab = 4 * DN_HEADS
    ab0 = 4 * DN_HEADS * DN_DK
    pad = PROJ_W - w_in.shape[1]
    return jnp.concatenate([w_in[:, :ab0], w_in[:, ab0 + n_ab:], w_in[:, ab0:ab0 + n_ab],
                            jnp.zeros((D_MODEL, pad), F32)], axis=1)


def kernel(x_prompt, x_sample, c, cache_k, cache_v, state_delta, c_ctx, ada_w, ada_b, norm1_g, norm2_g, final_g,
           ev_w_in, ev_w_out, ev_conv_w, ev_a_log, ev_dt_bias, ev_onorm_g, ev_rpb, od_w_in, od_ln_g, od_ln_b,
           od_w_s, od_b_s, od_w_out, moe_w_rg, moe_b_rg, moe_w_re, moe_b_re, moe_w_gate, moe_w_up, moe_w_down):
    x = jnp.concatenate([x_prompt.reshape(N_CTX, D_MODEL), x_sample.reshape(N_LAT, D_MODEL)], axis=0)
    cond = jnp.concatenate([c_ctx[None, :], c, jnp.zeros((N_COND - 1 - DEC_BATCH, D_MODEL), F32)], axis=0)
    mods_all = _ada_mods(cond, ada_w, ada_b)
    kctx_all = cache_k.reshape(DEC_BATCH, -1, PAST_LEN, NA_HEADS * NA_HD)
    vctx_all = cache_v.reshape(DEC_BATCH, -1, PAST_LEN, NA_HEADS * NA_HD)

    ks, vs, ss = [], [], []
    for l in range(DEPTH):
        mods = mods_all[l]
        router = _router_weights(moe_w_rg[l], moe_b_rg[l], moe_w_re[l], moe_b_re[l])
        if l % 2 == 0:
            e = l // 2
            proj = _norm_mod_linear(x, mods, norm1_g[l], _permute_even_w(ev_w_in[e]), gelu=False)
            dn = (proj, ev_conv_w[e], ev_a_log[e], ev_dt_bias[e], ev_onorm_g[e])
            oa_ctx, s_fin = _delta_heads(*dn, SEQ, BATCH, 0, None)
            oa_lat, _ = _delta_heads(*dn, DEC_SEQ, DEC_BATCH, N_CTX // DEC_SEQ, state_delta[:, e])
            ob_ctx = _ctx_attention(proj)
            ob_lat = _na_attention(proj, kctx_all[:, e], vctx_all[:, e], ev_rpb[e])
            x, h, rec, cnt = _even_out(oa_ctx, ob_ctx, oa_lat, ob_lat, x, ev_w_out[e], mods, norm2_g[l],
                                       router)
            kv0 = _KB * LANES
            kv1 = _VB * LANES
            ks.append(proj[:N_CTX, kv0:kv0 + NA_HEADS * NA_HD].reshape(BATCH, SEQ, NA_HEADS, NA_HD))
            vs.append(proj[:N_CTX, kv1:kv1 + NA_HEADS * NA_HD].reshape(BATCH, SEQ, NA_HEADS, NA_HD))
            ss.append(s_fin)
        else:
            o = l // 2
            uv = _norm_mod_linear(x, mods, norm1_g[l], od_w_in[o], gelu=True)
            x, h, rec, cnt = _sgu_out(uv, x, od_ln_g[o], od_ln_b[o], od_w_s[o], od_b_s[o], od_w_out[o], mods,
                                norm2_g[l], router)
        x = _moe(x, h, rec, cnt, mods, moe_w_gate, moe_w_up, moe_w_down, l, final_g, l == DEPTH - 1)

    y_prompt = x[:N_CTX].reshape(BATCH, SEQ, D_MODEL)
    y_sample = x[N_CTX:].reshape(DEC_BATCH, DEC_SEQ, D_MODEL)
    return (y_prompt, y_sample, jnp.stack(ks, axis=1), jnp.stack(vs, axis=1), jnp.stack(ss, axis=1))
```

```python
import functools

import jax
import jax.numpy as jnp
from jax import lax
from jax.experimental import pallas as pl
from jax.experimental.pallas import tpu as pltpu

F32 = jnp.float32
BF16 = jnp.bfloat16

D_MODEL = 1024
BATCH = 16
SEQ = 256
DEPTH = 4
DEC_BATCH = 4
DEC_SEQ = 2048
PAST_LEN = 512
GRID_W = 64
EPS = 1e-6
NEG_INF = -1e30

DN_HEADS = 4
DN_DK = 128
DN_CHUNK = 64
NA_HEADS = 8
NA_HD = 64
NA_ROWS = 8
NA_COLS = 16
SG_CHUNK = 128
SG_GROUPS = 8
SG_W = 2 * D_MODEL
SG_GW = SG_W // SG_GROUPS
N_EGROUPS = 4
EXP_PER_GROUP = 8
N_EXPERTS = 32
D_EXPERT = 512

N_CTX = BATCH * SEQ
N_LAT = DEC_BATCH * DEC_SEQ
N_TOK = N_CTX + N_LAT
N_COND = 8
PROJ_W = 4096
LANES = 128
MOE_BLK = 256
MOE_NBLK = -(-(2 * N_TOK + N_EXPERTS * (MOE_BLK - 1)) // MOE_BLK)
VMEM_LIMIT = 56 * 1024 * 1024

_QA, _KA, _VA, _ZA, _QB, _KB, _VB, _AB = 0, 4, 8, 12, 16, 20, 24, 28


def _params(sem):
    return pltpu.CompilerParams(dimension_semantics=sem, vmem_limit_bytes=VMEM_LIMIT)


def _bdot(a, b):
    return jnp.dot(a.astype(BF16), b.astype(BF16), preferred_element_type=F32)


def _bdot_nt(a, b):
    return lax.dot_general(a.astype(BF16), b.astype(BF16), (((1,), (1,)), ((), ())),
                           preferred_element_type=F32)


def _bdot_tn(a, b):
    return lax.dot_general(a.astype(BF16), b.astype(BF16), (((0,), (0,)), ((), ())),
                           preferred_element_type=F32)


def _split3(a):
    p0 = a.astype(BF16)
    r = a - p0.astype(F32)
    p1 = r.astype(BF16)
    p2 = (r - p1.astype(F32)).astype(BF16)
    return p0, p1, p2


def _dot3(a, b):
    ah = a.astype(BF16)
    al = (a - ah.astype(F32)).astype(BF16)
    bh = b.astype(BF16)
    bl = (b - bh.astype(F32)).astype(BF16)
    return (jnp.dot(ah, bh, preferred_element_type=F32) + jnp.dot(ah, bl, preferred_element_type=F32)
            + jnp.dot(al, bh, preferred_element_type=F32))


def _mask_bf16(m01):
    return jnp.where(m01, 1.0, 0.0).astype(BF16)


def _xdot(m01, a):
    m = _mask_bf16(m01)
    p0, p1, p2 = _split3(a)
    return (jnp.dot(m, p0, preferred_element_type=F32) + jnp.dot(m, p1, preferred_element_type=F32)
            + jnp.dot(m, p2, preferred_element_type=F32))


def _xdot_nt(m01, a):
    m = _mask_bf16(m01)
    dn = (((1,), (1,)), ((), ()))
    p0, p1, p2 = _split3(a)
    return (lax.dot_general(m, p0, dn, preferred_element_type=F32)
            + lax.dot_general(m, p1, dn, preferred_element_type=F32)
            + lax.dot_general(m, p2, dn, preferred_element_type=F32))


def _xdot_r(a, m01):
    m = _mask_bf16(m01)
    p0, p1, p2 = _split3(a)
    return (jnp.dot(p0, m, preferred_element_type=F32) + jnp.dot(p1, m, preferred_element_type=F32)
            + jnp.dot(p2, m, preferred_element_type=F32))


def _sigmoid(x):
    return 1.0 / (1.0 + jnp.exp(-x))


def _silu(x):
    return x * _sigmoid(x)


def _rms(x, g):
    return x * lax.rsqrt(jnp.mean(x * x, axis=-1, keepdims=True) + EPS) * g


def _cond_index(row):
    return jnp.where(row < N_CTX, 0, 1 + (row - N_CTX) // DEC_SEQ)


def _mod_spec(k, tm):
    return pl.BlockSpec((None, None, 1, D_MODEL), lambda i, *_: (_cond_index(i * tm), k, 0, 0))


def _ada_kernel(c_ref, w_ref, b_ref, o_ref):
    o_ref[...] = _bdot(_silu(c_ref[...]), w_ref[...]) + b_ref[...]


def _ada_mods(cond, ada_w, ada_b):
    tn = 1536
    out = pl.pallas_call(
        _ada_kernel, grid=(DEPTH, 6 * D_MODEL // tn),
        in_specs=[pl.BlockSpec((N_COND, D_MODEL), lambda l, j: (0, 0)),
                  pl.BlockSpec((None, D_MODEL, tn), lambda l, j: (l, 0, j)),
                  pl.BlockSpec((None, 1, tn), lambda l, j: (l, 0, j))],
        out_specs=pl.BlockSpec((None, N_COND, tn), lambda l, j: (l, 0, j)),
        out_shape=jax.ShapeDtypeStruct((DEPTH, N_COND, 6 * D_MODEL), F32),
        compiler_params=_params(("parallel", "parallel")), name="ada_mods",
    )(cond, ada_w, ada_b.reshape(DEPTH, 1, 6 * D_MODEL))
    return out.reshape(DEPTH, N_COND, 6, 1, D_MODEL)


def _gelu_tanh(x):
    return x * (0.5 * (1.0 + jnp.tanh(0.7978845608028654 * (x + 0.044715 * (x * x * x)))))


def _lin_kernel(x_ref, g_ref, sh_ref, sc_ref, w_ref, o_ref, h_scr, *, gelu):
    @pl.when(pl.program_id(1) == 0)
    def _():
        h = _rms(x_ref[...], g_ref[...]) * (1.0 + sc_ref[...]) + sh_ref[...]
        h_scr[...] = h.astype(BF16)

    y = jnp.dot(h_scr[...], w_ref[...].astype(BF16), preferred_element_type=F32)
    o_ref[...] = _gelu_tanh(y) if gelu else y


def _norm_mod_linear(x, mods, g, w, gelu):
    tm, tn = 1024, 512
    return pl.pallas_call(
        functools.partial(_lin_kernel, gelu=gelu), grid=(N_TOK // tm, PROJ_W // tn),
        in_specs=[pl.BlockSpec((tm, D_MODEL), lambda i, j: (i, 0)),
                  pl.BlockSpec((1, D_MODEL), lambda i, j: (0, 0)),
                  _mod_spec(0, tm), _mod_spec(1, tm),
                  pl.BlockSpec((D_MODEL, tn), lambda i, j: (0, j))],
        out_specs=pl.BlockSpec((tm, tn), lambda i, j: (i, j)),
        out_shape=jax.ShapeDtypeStruct((N_TOK, PROJ_W), F32),
        scratch_shapes=[pltpu.VMEM((tm, D_MODEL), BF16)],
        compiler_params=_params(("parallel", "arbitrary")), name="norm_mod_linear",
    )(x, g.reshape(1, D_MODEL), mods, mods, w)


_CHUNK_SHIFT = DN_CHUNK.bit_length() - 1
_CUM_ROWS = 256
_DN_PREP = 4
_SERIES_FINE = 3
_MQ_ROWS = DN_DK + DN_CHUNK


def _dn_kernel(*refs, T, has_s0, want_state):
    it = iter(refs)
    q_ref, k_ref, v_ref, z_ref, ab_ref = (next(it) for _ in range(5))
    cwq_ref, cwk_ref, cwv_ref, alog_ref, dtb_ref, og_ref = (next(it) for _ in range(6))
    s0_ref = next(it) if has_s0 else None
    o_ref = next(it)
    sfin_ref = next(it) if want_state else None
    qc, kc, vc, gsc, bsc, osc, b_s, mq_s = (next(it) for _ in range(8))

    C = DN_CHUNK
    n = T // C
    hd = pl.program_id(1)

    row = lax.broadcasted_iota(jnp.int32, (T, 1), 0)

    def conv(x_ref, cw_ref):
        x = x_ref[...]
        xp = jnp.where(row == 0, 0.0, pltpu.roll(x, 1, 0))
        xn = jnp.where(row == T - 1, 0.0, pltpu.roll(x, T - 1, 0))
        return _silu(cw_ref[0:1, :] * xp + cw_ref[1:2, :] * x + cw_ref[2:3, :] * xn)

    def l2n(x):
        return x * lax.rsqrt(jnp.sum(x * x, axis=-1, keepdims=True) + EPS)

    qc[...] = l2n(conv(q_ref, cwq_ref)) * (DN_DK ** -0.5)
    kc[...] = l2n(conv(k_ref, cwk_ref))
    vc[...] = conv(v_ref, cwv_ref)

    ab = ab_ref[...]
    sel_r = lax.broadcasted_iota(jnp.int32, (LANES, LANES), 0)
    for d in range(2):
        alpha = _xdot_r(ab, sel_r == d * DN_HEADS + hd)
        blog = _xdot_r(ab, sel_r == 2 * DN_HEADS + d * DN_HEADS + hd)
        x = alpha + dtb_ref[d, hd]
        sp = jnp.maximum(x, 0.0) + jnp.log1p(jnp.exp(-jnp.abs(x)))
        a = jnp.exp(jnp.full((1, LANES), alog_ref[d, hd], F32))
        gsc[d] = -a * sp
        bsc[d] = _sigmoid(blog)

    pr = lax.broadcasted_iota(jnp.int32, (_CUM_ROWS, _CUM_ROWS), 0)
    pc = lax.broadcasted_iota(jnp.int32, (_CUM_ROWS, _CUM_ROWS), 1)
    same = lax.shift_right_logical(pr, _CHUNK_SHIFT) == lax.shift_right_logical(pc, _CHUNK_SHIFT)
    cum_mask = (jnp.logical_and(same, pc <= pr), jnp.logical_and(same, pc >= pr))

    def cum_body(i, carry):
        sl = pl.ds(pl.multiple_of(i * _CUM_ROWS, _CUM_ROWS), _CUM_ROWS)
        for d in range(2):
            gsc[d, sl, :] = _xdot(cum_mask[d], gsc[d, sl, :])
        return carry

    lax.fori_loop(0, T // _CUM_ROWS, cum_body, 0)

    ri = lax.broadcasted_iota(jnp.int32, (C, C), 0)
    ci = lax.broadcasted_iota(jnp.int32, (C, C), 1)
    eye = (ri == ci).astype(F32)

    def prepare(items):
        lows, decays = [], []
        for d, c in items:
            sl = pl.ds(pl.multiple_of(c * C, C), C)
            k, gc = kc[sl, :], gsc[d, sl, :]
            incl = (ci <= ri) if d == 0 else (ci >= ri)
            strict = (ci < ri) if d == 0 else (ci > ri)
            gr = jnp.transpose(gc)[0:1, :C]
            decay = jnp.where(incl, jnp.exp(jnp.where(incl, gc[:, :C] - gr, 0.0)), 0.0)
            lows.append(jnp.where(strict, _bdot_nt(k * bsc[d, sl, :], k) * decay, 0.0))
            decays.append(decay)
        ts = [eye - low for low in lows]
        ps = lows
        for step in range(5):
            dot = _dot3 if step < _SERIES_FINE else _bdot
            ps = [dot(p, p) for p in ps]
            ts = [t + dot(t, p) for t, p in zip(ts, ps)]
        for (d, c), t, decay in zip(items, ts, decays):
            sl = pl.ds(pl.multiple_of(c * C, C), C)
            q, k, gc, beta = qc[sl, :], kc[sl, :], gsc[d, sl, :], bsc[d, sl, :]
            eg = jnp.exp(gc)
            uw = _bdot(t, jnp.concatenate([vc[sl, :] * beta, k * beta * eg], axis=-1))
            last = gc[C - 1:C, :] if d == 0 else gc[0:1, :]
            wu = jnp.concatenate([uw[:, LANES:], uw[:, :LANES]], axis=-1).astype(BF16)
            kd = (k * jnp.exp(last - gc)).astype(BF16)
            attn = (_bdot_nt(q, k) * decay).astype(BF16)
            kdwu = lax.dot_general(kd, wu, (((0,), (0,)), ((), ())), preferred_element_type=F32)
            awu = jnp.dot(attn, wu, preferred_element_type=F32)
            mq0 = pl.multiple_of(c * _MQ_ROWS, _MQ_ROWS)
            mq_s[d, pl.ds(mq0, DN_DK), :] = kdwu[:, :LANES].astype(BF16)
            mq_s[d, pl.ds(mq0 + DN_DK, C), :] = (q * eg - awu[:, :LANES]).astype(BF16)
            b_s[d, pl.ds(pl.multiple_of(c * DN_DK, DN_DK), DN_DK), :] = kdwu[:, LANES:]
            osc[d, sl, :] = awu[:, LANES:]

    def prep_body(i, carry):
        prepare([(d, i * _DN_PREP + j) for j in range(_DN_PREP) for d in range(2)])
        return carry

    lax.fori_loop(0, n // _DN_PREP, prep_body, 0)

    def advance(d, c, S):
        sl = pl.ds(pl.multiple_of(c * C, C), C)
        ms = jnp.dot(mq_s[d, pl.ds(pl.multiple_of(c * _MQ_ROWS, _MQ_ROWS), _MQ_ROWS), :], S.astype(BF16),
                     preferred_element_type=F32)
        osc[d, sl, :] = osc[d, sl, :] + ms[DN_DK:]
        last = gsc[d, pl.ds(c * C + (C - 1 if d == 0 else 0), 1), :]
        return S * jnp.exp(last) - ms[:DN_DK] + b_s[d, pl.ds(pl.multiple_of(c * DN_DK, DN_DK), DN_DK), :]

    def body(i, carry):
        return advance(0, i, carry[0]), advance(1, n - 1 - i, carry[1])

    if has_s0:
        init = (s0_ref[0], s0_ref[1])
    else:
        init = (jnp.zeros((DN_DK, LANES), F32), jnp.zeros((DN_DK, LANES), F32))
    s_f, s_b = lax.fori_loop(0, n, body, init)
    if want_state:
        sfin_ref[0] = s_f
        sfin_ref[1] = s_b

    o = osc[0] + osc[1]
    o_ref[...] = _rms(o, og_ref[...]) * _silu(z_ref[...])


def _delta_heads(proj, conv_w, a_log, dt_bias, onorm_g, T, n_seq, row0, s0):
    has_s0 = s0 is not None
    want_state = not has_s0

    def col(cb):
        return pl.BlockSpec((T, LANES), lambda s, h: (row0 + s, cb + h))

    def cw(cb):
        return pl.BlockSpec((3, LANES), lambda s, h: (0, cb + h))

    smem = pl.BlockSpec(memory_space=pltpu.SMEM)
    in_specs = [col(_QA), col(_KA), col(_VA), col(_ZA),
                pl.BlockSpec((T, LANES), lambda s, h: (row0 + s, _AB)),
                cw(0), cw(4), cw(8), smem, smem,
                pl.BlockSpec((1, LANES), lambda s, h: (0, 0))]
    args = [proj, proj, proj, proj, proj, conv_w, conv_w, conv_w, a_log, dt_bias,
            onorm_g.reshape(1, LANES)]
    if has_s0:
        in_specs.append(pl.BlockSpec((None, 2, None, DN_DK, LANES), lambda s, h: (s, 0, h, 0, 0)))
        args.append(s0)
    out_shape = [jax.ShapeDtypeStruct((n_seq * T, DN_HEADS * LANES), F32)]
    out_specs = [pl.BlockSpec((T, LANES), lambda s, h: (s, h))]
    if want_state:
        out_shape.append(jax.ShapeDtypeStruct((n_seq, 2, DN_HEADS, DN_DK, LANES), F32))
        out_specs.append(pl.BlockSpec((None, 2, None, DN_DK, LANES), lambda s, h: (s, 0, h, 0, 0)))
    res = pl.pallas_call(
        functools.partial(_dn_kernel, T=T, has_s0=has_s0, want_state=want_state),
        grid=(n_seq, DN_HEADS), in_specs=in_specs, out_specs=out_specs, out_shape=out_shape,
        scratch_shapes=[pltpu.VMEM((T, LANES), F32)] * 3
        + [pltpu.VMEM((2, T, LANES), F32)] * 3
        + [pltpu.VMEM((2, T // DN_CHUNK * DN_DK, LANES), F32),
           pltpu.VMEM((2, T // DN_CHUNK * _MQ_ROWS, LANES), BF16)],
        compiler_params=_params(("parallel", "parallel")), name="delta_heads_%d" % T,
    )(*args)
    return res if want_state else (res[0], None)


def _softmax_rows(parts):
    m = parts[0].max(axis=-1, keepdims=True)
    for s in parts[1:]:
        m = jnp.maximum(m, s.max(axis=-1, keepdims=True))
    es = [jnp.exp(s - m) for s in parts]
    den = es[0].sum(axis=-1, keepdims=True)
    for e in es[1:]:
        den = den + e.sum(axis=-1, keepdims=True)
    return [e / den for e in es]


def _ctx_attn_kernel(q_ref, k_ref, v_ref, o_ref):
    outs = []
    for hh in range(2):
        sl = slice(hh * NA_HD, (hh + 1) * NA_HD)
        s = _bdot_nt(q_ref[:, sl], k_ref[:, sl]) * (NA_HD ** -0.5)
        (p,) = _softmax_rows([s])
        outs.append(_bdot(p, v_ref[:, sl]))
    o_ref[...] = jnp.concatenate(outs, axis=-1)


def _ctx_attention(proj):
    def col(cb):
        return pl.BlockSpec((SEQ, LANES), lambda s, p: (s, cb + p))

    return pl.pallas_call(
        _ctx_attn_kernel, grid=(BATCH, NA_HEADS // 2),
        in_specs=[col(_QB), col(_KB), col(_VB)],
        out_specs=pl.BlockSpec((SEQ, LANES), lambda s, p: (s, p)),
        out_shape=jax.ShapeDtypeStruct((N_CTX, NA_HEADS * NA_HD), F32),
        compiler_params=_params(("parallel", "parallel")), name="ctx_attention",
    )(proj, proj, proj)


_NA_UNROLL = 4


def _na_kernel(q_ref, k_ref, v_ref, kc_ref, vc_ref, bias_ref, o_ref, kb_scr, vb_scr, kcb_scr, vcb_scr):
    rows = DEC_SEQ // GRID_W
    win = NA_ROWS * GRID_W
    scale = NA_HD ** -0.5
    dn_nt = (((1,), (1,)), ((), ()))
    dn_tn = (((0,), (0,)), ((), ()))

    kb_scr[...] = k_ref[...].astype(BF16)
    vb_scr[...] = v_ref[...].astype(BF16)
    kcb_scr[...] = kc_ref[...].astype(BF16)
    vcb_scr[...] = vc_ref[...].astype(BF16)
    first = lax.broadcasted_iota(jnp.int32, (GRID_W, LANES), 1) < NA_HD

    def body(it, carry):
        rr = [it * _NA_UNROLL + j for j in range(_NA_UNROLL)]
        rss = [jnp.clip(r - NA_ROWS // 2, 0, rows - NA_ROWS) for r in rr]
        qsls = [pl.ds(pl.multiple_of(r * GRID_W, GRID_W), GRID_W) for r in rr]
        wsls = [pl.ds(pl.multiple_of(rs * GRID_W, GRID_W), win) for rs in rss]
        qms, s_wins, s_ctxs = [], [], []
        for r, rs, qsl, wsl in zip(rr, rss, qsls, wsls):
            q = q_ref[qsl, :] * scale
            qm = jnp.concatenate([jnp.where(first, q, 0.0), jnp.where(first, 0.0, q)], axis=0).astype(BF16)
            bias = jnp.concatenate([bias_ref[NA_ROWS - 1 - (r - rs) + i] for i in range(NA_ROWS)], axis=0)
            s_wins.append(lax.dot_general(kb_scr[wsl, :], qm, dn_nt, preferred_element_type=F32) + bias)
            s_ctxs.append(lax.dot_general(kcb_scr[...], qm, dn_nt, preferred_element_type=F32))
        ms = [jnp.maximum(jnp.max(sw, axis=0, keepdims=True), jnp.max(sc, axis=0, keepdims=True))
              for sw, sc in zip(s_wins, s_ctxs)]
        e_wins = [jnp.exp(sw - m) for sw, m in zip(s_wins, ms)]
        e_ctxs = [jnp.exp(sc - m) for sc, m in zip(s_ctxs, ms)]
        dens = [jnp.sum(ew, axis=0, keepdims=True) + jnp.sum(ec, axis=0, keepdims=True)
                for ew, ec in zip(e_wins, e_ctxs)]
        for qsl, wsl, ew, ec, den in zip(qsls, wsls, e_wins, e_ctxs, dens):
            o = (lax.dot_general(ew.astype(BF16), vb_scr[wsl, :], dn_tn, preferred_element_type=F32)
                 + lax.dot_general(ec.astype(BF16), vcb_scr[...], dn_tn, preferred_element_type=F32))
            o = o / jnp.transpose(jnp.broadcast_to(den, (LANES, LANES)))
            o_ref[qsl, :] = jnp.where(first, o[:GRID_W], o[GRID_W:])
        return carry

    lax.fori_loop(0, rows // _NA_UNROLL, body, 0)


def _na_bias_table(rpb):
    col = jnp.arange(GRID_W)
    cs = jnp.clip(col - NA_COLS // 2, 0, GRID_W - NA_COLS)
    col_ok = (col[None, :] >= cs[:, None]) & (col[None, :] < cs[:, None] + NA_COLS)
    dc = jnp.clip(col[None, :] - col[:, None] + NA_COLS - 1, 0, 2 * NA_COLS - 2)
    onehot = (dc.T[None, :, :] == jnp.arange(2 * NA_COLS - 1)[:, None, None]).astype(F32)
    t = jnp.einsum('hrd,dkq->hrkq', rpb.astype(F32), onehot, precision=lax.Precision.HIGHEST)
    t = jnp.where(col_ok.T[None, None], t, NEG_INF)
    t = t.reshape(NA_HEADS // 2, 2, 2 * NA_ROWS - 1, GRID_W, GRID_W)
    return jnp.concatenate([t[:, 0], t[:, 1]], axis=-1)


def _na_attention(proj, kctx, vctx, rpb):
    blk = N_CTX // DEC_SEQ

    def col(cb):
        return pl.BlockSpec((DEC_SEQ, LANES), lambda b, p: (blk + b, cb + p))

    ctx = pl.BlockSpec((None, PAST_LEN, LANES), lambda b, p: (b, 0, p))
    return pl.pallas_call(
        _na_kernel, grid=(DEC_BATCH, NA_HEADS // 2),
        in_specs=[col(_QB), col(_KB), col(_VB), ctx, ctx,
                  pl.BlockSpec((None, 2 * NA_ROWS - 1, GRID_W, 2 * GRID_W), lambda b, p: (p, 0, 0, 0))],
        out_specs=pl.BlockSpec((DEC_SEQ, LANES), lambda b, p: (b, p)),
        out_shape=jax.ShapeDtypeStruct((N_LAT, NA_HEADS * NA_HD), F32),
        scratch_shapes=[pltpu.VMEM((DEC_SEQ, LANES), BF16), pltpu.VMEM((DEC_SEQ, LANES), BF16),
                        pltpu.VMEM((PAST_LEN, LANES), BF16), pltpu.VMEM((PAST_LEN, LANES), BF16)],
        compiler_params=_params(("parallel", "parallel")), name="na_attention",
    )(proj, proj, proj, kctx, vctx, _na_bias_table(rpb))


_LOGIT0 = N_EGROUPS
_R_E, _R_W, _R_RANK = 0, 2, 4


def _lane_min_where(mask, lane):
    return jnp.min(jnp.where(mask, lane, LANES), axis=-1, keepdims=True)


def _route_rows(lg, carry_ref, tri_ref):
    big = -3.0e38
    lane = lax.broadcasted_iota(jnp.int32, lg.shape, 1)
    is_g = lane < N_EGROUPS
    gmax = jnp.max(jnp.where(is_g, lg, big), axis=-1, keepdims=True)
    gsum = jnp.sum(jnp.where(is_g, jnp.exp(jnp.where(is_g, lg - gmax, 0.0)), 0.0), axis=-1, keepdims=True)
    pg_top = 1.0 / gsum
    g_idx = _lane_min_where(jnp.logical_and(is_g, lg == gmax), lane)
    in_g = jnp.logical_and(lane >= _LOGIT0, lax.shift_right_arithmetic(lane - _LOGIT0, 3) == g_idx)
    in_g = jnp.logical_and(in_g, lane < _LOGIT0 + N_EXPERTS)
    m1 = jnp.max(jnp.where(in_g, lg, big), axis=-1, keepdims=True)
    i1 = _lane_min_where(jnp.logical_and(in_g, lg == m1), lane)
    rest = jnp.logical_and(in_g, lane != i1)
    m2 = jnp.max(jnp.where(rest, lg, big), axis=-1, keepdims=True)
    i2 = _lane_min_where(jnp.logical_and(rest, lg == m2), lane)
    e2 = jnp.exp(m2 - m1)
    w1 = pg_top * (1.0 / (1.0 + e2))
    w2 = pg_top * (e2 / (1.0 + e2))
    hit1 = lane == i1
    hit2 = lane == i2
    picked = jnp.where(jnp.logical_or(hit1, hit2), 1.0, 0.0)
    before = jnp.dot(tri_ref[...], picked.astype(BF16), preferred_element_type=F32) + carry_ref[...]
    r1 = jnp.sum(jnp.where(hit1, before, 0.0), axis=-1, keepdims=True)
    r2 = jnp.sum(jnp.where(hit2, before, 0.0), axis=-1, keepdims=True)
    carry_ref[...] = carry_ref[...] + jnp.sum(picked, axis=0, keepdims=True)
    rec = jnp.zeros(lg.shape, F32)
    for ln, val in ((_R_E, (i1 - _LOGIT0).astype(F32)), (_R_E + 1, (i2 - _LOGIT0).astype(F32)),
                    (_R_W, w1), (_R_W + 1, w2), (_R_RANK, r1), (_R_RANK + 1, r2)):
        rec = jnp.where(lane == ln, val, rec)
    return rec


def _moe_input(xnew, first, tail_in, tail_out, tail_scr):
    g2_ref, sc2_ref, sh2_ref, wrh_ref, wrl_ref, br_ref = tail_in
    x_out, h_out, rec_out, cnt_out = tail_out
    tri_scr, carry_scr = tail_scr

    @pl.when(first)
    def _():
        tm = tri_scr.shape[0]
        r = lax.broadcasted_iota(jnp.int32, (tm, tm), 0)
        c = lax.broadcasted_iota(jnp.int32, (tm, tm), 1)
        tri_scr[...] = jnp.where(c < r, 1.0, 0.0).astype(BF16)
        carry_scr[...] = jnp.zeros(carry_scr.shape, F32)

    x_out[...] = xnew
    h = _rms(xnew, g2_ref[...]) * (1.0 + sc2_ref[...]) + sh2_ref[...]
    hh = h.astype(BF16)
    hl = (h - hh.astype(F32)).astype(BF16)
    h_out[...] = hh
    lg = (jnp.dot(hh, wrh_ref[...], preferred_element_type=F32)
          + jnp.dot(hh, wrl_ref[...], preferred_element_type=F32)
          + jnp.dot(hl, wrh_ref[...], preferred_element_type=F32) + br_ref[...])
    rec_out[...] = _route_rows(lg, carry_scr, tri_scr)
    cnt_out[...] = carry_scr[...]


def _even_out_kernel(oac_ref, obc_ref, oal_ref, obl_ref, x_ref, w_ref, gate_ref, *rest, ctx_tiles):
    tail_in, tail_out, (w_scr,), tail_scr = rest[:6], rest[6:10], rest[10:11], rest[11:]
    first = pl.program_id(0) == 0

    @pl.when(first)
    def _():
        w_scr[...] = w_ref[...].astype(BF16)

    is_ctx = pl.program_id(0) < ctx_tiles
    o_a = jnp.where(is_ctx, oac_ref[...], oal_ref[...])
    o_b = jnp.where(is_ctx, obc_ref[...], obl_ref[...])
    mix = jnp.concatenate([o_a.astype(BF16), o_b.astype(BF16)], axis=-1)
    out = jnp.dot(mix, w_scr[...], preferred_element_type=F32)
    _moe_input(x_ref[...] + gate_ref[...] * out, first, tail_in, tail_out, tail_scr)


def _tail_specs(tm):
    const = lambda shape: pl.BlockSpec(shape, lambda i: (0,) * len(shape))
    in_specs = [_mod_spec(2, tm), const((1, D_MODEL)), _mod_spec(4, tm), _mod_spec(3, tm),
                const((D_MODEL, LANES)), const((D_MODEL, LANES)), const((1, LANES))]
    out_specs = [pl.BlockSpec((tm, D_MODEL), lambda i: (i, 0)),
                 pl.BlockSpec((tm, D_MODEL), lambda i: (i, 0)),
                 pl.BlockSpec((tm, LANES), lambda i: (i, 0)),
                 const((1, LANES))]
    out_shape = [jax.ShapeDtypeStruct((N_TOK, D_MODEL), F32),
                 jax.ShapeDtypeStruct((N_TOK, D_MODEL), BF16),
                 jax.ShapeDtypeStruct((N_TOK, LANES), F32),
                 jax.ShapeDtypeStruct((1, LANES), F32)]
    scratch = [pltpu.VMEM((tm, tm), BF16), pltpu.VMEM((1, LANES), F32)]
    return in_specs, out_specs, out_shape, scratch


def _router_weights(w_rg, b_rg, w_re, b_re):
    pad = LANES - N_EGROUPS - N_EXPERTS
    w = jnp.concatenate([w_rg, w_re, jnp.zeros((D_MODEL, pad), F32)], axis=1)
    b = jnp.concatenate([b_rg, b_re, jnp.zeros((pad,), F32)]).reshape(1, LANES)
    hi = w.astype(BF16)
    lo = (w - hi.astype(F32)).astype(BF16)
    return hi, lo, b


def _even_out(oa_ctx, ob_ctx, oa_lat, ob_lat, x, w_out, mods, g2, router):
    tm = 512
    ctx_tiles = N_CTX // tm
    tail_in, out_specs, out_shape, tail_scr = _tail_specs(tm)
    width = DN_HEADS * LANES
    ctxblk = pl.BlockSpec((tm, width), lambda i: (jnp.minimum(i, ctx_tiles - 1), 0))
    latblk = pl.BlockSpec((tm, width), lambda i: (jnp.maximum(i - ctx_tiles, 0), 0))
    return pl.pallas_call(
        functools.partial(_even_out_kernel, ctx_tiles=ctx_tiles), grid=(N_TOK // tm,),
        in_specs=[ctxblk, ctxblk, latblk, latblk, pl.BlockSpec((tm, D_MODEL), lambda i: (i, 0)),
                  pl.BlockSpec((D_MODEL, D_MODEL), lambda i: (0, 0))] + tail_in,
        out_specs=out_specs, out_shape=out_shape,
        scratch_shapes=[pltpu.VMEM((D_MODEL, D_MODEL), BF16)] + tail_scr,
        compiler_params=_params(("arbitrary",)), name="even_out",
    )(oa_ctx, ob_ctx, oa_lat, ob_lat, x, w_out, mods, g2.reshape(1, D_MODEL), mods, mods, *router)


def _sgu_kernel(uv_ref, x_ref, lng_ref, lnb_ref, ws_ref, bst_ref, w_ref, gate_ref, *rest, tm):
    tail_in, tail_out, (w_scr, m_scr), tail_scr = rest[:6], rest[6:10], rest[10:12], rest[12:]
    first = pl.program_id(0) == 0

    @pl.when(first)
    def _():
        w_scr[...] = w_ref[...].astype(BF16)

    for c in range(tm // SG_CHUNK):
        rs = slice(c * SG_CHUNK, (c + 1) * SG_CHUNK)
        v = uv_ref[rs, SG_W:]
        mu = jnp.mean(v, axis=-1, keepdims=True)
        vc = v - mu
        var = jnp.mean(vc * vc, axis=-1, keepdims=True)
        vn = (vc * lax.rsqrt(var + EPS) * lng_ref[...] + lnb_ref[...]).astype(BF16)
        for g in range(SG_GROUPS):
            cs = slice(g * SG_GW, (g + 1) * SG_GW)
            sp = jnp.dot(ws_ref[g].astype(BF16), vn[:, cs], preferred_element_type=F32) + bst_ref[:, g:g + 1]
            m_scr[rs, cs] = (uv_ref[rs, cs] * sp).astype(BF16)
    out = jnp.dot(m_scr[...], w_scr[...], preferred_element_type=F32)
    _moe_input(x_ref[...] + gate_ref[...] * out, first, tail_in, tail_out, tail_scr)


def _sgu_out(uv, x, ln_g, ln_b, w_s, b_s, w_out, mods, g2, router):
    tm = 256
    tail_in, out_specs, out_shape, tail_scr = _tail_specs(tm)
    const = lambda shape: pl.BlockSpec(shape, lambda i: (0,) * len(shape))
    return pl.pallas_call(
        functools.partial(_sgu_kernel, tm=tm), grid=(N_TOK // tm,),
        in_specs=[pl.BlockSpec((tm, 2 * SG_W), lambda i: (i, 0)),
                  pl.BlockSpec((tm, D_MODEL), lambda i: (i, 0)),
                  const((1, SG_W)), const((1, SG_W)), const((SG_GROUPS, SG_CHUNK, SG_CHUNK)),
                  const((SG_CHUNK, SG_GROUPS)), const((SG_W, D_MODEL))] + tail_in,
        out_specs=out_specs, out_shape=out_shape,
        scratch_shapes=[pltpu.VMEM((SG_W, D_MODEL), BF16), pltpu.VMEM((tm, SG_W), BF16)] + tail_scr,
        compiler_params=_params(("arbitrary",)), name="sgu_out",
    )(uv, x, ln_g.reshape(1, SG_W), ln_b.reshape(1, SG_W), w_s, b_s.T, w_out, mods,
      g2.reshape(1, D_MODEL), mods, mods, *router)


def _plan(rec, cnt):
    e_idx = rec[:, _R_E:_R_E + 2].astype(jnp.int32)
    rank = rec[:, _R_RANK:_R_RANK + 2].astype(jnp.int32)
    counts = cnt[0, _LOGIT0:_LOGIT0 + N_EXPERTS].astype(jnp.int32)
    padded = (counts + MOE_BLK - 1) // MOE_BLK * MOE_BLK
    pad_end = jnp.cumsum(padded)
    pad_start = pad_end - padded
    hit = e_idx[:, :, None] == jnp.arange(N_EXPERTS, dtype=jnp.int32)[None, None, :]
    dest = jnp.sum(jnp.where(hit, pad_start[None, None, :], 0), axis=-1) + rank
    blk0 = jnp.arange(MOE_NBLK, dtype=jnp.int32) * MOE_BLK
    blk_e = jnp.minimum(jnp.sum((pad_end[None, :] <= blk0[:, None]).astype(jnp.int32), axis=-1),
                        N_EXPERTS - 1)
    n_used = (pad_end[-1] // MOE_BLK).astype(jnp.int32).reshape(1)
    return dest, blk_e, n_used


def _expert_kernel(blk_e_ref, n_used_ref, x_ref, wg_ref, wu_ref, wd_ref, o_ref, wg_scr, wu_scr, wd_scr):
    j = pl.program_id(0)
    prev = blk_e_ref[jnp.maximum(j - 1, 0)]
    fresh = jnp.logical_or(j == 0, blk_e_ref[j] != prev)
    live = j < n_used_ref[0]

    @pl.when(jnp.logical_and(fresh, live))
    def _():
        wg_scr[...] = wg_ref[...].astype(BF16)
        wu_scr[...] = wu_ref[...].astype(BF16)
        wd_scr[...] = wd_ref[...].astype(BF16)

    @pl.when(live)
    def _():
        x = x_ref[...]
        gt = jnp.dot(x, wg_scr[...], preferred_element_type=F32)
        up = jnp.dot(x, wu_scr[...], preferred_element_type=F32)
        hb = (_silu(gt) * up).astype(BF16)
        o_ref[...] = jnp.dot(hb, wd_scr[...], preferred_element_type=F32)


def _experts(x_pad, blk_e, n_used, w_gate, w_up, w_down, layer):
    grid_spec = pltpu.PrefetchScalarGridSpec(
        num_scalar_prefetch=2, grid=(MOE_NBLK,),
        in_specs=[pl.BlockSpec((MOE_BLK, D_MODEL), lambda j, be, nu: (j, 0)),
                  pl.BlockSpec((None, None, D_MODEL, D_EXPERT), lambda j, be, nu: (layer, be[j], 0, 0)),
                  pl.BlockSpec((None, None, D_MODEL, D_EXPERT), lambda j, be, nu: (layer, be[j], 0, 0)),
                  pl.BlockSpec((None, None, D_EXPERT, D_MODEL), lambda j, be, nu: (layer, be[j], 0, 0))],
        out_specs=pl.BlockSpec((MOE_BLK, D_MODEL), lambda j, be, nu: (j, 0)),
        scratch_shapes=[pltpu.VMEM((D_MODEL, D_EXPERT), BF16), pltpu.VMEM((D_MODEL, D_EXPERT), BF16),
                        pltpu.VMEM((D_EXPERT, D_MODEL), BF16)])
    return pl.pallas_call(
        _expert_kernel, grid_spec=grid_spec,
        out_shape=jax.ShapeDtypeStruct((MOE_NBLK * MOE_BLK, D_MODEL), F32),
        compiler_params=_params(("arbitrary",)), name="experts",
    )(blk_e, n_used, x_pad, w_gate, w_up, w_down)


def _combine_kernel(x_ref, ya_ref, yb_ref, rec_ref, gate_ref, fg_ref, o_ref, *, final):
    rec = rec_ref[...]
    y = rec[:, _R_W:_R_W + 1] * ya_ref[...] + rec[:, _R_W + 1:_R_W + 2] * yb_ref[...]
    xn = x_ref[...] + gate_ref[...] * y
    o_ref[...] = _rms(xn, fg_ref[...]) if final else xn


def _combine(x, ya, yb, rec, mods, final_g, final):
    tm = 512
    blk = pl.BlockSpec((tm, D_MODEL), lambda i: (i, 0))
    return pl.pallas_call(
        functools.partial(_combine_kernel, final=final), grid=(N_TOK // tm,),
        in_specs=[blk, blk, blk, pl.BlockSpec((tm, LANES), lambda i: (i, 0)), _mod_spec(5, tm),
                  pl.BlockSpec((1, D_MODEL), lambda i: (0, 0))],
        out_specs=blk, out_shape=jax.ShapeDtypeStruct((N_TOK, D_MODEL), F32),
        compiler_params=_params(("parallel",)), name="moe_combine",
    )(x, ya, yb, rec, mods, final_g.reshape(1, D_MODEL))


def _moe(x, h, rec, cnt, mods, w_gate, w_up, w_down, layer, final_g, final):
    dest, blk_e, n_used = _plan(rec, cnt)
    tok = jnp.arange(2 * N_TOK, dtype=jnp.int32) // 2
    row_tok = jnp.zeros((MOE_NBLK * MOE_BLK,), jnp.int32).at[dest.reshape(-1)].set(tok, unique_indices=True)
    y_pad = _experts(h[row_tok], blk_e, n_used, w_gate, w_up, w_down, layer)
    return _combine(x, y_pad[dest[:, 0]], y_pad[dest[:, 1]], rec, mods, final_g, final)


def _permute_even_w(w_in):
    n_ab = 4 * DN_HEADS
    ab0 = 4 * DN_HEADS * DN_DK
    pad = PROJ_W - w_in.shape[1]
    return jnp.concatenate([w_in[:, :ab0], w_in[:, ab0 + n_ab:], w_in[:, ab0:ab0 + n_ab],
                            jnp.zeros((D_MODEL, pad), F32)], axis=1)


def kernel(x_prompt, x_sample, c, cache_k, cache_v, state_delta, c_ctx, ada_w, ada_b, norm1_g, norm2_g, final_g,
           ev_w_in, ev_w_out, ev_conv_w, ev_a_log, ev_dt_bias, ev_onorm_g, ev_rpb, od_w_in, od_ln_g, od_ln_b,
           od_w_s, od_b_s, od_w_out, moe_w_rg, moe_b_rg, moe_w_re, moe_b_re, moe_w_gate, moe_w_up, moe_w_down):
    x = jnp.concatenate([x_prompt.reshape(N_CTX, D_MODEL), x_sample.reshape(N_LAT, D_MODEL)], axis=0)
    cond = jnp.concatenate([c_ctx[None, :], c, jnp.zeros((N_COND - 1 - DEC_BATCH, D_MODEL), F32)], axis=0)
    mods_all = _ada_mods(cond, ada_w, ada_b)
    kctx_all = cache_k.reshape(DEC_BATCH, -1, PAST_LEN, NA_HEADS * NA_HD)
    vctx_all = cache_v.reshape(DEC_BATCH, -1, PAST_LEN, NA_HEADS * NA_HD)

    ks, vs, ss = [], [], []
    for l in range(DEPTH):
        mods = mods_all[l]
        router = _router_weights(moe_w_rg[l], moe_b_rg[l], moe_w_re[l], moe_b_re[l])
        if l % 2 == 0:
            e = l // 2
            proj = _norm_mod_linear(x, mods, norm1_g[l], _permute_even_w(ev_w_in[e]), gelu=False)
            dn = (proj, ev_conv_w[e], ev_a_log[e], ev_dt_bias[e], ev_onorm_g[e])
            oa_ctx, s_fin = _delta_heads(*dn, SEQ, BATCH, 0, None)
            oa_lat, _ = _delta_heads(*dn, DEC_SEQ, DEC_BATCH, N_CTX // DEC_SEQ, state_delta[:, e])
            ob_ctx = _ctx_attention(proj)
            ob_lat = _na_attention(proj, kctx_all[:, e], vctx_all[:, e], ev_rpb[e])
            x, h, rec, cnt = _even_out(oa_ctx, ob_ctx, oa_lat, ob_lat, x, ev_w_out[e], mods, norm2_g[l],
                                       router)
            kv0 = _KB * LANES
            kv1 = _VB * LANES
            ks.append(proj[:N_CTX, kv0:kv0 + NA_HEADS * NA_HD].reshape(BATCH, SEQ, NA_HEADS, NA_HD))
            vs.append(proj[:N_CTX, kv1:kv1 + NA_HEADS * NA_HD].reshape(BATCH, SEQ, NA_HEADS, NA_HD))
            ss.append(s_fin)
        else:
            o = l // 2
            uv = _norm_mod_linear(x, mods, norm1_g[l], od_w_in[o], gelu=True)
            x, h, rec, cnt = _sgu_out(uv, x, od_ln_g[o], od_ln_b[o], od_w_s[o], od_b_s[o], od_w_out[o], mods,
                                norm2_g[l], router)
        x = _moe(x, h, rec, cnt, mods, moe_w_gate, moe_w_up, moe_w_down, l, final_g, l == DEPTH - 1)

    y_prompt = x[:N_CTX].reshape(BATCH, SEQ, D_MODEL)
    y_sample = x[N_CTX:].reshape(DEC_BATCH, DEC_SEQ, D_MODEL)
    return (y_prompt, y_sample, jnp.stack(ks, axis=1), jnp.stack(vs, axis=1), jnp.stack(ss, axis=1))
```

```python
import functools

import jax
import jax.numpy as jnp
from jax import lax
from jax.experimental import pallas as pl
from jax.experimental.pallas import tpu as pltpu

F32 = jnp.float32
BF16 = jnp.bfloat16

D_MODEL = 1024
BATCH = 16
SEQ = 256
DEPTH = 4
DEC_BATCH = 4
DEC_SEQ = 2048
PAST_LEN = 512
GRID_W = 64
EPS = 1e-6
NEG_INF = -1e30

DN_HEADS = 4
DN_DK = 128
DN_CHUNK = 64
NA_HEADS = 8
NA_HD = 64
NA_ROWS = 8
NA_COLS = 16
SG_CHUNK = 128
SG_GROUPS = 8
SG_W = 2 * D_MODEL
SG_GW = SG_W // SG_GROUPS
N_EGROUPS = 4
EXP_PER_GROUP = 8
N_EXPERTS = 32
D_EXPERT = 512

N_CTX = BATCH * SEQ
N_LAT = DEC_BATCH * DEC_SEQ
N_TOK = N_CTX + N_LAT
N_COND = 8
PROJ_W = 4096
LANES = 128
MOE_BLK = 256
MOE_NBLK = -(-(2 * N_TOK + N_EXPERTS * (MOE_BLK - 1)) // MOE_BLK)
VMEM_LIMIT = 56 * 1024 * 1024

_QA, _KA, _VA, _ZA, _QB, _KB, _VB = 0, 4, 8, 12, 16, 20, 24


def _params(sem):
    return pltpu.CompilerParams(dimension_semantics=sem, vmem_limit_bytes=VMEM_LIMIT)


def _bdot(a, b):
    return jnp.dot(a.astype(BF16), b.astype(BF16), preferred_element_type=F32)


def _bdot_nt(a, b):
    return lax.dot_general(a.astype(BF16), b.astype(BF16), (((1,), (1,)), ((), ())),
                           preferred_element_type=F32)


def _bdot_tn(a, b):
    return lax.dot_general(a.astype(BF16), b.astype(BF16), (((0,), (0,)), ((), ())),
                           preferred_element_type=F32)


def _split3(a):
    p0 = a.astype(BF16)
    r = a - p0.astype(F32)
    p1 = r.astype(BF16)
    p2 = (r - p1.astype(F32)).astype(BF16)
    return p0, p1, p2


def _dot3(a, b):
    ah = a.astype(BF16)
    al = (a - ah.astype(F32)).astype(BF16)
    bh = b.astype(BF16)
    bl = (b - bh.astype(F32)).astype(BF16)
    return (jnp.dot(ah, bh, preferred_element_type=F32) + jnp.dot(ah, bl, preferred_element_type=F32)
            + jnp.dot(al, bh, preferred_element_type=F32))


def _mask_bf16(m01):
    return jnp.where(m01, 1.0, 0.0).astype(BF16)


def _xdot(m01, a):
    m = _mask_bf16(m01)
    p0, p1, p2 = _split3(a)
    return (jnp.dot(m, p0, preferred_element_type=F32) + jnp.dot(m, p1, preferred_element_type=F32)
            + jnp.dot(m, p2, preferred_element_type=F32))


def _xdot_nt(m01, a):
    m = _mask_bf16(m01)
    dn = (((1,), (1,)), ((), ()))
    p0, p1, p2 = _split3(a)
    return (lax.dot_general(m, p0, dn, preferred_element_type=F32)
            + lax.dot_general(m, p1, dn, preferred_element_type=F32)
            + lax.dot_general(m, p2, dn, preferred_element_type=F32))


def _xdot_r(a, m01):
    m = _mask_bf16(m01)
    p0, p1, p2 = _split3(a)
    return (jnp.dot(p0, m, preferred_element_type=F32) + jnp.dot(p1, m, preferred_element_type=F32)
            + jnp.dot(p2, m, preferred_element_type=F32))


def _sigmoid(x):
    return 1.0 / (1.0 + jnp.exp(-x))


def _silu(x):
    return x * _sigmoid(x)


def _rms(x, g):
    return x * lax.rsqrt(jnp.mean(x * x, axis=-1, keepdims=True) + EPS) * g


def _cond_index(row):
    return jnp.where(row < N_CTX, 0, 1 + (row - N_CTX) // DEC_SEQ)


def _mod_spec(k, tm):
    return pl.BlockSpec((None, None, 1, D_MODEL), lambda i, *_: (_cond_index(i * tm), k, 0, 0))


def _ada_kernel(c_ref, w_ref, b_ref, o_ref):
    o_ref[...] = _bdot(_silu(c_ref[...]), w_ref[...]) + b_ref[...]


def _ada_mods(cond, ada_w, ada_b):
    tn = 1536
    out = pl.pallas_call(
        _ada_kernel, grid=(DEPTH, 6 * D_MODEL // tn),
        in_specs=[pl.BlockSpec((N_COND, D_MODEL), lambda l, j: (0, 0)),
                  pl.BlockSpec((None, D_MODEL, tn), lambda l, j: (l, 0, j)),
                  pl.BlockSpec((None, 1, tn), lambda l, j: (l, 0, j))],
        out_specs=pl.BlockSpec((None, N_COND, tn), lambda l, j: (l, 0, j)),
        out_shape=jax.ShapeDtypeStruct((DEPTH, N_COND, 6 * D_MODEL), F32),
        compiler_params=_params(("parallel", "parallel")), name="ada_mods",
    )(cond, ada_w, ada_b.reshape(DEPTH, 1, 6 * D_MODEL))
    return out.reshape(DEPTH, N_COND, 6, 1, D_MODEL)


def _gelu_tanh(x):
    return x * (0.5 * (1.0 + jnp.tanh(0.7978845608028654 * (x + 0.044715 * (x * x * x)))))


def _lin_kernel(x_ref, g_ref, sh_ref, sc_ref, w_ref, o_ref, h_scr, *, gelu):
    @pl.when(pl.program_id(1) == 0)
    def _():
        h = _rms(x_ref[...], g_ref[...]) * (1.0 + sc_ref[...]) + sh_ref[...]
        h_scr[...] = h.astype(BF16)

    y = jnp.dot(h_scr[...], w_ref[...].astype(BF16), preferred_element_type=F32)
    o_ref[...] = _gelu_tanh(y) if gelu else y


def _norm_mod_linear(x, mods, g, w, gelu):
    tm, tn = 1024, 512
    return pl.pallas_call(
        functools.partial(_lin_kernel, gelu=gelu), grid=(N_TOK // tm, PROJ_W // tn),
        in_specs=[pl.BlockSpec((tm, D_MODEL), lambda i, j: (i, 0)),
                  pl.BlockSpec((1, D_MODEL), lambda i, j: (0, 0)),
                  _mod_spec(0, tm), _mod_spec(1, tm),
                  pl.BlockSpec((D_MODEL, tn), lambda i, j: (0, j))],
        out_specs=pl.BlockSpec((tm, tn), lambda i, j: (i, j)),
        out_shape=jax.ShapeDtypeStruct((N_TOK, PROJ_W), F32),
        scratch_shapes=[pltpu.VMEM((tm, D_MODEL), BF16)],
        compiler_params=_params(("parallel", "arbitrary")), name="norm_mod_linear",
    )(x, g.reshape(1, D_MODEL), mods, mods, w)


def _even_proj_kernel(x_ref, g_ref, sh_ref, sc_ref, w_ref, wab_ref, o_ref, ab_ref, kv_ref, h_scr, *, tn):
    j = pl.program_id(1)

    @pl.when(j == 0)
    def _():
        h = _rms(x_ref[...], g_ref[...]) * (1.0 + sc_ref[...]) + sh_ref[...]
        h_scr[...] = h.astype(BF16)
        ab_ref[...] = jnp.dot(h_scr[...], wab_ref[...].astype(BF16), preferred_element_type=F32)

    y = jnp.dot(h_scr[...], w_ref[...].astype(BF16), preferred_element_type=F32)
    for c in range(tn // LANES):
        o_ref[c] = y[:, c * LANES:(c + 1) * LANES].astype(BF16)

    @pl.when(j >= _KV_TILE0)
    def _():
        kv_ref[...] = y


_EV_TN = 512
_EV_W = 7 * DN_HEADS * LANES
_KV_TILE0 = _KB * LANES // _EV_TN


def _even_proj(x, mods, g, w_in):
    tm, tn = 1024, _EV_TN
    n_ab = 4 * DN_HEADS
    ab0 = 4 * DN_HEADS * DN_DK
    w_main = jnp.concatenate([w_in[:, :ab0], w_in[:, ab0 + n_ab:]], axis=1)
    w_ab = jnp.concatenate([w_in[:, ab0:ab0 + n_ab], jnp.zeros((D_MODEL, LANES - n_ab), F32)], axis=1)
    cb = tn // LANES
    return pl.pallas_call(
        functools.partial(_even_proj_kernel, tn=tn), grid=(N_TOK // tm, _EV_W // tn),
        in_specs=[pl.BlockSpec((tm, D_MODEL), lambda i, j: (i, 0)),
                  pl.BlockSpec((1, D_MODEL), lambda i, j: (0, 0)),
                  _mod_spec(0, tm), _mod_spec(1, tm),
                  pl.BlockSpec((D_MODEL, tn), lambda i, j: (0, j)),
                  pl.BlockSpec((D_MODEL, LANES), lambda i, j: (0, 0))],
        out_specs=[pl.BlockSpec((cb, tm, LANES), lambda i, j: (j, i, 0)),
                   pl.BlockSpec((tm, LANES), lambda i, j: (i, 0)),
                   pl.BlockSpec((tm, tn), lambda i, j: (i, jnp.maximum(j - _KV_TILE0, 0)))],
        out_shape=[jax.ShapeDtypeStruct((_EV_W // LANES, N_TOK, LANES), BF16),
                   jax.ShapeDtypeStruct((N_TOK, LANES), F32),
                   jax.ShapeDtypeStruct((N_TOK, 2 * NA_HEADS * NA_HD), F32)],
        scratch_shapes=[pltpu.VMEM((tm, D_MODEL), BF16)],
        compiler_params=_params(("parallel", "arbitrary")), name="even_proj",
    )(x, g.reshape(1, D_MODEL), mods, mods, w_main, w_ab)


_CHUNK_SHIFT = DN_CHUNK.bit_length() - 1
_CUM_ROWS = 256
_DN_PREP = 4
_SERIES_FINE = 3
_MQ_ROWS = DN_DK + DN_CHUNK


def _dn_kernel(*refs, T, has_s0, want_state):
    it = iter(refs)
    q_ref, k_ref, v_ref, z_ref, ab_ref = (next(it) for _ in range(5))
    cwq_ref, cwk_ref, cwv_ref, alog_ref, dtb_ref, og_ref = (next(it) for _ in range(6))
    s0_ref = next(it) if has_s0 else None
    o_ref = next(it)
    sfin_ref = next(it) if want_state else None
    qc, kc, vc, gsc, bsc, osc, b_s, mq_s = (next(it) for _ in range(8))

    C = DN_CHUNK
    n = T // C
    hd = pl.program_id(1)

    row = lax.broadcasted_iota(jnp.int32, (T, 1), 0)

    def conv(x_ref, cw_ref):
        x = x_ref[...].astype(F32)
        xp = jnp.where(row == 0, 0.0, pltpu.roll(x, 1, 0))
        xn = jnp.where(row == T - 1, 0.0, pltpu.roll(x, T - 1, 0))
        return _silu(cw_ref[0:1, :] * xp + cw_ref[1:2, :] * x + cw_ref[2:3, :] * xn)

    def l2n(x):
        return x * lax.rsqrt(jnp.sum(x * x, axis=-1, keepdims=True) + EPS)

    qc[...] = l2n(conv(q_ref, cwq_ref)) * (DN_DK ** -0.5)
    kc[...] = l2n(conv(k_ref, cwk_ref))
    vc[...] = conv(v_ref, cwv_ref)

    ab = ab_ref[...]
    sel_r = lax.broadcasted_iota(jnp.int32, (LANES, LANES), 0)
    for d in range(2):
        alpha = _xdot_r(ab, sel_r == d * DN_HEADS + hd)
        blog = _xdot_r(ab, sel_r == 2 * DN_HEADS + d * DN_HEADS + hd)
        x = alpha + dtb_ref[d, hd]
        sp = jnp.maximum(x, 0.0) + jnp.log1p(jnp.exp(-jnp.abs(x)))
        a = jnp.exp(jnp.full((1, LANES), alog_ref[d, hd], F32))
        gsc[d] = -a * sp
        bsc[d] = _sigmoid(blog)

    pr = lax.broadcasted_iota(jnp.int32, (_CUM_ROWS, _CUM_ROWS), 0)
    pc = lax.broadcasted_iota(jnp.int32, (_CUM_ROWS, _CUM_ROWS), 1)
    same = lax.shift_right_logical(pr, _CHUNK_SHIFT) == lax.shift_right_logical(pc, _CHUNK_SHIFT)
    cum_mask = (jnp.logical_and(same, pc <= pr), jnp.logical_and(same, pc >= pr))

    def cum_body(i, carry):
        sl = pl.ds(pl.multiple_of(i * _CUM_ROWS, _CUM_ROWS), _CUM_ROWS)
        for d in range(2):
            gsc[d, sl, :] = _xdot(cum_mask[d], gsc[d, sl, :])
        return carry

    lax.fori_loop(0, T // _CUM_ROWS, cum_body, 0)

    ri = lax.broadcasted_iota(jnp.int32, (C, C), 0)
    ci = lax.broadcasted_iota(jnp.int32, (C, C), 1)
    eye = (ri == ci).astype(F32)

    def prepare(items):
        lows, decays = [], []
        for d, c in items:
            sl = pl.ds(pl.multiple_of(c * C, C), C)
            k, gc = kc[sl, :], gsc[d, sl, :]
            incl = (ci <= ri) if d == 0 else (ci >= ri)
            strict = (ci < ri) if d == 0 else (ci > ri)
            gr = jnp.transpose(gc)[0:1, :C]
            decay = jnp.where(incl, jnp.exp(jnp.where(incl, gc[:, :C] - gr, 0.0)), 0.0)
            lows.append(jnp.where(strict, _bdot_nt(k * bsc[d, sl, :], k) * decay, 0.0))
            decays.append(decay)
        ts = [eye - low for low in lows]
        ps = lows
        for step in range(5):
            dot = _dot3 if step < _SERIES_FINE else _bdot
            ps = [dot(p, p) for p in ps]
            ts = [t + dot(t, p) for t, p in zip(ts, ps)]
        for (d, c), t, decay in zip(items, ts, decays):
            sl = pl.ds(pl.multiple_of(c * C, C), C)
            q, k, gc, beta = qc[sl, :], kc[sl, :], gsc[d, sl, :], bsc[d, sl, :]
            eg = jnp.exp(gc)
            uw = _bdot(t, jnp.concatenate([vc[sl, :] * beta, k * beta * eg], axis=-1))
            last = gc[C - 1:C, :] if d == 0 else gc[0:1, :]
            wu = jnp.concatenate([uw[:, LANES:], uw[:, :LANES]], axis=-1).astype(BF16)
            kd = (k * jnp.exp(last - gc)).astype(BF16)
            attn = (_bdot_nt(q, k) * decay).astype(BF16)
            kdwu = lax.dot_general(kd, wu, (((0,), (0,)), ((), ())), preferred_element_type=F32)
            awu = jnp.dot(attn, wu, preferred_element_type=F32)
            mq0 = pl.multiple_of(c * _MQ_ROWS, _MQ_ROWS)
            mq_s[d, pl.ds(mq0, DN_DK), :] = kdwu[:, :LANES].astype(BF16)
            mq_s[d, pl.ds(mq0 + DN_DK, C), :] = (q * eg - awu[:, :LANES]).astype(BF16)
            b_s[d, pl.ds(pl.multiple_of(c * DN_DK, DN_DK), DN_DK), :] = kdwu[:, LANES:]
            osc[d, sl, :] = awu[:, LANES:]

    def prep_body(i, carry):
        prepare([(d, i * _DN_PREP + j) for j in range(_DN_PREP) for d in range(2)])
        return carry

    lax.fori_loop(0, n // _DN_PREP, prep_body, 0)

    def advance(d, c, S):
        sl = pl.ds(pl.multiple_of(c * C, C), C)
        ms = jnp.dot(mq_s[d, pl.ds(pl.multiple_of(c * _MQ_ROWS, _MQ_ROWS), _MQ_ROWS), :], S.astype(BF16),
                     preferred_element_type=F32)
        osc[d, sl, :] = osc[d, sl, :] + ms[DN_DK:]
        last = gsc[d, pl.ds(c * C + (C - 1 if d == 0 else 0), 1), :]
        return S * jnp.exp(last) - ms[:DN_DK] + b_s[d, pl.ds(pl.multiple_of(c * DN_DK, DN_DK), DN_DK), :]

    def body(i, carry):
        return advance(0, i, carry[0]), advance(1, n - 1 - i, carry[1])

    if has_s0:
        init = (s0_ref[0], s0_ref[1])
    else:
        init = (jnp.zeros((DN_DK, LANES), F32), jnp.zeros((DN_DK, LANES), F32))
    s_f, s_b = lax.fori_loop(0, n, body, init)
    if want_state:
        sfin_ref[0] = s_f
        sfin_ref[1] = s_b

    o = osc[0] + osc[1]
    o_ref[...] = (_rms(o, og_ref[...]) * _silu(z_ref[...].astype(F32))).astype(o_ref.dtype)


def _delta_heads(proj, ab, conv_w, a_log, dt_bias, onorm_g, T, n_seq, row0, s0):
    has_s0 = s0 is not None
    want_state = not has_s0

    def col(cb):
        return pl.BlockSpec((None, T, LANES), lambda s, h: (cb + h, row0 + s, 0))

    def cw(cb):
        return pl.BlockSpec((3, LANES), lambda s, h: (0, cb + h))

    smem = pl.BlockSpec(memory_space=pltpu.SMEM)
    in_specs = [col(_QA), col(_KA), col(_VA), col(_ZA),
                pl.BlockSpec((T, LANES), lambda s, h: (row0 + s, 0)),
                cw(0), cw(4), cw(8), smem, smem,
                pl.BlockSpec((1, LANES), lambda s, h: (0, 0))]
    args = [proj, proj, proj, proj, ab, conv_w, conv_w, conv_w, a_log, dt_bias,
            onorm_g.reshape(1, LANES)]
    if has_s0:
        in_specs.append(pl.BlockSpec((None, 2, None, DN_DK, LANES), lambda s, h: (s, 0, h, 0, 0)))
        args.append(s0)
    out_shape = [jax.ShapeDtypeStruct((DN_HEADS, n_seq * T, LANES), BF16)]
    out_specs = [pl.BlockSpec((None, T, LANES), lambda s, h: (h, s, 0))]
    if want_state:
        out_shape.append(jax.ShapeDtypeStruct((n_seq, 2, DN_HEADS, DN_DK, LANES), F32))
        out_specs.append(pl.BlockSpec((None, 2, None, DN_DK, LANES), lambda s, h: (s, 0, h, 0, 0)))
    res = pl.pallas_call(
        functools.partial(_dn_kernel, T=T, has_s0=has_s0, want_state=want_state),
        grid=(n_seq, DN_HEADS), in_specs=in_specs, out_specs=out_specs, out_shape=out_shape,
        scratch_shapes=[pltpu.VMEM((T, LANES), F32)] * 3
        + [pltpu.VMEM((2, T, LANES), F32)] * 3
        + [pltpu.VMEM((2, T // DN_CHUNK * DN_DK, LANES), F32),
           pltpu.VMEM((2, T // DN_CHUNK * _MQ_ROWS, LANES), BF16)],
        compiler_params=_params(("parallel", "parallel")), name="delta_heads_%d" % T,
    )(*args)
    return res if want_state else (res[0], None)


def _pair_queries(q, first):
    return jnp.concatenate([jnp.where(first, q, 0.0), jnp.where(first, 0.0, q)], axis=0).astype(BF16)


def _ctx_attn_kernel(q_ref, k_ref, v_ref, o_ref):
    first = lax.broadcasted_iota(jnp.int32, (SEQ, LANES), 1) < NA_HD
    qm = _pair_queries(q_ref[...] * (NA_HD ** -0.5), first)
    s = lax.dot_general(k_ref[...], qm, (((1,), (1,)), ((), ())), preferred_element_type=F32)
    e = jnp.exp(s - jnp.max(s, axis=0, keepdims=True))
    den = jnp.sum(e, axis=0, keepdims=True)
    o = lax.dot_general(e.astype(BF16), v_ref[...], (((0,), (0,)), ((), ())), preferred_element_type=F32)
    o = jnp.where(first, o[:SEQ], o[SEQ:])
    den_t = jnp.transpose(jnp.broadcast_to(den, (LANES, 2 * SEQ)))
    o_ref[...] = (o / jnp.where(first, den_t[:SEQ], den_t[SEQ:])).astype(o_ref.dtype)


def _ctx_attention(proj):
    def col(cb):
        return pl.BlockSpec((None, SEQ, LANES), lambda s, p: (cb + p, s, 0))

    return pl.pallas_call(
        _ctx_attn_kernel, grid=(BATCH, NA_HEADS // 2),
        in_specs=[col(_QB), col(_KB), col(_VB)],
        out_specs=pl.BlockSpec((None, SEQ, LANES), lambda s, p: (p, s, 0)),
        out_shape=jax.ShapeDtypeStruct((NA_HEADS // 2, N_CTX, LANES), BF16),
        compiler_params=_params(("parallel", "parallel")), name="ctx_attention",
    )(proj, proj, proj)


_NA_UNROLL = 4


def _na_kernel(q_ref, k_ref, v_ref, kc_ref, vc_ref, bias_ref, o_ref, kcb_scr, vcb_scr):
    rows = DEC_SEQ // GRID_W
    win = NA_ROWS * GRID_W
    scale = NA_HD ** -0.5
    dn_nt = (((1,), (1,)), ((), ()))
    dn_tn = (((0,), (0,)), ((), ()))

    kcb_scr[...] = kc_ref[...].astype(BF16)
    vcb_scr[...] = vc_ref[...].astype(BF16)
    first = lax.broadcasted_iota(jnp.int32, (GRID_W, LANES), 1) < NA_HD

    def body(it, carry):
        rr = [it * _NA_UNROLL + j for j in range(_NA_UNROLL)]
        rss = [jnp.clip(r - NA_ROWS // 2, 0, rows - NA_ROWS) for r in rr]
        qsls = [pl.ds(pl.multiple_of(r * GRID_W, GRID_W), GRID_W) for r in rr]
        wsls = [pl.ds(pl.multiple_of(rs * GRID_W, GRID_W), win) for rs in rss]
        qms, s_wins, s_ctxs = [], [], []
        for r, rs, qsl, wsl in zip(rr, rss, qsls, wsls):
            qm = _pair_queries(q_ref[qsl, :] * scale, first)
            bias = jnp.concatenate([bias_ref[NA_ROWS - 1 - (r - rs) + i] for i in range(NA_ROWS)], axis=0)
            s_wins.append(lax.dot_general(k_ref[wsl, :], qm, dn_nt, preferred_element_type=F32) + bias)
            s_ctxs.append(lax.dot_general(kcb_scr[...], qm, dn_nt, preferred_element_type=F32))
        ms = [jnp.maximum(jnp.max(sw, axis=0, keepdims=True), jnp.max(sc, axis=0, keepdims=True))
              for sw, sc in zip(s_wins, s_ctxs)]
        e_wins = [jnp.exp(sw - m) for sw, m in zip(s_wins, ms)]
        e_ctxs = [jnp.exp(sc - m) for sc, m in zip(s_ctxs, ms)]
        dens = [jnp.sum(ew, axis=0, keepdims=True) + jnp.sum(ec, axis=0, keepdims=True)
                for ew, ec in zip(e_wins, e_ctxs)]
        for qsl, wsl, ew, ec, den in zip(qsls, wsls, e_wins, e_ctxs, dens):
            o = (lax.dot_general(ew.astype(BF16), v_ref[wsl, :], dn_tn, preferred_element_type=F32)
                 + lax.dot_general(ec.astype(BF16), vcb_scr[...], dn_tn, preferred_element_type=F32))
            o = o / jnp.transpose(jnp.broadcast_to(den, (LANES, LANES)))
            o_ref[qsl, :] = jnp.where(first, o[:GRID_W], o[GRID_W:]).astype(o_ref.dtype)
        return carry

    lax.fori_loop(0, rows // _NA_UNROLL, body, 0)


def _na_bias_table(rpb):
    col = jnp.arange(GRID_W)
    cs = jnp.clip(col - NA_COLS // 2, 0, GRID_W - NA_COLS)
    col_ok = (col[None, :] >= cs[:, None]) & (col[None, :] < cs[:, None] + NA_COLS)
    dc = jnp.clip(col[None, :] - col[:, None] + NA_COLS - 1, 0, 2 * NA_COLS - 2)
    onehot = (dc.T[None, :, :] == jnp.arange(2 * NA_COLS - 1)[:, None, None]).astype(F32)
    t = jnp.einsum('hrd,dkq->hrkq', rpb.astype(F32), onehot, precision=lax.Precision.HIGHEST)
    t = jnp.where(col_ok.T[None, None], t, NEG_INF)
    t = t.reshape(NA_HEADS // 2, 2, 2 * NA_ROWS - 1, GRID_W, GRID_W)
    return jnp.concatenate([t[:, 0], t[:, 1]], axis=-1)


def _na_attention(proj, kctx, vctx, rpb):
    blk = N_CTX // DEC_SEQ

    def col(cb):
        return pl.BlockSpec((None, DEC_SEQ, LANES), lambda b, p: (cb + p, blk + b, 0))

    ctx = pl.BlockSpec((None, PAST_LEN, LANES), lambda b, p: (b, 0, p))
    return pl.pallas_call(
        _na_kernel, grid=(DEC_BATCH, NA_HEADS // 2),
        in_specs=[col(_QB), col(_KB), col(_VB), ctx, ctx,
                  pl.BlockSpec((None, 2 * NA_ROWS - 1, GRID_W, 2 * GRID_W), lambda b, p: (p, 0, 0, 0))],
        out_specs=pl.BlockSpec((None, DEC_SEQ, LANES), lambda b, p: (p, b, 0)),
        out_shape=jax.ShapeDtypeStruct((NA_HEADS // 2, N_LAT, LANES), BF16),
        scratch_shapes=[pltpu.VMEM((PAST_LEN, LANES), BF16), pltpu.VMEM((PAST_LEN, LANES), BF16)],
        compiler_params=_params(("parallel", "parallel")), name="na_attention",
    )(proj, proj, proj, kctx, vctx, _na_bias_table(rpb))


_LOGIT0 = N_EGROUPS
_R_E, _R_W, _R_RANK = 0, 2, 4


def _lane_min_where(mask, lane):
    return jnp.min(jnp.where(mask, lane, LANES), axis=-1, keepdims=True)


def _route_rows(lg, carry_ref, tri_ref):
    big = -3.0e38
    lane = lax.broadcasted_iota(jnp.int32, lg.shape, 1)
    is_g = lane < N_EGROUPS
    gmax = jnp.max(jnp.where(is_g, lg, big), axis=-1, keepdims=True)
    gsum = jnp.sum(jnp.where(is_g, jnp.exp(jnp.where(is_g, lg - gmax, 0.0)), 0.0), axis=-1, keepdims=True)
    pg_top = 1.0 / gsum
    g_idx = _lane_min_where(jnp.logical_and(is_g, lg == gmax), lane)
    in_g = jnp.logical_and(lane >= _LOGIT0, lax.shift_right_arithmetic(lane - _LOGIT0, 3) == g_idx)
    in_g = jnp.logical_and(in_g, lane < _LOGIT0 + N_EXPERTS)
    m1 = jnp.max(jnp.where(in_g, lg, big), axis=-1, keepdims=True)
    i1 = _lane_min_where(jnp.logical_and(in_g, lg == m1), lane)
    rest = jnp.logical_and(in_g, lane != i1)
    m2 = jnp.max(jnp.where(rest, lg, big), axis=-1, keepdims=True)
    i2 = _lane_min_where(jnp.logical_and(rest, lg == m2), lane)
    e2 = jnp.exp(m2 - m1)
    w1 = pg_top * (1.0 / (1.0 + e2))
    w2 = pg_top * (e2 / (1.0 + e2))
    hit1 = lane == i1
    hit2 = lane == i2
    picked = jnp.where(jnp.logical_or(hit1, hit2), 1.0, 0.0)
    before = jnp.dot(tri_ref[...], picked.astype(BF16), preferred_element_type=F32) + carry_ref[...]
    r1 = jnp.sum(jnp.where(hit1, before, 0.0), axis=-1, keepdims=True)
    r2 = jnp.sum(jnp.where(hit2, before, 0.0), axis=-1, keepdims=True)
    carry_ref[...] = carry_ref[...] + jnp.sum(picked, axis=0, keepdims=True)
    rec = jnp.zeros(lg.shape, F32)
    for ln, val in ((_R_E, (i1 - _LOGIT0).astype(F32)), (_R_E + 1, (i2 - _LOGIT0).astype(F32)),
                    (_R_W, w1), (_R_W + 1, w2), (_R_RANK, r1), (_R_RANK + 1, r2)):
        rec = jnp.where(lane == ln, val, rec)
    return rec


def _moe_input(xnew, first, tail_in, tail_out, tail_scr):
    g2_ref, sc2_ref, sh2_ref, wrh_ref, wrl_ref, br_ref = tail_in
    x_out, h_out, rec_out, cnt_out = tail_out
    tri_scr, carry_scr = tail_scr

    @pl.when(first)
    def _():
        tm = tri_scr.shape[0]
        r = lax.broadcasted_iota(jnp.int32, (tm, tm), 0)
        c = lax.broadcasted_iota(jnp.int32, (tm, tm), 1)
        tri_scr[...] = jnp.where(c < r, 1.0, 0.0).astype(BF16)
        carry_scr[...] = jnp.zeros(carry_scr.shape, F32)

    x_out[...] = xnew
    h = _rms(xnew, g2_ref[...]) * (1.0 + sc2_ref[...]) + sh2_ref[...]
    hh = h.astype(BF16)
    hl = (h - hh.astype(F32)).astype(BF16)
    h_out[...] = hh
    lg = (jnp.dot(hh, wrh_ref[...], preferred_element_type=F32)
          + jnp.dot(hh, wrl_ref[...], preferred_element_type=F32)
          + jnp.dot(hl, wrh_ref[...], preferred_element_type=F32) + br_ref[...])
    rec_out[...] = _route_rows(lg, carry_scr, tri_scr)
    cnt_out[...] = carry_scr[...]


def _even_out_kernel(oac_ref, obc_ref, oal_ref, obl_ref, x_ref, w_ref, gate_ref, *rest, ctx_tiles):
    tail_in, tail_out, (w_scr,), tail_scr = rest[:6], rest[6:10], rest[10:11], rest[11:]
    first = pl.program_id(0) == 0

    @pl.when(first)
    def _():
        w_scr[...] = w_ref[...].astype(BF16)

    is_ctx = pl.program_id(0) < ctx_tiles
    parts = [jnp.where(is_ctx, c_ref[hb], l_ref[hb])
             for c_ref, l_ref in ((oac_ref, oal_ref), (obc_ref, obl_ref)) for hb in range(DN_HEADS)]
    mix = jnp.concatenate(parts, axis=-1)
    out = jnp.dot(mix, w_scr[...], preferred_element_type=F32)
    _moe_input(x_ref[...] + gate_ref[...] * out, first, tail_in, tail_out, tail_scr)


def _tail_specs(tm):
    const = lambda shape: pl.BlockSpec(shape, lambda i: (0,) * len(shape))
    in_specs = [_mod_spec(2, tm), const((1, D_MODEL)), _mod_spec(4, tm), _mod_spec(3, tm),
                const((D_MODEL, LANES)), const((D_MODEL, LANES)), const((1, LANES))]
    out_specs = [pl.BlockSpec((tm, D_MODEL), lambda i: (i, 0)),
                 pl.BlockSpec((tm, D_MODEL), lambda i: (i, 0)),
                 pl.BlockSpec((tm, LANES), lambda i: (i, 0)),
                 const((1, LANES))]
    out_shape = [jax.ShapeDtypeStruct((N_TOK, D_MODEL), F32),
                 jax.ShapeDtypeStruct((N_TOK, D_MODEL), BF16),
                 jax.ShapeDtypeStruct((N_TOK, LANES), F32),
                 jax.ShapeDtypeStruct((1, LANES), F32)]
    scratch = [pltpu.VMEM((tm, tm), BF16), pltpu.VMEM((1, LANES), F32)]
    return in_specs, out_specs, out_shape, scratch


def _router_weights(w_rg, b_rg, w_re, b_re):
    pad = LANES - N_EGROUPS - N_EXPERTS
    w = jnp.concatenate([w_rg, w_re, jnp.zeros((D_MODEL, pad), F32)], axis=1)
    b = jnp.concatenate([b_rg, b_re, jnp.zeros((pad,), F32)]).reshape(1, LANES)
    hi = w.astype(BF16)
    lo = (w - hi.astype(F32)).astype(BF16)
    return hi, lo, b


def _even_out(oa_ctx, ob_ctx, oa_lat, ob_lat, x, w_out, mods, g2, router):
    tm = 512
    ctx_tiles = N_CTX // tm
    tail_in, out_specs, out_shape, tail_scr = _tail_specs(tm)
    ctxblk = pl.BlockSpec((DN_HEADS, tm, LANES), lambda i: (0, jnp.minimum(i, ctx_tiles - 1), 0))
    latblk = pl.BlockSpec((DN_HEADS, tm, LANES), lambda i: (0, jnp.maximum(i - ctx_tiles, 0), 0))
    return pl.pallas_call(
        functools.partial(_even_out_kernel, ctx_tiles=ctx_tiles), grid=(N_TOK // tm,),
        in_specs=[ctxblk, ctxblk, latblk, latblk, pl.BlockSpec((tm, D_MODEL), lambda i: (i, 0)),
                  pl.BlockSpec((D_MODEL, D_MODEL), lambda i: (0, 0))] + tail_in,
        out_specs=out_specs, out_shape=out_shape,
        scratch_shapes=[pltpu.VMEM((D_MODEL, D_MODEL), BF16)] + tail_scr,
        compiler_params=_params(("arbitrary",)), name="even_out",
    )(oa_ctx, ob_ctx, oa_lat, ob_lat, x, w_out, mods, g2.reshape(1, D_MODEL), mods, mods, *router)


def _sgu_kernel(uv_ref, x_ref, lng_ref, lnb_ref, ws_ref, bst_ref, w_ref, gate_ref, *rest, tm):
    tail_in, tail_out, (w_scr, m_scr), tail_scr = rest[:6], rest[6:10], rest[10:12], rest[12:]
    first = pl.program_id(0) == 0

    @pl.when(first)
    def _():
        w_scr[...] = w_ref[...].astype(BF16)

    for c in range(tm // SG_CHUNK):
        rs = slice(c * SG_CHUNK, (c + 1) * SG_CHUNK)
        v = uv_ref[rs, SG_W:]
        mu = jnp.mean(v, axis=-1, keepdims=True)
        vc = v - mu
        var = jnp.mean(vc * vc, axis=-1, keepdims=True)
        vn = (vc * lax.rsqrt(var + EPS) * lng_ref[...] + lnb_ref[...]).astype(BF16)
        for g in range(SG_GROUPS):
            cs = slice(g * SG_GW, (g + 1) * SG_GW)
            sp = jnp.dot(ws_ref[g].astype(BF16), vn[:, cs], preferred_element_type=F32) + bst_ref[:, g:g + 1]
            m_scr[rs, cs] = (uv_ref[rs, cs] * sp).astype(BF16)
    out = jnp.dot(m_scr[...], w_scr[...], preferred_element_type=F32)
    _moe_input(x_ref[...] + gate_ref[...] * out, first, tail_in, tail_out, tail_scr)


def _sgu_out(uv, x, ln_g, ln_b, w_s, b_s, w_out, mods, g2, router):
    tm = 256
    tail_in, out_specs, out_shape, tail_scr = _tail_specs(tm)
    const = lambda shape: pl.BlockSpec(shape, lambda i: (0,) * len(shape))
    return pl.pallas_call(
        functools.partial(_sgu_kernel, tm=tm), grid=(N_TOK // tm,),
        in_specs=[pl.BlockSpec((tm, 2 * SG_W), lambda i: (i, 0)),
                  pl.BlockSpec((tm, D_MODEL), lambda i: (i, 0)),
                  const((1, SG_W)), const((1, SG_W)), const((SG_GROUPS, SG_CHUNK, SG_CHUNK)),
                  const((SG_CHUNK, SG_GROUPS)), const((SG_W, D_MODEL))] + tail_in,
        out_specs=out_specs, out_shape=out_shape,
        scratch_shapes=[pltpu.VMEM((SG_W, D_MODEL), BF16), pltpu.VMEM((tm, SG_W), BF16)] + tail_scr,
        compiler_params=_params(("arbitrary",)), name="sgu_out",
    )(uv, x, ln_g.reshape(1, SG_W), ln_b.reshape(1, SG_W), w_s, b_s.T, w_out, mods,
      g2.reshape(1, D_MODEL), mods, mods, *router)


def _plan(rec, cnt):
    e_idx = rec[:, _R_E:_R_E + 2].astype(jnp.int32)
    rank = rec[:, _R_RANK:_R_RANK + 2].astype(jnp.int32)
    counts = cnt[0, _LOGIT0:_LOGIT0 + N_EXPERTS].astype(jnp.int32)
    padded = (counts + MOE_BLK - 1) // MOE_BLK * MOE_BLK
    pad_end = jnp.cumsum(padded)
    pad_start = pad_end - padded
    hit = e_idx[:, :, None] == jnp.arange(N_EXPERTS, dtype=jnp.int32)[None, None, :]
    dest = jnp.sum(jnp.where(hit, pad_start[None, None, :], 0), axis=-1) + rank
    blk0 = jnp.arange(MOE_NBLK, dtype=jnp.int32) * MOE_BLK
    blk_e = jnp.minimum(jnp.sum((pad_end[None, :] <= blk0[:, None]).astype(jnp.int32), axis=-1),
                        N_EXPERTS - 1)
    n_used = (pad_end[-1] // MOE_BLK).astype(jnp.int32).reshape(1)
    return dest, blk_e, n_used


def _expert_kernel(blk_e_ref, n_used_ref, x_ref, wg_ref, wu_ref, wd_ref, o_ref, wg_scr, wu_scr, wd_scr):
    j = pl.program_id(0)
    prev = blk_e_ref[jnp.maximum(j - 1, 0)]
    fresh = jnp.logical_or(j == 0, blk_e_ref[j] != prev)
    live = j < n_used_ref[0]

    @pl.when(jnp.logical_and(fresh, live))
    def _():
        wg_scr[...] = wg_ref[...].astype(BF16)
        wu_scr[...] = wu_ref[...].astype(BF16)
        wd_scr[...] = wd_ref[...].astype(BF16)

    @pl.when(live)
    def _():
        x = x_ref[...]
        gt = jnp.dot(x, wg_scr[...], preferred_element_type=F32)
        up = jnp.dot(x, wu_scr[...], preferred_element_type=F32)
        hb = (_silu(gt) * up).astype(BF16)
        o_ref[...] = jnp.dot(hb, wd_scr[...], preferred_element_type=F32)


def _experts(x_pad, blk_e, n_used, w_gate, w_up, w_down, layer):
    grid_spec = pltpu.PrefetchScalarGridSpec(
        num_scalar_prefetch=2, grid=(MOE_NBLK,),
        in_specs=[pl.BlockSpec((MOE_BLK, D_MODEL), lambda j, be, nu: (j, 0)),
                  pl.BlockSpec((None, None, D_MODEL, D_EXPERT), lambda j, be, nu: (layer, be[j], 0, 0)),
                  pl.BlockSpec((None, None, D_MODEL, D_EXPERT), lambda j, be, nu: (layer, be[j], 0, 0)),
                  pl.BlockSpec((None, None, D_EXPERT, D_MODEL), lambda j, be, nu: (layer, be[j], 0, 0))],
        out_specs=pl.BlockSpec((MOE_BLK, D_MODEL), lambda j, be, nu: (j, 0)),
        scratch_shapes=[pltpu.VMEM((D_MODEL, D_EXPERT), BF16), pltpu.VMEM((D_MODEL, D_EXPERT), BF16),
                        pltpu.VMEM((D_EXPERT, D_MODEL), BF16)])
    return pl.pallas_call(
        _expert_kernel, grid_spec=grid_spec,
        out_shape=jax.ShapeDtypeStruct((MOE_NBLK * MOE_BLK, D_MODEL), F32),
        compiler_params=_params(("arbitrary",)), name="experts",
    )(blk_e, n_used, x_pad, w_gate, w_up, w_down)


def _combine_kernel(x_ref, ya_ref, yb_ref, rec_ref, gate_ref, fg_ref, o_ref, *, final):
    rec = rec_ref[...]
    y = rec[:, _R_W:_R_W + 1] * ya_ref[...] + rec[:, _R_W + 1:_R_W + 2] * yb_ref[...]
    xn = x_ref[...] + gate_ref[...] * y
    o_ref[...] = _rms(xn, fg_ref[...]) if final else xn


def _combine(x, ya, yb, rec, mods, final_g, final):
    tm = 512
    blk = pl.BlockSpec((tm, D_MODEL), lambda i: (i, 0))
    return pl.pallas_call(
        functools.partial(_combine_kernel, final=final), grid=(N_TOK // tm,),
        in_specs=[blk, blk, blk, pl.BlockSpec((tm, LANES), lambda i: (i, 0)), _mod_spec(5, tm),
                  pl.BlockSpec((1, D_MODEL), lambda i: (0, 0))],
        out_specs=blk, out_shape=jax.ShapeDtypeStruct((N_TOK, D_MODEL), F32),
        compiler_params=_params(("parallel",)), name="moe_combine",
    )(x, ya, yb, rec, mods, final_g.reshape(1, D_MODEL))


def _moe(x, h, rec, cnt, mods, w_gate, w_up, w_down, layer, final_g, final):
    dest, blk_e, n_used = _plan(rec, cnt)
    tok = jnp.arange(2 * N_TOK, dtype=jnp.int32) // 2
    row_tok = jnp.zeros((MOE_NBLK * MOE_BLK,), jnp.int32).at[dest.reshape(-1)].set(tok, unique_indices=True)
    y_pad = _experts(h[row_tok], blk_e, n_used, w_gate, w_up, w_down, layer)
    return _combine(x, y_pad[dest[:, 0]], y_pad[dest[:, 1]], rec, mods, final_g, final)


def kernel(x_prompt, x_sample, c, cache_k, cache_v, state_delta, c_ctx, ada_w, ada_b, norm1_g, norm2_g, final_g,
           ev_w_in, ev_w_out, ev_conv_w, ev_a_log, ev_dt_bias, ev_onorm_g, ev_rpb, od_w_in, od_ln_g, od_ln_b,
           od_w_s, od_b_s, od_w_out, moe_w_rg, moe_b_rg, moe_w_re, moe_b_re, moe_w_gate, moe_w_up, moe_w_down):
    x = jnp.concatenate([x_prompt.reshape(N_CTX, D_MODEL), x_sample.reshape(N_LAT, D_MODEL)], axis=0)
    cond = jnp.concatenate([c_ctx[None, :], c, jnp.zeros((N_COND - 1 - DEC_BATCH, D_MODEL), F32)], axis=0)
    mods_all = _ada_mods(cond, ada_w, ada_b)
    kctx_all = cache_k.reshape(DEC_BATCH, -1, PAST_LEN, NA_HEADS * NA_HD)
    vctx_all = cache_v.reshape(DEC_BATCH, -1, PAST_LEN, NA_HEADS * NA_HD)

    ks, vs, ss = [], [], []
    for l in range(DEPTH):
        mods = mods_all[l]
        router = _router_weights(moe_w_rg[l], moe_b_rg[l], moe_w_re[l], moe_b_re[l])
        if l % 2 == 0:
            e = l // 2
            proj, ab, kv = _even_proj(x, mods, norm1_g[l], ev_w_in[e])
            dn = (proj, ab, ev_conv_w[e], ev_a_log[e], ev_dt_bias[e], ev_onorm_g[e])
            oa_ctx, s_fin = _delta_heads(*dn, SEQ, BATCH, 0, None)
            oa_lat, _ = _delta_heads(*dn, DEC_SEQ, DEC_BATCH, N_CTX // DEC_SEQ, state_delta[:, e])
            ob_ctx = _ctx_attention(proj)
            ob_lat = _na_attention(proj, kctx_all[:, e], vctx_all[:, e], ev_rpb[e])
            x, h, rec, cnt = _even_out(oa_ctx, ob_ctx, oa_lat, ob_lat, x, ev_w_out[e], mods, norm2_g[l],
                                       router)
            na_w = NA_HEADS * NA_HD
            ks.append(kv[:N_CTX, :na_w].reshape(BATCH, SEQ, NA_HEADS, NA_HD))
            vs.append(kv[:N_CTX, na_w:].reshape(BATCH, SEQ, NA_HEADS, NA_HD))
            ss.append(s_fin)
        else:
            o = l // 2
            uv = _norm_mod_linear(x, mods, norm1_g[l], od_w_in[o], gelu=True)
            x, h, rec, cnt = _sgu_out(uv, x, od_ln_g[o], od_ln_b[o], od_w_s[o], od_b_s[o], od_w_out[o], mods,
                                norm2_g[l], router)
        x = _moe(x, h, rec, cnt, mods, moe_w_gate, moe_w_up, moe_w_down, l, final_g, l == DEPTH - 1)

    y_prompt = x[:N_CTX].reshape(BATCH, SEQ, D_MODEL)
    y_sample = x[N_CTX:].reshape(DEC_BATCH, DEC_SEQ, D_MODEL)
    return (y_prompt, y_sample, jnp.stack(ks, axis=1), jnp.stack(vs, axis=1), jnp.stack(ss, axis=1))
```

```python
import functools

import jax
import jax.numpy as jnp
from jax import lax
from jax.experimental import pallas as pl
from jax.experimental.pallas import tpu as pltpu

F32 = jnp.float32
BF16 = jnp.bfloat16

D_MODEL = 1024
BATCH = 16
SEQ = 256
DEPTH = 4
DEC_BATCH = 4
DEC_SEQ = 2048
PAST_LEN = 512
GRID_W = 64
EPS = 1e-6
NEG_INF = -1e30

DN_HEADS = 4
DN_DK = 128
DN_CHUNK = 64
NA_HEADS = 8
NA_HD = 64
NA_ROWS = 8
NA_COLS = 16
SG_CHUNK = 128
SG_GROUPS = 8
SG_W = 2 * D_MODEL
SG_GW = SG_W // SG_GROUPS
N_EGROUPS = 4
EXP_PER_GROUP = 8
N_EXPERTS = 32
D_EXPERT = 512

N_CTX = BATCH * SEQ
N_LAT = DEC_BATCH * DEC_SEQ
N_TOK = N_CTX + N_LAT
N_COND = 8
PROJ_W = 4096
LANES = 128
MOE_BLK = 256
MOE_NBLK = -(-(2 * N_TOK + N_EXPERTS * (MOE_BLK - 1)) // MOE_BLK)
VMEM_LIMIT = 56 * 1024 * 1024

_QA, _KA, _VA, _ZA, _QB, _KB, _VB = 0, 4, 8, 12, 16, 20, 24


def _params(sem):
    return pltpu.CompilerParams(dimension_semantics=sem, vmem_limit_bytes=VMEM_LIMIT)


def _bdot(a, b):
    return jnp.dot(a.astype(BF16), b.astype(BF16), preferred_element_type=F32)


def _bdot_nt(a, b):
    return lax.dot_general(a.astype(BF16), b.astype(BF16), (((1,), (1,)), ((), ())),
                           preferred_element_type=F32)


def _bdot_tn(a, b):
    return lax.dot_general(a.astype(BF16), b.astype(BF16), (((0,), (0,)), ((), ())),
                           preferred_element_type=F32)


def _split3(a):
    p0 = a.astype(BF16)
    r = a - p0.astype(F32)
    p1 = r.astype(BF16)
    p2 = (r - p1.astype(F32)).astype(BF16)
    return p0, p1, p2


def _dot3(a, b):
    ah = a.astype(BF16)
    al = (a - ah.astype(F32)).astype(BF16)
    bh = b.astype(BF16)
    bl = (b - bh.astype(F32)).astype(BF16)
    return (jnp.dot(ah, bh, preferred_element_type=F32) + jnp.dot(ah, bl, preferred_element_type=F32)
            + jnp.dot(al, bh, preferred_element_type=F32))


def _mask_bf16(m01):
    return jnp.where(m01, 1.0, 0.0).astype(BF16)


def _xdot(m01, a):
    m = _mask_bf16(m01)
    p0, p1, p2 = _split3(a)
    return (jnp.dot(m, p0, preferred_element_type=F32) + jnp.dot(m, p1, preferred_element_type=F32)
            + jnp.dot(m, p2, preferred_element_type=F32))


def _xdot_nt(m01, a):
    m = _mask_bf16(m01)
    dn = (((1,), (1,)), ((), ()))
    p0, p1, p2 = _split3(a)
    return (lax.dot_general(m, p0, dn, preferred_element_type=F32)
            + lax.dot_general(m, p1, dn, preferred_element_type=F32)
            + lax.dot_general(m, p2, dn, preferred_element_type=F32))


def _xdot_r(a, m01):
    m = _mask_bf16(m01)
    p0, p1, p2 = _split3(a)
    return (jnp.dot(p0, m, preferred_element_type=F32) + jnp.dot(p1, m, preferred_element_type=F32)
            + jnp.dot(p2, m, preferred_element_type=F32))


def _sigmoid(x):
    return 0.5 * jnp.tanh(0.5 * x) + 0.5


def _silu(x):
    return x * _sigmoid(x)


def _rms(x, g):
    return x * lax.rsqrt(jnp.mean(x * x, axis=-1, keepdims=True) + EPS) * g


def _cond_index(row):
    return jnp.where(row < N_CTX, 0, 1 + (row - N_CTX) // DEC_SEQ)


def _mod_spec(k, tm):
    return pl.BlockSpec((None, None, 1, D_MODEL), lambda i, *_: (_cond_index(i * tm), k, 0, 0))


def _ada_kernel(c_ref, w_ref, b_ref, o_ref):
    o_ref[...] = _bdot(_silu(c_ref[...]), w_ref[...]) + b_ref[...]


def _ada_mods(cond, ada_w, ada_b):
    tn = 1536
    out = pl.pallas_call(
        _ada_kernel, grid=(DEPTH, 6 * D_MODEL // tn),
        in_specs=[pl.BlockSpec((N_COND, D_MODEL), lambda l, j: (0, 0)),
                  pl.BlockSpec((None, D_MODEL, tn), lambda l, j: (l, 0, j)),
                  pl.BlockSpec((None, 1, tn), lambda l, j: (l, 0, j))],
        out_specs=pl.BlockSpec((None, N_COND, tn), lambda l, j: (l, 0, j)),
        out_shape=jax.ShapeDtypeStruct((DEPTH, N_COND, 6 * D_MODEL), F32),
        compiler_params=_params(("parallel", "parallel")), name="ada_mods",
    )(cond, ada_w, ada_b.reshape(DEPTH, 1, 6 * D_MODEL))
    return out.reshape(DEPTH, N_COND, 6, 1, D_MODEL)


_EV_TN = 512
_EV_W = 7 * DN_HEADS * LANES
_KV_COL0 = _KB * LANES


def _even_proj_kernel(x_ref, g_ref, sh_ref, sc_ref, w_ref, wab_ref, o_ref, ab_ref, kv_ref):
    h = (_rms(x_ref[...], g_ref[...]) * (1.0 + sc_ref[...]) + sh_ref[...]).astype(BF16)
    ab_ref[...] = jnp.dot(h, wab_ref[...], preferred_element_type=F32)
    for j in range(_EV_W // _EV_TN):
        c0 = j * _EV_TN
        y = jnp.dot(h, w_ref[:, c0:c0 + _EV_TN], preferred_element_type=F32)
        for c in range(_EV_TN // LANES):
            o_ref[c0 // LANES + c] = y[:, c * LANES:(c + 1) * LANES].astype(BF16)
        if c0 >= _KV_COL0:
            kv_ref[:, c0 - _KV_COL0:c0 - _KV_COL0 + _EV_TN] = y


def _even_proj(x, mods, g, w_in):
    tm = 512
    n_ab = 4 * DN_HEADS
    ab0 = 4 * DN_HEADS * DN_DK
    w_main = jnp.concatenate([w_in[:, :ab0], w_in[:, ab0 + n_ab:]], axis=1).astype(BF16)
    w_ab = jnp.concatenate([w_in[:, ab0:ab0 + n_ab], jnp.zeros((D_MODEL, LANES - n_ab), F32)],
                           axis=1).astype(BF16)
    held = lambda shape: pl.BlockSpec(shape, lambda i: (0,) * len(shape), pipeline_mode=pl.Buffered(1))
    return pl.pallas_call(
        _even_proj_kernel, grid=(N_TOK // tm,),
        in_specs=[pl.BlockSpec((tm, D_MODEL), lambda i: (i, 0)),
                  pl.BlockSpec((1, D_MODEL), lambda i: (0, 0)),
                  _mod_spec(0, tm), _mod_spec(1, tm),
                  held((D_MODEL, _EV_W)), held((D_MODEL, LANES))],
        out_specs=[pl.BlockSpec((_EV_W // LANES, tm, LANES), lambda i: (0, i, 0)),
                   pl.BlockSpec((tm, LANES), lambda i: (i, 0)),
                   pl.BlockSpec((tm, 2 * NA_HEADS * NA_HD), lambda i: (i, 0))],
        out_shape=[jax.ShapeDtypeStruct((_EV_W // LANES, N_TOK, LANES), BF16),
                   jax.ShapeDtypeStruct((N_TOK, LANES), F32),
                   jax.ShapeDtypeStruct((N_TOK, 2 * NA_HEADS * NA_HD), F32)],
        compiler_params=_params(("parallel",)), name="even_proj",
    )(x, g.reshape(1, D_MODEL), mods, mods, w_main, w_ab)


_CHUNK_SHIFT = DN_CHUNK.bit_length() - 1
_CUM_ROWS = 256
_DN_PREP = 8
_SERIES_FINE = 3
_MQ_ROWS = DN_DK + DN_CHUNK


def _dn_kernel(*refs, T, has_s0, want_state):
    it = iter(refs)
    q_ref, k_ref, v_ref, z_ref, ab_ref = (next(it) for _ in range(5))
    cwq_ref, cwk_ref, cwv_ref, alog_ref, dtb_ref, og_ref = (next(it) for _ in range(6))
    s0_ref = next(it) if has_s0 else None
    o_ref = next(it)
    sfin_ref = next(it) if want_state else None
    qc, kc, vc, gsc, bsc, osc, b_s, mq_s = (next(it) for _ in range(8))

    C = DN_CHUNK
    n = T // C
    hd = pl.program_id(1)

    row = lax.broadcasted_iota(jnp.int32, (T, 1), 0)

    def conv(x_ref, cw_ref):
        x = x_ref[...].astype(F32)
        xp = jnp.where(row == 0, 0.0, pltpu.roll(x, 1, 0))
        xn = jnp.where(row == T - 1, 0.0, pltpu.roll(x, T - 1, 0))
        return _silu(cw_ref[0:1, :] * xp + cw_ref[1:2, :] * x + cw_ref[2:3, :] * xn)

    def l2n(x):
        return x * lax.rsqrt(jnp.sum(x * x, axis=-1, keepdims=True) + EPS)

    qc[...] = l2n(conv(q_ref, cwq_ref)) * (DN_DK ** -0.5)
    kc[...] = l2n(conv(k_ref, cwk_ref))
    vc[...] = conv(v_ref, cwv_ref)

    ab = ab_ref[...]
    sel_r = lax.broadcasted_iota(jnp.int32, (LANES, LANES), 0)
    for d in range(2):
        alpha = _xdot_r(ab, sel_r == d * DN_HEADS + hd)
        blog = _xdot_r(ab, sel_r == 2 * DN_HEADS + d * DN_HEADS + hd)
        x = alpha + dtb_ref[d, hd]
        sp = jnp.maximum(x, 0.0) + jnp.log1p(jnp.exp(-jnp.abs(x)))
        a = jnp.exp(jnp.full((1, LANES), alog_ref[d, hd], F32))
        gsc[d] = -a * sp
        bsc[d] = _sigmoid(blog)

    pr = lax.broadcasted_iota(jnp.int32, (_CUM_ROWS, _CUM_ROWS), 0)
    pc = lax.broadcasted_iota(jnp.int32, (_CUM_ROWS, _CUM_ROWS), 1)
    same = lax.shift_right_logical(pr, _CHUNK_SHIFT) == lax.shift_right_logical(pc, _CHUNK_SHIFT)
    cum_mask = (jnp.logical_and(same, pc <= pr), jnp.logical_and(same, pc >= pr))

    def cum_body(i, carry):
        sl = pl.ds(pl.multiple_of(i * _CUM_ROWS, _CUM_ROWS), _CUM_ROWS)
        for d in range(2):
            gsc[d, sl, :] = _xdot(cum_mask[d], gsc[d, sl, :])
        return carry

    lax.fori_loop(0, T // _CUM_ROWS, cum_body, 0)

    ri = lax.broadcasted_iota(jnp.int32, (C, C), 0)
    ci = lax.broadcasted_iota(jnp.int32, (C, C), 1)
    eye = (ri == ci).astype(F32)

    def prepare(items):
        lows, decays = [], []
        for d, c in items:
            sl = pl.ds(pl.multiple_of(c * C, C), C)
            k, gc = kc[sl, :], gsc[d, sl, :]
            incl = (ci <= ri) if d == 0 else (ci >= ri)
            strict = (ci < ri) if d == 0 else (ci > ri)
            gr = jnp.transpose(gc)[0:1, :C]
            decay = jnp.where(incl, jnp.exp(jnp.where(incl, gc[:, :C] - gr, 0.0)), 0.0)
            lows.append(jnp.where(strict, _bdot_nt(k * bsc[d, sl, :], k) * decay, 0.0))
            decays.append(decay)
        ts = [eye - low for low in lows]
        ps = lows
        for step in range(5):
            dot = _dot3 if step < _SERIES_FINE else _bdot
            ps = [dot(p, p) for p in ps]
            ts = [t + dot(t, p) for t, p in zip(ts, ps)]
        for (d, c), t, decay in zip(items, ts, decays):
            sl = pl.ds(pl.multiple_of(c * C, C), C)
            q, k, gc, beta = qc[sl, :], kc[sl, :], gsc[d, sl, :], bsc[d, sl, :]
            eg = jnp.exp(gc)
            uw = _bdot(t, jnp.concatenate([vc[sl, :] * beta, k * beta * eg], axis=-1))
            last = gc[C - 1:C, :] if d == 0 else gc[0:1, :]
            wu = jnp.concatenate([uw[:, LANES:], uw[:, :LANES]], axis=-1).astype(BF16)
            kd = (k * jnp.exp(last - gc)).astype(BF16)
            attn = (_bdot_nt(q, k) * decay).astype(BF16)
            kdwu = lax.dot_general(kd, wu, (((0,), (0,)), ((), ())), preferred_element_type=F32)
            awu = jnp.dot(attn, wu, preferred_element_type=F32)
            mq0 = pl.multiple_of(c * _MQ_ROWS, _MQ_ROWS)
            mq_s[d, pl.ds(mq0, DN_DK), :] = kdwu[:, :LANES].astype(BF16)
            mq_s[d, pl.ds(mq0 + DN_DK, C), :] = (q * eg - awu[:, :LANES]).astype(BF16)
            b_s[d, pl.ds(pl.multiple_of(c * DN_DK, DN_DK), DN_DK), :] = kdwu[:, LANES:]
            osc[d, sl, :] = awu[:, LANES:]

    def prep_body(i, carry):
        prepare([(d, i * n_prep + j) for j in range(n_prep) for d in range(2)])
        return carry

    n_prep = min(n, _DN_PREP)
    lax.fori_loop(0, n // n_prep, prep_body, 0)

    def advance(d, c, S):
        sl = pl.ds(pl.multiple_of(c * C, C), C)
        ms = jnp.dot(mq_s[d, pl.ds(pl.multiple_of(c * _MQ_ROWS, _MQ_ROWS), _MQ_ROWS), :], S.astype(BF16),
                     preferred_element_type=F32)
        osc[d, sl, :] = osc[d, sl, :] + ms[DN_DK:]
        last = gsc[d, pl.ds(c * C + (C - 1 if d == 0 else 0), 1), :]
        return S * jnp.exp(last) - ms[:DN_DK] + b_s[d, pl.ds(pl.multiple_of(c * DN_DK, DN_DK), DN_DK), :]

    def body(i, carry):
        return advance(0, i, carry[0]), advance(1, n - 1 - i, carry[1])

    if has_s0:
        init = (s0_ref[0], s0_ref[1])
    else:
        init = (jnp.zeros((DN_DK, LANES), F32), jnp.zeros((DN_DK, LANES), F32))
    s_f, s_b = lax.fori_loop(0, n, body, init)
    if want_state:
        sfin_ref[0] = s_f
        sfin_ref[1] = s_b

    o = osc[0] + osc[1]
    o_ref[...] = (_rms(o, og_ref[...]) * _silu(z_ref[...].astype(F32))).astype(o_ref.dtype)


def _delta_heads(proj, ab, conv_w, a_log, dt_bias, onorm_g, T, n_seq, row0, s0):
    has_s0 = s0 is not None
    want_state = not has_s0

    def col(cb):
        return pl.BlockSpec((None, T, LANES), lambda s, h: (cb + h, row0 + s, 0))

    def cw(cb):
        return pl.BlockSpec((3, LANES), lambda s, h: (0, cb + h))

    smem = pl.BlockSpec(memory_space=pltpu.SMEM)
    in_specs = [col(_QA), col(_KA), col(_VA), col(_ZA),
                pl.BlockSpec((T, LANES), lambda s, h: (row0 + s, 0)),
                cw(0), cw(4), cw(8), smem, smem,
                pl.BlockSpec((1, LANES), lambda s, h: (0, 0))]
    args = [proj, proj, proj, proj, ab, conv_w, conv_w, conv_w, a_log, dt_bias,
            onorm_g.reshape(1, LANES)]
    if has_s0:
        in_specs.append(pl.BlockSpec((None, 2, None, DN_DK, LANES), lambda s, h: (s, 0, h, 0, 0)))
        args.append(s0)
    out_shape = [jax.ShapeDtypeStruct((DN_HEADS, n_seq * T, LANES), BF16)]
    out_specs = [pl.BlockSpec((None, T, LANES), lambda s, h: (h, s, 0))]
    if want_state:
        out_shape.append(jax.ShapeDtypeStruct((n_seq, 2, DN_HEADS, DN_DK, LANES), F32))
        out_specs.append(pl.BlockSpec((None, 2, None, DN_DK, LANES), lambda s, h: (s, 0, h, 0, 0)))
    res = pl.pallas_call(
        functools.partial(_dn_kernel, T=T, has_s0=has_s0, want_state=want_state),
        grid=(n_seq, DN_HEADS), in_specs=in_specs, out_specs=out_specs, out_shape=out_shape,
        scratch_shapes=[pltpu.VMEM((T, LANES), F32)] * 3
        + [pltpu.VMEM((2, T, LANES), F32)] * 3
        + [pltpu.VMEM((2, T // DN_CHUNK * DN_DK, LANES), F32),
           pltpu.VMEM((2, T // DN_CHUNK * _MQ_ROWS, LANES), BF16)],
        compiler_params=_params(("parallel", "parallel")), name="delta_heads_%d" % T,
    )(*args)
    return res if want_state else (res[0], None)


def _pair_queries(q, first):
    return jnp.concatenate([jnp.where(first, q, 0.0), jnp.where(first, 0.0, q)], axis=0).astype(BF16)


def _ctx_attn_kernel(q_ref, k_ref, v_ref, o_ref):
    first = lax.broadcasted_iota(jnp.int32, (SEQ, LANES), 1) < NA_HD
    qm = _pair_queries(q_ref[...] * (NA_HD ** -0.5), first)
    s = lax.dot_general(k_ref[...], qm, (((1,), (1,)), ((), ())), preferred_element_type=F32)
    e = jnp.exp(s - jnp.max(s, axis=0, keepdims=True))
    den = jnp.sum(e, axis=0, keepdims=True)
    o = lax.dot_general(e.astype(BF16), v_ref[...], (((0,), (0,)), ((), ())), preferred_element_type=F32)
    o = jnp.where(first, o[:SEQ], o[SEQ:])
    den_t = jnp.transpose(jnp.broadcast_to(den, (LANES, 2 * SEQ)))
    o_ref[...] = (o / jnp.where(first, den_t[:SEQ], den_t[SEQ:])).astype(o_ref.dtype)


def _ctx_attention(proj):
    def col(cb):
        return pl.BlockSpec((None, SEQ, LANES), lambda s, p: (cb + p, s, 0))

    return pl.pallas_call(
        _ctx_attn_kernel, grid=(BATCH, NA_HEADS // 2),
        in_specs=[col(_QB), col(_KB), col(_VB)],
        out_specs=pl.BlockSpec((None, SEQ, LANES), lambda s, p: (p, s, 0)),
        out_shape=jax.ShapeDtypeStruct((NA_HEADS // 2, N_CTX, LANES), BF16),
        compiler_params=_params(("parallel", "parallel")), name="ctx_attention",
    )(proj, proj, proj)


_NA_UNROLL = 4


def _na_kernel(q_ref, k_ref, v_ref, kc_ref, vc_ref, bias_ref, o_ref, kcb_scr, vcb_scr):
    rows = DEC_SEQ // GRID_W
    win = NA_ROWS * GRID_W
    scale = NA_HD ** -0.5
    dn_nt = (((1,), (1,)), ((), ()))
    dn_tn = (((0,), (0,)), ((), ()))

    kcb_scr[...] = kc_ref[...].astype(BF16)
    vcb_scr[...] = vc_ref[...].astype(BF16)
    first = lax.broadcasted_iota(jnp.int32, (GRID_W, LANES), 1) < NA_HD

    def body(it, carry):
        rr = [it * _NA_UNROLL + j for j in range(_NA_UNROLL)]
        rss = [jnp.clip(r - NA_ROWS // 2, 0, rows - NA_ROWS) for r in rr]
        qsls = [pl.ds(pl.multiple_of(r * GRID_W, GRID_W), GRID_W) for r in rr]
        wsls = [pl.ds(pl.multiple_of(rs * GRID_W, GRID_W), win) for rs in rss]
        qms, s_wins, s_ctxs = [], [], []
        for r, rs, qsl, wsl in zip(rr, rss, qsls, wsls):
            qm = _pair_queries(q_ref[qsl, :] * scale, first)
            bias = jnp.concatenate([bias_ref[NA_ROWS - 1 - (r - rs) + i] for i in range(NA_ROWS)], axis=0)
            s_wins.append(lax.dot_general(k_ref[wsl, :], qm, dn_nt, preferred_element_type=F32) + bias)
            s_ctxs.append(lax.dot_general(kcb_scr[...], qm, dn_nt, preferred_element_type=F32))
        ms = [jnp.maximum(jnp.max(sw, axis=0, keepdims=True), jnp.max(sc, axis=0, keepdims=True))
              for sw, sc in zip(s_wins, s_ctxs)]
        e_wins = [jnp.exp(sw - m) for sw, m in zip(s_wins, ms)]
        e_ctxs = [jnp.exp(sc - m) for sc, m in zip(s_ctxs, ms)]
        dens = [jnp.sum(ew, axis=0, keepdims=True) + jnp.sum(ec, axis=0, keepdims=True)
                for ew, ec in zip(e_wins, e_ctxs)]
        for qsl, wsl, ew, ec, den in zip(qsls, wsls, e_wins, e_ctxs, dens):
            o = (lax.dot_general(ew.astype(BF16), v_ref[wsl, :], dn_tn, preferred_element_type=F32)
                 + lax.dot_general(ec.astype(BF16), vcb_scr[...], dn_tn, preferred_element_type=F32))
            o = o / jnp.transpose(jnp.broadcast_to(den, (LANES, LANES)))
            o_ref[qsl, :] = jnp.where(first, o[:GRID_W], o[GRID_W:]).astype(o_ref.dtype)
        return carry

    lax.fori_loop(0, rows // _NA_UNROLL, body, 0)


def _na_bias_table(rpb):
    col = jnp.arange(GRID_W)
    cs = jnp.clip(col - NA_COLS // 2, 0, GRID_W - NA_COLS)
    col_ok = (col[None, :] >= cs[:, None]) & (col[None, :] < cs[:, None] + NA_COLS)
    dc = jnp.clip(col[None, :] - col[:, None] + NA_COLS - 1, 0, 2 * NA_COLS - 2)
    onehot = (dc.T[None, :, :] == jnp.arange(2 * NA_COLS - 1)[:, None, None]).astype(F32)
    t = jnp.einsum('hrd,dkq->hrkq', rpb.astype(F32), onehot, precision=lax.Precision.HIGHEST)
    t = jnp.where(col_ok.T[None, None], t, NEG_INF)
    t = t.reshape(NA_HEADS // 2, 2, 2 * NA_ROWS - 1, GRID_W, GRID_W)
    return jnp.concatenate([t[:, 0], t[:, 1]], axis=-1)


def _na_attention(proj, kctx, vctx, rpb):
    blk = N_CTX // DEC_SEQ

    def col(cb):
        return pl.BlockSpec((None, DEC_SEQ, LANES), lambda b, p: (cb + p, blk + b, 0))

    ctx = pl.BlockSpec((None, PAST_LEN, LANES), lambda b, p: (b, 0, p))
    return pl.pallas_call(
        _na_kernel, grid=(DEC_BATCH, NA_HEADS // 2),
        in_specs=[col(_QB), col(_KB), col(_VB), ctx, ctx,
                  pl.BlockSpec((None, 2 * NA_ROWS - 1, GRID_W, 2 * GRID_W), lambda b, p: (p, 0, 0, 0))],
        out_specs=pl.BlockSpec((None, DEC_SEQ, LANES), lambda b, p: (p, b, 0)),
        out_shape=jax.ShapeDtypeStruct((NA_HEADS // 2, N_LAT, LANES), BF16),
        scratch_shapes=[pltpu.VMEM((PAST_LEN, LANES), BF16), pltpu.VMEM((PAST_LEN, LANES), BF16)],
        compiler_params=_params(("parallel", "parallel")), name="na_attention",
    )(proj, proj, proj, kctx, vctx, _na_bias_table(rpb))


_LOGIT0 = N_EGROUPS
_R_E, _R_W, _R_RANK = 0, 2, 4


def _lane_min_where(mask, lane):
    return jnp.min(jnp.where(mask, lane, LANES), axis=-1, keepdims=True)


def _route_rows(lg, carry_ref, tri_ref):
    big = -3.0e38
    lane = lax.broadcasted_iota(jnp.int32, lg.shape, 1)
    is_g = lane < N_EGROUPS
    gmax = jnp.max(jnp.where(is_g, lg, big), axis=-1, keepdims=True)
    gsum = jnp.sum(jnp.where(is_g, jnp.exp(jnp.where(is_g, lg - gmax, 0.0)), 0.0), axis=-1, keepdims=True)
    pg_top = 1.0 / gsum
    g_idx = _lane_min_where(jnp.logical_and(is_g, lg == gmax), lane)
    in_g = jnp.logical_and(lane >= _LOGIT0, lax.shift_right_arithmetic(lane - _LOGIT0, 3) == g_idx)
    in_g = jnp.logical_and(in_g, lane < _LOGIT0 + N_EXPERTS)
    m1 = jnp.max(jnp.where(in_g, lg, big), axis=-1, keepdims=True)
    i1 = _lane_min_where(jnp.logical_and(in_g, lg == m1), lane)
    rest = jnp.logical_and(in_g, lane != i1)
    m2 = jnp.max(jnp.where(rest, lg, big), axis=-1, keepdims=True)
    i2 = _lane_min_where(jnp.logical_and(rest, lg == m2), lane)
    e2 = jnp.exp(m2 - m1)
    w1 = pg_top * (1.0 / (1.0 + e2))
    w2 = pg_top * (e2 / (1.0 + e2))
    hit1 = lane == i1
    hit2 = lane == i2
    picked = jnp.where(jnp.logical_or(hit1, hit2), 1.0, 0.0)
    before = jnp.dot(tri_ref[...], picked.astype(BF16), preferred_element_type=F32) + carry_ref[...]
    r1 = jnp.sum(jnp.where(hit1, before, 0.0), axis=-1, keepdims=True)
    r2 = jnp.sum(jnp.where(hit2, before, 0.0), axis=-1, keepdims=True)
    carry_ref[...] = carry_ref[...] + jnp.sum(picked, axis=0, keepdims=True)
    rec = jnp.zeros(lg.shape, F32)
    for ln, val in ((_R_E, (i1 - _LOGIT0).astype(F32)), (_R_E + 1, (i2 - _LOGIT0).astype(F32)),
                    (_R_W, w1), (_R_W + 1, w2), (_R_RANK, r1), (_R_RANK + 1, r2)):
        rec = jnp.where(lane == ln, val, rec)
    return rec


def _moe_input(xnew, first, tail_in, tail_out, tail_scr):
    g2_ref, sc2_ref, sh2_ref, wrh_ref, wrl_ref, br_ref = tail_in
    x_out, h_out, rec_out, cnt_out = tail_out
    tri_scr, carry_scr = tail_scr

    @pl.when(first)
    def _():
        tm = tri_scr.shape[0]
        r = lax.broadcasted_iota(jnp.int32, (tm, tm), 0)
        c = lax.broadcasted_iota(jnp.int32, (tm, tm), 1)
        tri_scr[...] = jnp.where(c < r, 1.0, 0.0).astype(BF16)
        carry_scr[...] = jnp.zeros(carry_scr.shape, F32)

    x_out[...] = xnew
    h = _rms(xnew, g2_ref[...]) * (1.0 + sc2_ref[...]) + sh2_ref[...]
    hh = h.astype(BF16)
    hl = (h - hh.astype(F32)).astype(BF16)
    h_out[...] = hh
    lg = (jnp.dot(hh, wrh_ref[...], preferred_element_type=F32)
          + jnp.dot(hh, wrl_ref[...], preferred_element_type=F32)
          + jnp.dot(hl, wrh_ref[...], preferred_element_type=F32) + br_ref[...])
    rec_out[...] = _route_rows(lg, carry_scr, tri_scr)
    cnt_out[...] = carry_scr[...]


def _even_out_kernel(oac_ref, obc_ref, oal_ref, obl_ref, x_ref, w_ref, gate_ref, *rest, ctx_tiles):
    tail_in, tail_out, (w_scr,), tail_scr = rest[:6], rest[6:10], rest[10:11], rest[11:]
    first = pl.program_id(0) == 0

    @pl.when(first)
    def _():
        w_scr[...] = w_ref[...].astype(BF16)

    is_ctx = pl.program_id(0) < ctx_tiles
    parts = [jnp.where(is_ctx, c_ref[hb], l_ref[hb])
             for c_ref, l_ref in ((oac_ref, oal_ref), (obc_ref, obl_ref)) for hb in range(DN_HEADS)]
    mix = jnp.concatenate(parts, axis=-1)
    out = jnp.dot(mix, w_scr[...], preferred_element_type=F32)
    _moe_input(x_ref[...] + gate_ref[...] * out, first, tail_in, tail_out, tail_scr)


def _tail_specs(tm):
    const = lambda shape: pl.BlockSpec(shape, lambda i: (0,) * len(shape))
    in_specs = [_mod_spec(2, tm), const((1, D_MODEL)), _mod_spec(4, tm), _mod_spec(3, tm),
                const((D_MODEL, LANES)), const((D_MODEL, LANES)), const((1, LANES))]
    out_specs = [pl.BlockSpec((tm, D_MODEL), lambda i: (i, 0)),
                 pl.BlockSpec((tm, D_MODEL), lambda i: (i, 0)),
                 pl.BlockSpec((tm, LANES), lambda i: (i, 0)),
                 const((1, LANES))]
    out_shape = [jax.ShapeDtypeStruct((N_TOK, D_MODEL), F32),
                 jax.ShapeDtypeStruct((N_TOK, D_MODEL), BF16),
                 jax.ShapeDtypeStruct((N_TOK, LANES), F32),
                 jax.ShapeDtypeStruct((1, LANES), F32)]
    scratch = [pltpu.VMEM((tm, tm), BF16), pltpu.VMEM((1, LANES), F32)]
    return in_specs, out_specs, out_shape, scratch


def _router_weights(w_rg, b_rg, w_re, b_re):
    pad = LANES - N_EGROUPS - N_EXPERTS
    w = jnp.concatenate([w_rg, w_re, jnp.zeros((D_MODEL, pad), F32)], axis=1)
    b = jnp.concatenate([b_rg, b_re, jnp.zeros((pad,), F32)]).reshape(1, LANES)
    hi = w.astype(BF16)
    lo = (w - hi.astype(F32)).astype(BF16)
    return hi, lo, b


def _even_out(oa_ctx, ob_ctx, oa_lat, ob_lat, x, w_out, mods, g2, router):
    tm = 512
    ctx_tiles = N_CTX // tm
    tail_in, out_specs, out_shape, tail_scr = _tail_specs(tm)
    ctxblk = pl.BlockSpec((DN_HEADS, tm, LANES), lambda i: (0, jnp.minimum(i, ctx_tiles - 1), 0))
    latblk = pl.BlockSpec((DN_HEADS, tm, LANES), lambda i: (0, jnp.maximum(i - ctx_tiles, 0), 0))
    return pl.pallas_call(
        functools.partial(_even_out_kernel, ctx_tiles=ctx_tiles), grid=(N_TOK // tm,),
        in_specs=[ctxblk, ctxblk, latblk, latblk, pl.BlockSpec((tm, D_MODEL), lambda i: (i, 0)),
                  pl.BlockSpec((D_MODEL, D_MODEL), lambda i: (0, 0))] + tail_in,
        out_specs=out_specs, out_shape=out_shape,
        scratch_shapes=[pltpu.VMEM((D_MODEL, D_MODEL), BF16)] + tail_scr,
        compiler_params=_params(("arbitrary",)), name="even_out",
    )(oa_ctx, ob_ctx, oa_lat, ob_lat, x, w_out, mods, g2.reshape(1, D_MODEL), mods, mods, *router)


def _gelu_tanh(x):
    return x * (0.5 * (1.0 + jnp.tanh(0.7978845608028654 * (x + 0.044715 * (x * x * x)))))


def _sgu_kernel(x_ref, g1_ref, sh1_ref, sc1_ref, win_ref, lng_ref, lnb_ref, ws_ref, bst_ref, wout_ref, gate_ref,
                *rest, tm):
    tail_in, tail_out, (v_scr, m_scr), tail_scr = rest[:6], rest[6:10], rest[10:12], rest[12:]
    first = pl.program_id(0) == 0
    x = x_ref[...]
    h = (_rms(x, g1_ref[...]) * (1.0 + sc1_ref[...]) + sh1_ref[...]).astype(BF16)

    v = _gelu_tanh(jnp.dot(h, win_ref[:, SG_W:], preferred_element_type=F32))
    mu = jnp.mean(v, axis=-1, keepdims=True)
    vc = v - mu
    var = jnp.mean(vc * vc, axis=-1, keepdims=True)
    v_scr[...] = (vc * lax.rsqrt(var + EPS) * lng_ref[...] + lnb_ref[...]).astype(BF16)

    for g in range(SG_GROUPS):
        cs = slice(g * SG_GW, (g + 1) * SG_GW)
        u = _gelu_tanh(jnp.dot(h, win_ref[:, cs], preferred_element_type=F32))
        w_sp = ws_ref[g].astype(BF16)
        for c in range(tm // SG_CHUNK):
            rs = slice(c * SG_CHUNK, (c + 1) * SG_CHUNK)
            sp = jnp.dot(w_sp, v_scr[rs, cs], preferred_element_type=F32) + bst_ref[:, g:g + 1]
            m_scr[rs, cs] = (u[rs] * sp).astype(BF16)
    out = jnp.dot(m_scr[...], wout_ref[...], preferred_element_type=F32)
    _moe_input(x + gate_ref[...] * out, first, tail_in, tail_out, tail_scr)


def _sgu_layer(x, mods, g1, w_in, ln_g, ln_b, w_s, b_s, w_out, g2, router):
    tm = 512
    tail_in, out_specs, out_shape, tail_scr = _tail_specs(tm)
    const = lambda shape: pl.BlockSpec(shape, lambda i: (0,) * len(shape))
    held = lambda shape: pl.BlockSpec(shape, lambda i: (0,) * len(shape), pipeline_mode=pl.Buffered(1))
    return pl.pallas_call(
        functools.partial(_sgu_kernel, tm=tm), grid=(N_TOK // tm,),
        in_specs=[pl.BlockSpec((tm, D_MODEL), lambda i: (i, 0)),
                  const((1, D_MODEL)), _mod_spec(0, tm), _mod_spec(1, tm),
                  held((D_MODEL, 2 * SG_W)), const((1, SG_W)), const((1, SG_W)),
                  const((SG_GROUPS, SG_CHUNK, SG_CHUNK)), const((SG_CHUNK, SG_GROUPS)),
                  held((SG_W, D_MODEL))] + tail_in,
        out_specs=out_specs, out_shape=out_shape,
        scratch_shapes=[pltpu.VMEM((tm, SG_W), BF16), pltpu.VMEM((tm, SG_W), BF16)] + tail_scr,
        compiler_params=_params(("arbitrary",)), name="sgu_layer",
    )(x, g1.reshape(1, D_MODEL), mods, mods, w_in.astype(BF16), ln_g.reshape(1, SG_W), ln_b.reshape(1, SG_W),
      w_s, b_s.T, w_out.astype(BF16), mods, g2.reshape(1, D_MODEL), mods, mods, *router)


def _plan(rec, cnt):
    e_idx = rec[:, _R_E:_R_E + 2].astype(jnp.int32)
    rank = rec[:, _R_RANK:_R_RANK + 2].astype(jnp.int32)
    counts = cnt[0, _LOGIT0:_LOGIT0 + N_EXPERTS].astype(jnp.int32)
    padded = (counts + MOE_BLK - 1) // MOE_BLK * MOE_BLK
    pad_end = jnp.cumsum(padded)
    pad_start = pad_end - padded
    hit = e_idx[:, :, None] == jnp.arange(N_EXPERTS, dtype=jnp.int32)[None, None, :]
    dest = jnp.sum(jnp.where(hit, pad_start[None, None, :], 0), axis=-1) + rank
    blk0 = jnp.arange(MOE_NBLK, dtype=jnp.int32) * MOE_BLK
    blk_e = jnp.minimum(jnp.sum((pad_end[None, :] <= blk0[:, None]).astype(jnp.int32), axis=-1),
                        N_EXPERTS - 1)
    n_used = (pad_end[-1] // MOE_BLK).astype(jnp.int32).reshape(1)
    return dest, blk_e, n_used


def _expert_kernel(blk_e_ref, n_used_ref, x_ref, wg_ref, wu_ref, wd_ref, o_ref, wg_scr, wu_scr, wd_scr):
    j = pl.program_id(0)
    prev = blk_e_ref[jnp.maximum(j - 1, 0)]
    fresh = jnp.logical_or(j == 0, blk_e_ref[j] != prev)
    live = j < n_used_ref[0]

    @pl.when(jnp.logical_and(fresh, live))
    def _():
        wg_scr[...] = wg_ref[...].astype(BF16)
        wu_scr[...] = wu_ref[...].astype(BF16)
        wd_scr[...] = wd_ref[...].astype(BF16)

    @pl.when(live)
    def _():
        x = x_ref[...]
        gt = jnp.dot(x, wg_scr[...], preferred_element_type=F32)
        up = jnp.dot(x, wu_scr[...], preferred_element_type=F32)
        hb = (_silu(gt) * up).astype(BF16)
        o_ref[...] = jnp.dot(hb, wd_scr[...], preferred_element_type=F32)


def _experts(x_pad, blk_e, n_used, w_gate, w_up, w_down, layer):
    grid_spec = pltpu.PrefetchScalarGridSpec(
        num_scalar_prefetch=2, grid=(MOE_NBLK,),
        in_specs=[pl.BlockSpec((MOE_BLK, D_MODEL), lambda j, be, nu: (j, 0)),
                  pl.BlockSpec((None, None, D_MODEL, D_EXPERT), lambda j, be, nu: (layer, be[j], 0, 0)),
                  pl.BlockSpec((None, None, D_MODEL, D_EXPERT), lambda j, be, nu: (layer, be[j], 0, 0)),
                  pl.BlockSpec((None, None, D_EXPERT, D_MODEL), lambda j, be, nu: (layer, be[j], 0, 0))],
        out_specs=pl.BlockSpec((MOE_BLK, D_MODEL), lambda j, be, nu: (j, 0)),
        scratch_shapes=[pltpu.VMEM((D_MODEL, D_EXPERT), BF16), pltpu.VMEM((D_MODEL, D_EXPERT), BF16),
                        pltpu.VMEM((D_EXPERT, D_MODEL), BF16)])
    return pl.pallas_call(
        _expert_kernel, grid_spec=grid_spec,
        out_shape=jax.ShapeDtypeStruct((MOE_NBLK * MOE_BLK, D_MODEL), F32),
        compiler_params=_params(("arbitrary",)), name="experts",
    )(blk_e, n_used, x_pad, w_gate, w_up, w_down)


def _combine_kernel(x_ref, ya_ref, yb_ref, rec_ref, gate_ref, fg_ref, o_ref, *, final):
    rec = rec_ref[...]
    y = rec[:, _R_W:_R_W + 1] * ya_ref[...] + rec[:, _R_W + 1:_R_W + 2] * yb_ref[...]
    xn = x_ref[...] + gate_ref[...] * y
    o_ref[...] = _rms(xn, fg_ref[...]) if final else xn


def _combine(x, ya, yb, rec, mods, final_g, final):
    tm = 512
    blk = pl.BlockSpec((tm, D_MODEL), lambda i: (i, 0))
    return pl.pallas_call(
        functools.partial(_combine_kernel, final=final), grid=(N_TOK // tm,),
        in_specs=[blk, blk, blk, pl.BlockSpec((tm, LANES), lambda i: (i, 0)), _mod_spec(5, tm),
                  pl.BlockSpec((1, D_MODEL), lambda i: (0, 0))],
        out_specs=blk, out_shape=jax.ShapeDtypeStruct((N_TOK, D_MODEL), F32),
        compiler_params=_params(("parallel",)), name="moe_combine",
    )(x, ya, yb, rec, mods, final_g.reshape(1, D_MODEL))


def _moe(x, h, rec, cnt, mods, w_gate, w_up, w_down, layer, final_g, final):
    dest, blk_e, n_used = _plan(rec, cnt)
    tok = jnp.arange(2 * N_TOK, dtype=jnp.int32) // 2
    row_tok = jnp.zeros((MOE_NBLK * MOE_BLK,), jnp.int32).at[dest.reshape(-1)].set(tok, unique_indices=True)
    y_pad = _experts(h[row_tok], blk_e, n_used, w_gate, w_up, w_down, layer)
    return _combine(x, y_pad[dest[:, 0]], y_pad[dest[:, 1]], rec, mods, final_g, final)


def kernel(x_prompt, x_sample, c, cache_k, cache_v, state_delta, c_ctx, ada_w, ada_b, norm1_g, norm2_g, final_g,
           ev_w_in, ev_w_out, ev_conv_w, ev_a_log, ev_dt_bias, ev_onorm_g, ev_rpb, od_w_in, od_ln_g, od_ln_b,
           od_w_s, od_b_s, od_w_out, moe_w_rg, moe_b_rg, moe_w_re, moe_b_re, moe_w_gate, moe_w_up, moe_w_down):
    x = jnp.concatenate([x_prompt.reshape(N_CTX, D_MODEL), x_sample.reshape(N_LAT, D_MODEL)], axis=0)
    cond = jnp.concatenate([c_ctx[None, :], c, jnp.zeros((N_COND - 1 - DEC_BATCH, D_MODEL), F32)], axis=0)
    mods_all = _ada_mods(cond, ada_w, ada_b)
    kctx_all = cache_k.reshape(DEC_BATCH, -1, PAST_LEN, NA_HEADS * NA_HD)
    vctx_all = cache_v.reshape(DEC_BATCH, -1, PAST_LEN, NA_HEADS * NA_HD)

    ks, vs, ss = [], [], []
    for l in range(DEPTH):
        mods = mods_all[l]
        router = _router_weights(moe_w_rg[l], moe_b_rg[l], moe_w_re[l], moe_b_re[l])
        if l % 2 == 0:
            e = l // 2
            proj, ab, kv = _even_proj(x, mods, norm1_g[l], ev_w_in[e])
            dn = (proj, ab, ev_conv_w[e], ev_a_log[e], ev_dt_bias[e], ev_onorm_g[e])
            oa_ctx, s_fin = _delta_heads(*dn, SEQ, BATCH, 0, None)
            oa_lat, _ = _delta_heads(*dn, DEC_SEQ, DEC_BATCH, N_CTX // DEC_SEQ, state_delta[:, e])
            ob_ctx = _ctx_attention(proj)
            ob_lat = _na_attention(proj, kctx_all[:, e], vctx_all[:, e], ev_rpb[e])
            x, h, rec, cnt = _even_out(oa_ctx, ob_ctx, oa_lat, ob_lat, x, ev_w_out[e], mods, norm2_g[l],
                                       router)
            na_w = NA_HEADS * NA_HD
            ks.append(kv[:N_CTX, :na_w].reshape(BATCH, SEQ, NA_HEADS, NA_HD))
            vs.append(kv[:N_CTX, na_w:].reshape(BATCH, SEQ, NA_HEADS, NA_HD))
            ss.append(s_fin)
        else:
            o = l // 2
            x, h, rec, cnt = _sgu_layer(x, mods, norm1_g[l], od_w_in[o], od_ln_g[o], od_ln_b[o], od_w_s[o],
                                        od_b_s[o], od_w_out[o], norm2_g[l], router)
        x = _moe(x, h, rec, cnt, mods, moe_w_gate, moe_w_up, moe_w_down, l, final_g, l == DEPTH - 1)

    y_prompt = x[:N_CTX].reshape(BATCH, SEQ, D_MODEL)
    y_sample = x[N_CTX:].reshape(DEC_BATCH, DEC_SEQ, D_MODEL)
    return (y_prompt, y_sample, jnp.stack(ks, axis=1), jnp.stack(vs, axis=1), jnp.stack(ss, axis=1))
```

```python
import functools

import jax
import jax.numpy as jnp
from jax import lax
from jax.experimental import pallas as pl
from jax.experimental.pallas import tpu as pltpu

F32 = jnp.float32
BF16 = jnp.bfloat16

D_MODEL = 1024
BATCH = 16
SEQ = 256
DEPTH = 4
DEC_BATCH = 4
DEC_SEQ = 2048
PAST_LEN = 512
GRID_W = 64
EPS = 1e-6
NEG_INF = -1e30

DN_HEADS = 4
DN_DK = 128
DN_CHUNK = 64
NA_HEADS = 8
NA_HD = 64
NA_ROWS = 8
NA_COLS = 16
SG_CHUNK = 128
SG_GROUPS = 8
SG_W = 2 * D_MODEL
SG_GW = SG_W // SG_GROUPS
N_EGROUPS = 4
EXP_PER_GROUP = 8
N_EXPERTS = 32
D_EXPERT = 512

N_CTX = BATCH * SEQ
N_LAT = DEC_BATCH * DEC_SEQ
N_TOK = N_CTX + N_LAT
N_COND = 8
PROJ_W = 4096
LANES = 128
MOE_BLK = 256
MOE_NBLK = -(-(2 * N_TOK + N_EXPERTS * (MOE_BLK - 1)) // MOE_BLK)
VMEM_LIMIT = 56 * 1024 * 1024

_QA, _KA, _VA, _ZA, _QB, _KB, _VB = 0, 4, 8, 12, 16, 20, 24


def _params(sem):
    return pltpu.CompilerParams(dimension_semantics=sem, vmem_limit_bytes=VMEM_LIMIT)


def _bdot(a, b):
    return jnp.dot(a.astype(BF16), b.astype(BF16), preferred_element_type=F32)


def _bdot_nt(a, b):
    return lax.dot_general(a.astype(BF16), b.astype(BF16), (((1,), (1,)), ((), ())),
                           preferred_element_type=F32)


def _bdot_tn(a, b):
    return lax.dot_general(a.astype(BF16), b.astype(BF16), (((0,), (0,)), ((), ())),
                           preferred_element_type=F32)


def _split3(a):
    p0 = a.astype(BF16)
    r = a - p0.astype(F32)
    p1 = r.astype(BF16)
    p2 = (r - p1.astype(F32)).astype(BF16)
    return p0, p1, p2


def _dot3(a, b):
    ah = a.astype(BF16)
    al = (a - ah.astype(F32)).astype(BF16)
    bh = b.astype(BF16)
    bl = (b - bh.astype(F32)).astype(BF16)
    return (jnp.dot(ah, bh, preferred_element_type=F32) + jnp.dot(ah, bl, preferred_element_type=F32)
            + jnp.dot(al, bh, preferred_element_type=F32))


def _mask_bf16(m01):
    return jnp.where(m01, 1.0, 0.0).astype(BF16)


def _xdot(m01, a):
    m = _mask_bf16(m01)
    p0, p1, p2 = _split3(a)
    return (jnp.dot(m, p0, preferred_element_type=F32) + jnp.dot(m, p1, preferred_element_type=F32)
            + jnp.dot(m, p2, preferred_element_type=F32))


def _xdot_nt(m01, a):
    m = _mask_bf16(m01)
    dn = (((1,), (1,)), ((), ()))
    p0, p1, p2 = _split3(a)
    return (lax.dot_general(m, p0, dn, preferred_element_type=F32)
            + lax.dot_general(m, p1, dn, preferred_element_type=F32)
            + lax.dot_general(m, p2, dn, preferred_element_type=F32))


def _xdot_r(a, m01):
    m = _mask_bf16(m01)
    p0, p1, p2 = _split3(a)
    return (jnp.dot(p0, m, preferred_element_type=F32) + jnp.dot(p1, m, preferred_element_type=F32)
            + jnp.dot(p2, m, preferred_element_type=F32))


def _sigmoid(x):
    return 0.5 * jnp.tanh(0.5 * x) + 0.5


def _silu(x):
    return x * _sigmoid(x)


def _rms(x, g):
    return x * lax.rsqrt(jnp.mean(x * x, axis=-1, keepdims=True) + EPS) * g


def _cond_index(row):
    return jnp.where(row < N_CTX, 0, 1 + (row - N_CTX) // DEC_SEQ)


def _mod_spec(k, tm):
    return pl.BlockSpec((None, None, 1, D_MODEL), lambda i, *_: (_cond_index(i * tm), k, 0, 0))


def _ada_kernel(c_ref, w_ref, b_ref, o_ref):
    o_ref[...] = _bdot(_silu(c_ref[...]), w_ref[...]) + b_ref[...]


def _ada_mods(cond, ada_w, ada_b):
    tn = 1536
    out = pl.pallas_call(
        _ada_kernel, grid=(DEPTH, 6 * D_MODEL // tn),
        in_specs=[pl.BlockSpec((N_COND, D_MODEL), lambda l, j: (0, 0)),
                  pl.BlockSpec((None, D_MODEL, tn), lambda l, j: (l, 0, j)),
                  pl.BlockSpec((None, 1, tn), lambda l, j: (l, 0, j))],
        out_specs=pl.BlockSpec((None, N_COND, tn), lambda l, j: (l, 0, j)),
        out_shape=jax.ShapeDtypeStruct((DEPTH, N_COND, 6 * D_MODEL), F32),
        compiler_params=_params(("parallel", "parallel")), name="ada_mods",
    )(cond, ada_w, ada_b.reshape(DEPTH, 1, 6 * D_MODEL))
    return out.reshape(DEPTH, N_COND, 6, 1, D_MODEL)


_EV_TN = 512
_EV_W = 7 * DN_HEADS * LANES
_KV_COL0 = _KB * LANES


def _even_proj_kernel(x_ref, g_ref, sh_ref, sc_ref, w_ref, wab_ref, o_ref, ab_ref, kv_ref):
    h = (_rms(x_ref[...], g_ref[...]) * (1.0 + sc_ref[...]) + sh_ref[...]).astype(BF16)
    ab_ref[...] = jnp.dot(h, wab_ref[...], preferred_element_type=F32)
    for j in range(_EV_W // _EV_TN):
        c0 = j * _EV_TN
        y = jnp.dot(h, w_ref[:, c0:c0 + _EV_TN], preferred_element_type=F32)
        for c in range(_EV_TN // LANES):
            o_ref[c0 // LANES + c] = y[:, c * LANES:(c + 1) * LANES].astype(BF16)
        if c0 >= _KV_COL0:
            kv_ref[:, c0 - _KV_COL0:c0 - _KV_COL0 + _EV_TN] = y


def _even_proj(x, mods, g, w_in):
    tm = 512
    n_ab = 4 * DN_HEADS
    ab0 = 4 * DN_HEADS * DN_DK
    w_main = jnp.concatenate([w_in[:, :ab0], w_in[:, ab0 + n_ab:]], axis=1).astype(BF16)
    w_ab = jnp.concatenate([w_in[:, ab0:ab0 + n_ab], jnp.zeros((D_MODEL, LANES - n_ab), F32)],
                           axis=1).astype(BF16)
    held = lambda shape: pl.BlockSpec(shape, lambda i: (0,) * len(shape), pipeline_mode=pl.Buffered(1))
    return pl.pallas_call(
        _even_proj_kernel, grid=(N_TOK // tm,),
        in_specs=[pl.BlockSpec((tm, D_MODEL), lambda i: (i, 0)),
                  pl.BlockSpec((1, D_MODEL), lambda i: (0, 0)),
                  _mod_spec(0, tm), _mod_spec(1, tm),
                  held((D_MODEL, _EV_W)), held((D_MODEL, LANES))],
        out_specs=[pl.BlockSpec((_EV_W // LANES, tm, LANES), lambda i: (0, i, 0)),
                   pl.BlockSpec((tm, LANES), lambda i: (i, 0)),
                   pl.BlockSpec((tm, 2 * NA_HEADS * NA_HD), lambda i: (i, 0))],
        out_shape=[jax.ShapeDtypeStruct((_EV_W // LANES, N_TOK, LANES), BF16),
                   jax.ShapeDtypeStruct((N_TOK, LANES), F32),
                   jax.ShapeDtypeStruct((N_TOK, 2 * NA_HEADS * NA_HD), F32)],
        compiler_params=_params(("parallel",)), name="even_proj",
    )(x, g.reshape(1, D_MODEL), mods, mods, w_main, w_ab)


_CHUNK_SHIFT = DN_CHUNK.bit_length() - 1
_CUM_ROWS = 256
_DN_PREP = 8
_SERIES_FINE = 3
_MQ_ROWS = DN_DK + DN_CHUNK


def _dn_kernel(*refs, T, has_s0, want_state):
    it = iter(refs)
    q_ref, k_ref, v_ref, z_ref, ab_ref = (next(it) for _ in range(5))
    cwq_ref, cwk_ref, cwv_ref, alog_ref, dtb_ref, og_ref = (next(it) for _ in range(6))
    s0_ref = next(it) if has_s0 else None
    o_ref = next(it)
    sfin_ref = next(it) if want_state else None
    qc, kc, vc, gsc, bsc, osc, b_s, mq_s = (next(it) for _ in range(8))

    C = DN_CHUNK
    n = T // C
    hd = pl.program_id(1)

    row = lax.broadcasted_iota(jnp.int32, (T, 1), 0)

    def conv(x_ref, cw_ref):
        x = x_ref[...].astype(F32)
        xp = jnp.where(row == 0, 0.0, pltpu.roll(x, 1, 0))
        xn = jnp.where(row == T - 1, 0.0, pltpu.roll(x, T - 1, 0))
        return _silu(cw_ref[0:1, :] * xp + cw_ref[1:2, :] * x + cw_ref[2:3, :] * xn)

    def l2n(x):
        return x * lax.rsqrt(jnp.sum(x * x, axis=-1, keepdims=True) + EPS)

    qc[...] = l2n(conv(q_ref, cwq_ref)) * (DN_DK ** -0.5)
    kc[...] = l2n(conv(k_ref, cwk_ref))
    vc[...] = conv(v_ref, cwv_ref)

    ab = ab_ref[...]
    sel_r = lax.broadcasted_iota(jnp.int32, (LANES, LANES), 0)
    for d in range(2):
        alpha = _xdot_r(ab, sel_r == d * DN_HEADS + hd)
        blog = _xdot_r(ab, sel_r == 2 * DN_HEADS + d * DN_HEADS + hd)
        x = alpha + dtb_ref[d, hd]
        sp = jnp.maximum(x, 0.0) + jnp.log1p(jnp.exp(-jnp.abs(x)))
        a = jnp.exp(jnp.full((1, LANES), alog_ref[d, hd], F32))
        gsc[d] = -a * sp
        bsc[d] = _sigmoid(blog)

    pr = lax.broadcasted_iota(jnp.int32, (_CUM_ROWS, _CUM_ROWS), 0)
    pc = lax.broadcasted_iota(jnp.int32, (_CUM_ROWS, _CUM_ROWS), 1)
    same = lax.shift_right_logical(pr, _CHUNK_SHIFT) == lax.shift_right_logical(pc, _CHUNK_SHIFT)
    cum_mask = (jnp.logical_and(same, pc <= pr), jnp.logical_and(same, pc >= pr))

    def cum_body(i, carry):
        sl = pl.ds(pl.multiple_of(i * _CUM_ROWS, _CUM_ROWS), _CUM_ROWS)
        for d in range(2):
            gsc[d, sl, :] = _xdot(cum_mask[d], gsc[d, sl, :])
        return carry

    lax.fori_loop(0, T // _CUM_ROWS, cum_body, 0)

    ri = lax.broadcasted_iota(jnp.int32, (C, C), 0)
    ci = lax.broadcasted_iota(jnp.int32, (C, C), 1)
    eye = (ri == ci).astype(F32)

    def prepare(items):
        lows, decays = [], []
        for d, c in items:
            sl = pl.ds(pl.multiple_of(c * C, C), C)
            k, gc = kc[sl, :], gsc[d, sl, :]
            incl = (ci <= ri) if d == 0 else (ci >= ri)
            strict = (ci < ri) if d == 0 else (ci > ri)
            gr = jnp.transpose(gc)[0:1, :C]
            decay = jnp.where(incl, jnp.exp(jnp.where(incl, gc[:, :C] - gr, 0.0)), 0.0)
            lows.append(jnp.where(strict, _bdot_nt(k * bsc[d, sl, :], k) * decay, 0.0))
            decays.append(decay)
        ts = [eye - low for low in lows]
        ps = lows
        for step in range(5):
            dot = _dot3 if step < _SERIES_FINE else _bdot
            ps = [dot(p, p) for p in ps]
            ts = [t + dot(t, p) for t, p in zip(ts, ps)]
        for (d, c), t, decay in zip(items, ts, decays):
            sl = pl.ds(pl.multiple_of(c * C, C), C)
            q, k, gc, beta = qc[sl, :], kc[sl, :], gsc[d, sl, :], bsc[d, sl, :]
            eg = jnp.exp(gc)
            uw = _bdot(t, jnp.concatenate([vc[sl, :] * beta, k * beta * eg], axis=-1))
            last = gc[C - 1:C, :] if d == 0 else gc[0:1, :]
            wu = jnp.concatenate([uw[:, LANES:], uw[:, :LANES]], axis=-1).astype(BF16)
            kd = (k * jnp.exp(last - gc)).astype(BF16)
            attn = (_bdot_nt(q, k) * decay).astype(BF16)
            kdwu = lax.dot_general(kd, wu, (((0,), (0,)), ((), ())), preferred_element_type=F32)
            awu = jnp.dot(attn, wu, preferred_element_type=F32)
            mq0 = pl.multiple_of(c * _MQ_ROWS, _MQ_ROWS)
            mq_s[d, pl.ds(mq0, DN_DK), :] = kdwu[:, :LANES].astype(BF16)
            mq_s[d, pl.ds(mq0 + DN_DK, C), :] = (q * eg - awu[:, :LANES]).astype(BF16)
            b_s[d, pl.ds(pl.multiple_of(c * DN_DK, DN_DK), DN_DK), :] = kdwu[:, LANES:]
            osc[d, sl, :] = awu[:, LANES:]

    def prep_body(i, carry):
        prepare([(d, i * n_prep + j) for j in range(n_prep) for d in range(2)])
        return carry

    n_prep = min(n, _DN_PREP)
    lax.fori_loop(0, n // n_prep, prep_body, 0)

    def advance(d, c, S):
        sl = pl.ds(pl.multiple_of(c * C, C), C)
        ms = jnp.dot(mq_s[d, pl.ds(pl.multiple_of(c * _MQ_ROWS, _MQ_ROWS), _MQ_ROWS), :], S.astype(BF16),
                     preferred_element_type=F32)
        osc[d, sl, :] = osc[d, sl, :] + ms[DN_DK:]
        last = gsc[d, pl.ds(c * C + (C - 1 if d == 0 else 0), 1), :]
        return S * jnp.exp(last) - ms[:DN_DK] + b_s[d, pl.ds(pl.multiple_of(c * DN_DK, DN_DK), DN_DK), :]

    def body(i, carry):
        return advance(0, i, carry[0]), advance(1, n - 1 - i, carry[1])

    if has_s0:
        init = (s0_ref[0], s0_ref[1])
    else:
        init = (jnp.zeros((DN_DK, LANES), F32), jnp.zeros((DN_DK, LANES), F32))
    s_f, s_b = lax.fori_loop(0, n, body, init)
    if want_state:
        sfin_ref[0] = s_f
        sfin_ref[1] = s_b

    o = osc[0] + osc[1]
    o_ref[...] = (_rms(o, og_ref[...]) * _silu(z_ref[...].astype(F32))).astype(o_ref.dtype)


def _delta_heads(proj, ab, conv_w, a_log, dt_bias, onorm_g, T, n_seq, row0, s0):
    has_s0 = s0 is not None
    want_state = not has_s0

    def col(cb):
        return pl.BlockSpec((None, T, LANES), lambda s, h: (cb + h, row0 + s, 0))

    def cw(cb):
        return pl.BlockSpec((3, LANES), lambda s, h: (0, cb + h))

    smem = pl.BlockSpec(memory_space=pltpu.SMEM)
    in_specs = [col(_QA), col(_KA), col(_VA), col(_ZA),
                pl.BlockSpec((T, LANES), lambda s, h: (row0 + s, 0)),
                cw(0), cw(4), cw(8), smem, smem,
                pl.BlockSpec((1, LANES), lambda s, h: (0, 0))]
    args = [proj, proj, proj, proj, ab, conv_w, conv_w, conv_w, a_log, dt_bias,
            onorm_g.reshape(1, LANES)]
    if has_s0:
        in_specs.append(pl.BlockSpec((None, 2, None, DN_DK, LANES), lambda s, h: (s, 0, h, 0, 0)))
        args.append(s0)
    out_shape = [jax.ShapeDtypeStruct((DN_HEADS, n_seq * T, LANES), BF16)]
    out_specs = [pl.BlockSpec((None, T, LANES), lambda s, h: (h, s, 0))]
    if want_state:
        out_shape.append(jax.ShapeDtypeStruct((n_seq, 2, DN_HEADS, DN_DK, LANES), F32))
        out_specs.append(pl.BlockSpec((None, 2, None, DN_DK, LANES), lambda s, h: (s, 0, h, 0, 0)))
    res = pl.pallas_call(
        functools.partial(_dn_kernel, T=T, has_s0=has_s0, want_state=want_state),
        grid=(n_seq, DN_HEADS), in_specs=in_specs, out_specs=out_specs, out_shape=out_shape,
        scratch_shapes=[pltpu.VMEM((T, LANES), F32)] * 3
        + [pltpu.VMEM((2, T, LANES), F32)] * 3
        + [pltpu.VMEM((2, T // DN_CHUNK * DN_DK, LANES), F32),
           pltpu.VMEM((2, T // DN_CHUNK * _MQ_ROWS, LANES), BF16)],
        compiler_params=_params(("parallel", "parallel")), name="delta_heads_%d" % T,
    )(*args)
    return res if want_state else (res[0], None)


def _pair_queries(q, first):
    return jnp.concatenate([jnp.where(first, q, 0.0), jnp.where(first, 0.0, q)], axis=0).astype(BF16)


def _ctx_attn_kernel(q_ref, k_ref, v_ref, o_ref):
    first = lax.broadcasted_iota(jnp.int32, (SEQ, LANES), 1) < NA_HD
    qm = _pair_queries(q_ref[...] * (NA_HD ** -0.5), first)
    s = lax.dot_general(k_ref[...], qm, (((1,), (1,)), ((), ())), preferred_element_type=F32)
    e = jnp.exp(s - jnp.max(s, axis=0, keepdims=True))
    den = jnp.sum(e, axis=0, keepdims=True)
    o = lax.dot_general(e.astype(BF16), v_ref[...], (((0,), (0,)), ((), ())), preferred_element_type=F32)
    o = jnp.where(first, o[:SEQ], o[SEQ:])
    den_t = jnp.transpose(jnp.broadcast_to(den, (LANES, 2 * SEQ)))
    o_ref[...] = (o / jnp.where(first, den_t[:SEQ], den_t[SEQ:])).astype(o_ref.dtype)


def _ctx_attention(proj):
    def col(cb):
        return pl.BlockSpec((None, SEQ, LANES), lambda s, p: (cb + p, s, 0))

    return pl.pallas_call(
        _ctx_attn_kernel, grid=(BATCH, NA_HEADS // 2),
        in_specs=[col(_QB), col(_KB), col(_VB)],
        out_specs=pl.BlockSpec((None, SEQ, LANES), lambda s, p: (p, s, 0)),
        out_shape=jax.ShapeDtypeStruct((NA_HEADS // 2, N_CTX, LANES), BF16),
        compiler_params=_params(("parallel", "parallel")), name="ctx_attention",
    )(proj, proj, proj)


_NA_UNROLL = 4


def _na_kernel(q_ref, k_ref, v_ref, kc_ref, vc_ref, bias_ref, o_ref, kcb_scr, vcb_scr):
    rows = DEC_SEQ // GRID_W
    win = NA_ROWS * GRID_W
    scale = NA_HD ** -0.5
    dn_nt = (((1,), (1,)), ((), ()))
    dn_tn = (((0,), (0,)), ((), ()))

    kcb_scr[...] = kc_ref[...].astype(BF16)
    vcb_scr[...] = vc_ref[...].astype(BF16)
    first = lax.broadcasted_iota(jnp.int32, (GRID_W, LANES), 1) < NA_HD

    def body(it, carry):
        rr = [it * _NA_UNROLL + j for j in range(_NA_UNROLL)]
        rss = [jnp.clip(r - NA_ROWS // 2, 0, rows - NA_ROWS) for r in rr]
        qsls = [pl.ds(pl.multiple_of(r * GRID_W, GRID_W), GRID_W) for r in rr]
        wsls = [pl.ds(pl.multiple_of(rs * GRID_W, GRID_W), win) for rs in rss]
        qms, s_wins, s_ctxs = [], [], []
        for r, rs, qsl, wsl in zip(rr, rss, qsls, wsls):
            qm = _pair_queries(q_ref[qsl, :] * scale, first)
            bias = jnp.concatenate([bias_ref[NA_ROWS - 1 - (r - rs) + i] for i in range(NA_ROWS)], axis=0)
            s_wins.append(lax.dot_general(k_ref[wsl, :], qm, dn_nt, preferred_element_type=F32) + bias)
            s_ctxs.append(lax.dot_general(kcb_scr[...], qm, dn_nt, preferred_element_type=F32))
        ms = [jnp.maximum(jnp.max(sw, axis=0, keepdims=True), jnp.max(sc, axis=0, keepdims=True))
              for sw, sc in zip(s_wins, s_ctxs)]
        e_wins = [jnp.exp(sw - m) for sw, m in zip(s_wins, ms)]
        e_ctxs = [jnp.exp(sc - m) for sc, m in zip(s_ctxs, ms)]
        dens = [jnp.sum(ew, axis=0, keepdims=True) + jnp.sum(ec, axis=0, keepdims=True)
                for ew, ec in zip(e_wins, e_ctxs)]
        for qsl, wsl, ew, ec, den in zip(qsls, wsls, e_wins, e_ctxs, dens):
            o = (lax.dot_general(ew.astype(BF16), v_ref[wsl, :], dn_tn, preferred_element_type=F32)
                 + lax.dot_general(ec.astype(BF16), vcb_scr[...], dn_tn, preferred_element_type=F32))
            o = o / jnp.transpose(jnp.broadcast_to(den, (LANES, LANES)))
            o_ref[qsl, :] = jnp.where(first, o[:GRID_W], o[GRID_W:]).astype(o_ref.dtype)
        return carry

    lax.fori_loop(0, rows // _NA_UNROLL, body, 0)


def _na_bias_table(rpb):
    col = jnp.arange(GRID_W)
    cs = jnp.clip(col - NA_COLS // 2, 0, GRID_W - NA_COLS)
    col_ok = (col[None, :] >= cs[:, None]) & (col[None, :] < cs[:, None] + NA_COLS)
    dc = jnp.clip(col[None, :] - col[:, None] + NA_COLS - 1, 0, 2 * NA_COLS - 2)
    onehot = (dc.T[None, :, :] == jnp.arange(2 * NA_COLS - 1)[:, None, None]).astype(F32)
    t = jnp.einsum('hrd,dkq->hrkq', rpb.astype(F32), onehot, precision=lax.Precision.HIGHEST)
    t = jnp.where(col_ok.T[None, None], t, NEG_INF)
    t = t.reshape(NA_HEADS // 2, 2, 2 * NA_ROWS - 1, GRID_W, GRID_W)
    return jnp.concatenate([t[:, 0], t[:, 1]], axis=-1)


def _na_attention(proj, kctx, vctx, rpb):
    blk = N_CTX // DEC_SEQ

    def col(cb):
        return pl.BlockSpec((None, DEC_SEQ, LANES), lambda b, p: (cb + p, blk + b, 0))

    ctx = pl.BlockSpec((None, PAST_LEN, LANES), lambda b, p: (b, 0, p))
    return pl.pallas_call(
        _na_kernel, grid=(DEC_BATCH, NA_HEADS // 2),
        in_specs=[col(_QB), col(_KB), col(_VB), ctx, ctx,
                  pl.BlockSpec((None, 2 * NA_ROWS - 1, GRID_W, 2 * GRID_W), lambda b, p: (p, 0, 0, 0))],
        out_specs=pl.BlockSpec((None, DEC_SEQ, LANES), lambda b, p: (p, b, 0)),
        out_shape=jax.ShapeDtypeStruct((NA_HEADS // 2, N_LAT, LANES), BF16),
        scratch_shapes=[pltpu.VMEM((PAST_LEN, LANES), BF16), pltpu.VMEM((PAST_LEN, LANES), BF16)],
        compiler_params=_params(("parallel", "parallel")), name="na_attention",
    )(proj, proj, proj, kctx, vctx, _na_bias_table(rpb))


_LOGIT0 = N_EGROUPS
_R_E, _R_W, _R_RANK = 0, 2, 4


def _lane_min_where(mask, lane):
    return jnp.min(jnp.where(mask, lane, LANES), axis=-1, keepdims=True)


def _route_rows(lg, carry_ref, tri_ref):
    big = -3.0e38
    lane = lax.broadcasted_iota(jnp.int32, lg.shape, 1)
    is_g = lane < N_EGROUPS
    gmax = jnp.max(jnp.where(is_g, lg, big), axis=-1, keepdims=True)
    gsum = jnp.sum(jnp.where(is_g, jnp.exp(jnp.where(is_g, lg - gmax, 0.0)), 0.0), axis=-1, keepdims=True)
    pg_top = 1.0 / gsum
    g_idx = _lane_min_where(jnp.logical_and(is_g, lg == gmax), lane)
    in_g = jnp.logical_and(lane >= _LOGIT0, lax.shift_right_arithmetic(lane - _LOGIT0, 3) == g_idx)
    in_g = jnp.logical_and(in_g, lane < _LOGIT0 + N_EXPERTS)
    m1 = jnp.max(jnp.where(in_g, lg, big), axis=-1, keepdims=True)
    i1 = _lane_min_where(jnp.logical_and(in_g, lg == m1), lane)
    rest = jnp.logical_and(in_g, lane != i1)
    m2 = jnp.max(jnp.where(rest, lg, big), axis=-1, keepdims=True)
    i2 = _lane_min_where(jnp.logical_and(rest, lg == m2), lane)
    e2 = jnp.exp(m2 - m1)
    w1 = pg_top * (1.0 / (1.0 + e2))
    w2 = pg_top * (e2 / (1.0 + e2))
    hit1 = lane == i1
    hit2 = lane == i2
    picked = jnp.where(jnp.logical_or(hit1, hit2), 1.0, 0.0)
    before = jnp.dot(tri_ref[...], picked.astype(BF16), preferred_element_type=F32) + carry_ref[...]
    r1 = jnp.sum(jnp.where(hit1, before, 0.0), axis=-1, keepdims=True)
    r2 = jnp.sum(jnp.where(hit2, before, 0.0), axis=-1, keepdims=True)
    carry_ref[...] = carry_ref[...] + jnp.sum(picked, axis=0, keepdims=True)
    rec = jnp.zeros(lg.shape, F32)
    for ln, val in ((_R_E, (i1 - _LOGIT0).astype(F32)), (_R_E + 1, (i2 - _LOGIT0).astype(F32)),
                    (_R_W, w1), (_R_W + 1, w2), (_R_RANK, r1), (_R_RANK + 1, r2)):
        rec = jnp.where(lane == ln, val, rec)
    return rec


def _moe_input(xnew, first, tail_in, tail_out, tail_scr):
    g2_ref, sc2_ref, sh2_ref, wrh_ref, wrl_ref, br_ref = tail_in
    x_out, h_out, rec_out, cnt_out = tail_out
    tri_scr, carry_scr = tail_scr

    @pl.when(first)
    def _():
        tm = tri_scr.shape[0]
        r = lax.broadcasted_iota(jnp.int32, (tm, tm), 0)
        c = lax.broadcasted_iota(jnp.int32, (tm, tm), 1)
        tri_scr[...] = jnp.where(c < r, 1.0, 0.0).astype(BF16)
        carry_scr[...] = jnp.zeros(carry_scr.shape, F32)

    x_out[...] = xnew
    h = _rms(xnew, g2_ref[...]) * (1.0 + sc2_ref[...]) + sh2_ref[...]
    hh = h.astype(BF16)
    hl = (h - hh.astype(F32)).astype(BF16)
    h_out[...] = hh
    lg = (jnp.dot(hh, wrh_ref[...], preferred_element_type=F32)
          + jnp.dot(hh, wrl_ref[...], preferred_element_type=F32)
          + jnp.dot(hl, wrh_ref[...], preferred_element_type=F32) + br_ref[...])
    rec_out[...] = _route_rows(lg, carry_scr, tri_scr)
    cnt_out[...] = carry_scr[...]


def _even_out_kernel(oac_ref, obc_ref, oal_ref, obl_ref, x_ref, w_ref, gate_ref, *rest, ctx_tiles):
    tail_in, tail_out, (w_scr,), tail_scr = rest[:6], rest[6:10], rest[10:11], rest[11:]
    first = pl.program_id(0) == 0

    @pl.when(first)
    def _():
        w_scr[...] = w_ref[...].astype(BF16)

    is_ctx = pl.program_id(0) < ctx_tiles
    parts = [jnp.where(is_ctx, c_ref[hb], l_ref[hb])
             for c_ref, l_ref in ((oac_ref, oal_ref), (obc_ref, obl_ref)) for hb in range(DN_HEADS)]
    mix = jnp.concatenate(parts, axis=-1)
    out = jnp.dot(mix, w_scr[...], preferred_element_type=F32)
    _moe_input(x_ref[...] + gate_ref[...] * out, first, tail_in, tail_out, tail_scr)


def _tail_specs(tm):
    const = lambda shape: pl.BlockSpec(shape, lambda i: (0,) * len(shape))
    in_specs = [_mod_spec(2, tm), const((1, D_MODEL)), _mod_spec(4, tm), _mod_spec(3, tm),
                const((D_MODEL, LANES)), const((D_MODEL, LANES)), const((1, LANES))]
    out_specs = [pl.BlockSpec((tm, D_MODEL), lambda i: (i, 0)),
                 pl.BlockSpec((tm, D_MODEL), lambda i: (i, 0)),
                 pl.BlockSpec((tm, LANES), lambda i: (i, 0)),
                 const((1, LANES))]
    out_shape = [jax.ShapeDtypeStruct((N_TOK, D_MODEL), F32),
                 jax.ShapeDtypeStruct((N_TOK, D_MODEL), BF16),
                 jax.ShapeDtypeStruct((N_TOK, LANES), F32),
                 jax.ShapeDtypeStruct((1, LANES), F32)]
    scratch = [pltpu.VMEM((tm, tm), BF16), pltpu.VMEM((1, LANES), F32)]
    return in_specs, out_specs, out_shape, scratch


def _router_weights(w_rg, b_rg, w_re, b_re):
    pad = LANES - N_EGROUPS - N_EXPERTS
    w = jnp.concatenate([w_rg, w_re, jnp.zeros((D_MODEL, pad), F32)], axis=1)
    b = jnp.concatenate([b_rg, b_re, jnp.zeros((pad,), F32)]).reshape(1, LANES)
    hi = w.astype(BF16)
    lo = (w - hi.astype(F32)).astype(BF16)
    return hi, lo, b


def _even_out(oa_ctx, ob_ctx, oa_lat, ob_lat, x, w_out, mods, g2, router):
    tm = 512
    ctx_tiles = N_CTX // tm
    tail_in, out_specs, out_shape, tail_scr = _tail_specs(tm)
    ctxblk = pl.BlockSpec((DN_HEADS, tm, LANES), lambda i: (0, jnp.minimum(i, ctx_tiles - 1), 0))
    latblk = pl.BlockSpec((DN_HEADS, tm, LANES), lambda i: (0, jnp.maximum(i - ctx_tiles, 0), 0))
    return pl.pallas_call(
        functools.partial(_even_out_kernel, ctx_tiles=ctx_tiles), grid=(N_TOK // tm,),
        in_specs=[ctxblk, ctxblk, latblk, latblk, pl.BlockSpec((tm, D_MODEL), lambda i: (i, 0)),
                  pl.BlockSpec((D_MODEL, D_MODEL), lambda i: (0, 0))] + tail_in,
        out_specs=out_specs, out_shape=out_shape,
        scratch_shapes=[pltpu.VMEM((D_MODEL, D_MODEL), BF16)] + tail_scr,
        compiler_params=_params(("arbitrary",)), name="even_out",
    )(oa_ctx, ob_ctx, oa_lat, ob_lat, x, w_out, mods, g2.reshape(1, D_MODEL), mods, mods, *router)


def _gelu_tanh(x):
    return x * (0.5 * (1.0 + jnp.tanh(0.7978845608028654 * (x + 0.044715 * (x * x * x)))))


def _sgu_kernel(x_ref, g1_ref, sh1_ref, sc1_ref, win_ref, lng_ref, lnb_ref, ws_ref, bst_ref, wout_ref, gate_ref,
                *rest, tm):
    tail_in, tail_out, (v_scr, m_scr), tail_scr = rest[:6], rest[6:10], rest[10:12], rest[12:]
    first = pl.program_id(0) == 0
    x = x_ref[...]
    h = (_rms(x, g1_ref[...]) * (1.0 + sc1_ref[...]) + sh1_ref[...]).astype(BF16)

    v = _gelu_tanh(jnp.dot(h, win_ref[:, SG_W:], preferred_element_type=F32))
    mu = jnp.mean(v, axis=-1, keepdims=True)
    vc = v - mu
    var = jnp.mean(vc * vc, axis=-1, keepdims=True)
    v_scr[...] = (vc * lax.rsqrt(var + EPS) * lng_ref[...] + lnb_ref[...]).astype(BF16)

    for g in range(SG_GROUPS):
        cs = slice(g * SG_GW, (g + 1) * SG_GW)
        u = _gelu_tanh(jnp.dot(h, win_ref[:, cs], preferred_element_type=F32))
        w_sp = ws_ref[g].astype(BF16)
        for c in range(tm // SG_CHUNK):
            rs = slice(c * SG_CHUNK, (c + 1) * SG_CHUNK)
            sp = jnp.dot(w_sp, v_scr[rs, cs], preferred_element_type=F32) + bst_ref[:, g:g + 1]
            m_scr[rs, cs] = (u[rs] * sp).astype(BF16)
    out = jnp.dot(m_scr[...], wout_ref[...], preferred_element_type=F32)
    _moe_input(x + gate_ref[...] * out, first, tail_in, tail_out, tail_scr)


def _sgu_layer(x, mods, g1, w_in, ln_g, ln_b, w_s, b_s, w_out, g2, router):
    tm = 512
    tail_in, out_specs, out_shape, tail_scr = _tail_specs(tm)
    const = lambda shape: pl.BlockSpec(shape, lambda i: (0,) * len(shape))
    held = lambda shape: pl.BlockSpec(shape, lambda i: (0,) * len(shape), pipeline_mode=pl.Buffered(1))
    return pl.pallas_call(
        functools.partial(_sgu_kernel, tm=tm), grid=(N_TOK // tm,),
        in_specs=[pl.BlockSpec((tm, D_MODEL), lambda i: (i, 0)),
                  const((1, D_MODEL)), _mod_spec(0, tm), _mod_spec(1, tm),
                  held((D_MODEL, 2 * SG_W)), const((1, SG_W)), const((1, SG_W)),
                  const((SG_GROUPS, SG_CHUNK, SG_CHUNK)), const((SG_CHUNK, SG_GROUPS)),
                  held((SG_W, D_MODEL))] + tail_in,
        out_specs=out_specs, out_shape=out_shape,
        scratch_shapes=[pltpu.VMEM((tm, SG_W), BF16), pltpu.VMEM((tm, SG_W), BF16)] + tail_scr,
        compiler_params=_params(("arbitrary",)), name="sgu_layer",
    )(x, g1.reshape(1, D_MODEL), mods, mods, w_in.astype(BF16), ln_g.reshape(1, SG_W), ln_b.reshape(1, SG_W),
      w_s, b_s.T, w_out.astype(BF16), mods, g2.reshape(1, D_MODEL), mods, mods, *router)


def _plan(rec, cnt):
    e_idx = rec[:, _R_E:_R_E + 2].astype(jnp.int32)
    rank = rec[:, _R_RANK:_R_RANK + 2].astype(jnp.int32)
    counts = cnt[0, _LOGIT0:_LOGIT0 + N_EXPERTS].astype(jnp.int32)
    padded = (counts + MOE_BLK - 1) // MOE_BLK * MOE_BLK
    pad_end = jnp.cumsum(padded)
    pad_start = pad_end - padded
    hit = e_idx[:, :, None] == jnp.arange(N_EXPERTS, dtype=jnp.int32)[None, None, :]
    dest = jnp.sum(jnp.where(hit, pad_start[None, None, :], 0), axis=-1) + rank
    blk0 = jnp.arange(MOE_NBLK, dtype=jnp.int32) * MOE_BLK
    blk_e = jnp.minimum(jnp.sum((pad_end[None, :] <= blk0[:, None]).astype(jnp.int32), axis=-1),
                        N_EXPERTS - 1)
    n_used = (pad_end[-1] // MOE_BLK).astype(jnp.int32).reshape(1)
    owns = counts > 0
    slot_of = (jnp.cumsum(owns.astype(jnp.int32)) - 1) % 2
    ids = jnp.arange(N_EXPERTS, dtype=jnp.int32)
    later = jnp.logical_and(owns[None, :], ids[None, :] > ids[:, None])
    next_of = jnp.min(jnp.where(later, ids[None, :], N_EXPERTS), axis=-1)
    next_of = jnp.where(next_of == N_EXPERTS, -1, next_of)
    return dest, blk_e, n_used, slot_of[blk_e], next_of[blk_e]


def _expert_kernel(blk_e_ref, n_used_ref, slot_ref, next_ref, x_ref, wg_hbm, wu_hbm, wd_hbm, o_ref,
                   wg_buf, wu_buf, wd_buf, wg_scr, wu_scr, wd_scr, sems, *, layer):
    j = pl.program_id(0)
    e = blk_e_ref[j]
    slot = slot_ref[j]
    fresh = jnp.logical_or(j == 0, e != blk_e_ref[jnp.maximum(j - 1, 0)])
    live = j < n_used_ref[0]

    def copies(expert, s):
        return (pltpu.make_async_copy(wg_hbm.at[layer, expert], wg_buf.at[s], sems.at[s, 0]),
                pltpu.make_async_copy(wu_hbm.at[layer, expert], wu_buf.at[s], sems.at[s, 1]),
                pltpu.make_async_copy(wd_hbm.at[layer, expert], wd_buf.at[s], sems.at[s, 2]))

    @pl.when(j == 0)
    def _():
        for cp in copies(e, slot):
            cp.start()

    @pl.when(jnp.logical_and(fresh, live))
    def _():
        for cp in copies(e, slot):
            cp.wait()
        nxt = next_ref[j]

        @pl.when(nxt >= 0)
        def _():
            for cp in copies(nxt, 1 - slot):
                cp.start()

        wg_scr[...] = wg_buf[slot].astype(BF16)
        wu_scr[...] = wu_buf[slot].astype(BF16)
        wd_scr[...] = wd_buf[slot].astype(BF16)

    @pl.when(live)
    def _():
        x = x_ref[...]
        gt = jnp.dot(x, wg_scr[...], preferred_element_type=F32)
        up = jnp.dot(x, wu_scr[...], preferred_element_type=F32)
        hb = (_silu(gt) * up).astype(BF16)
        o_ref[...] = jnp.dot(hb, wd_scr[...], preferred_element_type=F32)

    @pl.when(jnp.logical_not(live))
    def _():
        o_ref[...] = jnp.zeros(o_ref.shape, o_ref.dtype)


def _experts(x_pad, blk_e, n_used, slot, nxt, w_gate, w_up, w_down, layer):
    hbm = pl.BlockSpec(memory_space=pl.ANY)
    grid_spec = pltpu.PrefetchScalarGridSpec(
        num_scalar_prefetch=4, grid=(MOE_NBLK,),
        in_specs=[pl.BlockSpec((MOE_BLK, D_MODEL), lambda j, *_: (j, 0)), hbm, hbm, hbm],
        out_specs=pl.BlockSpec((MOE_BLK, D_MODEL), lambda j, *_: (j, 0)),
        scratch_shapes=[pltpu.VMEM((2, D_MODEL, D_EXPERT), F32), pltpu.VMEM((2, D_MODEL, D_EXPERT), F32),
                        pltpu.VMEM((2, D_EXPERT, D_MODEL), F32),
                        pltpu.VMEM((D_MODEL, D_EXPERT), BF16), pltpu.VMEM((D_MODEL, D_EXPERT), BF16),
                        pltpu.VMEM((D_EXPERT, D_MODEL), BF16), pltpu.SemaphoreType.DMA((2, 3))])
    return pl.pallas_call(
        functools.partial(_expert_kernel, layer=layer), grid_spec=grid_spec,
        out_shape=jax.ShapeDtypeStruct((MOE_NBLK * MOE_BLK, D_MODEL), F32),
        compiler_params=_params(("arbitrary",)), name="experts",
    )(blk_e, n_used, slot, nxt, x_pad, w_gate, w_up, w_down)


def _combine_kernel(x_ref, ya_ref, yb_ref, rec_ref, gate_ref, fg_ref, o_ref, *, final):
    rec = rec_ref[...]
    y = rec[:, _R_W:_R_W + 1] * ya_ref[...] + rec[:, _R_W + 1:_R_W + 2] * yb_ref[...]
    xn = x_ref[...] + gate_ref[...] * y
    o_ref[...] = _rms(xn, fg_ref[...]) if final else xn


def _combine(x, ya, yb, rec, mods, final_g, final):
    tm = 512
    blk = pl.BlockSpec((tm, D_MODEL), lambda i: (i, 0))
    return pl.pallas_call(
        functools.partial(_combine_kernel, final=final), grid=(N_TOK // tm,),
        in_specs=[blk, blk, blk, pl.BlockSpec((tm, LANES), lambda i: (i, 0)), _mod_spec(5, tm),
                  pl.BlockSpec((1, D_MODEL), lambda i: (0, 0))],
        out_specs=blk, out_shape=jax.ShapeDtypeStruct((N_TOK, D_MODEL), F32),
        compiler_params=_params(("parallel",)), name="moe_combine",
    )(x, ya, yb, rec, mods, final_g.reshape(1, D_MODEL))


def _moe(x, h, rec, cnt, mods, w_gate, w_up, w_down, layer, final_g, final):
    dest, blk_e, n_used, slot, nxt = _plan(rec, cnt)
    tok = jnp.arange(2 * N_TOK, dtype=jnp.int32) // 2
    row_tok = jnp.zeros((MOE_NBLK * MOE_BLK,), jnp.int32).at[dest.reshape(-1)].set(tok, unique_indices=True)
    y_pad = _experts(h[row_tok], blk_e, n_used, slot, nxt, w_gate, w_up, w_down, layer)
    return _combine(x, y_pad[dest[:, 0]], y_pad[dest[:, 1]], rec, mods, final_g, final)


def kernel(x_prompt, x_sample, c, cache_k, cache_v, state_delta, c_ctx, ada_w, ada_b, norm1_g, norm2_g, final_g,
           ev_w_in, ev_w_out, ev_conv_w, ev_a_log, ev_dt_bias, ev_onorm_g, ev_rpb, od_w_in, od_ln_g, od_ln_b,
           od_w_s, od_b_s, od_w_out, moe_w_rg, moe_b_rg, moe_w_re, moe_b_re, moe_w_gate, moe_w_up, moe_w_down):
    x = jnp.concatenate([x_prompt.reshape(N_CTX, D_MODEL), x_sample.reshape(N_LAT, D_MODEL)], axis=0)
    cond = jnp.concatenate([c_ctx[None, :], c, jnp.zeros((N_COND - 1 - DEC_BATCH, D_MODEL), F32)], axis=0)
    mods_all = _ada_mods(cond, ada_w, ada_b)
    kctx_all = cache_k.reshape(DEC_BATCH, -1, PAST_LEN, NA_HEADS * NA_HD)
    vctx_all = cache_v.reshape(DEC_BATCH, -1, PAST_LEN, NA_HEADS * NA_HD)

    ks, vs, ss = [], [], []
    for l in range(DEPTH):
        mods = mods_all[l]
        router = _router_weights(moe_w_rg[l], moe_b_rg[l], moe_w_re[l], moe_b_re[l])
        if l % 2 == 0:
            e = l // 2
            proj, ab, kv = _even_proj(x, mods, norm1_g[l], ev_w_in[e])
            dn = (proj, ab, ev_conv_w[e], ev_a_log[e], ev_dt_bias[e], ev_onorm_g[e])
            oa_ctx, s_fin = _delta_heads(*dn, SEQ, BATCH, 0, None)
            oa_lat, _ = _delta_heads(*dn, DEC_SEQ, DEC_BATCH, N_CTX // DEC_SEQ, state_delta[:, e])
            ob_ctx = _ctx_attention(proj)
            ob_lat = _na_attention(proj, kctx_all[:, e], vctx_all[:, e], ev_rpb[e])
            x, h, rec, cnt = _even_out(oa_ctx, ob_ctx, oa_lat, ob_lat, x, ev_w_out[e], mods, norm2_g[l],
                                       router)
            na_w = NA_HEADS * NA_HD
            ks.append(kv[:N_CTX, :na_w].reshape(BATCH, SEQ, NA_HEADS, NA_HD))
            vs.append(kv[:N_CTX, na_w:].reshape(BATCH, SEQ, NA_HEADS, NA_HD))
            ss.append(s_fin)
        else:
            o = l // 2
            x, h, rec, cnt = _sgu_layer(x, mods, norm1_g[l], od_w_in[o], od_ln_g[o], od_ln_b[o], od_w_s[o],
                                        od_b_s[o], od_w_out[o], norm2_g[l], router)
        x = _moe(x, h, rec, cnt, mods, moe_w_gate, moe_w_up, moe_w_down, l, final_g, l == DEPTH - 1)

    y_prompt = x[:N_CTX].reshape(BATCH, SEQ, D_MODEL)
    y_sample = x[N_CTX:].reshape(DEC_BATCH, DEC_SEQ, D_MODEL)
    return (y_prompt, y_sample, jnp.stack(ks, axis=1), jnp.stack(vs, axis=1), jnp.stack(ss, axis=1))
```

```python
import functools

import jax
import jax.numpy as jnp
from jax import lax
from jax.experimental import pallas as pl
from jax.experimental.pallas import tpu as pltpu

F32 = jnp.float32
BF16 = jnp.bfloat16

D_MODEL = 1024
BATCH = 16
SEQ = 256
DEPTH = 4
DEC_BATCH = 4
DEC_SEQ = 2048
PAST_LEN = 512
GRID_W = 64
EPS = 1e-6
NEG_INF = -1e30

DN_HEADS = 4
DN_DK = 128
DN_CHUNK = 64
NA_HEADS = 8
NA_HD = 64
NA_ROWS = 8
NA_COLS = 16
SG_CHUNK = 128
SG_GROUPS = 8
SG_W = 2 * D_MODEL
SG_GW = SG_W // SG_GROUPS
N_EGROUPS = 4
EXP_PER_GROUP = 8
N_EXPERTS = 32
D_EXPERT = 512

N_CTX = BATCH * SEQ
N_LAT = DEC_BATCH * DEC_SEQ
N_TOK = N_CTX + N_LAT
N_COND = 8
PROJ_W = 4096
LANES = 128
MOE_BLK = 256
MOE_NBLK = -(-(2 * N_TOK + N_EXPERTS * (MOE_BLK - 1)) // MOE_BLK)
VMEM_LIMIT = 56 * 1024 * 1024

_QA, _KA, _VA, _ZA, _QB, _KB, _VB = 0, 4, 8, 12, 16, 20, 24


def _params(sem):
    return pltpu.CompilerParams(dimension_semantics=sem, vmem_limit_bytes=VMEM_LIMIT)


def _bdot(a, b):
    return jnp.dot(a.astype(BF16), b.astype(BF16), preferred_element_type=F32)


def _bdot_nt(a, b):
    return lax.dot_general(a.astype(BF16), b.astype(BF16), (((1,), (1,)), ((), ())),
                           preferred_element_type=F32)


def _bdot_tn(a, b):
    return lax.dot_general(a.astype(BF16), b.astype(BF16), (((0,), (0,)), ((), ())),
                           preferred_element_type=F32)


def _split3(a):
    p0 = a.astype(BF16)
    r = a - p0.astype(F32)
    p1 = r.astype(BF16)
    p2 = (r - p1.astype(F32)).astype(BF16)
    return p0, p1, p2


def _dot3(a, b):
    ah = a.astype(BF16)
    al = (a - ah.astype(F32)).astype(BF16)
    bh = b.astype(BF16)
    bl = (b - bh.astype(F32)).astype(BF16)
    return (jnp.dot(ah, bh, preferred_element_type=F32) + jnp.dot(ah, bl, preferred_element_type=F32)
            + jnp.dot(al, bh, preferred_element_type=F32))


def _mask_bf16(m01):
    return jnp.where(m01, 1.0, 0.0).astype(BF16)


def _xdot(m01, a):
    m = _mask_bf16(m01)
    p0, p1, p2 = _split3(a)
    return (jnp.dot(m, p0, preferred_element_type=F32) + jnp.dot(m, p1, preferred_element_type=F32)
            + jnp.dot(m, p2, preferred_element_type=F32))


def _xdot_nt(m01, a):
    m = _mask_bf16(m01)
    dn = (((1,), (1,)), ((), ()))
    p0, p1, p2 = _split3(a)
    return (lax.dot_general(m, p0, dn, preferred_element_type=F32)
            + lax.dot_general(m, p1, dn, preferred_element_type=F32)
            + lax.dot_general(m, p2, dn, preferred_element_type=F32))


def _xdot_r(a, m01):
    m = _mask_bf16(m01)
    p0, p1, p2 = _split3(a)
    return (jnp.dot(p0, m, preferred_element_type=F32) + jnp.dot(p1, m, preferred_element_type=F32)
            + jnp.dot(p2, m, preferred_element_type=F32))


def _sigmoid(x):
    return 0.5 * jnp.tanh(0.5 * x) + 0.5


def _silu(x):
    return x * _sigmoid(x)


def _rms(x, g):
    return x * lax.rsqrt(jnp.mean(x * x, axis=-1, keepdims=True) + EPS) * g


def _cond_index(row):
    return jnp.where(row < N_CTX, 0, 1 + (row - N_CTX) // DEC_SEQ)


def _mod_spec(k, tm):
    return pl.BlockSpec((None, None, 1, D_MODEL), lambda i, *_: (_cond_index(i * tm), k, 0, 0))


def _ada_kernel(c_ref, w_ref, b_ref, o_ref):
    o_ref[...] = _bdot(_silu(c_ref[...]), w_ref[...]) + b_ref[...]


def _ada_mods(cond, ada_w, ada_b):
    tn = 1536
    out = pl.pallas_call(
        _ada_kernel, grid=(DEPTH, 6 * D_MODEL // tn),
        in_specs=[pl.BlockSpec((N_COND, D_MODEL), lambda l, j: (0, 0)),
                  pl.BlockSpec((None, D_MODEL, tn), lambda l, j: (l, 0, j)),
                  pl.BlockSpec((None, 1, tn), lambda l, j: (l, 0, j))],
        out_specs=pl.BlockSpec((None, N_COND, tn), lambda l, j: (l, 0, j)),
        out_shape=jax.ShapeDtypeStruct((DEPTH, N_COND, 6 * D_MODEL), F32),
        compiler_params=_params(("parallel", "parallel")), name="ada_mods",
    )(cond, ada_w, ada_b.reshape(DEPTH, 1, 6 * D_MODEL))
    return out.reshape(DEPTH, N_COND, 6, 1, D_MODEL)


_EV_TN = 512
_EV_W = 7 * DN_HEADS * LANES
_KV_COL0 = _KB * LANES


def _even_proj_kernel(x_ref, g_ref, sh_ref, sc_ref, w_ref, wab_ref, o_ref, ab_ref, kv_ref):
    h = (_rms(x_ref[...], g_ref[...]) * (1.0 + sc_ref[...]) + sh_ref[...]).astype(BF16)
    ab_ref[...] = jnp.dot(h, wab_ref[...], preferred_element_type=F32)
    for j in range(_EV_W // _EV_TN):
        c0 = j * _EV_TN
        y = jnp.dot(h, w_ref[:, c0:c0 + _EV_TN], preferred_element_type=F32)
        for c in range(_EV_TN // LANES):
            o_ref[c0 // LANES + c] = y[:, c * LANES:(c + 1) * LANES].astype(BF16)
        if c0 >= _KV_COL0:
            kv_ref[:, c0 - _KV_COL0:c0 - _KV_COL0 + _EV_TN] = y


def _even_proj(x, mods, g, w_in):
    tm = 512
    n_ab = 4 * DN_HEADS
    ab0 = 4 * DN_HEADS * DN_DK
    w_main = jnp.concatenate([w_in[:, :ab0], w_in[:, ab0 + n_ab:]], axis=1).astype(BF16)
    w_ab = jnp.concatenate([w_in[:, ab0:ab0 + n_ab], jnp.zeros((D_MODEL, LANES - n_ab), F32)],
                           axis=1).astype(BF16)
    held = lambda shape: pl.BlockSpec(shape, lambda i: (0,) * len(shape), pipeline_mode=pl.Buffered(1))
    return pl.pallas_call(
        _even_proj_kernel, grid=(N_TOK // tm,),
        in_specs=[pl.BlockSpec((tm, D_MODEL), lambda i: (i, 0)),
                  pl.BlockSpec((1, D_MODEL), lambda i: (0, 0)),
                  _mod_spec(0, tm), _mod_spec(1, tm),
                  held((D_MODEL, _EV_W)), held((D_MODEL, LANES))],
        out_specs=[pl.BlockSpec((_EV_W // LANES, tm, LANES), lambda i: (0, i, 0)),
                   pl.BlockSpec((tm, LANES), lambda i: (i, 0)),
                   pl.BlockSpec((tm, 2 * NA_HEADS * NA_HD), lambda i: (i, 0))],
        out_shape=[jax.ShapeDtypeStruct((_EV_W // LANES, N_TOK, LANES), BF16),
                   jax.ShapeDtypeStruct((N_TOK, LANES), F32),
                   jax.ShapeDtypeStruct((N_TOK, 2 * NA_HEADS * NA_HD), F32)],
        compiler_params=_params(("parallel",)), name="even_proj",
    )(x, g.reshape(1, D_MODEL), mods, mods, w_main, w_ab)


_CHUNK_SHIFT = DN_CHUNK.bit_length() - 1
_CUM_ROWS = 256
_DN_PREP = 8
_SERIES_FINE = 3
_MQ_ROWS = DN_DK + DN_CHUNK


def _dn_kernel(*refs, T, has_s0, want_state):
    it = iter(refs)
    q_ref, k_ref, v_ref, z_ref, ab_ref = (next(it) for _ in range(5))
    cwq_ref, cwk_ref, cwv_ref, alog_ref, dtb_ref, og_ref = (next(it) for _ in range(6))
    s0_ref = next(it) if has_s0 else None
    o_ref = next(it)
    sfin_ref = next(it) if want_state else None
    qc, kc, vc, gsc, bsc, osc, b_s, mq_s = (next(it) for _ in range(8))

    C = DN_CHUNK
    n = T // C
    hd = pl.program_id(1)

    row = lax.broadcasted_iota(jnp.int32, (T, 1), 0)

    def conv(x_ref, cw_ref):
        x = x_ref[...].astype(F32)
        xp = jnp.where(row == 0, 0.0, pltpu.roll(x, 1, 0))
        xn = jnp.where(row == T - 1, 0.0, pltpu.roll(x, T - 1, 0))
        return _silu(cw_ref[0:1, :] * xp + cw_ref[1:2, :] * x + cw_ref[2:3, :] * xn)

    def l2n(x):
        return x * lax.rsqrt(jnp.sum(x * x, axis=-1, keepdims=True) + EPS)

    qc[...] = l2n(conv(q_ref, cwq_ref)) * (DN_DK ** -0.5)
    kc[...] = l2n(conv(k_ref, cwk_ref))
    vc[...] = conv(v_ref, cwv_ref)

    ab = ab_ref[...]
    sel_r = lax.broadcasted_iota(jnp.int32, (LANES, LANES), 0)
    for d in range(2):
        alpha = _xdot_r(ab, sel_r == d * DN_HEADS + hd)
        blog = _xdot_r(ab, sel_r == 2 * DN_HEADS + d * DN_HEADS + hd)
        x = alpha + dtb_ref[d, hd]
        sp = jnp.maximum(x, 0.0) + jnp.log1p(jnp.exp(-jnp.abs(x)))
        a = jnp.exp(jnp.full((1, LANES), alog_ref[d, hd], F32))
        gsc[d] = -a * sp
        bsc[d] = _sigmoid(blog)

    pr = lax.broadcasted_iota(jnp.int32, (_CUM_ROWS, _CUM_ROWS), 0)
    pc = lax.broadcasted_iota(jnp.int32, (_CUM_ROWS, _CUM_ROWS), 1)
    same = lax.shift_right_logical(pr, _CHUNK_SHIFT) == lax.shift_right_logical(pc, _CHUNK_SHIFT)
    cum_mask = (jnp.logical_and(same, pc <= pr), jnp.logical_and(same, pc >= pr))

    def cum_body(i, carry):
        sl = pl.ds(pl.multiple_of(i * _CUM_ROWS, _CUM_ROWS), _CUM_ROWS)
        for d in range(2):
            gsc[d, sl, :] = _xdot(cum_mask[d], gsc[d, sl, :])
        return carry

    lax.fori_loop(0, T // _CUM_ROWS, cum_body, 0)

    ri = lax.broadcasted_iota(jnp.int32, (C, C), 0)
    ci = lax.broadcasted_iota(jnp.int32, (C, C), 1)
    eye = (ri == ci).astype(F32)

    def prepare(items):
        lows, decays = [], []
        for d, c in items:
            sl = pl.ds(pl.multiple_of(c * C, C), C)
            k, gc = kc[sl, :], gsc[d, sl, :]
            incl = (ci <= ri) if d == 0 else (ci >= ri)
            strict = (ci < ri) if d == 0 else (ci > ri)
            gr = jnp.transpose(gc)[0:1, :C]
            decay = jnp.where(incl, jnp.exp(jnp.where(incl, gc[:, :C] - gr, 0.0)), 0.0)
            lows.append(jnp.where(strict, _bdot_nt(k * bsc[d, sl, :], k) * decay, 0.0))
            decays.append(decay)
        ts = [eye - low for low in lows]
        ps = lows
        for step in range(5):
            dot = _dot3 if step < _SERIES_FINE else _bdot
            ps = [dot(p, p) for p in ps]
            ts = [t + dot(t, p) for t, p in zip(ts, ps)]
        for (d, c), t, decay in zip(items, ts, decays):
            sl = pl.ds(pl.multiple_of(c * C, C), C)
            q, k, gc, beta = qc[sl, :], kc[sl, :], gsc[d, sl, :], bsc[d, sl, :]
            eg = jnp.exp(gc)
            uw = _bdot(t, jnp.concatenate([vc[sl, :] * beta, k * beta * eg], axis=-1))
            last = gc[C - 1:C, :] if d == 0 else gc[0:1, :]
            wu = jnp.concatenate([uw[:, LANES:], uw[:, :LANES]], axis=-1).astype(BF16)
            kd = (k * jnp.exp(last - gc)).astype(BF16)
            attn = (_bdot_nt(q, k) * decay).astype(BF16)
            kdwu = lax.dot_general(kd, wu, (((0,), (0,)), ((), ())), preferred_element_type=F32)
            awu = jnp.dot(attn, wu, preferred_element_type=F32)
            mq0 = pl.multiple_of(c * _MQ_ROWS, _MQ_ROWS)
            mq_s[d, pl.ds(mq0, DN_DK), :] = kdwu[:, :LANES].astype(BF16)
            mq_s[d, pl.ds(mq0 + DN_DK, C), :] = (q * eg - awu[:, :LANES]).astype(BF16)
            b_s[d, pl.ds(pl.multiple_of(c * DN_DK, DN_DK), DN_DK), :] = kdwu[:, LANES:]
            osc[d, sl, :] = awu[:, LANES:]

    def prep_body(i, carry):
        prepare([(d, i * n_prep + j) for j in range(n_prep) for d in range(2)])
        return carry

    n_prep = min(n, _DN_PREP)
    lax.fori_loop(0, n // n_prep, prep_body, 0)

    def advance(d, c, S):
        sl = pl.ds(pl.multiple_of(c * C, C), C)
        ms = jnp.dot(mq_s[d, pl.ds(pl.multiple_of(c * _MQ_ROWS, _MQ_ROWS), _MQ_ROWS), :], S.astype(BF16),
                     preferred_element_type=F32)
        osc[d, sl, :] = osc[d, sl, :] + ms[DN_DK:]
        last = gsc[d, pl.ds(c * C + (C - 1 if d == 0 else 0), 1), :]
        return S * jnp.exp(last) - ms[:DN_DK] + b_s[d, pl.ds(pl.multiple_of(c * DN_DK, DN_DK), DN_DK), :]

    def body(i, carry):
        return advance(0, i, carry[0]), advance(1, n - 1 - i, carry[1])

    if has_s0:
        init = (s0_ref[0], s0_ref[1])
    else:
        init = (jnp.zeros((DN_DK, LANES), F32), jnp.zeros((DN_DK, LANES), F32))
    s_f, s_b = lax.fori_loop(0, n, body, init)
    if want_state:
        sfin_ref[0] = s_f
        sfin_ref[1] = s_b

    o = osc[0] + osc[1]
    o_ref[...] = (_rms(o, og_ref[...]) * _silu(z_ref[...].astype(F32))).astype(o_ref.dtype)


def _delta_heads(proj, ab, conv_w, a_log, dt_bias, onorm_g, T, n_seq, row0, s0):
    has_s0 = s0 is not None
    want_state = not has_s0

    def col(cb):
        return pl.BlockSpec((None, T, LANES), lambda s, h: (cb + h, row0 + s, 0))

    def cw(cb):
        return pl.BlockSpec((3, LANES), lambda s, h: (0, cb + h))

    smem = pl.BlockSpec(memory_space=pltpu.SMEM)
    in_specs = [col(_QA), col(_KA), col(_VA), col(_ZA),
                pl.BlockSpec((T, LANES), lambda s, h: (row0 + s, 0)),
                cw(0), cw(4), cw(8), smem, smem,
                pl.BlockSpec((1, LANES), lambda s, h: (0, 0))]
    args = [proj, proj, proj, proj, ab, conv_w, conv_w, conv_w, a_log, dt_bias,
            onorm_g.reshape(1, LANES)]
    if has_s0:
        in_specs.append(pl.BlockSpec((None, 2, None, DN_DK, LANES), lambda s, h: (s, 0, h, 0, 0)))
        args.append(s0)
    out_shape = [jax.ShapeDtypeStruct((DN_HEADS, n_seq * T, LANES), BF16)]
    out_specs = [pl.BlockSpec((None, T, LANES), lambda s, h: (h, s, 0))]
    if want_state:
        out_shape.append(jax.ShapeDtypeStruct((n_seq, 2, DN_HEADS, DN_DK, LANES), F32))
        out_specs.append(pl.BlockSpec((None, 2, None, DN_DK, LANES), lambda s, h: (s, 0, h, 0, 0)))
    res = pl.pallas_call(
        functools.partial(_dn_kernel, T=T, has_s0=has_s0, want_state=want_state),
        grid=(n_seq, DN_HEADS), in_specs=in_specs, out_specs=out_specs, out_shape=out_shape,
        scratch_shapes=[pltpu.VMEM((T, LANES), F32)] * 3
        + [pltpu.VMEM((2, T, LANES), F32)] * 3
        + [pltpu.VMEM((2, T // DN_CHUNK * DN_DK, LANES), F32),
           pltpu.VMEM((2, T // DN_CHUNK * _MQ_ROWS, LANES), BF16)],
        compiler_params=_params(("parallel", "parallel")), name="delta_heads_%d" % T,
    )(*args)
    return res if want_state else (res[0], None)


def _pair_queries(q, first):
    return jnp.concatenate([jnp.where(first, q, 0.0), jnp.where(first, 0.0, q)], axis=0).astype(BF16)


def _ctx_attn_kernel(q_ref, k_ref, v_ref, o_ref):
    first = lax.broadcasted_iota(jnp.int32, (SEQ, LANES), 1) < NA_HD
    qm = _pair_queries(q_ref[...] * (NA_HD ** -0.5), first)
    s = lax.dot_general(k_ref[...], qm, (((1,), (1,)), ((), ())), preferred_element_type=F32)
    e = jnp.exp(s - jnp.max(s, axis=0, keepdims=True))
    den = jnp.sum(e, axis=0, keepdims=True)
    o = lax.dot_general(e.astype(BF16), v_ref[...], (((0,), (0,)), ((), ())), preferred_element_type=F32)
    o = jnp.where(first, o[:SEQ], o[SEQ:])
    den_t = jnp.transpose(jnp.broadcast_to(den, (LANES, 2 * SEQ)))
    o_ref[...] = (o / jnp.where(first, den_t[:SEQ], den_t[SEQ:])).astype(o_ref.dtype)


def _ctx_attention(proj):
    def col(cb):
        return pl.BlockSpec((None, SEQ, LANES), lambda s, p: (cb + p, s, 0))

    return pl.pallas_call(
        _ctx_attn_kernel, grid=(BATCH, NA_HEADS // 2),
        in_specs=[col(_QB), col(_KB), col(_VB)],
        out_specs=pl.BlockSpec((None, SEQ, LANES), lambda s, p: (p, s, 0)),
        out_shape=jax.ShapeDtypeStruct((NA_HEADS // 2, N_CTX, LANES), BF16),
        compiler_params=_params(("parallel", "parallel")), name="ctx_attention",
    )(proj, proj, proj)


_NA_UNROLL = 4


def _na_kernel(q_ref, k_ref, v_ref, kc_ref, vc_ref, bias_ref, o_ref, kcb_scr, vcb_scr):
    rows = DEC_SEQ // GRID_W
    win = NA_ROWS * GRID_W
    scale = NA_HD ** -0.5
    dn_nt = (((1,), (1,)), ((), ()))
    dn_tn = (((0,), (0,)), ((), ()))

    kcb_scr[...] = kc_ref[...].astype(BF16)
    vcb_scr[...] = vc_ref[...].astype(BF16)
    first = lax.broadcasted_iota(jnp.int32, (GRID_W, LANES), 1) < NA_HD

    def body(it, carry):
        rr = [it * _NA_UNROLL + j for j in range(_NA_UNROLL)]
        rss = [jnp.clip(r - NA_ROWS // 2, 0, rows - NA_ROWS) for r in rr]
        qsls = [pl.ds(pl.multiple_of(r * GRID_W, GRID_W), GRID_W) for r in rr]
        wsls = [pl.ds(pl.multiple_of(rs * GRID_W, GRID_W), win) for rs in rss]
        qms, s_wins, s_ctxs = [], [], []
        for r, rs, qsl, wsl in zip(rr, rss, qsls, wsls):
            qm = _pair_queries(q_ref[qsl, :] * scale, first)
            bias = jnp.concatenate([bias_ref[NA_ROWS - 1 - (r - rs) + i] for i in range(NA_ROWS)], axis=0)
            s_wins.append(lax.dot_general(k_ref[wsl, :], qm, dn_nt, preferred_element_type=F32) + bias)
            s_ctxs.append(lax.dot_general(kcb_scr[...], qm, dn_nt, preferred_element_type=F32))
        ms = [jnp.maximum(jnp.max(sw, axis=0, keepdims=True), jnp.max(sc, axis=0, keepdims=True))
              for sw, sc in zip(s_wins, s_ctxs)]
        e_wins = [jnp.exp(sw - m) for sw, m in zip(s_wins, ms)]
        e_ctxs = [jnp.exp(sc - m) for sc, m in zip(s_ctxs, ms)]
        dens = [jnp.sum(ew, axis=0, keepdims=True) + jnp.sum(ec, axis=0, keepdims=True)
                for ew, ec in zip(e_wins, e_ctxs)]
        for qsl, wsl, ew, ec, den in zip(qsls, wsls, e_wins, e_ctxs, dens):
            o = (lax.dot_general(ew.astype(BF16), v_ref[wsl, :], dn_tn, preferred_element_type=F32)
                 + lax.dot_general(ec.astype(BF16), vcb_scr[...], dn_tn, preferred_element_type=F32))
            o = o / jnp.transpose(jnp.broadcast_to(den, (LANES, LANES)))
            o_ref[qsl, :] = jnp.where(first, o[:GRID_W], o[GRID_W:]).astype(o_ref.dtype)
        return carry

    lax.fori_loop(0, rows // _NA_UNROLL, body, 0)


def _na_bias_table(rpb):
    col = jnp.arange(GRID_W)
    cs = jnp.clip(col - NA_COLS // 2, 0, GRID_W - NA_COLS)
    col_ok = (col[None, :] >= cs[:, None]) & (col[None, :] < cs[:, None] + NA_COLS)
    dc = jnp.clip(col[None, :] - col[:, None] + NA_COLS - 1, 0, 2 * NA_COLS - 2)
    onehot = (dc.T[None, :, :] == jnp.arange(2 * NA_COLS - 1)[:, None, None]).astype(F32)
    t = jnp.einsum('hrd,dkq->hrkq', rpb.astype(F32), onehot, precision=lax.Precision.HIGHEST)
    t = jnp.where(col_ok.T[None, None], t, NEG_INF)
    t = t.reshape(NA_HEADS // 2, 2, 2 * NA_ROWS - 1, GRID_W, GRID_W)
    return jnp.concatenate([t[:, 0], t[:, 1]], axis=-1)


def _na_attention(proj, kctx, vctx, rpb):
    blk = N_CTX // DEC_SEQ

    def col(cb):
        return pl.BlockSpec((None, DEC_SEQ, LANES), lambda b, p: (cb + p, blk + b, 0))

    ctx = pl.BlockSpec((None, PAST_LEN, LANES), lambda b, p: (b, 0, p))
    return pl.pallas_call(
        _na_kernel, grid=(DEC_BATCH, NA_HEADS // 2),
        in_specs=[col(_QB), col(_KB), col(_VB), ctx, ctx,
                  pl.BlockSpec((None, 2 * NA_ROWS - 1, GRID_W, 2 * GRID_W), lambda b, p: (p, 0, 0, 0))],
        out_specs=pl.BlockSpec((None, DEC_SEQ, LANES), lambda b, p: (p, b, 0)),
        out_shape=jax.ShapeDtypeStruct((NA_HEADS // 2, N_LAT, LANES), BF16),
        scratch_shapes=[pltpu.VMEM((PAST_LEN, LANES), BF16), pltpu.VMEM((PAST_LEN, LANES), BF16)],
        compiler_params=_params(("parallel", "parallel")), name="na_attention",
    )(proj, proj, proj, kctx, vctx, _na_bias_table(rpb))


_LOGIT0 = N_EGROUPS
_R_E, _R_W, _R_RANK = 0, 2, 4


def _lane_min_where(mask, lane):
    return jnp.min(jnp.where(mask, lane, LANES), axis=-1, keepdims=True)


def _route_rows(lg, carry_ref, tri_ref):
    big = -3.0e38
    lane = lax.broadcasted_iota(jnp.int32, lg.shape, 1)
    is_g = lane < N_EGROUPS
    gmax = jnp.max(jnp.where(is_g, lg, big), axis=-1, keepdims=True)
    gsum = jnp.sum(jnp.where(is_g, jnp.exp(jnp.where(is_g, lg - gmax, 0.0)), 0.0), axis=-1, keepdims=True)
    pg_top = 1.0 / gsum
    g_idx = _lane_min_where(jnp.logical_and(is_g, lg == gmax), lane)
    in_g = jnp.logical_and(lane >= _LOGIT0, lax.shift_right_arithmetic(lane - _LOGIT0, 3) == g_idx)
    in_g = jnp.logical_and(in_g, lane < _LOGIT0 + N_EXPERTS)
    m1 = jnp.max(jnp.where(in_g, lg, big), axis=-1, keepdims=True)
    i1 = _lane_min_where(jnp.logical_and(in_g, lg == m1), lane)
    rest = jnp.logical_and(in_g, lane != i1)
    m2 = jnp.max(jnp.where(rest, lg, big), axis=-1, keepdims=True)
    i2 = _lane_min_where(jnp.logical_and(rest, lg == m2), lane)
    e2 = jnp.exp(m2 - m1)
    w1 = pg_top * (1.0 / (1.0 + e2))
    w2 = pg_top * (e2 / (1.0 + e2))
    hit1 = lane == i1
    hit2 = lane == i2
    picked = jnp.where(jnp.logical_or(hit1, hit2), 1.0, 0.0)
    before = jnp.dot(tri_ref[...], picked.astype(BF16), preferred_element_type=F32) + carry_ref[...]
    r1 = jnp.sum(jnp.where(hit1, before, 0.0), axis=-1, keepdims=True)
    r2 = jnp.sum(jnp.where(hit2, before, 0.0), axis=-1, keepdims=True)
    carry_ref[...] = carry_ref[...] + jnp.sum(picked, axis=0, keepdims=True)
    rec = jnp.zeros(lg.shape, F32)
    for ln, val in ((_R_E, (i1 - _LOGIT0).astype(F32)), (_R_E + 1, (i2 - _LOGIT0).astype(F32)),
                    (_R_W, w1), (_R_W + 1, w2), (_R_RANK, r1), (_R_RANK + 1, r2)):
        rec = jnp.where(lane == ln, val, rec)
    return rec


def _moe_input(xnew, first, tail_in, tail_out, tail_scr):
    g2_ref, sc2_ref, sh2_ref, wrh_ref, wrl_ref, br_ref = tail_in
    x_out, h_out, rec_out, cnt_out = tail_out
    tri_scr, carry_scr = tail_scr

    @pl.when(first)
    def _():
        tm = tri_scr.shape[0]
        r = lax.broadcasted_iota(jnp.int32, (tm, tm), 0)
        c = lax.broadcasted_iota(jnp.int32, (tm, tm), 1)
        tri_scr[...] = jnp.where(c < r, 1.0, 0.0).astype(BF16)
        carry_scr[...] = jnp.zeros(carry_scr.shape, F32)

    x_out[...] = xnew
    h = _rms(xnew, g2_ref[...]) * (1.0 + sc2_ref[...]) + sh2_ref[...]
    hh = h.astype(BF16)
    hl = (h - hh.astype(F32)).astype(BF16)
    h_out[...] = hh
    lg = (jnp.dot(hh, wrh_ref[...], preferred_element_type=F32)
          + jnp.dot(hh, wrl_ref[...], preferred_element_type=F32)
          + jnp.dot(hl, wrh_ref[...], preferred_element_type=F32) + br_ref[...])
    rec_out[...] = _route_rows(lg, carry_scr, tri_scr)
    cnt_out[...] = carry_scr[...]


def _even_out_kernel(oac_ref, obc_ref, oal_ref, obl_ref, x_ref, w_ref, gate_ref, *rest, ctx_tiles):
    tail_in, tail_out, (w_scr,), tail_scr = rest[:6], rest[6:10], rest[10:11], rest[11:]
    first = pl.program_id(0) == 0

    @pl.when(first)
    def _():
        w_scr[...] = w_ref[...].astype(BF16)

    is_ctx = pl.program_id(0) < ctx_tiles
    parts = [jnp.where(is_ctx, c_ref[hb], l_ref[hb])
             for c_ref, l_ref in ((oac_ref, oal_ref), (obc_ref, obl_ref)) for hb in range(DN_HEADS)]
    mix = jnp.concatenate(parts, axis=-1)
    out = jnp.dot(mix, w_scr[...], preferred_element_type=F32)
    _moe_input(x_ref[...] + gate_ref[...] * out, first, tail_in, tail_out, tail_scr)


def _tail_specs(tm):
    const = lambda shape: pl.BlockSpec(shape, lambda i: (0,) * len(shape))
    in_specs = [_mod_spec(2, tm), const((1, D_MODEL)), _mod_spec(4, tm), _mod_spec(3, tm),
                const((D_MODEL, LANES)), const((D_MODEL, LANES)), const((1, LANES))]
    out_specs = [pl.BlockSpec((tm, D_MODEL), lambda i: (i, 0)),
                 pl.BlockSpec((tm, D_MODEL), lambda i: (i, 0)),
                 pl.BlockSpec((tm, LANES), lambda i: (i, 0)),
                 const((1, LANES))]
    out_shape = [jax.ShapeDtypeStruct((N_TOK, D_MODEL), F32),
                 jax.ShapeDtypeStruct((N_TOK, D_MODEL), BF16),
                 jax.ShapeDtypeStruct((N_TOK, LANES), F32),
                 jax.ShapeDtypeStruct((1, LANES), F32)]
    scratch = [pltpu.VMEM((tm, tm), BF16), pltpu.VMEM((1, LANES), F32)]
    return in_specs, out_specs, out_shape, scratch


def _router_weights(w_rg, b_rg, w_re, b_re):
    pad = LANES - N_EGROUPS - N_EXPERTS
    w = jnp.concatenate([w_rg, w_re, jnp.zeros((D_MODEL, pad), F32)], axis=1)
    b = jnp.concatenate([b_rg, b_re, jnp.zeros((pad,), F32)]).reshape(1, LANES)
    hi = w.astype(BF16)
    lo = (w - hi.astype(F32)).astype(BF16)
    return hi, lo, b


def _even_out(oa_ctx, ob_ctx, oa_lat, ob_lat, x, w_out, mods, g2, router):
    tm = 512
    ctx_tiles = N_CTX // tm
    tail_in, out_specs, out_shape, tail_scr = _tail_specs(tm)
    ctxblk = pl.BlockSpec((DN_HEADS, tm, LANES), lambda i: (0, jnp.minimum(i, ctx_tiles - 1), 0))
    latblk = pl.BlockSpec((DN_HEADS, tm, LANES), lambda i: (0, jnp.maximum(i - ctx_tiles, 0), 0))
    return pl.pallas_call(
        functools.partial(_even_out_kernel, ctx_tiles=ctx_tiles), grid=(N_TOK // tm,),
        in_specs=[ctxblk, ctxblk, latblk, latblk, pl.BlockSpec((tm, D_MODEL), lambda i: (i, 0)),
                  pl.BlockSpec((D_MODEL, D_MODEL), lambda i: (0, 0))] + tail_in,
        out_specs=out_specs, out_shape=out_shape,
        scratch_shapes=[pltpu.VMEM((D_MODEL, D_MODEL), BF16)] + tail_scr,
        compiler_params=_params(("arbitrary",)), name="even_out",
    )(oa_ctx, ob_ctx, oa_lat, ob_lat, x, w_out, mods, g2.reshape(1, D_MODEL), mods, mods, *router)


def _gelu_tanh(x):
    return x * (0.5 * (1.0 + jnp.tanh(0.7978845608028654 * (x + 0.044715 * (x * x * x)))))


def _sgu_kernel(x_ref, g1_ref, sh1_ref, sc1_ref, win_ref, lng_ref, lnb_ref, ws_ref, bst_ref, wout_ref, gate_ref,
                *rest, tm):
    tail_in, tail_out, (v_scr, m_scr), tail_scr = rest[:6], rest[6:10], rest[10:12], rest[12:]
    first = pl.program_id(0) == 0
    x = x_ref[...]
    h = (_rms(x, g1_ref[...]) * (1.0 + sc1_ref[...]) + sh1_ref[...]).astype(BF16)

    v = _gelu_tanh(jnp.dot(h, win_ref[:, SG_W:], preferred_element_type=F32))
    mu = jnp.mean(v, axis=-1, keepdims=True)
    vc = v - mu
    var = jnp.mean(vc * vc, axis=-1, keepdims=True)
    v_scr[...] = (vc * lax.rsqrt(var + EPS) * lng_ref[...] + lnb_ref[...]).astype(BF16)

    for g in range(SG_GROUPS):
        cs = slice(g * SG_GW, (g + 1) * SG_GW)
        u = _gelu_tanh(jnp.dot(h, win_ref[:, cs], preferred_element_type=F32))
        w_sp = ws_ref[g].astype(BF16)
        for c in range(tm // SG_CHUNK):
            rs = slice(c * SG_CHUNK, (c + 1) * SG_CHUNK)
            sp = jnp.dot(w_sp, v_scr[rs, cs], preferred_element_type=F32) + bst_ref[:, g:g + 1]
            m_scr[rs, cs] = (u[rs] * sp).astype(BF16)
    out = jnp.dot(m_scr[...], wout_ref[...], preferred_element_type=F32)
    _moe_input(x + gate_ref[...] * out, first, tail_in, tail_out, tail_scr)


def _sgu_layer(x, mods, g1, w_in, ln_g, ln_b, w_s, b_s, w_out, g2, router):
    tm = 512
    tail_in, out_specs, out_shape, tail_scr = _tail_specs(tm)
    const = lambda shape: pl.BlockSpec(shape, lambda i: (0,) * len(shape))
    held = lambda shape: pl.BlockSpec(shape, lambda i: (0,) * len(shape), pipeline_mode=pl.Buffered(1))
    return pl.pallas_call(
        functools.partial(_sgu_kernel, tm=tm), grid=(N_TOK // tm,),
        in_specs=[pl.BlockSpec((tm, D_MODEL), lambda i: (i, 0)),
                  const((1, D_MODEL)), _mod_spec(0, tm), _mod_spec(1, tm),
                  held((D_MODEL, 2 * SG_W)), const((1, SG_W)), const((1, SG_W)),
                  const((SG_GROUPS, SG_CHUNK, SG_CHUNK)), const((SG_CHUNK, SG_GROUPS)),
                  held((SG_W, D_MODEL))] + tail_in,
        out_specs=out_specs, out_shape=out_shape,
        scratch_shapes=[pltpu.VMEM((tm, SG_W), BF16), pltpu.VMEM((tm, SG_W), BF16)] + tail_scr,
        compiler_params=_params(("arbitrary",)), name="sgu_layer",
    )(x, g1.reshape(1, D_MODEL), mods, mods, w_in.astype(BF16), ln_g.reshape(1, SG_W), ln_b.reshape(1, SG_W),
      w_s, b_s.T, w_out.astype(BF16), mods, g2.reshape(1, D_MODEL), mods, mods, *router)


def _plan(rec, cnt):
    e_idx = rec[:, _R_E:_R_E + 2].astype(jnp.int32)
    rank = rec[:, _R_RANK:_R_RANK + 2].astype(jnp.int32)
    counts = cnt[0, _LOGIT0:_LOGIT0 + N_EXPERTS].astype(jnp.int32)
    padded = (counts + MOE_BLK - 1) // MOE_BLK * MOE_BLK
    pad_end = jnp.cumsum(padded)
    pad_start = pad_end - padded
    hit = e_idx[:, :, None] == jnp.arange(N_EXPERTS, dtype=jnp.int32)[None, None, :]
    dest = jnp.sum(jnp.where(hit, pad_start[None, None, :], 0), axis=-1) + rank
    blk0 = jnp.arange(MOE_NBLK, dtype=jnp.int32) * MOE_BLK
    blk_e = jnp.minimum(jnp.sum((pad_end[None, :] <= blk0[:, None]).astype(jnp.int32), axis=-1),
                        N_EXPERTS - 1)
    n_used = (pad_end[-1] // MOE_BLK).astype(jnp.int32).reshape(1)
    owns = counts > 0
    slot_of = (jnp.cumsum(owns.astype(jnp.int32)) - 1) % 2
    ids = jnp.arange(N_EXPERTS, dtype=jnp.int32)
    later = jnp.logical_and(owns[None, :], ids[None, :] > ids[:, None])
    next_of = jnp.min(jnp.where(later, ids[None, :], N_EXPERTS), axis=-1)
    next_of = jnp.where(next_of == N_EXPERTS, -1, next_of)
    return dest, blk_e, n_used, slot_of[blk_e], next_of[blk_e]


def _expert_kernel(blk_e_ref, n_used_ref, slot_ref, next_ref, x_ref, wg_hbm, wu_hbm, wd_hbm, o_ref,
                   wg_buf, wu_buf, wd_buf, wg_scr, wu_scr, wd_scr, sems, *, layer):
    j = pl.program_id(0)
    e = blk_e_ref[j]
    slot = slot_ref[j]
    fresh = jnp.logical_or(j == 0, e != blk_e_ref[jnp.maximum(j - 1, 0)])
    live = j < n_used_ref[0]

    def copies(expert, s):
        return (pltpu.make_async_copy(wg_hbm.at[layer, expert], wg_buf.at[s], sems.at[s, 0]),
                pltpu.make_async_copy(wu_hbm.at[layer, expert], wu_buf.at[s], sems.at[s, 1]),
                pltpu.make_async_copy(wd_hbm.at[layer, expert], wd_buf.at[s], sems.at[s, 2]))

    @pl.when(j == 0)
    def _():
        for cp in copies(e, slot):
            cp.start()

    @pl.when(jnp.logical_and(fresh, live))
    def _():
        for cp in copies(e, slot):
            cp.wait()
        nxt = next_ref[j]

        @pl.when(nxt >= 0)
        def _():
            for cp in copies(nxt, 1 - slot):
                cp.start()

        wg_scr[...] = wg_buf[slot].astype(BF16)
        wu_scr[...] = wu_buf[slot].astype(BF16)
        wd_scr[...] = wd_buf[slot].astype(BF16)

    @pl.when(live)
    def _():
        x = x_ref[...]
        gt = jnp.dot(x, wg_scr[...], preferred_element_type=F32)
        up = jnp.dot(x, wu_scr[...], preferred_element_type=F32)
        hb = (_silu(gt) * up).astype(BF16)
        o_ref[...] = jnp.dot(hb, wd_scr[...], preferred_element_type=F32).astype(o_ref.dtype)

    @pl.when(jnp.logical_not(live))
    def _():
        o_ref[...] = jnp.zeros(o_ref.shape, o_ref.dtype)


def _experts(x_pad, blk_e, n_used, slot, nxt, w_gate, w_up, w_down, layer):
    hbm = pl.BlockSpec(memory_space=pl.ANY)
    grid_spec = pltpu.PrefetchScalarGridSpec(
        num_scalar_prefetch=4, grid=(MOE_NBLK,),
        in_specs=[pl.BlockSpec((MOE_BLK, D_MODEL), lambda j, *_: (j, 0)), hbm, hbm, hbm],
        out_specs=pl.BlockSpec((MOE_BLK, D_MODEL), lambda j, *_: (j, 0)),
        scratch_shapes=[pltpu.VMEM((2, D_MODEL, D_EXPERT), F32), pltpu.VMEM((2, D_MODEL, D_EXPERT), F32),
                        pltpu.VMEM((2, D_EXPERT, D_MODEL), F32),
                        pltpu.VMEM((D_MODEL, D_EXPERT), BF16), pltpu.VMEM((D_MODEL, D_EXPERT), BF16),
                        pltpu.VMEM((D_EXPERT, D_MODEL), BF16), pltpu.SemaphoreType.DMA((2, 3))])
    return pl.pallas_call(
        functools.partial(_expert_kernel, layer=layer), grid_spec=grid_spec,
        out_shape=jax.ShapeDtypeStruct((MOE_NBLK * MOE_BLK, D_MODEL), BF16),
        compiler_params=_params(("arbitrary",)), name="experts",
    )(blk_e, n_used, slot, nxt, x_pad, w_gate, w_up, w_down)


def _combine_kernel(x_ref, ya_ref, yb_ref, rec_ref, gate_ref, fg_ref, o_ref, *, final):
    rec = rec_ref[...]
    y = (rec[:, _R_W:_R_W + 1] * ya_ref[...].astype(F32)
         + rec[:, _R_W + 1:_R_W + 2] * yb_ref[...].astype(F32))
    xn = x_ref[...] + gate_ref[...] * y
    o_ref[...] = _rms(xn, fg_ref[...]) if final else xn


def _combine(x, ya, yb, rec, mods, final_g, final):
    tm = 512
    blk = pl.BlockSpec((tm, D_MODEL), lambda i: (i, 0))
    return pl.pallas_call(
        functools.partial(_combine_kernel, final=final), grid=(N_TOK // tm,),
        in_specs=[blk, blk, blk, pl.BlockSpec((tm, LANES), lambda i: (i, 0)), _mod_spec(5, tm),
                  pl.BlockSpec((1, D_MODEL), lambda i: (0, 0))],
        out_specs=blk, out_shape=jax.ShapeDtypeStruct((N_TOK, D_MODEL), F32),
        compiler_params=_params(("parallel",)), name="moe_combine",
    )(x, ya, yb, rec, mods, final_g.reshape(1, D_MODEL))


def _moe(x, h, rec, cnt, mods, w_gate, w_up, w_down, layer, final_g, final):
    dest, blk_e, n_used, slot, nxt = _plan(rec, cnt)
    tok = jnp.arange(2 * N_TOK, dtype=jnp.int32) // 2
    row_tok = jnp.zeros((MOE_NBLK * MOE_BLK,), jnp.int32).at[dest.reshape(-1)].set(tok, unique_indices=True)
    y_pad = _experts(h[row_tok], blk_e, n_used, slot, nxt, w_gate, w_up, w_down, layer)
    return _combine(x, y_pad[dest[:, 0]], y_pad[dest[:, 1]], rec, mods, final_g, final)


def kernel(x_prompt, x_sample, c, cache_k, cache_v, state_delta, c_ctx, ada_w, ada_b, norm1_g, norm2_g, final_g,
           ev_w_in, ev_w_out, ev_conv_w, ev_a_log, ev_dt_bias, ev_onorm_g, ev_rpb, od_w_in, od_ln_g, od_ln_b,
           od_w_s, od_b_s, od_w_out, moe_w_rg, moe_b_rg, moe_w_re, moe_b_re, moe_w_gate, moe_w_up, moe_w_down):
    x = jnp.concatenate([x_prompt.reshape(N_CTX, D_MODEL), x_sample.reshape(N_LAT, D_MODEL)], axis=0)
    cond = jnp.concatenate([c_ctx[None, :], c, jnp.zeros((N_COND - 1 - DEC_BATCH, D_MODEL), F32)], axis=0)
    mods_all = _ada_mods(cond, ada_w, ada_b)
    kctx_all = cache_k.reshape(DEC_BATCH, -1, PAST_LEN, NA_HEADS * NA_HD)
    vctx_all = cache_v.reshape(DEC_BATCH, -1, PAST_LEN, NA_HEADS * NA_HD)

    ks, vs, ss = [], [], []
    for l in range(DEPTH):
        mods = mods_all[l]
        router = _router_weights(moe_w_rg[l], moe_b_rg[l], moe_w_re[l], moe_b_re[l])
        if l % 2 == 0:
            e = l // 2
            proj, ab, kv = _even_proj(x, mods, norm1_g[l], ev_w_in[e])
            dn = (proj, ab, ev_conv_w[e], ev_a_log[e], ev_dt_bias[e], ev_onorm_g[e])
            oa_ctx, s_fin = _delta_heads(*dn, SEQ, BATCH, 0, None)
            oa_lat, _ = _delta_heads(*dn, DEC_SEQ, DEC_BATCH, N_CTX // DEC_SEQ, state_delta[:, e])
            ob_ctx = _ctx_attention(proj)
            ob_lat = _na_attention(proj, kctx_all[:, e], vctx_all[:, e], ev_rpb[e])
            x, h, rec, cnt = _even_out(oa_ctx, ob_ctx, oa_lat, ob_lat, x, ev_w_out[e], mods, norm2_g[l],
                                       router)
            na_w = NA_HEADS * NA_HD
            ks.append(kv[:N_CTX, :na_w].reshape(BATCH, SEQ, NA_HEADS, NA_HD))
            vs.append(kv[:N_CTX, na_w:].reshape(BATCH, SEQ, NA_HEADS, NA_HD))
            ss.append(s_fin)
        else:
            o = l // 2
            x, h, rec, cnt = _sgu_layer(x, mods, norm1_g[l], od_w_in[o], od_ln_g[o], od_ln_b[o], od_w_s[o],
                                        od_b_s[o], od_w_out[o], norm2_g[l], router)
        x = _moe(x, h, rec, cnt, mods, moe_w_gate, moe_w_up, moe_w_down, l, final_g, l == DEPTH - 1)

    y_prompt = x[:N_CTX].reshape(BATCH, SEQ, D_MODEL)
    y_sample = x[N_CTX:].reshape(DEC_BATCH, DEC_SEQ, D_MODEL)
    return (y_prompt, y_sample, jnp.stack(ks, axis=1), jnp.stack(vs, axis=1), jnp.stack(ss, axis=1))
```

```python
import functools

import jax
import jax.numpy as jnp
from jax import lax
from jax.experimental import pallas as pl
from jax.experimental.pallas import tpu as pltpu
from jax.experimental.pallas import tpu_sc as plsc

F32 = jnp.float32
BF16 = jnp.bfloat16

D_MODEL = 1024
BATCH = 16
SEQ = 256
DEPTH = 4
DEC_BATCH = 4
DEC_SEQ = 2048
PAST_LEN = 512
GRID_W = 64
EPS = 1e-6
NEG_INF = -1e30

DN_HEADS = 4
DN_DK = 128
DN_CHUNK = 64
NA_HEADS = 8
NA_HD = 64
NA_ROWS = 8
NA_COLS = 16
SG_CHUNK = 128
SG_GROUPS = 8
SG_W = 2 * D_MODEL
SG_GW = SG_W // SG_GROUPS
N_EGROUPS = 4
EXP_PER_GROUP = 8
N_EXPERTS = 32
D_EXPERT = 512

N_CTX = BATCH * SEQ
N_LAT = DEC_BATCH * DEC_SEQ
N_TOK = N_CTX + N_LAT
N_COND = 8
PROJ_W = 4096
LANES = 128
MOE_BLK = 256
MOE_NBLK = -(-(2 * N_TOK + N_EXPERTS * (MOE_BLK - 1)) // MOE_BLK)
VMEM_LIMIT = 56 * 1024 * 1024

_QA, _KA, _VA, _ZA, _QB, _KB, _VB = 0, 4, 8, 12, 16, 20, 24


def _params(sem):
    return pltpu.CompilerParams(dimension_semantics=sem, vmem_limit_bytes=VMEM_LIMIT)


def _bdot(a, b):
    return jnp.dot(a.astype(BF16), b.astype(BF16), preferred_element_type=F32)


def _bdot_nt(a, b):
    return lax.dot_general(a.astype(BF16), b.astype(BF16), (((1,), (1,)), ((), ())),
                           preferred_element_type=F32)


def _bdot_tn(a, b):
    return lax.dot_general(a.astype(BF16), b.astype(BF16), (((0,), (0,)), ((), ())),
                           preferred_element_type=F32)


def _split3(a):
    p0 = a.astype(BF16)
    r = a - p0.astype(F32)
    p1 = r.astype(BF16)
    p2 = (r - p1.astype(F32)).astype(BF16)
    return p0, p1, p2


def _dot3(a, b):
    ah = a.astype(BF16)
    al = (a - ah.astype(F32)).astype(BF16)
    bh = b.astype(BF16)
    bl = (b - bh.astype(F32)).astype(BF16)
    return (jnp.dot(ah, bh, preferred_element_type=F32) + jnp.dot(ah, bl, preferred_element_type=F32)
            + jnp.dot(al, bh, preferred_element_type=F32))


def _mask_bf16(m01):
    return jnp.where(m01, 1.0, 0.0).astype(BF16)


def _xdot(m01, a):
    m = _mask_bf16(m01)
    p0, p1, p2 = _split3(a)
    return (jnp.dot(m, p0, preferred_element_type=F32) + jnp.dot(m, p1, preferred_element_type=F32)
            + jnp.dot(m, p2, preferred_element_type=F32))


def _xdot_nt(m01, a):
    m = _mask_bf16(m01)
    dn = (((1,), (1,)), ((), ()))
    p0, p1, p2 = _split3(a)
    return (lax.dot_general(m, p0, dn, preferred_element_type=F32)
            + lax.dot_general(m, p1, dn, preferred_element_type=F32)
            + lax.dot_general(m, p2, dn, preferred_element_type=F32))


def _xdot_r(a, m01):
    m = _mask_bf16(m01)
    p0, p1, p2 = _split3(a)
    return (jnp.dot(p0, m, preferred_element_type=F32) + jnp.dot(p1, m, preferred_element_type=F32)
            + jnp.dot(p2, m, preferred_element_type=F32))


def _sigmoid(x):
    return 0.5 * jnp.tanh(0.5 * x) + 0.5


def _silu(x):
    return x * _sigmoid(x)


def _rms(x, g):
    return x * lax.rsqrt(jnp.mean(x * x, axis=-1, keepdims=True) + EPS) * g


def _cond_index(row):
    return jnp.where(row < N_CTX, 0, 1 + (row - N_CTX) // DEC_SEQ)


def _mod_spec(k, tm):
    return pl.BlockSpec((None, None, 1, D_MODEL), lambda i, *_: (_cond_index(i * tm), k, 0, 0))


def _ada_kernel(c_ref, w_ref, b_ref, o_ref):
    o_ref[...] = _bdot(_silu(c_ref[...]), w_ref[...]) + b_ref[...]


def _ada_mods(cond, ada_w, ada_b):
    tn = 1536
    out = pl.pallas_call(
        _ada_kernel, grid=(DEPTH, 6 * D_MODEL // tn),
        in_specs=[pl.BlockSpec((N_COND, D_MODEL), lambda l, j: (0, 0)),
                  pl.BlockSpec((None, D_MODEL, tn), lambda l, j: (l, 0, j)),
                  pl.BlockSpec((None, 1, tn), lambda l, j: (l, 0, j))],
        out_specs=pl.BlockSpec((None, N_COND, tn), lambda l, j: (l, 0, j)),
        out_shape=jax.ShapeDtypeStruct((DEPTH, N_COND, 6 * D_MODEL), F32),
        compiler_params=_params(("parallel", "parallel")), name="ada_mods",
    )(cond, ada_w, ada_b.reshape(DEPTH, 1, 6 * D_MODEL))
    return out.reshape(DEPTH, N_COND, 6, 1, D_MODEL)


_EV_TN = 512
_EV_W = 7 * DN_HEADS * LANES
_KV_COL0 = _KB * LANES


def _even_proj_kernel(x_ref, g_ref, sh_ref, sc_ref, w_ref, wab_ref, o_ref, ab_ref, kv_ref):
    h = (_rms(x_ref[...], g_ref[...]) * (1.0 + sc_ref[...]) + sh_ref[...]).astype(BF16)
    ab_ref[...] = jnp.dot(h, wab_ref[...], preferred_element_type=F32)
    for j in range(_EV_W // _EV_TN):
        c0 = j * _EV_TN
        y = jnp.dot(h, w_ref[:, c0:c0 + _EV_TN], preferred_element_type=F32)
        for c in range(_EV_TN // LANES):
            o_ref[c0 // LANES + c] = y[:, c * LANES:(c + 1) * LANES].astype(BF16)
        if c0 >= _KV_COL0:
            kv_ref[:, c0 - _KV_COL0:c0 - _KV_COL0 + _EV_TN] = y


def _even_proj(x, mods, g, w_in):
    tm = 512
    n_ab = 4 * DN_HEADS
    ab0 = 4 * DN_HEADS * DN_DK
    w_main = jnp.concatenate([w_in[:, :ab0], w_in[:, ab0 + n_ab:]], axis=1).astype(BF16)
    w_ab = jnp.concatenate([w_in[:, ab0:ab0 + n_ab], jnp.zeros((D_MODEL, LANES - n_ab), F32)],
                           axis=1).astype(BF16)
    held = lambda shape: pl.BlockSpec(shape, lambda i: (0,) * len(shape), pipeline_mode=pl.Buffered(1))
    return pl.pallas_call(
        _even_proj_kernel, grid=(N_TOK // tm,),
        in_specs=[pl.BlockSpec((tm, D_MODEL), lambda i: (i, 0)),
                  pl.BlockSpec((1, D_MODEL), lambda i: (0, 0)),
                  _mod_spec(0, tm), _mod_spec(1, tm),
                  held((D_MODEL, _EV_W)), held((D_MODEL, LANES))],
        out_specs=[pl.BlockSpec((_EV_W // LANES, tm, LANES), lambda i: (0, i, 0)),
                   pl.BlockSpec((tm, LANES), lambda i: (i, 0)),
                   pl.BlockSpec((tm, 2 * NA_HEADS * NA_HD), lambda i: (i, 0))],
        out_shape=[jax.ShapeDtypeStruct((_EV_W // LANES, N_TOK, LANES), BF16),
                   jax.ShapeDtypeStruct((N_TOK, LANES), F32),
                   jax.ShapeDtypeStruct((N_TOK, 2 * NA_HEADS * NA_HD), F32)],
        compiler_params=_params(("parallel",)), name="even_proj",
    )(x, g.reshape(1, D_MODEL), mods, mods, w_main, w_ab)


_CHUNK_SHIFT = DN_CHUNK.bit_length() - 1
_CUM_ROWS = 256
_DN_PREP = 8
_SERIES_FINE = 3
_MQ_ROWS = DN_DK + DN_CHUNK


def _dn_kernel(*refs, T, has_s0, want_state):
    it = iter(refs)
    q_ref, k_ref, v_ref, z_ref, ab_ref = (next(it) for _ in range(5))
    cwq_ref, cwk_ref, cwv_ref, alog_ref, dtb_ref, og_ref = (next(it) for _ in range(6))
    s0_ref = next(it) if has_s0 else None
    o_ref = next(it)
    sfin_ref = next(it) if want_state else None
    qc, kc, vc, gsc, bsc, osc, b_s, mq_s = (next(it) for _ in range(8))

    C = DN_CHUNK
    n = T // C
    hd = pl.program_id(1)

    row = lax.broadcasted_iota(jnp.int32, (T, 1), 0)

    def conv(x_ref, cw_ref):
        x = x_ref[...].astype(F32)
        xp = jnp.where(row == 0, 0.0, pltpu.roll(x, 1, 0))
        xn = jnp.where(row == T - 1, 0.0, pltpu.roll(x, T - 1, 0))
        return _silu(cw_ref[0:1, :] * xp + cw_ref[1:2, :] * x + cw_ref[2:3, :] * xn)

    def l2n(x):
        return x * lax.rsqrt(jnp.sum(x * x, axis=-1, keepdims=True) + EPS)

    qc[...] = l2n(conv(q_ref, cwq_ref)) * (DN_DK ** -0.5)
    kc[...] = l2n(conv(k_ref, cwk_ref))
    vc[...] = conv(v_ref, cwv_ref)

    ab = ab_ref[...]
    sel_r = lax.broadcasted_iota(jnp.int32, (LANES, LANES), 0)
    for d in range(2):
        alpha = _xdot_r(ab, sel_r == d * DN_HEADS + hd)
        blog = _xdot_r(ab, sel_r == 2 * DN_HEADS + d * DN_HEADS + hd)
        x = alpha + dtb_ref[d, hd]
        sp = jnp.maximum(x, 0.0) + jnp.log1p(jnp.exp(-jnp.abs(x)))
        a = jnp.exp(jnp.full((1, LANES), alog_ref[d, hd], F32))
        gsc[d] = -a * sp
        bsc[d] = _sigmoid(blog)

    pr = lax.broadcasted_iota(jnp.int32, (_CUM_ROWS, _CUM_ROWS), 0)
    pc = lax.broadcasted_iota(jnp.int32, (_CUM_ROWS, _CUM_ROWS), 1)
    same = lax.shift_right_logical(pr, _CHUNK_SHIFT) == lax.shift_right_logical(pc, _CHUNK_SHIFT)
    cum_mask = (jnp.logical_and(same, pc <= pr), jnp.logical_and(same, pc >= pr))

    def cum_body(i, carry):
        sl = pl.ds(pl.multiple_of(i * _CUM_ROWS, _CUM_ROWS), _CUM_ROWS)
        for d in range(2):
            gsc[d, sl, :] = _xdot(cum_mask[d], gsc[d, sl, :])
        return carry

    lax.fori_loop(0, T // _CUM_ROWS, cum_body, 0)

    ri = lax.broadcasted_iota(jnp.int32, (C, C), 0)
    ci = lax.broadcasted_iota(jnp.int32, (C, C), 1)
    eye = (ri == ci).astype(F32)

    def prepare(items):
        lows, decays = [], []
        for d, c in items:
            sl = pl.ds(pl.multiple_of(c * C, C), C)
            k, gc = kc[sl, :], gsc[d, sl, :]
            incl = (ci <= ri) if d == 0 else (ci >= ri)
            strict = (ci < ri) if d == 0 else (ci > ri)
            gr = jnp.transpose(gc)[0:1, :C]
            decay = jnp.where(incl, jnp.exp(jnp.where(incl, gc[:, :C] - gr, 0.0)), 0.0)
            lows.append(jnp.where(strict, _bdot_nt(k * bsc[d, sl, :], k) * decay, 0.0))
            decays.append(decay)
        ts = [eye - low for low in lows]
        ps = lows
        for step in range(5):
            dot = _dot3 if step < _SERIES_FINE else _bdot
            ps = [dot(p, p) for p in ps]
            ts = [t + dot(t, p) for t, p in zip(ts, ps)]
        for (d, c), t, decay in zip(items, ts, decays):
            sl = pl.ds(pl.multiple_of(c * C, C), C)
            q, k, gc, beta = qc[sl, :], kc[sl, :], gsc[d, sl, :], bsc[d, sl, :]
            eg = jnp.exp(gc)
            uw = _bdot(t, jnp.concatenate([vc[sl, :] * beta, k * beta * eg], axis=-1))
            last = gc[C - 1:C, :] if d == 0 else gc[0:1, :]
            wu = jnp.concatenate([uw[:, LANES:], uw[:, :LANES]], axis=-1).astype(BF16)
            kd = (k * jnp.exp(last - gc)).astype(BF16)
            attn = (_bdot_nt(q, k) * decay).astype(BF16)
            kdwu = lax.dot_general(kd, wu, (((0,), (0,)), ((), ())), preferred_element_type=F32)
            awu = jnp.dot(attn, wu, preferred_element_type=F32)
            mq0 = pl.multiple_of(c * _MQ_ROWS, _MQ_ROWS)
            mq_s[d, pl.ds(mq0, DN_DK), :] = kdwu[:, :LANES].astype(BF16)
            mq_s[d, pl.ds(mq0 + DN_DK, C), :] = (q * eg - awu[:, :LANES]).astype(BF16)
            b_s[d, pl.ds(pl.multiple_of(c * DN_DK, DN_DK), DN_DK), :] = kdwu[:, LANES:]
            osc[d, sl, :] = awu[:, LANES:]

    def prep_body(i, carry):
        prepare([(d, i * n_prep + j) for j in range(n_prep) for d in range(2)])
        return carry

    n_prep = min(n, _DN_PREP)
    lax.fori_loop(0, n // n_prep, prep_body, 0)

    def advance(d, c, S):
        sl = pl.ds(pl.multiple_of(c * C, C), C)
        ms = jnp.dot(mq_s[d, pl.ds(pl.multiple_of(c * _MQ_ROWS, _MQ_ROWS), _MQ_ROWS), :], S.astype(BF16),
                     preferred_element_type=F32)
        osc[d, sl, :] = osc[d, sl, :] + ms[DN_DK:]
        last = gsc[d, pl.ds(c * C + (C - 1 if d == 0 else 0), 1), :]
        return S * jnp.exp(last) - ms[:DN_DK] + b_s[d, pl.ds(pl.multiple_of(c * DN_DK, DN_DK), DN_DK), :]

    def body(i, carry):
        return advance(0, i, carry[0]), advance(1, n - 1 - i, carry[1])

    if has_s0:
        init = (s0_ref[0], s0_ref[1])
    else:
        init = (jnp.zeros((DN_DK, LANES), F32), jnp.zeros((DN_DK, LANES), F32))
    s_f, s_b = lax.fori_loop(0, n, body, init)
    if want_state:
        sfin_ref[0] = s_f
        sfin_ref[1] = s_b

    o = osc[0] + osc[1]
    o_ref[...] = (_rms(o, og_ref[...]) * _silu(z_ref[...].astype(F32))).astype(o_ref.dtype)


def _delta_heads(proj, ab, conv_w, a_log, dt_bias, onorm_g, T, n_seq, row0, s0):
    has_s0 = s0 is not None
    want_state = not has_s0

    def col(cb):
        return pl.BlockSpec((None, T, LANES), lambda s, h: (cb + h, row0 + s, 0))

    def cw(cb):
        return pl.BlockSpec((3, LANES), lambda s, h: (0, cb + h))

    smem = pl.BlockSpec(memory_space=pltpu.SMEM)
    in_specs = [col(_QA), col(_KA), col(_VA), col(_ZA),
                pl.BlockSpec((T, LANES), lambda s, h: (row0 + s, 0)),
                cw(0), cw(4), cw(8), smem, smem,
                pl.BlockSpec((1, LANES), lambda s, h: (0, 0))]
    args = [proj, proj, proj, proj, ab, conv_w, conv_w, conv_w, a_log, dt_bias,
            onorm_g.reshape(1, LANES)]
    if has_s0:
        in_specs.append(pl.BlockSpec((None, 2, None, DN_DK, LANES), lambda s, h: (s, 0, h, 0, 0)))
        args.append(s0)
    out_shape = [jax.ShapeDtypeStruct((DN_HEADS, n_seq * T, LANES), BF16)]
    out_specs = [pl.BlockSpec((None, T, LANES), lambda s, h: (h, s, 0))]
    if want_state:
        out_shape.append(jax.ShapeDtypeStruct((n_seq, 2, DN_HEADS, DN_DK, LANES), F32))
        out_specs.append(pl.BlockSpec((None, 2, None, DN_DK, LANES), lambda s, h: (s, 0, h, 0, 0)))
    res = pl.pallas_call(
        functools.partial(_dn_kernel, T=T, has_s0=has_s0, want_state=want_state),
        grid=(n_seq, DN_HEADS), in_specs=in_specs, out_specs=out_specs, out_shape=out_shape,
        scratch_shapes=[pltpu.VMEM((T, LANES), F32)] * 3
        + [pltpu.VMEM((2, T, LANES), F32)] * 3
        + [pltpu.VMEM((2, T // DN_CHUNK * DN_DK, LANES), F32),
           pltpu.VMEM((2, T // DN_CHUNK * _MQ_ROWS, LANES), BF16)],
        compiler_params=_params(("parallel", "parallel")), name="delta_heads_%d" % T,
    )(*args)
    return res if want_state else (res[0], None)


def _pair_queries(q, first):
    return jnp.concatenate([jnp.where(first, q, 0.0), jnp.where(first, 0.0, q)], axis=0).astype(BF16)


def _ctx_attn_kernel(q_ref, k_ref, v_ref, o_ref):
    first = lax.broadcasted_iota(jnp.int32, (SEQ, LANES), 1) < NA_HD
    qm = _pair_queries(q_ref[...] * (NA_HD ** -0.5), first)
    s = lax.dot_general(k_ref[...], qm, (((1,), (1,)), ((), ())), preferred_element_type=F32)
    e = jnp.exp(s - jnp.max(s, axis=0, keepdims=True))
    den = jnp.sum(e, axis=0, keepdims=True)
    o = lax.dot_general(e.astype(BF16), v_ref[...], (((0,), (0,)), ((), ())), preferred_element_type=F32)
    o = jnp.where(first, o[:SEQ], o[SEQ:])
    den_t = jnp.transpose(jnp.broadcast_to(den, (LANES, 2 * SEQ)))
    o_ref[...] = (o / jnp.where(first, den_t[:SEQ], den_t[SEQ:])).astype(o_ref.dtype)


def _ctx_attention(proj):
    def col(cb):
        return pl.BlockSpec((None, SEQ, LANES), lambda s, p: (cb + p, s, 0))

    return pl.pallas_call(
        _ctx_attn_kernel, grid=(BATCH, NA_HEADS // 2),
        in_specs=[col(_QB), col(_KB), col(_VB)],
        out_specs=pl.BlockSpec((None, SEQ, LANES), lambda s, p: (p, s, 0)),
        out_shape=jax.ShapeDtypeStruct((NA_HEADS // 2, N_CTX, LANES), BF16),
        compiler_params=_params(("parallel", "parallel")), name="ctx_attention",
    )(proj, proj, proj)


_NA_UNROLL = 4


def _na_kernel(q_ref, k_ref, v_ref, kc_ref, vc_ref, bias_ref, o_ref, kcb_scr, vcb_scr):
    rows = DEC_SEQ // GRID_W
    win = NA_ROWS * GRID_W
    scale = NA_HD ** -0.5
    dn_nt = (((1,), (1,)), ((), ()))
    dn_tn = (((0,), (0,)), ((), ()))

    kcb_scr[...] = kc_ref[...].astype(BF16)
    vcb_scr[...] = vc_ref[...].astype(BF16)
    first = lax.broadcasted_iota(jnp.int32, (GRID_W, LANES), 1) < NA_HD

    def body(it, carry):
        rr = [it * _NA_UNROLL + j for j in range(_NA_UNROLL)]
        rss = [jnp.clip(r - NA_ROWS // 2, 0, rows - NA_ROWS) for r in rr]
        qsls = [pl.ds(pl.multiple_of(r * GRID_W, GRID_W), GRID_W) for r in rr]
        wsls = [pl.ds(pl.multiple_of(rs * GRID_W, GRID_W), win) for rs in rss]
        qms, s_wins, s_ctxs = [], [], []
        for r, rs, qsl, wsl in zip(rr, rss, qsls, wsls):
            qm = _pair_queries(q_ref[qsl, :] * scale, first)
            bias = jnp.concatenate([bias_ref[NA_ROWS - 1 - (r - rs) + i] for i in range(NA_ROWS)], axis=0)
            s_wins.append(lax.dot_general(k_ref[wsl, :], qm, dn_nt, preferred_element_type=F32) + bias)
            s_ctxs.append(lax.dot_general(kcb_scr[...], qm, dn_nt, preferred_element_type=F32))
        ms = [jnp.maximum(jnp.max(sw, axis=0, keepdims=True), jnp.max(sc, axis=0, keepdims=True))
              for sw, sc in zip(s_wins, s_ctxs)]
        e_wins = [jnp.exp(sw - m) for sw, m in zip(s_wins, ms)]
        e_ctxs = [jnp.exp(sc - m) for sc, m in zip(s_ctxs, ms)]
        dens = [jnp.sum(ew, axis=0, keepdims=True) + jnp.sum(ec, axis=0, keepdims=True)
                for ew, ec in zip(e_wins, e_ctxs)]
        for qsl, wsl, ew, ec, den in zip(qsls, wsls, e_wins, e_ctxs, dens):
            o = (lax.dot_general(ew.astype(BF16), v_ref[wsl, :], dn_tn, preferred_element_type=F32)
                 + lax.dot_general(ec.astype(BF16), vcb_scr[...], dn_tn, preferred_element_type=F32))
            o = o / jnp.transpose(jnp.broadcast_to(den, (LANES, LANES)))
            o_ref[qsl, :] = jnp.where(first, o[:GRID_W], o[GRID_W:]).astype(o_ref.dtype)
        return carry

    lax.fori_loop(0, rows // _NA_UNROLL, body, 0)


def _na_bias_table(rpb):
    col = jnp.arange(GRID_W)
    cs = jnp.clip(col - NA_COLS // 2, 0, GRID_W - NA_COLS)
    col_ok = (col[None, :] >= cs[:, None]) & (col[None, :] < cs[:, None] + NA_COLS)
    dc = jnp.clip(col[None, :] - col[:, None] + NA_COLS - 1, 0, 2 * NA_COLS - 2)
    onehot = (dc.T[None, :, :] == jnp.arange(2 * NA_COLS - 1)[:, None, None]).astype(F32)
    t = jnp.einsum('hrd,dkq->hrkq', rpb.astype(F32), onehot, precision=lax.Precision.HIGHEST)
    t = jnp.where(col_ok.T[None, None], t, NEG_INF)
    t = t.reshape(NA_HEADS // 2, 2, 2 * NA_ROWS - 1, GRID_W, GRID_W)
    return jnp.concatenate([t[:, 0], t[:, 1]], axis=-1)


def _na_attention(proj, kctx, vctx, rpb):
    blk = N_CTX // DEC_SEQ

    def col(cb):
        return pl.BlockSpec((None, DEC_SEQ, LANES), lambda b, p: (cb + p, blk + b, 0))

    ctx = pl.BlockSpec((None, PAST_LEN, LANES), lambda b, p: (b, 0, p))
    return pl.pallas_call(
        _na_kernel, grid=(DEC_BATCH, NA_HEADS // 2),
        in_specs=[col(_QB), col(_KB), col(_VB), ctx, ctx,
                  pl.BlockSpec((None, 2 * NA_ROWS - 1, GRID_W, 2 * GRID_W), lambda b, p: (p, 0, 0, 0))],
        out_specs=pl.BlockSpec((None, DEC_SEQ, LANES), lambda b, p: (p, b, 0)),
        out_shape=jax.ShapeDtypeStruct((NA_HEADS // 2, N_LAT, LANES), BF16),
        scratch_shapes=[pltpu.VMEM((PAST_LEN, LANES), BF16), pltpu.VMEM((PAST_LEN, LANES), BF16)],
        compiler_params=_params(("parallel", "parallel")), name="na_attention",
    )(proj, proj, proj, kctx, vctx, _na_bias_table(rpb))


_LOGIT0 = N_EGROUPS
_R_E, _R_W, _R_RANK = 0, 2, 4


def _lane_min_where(mask, lane):
    return jnp.min(jnp.where(mask, lane, LANES), axis=-1, keepdims=True)


def _route_rows(lg, carry_ref, tri_ref):
    big = -3.0e38
    lane = lax.broadcasted_iota(jnp.int32, lg.shape, 1)
    is_g = lane < N_EGROUPS
    gmax = jnp.max(jnp.where(is_g, lg, big), axis=-1, keepdims=True)
    gsum = jnp.sum(jnp.where(is_g, jnp.exp(jnp.where(is_g, lg - gmax, 0.0)), 0.0), axis=-1, keepdims=True)
    pg_top = 1.0 / gsum
    g_idx = _lane_min_where(jnp.logical_and(is_g, lg == gmax), lane)
    in_g = jnp.logical_and(lane >= _LOGIT0, lax.shift_right_arithmetic(lane - _LOGIT0, 3) == g_idx)
    in_g = jnp.logical_and(in_g, lane < _LOGIT0 + N_EXPERTS)
    m1 = jnp.max(jnp.where(in_g, lg, big), axis=-1, keepdims=True)
    i1 = _lane_min_where(jnp.logical_and(in_g, lg == m1), lane)
    rest = jnp.logical_and(in_g, lane != i1)
    m2 = jnp.max(jnp.where(rest, lg, big), axis=-1, keepdims=True)
    i2 = _lane_min_where(jnp.logical_and(rest, lg == m2), lane)
    e2 = jnp.exp(m2 - m1)
    w1 = pg_top * (1.0 / (1.0 + e2))
    w2 = pg_top * (e2 / (1.0 + e2))
    hit1 = lane == i1
    hit2 = lane == i2
    picked = jnp.where(jnp.logical_or(hit1, hit2), 1.0, 0.0)
    before = jnp.dot(tri_ref[...], picked.astype(BF16), preferred_element_type=F32) + carry_ref[...]
    r1 = jnp.sum(jnp.where(hit1, before, 0.0), axis=-1, keepdims=True)
    r2 = jnp.sum(jnp.where(hit2, before, 0.0), axis=-1, keepdims=True)
    carry_ref[...] = carry_ref[...] + jnp.sum(picked, axis=0, keepdims=True)
    rec = jnp.zeros(lg.shape, F32)
    for ln, val in ((_R_E, (i1 - _LOGIT0).astype(F32)), (_R_E + 1, (i2 - _LOGIT0).astype(F32)),
                    (_R_W, w1), (_R_W + 1, w2), (_R_RANK, r1), (_R_RANK + 1, r2)):
        rec = jnp.where(lane == ln, val, rec)
    return rec


_PACK_W = D_MODEL // 2


def _pack_rows(hb):
    lo = lax.bitcast_convert_type(hb[:, :_PACK_W].astype(F32), jnp.int32)
    hi = lax.bitcast_convert_type(hb[:, _PACK_W:].astype(F32), jnp.int32)
    return jnp.bitwise_or(jnp.bitwise_and(hi, -65536), lax.shift_right_logical(lo, 16))


def _unpack_rows(w):
    lo = lax.bitcast_convert_type(lax.shift_left(w, 16), F32)
    hi = lax.bitcast_convert_type(jnp.bitwise_and(w, -65536), F32)
    return jnp.concatenate([lo, hi], axis=-1).astype(BF16)


def _moe_input(xnew, first, tail_in, tail_out, tail_scr):
    g2_ref, sc2_ref, sh2_ref, wrh_ref, wrl_ref, br_ref = tail_in
    x_out, h_out, rec_out, cnt_out = tail_out
    tri_scr, carry_scr = tail_scr

    @pl.when(first)
    def _():
        tm = tri_scr.shape[0]
        r = lax.broadcasted_iota(jnp.int32, (tm, tm), 0)
        c = lax.broadcasted_iota(jnp.int32, (tm, tm), 1)
        tri_scr[...] = jnp.where(c < r, 1.0, 0.0).astype(BF16)
        carry_scr[...] = jnp.zeros(carry_scr.shape, F32)

    x_out[...] = xnew
    h = _rms(xnew, g2_ref[...]) * (1.0 + sc2_ref[...]) + sh2_ref[...]
    hh = h.astype(BF16)
    hl = (h - hh.astype(F32)).astype(BF16)
    h_out[...] = _pack_rows(hh)
    lg = (jnp.dot(hh, wrh_ref[...], preferred_element_type=F32)
          + jnp.dot(hh, wrl_ref[...], preferred_element_type=F32)
          + jnp.dot(hl, wrh_ref[...], preferred_element_type=F32) + br_ref[...])
    rec_out[...] = _route_rows(lg, carry_scr, tri_scr)
    cnt_out[...] = carry_scr[...]


def _even_out_kernel(oac_ref, obc_ref, oal_ref, obl_ref, x_ref, w_ref, gate_ref, *rest, ctx_tiles):
    tail_in, tail_out, (w_scr,), tail_scr = rest[:6], rest[6:10], rest[10:11], rest[11:]
    first = pl.program_id(0) == 0

    @pl.when(first)
    def _():
        w_scr[...] = w_ref[...].astype(BF16)

    is_ctx = pl.program_id(0) < ctx_tiles
    parts = [jnp.where(is_ctx, c_ref[hb], l_ref[hb])
             for c_ref, l_ref in ((oac_ref, oal_ref), (obc_ref, obl_ref)) for hb in range(DN_HEADS)]
    mix = jnp.concatenate(parts, axis=-1)
    out = jnp.dot(mix, w_scr[...], preferred_element_type=F32)
    _moe_input(x_ref[...] + gate_ref[...] * out, first, tail_in, tail_out, tail_scr)


def _tail_specs(tm):
    const = lambda shape: pl.BlockSpec(shape, lambda i: (0,) * len(shape))
    in_specs = [_mod_spec(2, tm), const((1, D_MODEL)), _mod_spec(4, tm), _mod_spec(3, tm),
                const((D_MODEL, LANES)), const((D_MODEL, LANES)), const((1, LANES))]
    out_specs = [pl.BlockSpec((tm, D_MODEL), lambda i: (i, 0)),
                 pl.BlockSpec((tm, _PACK_W), lambda i: (i, 0)),
                 pl.BlockSpec((tm, LANES), lambda i: (i, 0)),
                 const((1, LANES))]
    out_shape = [jax.ShapeDtypeStruct((N_TOK, D_MODEL), F32),
                 jax.ShapeDtypeStruct((N_TOK, _PACK_W), jnp.int32),
                 jax.ShapeDtypeStruct((N_TOK, LANES), F32),
                 jax.ShapeDtypeStruct((1, LANES), F32)]
    scratch = [pltpu.VMEM((tm, tm), BF16), pltpu.VMEM((1, LANES), F32)]
    return in_specs, out_specs, out_shape, scratch


def _router_weights(w_rg, b_rg, w_re, b_re):
    pad = LANES - N_EGROUPS - N_EXPERTS
    w = jnp.concatenate([w_rg, w_re, jnp.zeros((D_MODEL, pad), F32)], axis=1)
    b = jnp.concatenate([b_rg, b_re, jnp.zeros((pad,), F32)]).reshape(1, LANES)
    hi = w.astype(BF16)
    lo = (w - hi.astype(F32)).astype(BF16)
    return hi, lo, b


def _even_out(oa_ctx, ob_ctx, oa_lat, ob_lat, x, w_out, mods, g2, router):
    tm = 512
    ctx_tiles = N_CTX // tm
    tail_in, out_specs, out_shape, tail_scr = _tail_specs(tm)
    ctxblk = pl.BlockSpec((DN_HEADS, tm, LANES), lambda i: (0, jnp.minimum(i, ctx_tiles - 1), 0))
    latblk = pl.BlockSpec((DN_HEADS, tm, LANES), lambda i: (0, jnp.maximum(i - ctx_tiles, 0), 0))
    return pl.pallas_call(
        functools.partial(_even_out_kernel, ctx_tiles=ctx_tiles), grid=(N_TOK // tm,),
        in_specs=[ctxblk, ctxblk, latblk, latblk, pl.BlockSpec((tm, D_MODEL), lambda i: (i, 0)),
                  pl.BlockSpec((D_MODEL, D_MODEL), lambda i: (0, 0))] + tail_in,
        out_specs=out_specs, out_shape=out_shape,
        scratch_shapes=[pltpu.VMEM((D_MODEL, D_MODEL), BF16)] + tail_scr,
        compiler_params=_params(("arbitrary",)), name="even_out",
    )(oa_ctx, ob_ctx, oa_lat, ob_lat, x, w_out, mods, g2.reshape(1, D_MODEL), mods, mods, *router)


def _gelu_tanh(x):
    return x * (0.5 * (1.0 + jnp.tanh(0.7978845608028654 * (x + 0.044715 * (x * x * x)))))


def _sgu_kernel(x_ref, g1_ref, sh1_ref, sc1_ref, win_ref, lng_ref, lnb_ref, ws_ref, bst_ref, wout_ref, gate_ref,
                *rest, tm):
    tail_in, tail_out, (v_scr, m_scr), tail_scr = rest[:6], rest[6:10], rest[10:12], rest[12:]
    first = pl.program_id(0) == 0
    x = x_ref[...]
    h = (_rms(x, g1_ref[...]) * (1.0 + sc1_ref[...]) + sh1_ref[...]).astype(BF16)

    v = _gelu_tanh(jnp.dot(h, win_ref[:, SG_W:], preferred_element_type=F32))
    mu = jnp.mean(v, axis=-1, keepdims=True)
    vc = v - mu
    var = jnp.mean(vc * vc, axis=-1, keepdims=True)
    v_scr[...] = (vc * lax.rsqrt(var + EPS) * lng_ref[...] + lnb_ref[...]).astype(BF16)

    for g in range(SG_GROUPS):
        cs = slice(g * SG_GW, (g + 1) * SG_GW)
        u = _gelu_tanh(jnp.dot(h, win_ref[:, cs], preferred_element_type=F32))
        w_sp = ws_ref[g].astype(BF16)
        for c in range(tm // SG_CHUNK):
            rs = slice(c * SG_CHUNK, (c + 1) * SG_CHUNK)
            sp = jnp.dot(w_sp, v_scr[rs, cs], preferred_element_type=F32) + bst_ref[:, g:g + 1]
            m_scr[rs, cs] = (u[rs] * sp).astype(BF16)
    out = jnp.dot(m_scr[...], wout_ref[...], preferred_element_type=F32)
    _moe_input(x + gate_ref[...] * out, first, tail_in, tail_out, tail_scr)


def _sgu_layer(x, mods, g1, w_in, ln_g, ln_b, w_s, b_s, w_out, g2, router):
    tm = 512
    tail_in, out_specs, out_shape, tail_scr = _tail_specs(tm)
    const = lambda shape: pl.BlockSpec(shape, lambda i: (0,) * len(shape))
    held = lambda shape: pl.BlockSpec(shape, lambda i: (0,) * len(shape), pipeline_mode=pl.Buffered(1))
    return pl.pallas_call(
        functools.partial(_sgu_kernel, tm=tm), grid=(N_TOK // tm,),
        in_specs=[pl.BlockSpec((tm, D_MODEL), lambda i: (i, 0)),
                  const((1, D_MODEL)), _mod_spec(0, tm), _mod_spec(1, tm),
                  held((D_MODEL, 2 * SG_W)), const((1, SG_W)), const((1, SG_W)),
                  const((SG_GROUPS, SG_CHUNK, SG_CHUNK)), const((SG_CHUNK, SG_GROUPS)),
                  held((SG_W, D_MODEL))] + tail_in,
        out_specs=out_specs, out_shape=out_shape,
        scratch_shapes=[pltpu.VMEM((tm, SG_W), BF16), pltpu.VMEM((tm, SG_W), BF16)] + tail_scr,
        compiler_params=_params(("arbitrary",)), name="sgu_layer",
    )(x, g1.reshape(1, D_MODEL), mods, mods, w_in.astype(BF16), ln_g.reshape(1, SG_W), ln_b.reshape(1, SG_W),
      w_s, b_s.T, w_out.astype(BF16), mods, g2.reshape(1, D_MODEL), mods, mods, *router)


def _plan(rec, cnt):
    e_idx = rec[:, _R_E:_R_E + 2].astype(jnp.int32)
    rank = rec[:, _R_RANK:_R_RANK + 2].astype(jnp.int32)
    counts = cnt[0, _LOGIT0:_LOGIT0 + N_EXPERTS].astype(jnp.int32)
    padded = (counts + MOE_BLK - 1) // MOE_BLK * MOE_BLK
    pad_end = jnp.cumsum(padded)
    pad_start = pad_end - padded
    hit = e_idx[:, :, None] == jnp.arange(N_EXPERTS, dtype=jnp.int32)[None, None, :]
    dest = jnp.sum(jnp.where(hit, pad_start[None, None, :], 0), axis=-1) + rank
    blk0 = jnp.arange(MOE_NBLK, dtype=jnp.int32) * MOE_BLK
    blk_e = jnp.minimum(jnp.sum((pad_end[None, :] <= blk0[:, None]).astype(jnp.int32), axis=-1),
                        N_EXPERTS - 1)
    n_used = (pad_end[-1] // MOE_BLK).astype(jnp.int32).reshape(1)
    owns = counts > 0
    slot_of = (jnp.cumsum(owns.astype(jnp.int32)) - 1) % 2
    ids = jnp.arange(N_EXPERTS, dtype=jnp.int32)
    later = jnp.logical_and(owns[None, :], ids[None, :] > ids[:, None])
    next_of = jnp.min(jnp.where(later, ids[None, :], N_EXPERTS), axis=-1)
    next_of = jnp.where(next_of == N_EXPERTS, -1, next_of)
    return dest, blk_e, n_used, slot_of[blk_e], next_of[blk_e]


def _expert_kernel(blk_e_ref, n_used_ref, slot_ref, next_ref, x_ref, wg_hbm, wu_hbm, wd_hbm, o_ref,
                   wg_buf, wu_buf, wd_buf, wg_scr, wu_scr, wd_scr, sems, *, layer):
    j = pl.program_id(0)
    e = blk_e_ref[j]
    slot = slot_ref[j]
    fresh = jnp.logical_or(j == 0, e != blk_e_ref[jnp.maximum(j - 1, 0)])
    live = j < n_used_ref[0]

    def copies(expert, s):
        return (pltpu.make_async_copy(wg_hbm.at[layer, expert], wg_buf.at[s], sems.at[s, 0]),
                pltpu.make_async_copy(wu_hbm.at[layer, expert], wu_buf.at[s], sems.at[s, 1]),
                pltpu.make_async_copy(wd_hbm.at[layer, expert], wd_buf.at[s], sems.at[s, 2]))

    @pl.when(j == 0)
    def _():
        for cp in copies(e, slot):
            cp.start()

    @pl.when(jnp.logical_and(fresh, live))
    def _():
        for cp in copies(e, slot):
            cp.wait()
        nxt = next_ref[j]

        @pl.when(nxt >= 0)
        def _():
            for cp in copies(nxt, 1 - slot):
                cp.start()

        wg_scr[...] = wg_buf[slot].astype(BF16)
        wu_scr[...] = wu_buf[slot].astype(BF16)
        wd_scr[...] = wd_buf[slot].astype(BF16)

    @pl.when(live)
    def _():
        x = _unpack_rows(x_ref[...])
        gt = jnp.dot(x, wg_scr[...], preferred_element_type=F32)
        up = jnp.dot(x, wu_scr[...], preferred_element_type=F32)
        hb = (_silu(gt) * up).astype(BF16)
        o_ref[...] = jnp.dot(hb, wd_scr[...], preferred_element_type=F32).astype(o_ref.dtype)

    @pl.when(jnp.logical_not(live))
    def _():
        o_ref[...] = jnp.zeros(o_ref.shape, o_ref.dtype)


def _experts(x_pad, blk_e, n_used, slot, nxt, w_gate, w_up, w_down, layer):
    hbm = pl.BlockSpec(memory_space=pl.ANY)
    grid_spec = pltpu.PrefetchScalarGridSpec(
        num_scalar_prefetch=4, grid=(MOE_NBLK,),
        in_specs=[pl.BlockSpec((MOE_BLK, _PACK_W), lambda j, *_: (j, 0)), hbm, hbm, hbm],
        out_specs=pl.BlockSpec((MOE_BLK, D_MODEL), lambda j, *_: (j, 0)),
        scratch_shapes=[pltpu.VMEM((2, D_MODEL, D_EXPERT), F32), pltpu.VMEM((2, D_MODEL, D_EXPERT), F32),
                        pltpu.VMEM((2, D_EXPERT, D_MODEL), F32),
                        pltpu.VMEM((D_MODEL, D_EXPERT), BF16), pltpu.VMEM((D_MODEL, D_EXPERT), BF16),
                        pltpu.VMEM((D_EXPERT, D_MODEL), BF16), pltpu.SemaphoreType.DMA((2, 3))])
    return pl.pallas_call(
        functools.partial(_expert_kernel, layer=layer), grid_spec=grid_spec,
        out_shape=jax.ShapeDtypeStruct((MOE_NBLK * MOE_BLK, D_MODEL), BF16),
        compiler_params=_params(("arbitrary",)), name="experts",
    )(blk_e, n_used, slot, nxt, x_pad, w_gate, w_up, w_down)


def _combine_kernel(x_ref, ya_ref, yb_ref, rec_ref, gate_ref, fg_ref, o_ref, *, final):
    rec = rec_ref[...]
    y = (rec[:, _R_W:_R_W + 1] * ya_ref[...].astype(F32)
         + rec[:, _R_W + 1:_R_W + 2] * yb_ref[...].astype(F32))
    xn = x_ref[...] + gate_ref[...] * y
    o_ref[...] = _rms(xn, fg_ref[...]) if final else xn


def _combine(x, ya, yb, rec, mods, final_g, final):
    tm = 512
    blk = pl.BlockSpec((tm, D_MODEL), lambda i: (i, 0))
    return pl.pallas_call(
        functools.partial(_combine_kernel, final=final), grid=(N_TOK // tm,),
        in_specs=[blk, blk, blk, pl.BlockSpec((tm, LANES), lambda i: (i, 0)), _mod_spec(5, tm),
                  pl.BlockSpec((1, D_MODEL), lambda i: (0, 0))],
        out_specs=blk, out_shape=jax.ShapeDtypeStruct((N_TOK, D_MODEL), F32),
        compiler_params=_params(("parallel",)), name="moe_combine",
    )(x, ya, yb, rec, mods, final_g.reshape(1, D_MODEL))


_SC_WORKERS = 32
_SC_CORES = 2
_SC_ROWS = 64


def _dispatch_rows(hp, dest):
    n, width = hp.shape
    per_w = n // _SC_WORKERS
    n_ch = per_w // _SC_ROWS
    idx = dest.T.reshape(2, _SC_WORKERS, n_ch, _SC_ROWS)
    mesh = plsc.VectorSubcoreMesh(core_axis_name="c", subcore_axis_name="s")

    @functools.partial(
        pl.kernel, mesh=mesh, out_type=jax.ShapeDtypeStruct((MOE_NBLK * MOE_BLK, width), hp.dtype),
        scratch_types=[pltpu.VMEM((n_ch, _SC_ROWS), jnp.int32), pltpu.VMEM((n_ch, _SC_ROWS), jnp.int32),
                       pltpu.VMEM((_SC_ROWS, width), hp.dtype)], name="dispatch_rows")
    def scatter(h_hbm, idx_hbm, out_hbm, i0_v, i1_v, rows_v):
        wid = lax.axis_index("s") * _SC_CORES + lax.axis_index("c")
        pltpu.sync_copy(idx_hbm.at[0, wid], i0_v)
        pltpu.sync_copy(idx_hbm.at[1, wid], i1_v)

        @pl.loop(0, n_ch)
        def _(g):
            pltpu.sync_copy(h_hbm.at[pl.ds(wid * per_w + g * _SC_ROWS, _SC_ROWS)], rows_v)
            pltpu.sync_copy(rows_v, out_hbm.at[i0_v.at[g]])
            pltpu.sync_copy(rows_v, out_hbm.at[i1_v.at[g]])

    return scatter(hp, idx)


def _moe(x, h, rec, cnt, mods, w_gate, w_up, w_down, layer, final_g, final):
    dest, blk_e, n_used, slot, nxt = _plan(rec, cnt)
    y_pad = _experts(_dispatch_rows(h, dest), blk_e, n_used, slot, nxt, w_gate, w_up, w_down, layer)
    return _combine(x, y_pad[dest[:, 0]], y_pad[dest[:, 1]], rec, mods, final_g, final)


def kernel(x_prompt, x_sample, c, cache_k, cache_v, state_delta, c_ctx, ada_w, ada_b, norm1_g, norm2_g, final_g,
           ev_w_in, ev_w_out, ev_conv_w, ev_a_log, ev_dt_bias, ev_onorm_g, ev_rpb, od_w_in, od_ln_g, od_ln_b,
           od_w_s, od_b_s, od_w_out, moe_w_rg, moe_b_rg, moe_w_re, moe_b_re, moe_w_gate, moe_w_up, moe_w_down):
    x = jnp.concatenate([x_prompt.reshape(N_CTX, D_MODEL), x_sample.reshape(N_LAT, D_MODEL)], axis=0)
    cond = jnp.concatenate([c_ctx[None, :], c, jnp.zeros((N_COND - 1 - DEC_BATCH, D_MODEL), F32)], axis=0)
    mods_all = _ada_mods(cond, ada_w, ada_b)
    kctx_all = cache_k.reshape(DEC_BATCH, -1, PAST_LEN, NA_HEADS * NA_HD)
    vctx_all = cache_v.reshape(DEC_BATCH, -1, PAST_LEN, NA_HEADS * NA_HD)

    ks, vs, ss = [], [], []
    for l in range(DEPTH):
        mods = mods_all[l]
        router = _router_weights(moe_w_rg[l], moe_b_rg[l], moe_w_re[l], moe_b_re[l])
        if l % 2 == 0:
            e = l // 2
            proj, ab, kv = _even_proj(x, mods, norm1_g[l], ev_w_in[e])
            dn = (proj, ab, ev_conv_w[e], ev_a_log[e], ev_dt_bias[e], ev_onorm_g[e])
            oa_ctx, s_fin = _delta_heads(*dn, SEQ, BATCH, 0, None)
            oa_lat, _ = _delta_heads(*dn, DEC_SEQ, DEC_BATCH, N_CTX // DEC_SEQ, state_delta[:, e])
            ob_ctx = _ctx_attention(proj)
            ob_lat = _na_attention(proj, kctx_all[:, e], vctx_all[:, e], ev_rpb[e])
            x, h, rec, cnt = _even_out(oa_ctx, ob_ctx, oa_lat, ob_lat, x, ev_w_out[e], mods, norm2_g[l],
                                       router)
            na_w = NA_HEADS * NA_HD
            ks.append(kv[:N_CTX, :na_w].reshape(BATCH, SEQ, NA_HEADS, NA_HD))
            vs.append(kv[:N_CTX, na_w:].reshape(BATCH, SEQ, NA_HEADS, NA_HD))
            ss.append(s_fin)
        else:
            o = l // 2
            x, h, rec, cnt = _sgu_layer(x, mods, norm1_g[l], od_w_in[o], od_ln_g[o], od_ln_b[o], od_w_s[o],
                                        od_b_s[o], od_w_out[o], norm2_g[l], router)
        x = _moe(x, h, rec, cnt, mods, moe_w_gate, moe_w_up, moe_w_down, l, final_g, l == DEPTH - 1)

    y_prompt = x[:N_CTX].reshape(BATCH, SEQ, D_MODEL)
    y_sample = x[N_CTX:].reshape(DEC_BATCH, DEC_SEQ, D_MODEL)
    return (y_prompt, y_sample, jnp.stack(ks, axis=1), jnp.stack(vs, axis=1), jnp.stack(ss, axis=1))
```

```python
import functools

import jax
import jax.numpy as jnp
from jax import lax
from jax.experimental import pallas as pl
from jax.experimental.pallas import tpu as pltpu
from jax.experimental.pallas import tpu_sc as plsc

F32 = jnp.float32
BF16 = jnp.bfloat16

D_MODEL = 1024
BATCH = 16
SEQ = 256
DEPTH = 4
DEC_BATCH = 4
DEC_SEQ = 2048
PAST_LEN = 512
GRID_W = 64
EPS = 1e-6
NEG_INF = -1e30

DN_HEADS = 4
DN_DK = 128
DN_CHUNK = 64
NA_HEADS = 8
NA_HD = 64
NA_ROWS = 8
NA_COLS = 16
SG_CHUNK = 128
SG_GROUPS = 8
SG_W = 2 * D_MODEL
SG_GW = SG_W // SG_GROUPS
N_EGROUPS = 4
EXP_PER_GROUP = 8
N_EXPERTS = 32
D_EXPERT = 512

N_CTX = BATCH * SEQ
N_LAT = DEC_BATCH * DEC_SEQ
N_TOK = N_CTX + N_LAT
N_COND = 8
PROJ_W = 4096
LANES = 128
MOE_BLK = 256
MOE_NBLK = -(-(2 * N_TOK + N_EXPERTS * (MOE_BLK - 1)) // MOE_BLK)
VMEM_LIMIT = 56 * 1024 * 1024

_QA, _KA, _VA, _ZA, _QB, _KB, _VB = 0, 4, 8, 12, 16, 20, 24


def _params(sem):
    return pltpu.CompilerParams(dimension_semantics=sem, vmem_limit_bytes=VMEM_LIMIT)


def _bdot(a, b):
    return jnp.dot(a.astype(BF16), b.astype(BF16), preferred_element_type=F32)


def _bdot_nt(a, b):
    return lax.dot_general(a.astype(BF16), b.astype(BF16), (((1,), (1,)), ((), ())),
                           preferred_element_type=F32)


def _bdot_tn(a, b):
    return lax.dot_general(a.astype(BF16), b.astype(BF16), (((0,), (0,)), ((), ())),
                           preferred_element_type=F32)


def _split3(a):
    p0 = a.astype(BF16)
    r = a - p0.astype(F32)
    p1 = r.astype(BF16)
    p2 = (r - p1.astype(F32)).astype(BF16)
    return p0, p1, p2


def _dot3(a, b):
    ah = a.astype(BF16)
    al = (a - ah.astype(F32)).astype(BF16)
    bh = b.astype(BF16)
    bl = (b - bh.astype(F32)).astype(BF16)
    return (jnp.dot(ah, bh, preferred_element_type=F32) + jnp.dot(ah, bl, preferred_element_type=F32)
            + jnp.dot(al, bh, preferred_element_type=F32))


def _mask_bf16(m01):
    return jnp.where(m01, 1.0, 0.0).astype(BF16)


def _xdot(m01, a):
    m = _mask_bf16(m01)
    p0, p1, p2 = _split3(a)
    return (jnp.dot(m, p0, preferred_element_type=F32) + jnp.dot(m, p1, preferred_element_type=F32)
            + jnp.dot(m, p2, preferred_element_type=F32))


def _xdot_nt(m01, a):
    m = _mask_bf16(m01)
    dn = (((1,), (1,)), ((), ()))
    p0, p1, p2 = _split3(a)
    return (lax.dot_general(m, p0, dn, preferred_element_type=F32)
            + lax.dot_general(m, p1, dn, preferred_element_type=F32)
            + lax.dot_general(m, p2, dn, preferred_element_type=F32))


def _xdot_r(a, m01):
    m = _mask_bf16(m01)
    p0, p1, p2 = _split3(a)
    return (jnp.dot(p0, m, preferred_element_type=F32) + jnp.dot(p1, m, preferred_element_type=F32)
            + jnp.dot(p2, m, preferred_element_type=F32))


def _sigmoid(x):
    return 0.5 * jnp.tanh(0.5 * x) + 0.5


def _silu(x):
    return x * _sigmoid(x)


def _rms(x, g):
    return x * lax.rsqrt(jnp.mean(x * x, axis=-1, keepdims=True) + EPS) * g


def _cond_index(row):
    return jnp.where(row < N_CTX, 0, 1 + (row - N_CTX) // DEC_SEQ)


def _mod_spec(k, tm):
    return pl.BlockSpec((None, None, 1, D_MODEL), lambda i, *_: (_cond_index(i * tm), k, 0, 0))


def _ada_kernel(c_ref, w_ref, b_ref, o_ref):
    o_ref[...] = _bdot(_silu(c_ref[...]), w_ref[...]) + b_ref[...]


def _ada_mods(cond, ada_w, ada_b):
    tn = 1536
    out = pl.pallas_call(
        _ada_kernel, grid=(DEPTH, 6 * D_MODEL // tn),
        in_specs=[pl.BlockSpec((N_COND, D_MODEL), lambda l, j: (0, 0)),
                  pl.BlockSpec((None, D_MODEL, tn), lambda l, j: (l, 0, j)),
                  pl.BlockSpec((None, 1, tn), lambda l, j: (l, 0, j))],
        out_specs=pl.BlockSpec((None, N_COND, tn), lambda l, j: (l, 0, j)),
        out_shape=jax.ShapeDtypeStruct((DEPTH, N_COND, 6 * D_MODEL), F32),
        compiler_params=_params(("parallel", "parallel")), name="ada_mods",
    )(cond, ada_w, ada_b.reshape(DEPTH, 1, 6 * D_MODEL))
    return out.reshape(DEPTH, N_COND, 6, 1, D_MODEL)


_EV_TN = 512
_EV_W = 7 * DN_HEADS * LANES
_KV_COL0 = _KB * LANES


def _even_proj_kernel(x_ref, g_ref, sh_ref, sc_ref, w_ref, wab_ref, o_ref, ab_ref, kv_ref):
    h = (_rms(x_ref[...], g_ref[...]) * (1.0 + sc_ref[...]) + sh_ref[...]).astype(BF16)
    ab_ref[...] = jnp.dot(h, wab_ref[...], preferred_element_type=F32)
    for j in range(_EV_W // _EV_TN):
        c0 = j * _EV_TN
        y = jnp.dot(h, w_ref[:, c0:c0 + _EV_TN], preferred_element_type=F32)
        for c in range(_EV_TN // LANES):
            o_ref[c0 // LANES + c] = y[:, c * LANES:(c + 1) * LANES].astype(BF16)
        if c0 >= _KV_COL0:
            kv_ref[:, c0 - _KV_COL0:c0 - _KV_COL0 + _EV_TN] = y


def _even_proj(x, mods, g, w_in):
    tm = 512
    n_ab = 4 * DN_HEADS
    ab0 = 4 * DN_HEADS * DN_DK
    w_main = jnp.concatenate([w_in[:, :ab0], w_in[:, ab0 + n_ab:]], axis=1).astype(BF16)
    w_ab = jnp.concatenate([w_in[:, ab0:ab0 + n_ab], jnp.zeros((D_MODEL, LANES - n_ab), F32)],
                           axis=1).astype(BF16)
    held = lambda shape: pl.BlockSpec(shape, lambda i: (0,) * len(shape), pipeline_mode=pl.Buffered(1))
    return pl.pallas_call(
        _even_proj_kernel, grid=(N_TOK // tm,),
        in_specs=[pl.BlockSpec((tm, D_MODEL), lambda i: (i, 0)),
                  pl.BlockSpec((1, D_MODEL), lambda i: (0, 0)),
                  _mod_spec(0, tm), _mod_spec(1, tm),
                  held((D_MODEL, _EV_W)), held((D_MODEL, LANES))],
        out_specs=[pl.BlockSpec((_EV_W // LANES, tm, LANES), lambda i: (0, i, 0)),
                   pl.BlockSpec((tm, LANES), lambda i: (i, 0)),
                   pl.BlockSpec((tm, 2 * NA_HEADS * NA_HD), lambda i: (i, 0))],
        out_shape=[jax.ShapeDtypeStruct((_EV_W // LANES, N_TOK, LANES), BF16),
                   jax.ShapeDtypeStruct((N_TOK, LANES), F32),
                   jax.ShapeDtypeStruct((N_TOK, 2 * NA_HEADS * NA_HD), F32)],
        compiler_params=_params(("parallel",)), name="even_proj",
    )(x, g.reshape(1, D_MODEL), mods, mods, w_main, w_ab)


_CHUNK_SHIFT = DN_CHUNK.bit_length() - 1
_CUM_ROWS = 256
_DN_CHAINS = 16
_DN_SHORT = 256
_SERIES_FINE = 3
_MQ_ROWS = DN_DK + DN_CHUNK


def _dn_kernel(*refs, T, HB, has_s0, want_state):
    it = iter(refs)
    q_ref, k_ref, v_ref, z_ref, ab_ref = (next(it) for _ in range(5))
    cwq_ref, cwk_ref, cwv_ref, alog_ref, dtb_ref, og_ref = (next(it) for _ in range(6))
    s0_ref = next(it) if has_s0 else None
    o_ref = next(it)
    sfin_ref = next(it) if want_state else None
    qc, kc, vc, gsc, bsc, osc, b_s, mq_s = (next(it) for _ in range(8))

    C = DN_CHUNK
    n = T // C
    h0 = pl.program_id(1) * HB

    row = lax.broadcasted_iota(jnp.int32, (T, 1), 0)

    def conv(x_ref, cw_ref, hh):
        x = x_ref[hh].astype(F32)
        cw = cw_ref[:, hh * LANES:(hh + 1) * LANES]
        xp = jnp.where(row == 0, 0.0, pltpu.roll(x, 1, 0))
        xn = jnp.where(row == T - 1, 0.0, pltpu.roll(x, T - 1, 0))
        return _silu(cw[0:1, :] * xp + cw[1:2, :] * x + cw[2:3, :] * xn)

    def l2n(x):
        return x * lax.rsqrt(jnp.sum(x * x, axis=-1, keepdims=True) + EPS)

    ab = ab_ref[...]
    sel_r = lax.broadcasted_iota(jnp.int32, (LANES, LANES), 0)
    for hh in range(HB):
        qc[hh] = l2n(conv(q_ref, cwq_ref, hh)) * (DN_DK ** -0.5)
        kc[hh] = l2n(conv(k_ref, cwk_ref, hh))
        vc[hh] = conv(v_ref, cwv_ref, hh)
        hd = h0 + hh
        for d in range(2):
            alpha = _xdot_r(ab, sel_r == d * DN_HEADS + hd)
            blog = _xdot_r(ab, sel_r == 2 * DN_HEADS + d * DN_HEADS + hd)
            x = alpha + dtb_ref[d, hd]
            sp = jnp.maximum(x, 0.0) + jnp.log1p(jnp.exp(-jnp.abs(x)))
            a = jnp.exp(jnp.full((1, LANES), alog_ref[d, hd], F32))
            gsc[hh, d] = -a * sp
            bsc[hh, d] = _sigmoid(blog)

    pr = lax.broadcasted_iota(jnp.int32, (_CUM_ROWS, _CUM_ROWS), 0)
    pc = lax.broadcasted_iota(jnp.int32, (_CUM_ROWS, _CUM_ROWS), 1)
    same = lax.shift_right_logical(pr, _CHUNK_SHIFT) == lax.shift_right_logical(pc, _CHUNK_SHIFT)
    cum_mask = (jnp.logical_and(same, pc <= pr), jnp.logical_and(same, pc >= pr))

    def cum_body(i, carry):
        sl = pl.ds(pl.multiple_of(i * _CUM_ROWS, _CUM_ROWS), _CUM_ROWS)
        for hh in range(HB):
            for d in range(2):
                gsc[hh, d, sl, :] = _xdot(cum_mask[d], gsc[hh, d, sl, :])
        return carry

    lax.fori_loop(0, T // _CUM_ROWS, cum_body, 0)

    ri = lax.broadcasted_iota(jnp.int32, (C, C), 0)
    ci = lax.broadcasted_iota(jnp.int32, (C, C), 1)
    eye = (ri == ci).astype(F32)

    def prepare(items):
        lows, decays = [], []
        for hh, d, c in items:
            sl = pl.ds(pl.multiple_of(c * C, C), C)
            k, gc = kc[hh, sl, :], gsc[hh, d, sl, :]
            incl = (ci <= ri) if d == 0 else (ci >= ri)
            strict = (ci < ri) if d == 0 else (ci > ri)
            gr = jnp.transpose(gc)[0:1, :C]
            decay = jnp.where(incl, jnp.exp(jnp.where(incl, gc[:, :C] - gr, 0.0)), 0.0)
            lows.append(jnp.where(strict, _bdot_nt(k * bsc[hh, d, sl, :], k) * decay, 0.0))
            decays.append(decay)
        ts = [eye - low for low in lows]
        ps = lows
        for step in range(5):
            dot = _dot3 if step < _SERIES_FINE else _bdot
            ps = [dot(p, p) for p in ps]
            ts = [t + dot(t, p) for t, p in zip(ts, ps)]
        for (hh, d, c), t, decay in zip(items, ts, decays):
            sl = pl.ds(pl.multiple_of(c * C, C), C)
            q, k, gc, beta = qc[hh, sl, :], kc[hh, sl, :], gsc[hh, d, sl, :], bsc[hh, d, sl, :]
            eg = jnp.exp(gc)
            uw = _bdot(t, jnp.concatenate([vc[hh, sl, :] * beta, k * beta * eg], axis=-1))
            last = gc[C - 1:C, :] if d == 0 else gc[0:1, :]
            wu = jnp.concatenate([uw[:, LANES:], uw[:, :LANES]], axis=-1).astype(BF16)
            kd = (k * jnp.exp(last - gc)).astype(BF16)
            attn = (_bdot_nt(q, k) * decay).astype(BF16)
            kdwu = lax.dot_general(kd, wu, (((0,), (0,)), ((), ())), preferred_element_type=F32)
            awu = jnp.dot(attn, wu, preferred_element_type=F32)
            mq0 = pl.multiple_of(c * _MQ_ROWS, _MQ_ROWS)
            mq_s[hh, d, pl.ds(mq0, DN_DK), :] = kdwu[:, :LANES].astype(BF16)
            mq_s[hh, d, pl.ds(mq0 + DN_DK, C), :] = (q * eg - awu[:, :LANES]).astype(BF16)
            b_s[hh, d, pl.ds(pl.multiple_of(c * DN_DK, DN_DK), DN_DK), :] = kdwu[:, LANES:]
            osc[hh, d, sl, :] = awu[:, LANES:]

    n_prep = min(n, _DN_CHAINS // 2)
    h_prep = max(1, min(HB, _DN_CHAINS // (2 * n_prep)))

    def prep_body(i, carry):
        for hg in range(0, HB, h_prep):
            prepare([(hh, d, i * n_prep + j) for hh in range(hg, hg + h_prep) for j in range(n_prep)
                     for d in range(2)])
        return carry

    lax.fori_loop(0, n // n_prep, prep_body, 0)

    def advance(hh, d, c, S):
        sl = pl.ds(pl.multiple_of(c * C, C), C)
        ms = jnp.dot(mq_s[hh, d, pl.ds(pl.multiple_of(c * _MQ_ROWS, _MQ_ROWS), _MQ_ROWS), :], S.astype(BF16),
                     preferred_element_type=F32)
        osc[hh, d, sl, :] = osc[hh, d, sl, :] + ms[DN_DK:]
        last = gsc[hh, d, pl.ds(c * C + (C - 1 if d == 0 else 0), 1), :]
        return (S * jnp.exp(last) - ms[:DN_DK]
                + b_s[hh, d, pl.ds(pl.multiple_of(c * DN_DK, DN_DK), DN_DK), :])

    def body(i, carry):
        return tuple(advance(hh, d, i if d == 0 else n - 1 - i, carry[2 * hh + d])
                     for hh in range(HB) for d in range(2))

    if has_s0:
        init = tuple(s0_ref[d, hh] for hh in range(HB) for d in range(2))
    else:
        init = tuple(jnp.zeros((DN_DK, LANES), F32) for _ in range(2 * HB))
    fin = lax.fori_loop(0, n, body, init)
    for hh in range(HB):
        if want_state:
            sfin_ref[0, hh] = fin[2 * hh]
            sfin_ref[1, hh] = fin[2 * hh + 1]
        o = osc[hh, 0] + osc[hh, 1]
        o_ref[hh] = (_rms(o, og_ref[...]) * _silu(z_ref[hh].astype(F32))).astype(o_ref.dtype)


def _delta_heads(proj, ab, conv_w, a_log, dt_bias, onorm_g, T, n_seq, row0, s0):
    has_s0 = s0 is not None
    want_state = not has_s0
    hb = DN_HEADS if T <= _DN_SHORT else 1

    def col(cb):
        return pl.BlockSpec((hb, T, LANES), lambda s, h: (cb // hb + h, row0 + s, 0))

    def cw(cb):
        return pl.BlockSpec((3, hb * LANES), lambda s, h: (0, cb // hb + h))

    smem = pl.BlockSpec(memory_space=pltpu.SMEM)
    in_specs = [col(_QA), col(_KA), col(_VA), col(_ZA),
                pl.BlockSpec((T, LANES), lambda s, h: (row0 + s, 0)),
                cw(0), cw(4), cw(8), smem, smem,
                pl.BlockSpec((1, LANES), lambda s, h: (0, 0))]
    args = [proj, proj, proj, proj, ab, conv_w, conv_w, conv_w, a_log, dt_bias,
            onorm_g.reshape(1, LANES)]
    state_spec = pl.BlockSpec((None, 2, hb, DN_DK, LANES), lambda s, h: (s, 0, h, 0, 0))
    if has_s0:
        in_specs.append(state_spec)
        args.append(s0)
    out_shape = [jax.ShapeDtypeStruct((DN_HEADS, n_seq * T, LANES), BF16)]
    out_specs = [pl.BlockSpec((hb, T, LANES), lambda s, h: (h, s, 0))]
    if want_state:
        out_shape.append(jax.ShapeDtypeStruct((n_seq, 2, DN_HEADS, DN_DK, LANES), F32))
        out_specs.append(state_spec)
    res = pl.pallas_call(
        functools.partial(_dn_kernel, T=T, HB=hb, has_s0=has_s0, want_state=want_state),
        grid=(n_seq, DN_HEADS // hb), in_specs=in_specs, out_specs=out_specs, out_shape=out_shape,
        scratch_shapes=[pltpu.VMEM((hb, T, LANES), F32)] * 3
        + [pltpu.VMEM((hb, 2, T, LANES), F32)] * 3
        + [pltpu.VMEM((hb, 2, T // DN_CHUNK * DN_DK, LANES), F32),
           pltpu.VMEM((hb, 2, T // DN_CHUNK * _MQ_ROWS, LANES), BF16)],
        compiler_params=_params(("parallel", "parallel")), name="delta_heads_%d" % T,
    )(*args)
    return res if want_state else (res[0], None)


def _pair_queries(q, first):
    return jnp.concatenate([jnp.where(first, q, 0.0), jnp.where(first, 0.0, q)], axis=0).astype(BF16)


def _ctx_attn_kernel(q_ref, k_ref, v_ref, o_ref):
    first = lax.broadcasted_iota(jnp.int32, (SEQ, LANES), 1) < NA_HD
    qm = _pair_queries(q_ref[...] * (NA_HD ** -0.5), first)
    s = lax.dot_general(k_ref[...], qm, (((1,), (1,)), ((), ())), preferred_element_type=F32)
    e = jnp.exp(s - jnp.max(s, axis=0, keepdims=True))
    den = jnp.sum(e, axis=0, keepdims=True)
    o = lax.dot_general(e.astype(BF16), v_ref[...], (((0,), (0,)), ((), ())), preferred_element_type=F32)
    o = jnp.where(first, o[:SEQ], o[SEQ:])
    den_t = jnp.transpose(jnp.broadcast_to(den, (LANES, 2 * SEQ)))
    o_ref[...] = (o / jnp.where(first, den_t[:SEQ], den_t[SEQ:])).astype(o_ref.dtype)


def _ctx_attention(proj):
    def col(cb):
        return pl.BlockSpec((None, SEQ, LANES), lambda s, p: (cb + p, s, 0))

    return pl.pallas_call(
        _ctx_attn_kernel, grid=(BATCH, NA_HEADS // 2),
        in_specs=[col(_QB), col(_KB), col(_VB)],
        out_specs=pl.BlockSpec((None, SEQ, LANES), lambda s, p: (p, s, 0)),
        out_shape=jax.ShapeDtypeStruct((NA_HEADS // 2, N_CTX, LANES), BF16),
        compiler_params=_params(("parallel", "parallel")), name="ctx_attention",
    )(proj, proj, proj)


_NA_UNROLL = 4


def _na_kernel(q_ref, k_ref, v_ref, kc_ref, vc_ref, bias_ref, o_ref, kcb_scr, vcb_scr):
    rows = DEC_SEQ // GRID_W
    win = NA_ROWS * GRID_W
    scale = NA_HD ** -0.5
    dn_nt = (((1,), (1,)), ((), ()))
    dn_tn = (((0,), (0,)), ((), ()))

    kcb_scr[...] = kc_ref[...].astype(BF16)
    vcb_scr[...] = vc_ref[...].astype(BF16)
    first = lax.broadcasted_iota(jnp.int32, (GRID_W, LANES), 1) < NA_HD

    def body(it, carry):
        rr = [it * _NA_UNROLL + j for j in range(_NA_UNROLL)]
        rss = [jnp.clip(r - NA_ROWS // 2, 0, rows - NA_ROWS) for r in rr]
        qsls = [pl.ds(pl.multiple_of(r * GRID_W, GRID_W), GRID_W) for r in rr]
        wsls = [pl.ds(pl.multiple_of(rs * GRID_W, GRID_W), win) for rs in rss]
        qms, s_wins, s_ctxs = [], [], []
        for r, rs, qsl, wsl in zip(rr, rss, qsls, wsls):
            qm = _pair_queries(q_ref[qsl, :] * scale, first)
            bias = jnp.concatenate([bias_ref[NA_ROWS - 1 - (r - rs) + i] for i in range(NA_ROWS)], axis=0)
            s_wins.append(lax.dot_general(k_ref[wsl, :], qm, dn_nt, preferred_element_type=F32) + bias)
            s_ctxs.append(lax.dot_general(kcb_scr[...], qm, dn_nt, preferred_element_type=F32))
        ms = [jnp.maximum(jnp.max(sw, axis=0, keepdims=True), jnp.max(sc, axis=0, keepdims=True))
              for sw, sc in zip(s_wins, s_ctxs)]
        e_wins = [jnp.exp(sw - m) for sw, m in zip(s_wins, ms)]
        e_ctxs = [jnp.exp(sc - m) for sc, m in zip(s_ctxs, ms)]
        dens = [jnp.sum(ew, axis=0, keepdims=True) + jnp.sum(ec, axis=0, keepdims=True)
                for ew, ec in zip(e_wins, e_ctxs)]
        for qsl, wsl, ew, ec, den in zip(qsls, wsls, e_wins, e_ctxs, dens):
            o = (lax.dot_general(ew.astype(BF16), v_ref[wsl, :], dn_tn, preferred_element_type=F32)
                 + lax.dot_general(ec.astype(BF16), vcb_scr[...], dn_tn, preferred_element_type=F32))
            o = o / jnp.transpose(jnp.broadcast_to(den, (LANES, LANES)))
            o_ref[qsl, :] = jnp.where(first, o[:GRID_W], o[GRID_W:]).astype(o_ref.dtype)
        return carry

    lax.fori_loop(0, rows // _NA_UNROLL, body, 0)


def _na_bias_table(rpb):
    col = jnp.arange(GRID_W)
    cs = jnp.clip(col - NA_COLS // 2, 0, GRID_W - NA_COLS)
    col_ok = (col[None, :] >= cs[:, None]) & (col[None, :] < cs[:, None] + NA_COLS)
    dc = jnp.clip(col[None, :] - col[:, None] + NA_COLS - 1, 0, 2 * NA_COLS - 2)
    onehot = (dc.T[None, :, :] == jnp.arange(2 * NA_COLS - 1)[:, None, None]).astype(F32)
    t = jnp.einsum('hrd,dkq->hrkq', rpb.astype(F32), onehot, precision=lax.Precision.HIGHEST)
    t = jnp.where(col_ok.T[None, None], t, NEG_INF)
    t = t.reshape(NA_HEADS // 2, 2, 2 * NA_ROWS - 1, GRID_W, GRID_W)
    return jnp.concatenate([t[:, 0], t[:, 1]], axis=-1)


def _na_attention(proj, kctx, vctx, rpb):
    blk = N_CTX // DEC_SEQ

    def col(cb):
        return pl.BlockSpec((None, DEC_SEQ, LANES), lambda b, p: (cb + p, blk + b, 0))

    ctx = pl.BlockSpec((None, PAST_LEN, LANES), lambda b, p: (b, 0, p))
    return pl.pallas_call(
        _na_kernel, grid=(DEC_BATCH, NA_HEADS // 2),
        in_specs=[col(_QB), col(_KB), col(_VB), ctx, ctx,
                  pl.BlockSpec((None, 2 * NA_ROWS - 1, GRID_W, 2 * GRID_W), lambda b, p: (p, 0, 0, 0))],
        out_specs=pl.BlockSpec((None, DEC_SEQ, LANES), lambda b, p: (p, b, 0)),
        out_shape=jax.ShapeDtypeStruct((NA_HEADS // 2, N_LAT, LANES), BF16),
        scratch_shapes=[pltpu.VMEM((PAST_LEN, LANES), BF16), pltpu.VMEM((PAST_LEN, LANES), BF16)],
        compiler_params=_params(("parallel", "parallel")), name="na_attention",
    )(proj, proj, proj, kctx, vctx, _na_bias_table(rpb))


_LOGIT0 = N_EGROUPS
_R_E, _R_W, _R_RANK = 0, 2, 4


def _lane_min_where(mask, lane):
    return jnp.min(jnp.where(mask, lane, LANES), axis=-1, keepdims=True)


def _route_rows(lg, carry_ref, tri_ref):
    big = -3.0e38
    lane = lax.broadcasted_iota(jnp.int32, lg.shape, 1)
    is_g = lane < N_EGROUPS
    gmax = jnp.max(jnp.where(is_g, lg, big), axis=-1, keepdims=True)
    gsum = jnp.sum(jnp.where(is_g, jnp.exp(jnp.where(is_g, lg - gmax, 0.0)), 0.0), axis=-1, keepdims=True)
    pg_top = 1.0 / gsum
    g_idx = _lane_min_where(jnp.logical_and(is_g, lg == gmax), lane)
    in_g = jnp.logical_and(lane >= _LOGIT0, lax.shift_right_arithmetic(lane - _LOGIT0, 3) == g_idx)
    in_g = jnp.logical_and(in_g, lane < _LOGIT0 + N_EXPERTS)
    m1 = jnp.max(jnp.where(in_g, lg, big), axis=-1, keepdims=True)
    i1 = _lane_min_where(jnp.logical_and(in_g, lg == m1), lane)
    rest = jnp.logical_and(in_g, lane != i1)
    m2 = jnp.max(jnp.where(rest, lg, big), axis=-1, keepdims=True)
    i2 = _lane_min_where(jnp.logical_and(rest, lg == m2), lane)
    e2 = jnp.exp(m2 - m1)
    w1 = pg_top * (1.0 / (1.0 + e2))
    w2 = pg_top * (e2 / (1.0 + e2))
    hit1 = lane == i1
    hit2 = lane == i2
    picked = jnp.where(jnp.logical_or(hit1, hit2), 1.0, 0.0)
    before = jnp.dot(tri_ref[...], picked.astype(BF16), preferred_element_type=F32) + carry_ref[...]
    r1 = jnp.sum(jnp.where(hit1, before, 0.0), axis=-1, keepdims=True)
    r2 = jnp.sum(jnp.where(hit2, before, 0.0), axis=-1, keepdims=True)
    carry_ref[...] = carry_ref[...] + jnp.sum(picked, axis=0, keepdims=True)
    rec = jnp.zeros(lg.shape, F32)
    for ln, val in ((_R_E, (i1 - _LOGIT0).astype(F32)), (_R_E + 1, (i2 - _LOGIT0).astype(F32)),
                    (_R_W, w1), (_R_W + 1, w2), (_R_RANK, r1), (_R_RANK + 1, r2)):
        rec = jnp.where(lane == ln, val, rec)
    return rec


_PACK_W = D_MODEL // 2


def _pack_rows(hb):
    lo = lax.bitcast_convert_type(hb[:, :_PACK_W].astype(F32), jnp.int32)
    hi = lax.bitcast_convert_type(hb[:, _PACK_W:].astype(F32), jnp.int32)
    return jnp.bitwise_or(jnp.bitwise_and(hi, -65536), lax.shift_right_logical(lo, 16))


def _unpack_rows(w):
    lo = lax.bitcast_convert_type(lax.shift_left(w, 16), F32)
    hi = lax.bitcast_convert_type(jnp.bitwise_and(w, -65536), F32)
    return jnp.concatenate([lo, hi], axis=-1).astype(BF16)


def _moe_input(xnew, first, tail_in, tail_out, tail_scr):
    g2_ref, sc2_ref, sh2_ref, wrh_ref, wrl_ref, br_ref = tail_in
    x_out, h_out, rec_out, cnt_out = tail_out
    tri_scr, carry_scr = tail_scr

    @pl.when(first)
    def _():
        tm = tri_scr.shape[0]
        r = lax.broadcasted_iota(jnp.int32, (tm, tm), 0)
        c = lax.broadcasted_iota(jnp.int32, (tm, tm), 1)
        tri_scr[...] = jnp.where(c < r, 1.0, 0.0).astype(BF16)
        carry_scr[...] = jnp.zeros(carry_scr.shape, F32)

    x_out[...] = xnew
    h = _rms(xnew, g2_ref[...]) * (1.0 + sc2_ref[...]) + sh2_ref[...]
    hh = h.astype(BF16)
    hl = (h - hh.astype(F32)).astype(BF16)
    h_out[...] = _pack_rows(hh)
    lg = (jnp.dot(hh, wrh_ref[...], preferred_element_type=F32)
          + jnp.dot(hh, wrl_ref[...], preferred_element_type=F32)
          + jnp.dot(hl, wrh_ref[...], preferred_element_type=F32) + br_ref[...])
    rec_out[...] = _route_rows(lg, carry_scr, tri_scr)
    cnt_out[...] = carry_scr[...]


def _even_out_kernel(oac_ref, obc_ref, oal_ref, obl_ref, x_ref, w_ref, gate_ref, *rest, ctx_tiles):
    tail_in, tail_out, (w_scr,), tail_scr = rest[:6], rest[6:10], rest[10:11], rest[11:]
    first = pl.program_id(0) == 0

    @pl.when(first)
    def _():
        w_scr[...] = w_ref[...].astype(BF16)

    is_ctx = pl.program_id(0) < ctx_tiles
    parts = [jnp.where(is_ctx, c_ref[hb], l_ref[hb])
             for c_ref, l_ref in ((oac_ref, oal_ref), (obc_ref, obl_ref)) for hb in range(DN_HEADS)]
    mix = jnp.concatenate(parts, axis=-1)
    out = jnp.dot(mix, w_scr[...], preferred_element_type=F32)
    _moe_input(x_ref[...] + gate_ref[...] * out, first, tail_in, tail_out, tail_scr)


def _tail_specs(tm):
    const = lambda shape: pl.BlockSpec(shape, lambda i: (0,) * len(shape))
    in_specs = [_mod_spec(2, tm), const((1, D_MODEL)), _mod_spec(4, tm), _mod_spec(3, tm),
                const((D_MODEL, LANES)), const((D_MODEL, LANES)), const((1, LANES))]
    out_specs = [pl.BlockSpec((tm, D_MODEL), lambda i: (i, 0)),
                 pl.BlockSpec((tm, _PACK_W), lambda i: (i, 0)),
                 pl.BlockSpec((tm, LANES), lambda i: (i, 0)),
                 const((1, LANES))]
    out_shape = [jax.ShapeDtypeStruct((N_TOK, D_MODEL), F32),
                 jax.ShapeDtypeStruct((N_TOK, _PACK_W), jnp.int32),
                 jax.ShapeDtypeStruct((N_TOK, LANES), F32),
                 jax.ShapeDtypeStruct((1, LANES), F32)]
    scratch = [pltpu.VMEM((tm, tm), BF16), pltpu.VMEM((1, LANES), F32)]
    return in_specs, out_specs, out_shape, scratch


def _router_weights(w_rg, b_rg, w_re, b_re):
    pad = LANES - N_EGROUPS - N_EXPERTS
    w = jnp.concatenate([w_rg, w_re, jnp.zeros((D_MODEL, pad), F32)], axis=1)
    b = jnp.concatenate([b_rg, b_re, jnp.zeros((pad,), F32)]).reshape(1, LANES)
    hi = w.astype(BF16)
    lo = (w - hi.astype(F32)).astype(BF16)
    return hi, lo, b


def _even_out(oa_ctx, ob_ctx, oa_lat, ob_lat, x, w_out, mods, g2, router):
    tm = 512
    ctx_tiles = N_CTX // tm
    tail_in, out_specs, out_shape, tail_scr = _tail_specs(tm)
    ctxblk = pl.BlockSpec((DN_HEADS, tm, LANES), lambda i: (0, jnp.minimum(i, ctx_tiles - 1), 0))
    latblk = pl.BlockSpec((DN_HEADS, tm, LANES), lambda i: (0, jnp.maximum(i - ctx_tiles, 0), 0))
    return pl.pallas_call(
        functools.partial(_even_out_kernel, ctx_tiles=ctx_tiles), grid=(N_TOK // tm,),
        in_specs=[ctxblk, ctxblk, latblk, latblk, pl.BlockSpec((tm, D_MODEL), lambda i: (i, 0)),
                  pl.BlockSpec((D_MODEL, D_MODEL), lambda i: (0, 0))] + tail_in,
        out_specs=out_specs, out_shape=out_shape,
        scratch_shapes=[pltpu.VMEM((D_MODEL, D_MODEL), BF16)] + tail_scr,
        compiler_params=_params(("arbitrary",)), name="even_out",
    )(oa_ctx, ob_ctx, oa_lat, ob_lat, x, w_out, mods, g2.reshape(1, D_MODEL), mods, mods, *router)


def _gelu_tanh(x):
    return x * (0.5 * (1.0 + jnp.tanh(0.7978845608028654 * (x + 0.044715 * (x * x * x)))))


def _sgu_kernel(x_ref, g1_ref, sh1_ref, sc1_ref, win_ref, lng_ref, lnb_ref, ws_ref, bst_ref, wout_ref, gate_ref,
                *rest, tm):
    tail_in, tail_out, (v_scr, m_scr), tail_scr = rest[:6], rest[6:10], rest[10:12], rest[12:]
    first = pl.program_id(0) == 0
    x = x_ref[...]
    h = (_rms(x, g1_ref[...]) * (1.0 + sc1_ref[...]) + sh1_ref[...]).astype(BF16)

    v = _gelu_tanh(jnp.dot(h, win_ref[:, SG_W:], preferred_element_type=F32))
    mu = jnp.mean(v, axis=-1, keepdims=True)
    vc = v - mu
    var = jnp.mean(vc * vc, axis=-1, keepdims=True)
    v_scr[...] = (vc * lax.rsqrt(var + EPS) * lng_ref[...] + lnb_ref[...]).astype(BF16)

    for g in range(SG_GROUPS):
        cs = slice(g * SG_GW, (g + 1) * SG_GW)
        u = _gelu_tanh(jnp.dot(h, win_ref[:, cs], preferred_element_type=F32))
        w_sp = ws_ref[g].astype(BF16)
        for c in range(tm // SG_CHUNK):
            rs = slice(c * SG_CHUNK, (c + 1) * SG_CHUNK)
            sp = jnp.dot(w_sp, v_scr[rs, cs], preferred_element_type=F32) + bst_ref[:, g:g + 1]
            m_scr[rs, cs] = (u[rs] * sp).astype(BF16)
    out = jnp.dot(m_scr[...], wout_ref[...], preferred_element_type=F32)
    _moe_input(x + gate_ref[...] * out, first, tail_in, tail_out, tail_scr)


def _sgu_layer(x, mods, g1, w_in, ln_g, ln_b, w_s, b_s, w_out, g2, router):
    tm = 512
    tail_in, out_specs, out_shape, tail_scr = _tail_specs(tm)
    const = lambda shape: pl.BlockSpec(shape, lambda i: (0,) * len(shape))
    held = lambda shape: pl.BlockSpec(shape, lambda i: (0,) * len(shape), pipeline_mode=pl.Buffered(1))
    return pl.pallas_call(
        functools.partial(_sgu_kernel, tm=tm), grid=(N_TOK // tm,),
        in_specs=[pl.BlockSpec((tm, D_MODEL), lambda i: (i, 0)),
                  const((1, D_MODEL)), _mod_spec(0, tm), _mod_spec(1, tm),
                  held((D_MODEL, 2 * SG_W)), const((1, SG_W)), const((1, SG_W)),
                  const((SG_GROUPS, SG_CHUNK, SG_CHUNK)), const((SG_CHUNK, SG_GROUPS)),
                  held((SG_W, D_MODEL))] + tail_in,
        out_specs=out_specs, out_shape=out_shape,
        scratch_shapes=[pltpu.VMEM((tm, SG_W), BF16), pltpu.VMEM((tm, SG_W), BF16)] + tail_scr,
        compiler_params=_params(("arbitrary",)), name="sgu_layer",
    )(x, g1.reshape(1, D_MODEL), mods, mods, w_in.astype(BF16), ln_g.reshape(1, SG_W), ln_b.reshape(1, SG_W),
      w_s, b_s.T, w_out.astype(BF16), mods, g2.reshape(1, D_MODEL), mods, mods, *router)


def _plan(rec, cnt):
    e_idx = rec[:, _R_E:_R_E + 2].astype(jnp.int32)
    rank = rec[:, _R_RANK:_R_RANK + 2].astype(jnp.int32)
    counts = cnt[0, _LOGIT0:_LOGIT0 + N_EXPERTS].astype(jnp.int32)
    padded = (counts + MOE_BLK - 1) // MOE_BLK * MOE_BLK
    pad_end = jnp.cumsum(padded)
    pad_start = pad_end - padded
    hit = e_idx[:, :, None] == jnp.arange(N_EXPERTS, dtype=jnp.int32)[None, None, :]
    dest = jnp.sum(jnp.where(hit, pad_start[None, None, :], 0), axis=-1) + rank
    blk0 = jnp.arange(MOE_NBLK, dtype=jnp.int32) * MOE_BLK
    blk_e = jnp.minimum(jnp.sum((pad_end[None, :] <= blk0[:, None]).astype(jnp.int32), axis=-1),
                        N_EXPERTS - 1)
    n_used = (pad_end[-1] // MOE_BLK).astype(jnp.int32).reshape(1)
    owns = counts > 0
    slot_of = (jnp.cumsum(owns.astype(jnp.int32)) - 1) % 2
    ids = jnp.arange(N_EXPERTS, dtype=jnp.int32)
    later = jnp.logical_and(owns[None, :], ids[None, :] > ids[:, None])
    next_of = jnp.min(jnp.where(later, ids[None, :], N_EXPERTS), axis=-1)
    next_of = jnp.where(next_of == N_EXPERTS, -1, next_of)
    return dest, blk_e, n_used, slot_of[blk_e], next_of[blk_e]


_W_PARTS = 4


def _expert_kernel(blk_e_ref, n_used_ref, slot_ref, next_ref, x_ref, wg_hbm, wu_hbm, wd_hbm, o_ref,
                   wg_buf, wu_buf, wd_buf, wg_scr, wu_scr, wd_scr, sems, *, layer):
    j = pl.program_id(0)
    e = blk_e_ref[j]
    slot = slot_ref[j]
    fresh = jnp.logical_or(j == 0, e != blk_e_ref[jnp.maximum(j - 1, 0)])
    live = j < n_used_ref[0]

    def copies(expert, s):
        out = []
        for m, (hbm, buf) in enumerate(((wg_hbm, wg_buf), (wu_hbm, wu_buf), (wd_hbm, wd_buf))):
            rows = buf.shape[1] // _W_PARTS
            for part in range(_W_PARTS):
                band = pl.ds(part * rows, rows)
                out.append(pltpu.make_async_copy(hbm.at[layer, expert, band], buf.at[s, band],
                                                 sems.at[s, m, part]))
        return out

    @pl.when(j == 0)
    def _():
        for cp in copies(e, slot):
            cp.start()

    @pl.when(jnp.logical_and(fresh, live))
    def _():
        for cp in copies(e, slot):
            cp.wait()
        nxt = next_ref[j]

        @pl.when(nxt >= 0)
        def _():
            for cp in copies(nxt, 1 - slot):
                cp.start()

        wg_scr[...] = wg_buf[slot].astype(BF16)
        wu_scr[...] = wu_buf[slot].astype(BF16)
        wd_scr[...] = wd_buf[slot].astype(BF16)

    @pl.when(live)
    def _():
        x = _unpack_rows(x_ref[...])
        gt = jnp.dot(x, wg_scr[...], preferred_element_type=F32)
        up = jnp.dot(x, wu_scr[...], preferred_element_type=F32)
        hb = (_silu(gt) * up).astype(BF16)
        o_ref[...] = jnp.dot(hb, wd_scr[...], preferred_element_type=F32).astype(o_ref.dtype)

    @pl.when(jnp.logical_not(live))
    def _():
        o_ref[...] = jnp.zeros(o_ref.shape, o_ref.dtype)


def _experts(x_pad, blk_e, n_used, slot, nxt, w_gate, w_up, w_down, layer):
    hbm = pl.BlockSpec(memory_space=pl.ANY)
    grid_spec = pltpu.PrefetchScalarGridSpec(
        num_scalar_prefetch=4, grid=(MOE_NBLK,),
        in_specs=[pl.BlockSpec((MOE_BLK, _PACK_W), lambda j, *_: (j, 0)), hbm, hbm, hbm],
        out_specs=pl.BlockSpec((MOE_BLK, D_MODEL), lambda j, *_: (j, 0)),
        scratch_shapes=[pltpu.VMEM((2, D_MODEL, D_EXPERT), F32), pltpu.VMEM((2, D_MODEL, D_EXPERT), F32),
                        pltpu.VMEM((2, D_EXPERT, D_MODEL), F32),
                        pltpu.VMEM((D_MODEL, D_EXPERT), BF16), pltpu.VMEM((D_MODEL, D_EXPERT), BF16),
                        pltpu.VMEM((D_EXPERT, D_MODEL), BF16), pltpu.SemaphoreType.DMA((2, 3, _W_PARTS))])
    return pl.pallas_call(
        functools.partial(_expert_kernel, layer=layer), grid_spec=grid_spec,
        out_shape=jax.ShapeDtypeStruct((MOE_NBLK * MOE_BLK, D_MODEL), BF16),
        compiler_params=_params(("arbitrary",)), name="experts",
    )(blk_e, n_used, slot, nxt, x_pad, w_gate, w_up, w_down)


def _combine_kernel(x_ref, ya_ref, yb_ref, rec_ref, gate_ref, fg_ref, o_ref, *, final):
    rec = rec_ref[...]
    y = (rec[:, _R_W:_R_W + 1] * ya_ref[...].astype(F32)
         + rec[:, _R_W + 1:_R_W + 2] * yb_ref[...].astype(F32))
    xn = x_ref[...] + gate_ref[...] * y
    o_ref[...] = _rms(xn, fg_ref[...]) if final else xn


def _combine(x, ya, yb, rec, mods, final_g, final):
    tm = 512
    blk = pl.BlockSpec((tm, D_MODEL), lambda i: (i, 0))
    return pl.pallas_call(
        functools.partial(_combine_kernel, final=final), grid=(N_TOK // tm,),
        in_specs=[blk, blk, blk, pl.BlockSpec((tm, LANES), lambda i: (i, 0)), _mod_spec(5, tm),
                  pl.BlockSpec((1, D_MODEL), lambda i: (0, 0))],
        out_specs=blk, out_shape=jax.ShapeDtypeStruct((N_TOK, D_MODEL), F32),
        compiler_params=_params(("parallel",)), name="moe_combine",
    )(x, ya, yb, rec, mods, final_g.reshape(1, D_MODEL))


_SC_WORKERS = 32
_SC_CORES = 2
_SC_ROWS = 64


def _dispatch_rows(hp, dest):
    n, width = hp.shape
    per_w = n // _SC_WORKERS
    n_ch = per_w // _SC_ROWS
    idx = dest.T.reshape(2, _SC_WORKERS, n_ch, _SC_ROWS)
    mesh = plsc.VectorSubcoreMesh(core_axis_name="c", subcore_axis_name="s")

    @functools.partial(
        pl.kernel, mesh=mesh, out_type=jax.ShapeDtypeStruct((MOE_NBLK * MOE_BLK, width), hp.dtype),
        scratch_types=[pltpu.VMEM((n_ch, _SC_ROWS), jnp.int32), pltpu.VMEM((n_ch, _SC_ROWS), jnp.int32),
                       pltpu.VMEM((_SC_ROWS, width), hp.dtype)], name="dispatch_rows")
    def scatter(h_hbm, idx_hbm, out_hbm, i0_v, i1_v, rows_v):
        wid = lax.axis_index("s") * _SC_CORES + lax.axis_index("c")
        pltpu.sync_copy(idx_hbm.at[0, wid], i0_v)
        pltpu.sync_copy(idx_hbm.at[1, wid], i1_v)

        @pl.loop(0, n_ch)
        def _(g):
            pltpu.sync_copy(h_hbm.at[pl.ds(wid * per_w + g * _SC_ROWS, _SC_ROWS)], rows_v)
            pltpu.sync_copy(rows_v, out_hbm.at[i0_v.at[g]])
            pltpu.sync_copy(rows_v, out_hbm.at[i1_v.at[g]])

    return scatter(hp, idx)


def _moe(x, h, rec, cnt, mods, w_gate, w_up, w_down, layer, final_g, final):
    dest, blk_e, n_used, slot, nxt = _plan(rec, cnt)
    y_pad = _experts(_dispatch_rows(h, dest), blk_e, n_used, slot, nxt, w_gate, w_up, w_down, layer)
    return _combine(x, y_pad[dest[:, 0]], y_pad[dest[:, 1]], rec, mods, final_g, final)


def kernel(x_prompt, x_sample, c, cache_k, cache_v, state_delta, c_ctx, ada_w, ada_b, norm1_g, norm2_g, final_g,
           ev_w_in, ev_w_out, ev_conv_w, ev_a_log, ev_dt_bias, ev_onorm_g, ev_rpb, od_w_in, od_ln_g, od_ln_b,
           od_w_s, od_b_s, od_w_out, moe_w_rg, moe_b_rg, moe_w_re, moe_b_re, moe_w_gate, moe_w_up, moe_w_down):
    x = jnp.concatenate([x_prompt.reshape(N_CTX, D_MODEL), x_sample.reshape(N_LAT, D_MODEL)], axis=0)
    cond = jnp.concatenate([c_ctx[None, :], c, jnp.zeros((N_COND - 1 - DEC_BATCH, D_MODEL), F32)], axis=0)
    mods_all = _ada_mods(cond, ada_w, ada_b)
    kctx_all = cache_k.reshape(DEC_BATCH, -1, PAST_LEN, NA_HEADS * NA_HD)
    vctx_all = cache_v.reshape(DEC_BATCH, -1, PAST_LEN, NA_HEADS * NA_HD)

    ks, vs, ss = [], [], []
    for l in range(DEPTH):
        mods = mods_all[l]
        router = _router_weights(moe_w_rg[l], moe_b_rg[l], moe_w_re[l], moe_b_re[l])
        if l % 2 == 0:
            e = l // 2
            proj, ab, kv = _even_proj(x, mods, norm1_g[l], ev_w_in[e])
            dn = (proj, ab, ev_conv_w[e], ev_a_log[e], ev_dt_bias[e], ev_onorm_g[e])
            oa_ctx, s_fin = _delta_heads(*dn, SEQ, BATCH, 0, None)
            oa_lat, _ = _delta_heads(*dn, DEC_SEQ, DEC_BATCH, N_CTX // DEC_SEQ, state_delta[:, e])
            ob_ctx = _ctx_attention(proj)
            ob_lat = _na_attention(proj, kctx_all[:, e], vctx_all[:, e], ev_rpb[e])
            x, h, rec, cnt = _even_out(oa_ctx, ob_ctx, oa_lat, ob_lat, x, ev_w_out[e], mods, norm2_g[l],
                                       router)
            na_w = NA_HEADS * NA_HD
            ks.append(kv[:N_CTX, :na_w].reshape(BATCH, SEQ, NA_HEADS, NA_HD))
            vs.append(kv[:N_CTX, na_w:].reshape(BATCH, SEQ, NA_HEADS, NA_HD))
            ss.append(s_fin)
        else:
            o = l // 2
            x, h, rec, cnt = _sgu_layer(x, mods, norm1_g[l], od_w_in[o], od_ln_g[o], od_ln_b[o], od_w_s[o],
                                        od_b_s[o], od_w_out[o], norm2_g[l], router)
        x = _moe(x, h, rec, cnt, mods, moe_w_gate, moe_w_up, moe_w_down, l, final_g, l == DEPTH - 1)

    y_prompt = x[:N_CTX].reshape(BATCH, SEQ, D_MODEL)
    y_sample = x[N_CTX:].reshape(DEC_BATCH, DEC_SEQ, D_MODEL)
    return (y_prompt, y_sample, jnp.stack(ks, axis=1), jnp.stack(vs, axis=1), jnp.stack(ss, axis=1))
```

```python
import functools

import jax
import jax.numpy as jnp
from jax import lax
from jax.experimental import pallas as pl
from jax.experimental.pallas import tpu as pltpu
from jax.experimental.pallas import tpu_sc as plsc

F32 = jnp.float32
BF16 = jnp.bfloat16

D_MODEL = 1024
BATCH = 16
SEQ = 256
DEPTH = 4
DEC_BATCH = 4
DEC_SEQ = 2048
PAST_LEN = 512
GRID_W = 64
EPS = 1e-6
NEG_INF = -1e30

DN_HEADS = 4
DN_DK = 128
DN_CHUNK = 64
NA_HEADS = 8
NA_HD = 64
NA_ROWS = 8
NA_COLS = 16
SG_CHUNK = 128
SG_GROUPS = 8
SG_W = 2 * D_MODEL
SG_GW = SG_W // SG_GROUPS
N_EGROUPS = 4
EXP_PER_GROUP = 8
N_EXPERTS = 32
D_EXPERT = 512

N_CTX = BATCH * SEQ
N_LAT = DEC_BATCH * DEC_SEQ
N_TOK = N_CTX + N_LAT
N_COND = 8
PROJ_W = 4096
LANES = 128
MOE_BLK = 256
MOE_NBLK = -(-(2 * N_TOK + N_EXPERTS * (MOE_BLK - 1)) // MOE_BLK)
VMEM_LIMIT = 56 * 1024 * 1024

_QA, _KA, _VA, _ZA, _QB, _KB, _VB = 0, 4, 8, 12, 16, 20, 24


def _params(sem):
    return pltpu.CompilerParams(dimension_semantics=sem, vmem_limit_bytes=VMEM_LIMIT)


def _bdot(a, b):
    return jnp.dot(a.astype(BF16), b.astype(BF16), preferred_element_type=F32)


def _bdot_nt(a, b):
    return lax.dot_general(a.astype(BF16), b.astype(BF16), (((1,), (1,)), ((), ())),
                           preferred_element_type=F32)


def _bdot_tn(a, b):
    return lax.dot_general(a.astype(BF16), b.astype(BF16), (((0,), (0,)), ((), ())),
                           preferred_element_type=F32)


def _split3(a):
    p0 = a.astype(BF16)
    r = a - p0.astype(F32)
    p1 = r.astype(BF16)
    p2 = (r - p1.astype(F32)).astype(BF16)
    return p0, p1, p2


def _dot3(a, b):
    ah = a.astype(BF16)
    al = (a - ah.astype(F32)).astype(BF16)
    bh = b.astype(BF16)
    bl = (b - bh.astype(F32)).astype(BF16)
    return (jnp.dot(ah, bh, preferred_element_type=F32) + jnp.dot(ah, bl, preferred_element_type=F32)
            + jnp.dot(al, bh, preferred_element_type=F32))


def _mask_bf16(m01):
    return jnp.where(m01, 1.0, 0.0).astype(BF16)


def _xdot(m01, a):
    m = _mask_bf16(m01)
    p0, p1, p2 = _split3(a)
    return (jnp.dot(m, p0, preferred_element_type=F32) + jnp.dot(m, p1, preferred_element_type=F32)
            + jnp.dot(m, p2, preferred_element_type=F32))


def _xdot_nt(m01, a):
    m = _mask_bf16(m01)
    dn = (((1,), (1,)), ((), ()))
    p0, p1, p2 = _split3(a)
    return (lax.dot_general(m, p0, dn, preferred_element_type=F32)
            + lax.dot_general(m, p1, dn, preferred_element_type=F32)
            + lax.dot_general(m, p2, dn, preferred_element_type=F32))


def _xdot_r(a, m01):
    m = _mask_bf16(m01)
    p0, p1, p2 = _split3(a)
    return (jnp.dot(p0, m, preferred_element_type=F32) + jnp.dot(p1, m, preferred_element_type=F32)
            + jnp.dot(p2, m, preferred_element_type=F32))


def _sigmoid(x):
    return 0.5 * jnp.tanh(0.5 * x) + 0.5


def _silu(x):
    return x * _sigmoid(x)


def _rms(x, g):
    return x * lax.rsqrt(jnp.mean(x * x, axis=-1, keepdims=True) + EPS) * g


def _cond_index(row):
    return jnp.where(row < N_CTX, 0, 1 + (row - N_CTX) // DEC_SEQ)


def _mod_spec(k, tm):
    return pl.BlockSpec((None, None, 1, D_MODEL), lambda i, *_: (_cond_index(i * tm), k, 0, 0))


def _ada_kernel(c_ref, w_ref, b_ref, o_ref):
    o_ref[...] = _bdot(_silu(c_ref[...]), w_ref[...]) + b_ref[...]


def _ada_mods(cond, ada_w, ada_b):
    tn = 1536
    out = pl.pallas_call(
        _ada_kernel, grid=(DEPTH, 6 * D_MODEL // tn),
        in_specs=[pl.BlockSpec((N_COND, D_MODEL), lambda l, j: (0, 0)),
                  pl.BlockSpec((None, D_MODEL, tn), lambda l, j: (l, 0, j)),
                  pl.BlockSpec((None, 1, tn), lambda l, j: (l, 0, j))],
        out_specs=pl.BlockSpec((None, N_COND, tn), lambda l, j: (l, 0, j)),
        out_shape=jax.ShapeDtypeStruct((DEPTH, N_COND, 6 * D_MODEL), F32),
        compiler_params=_params(("parallel", "parallel")), name="ada_mods",
    )(cond, ada_w, ada_b.reshape(DEPTH, 1, 6 * D_MODEL))
    return out.reshape(DEPTH, N_COND, 6, 1, D_MODEL)


_EV_TN = 512
_EV_W = 7 * DN_HEADS * LANES
_KV_COL0 = _KB * LANES


def _even_proj_kernel(x_ref, g_ref, sh_ref, sc_ref, w_ref, wab_ref, o_ref, ab_ref, kv_ref):
    h = (_rms(x_ref[...], g_ref[...]) * (1.0 + sc_ref[...]) + sh_ref[...]).astype(BF16)
    ab_ref[...] = jnp.dot(h, wab_ref[...], preferred_element_type=F32)
    for j in range(_EV_W // _EV_TN):
        c0 = j * _EV_TN
        y = jnp.dot(h, w_ref[:, c0:c0 + _EV_TN], preferred_element_type=F32)
        for c in range(_EV_TN // LANES):
            o_ref[c0 // LANES + c] = y[:, c * LANES:(c + 1) * LANES].astype(BF16)
        if c0 >= _KV_COL0:
            kv_ref[:, c0 - _KV_COL0:c0 - _KV_COL0 + _EV_TN] = y


def _even_proj(x, mods, g, w_in):
    tm = 512
    n_ab = 4 * DN_HEADS
    ab0 = 4 * DN_HEADS * DN_DK
    w_main = jnp.concatenate([w_in[:, :ab0], w_in[:, ab0 + n_ab:]], axis=1).astype(BF16)
    w_ab = jnp.concatenate([w_in[:, ab0:ab0 + n_ab], jnp.zeros((D_MODEL, LANES - n_ab), F32)],
                           axis=1).astype(BF16)
    held = lambda shape: pl.BlockSpec(shape, lambda i: (0,) * len(shape), pipeline_mode=pl.Buffered(1))
    return pl.pallas_call(
        _even_proj_kernel, grid=(N_TOK // tm,),
        in_specs=[pl.BlockSpec((tm, D_MODEL), lambda i: (i, 0)),
                  pl.BlockSpec((1, D_MODEL), lambda i: (0, 0)),
                  _mod_spec(0, tm), _mod_spec(1, tm),
                  held((D_MODEL, _EV_W)), held((D_MODEL, LANES))],
        out_specs=[pl.BlockSpec((_EV_W // LANES, tm, LANES), lambda i: (0, i, 0)),
                   pl.BlockSpec((tm, LANES), lambda i: (i, 0)),
                   pl.BlockSpec((tm, 2 * NA_HEADS * NA_HD), lambda i: (i, 0))],
        out_shape=[jax.ShapeDtypeStruct((_EV_W // LANES, N_TOK, LANES), BF16),
                   jax.ShapeDtypeStruct((N_TOK, LANES), F32),
                   jax.ShapeDtypeStruct((N_TOK, 2 * NA_HEADS * NA_HD), F32)],
        compiler_params=_params(("parallel",)), name="even_proj",
    )(x, g.reshape(1, D_MODEL), mods, mods, w_main, w_ab)


_CHUNK_SHIFT = DN_CHUNK.bit_length() - 1
_CUM_ROWS = 256
_DN_CHAINS = 16
_DN_SHORT = 256
_SERIES_FINE = 3
_MQ_ROWS = DN_DK + DN_CHUNK


def _dn_kernel(*refs, T, HB, has_s0, want_state):
    it = iter(refs)
    q_ref, k_ref, v_ref, z_ref, ab_ref = (next(it) for _ in range(5))
    cwq_ref, cwk_ref, cwv_ref, alog_ref, dtb_ref, og_ref = (next(it) for _ in range(6))
    s0_ref = next(it) if has_s0 else None
    o_ref = next(it)
    sfin_ref = next(it) if want_state else None
    qc, kc, vc, gsc, bsc, osc, b_s, mq_s = (next(it) for _ in range(8))

    C = DN_CHUNK
    n = T // C
    h0 = pl.program_id(1) * HB

    row = lax.broadcasted_iota(jnp.int32, (T, 1), 0)

    def conv(x_ref, cw_ref, hh):
        x = x_ref[hh].astype(F32)
        cw = cw_ref[:, hh * LANES:(hh + 1) * LANES]
        xp = jnp.where(row == 0, 0.0, pltpu.roll(x, 1, 0))
        xn = jnp.where(row == T - 1, 0.0, pltpu.roll(x, T - 1, 0))
        return _silu(cw[0:1, :] * xp + cw[1:2, :] * x + cw[2:3, :] * xn)

    def l2n(x):
        return x * lax.rsqrt(jnp.sum(x * x, axis=-1, keepdims=True) + EPS)

    ab = ab_ref[...]
    sel_r = lax.broadcasted_iota(jnp.int32, (LANES, LANES), 0)
    for hh in range(HB):
        qc[hh] = l2n(conv(q_ref, cwq_ref, hh)) * (DN_DK ** -0.5)
        kc[hh] = l2n(conv(k_ref, cwk_ref, hh))
        vc[hh] = conv(v_ref, cwv_ref, hh)
        hd = h0 + hh
        for d in range(2):
            alpha = _xdot_r(ab, sel_r == d * DN_HEADS + hd)
            blog = _xdot_r(ab, sel_r == 2 * DN_HEADS + d * DN_HEADS + hd)
            x = alpha + dtb_ref[d, hd]
            sp = jnp.maximum(x, 0.0) + jnp.log1p(jnp.exp(-jnp.abs(x)))
            a = jnp.exp(jnp.full((1, LANES), alog_ref[d, hd], F32))
            gsc[hh, d] = -a * sp
            bsc[hh, d] = _sigmoid(blog)

    pr = lax.broadcasted_iota(jnp.int32, (_CUM_ROWS, _CUM_ROWS), 0)
    pc = lax.broadcasted_iota(jnp.int32, (_CUM_ROWS, _CUM_ROWS), 1)
    same = lax.shift_right_logical(pr, _CHUNK_SHIFT) == lax.shift_right_logical(pc, _CHUNK_SHIFT)
    cum_mask = (jnp.logical_and(same, pc <= pr), jnp.logical_and(same, pc >= pr))

    def cum_body(i, carry):
        sl = pl.ds(pl.multiple_of(i * _CUM_ROWS, _CUM_ROWS), _CUM_ROWS)
        for hh in range(HB):
            for d in range(2):
                gsc[hh, d, sl, :] = _xdot(cum_mask[d], gsc[hh, d, sl, :])
        return carry

    lax.fori_loop(0, T // _CUM_ROWS, cum_body, 0)

    ri = lax.broadcasted_iota(jnp.int32, (C, C), 0)
    ci = lax.broadcasted_iota(jnp.int32, (C, C), 1)
    eye = (ri == ci).astype(F32)

    def prepare(items):
        lows, decays = [], []
        for hh, d, c in items:
            sl = pl.ds(pl.multiple_of(c * C, C), C)
            k, gc = kc[hh, sl, :], gsc[hh, d, sl, :]
            incl = (ci <= ri) if d == 0 else (ci >= ri)
            strict = (ci < ri) if d == 0 else (ci > ri)
            gr = jnp.transpose(gc)[0:1, :C]
            decay = jnp.where(incl, jnp.exp(jnp.where(incl, gc[:, :C] - gr, 0.0)), 0.0)
            lows.append(jnp.where(strict, _bdot_nt(k * bsc[hh, d, sl, :], k) * decay, 0.0))
            decays.append(decay)
        ts = [eye - low for low in lows]
        ps = lows
        for step in range(5):
            dot = _dot3 if step < _SERIES_FINE else _bdot
            ps = [dot(p, p) for p in ps]
            ts = [t + dot(t, p) for t, p in zip(ts, ps)]
        for (hh, d, c), t, decay in zip(items, ts, decays):
            sl = pl.ds(pl.multiple_of(c * C, C), C)
            q, k, gc, beta = qc[hh, sl, :], kc[hh, sl, :], gsc[hh, d, sl, :], bsc[hh, d, sl, :]
            eg = jnp.exp(gc)
            uw = _bdot(t, jnp.concatenate([vc[hh, sl, :] * beta, k * beta * eg], axis=-1))
            last = gc[C - 1:C, :] if d == 0 else gc[0:1, :]
            wu = jnp.concatenate([uw[:, LANES:], uw[:, :LANES]], axis=-1).astype(BF16)
            kd = (k * jnp.exp(last - gc)).astype(BF16)
            attn = (_bdot_nt(q, k) * decay).astype(BF16)
            kdwu = lax.dot_general(kd, wu, (((0,), (0,)), ((), ())), preferred_element_type=F32)
            awu = jnp.dot(attn, wu, preferred_element_type=F32)
            mq0 = pl.multiple_of(c * _MQ_ROWS, _MQ_ROWS)
            mq_s[hh, d, pl.ds(mq0, DN_DK), :] = kdwu[:, :LANES].astype(BF16)
            mq_s[hh, d, pl.ds(mq0 + DN_DK, C), :] = (q * eg - awu[:, :LANES]).astype(BF16)
            b_s[hh, d, pl.ds(pl.multiple_of(c * DN_DK, DN_DK), DN_DK), :] = kdwu[:, LANES:]
            osc[hh, d, sl, :] = awu[:, LANES:]

    n_prep = min(n, _DN_CHAINS // 2)
    h_prep = max(1, min(HB, _DN_CHAINS // (2 * n_prep)))

    def prep_body(i, carry):
        for hg in range(0, HB, h_prep):
            prepare([(hh, d, i * n_prep + j) for hh in range(hg, hg + h_prep) for j in range(n_prep)
                     for d in range(2)])
        return carry

    lax.fori_loop(0, n // n_prep, prep_body, 0)

    def advance(hh, d, c, S):
        sl = pl.ds(pl.multiple_of(c * C, C), C)
        ms = jnp.dot(mq_s[hh, d, pl.ds(pl.multiple_of(c * _MQ_ROWS, _MQ_ROWS), _MQ_ROWS), :], S.astype(BF16),
                     preferred_element_type=F32)
        osc[hh, d, sl, :] = osc[hh, d, sl, :] + ms[DN_DK:]
        last = gsc[hh, d, pl.ds(c * C + (C - 1 if d == 0 else 0), 1), :]
        return (S * jnp.exp(last) - ms[:DN_DK]
                + b_s[hh, d, pl.ds(pl.multiple_of(c * DN_DK, DN_DK), DN_DK), :])

    def body(i, carry):
        return tuple(advance(hh, d, i if d == 0 else n - 1 - i, carry[2 * hh + d])
                     for hh in range(HB) for d in range(2))

    if has_s0:
        init = tuple(s0_ref[d, hh] for hh in range(HB) for d in range(2))
    else:
        init = tuple(jnp.zeros((DN_DK, LANES), F32) for _ in range(2 * HB))
    fin = lax.fori_loop(0, n, body, init)
    for hh in range(HB):
        if want_state:
            sfin_ref[0, hh] = fin[2 * hh]
            sfin_ref[1, hh] = fin[2 * hh + 1]
        o = osc[hh, 0] + osc[hh, 1]
        o_ref[hh] = (_rms(o, og_ref[...]) * _silu(z_ref[hh].astype(F32))).astype(o_ref.dtype)


def _delta_heads(proj, ab, conv_w, a_log, dt_bias, onorm_g, T, n_seq, row0, s0):
    has_s0 = s0 is not None
    want_state = not has_s0
    hb = DN_HEADS if T <= _DN_SHORT else 1

    def col(cb):
        return pl.BlockSpec((hb, T, LANES), lambda s, h: (cb // hb + h, row0 + s, 0))

    def cw(cb):
        return pl.BlockSpec((3, hb * LANES), lambda s, h: (0, cb // hb + h))

    smem = pl.BlockSpec(memory_space=pltpu.SMEM)
    in_specs = [col(_QA), col(_KA), col(_VA), col(_ZA),
                pl.BlockSpec((T, LANES), lambda s, h: (row0 + s, 0)),
                cw(0), cw(4), cw(8), smem, smem,
                pl.BlockSpec((1, LANES), lambda s, h: (0, 0))]
    args = [proj, proj, proj, proj, ab, conv_w, conv_w, conv_w, a_log, dt_bias,
            onorm_g.reshape(1, LANES)]
    state_spec = pl.BlockSpec((None, 2, hb, DN_DK, LANES), lambda s, h: (s, 0, h, 0, 0))
    if has_s0:
        in_specs.append(state_spec)
        args.append(s0)
    out_shape = [jax.ShapeDtypeStruct((DN_HEADS, n_seq * T, LANES), BF16)]
    out_specs = [pl.BlockSpec((hb, T, LANES), lambda s, h: (h, s, 0))]
    if want_state:
        out_shape.append(jax.ShapeDtypeStruct((n_seq, 2, DN_HEADS, DN_DK, LANES), F32))
        out_specs.append(state_spec)
    res = pl.pallas_call(
        functools.partial(_dn_kernel, T=T, HB=hb, has_s0=has_s0, want_state=want_state),
        grid=(n_seq, DN_HEADS // hb), in_specs=in_specs, out_specs=out_specs, out_shape=out_shape,
        scratch_shapes=[pltpu.VMEM((hb, T, LANES), F32)] * 3
        + [pltpu.VMEM((hb, 2, T, LANES), F32)] * 3
        + [pltpu.VMEM((hb, 2, T // DN_CHUNK * DN_DK, LANES), F32),
           pltpu.VMEM((hb, 2, T // DN_CHUNK * _MQ_ROWS, LANES), BF16)],
        compiler_params=_params(("parallel", "parallel")), name="delta_heads_%d" % T,
    )(*args)
    return res if want_state else (res[0], None)


def _pair_queries(q, first):
    return jnp.concatenate([jnp.where(first, q, 0.0), jnp.where(first, 0.0, q)], axis=0).astype(BF16)


def _ctx_attn_kernel(q_ref, k_ref, v_ref, o_ref):
    first = lax.broadcasted_iota(jnp.int32, (SEQ, LANES), 1) < NA_HD
    qm = _pair_queries(q_ref[...] * (NA_HD ** -0.5), first)
    s = lax.dot_general(k_ref[...], qm, (((1,), (1,)), ((), ())), preferred_element_type=F32)
    e = jnp.exp(s - jnp.max(s, axis=0, keepdims=True))
    den = jnp.sum(e, axis=0, keepdims=True)
    o = lax.dot_general(e.astype(BF16), v_ref[...], (((0,), (0,)), ((), ())), preferred_element_type=F32)
    o = jnp.where(first, o[:SEQ], o[SEQ:])
    den_t = jnp.transpose(jnp.broadcast_to(den, (LANES, 2 * SEQ)))
    o_ref[...] = (o / jnp.where(first, den_t[:SEQ], den_t[SEQ:])).astype(o_ref.dtype)


def _ctx_attention(proj):
    def col(cb):
        return pl.BlockSpec((None, SEQ, LANES), lambda s, p: (cb + p, s, 0))

    return pl.pallas_call(
        _ctx_attn_kernel, grid=(BATCH, NA_HEADS // 2),
        in_specs=[col(_QB), col(_KB), col(_VB)],
        out_specs=pl.BlockSpec((None, SEQ, LANES), lambda s, p: (p, s, 0)),
        out_shape=jax.ShapeDtypeStruct((NA_HEADS // 2, N_CTX, LANES), BF16),
        compiler_params=_params(("parallel", "parallel")), name="ctx_attention",
    )(proj, proj, proj)


_NA_UNROLL = 4


def _na_kernel(q_ref, k_ref, v_ref, kc_ref, vc_ref, bias_ref, o_ref, kcb_scr, vcb_scr):
    rows = DEC_SEQ // GRID_W
    win = NA_ROWS * GRID_W
    scale = NA_HD ** -0.5
    dn_nt = (((1,), (1,)), ((), ()))
    dn_tn = (((0,), (0,)), ((), ()))

    kcb_scr[...] = kc_ref[...].astype(BF16)
    vcb_scr[...] = vc_ref[...].astype(BF16)
    first = lax.broadcasted_iota(jnp.int32, (GRID_W, LANES), 1) < NA_HD

    def body(it, carry):
        rr = [it * _NA_UNROLL + j for j in range(_NA_UNROLL)]
        rss = [jnp.clip(r - NA_ROWS // 2, 0, rows - NA_ROWS) for r in rr]
        qsls = [pl.ds(pl.multiple_of(r * GRID_W, GRID_W), GRID_W) for r in rr]
        wsls = [pl.ds(pl.multiple_of(rs * GRID_W, GRID_W), win) for rs in rss]
        qms, s_wins, s_ctxs = [], [], []
        for r, rs, qsl, wsl in zip(rr, rss, qsls, wsls):
            qm = _pair_queries(q_ref[qsl, :] * scale, first)
            bias = jnp.concatenate([bias_ref[NA_ROWS - 1 - (r - rs) + i] for i in range(NA_ROWS)], axis=0)
            s_wins.append(lax.dot_general(k_ref[wsl, :], qm, dn_nt, preferred_element_type=F32) + bias)
            s_ctxs.append(lax.dot_general(kcb_scr[...], qm, dn_nt, preferred_element_type=F32))
        ms = [jnp.maximum(jnp.max(sw, axis=0, keepdims=True), jnp.max(sc, axis=0, keepdims=True))
              for sw, sc in zip(s_wins, s_ctxs)]
        e_wins = [jnp.exp(sw - m) for sw, m in zip(s_wins, ms)]
        e_ctxs = [jnp.exp(sc - m) for sc, m in zip(s_ctxs, ms)]
        dens = [jnp.sum(ew, axis=0, keepdims=True) + jnp.sum(ec, axis=0, keepdims=True)
                for ew, ec in zip(e_wins, e_ctxs)]
        for qsl, wsl, ew, ec, den in zip(qsls, wsls, e_wins, e_ctxs, dens):
            o = (lax.dot_general(ew.astype(BF16), v_ref[wsl, :], dn_tn, preferred_element_type=F32)
                 + lax.dot_general(ec.astype(BF16), vcb_scr[...], dn_tn, preferred_element_type=F32))
            o = o / jnp.transpose(jnp.broadcast_to(den, (LANES, LANES)))
            o_ref[qsl, :] = jnp.where(first, o[:GRID_W], o[GRID_W:]).astype(o_ref.dtype)
        return carry

    lax.fori_loop(0, rows // _NA_UNROLL, body, 0)


def _na_bias_table(rpb):
    col = jnp.arange(GRID_W)
    cs = jnp.clip(col - NA_COLS // 2, 0, GRID_W - NA_COLS)
    col_ok = (col[None, :] >= cs[:, None]) & (col[None, :] < cs[:, None] + NA_COLS)
    dc = jnp.clip(col[None, :] - col[:, None] + NA_COLS - 1, 0, 2 * NA_COLS - 2)
    onehot = (dc.T[None, :, :] == jnp.arange(2 * NA_COLS - 1)[:, None, None]).astype(F32)
    t = jnp.einsum('hrd,dkq->hrkq', rpb.astype(F32), onehot, precision=lax.Precision.HIGHEST)
    t = jnp.where(col_ok.T[None, None], t, NEG_INF)
    t = t.reshape(NA_HEADS // 2, 2, 2 * NA_ROWS - 1, GRID_W, GRID_W)
    return jnp.concatenate([t[:, 0], t[:, 1]], axis=-1)


def _na_attention(proj, kctx, vctx, rpb):
    blk = N_CTX // DEC_SEQ

    def col(cb):
        return pl.BlockSpec((None, DEC_SEQ, LANES), lambda b, p: (cb + p, blk + b, 0))

    ctx = pl.BlockSpec((None, PAST_LEN, LANES), lambda b, p: (b, 0, p))
    return pl.pallas_call(
        _na_kernel, grid=(DEC_BATCH, NA_HEADS // 2),
        in_specs=[col(_QB), col(_KB), col(_VB), ctx, ctx,
                  pl.BlockSpec((None, 2 * NA_ROWS - 1, GRID_W, 2 * GRID_W), lambda b, p: (p, 0, 0, 0))],
        out_specs=pl.BlockSpec((None, DEC_SEQ, LANES), lambda b, p: (p, b, 0)),
        out_shape=jax.ShapeDtypeStruct((NA_HEADS // 2, N_LAT, LANES), BF16),
        scratch_shapes=[pltpu.VMEM((PAST_LEN, LANES), BF16), pltpu.VMEM((PAST_LEN, LANES), BF16)],
        compiler_params=_params(("parallel", "parallel")), name="na_attention",
    )(proj, proj, proj, kctx, vctx, _na_bias_table(rpb))


_LOGIT0 = N_EGROUPS
_R_E, _R_W, _R_RANK = 0, 2, 4


def _lane_min_where(mask, lane):
    return jnp.min(jnp.where(mask, lane, LANES), axis=-1, keepdims=True)


def _route_rows(lg, carry_ref, tri_ref):
    big = -3.0e38
    lane = lax.broadcasted_iota(jnp.int32, lg.shape, 1)
    is_g = lane < N_EGROUPS
    gmax = jnp.max(jnp.where(is_g, lg, big), axis=-1, keepdims=True)
    gsum = jnp.sum(jnp.where(is_g, jnp.exp(jnp.where(is_g, lg - gmax, 0.0)), 0.0), axis=-1, keepdims=True)
    pg_top = 1.0 / gsum
    g_idx = _lane_min_where(jnp.logical_and(is_g, lg == gmax), lane)
    in_g = jnp.logical_and(lane >= _LOGIT0, lax.shift_right_arithmetic(lane - _LOGIT0, 3) == g_idx)
    in_g = jnp.logical_and(in_g, lane < _LOGIT0 + N_EXPERTS)
    m1 = jnp.max(jnp.where(in_g, lg, big), axis=-1, keepdims=True)
    i1 = _lane_min_where(jnp.logical_and(in_g, lg == m1), lane)
    rest = jnp.logical_and(in_g, lane != i1)
    m2 = jnp.max(jnp.where(rest, lg, big), axis=-1, keepdims=True)
    i2 = _lane_min_where(jnp.logical_and(rest, lg == m2), lane)
    e2 = jnp.exp(m2 - m1)
    w1 = pg_top * (1.0 / (1.0 + e2))
    w2 = pg_top * (e2 / (1.0 + e2))
    hit1 = lane == i1
    hit2 = lane == i2
    picked = jnp.where(jnp.logical_or(hit1, hit2), 1.0, 0.0)
    before = jnp.dot(tri_ref[...], picked.astype(BF16), preferred_element_type=F32) + carry_ref[...]
    r1 = jnp.sum(jnp.where(hit1, before, 0.0), axis=-1, keepdims=True)
    r2 = jnp.sum(jnp.where(hit2, before, 0.0), axis=-1, keepdims=True)
    carry_ref[...] = carry_ref[...] + jnp.sum(picked, axis=0, keepdims=True)
    rec = jnp.zeros(lg.shape, F32)
    for ln, val in ((_R_E, (i1 - _LOGIT0).astype(F32)), (_R_E + 1, (i2 - _LOGIT0).astype(F32)),
                    (_R_W, w1), (_R_W + 1, w2), (_R_RANK, r1), (_R_RANK + 1, r2)):
        rec = jnp.where(lane == ln, val, rec)
    return rec


_PACK_W = D_MODEL // 2


def _pack_rows(hb):
    lo = lax.bitcast_convert_type(hb[:, :_PACK_W].astype(F32), jnp.int32)
    hi = lax.bitcast_convert_type(hb[:, _PACK_W:].astype(F32), jnp.int32)
    return jnp.bitwise_or(jnp.bitwise_and(hi, -65536), lax.shift_right_logical(lo, 16))


def _unpack_rows(w):
    lo = lax.bitcast_convert_type(lax.shift_left(w, 16), F32)
    hi = lax.bitcast_convert_type(jnp.bitwise_and(w, -65536), F32)
    return jnp.concatenate([lo, hi], axis=-1).astype(BF16)


def _moe_input(xnew, first, tail_in, tail_out, tail_scr):
    g2_ref, sc2_ref, sh2_ref, wrh_ref, wrl_ref, br_ref = tail_in
    x_out, h_out, rec_out, cnt_out = tail_out
    tri_scr, carry_scr = tail_scr

    @pl.when(first)
    def _():
        tm = tri_scr.shape[0]
        r = lax.broadcasted_iota(jnp.int32, (tm, tm), 0)
        c = lax.broadcasted_iota(jnp.int32, (tm, tm), 1)
        tri_scr[...] = jnp.where(c < r, 1.0, 0.0).astype(BF16)
        carry_scr[...] = jnp.zeros(carry_scr.shape, F32)

    x_out[...] = xnew
    h = _rms(xnew, g2_ref[...]) * (1.0 + sc2_ref[...]) + sh2_ref[...]
    hh = h.astype(BF16)
    hl = (h - hh.astype(F32)).astype(BF16)
    h_out[...] = _pack_rows(hh)
    lg = (jnp.dot(hh, wrh_ref[...], preferred_element_type=F32)
          + jnp.dot(hh, wrl_ref[...], preferred_element_type=F32)
          + jnp.dot(hl, wrh_ref[...], preferred_element_type=F32) + br_ref[...])
    rec_out[...] = _route_rows(lg, carry_scr, tri_scr)
    cnt_out[...] = carry_scr[...]


def _even_out_kernel(oac_ref, obc_ref, oal_ref, obl_ref, x_ref, w_ref, gate_ref, *rest, ctx_tiles):
    tail_in, tail_out, (w_scr,), tail_scr = rest[:6], rest[6:10], rest[10:11], rest[11:]
    first = pl.program_id(0) == 0

    @pl.when(first)
    def _():
        w_scr[...] = w_ref[...].astype(BF16)

    is_ctx = pl.program_id(0) < ctx_tiles
    parts = [jnp.where(is_ctx, c_ref[hb], l_ref[hb])
             for c_ref, l_ref in ((oac_ref, oal_ref), (obc_ref, obl_ref)) for hb in range(DN_HEADS)]
    mix = jnp.concatenate(parts, axis=-1)
    out = jnp.dot(mix, w_scr[...], preferred_element_type=F32)
    _moe_input(x_ref[...] + gate_ref[...] * out, first, tail_in, tail_out, tail_scr)


def _tail_specs(tm):
    const = lambda shape: pl.BlockSpec(shape, lambda i: (0,) * len(shape))
    in_specs = [_mod_spec(2, tm), const((1, D_MODEL)), _mod_spec(4, tm), _mod_spec(3, tm),
                const((D_MODEL, LANES)), const((D_MODEL, LANES)), const((1, LANES))]
    out_specs = [pl.BlockSpec((tm, D_MODEL), lambda i: (i, 0)),
                 pl.BlockSpec((tm, _PACK_W), lambda i: (i, 0)),
                 pl.BlockSpec((tm, LANES), lambda i: (i, 0)),
                 const((1, LANES))]
    out_shape = [jax.ShapeDtypeStruct((N_TOK, D_MODEL), F32),
                 jax.ShapeDtypeStruct((N_TOK, _PACK_W), jnp.int32),
                 jax.ShapeDtypeStruct((N_TOK, LANES), F32),
                 jax.ShapeDtypeStruct((1, LANES), F32)]
    scratch = [pltpu.VMEM((tm, tm), BF16), pltpu.VMEM((1, LANES), F32)]
    return in_specs, out_specs, out_shape, scratch


def _router_weights(w_rg, b_rg, w_re, b_re):
    pad = LANES - N_EGROUPS - N_EXPERTS
    w = jnp.concatenate([w_rg, w_re, jnp.zeros((D_MODEL, pad), F32)], axis=1)
    b = jnp.concatenate([b_rg, b_re, jnp.zeros((pad,), F32)]).reshape(1, LANES)
    hi = w.astype(BF16)
    lo = (w - hi.astype(F32)).astype(BF16)
    return hi, lo, b


def _even_out(oa_ctx, ob_ctx, oa_lat, ob_lat, x, w_out, mods, g2, router):
    tm = 512
    ctx_tiles = N_CTX // tm
    tail_in, out_specs, out_shape, tail_scr = _tail_specs(tm)
    ctxblk = pl.BlockSpec((DN_HEADS, tm, LANES), lambda i: (0, jnp.minimum(i, ctx_tiles - 1), 0))
    latblk = pl.BlockSpec((DN_HEADS, tm, LANES), lambda i: (0, jnp.maximum(i - ctx_tiles, 0), 0))
    return pl.pallas_call(
        functools.partial(_even_out_kernel, ctx_tiles=ctx_tiles), grid=(N_TOK // tm,),
        in_specs=[ctxblk, ctxblk, latblk, latblk, pl.BlockSpec((tm, D_MODEL), lambda i: (i, 0)),
                  pl.BlockSpec((D_MODEL, D_MODEL), lambda i: (0, 0))] + tail_in,
        out_specs=out_specs, out_shape=out_shape,
        scratch_shapes=[pltpu.VMEM((D_MODEL, D_MODEL), BF16)] + tail_scr,
        compiler_params=_params(("arbitrary",)), name="even_out",
    )(oa_ctx, ob_ctx, oa_lat, ob_lat, x, w_out, mods, g2.reshape(1, D_MODEL), mods, mods, *router)


def _gelu_tanh(x):
    return x * (0.5 * (1.0 + jnp.tanh(0.7978845608028654 * (x + 0.044715 * (x * x * x)))))


def _sgu_kernel(x_ref, g1_ref, sh1_ref, sc1_ref, win_ref, lng_ref, lnb_ref, ws_ref, bst_ref, wout_ref, gate_ref,
                *rest, tm):
    tail_in, tail_out, (v_scr, m_scr), tail_scr = rest[:6], rest[6:10], rest[10:12], rest[12:]
    first = pl.program_id(0) == 0
    x = x_ref[...]
    h = (_rms(x, g1_ref[...]) * (1.0 + sc1_ref[...]) + sh1_ref[...]).astype(BF16)

    v = _gelu_tanh(jnp.dot(h, win_ref[:, SG_W:], preferred_element_type=F32))
    mu = jnp.mean(v, axis=-1, keepdims=True)
    vc = v - mu
    var = jnp.mean(vc * vc, axis=-1, keepdims=True)
    v_scr[...] = (vc * lax.rsqrt(var + EPS) * lng_ref[...] + lnb_ref[...]).astype(BF16)

    for g in range(SG_GROUPS):
        cs = slice(g * SG_GW, (g + 1) * SG_GW)
        u = _gelu_tanh(jnp.dot(h, win_ref[:, cs], preferred_element_type=F32))
        w_sp = ws_ref[g].astype(BF16)
        for c in range(tm // SG_CHUNK):
            rs = slice(c * SG_CHUNK, (c + 1) * SG_CHUNK)
            sp = jnp.dot(w_sp, v_scr[rs, cs], preferred_element_type=F32) + bst_ref[:, g:g + 1]
            m_scr[rs, cs] = (u[rs] * sp).astype(BF16)
    out = jnp.dot(m_scr[...], wout_ref[...], preferred_element_type=F32)
    _moe_input(x + gate_ref[...] * out, first, tail_in, tail_out, tail_scr)


def _sgu_layer(x, mods, g1, w_in, ln_g, ln_b, w_s, b_s, w_out, g2, router):
    tm = 512
    tail_in, out_specs, out_shape, tail_scr = _tail_specs(tm)
    const = lambda shape: pl.BlockSpec(shape, lambda i: (0,) * len(shape))
    held = lambda shape: pl.BlockSpec(shape, lambda i: (0,) * len(shape), pipeline_mode=pl.Buffered(1))
    return pl.pallas_call(
        functools.partial(_sgu_kernel, tm=tm), grid=(N_TOK // tm,),
        in_specs=[pl.BlockSpec((tm, D_MODEL), lambda i: (i, 0)),
                  const((1, D_MODEL)), _mod_spec(0, tm), _mod_spec(1, tm),
                  held((D_MODEL, 2 * SG_W)), const((1, SG_W)), const((1, SG_W)),
                  const((SG_GROUPS, SG_CHUNK, SG_CHUNK)), const((SG_CHUNK, SG_GROUPS)),
                  held((SG_W, D_MODEL))] + tail_in,
        out_specs=out_specs, out_shape=out_shape,
        scratch_shapes=[pltpu.VMEM((tm, SG_W), BF16), pltpu.VMEM((tm, SG_W), BF16)] + tail_scr,
        compiler_params=_params(("arbitrary",)), name="sgu_layer",
    )(x, g1.reshape(1, D_MODEL), mods, mods, w_in.astype(BF16), ln_g.reshape(1, SG_W), ln_b.reshape(1, SG_W),
      w_s, b_s.T, w_out.astype(BF16), mods, g2.reshape(1, D_MODEL), mods, mods, *router)


def _plan(rec, cnt):
    e_idx = rec[:, _R_E:_R_E + 2].astype(jnp.int32)
    rank = rec[:, _R_RANK:_R_RANK + 2].astype(jnp.int32)
    counts = cnt[0, _LOGIT0:_LOGIT0 + N_EXPERTS].astype(jnp.int32)
    padded = (counts + MOE_BLK - 1) // MOE_BLK * MOE_BLK
    pad_end = jnp.cumsum(padded)
    pad_start = pad_end - padded
    hit = e_idx[:, :, None] == jnp.arange(N_EXPERTS, dtype=jnp.int32)[None, None, :]
    dest = jnp.sum(jnp.where(hit, pad_start[None, None, :], 0), axis=-1) + rank
    blk0 = jnp.arange(MOE_NBLK, dtype=jnp.int32) * MOE_BLK
    blk_e = jnp.minimum(jnp.sum((pad_end[None, :] <= blk0[:, None]).astype(jnp.int32), axis=-1),
                        N_EXPERTS - 1)
    n_used = (pad_end[-1] // MOE_BLK).astype(jnp.int32).reshape(1)
    owns = counts > 0
    slot_of = (jnp.cumsum(owns.astype(jnp.int32)) - 1) % _W_SLOTS
    ids = jnp.arange(N_EXPERTS, dtype=jnp.int32)
    later = jnp.logical_and(owns[None, :], ids[None, :] > ids[:, None])
    next_of = jnp.min(jnp.where(later, ids[None, :], N_EXPERTS), axis=-1)
    next2_of = jnp.concatenate([next_of, jnp.full((1,), N_EXPERTS, jnp.int32)])[next_of]
    ahead = jnp.stack([next_of, next2_of], axis=0)
    ahead = jnp.where(ahead == N_EXPERTS, -1, ahead)
    return dest, blk_e, n_used, slot_of[blk_e], ahead[:, blk_e].reshape(-1)


_W_PARTS = 4
_W_SLOTS = 3


def _expert_kernel(blk_e_ref, n_used_ref, slot_ref, next_ref, x_ref, wg_hbm, wu_hbm, wd_hbm, o_ref,
                   wg_buf, wu_buf, wd_buf, wg_scr, wu_scr, wd_scr, sems, *, layer):
    j = pl.program_id(0)
    e = blk_e_ref[j]
    slot = slot_ref[j]
    fresh = jnp.logical_or(j == 0, e != blk_e_ref[jnp.maximum(j - 1, 0)])
    live = j < n_used_ref[0]

    def copies(expert, s):
        out = []
        for m, (hbm, buf) in enumerate(((wg_hbm, wg_buf), (wu_hbm, wu_buf), (wd_hbm, wd_buf))):
            rows = buf.shape[1] // _W_PARTS
            for part in range(_W_PARTS):
                band = pl.ds(part * rows, rows)
                out.append(pltpu.make_async_copy(hbm.at[layer, expert, band], buf.at[s, band],
                                                 sems.at[s, m, part]))
        return out

    def start_if_any(expert, s):
        @pl.when(expert >= 0)
        def _():
            for cp in copies(expert, s):
                cp.start()

    @pl.when(j == 0)
    def _():
        for cp in copies(e, slot):
            cp.start()
        start_if_any(next_ref[j], lax.rem(slot + 1, _W_SLOTS))

    @pl.when(jnp.logical_and(fresh, live))
    def _():
        for cp in copies(e, slot):
            cp.wait()
        start_if_any(next_ref[MOE_NBLK + j], lax.rem(slot + 2, _W_SLOTS))

        wg_scr[...] = wg_buf[slot].astype(BF16)
        wu_scr[...] = wu_buf[slot].astype(BF16)
        wd_scr[...] = wd_buf[slot].astype(BF16)

    @pl.when(live)
    def _():
        x = _unpack_rows(x_ref[...])
        gt = jnp.dot(x, wg_scr[...], preferred_element_type=F32)
        up = jnp.dot(x, wu_scr[...], preferred_element_type=F32)
        hb = (_silu(gt) * up).astype(BF16)
        o_ref[...] = jnp.dot(hb, wd_scr[...], preferred_element_type=F32).astype(o_ref.dtype)

    @pl.when(jnp.logical_not(live))
    def _():
        o_ref[...] = jnp.zeros(o_ref.shape, o_ref.dtype)


def _experts(x_pad, blk_e, n_used, slot, nxt, w_gate, w_up, w_down, layer):
    hbm = pl.BlockSpec(memory_space=pl.ANY)
    grid_spec = pltpu.PrefetchScalarGridSpec(
        num_scalar_prefetch=4, grid=(MOE_NBLK,),
        in_specs=[pl.BlockSpec((MOE_BLK, _PACK_W), lambda j, *_: (j, 0)), hbm, hbm, hbm],
        out_specs=pl.BlockSpec((MOE_BLK, D_MODEL), lambda j, *_: (j, 0)),
        scratch_shapes=[pltpu.VMEM((_W_SLOTS, D_MODEL, D_EXPERT), F32),
                        pltpu.VMEM((_W_SLOTS, D_MODEL, D_EXPERT), F32),
                        pltpu.VMEM((_W_SLOTS, D_EXPERT, D_MODEL), F32),
                        pltpu.VMEM((D_MODEL, D_EXPERT), BF16), pltpu.VMEM((D_MODEL, D_EXPERT), BF16),
                        pltpu.VMEM((D_EXPERT, D_MODEL), BF16),
                        pltpu.SemaphoreType.DMA((_W_SLOTS, 3, _W_PARTS))])
    return pl.pallas_call(
        functools.partial(_expert_kernel, layer=layer), grid_spec=grid_spec,
        out_shape=jax.ShapeDtypeStruct((MOE_NBLK * MOE_BLK, D_MODEL), BF16),
        compiler_params=_params(("arbitrary",)), name="experts",
    )(blk_e, n_used, slot, nxt, x_pad, w_gate, w_up, w_down)


def _combine_kernel(x_ref, ya_ref, yb_ref, rec_ref, gate_ref, fg_ref, o_ref, *, final):
    rec = rec_ref[...]
    y = (rec[:, _R_W:_R_W + 1] * ya_ref[...].astype(F32)
         + rec[:, _R_W + 1:_R_W + 2] * yb_ref[...].astype(F32))
    xn = x_ref[...] + gate_ref[...] * y
    o_ref[...] = _rms(xn, fg_ref[...]) if final else xn


def _combine(x, ya, yb, rec, mods, final_g, final):
    tm = 512
    blk = pl.BlockSpec((tm, D_MODEL), lambda i: (i, 0))
    return pl.pallas_call(
        functools.partial(_combine_kernel, final=final), grid=(N_TOK // tm,),
        in_specs=[blk, blk, blk, pl.BlockSpec((tm, LANES), lambda i: (i, 0)), _mod_spec(5, tm),
                  pl.BlockSpec((1, D_MODEL), lambda i: (0, 0))],
        out_specs=blk, out_shape=jax.ShapeDtypeStruct((N_TOK, D_MODEL), F32),
        compiler_params=_params(("parallel",)), name="moe_combine",
    )(x, ya, yb, rec, mods, final_g.reshape(1, D_MODEL))


_SC_WORKERS = 32
_SC_CORES = 2
_SC_ROWS = 64


def _dispatch_rows(hp, dest):
    n, width = hp.shape
    per_w = n // _SC_WORKERS
    n_ch = per_w // _SC_ROWS
    idx = dest.T.reshape(2, _SC_WORKERS, n_ch, _SC_ROWS)
    mesh = plsc.VectorSubcoreMesh(core_axis_name="c", subcore_axis_name="s")

    @functools.partial(
        pl.kernel, mesh=mesh, out_type=jax.ShapeDtypeStruct((MOE_NBLK * MOE_BLK, width), hp.dtype),
        scratch_types=[pltpu.VMEM((n_ch, _SC_ROWS), jnp.int32), pltpu.VMEM((n_ch, _SC_ROWS), jnp.int32),
                       pltpu.VMEM((_SC_ROWS, width), hp.dtype)], name="dispatch_rows")
    def scatter(h_hbm, idx_hbm, out_hbm, i0_v, i1_v, rows_v):
        wid = lax.axis_index("s") * _SC_CORES + lax.axis_index("c")
        pltpu.sync_copy(idx_hbm.at[0, wid], i0_v)
        pltpu.sync_copy(idx_hbm.at[1, wid], i1_v)

        @pl.loop(0, n_ch)
        def _(g):
            pltpu.sync_copy(h_hbm.at[pl.ds(wid * per_w + g * _SC_ROWS, _SC_ROWS)], rows_v)
            pltpu.sync_copy(rows_v, out_hbm.at[i0_v.at[g]])
            pltpu.sync_copy(rows_v, out_hbm.at[i1_v.at[g]])

    return scatter(hp, idx)


def _moe(x, h, rec, cnt, mods, w_gate, w_up, w_down, layer, final_g, final):
    dest, blk_e, n_used, slot, nxt = _plan(rec, cnt)
    y_pad = _experts(_dispatch_rows(h, dest), blk_e, n_used, slot, nxt, w_gate, w_up, w_down, layer)
    return _combine(x, y_pad[dest[:, 0]], y_pad[dest[:, 1]], rec, mods, final_g, final)


def kernel(x_prompt, x_sample, c, cache_k, cache_v, state_delta, c_ctx, ada_w, ada_b, norm1_g, norm2_g, final_g,
           ev_w_in, ev_w_out, ev_conv_w, ev_a_log, ev_dt_bias, ev_onorm_g, ev_rpb, od_w_in, od_ln_g, od_ln_b,
           od_w_s, od_b_s, od_w_out, moe_w_rg, moe_b_rg, moe_w_re, moe_b_re, moe_w_gate, moe_w_up, moe_w_down):
    x = jnp.concatenate([x_prompt.reshape(N_CTX, D_MODEL), x_sample.reshape(N_LAT, D_MODEL)], axis=0)
    cond = jnp.concatenate([c_ctx[None, :], c, jnp.zeros((N_COND - 1 - DEC_BATCH, D_MODEL), F32)], axis=0)
    mods_all = _ada_mods(cond, ada_w, ada_b)
    kctx_all = cache_k.reshape(DEC_BATCH, -1, PAST_LEN, NA_HEADS * NA_HD)
    vctx_all = cache_v.reshape(DEC_BATCH, -1, PAST_LEN, NA_HEADS * NA_HD)

    ks, vs, ss = [], [], []
    for l in range(DEPTH):
        mods = mods_all[l]
        router = _router_weights(moe_w_rg[l], moe_b_rg[l], moe_w_re[l], moe_b_re[l])
        if l % 2 == 0:
            e = l // 2
            proj, ab, kv = _even_proj(x, mods, norm1_g[l], ev_w_in[e])
            dn = (proj, ab, ev_conv_w[e], ev_a_log[e], ev_dt_bias[e], ev_onorm_g[e])
            oa_ctx, s_fin = _delta_heads(*dn, SEQ, BATCH, 0, None)
            oa_lat, _ = _delta_heads(*dn, DEC_SEQ, DEC_BATCH, N_CTX // DEC_SEQ, state_delta[:, e])
            ob_ctx = _ctx_attention(proj)
            ob_lat = _na_attention(proj, kctx_all[:, e], vctx_all[:, e], ev_rpb[e])
            x, h, rec, cnt = _even_out(oa_ctx, ob_ctx, oa_lat, ob_lat, x, ev_w_out[e], mods, norm2_g[l],
                                       router)
            na_w = NA_HEADS * NA_HD
            ks.append(kv[:N_CTX, :na_w].reshape(BATCH, SEQ, NA_HEADS, NA_HD))
            vs.append(kv[:N_CTX, na_w:].reshape(BATCH, SEQ, NA_HEADS, NA_HD))
            ss.append(s_fin)
        else:
            o = l // 2
            x, h, rec, cnt = _sgu_layer(x, mods, norm1_g[l], od_w_in[o], od_ln_g[o], od_ln_b[o], od_w_s[o],
                                        od_b_s[o], od_w_out[o], norm2_g[l], router)
        x = _moe(x, h, rec, cnt, mods, moe_w_gate, moe_w_up, moe_w_down, l, final_g, l == DEPTH - 1)

    y_prompt = x[:N_CTX].reshape(BATCH, SEQ, D_MODEL)
    y_sample = x[N_CTX:].reshape(DEC_BATCH, DEC_SEQ, D_MODEL)
    return (y_prompt, y_sample, jnp.stack(ks, axis=1), jnp.stack(vs, axis=1), jnp.stack(ss, axis=1))
```

```python
import functools

import jax
import jax.numpy as jnp
from jax import lax
from jax.experimental import pallas as pl
from jax.experimental.pallas import tpu as pltpu
from jax.experimental.pallas import tpu_sc as plsc

F32 = jnp.float32
BF16 = jnp.bfloat16

D_MODEL = 1024
BATCH = 16
SEQ = 256
DEPTH = 4
DEC_BATCH = 4
DEC_SEQ = 2048
PAST_LEN = 512
GRID_W = 64
EPS = 1e-6
NEG_INF = -1e30

DN_HEADS = 4
DN_DK = 128
DN_CHUNK = 64
NA_HEADS = 8
NA_HD = 64
NA_ROWS = 8
NA_COLS = 16
SG_CHUNK = 128
SG_GROUPS = 8
SG_W = 2 * D_MODEL
SG_GW = SG_W // SG_GROUPS
N_EGROUPS = 4
EXP_PER_GROUP = 8
N_EXPERTS = 32
D_EXPERT = 512

N_CTX = BATCH * SEQ
N_LAT = DEC_BATCH * DEC_SEQ
N_TOK = N_CTX + N_LAT
N_COND = 8
PROJ_W = 4096
LANES = 128
MOE_BLK = 256
MOE_NBLK = -(-(2 * N_TOK + N_EXPERTS * (MOE_BLK - 1)) // MOE_BLK)
VMEM_LIMIT = 56 * 1024 * 1024

_QA, _KA, _VA, _ZA, _QB, _KB, _VB = 0, 4, 8, 12, 16, 20, 24


def _params(sem):
    return pltpu.CompilerParams(dimension_semantics=sem, vmem_limit_bytes=VMEM_LIMIT)


def _bdot(a, b):
    return jnp.dot(a.astype(BF16), b.astype(BF16), preferred_element_type=F32)


def _bdot_nt(a, b):
    return lax.dot_general(a.astype(BF16), b.astype(BF16), (((1,), (1,)), ((), ())),
                           preferred_element_type=F32)


def _bdot_tn(a, b):
    return lax.dot_general(a.astype(BF16), b.astype(BF16), (((0,), (0,)), ((), ())),
                           preferred_element_type=F32)


def _split2(a):
    p0 = a.astype(BF16)
    return p0, (a - p0.astype(F32)).astype(BF16)


def _dot3(a, b):
    ah = a.astype(BF16)
    al = (a - ah.astype(F32)).astype(BF16)
    bh = b.astype(BF16)
    bl = (b - bh.astype(F32)).astype(BF16)
    return (jnp.dot(ah, bh, preferred_element_type=F32) + jnp.dot(ah, bl, preferred_element_type=F32)
            + jnp.dot(al, bh, preferred_element_type=F32))


def _mask_bf16(m01):
    return jnp.where(m01, 1.0, 0.0).astype(BF16)


def _xdot(m01, a):
    m = _mask_bf16(m01)
    p0, p1 = _split2(a)
    return jnp.dot(m, p0, preferred_element_type=F32) + jnp.dot(m, p1, preferred_element_type=F32)


def _xdot_r(a, m01):
    m = _mask_bf16(m01)
    p0, p1 = _split2(a)
    return jnp.dot(p0, m, preferred_element_type=F32) + jnp.dot(p1, m, preferred_element_type=F32)


def _sigmoid(x):
    return 0.5 * jnp.tanh(0.5 * x) + 0.5


def _silu(x):
    return x * _sigmoid(x)


def _rms(x, g):
    return x * lax.rsqrt(jnp.mean(x * x, axis=-1, keepdims=True) + EPS) * g


def _cond_index(row):
    return jnp.where(row < N_CTX, 0, 1 + (row - N_CTX) // DEC_SEQ)


def _mod_spec(k, tm):
    return pl.BlockSpec((None, None, 1, D_MODEL), lambda i, *_: (_cond_index(i * tm), k, 0, 0))


def _ada_kernel(c_ref, w_ref, b_ref, o_ref):
    o_ref[...] = _bdot(_silu(c_ref[...]), w_ref[...]) + b_ref[...]


def _ada_mods(cond, ada_w, ada_b):
    tn = 1536
    out = pl.pallas_call(
        _ada_kernel, grid=(DEPTH, 6 * D_MODEL // tn),
        in_specs=[pl.BlockSpec((N_COND, D_MODEL), lambda l, j: (0, 0)),
                  pl.BlockSpec((None, D_MODEL, tn), lambda l, j: (l, 0, j)),
                  pl.BlockSpec((None, 1, tn), lambda l, j: (l, 0, j))],
        out_specs=pl.BlockSpec((None, N_COND, tn), lambda l, j: (l, 0, j)),
        out_shape=jax.ShapeDtypeStruct((DEPTH, N_COND, 6 * D_MODEL), F32),
        compiler_params=_params(("parallel", "parallel")), name="ada_mods",
    )(cond, ada_w, ada_b.reshape(DEPTH, 1, 6 * D_MODEL))
    return out.reshape(DEPTH, N_COND, 6, 1, D_MODEL)


_EV_TN = 512
_EV_W = 7 * DN_HEADS * LANES
_KV_COL0 = _KB * LANES


def _even_proj_kernel(x_ref, g_ref, sh_ref, sc_ref, w_ref, wab_ref, o_ref, ab_ref, kv_ref):
    h = (_rms(x_ref[...], g_ref[...]) * (1.0 + sc_ref[...]) + sh_ref[...]).astype(BF16)
    ab_ref[...] = jnp.dot(h, wab_ref[...], preferred_element_type=F32)
    for j in range(_EV_W // _EV_TN):
        c0 = j * _EV_TN
        y = jnp.dot(h, w_ref[:, c0:c0 + _EV_TN], preferred_element_type=F32)
        for c in range(_EV_TN // LANES):
            o_ref[c0 // LANES + c] = y[:, c * LANES:(c + 1) * LANES].astype(BF16)
        if c0 >= _KV_COL0:
            kv_ref[:, c0 - _KV_COL0:c0 - _KV_COL0 + _EV_TN] = y


def _even_proj(x, mods, g, w_in):
    tm = 512
    n_ab = 4 * DN_HEADS
    ab0 = 4 * DN_HEADS * DN_DK
    w_main = jnp.concatenate([w_in[:, :ab0], w_in[:, ab0 + n_ab:]], axis=1).astype(BF16)
    w_ab = jnp.concatenate([w_in[:, ab0:ab0 + n_ab], jnp.zeros((D_MODEL, LANES - n_ab), F32)],
                           axis=1).astype(BF16)
    held = lambda shape: pl.BlockSpec(shape, lambda i: (0,) * len(shape), pipeline_mode=pl.Buffered(1))
    return pl.pallas_call(
        _even_proj_kernel, grid=(N_TOK // tm,),
        in_specs=[pl.BlockSpec((tm, D_MODEL), lambda i: (i, 0)),
                  pl.BlockSpec((1, D_MODEL), lambda i: (0, 0)),
                  _mod_spec(0, tm), _mod_spec(1, tm),
                  held((D_MODEL, _EV_W)), held((D_MODEL, LANES))],
        out_specs=[pl.BlockSpec((_EV_W // LANES, tm, LANES), lambda i: (0, i, 0)),
                   pl.BlockSpec((tm, LANES), lambda i: (i, 0)),
                   pl.BlockSpec((tm, 2 * NA_HEADS * NA_HD), lambda i: (i, 0))],
        out_shape=[jax.ShapeDtypeStruct((_EV_W // LANES, N_TOK, LANES), BF16),
                   jax.ShapeDtypeStruct((N_TOK, LANES), F32),
                   jax.ShapeDtypeStruct((N_TOK, 2 * NA_HEADS * NA_HD), F32)],
        compiler_params=_params(("parallel",)), name="even_proj",
    )(x, g.reshape(1, D_MODEL), mods, mods, w_main, w_ab)


_CHUNK_SHIFT = DN_CHUNK.bit_length() - 1
_CUM_ROWS = 256
_DN_CHAINS = 16
_DN_SHORT = 256
_SERIES_FINE = 3
_MQ_ROWS = DN_DK + DN_CHUNK


def _dn_kernel(*refs, T, HB, has_s0, want_state):
    it = iter(refs)
    q_ref, k_ref, v_ref, z_ref, ab_ref = (next(it) for _ in range(5))
    cwq_ref, cwk_ref, cwv_ref, alog_ref, dtb_ref, og_ref = (next(it) for _ in range(6))
    s0_ref = next(it) if has_s0 else None
    o_ref = next(it)
    sfin_ref = next(it) if want_state else None
    qc, kc, vc, gsc, bsc, osc, b_s, mq_s = (next(it) for _ in range(8))

    C = DN_CHUNK
    n = T // C
    h0 = pl.program_id(1) * HB

    row = lax.broadcasted_iota(jnp.int32, (T, 1), 0)

    def conv(x_ref, cw_ref, hh):
        x = x_ref[hh].astype(F32)
        cw = cw_ref[:, hh * LANES:(hh + 1) * LANES]
        xp = jnp.where(row == 0, 0.0, pltpu.roll(x, 1, 0))
        xn = jnp.where(row == T - 1, 0.0, pltpu.roll(x, T - 1, 0))
        return _silu(cw[0:1, :] * xp + cw[1:2, :] * x + cw[2:3, :] * xn)

    def l2n(x):
        return x * lax.rsqrt(jnp.sum(x * x, axis=-1, keepdims=True) + EPS)

    ab = ab_ref[...]
    sel_r = lax.broadcasted_iota(jnp.int32, (LANES, LANES), 0)
    for hh in range(HB):
        qc[hh] = l2n(conv(q_ref, cwq_ref, hh)) * (DN_DK ** -0.5)
        kc[hh] = l2n(conv(k_ref, cwk_ref, hh))
        vc[hh] = conv(v_ref, cwv_ref, hh)
        hd = h0 + hh
        for d in range(2):
            alpha = _xdot_r(ab, sel_r == d * DN_HEADS + hd)
            blog = _xdot_r(ab, sel_r == 2 * DN_HEADS + d * DN_HEADS + hd)
            x = alpha + dtb_ref[d, hd]
            sp = jnp.maximum(x, 0.0) + jnp.log1p(jnp.exp(-jnp.abs(x)))
            a = jnp.exp(jnp.full((1, LANES), alog_ref[d, hd], F32))
            gsc[hh, d] = -a * sp
            bsc[hh, d] = _sigmoid(blog)

    pr = lax.broadcasted_iota(jnp.int32, (_CUM_ROWS, _CUM_ROWS), 0)
    pc = lax.broadcasted_iota(jnp.int32, (_CUM_ROWS, _CUM_ROWS), 1)
    same = lax.shift_right_logical(pr, _CHUNK_SHIFT) == lax.shift_right_logical(pc, _CHUNK_SHIFT)
    cum_mask = (jnp.logical_and(same, pc <= pr), jnp.logical_and(same, pc >= pr))

    def cum_body(i, carry):
        sl = pl.ds(pl.multiple_of(i * _CUM_ROWS, _CUM_ROWS), _CUM_ROWS)
        for hh in range(HB):
            for d in range(2):
                gsc[hh, d, sl, :] = _xdot(cum_mask[d], gsc[hh, d, sl, :])
        return carry

    lax.fori_loop(0, T // _CUM_ROWS, cum_body, 0)

    ri = lax.broadcasted_iota(jnp.int32, (C, C), 0)
    ci = lax.broadcasted_iota(jnp.int32, (C, C), 1)
    eye = (ri == ci).astype(F32)

    def prepare(items):
        lows, decays = [], []
        kk, qk = {}, {}
        for hh, d, c, slot in items:
            sl = pl.ds(pl.multiple_of(c * C, C), C)
            gc = gsc[hh, d, sl, :]
            if (hh, slot) not in kk:
                k = kc[hh, sl, :]
                kk[hh, slot] = _bdot_nt(k, k)
                qk[hh, slot] = _bdot_nt(qc[hh, sl, :], k)
            incl = (ci <= ri) if d == 0 else (ci >= ri)
            strict = (ci < ri) if d == 0 else (ci > ri)
            gr = jnp.transpose(gc)[0:1, :C]
            decay = jnp.where(incl, jnp.exp(jnp.where(incl, gc[:, :C] - gr, 0.0)), 0.0)
            lows.append(jnp.where(strict, bsc[hh, d, sl, :C] * kk[hh, slot] * decay, 0.0))
            decays.append(decay)
        ts = [eye - low for low in lows]
        ps = lows
        for step in range(5):
            dot = _dot3 if step < _SERIES_FINE else _bdot
            ps = [dot(p, p) for p in ps]
            ts = [t + dot(t, p) for t, p in zip(ts, ps)]
        for (hh, d, c, slot), t, decay in zip(items, ts, decays):
            sl = pl.ds(pl.multiple_of(c * C, C), C)
            q, k, gc, beta = qc[hh, sl, :], kc[hh, sl, :], gsc[hh, d, sl, :], bsc[hh, d, sl, :]
            eg = jnp.exp(gc)
            uw = _bdot(t, jnp.concatenate([vc[hh, sl, :] * beta, k * beta * eg], axis=-1))
            last = gc[C - 1:C, :] if d == 0 else gc[0:1, :]
            wu = jnp.concatenate([uw[:, LANES:], uw[:, :LANES]], axis=-1).astype(BF16)
            kd = (k * jnp.exp(last - gc)).astype(BF16)
            attn = (qk[hh, slot] * decay).astype(BF16)
            kdwu = lax.dot_general(kd, wu, (((0,), (0,)), ((), ())), preferred_element_type=F32)
            awu = jnp.dot(attn, wu, preferred_element_type=F32)
            mq0 = pl.multiple_of(c * _MQ_ROWS, _MQ_ROWS)
            mq_s[hh, d, pl.ds(mq0, DN_DK), :] = kdwu[:, :LANES].astype(BF16)
            mq_s[hh, d, pl.ds(mq0 + DN_DK, C), :] = (q * eg - awu[:, :LANES]).astype(BF16)
            b_s[hh, d, pl.ds(pl.multiple_of(c * DN_DK, DN_DK), DN_DK), :] = kdwu[:, LANES:]
            osc[hh, d, sl, :] = awu[:, LANES:]

    n_prep = min(n, _DN_CHAINS // 2)
    h_prep = max(1, min(HB, _DN_CHAINS // (2 * n_prep)))

    def prep_body(i, carry):
        for hg in range(0, HB, h_prep):
            prepare([(hh, d, i * n_prep + j, j) for hh in range(hg, hg + h_prep) for j in range(n_prep)
                     for d in range(2)])
        return carry

    lax.fori_loop(0, n // n_prep, prep_body, 0)

    def advance(hh, d, c, S):
        sl = pl.ds(pl.multiple_of(c * C, C), C)
        ms = jnp.dot(mq_s[hh, d, pl.ds(pl.multiple_of(c * _MQ_ROWS, _MQ_ROWS), _MQ_ROWS), :], S.astype(BF16),
                     preferred_element_type=F32)
        osc[hh, d, sl, :] = osc[hh, d, sl, :] + ms[DN_DK:]
        last = gsc[hh, d, pl.ds(c * C + (C - 1 if d == 0 else 0), 1), :]
        return (S * jnp.exp(last) - ms[:DN_DK]
                + b_s[hh, d, pl.ds(pl.multiple_of(c * DN_DK, DN_DK), DN_DK), :])

    def body(i, carry):
        return tuple(advance(hh, d, i if d == 0 else n - 1 - i, carry[2 * hh + d])
                     for hh in range(HB) for d in range(2))

    if has_s0:
        init = tuple(s0_ref[d, hh] for hh in range(HB) for d in range(2))
    else:
        init = tuple(jnp.zeros((DN_DK, LANES), F32) for _ in range(2 * HB))
    fin = lax.fori_loop(0, n, body, init)
    for hh in range(HB):
        if want_state:
            sfin_ref[0, hh] = fin[2 * hh]
            sfin_ref[1, hh] = fin[2 * hh + 1]
        o = osc[hh, 0] + osc[hh, 1]
        o_ref[hh] = (_rms(o, og_ref[...]) * _silu(z_ref[hh].astype(F32))).astype(o_ref.dtype)


def _delta_heads(proj, ab, conv_w, a_log, dt_bias, onorm_g, T, n_seq, row0, s0):
    has_s0 = s0 is not None
    want_state = not has_s0
    hb = DN_HEADS if T <= _DN_SHORT else 1

    def col(cb):
        return pl.BlockSpec((hb, T, LANES), lambda s, h: (cb // hb + h, row0 + s, 0))

    def cw(cb):
        return pl.BlockSpec((3, hb * LANES), lambda s, h: (0, cb // hb + h))

    smem = pl.BlockSpec(memory_space=pltpu.SMEM)
    in_specs = [col(_QA), col(_KA), col(_VA), col(_ZA),
                pl.BlockSpec((T, LANES), lambda s, h: (row0 + s, 0)),
                cw(0), cw(4), cw(8), smem, smem,
                pl.BlockSpec((1, LANES), lambda s, h: (0, 0))]
    args = [proj, proj, proj, proj, ab, conv_w, conv_w, conv_w, a_log, dt_bias,
            onorm_g.reshape(1, LANES)]
    state_spec = pl.BlockSpec((None, 2, hb, DN_DK, LANES), lambda s, h: (s, 0, h, 0, 0))
    if has_s0:
        in_specs.append(state_spec)
        args.append(s0)
    out_shape = [jax.ShapeDtypeStruct((DN_HEADS, n_seq * T, LANES), BF16)]
    out_specs = [pl.BlockSpec((hb, T, LANES), lambda s, h: (h, s, 0))]
    if want_state:
        out_shape.append(jax.ShapeDtypeStruct((n_seq, 2, DN_HEADS, DN_DK, LANES), F32))
        out_specs.append(state_spec)
    res = pl.pallas_call(
        functools.partial(_dn_kernel, T=T, HB=hb, has_s0=has_s0, want_state=want_state),
        grid=(n_seq, DN_HEADS // hb), in_specs=in_specs, out_specs=out_specs, out_shape=out_shape,
        scratch_shapes=[pltpu.VMEM((hb, T, LANES), F32)] * 3
        + [pltpu.VMEM((hb, 2, T, LANES), F32)] * 3
        + [pltpu.VMEM((hb, 2, T // DN_CHUNK * DN_DK, LANES), F32),
           pltpu.VMEM((hb, 2, T // DN_CHUNK * _MQ_ROWS, LANES), BF16)],
        compiler_params=_params(("parallel", "parallel")), name="delta_heads_%d" % T,
    )(*args)
    return res if want_state else (res[0], None)


def _pair_queries(q, first):
    return jnp.concatenate([jnp.where(first, q, 0.0), jnp.where(first, 0.0, q)], axis=0).astype(BF16)


def _ctx_attn_kernel(q_ref, k_ref, v_ref, o_ref):
    first = lax.broadcasted_iota(jnp.int32, (SEQ, LANES), 1) < NA_HD
    qm = _pair_queries(q_ref[...] * (NA_HD ** -0.5), first)
    s = lax.dot_general(k_ref[...], qm, (((1,), (1,)), ((), ())), preferred_element_type=F32)
    e = jnp.exp(s - jnp.max(s, axis=0, keepdims=True))
    den = jnp.sum(e, axis=0, keepdims=True)
    o = lax.dot_general(e.astype(BF16), v_ref[...], (((0,), (0,)), ((), ())), preferred_element_type=F32)
    o = jnp.where(first, o[:SEQ], o[SEQ:])
    den_t = jnp.transpose(jnp.broadcast_to(den, (LANES, 2 * SEQ)))
    o_ref[...] = (o / jnp.where(first, den_t[:SEQ], den_t[SEQ:])).astype(o_ref.dtype)


def _ctx_attention(proj):
    def col(cb):
        return pl.BlockSpec((None, SEQ, LANES), lambda s, p: (cb + p, s, 0))

    return pl.pallas_call(
        _ctx_attn_kernel, grid=(BATCH, NA_HEADS // 2),
        in_specs=[col(_QB), col(_KB), col(_VB)],
        out_specs=pl.BlockSpec((None, SEQ, LANES), lambda s, p: (p, s, 0)),
        out_shape=jax.ShapeDtypeStruct((NA_HEADS // 2, N_CTX, LANES), BF16),
        compiler_params=_params(("parallel", "parallel")), name="ctx_attention",
    )(proj, proj, proj)


_NA_UNROLL = 4


def _na_kernel(q_ref, k_ref, v_ref, kc_ref, vc_ref, bias_ref, o_ref, kcb_scr, vcb_scr):
    rows = DEC_SEQ // GRID_W
    win = NA_ROWS * GRID_W
    scale = NA_HD ** -0.5
    dn_nt = (((1,), (1,)), ((), ()))
    dn_tn = (((0,), (0,)), ((), ()))

    kcb_scr[...] = kc_ref[...].astype(BF16)
    vcb_scr[...] = vc_ref[...].astype(BF16)
    first = lax.broadcasted_iota(jnp.int32, (GRID_W, LANES), 1) < NA_HD

    def body(it, carry):
        rr = [it * _NA_UNROLL + j for j in range(_NA_UNROLL)]
        rss = [jnp.clip(r - NA_ROWS // 2, 0, rows - NA_ROWS) for r in rr]
        qsls = [pl.ds(pl.multiple_of(r * GRID_W, GRID_W), GRID_W) for r in rr]
        wsls = [pl.ds(pl.multiple_of(rs * GRID_W, GRID_W), win) for rs in rss]
        qms, s_wins, s_ctxs = [], [], []
        for r, rs, qsl, wsl in zip(rr, rss, qsls, wsls):
            qm = _pair_queries(q_ref[qsl, :] * scale, first)
            bias = jnp.concatenate([bias_ref[NA_ROWS - 1 - (r - rs) + i] for i in range(NA_ROWS)], axis=0)
            s_wins.append(lax.dot_general(k_ref[wsl, :], qm, dn_nt, preferred_element_type=F32) + bias)
            s_ctxs.append(lax.dot_general(kcb_scr[...], qm, dn_nt, preferred_element_type=F32))
        ms = [jnp.maximum(jnp.max(sw, axis=0, keepdims=True), jnp.max(sc, axis=0, keepdims=True))
              for sw, sc in zip(s_wins, s_ctxs)]
        e_wins = [jnp.exp(sw - m) for sw, m in zip(s_wins, ms)]
        e_ctxs = [jnp.exp(sc - m) for sc, m in zip(s_ctxs, ms)]
        dens = [jnp.sum(ew, axis=0, keepdims=True) + jnp.sum(ec, axis=0, keepdims=True)
                for ew, ec in zip(e_wins, e_ctxs)]
        for qsl, wsl, ew, ec, den in zip(qsls, wsls, e_wins, e_ctxs, dens):
            o = (lax.dot_general(ew.astype(BF16), v_ref[wsl, :], dn_tn, preferred_element_type=F32)
                 + lax.dot_general(ec.astype(BF16), vcb_scr[...], dn_tn, preferred_element_type=F32))
            o = o / jnp.transpose(jnp.broadcast_to(den, (LANES, LANES)))
            o_ref[qsl, :] = jnp.where(first, o[:GRID_W], o[GRID_W:]).astype(o_ref.dtype)
        return carry

    lax.fori_loop(0, rows // _NA_UNROLL, body, 0)


def _na_bias_table(rpb):
    col = jnp.arange(GRID_W)
    cs = jnp.clip(col - NA_COLS // 2, 0, GRID_W - NA_COLS)
    col_ok = (col[None, :] >= cs[:, None]) & (col[None, :] < cs[:, None] + NA_COLS)
    dc = jnp.clip(col[None, :] - col[:, None] + NA_COLS - 1, 0, 2 * NA_COLS - 2)
    onehot = (dc.T[None, :, :] == jnp.arange(2 * NA_COLS - 1)[:, None, None]).astype(F32)
    t = jnp.einsum('hrd,dkq->hrkq', rpb.astype(F32), onehot, precision=lax.Precision.HIGHEST)
    t = jnp.where(col_ok.T[None, None], t, NEG_INF)
    t = t.reshape(NA_HEADS // 2, 2, 2 * NA_ROWS - 1, GRID_W, GRID_W)
    return jnp.concatenate([t[:, 0], t[:, 1]], axis=-1)


def _na_attention(proj, kctx, vctx, rpb):
    blk = N_CTX // DEC_SEQ

    def col(cb):
        return pl.BlockSpec((None, DEC_SEQ, LANES), lambda b, p: (cb + p, blk + b, 0))

    ctx = pl.BlockSpec((None, PAST_LEN, LANES), lambda b, p: (b, 0, p))
    return pl.pallas_call(
        _na_kernel, grid=(DEC_BATCH, NA_HEADS // 2),
        in_specs=[col(_QB), col(_KB), col(_VB), ctx, ctx,
                  pl.BlockSpec((None, 2 * NA_ROWS - 1, GRID_W, 2 * GRID_W), lambda b, p: (p, 0, 0, 0))],
        out_specs=pl.BlockSpec((None, DEC_SEQ, LANES), lambda b, p: (p, b, 0)),
        out_shape=jax.ShapeDtypeStruct((NA_HEADS // 2, N_LAT, LANES), BF16),
        scratch_shapes=[pltpu.VMEM((PAST_LEN, LANES), BF16), pltpu.VMEM((PAST_LEN, LANES), BF16)],
        compiler_params=_params(("parallel", "parallel")), name="na_attention",
    )(proj, proj, proj, kctx, vctx, _na_bias_table(rpb))


_LOGIT0 = N_EGROUPS
_R_E, _R_W, _R_RANK = 0, 2, 4


def _lane_min_where(mask, lane):
    return jnp.min(jnp.where(mask, lane, LANES), axis=-1, keepdims=True)


def _route_rows(lg, carry_ref, tri_ref):
    big = -3.0e38
    lane = lax.broadcasted_iota(jnp.int32, lg.shape, 1)
    is_g = lane < N_EGROUPS
    gmax = jnp.max(jnp.where(is_g, lg, big), axis=-1, keepdims=True)
    gsum = jnp.sum(jnp.where(is_g, jnp.exp(jnp.where(is_g, lg - gmax, 0.0)), 0.0), axis=-1, keepdims=True)
    pg_top = 1.0 / gsum
    g_idx = _lane_min_where(jnp.logical_and(is_g, lg == gmax), lane)
    in_g = jnp.logical_and(lane >= _LOGIT0, lax.shift_right_arithmetic(lane - _LOGIT0, 3) == g_idx)
    in_g = jnp.logical_and(in_g, lane < _LOGIT0 + N_EXPERTS)
    m1 = jnp.max(jnp.where(in_g, lg, big), axis=-1, keepdims=True)
    i1 = _lane_min_where(jnp.logical_and(in_g, lg == m1), lane)
    rest = jnp.logical_and(in_g, lane != i1)
    m2 = jnp.max(jnp.where(rest, lg, big), axis=-1, keepdims=True)
    i2 = _lane_min_where(jnp.logical_and(rest, lg == m2), lane)
    e2 = jnp.exp(m2 - m1)
    w1 = pg_top * (1.0 / (1.0 + e2))
    w2 = pg_top * (e2 / (1.0 + e2))
    hit1 = lane == i1
    hit2 = lane == i2
    picked = jnp.where(jnp.logical_or(hit1, hit2), 1.0, 0.0)
    before = jnp.dot(tri_ref[...], picked.astype(BF16), preferred_element_type=F32) + carry_ref[...]
    r1 = jnp.sum(jnp.where(hit1, before, 0.0), axis=-1, keepdims=True)
    r2 = jnp.sum(jnp.where(hit2, before, 0.0), axis=-1, keepdims=True)
    carry_ref[...] = carry_ref[...] + jnp.sum(picked, axis=0, keepdims=True)
    rec = jnp.zeros(lg.shape, F32)
    for ln, val in ((_R_E, (i1 - _LOGIT0).astype(F32)), (_R_E + 1, (i2 - _LOGIT0).astype(F32)),
                    (_R_W, w1), (_R_W + 1, w2), (_R_RANK, r1), (_R_RANK + 1, r2)):
        rec = jnp.where(lane == ln, val, rec)
    return rec


_PACK_W = D_MODEL // 2


def _pack_rows(hb):
    lo = lax.bitcast_convert_type(hb[:, :_PACK_W].astype(F32), jnp.int32)
    hi = lax.bitcast_convert_type(hb[:, _PACK_W:].astype(F32), jnp.int32)
    return jnp.bitwise_or(jnp.bitwise_and(hi, -65536), lax.shift_right_logical(lo, 16))


def _unpack_rows(w):
    lo = lax.bitcast_convert_type(lax.shift_left(w, 16), F32)
    hi = lax.bitcast_convert_type(jnp.bitwise_and(w, -65536), F32)
    return jnp.concatenate([lo, hi], axis=-1).astype(BF16)


def _moe_input(xnew, first, tail_in, tail_out, tail_scr):
    g2_ref, sc2_ref, sh2_ref, wr_ref, br_ref = tail_in
    x_out, h_out, rec_out, cnt_out = tail_out
    tri_scr, carry_scr = tail_scr

    @pl.when(first)
    def _():
        tm = tri_scr.shape[0]
        r = lax.broadcasted_iota(jnp.int32, (tm, tm), 0)
        c = lax.broadcasted_iota(jnp.int32, (tm, tm), 1)
        tri_scr[...] = jnp.where(c < r, 1.0, 0.0).astype(BF16)
        carry_scr[...] = jnp.zeros(carry_scr.shape, F32)

    x_out[...] = xnew
    h = _rms(xnew, g2_ref[...]) * (1.0 + sc2_ref[...]) + sh2_ref[...]
    hh = h.astype(BF16)
    h_out[...] = _pack_rows(hh)
    lg = jnp.dot(hh, wr_ref[...], preferred_element_type=F32) + br_ref[...]
    rec_out[...] = _route_rows(lg, carry_scr, tri_scr)
    cnt_out[...] = carry_scr[...]


def _even_out_kernel(oac_ref, obc_ref, oal_ref, obl_ref, x_ref, w_ref, gate_ref, *rest, ctx_tiles):
    tail_in, tail_out, (w_scr,), tail_scr = rest[:5], rest[5:9], rest[9:10], rest[10:]
    first = pl.program_id(0) == 0

    @pl.when(first)
    def _():
        w_scr[...] = w_ref[...].astype(BF16)

    is_ctx = pl.program_id(0) < ctx_tiles
    parts = [jnp.where(is_ctx, c_ref[hb], l_ref[hb])
             for c_ref, l_ref in ((oac_ref, oal_ref), (obc_ref, obl_ref)) for hb in range(DN_HEADS)]
    mix = jnp.concatenate(parts, axis=-1)
    out = jnp.dot(mix, w_scr[...], preferred_element_type=F32)
    _moe_input(x_ref[...] + gate_ref[...] * out, first, tail_in, tail_out, tail_scr)


def _tail_specs(tm):
    const = lambda shape: pl.BlockSpec(shape, lambda i: (0,) * len(shape))
    in_specs = [_mod_spec(2, tm), const((1, D_MODEL)), _mod_spec(4, tm), _mod_spec(3, tm),
                const((D_MODEL, LANES)), const((1, LANES))]
    out_specs = [pl.BlockSpec((tm, D_MODEL), lambda i: (i, 0)),
                 pl.BlockSpec((tm, _PACK_W), lambda i: (i, 0)),
                 pl.BlockSpec((tm, LANES), lambda i: (i, 0)),
                 const((1, LANES))]
    out_shape = [jax.ShapeDtypeStruct((N_TOK, D_MODEL), F32),
                 jax.ShapeDtypeStruct((N_TOK, _PACK_W), jnp.int32),
                 jax.ShapeDtypeStruct((N_TOK, LANES), F32),
                 jax.ShapeDtypeStruct((1, LANES), F32)]
    scratch = [pltpu.VMEM((tm, tm), BF16), pltpu.VMEM((1, LANES), F32)]
    return in_specs, out_specs, out_shape, scratch


def _router_weights(w_rg, b_rg, w_re, b_re):
    pad = LANES - N_EGROUPS - N_EXPERTS
    w = jnp.concatenate([w_rg, w_re, jnp.zeros((D_MODEL, pad), F32)], axis=1)
    b = jnp.concatenate([b_rg, b_re, jnp.zeros((pad,), F32)]).reshape(1, LANES)
    return w.astype(BF16), b


def _even_out(oa_ctx, ob_ctx, oa_lat, ob_lat, x, w_out, mods, g2, router):
    tm = 512
    ctx_tiles = N_CTX // tm
    tail_in, out_specs, out_shape, tail_scr = _tail_specs(tm)
    ctxblk = pl.BlockSpec((DN_HEADS, tm, LANES), lambda i: (0, jnp.minimum(i, ctx_tiles - 1), 0))
    latblk = pl.BlockSpec((DN_HEADS, tm, LANES), lambda i: (0, jnp.maximum(i - ctx_tiles, 0), 0))
    return pl.pallas_call(
        functools.partial(_even_out_kernel, ctx_tiles=ctx_tiles), grid=(N_TOK // tm,),
        in_specs=[ctxblk, ctxblk, latblk, latblk, pl.BlockSpec((tm, D_MODEL), lambda i: (i, 0)),
                  pl.BlockSpec((D_MODEL, D_MODEL), lambda i: (0, 0))] + tail_in,
        out_specs=out_specs, out_shape=out_shape,
        scratch_shapes=[pltpu.VMEM((D_MODEL, D_MODEL), BF16)] + tail_scr,
        compiler_params=_params(("arbitrary",)), name="even_out",
    )(oa_ctx, ob_ctx, oa_lat, ob_lat, x, w_out, mods, g2.reshape(1, D_MODEL), mods, mods, *router)


def _gelu_tanh(x):
    return x * (0.5 * (1.0 + jnp.tanh(0.7978845608028654 * (x + 0.044715 * (x * x * x)))))


def _sgu_kernel(x_ref, g1_ref, sh1_ref, sc1_ref, win_ref, lng_ref, lnb_ref, ws_ref, bst_ref, wout_ref, gate_ref,
                *rest, tm):
    tail_in, tail_out, (v_scr, m_scr), tail_scr = rest[:5], rest[5:9], rest[9:11], rest[11:]
    first = pl.program_id(0) == 0
    x = x_ref[...]
    h = (_rms(x, g1_ref[...]) * (1.0 + sc1_ref[...]) + sh1_ref[...]).astype(BF16)

    v = _gelu_tanh(jnp.dot(h, win_ref[:, SG_W:], preferred_element_type=F32))
    mu = jnp.mean(v, axis=-1, keepdims=True)
    vc = v - mu
    var = jnp.mean(vc * vc, axis=-1, keepdims=True)
    v_scr[...] = (vc * lax.rsqrt(var + EPS) * lng_ref[...] + lnb_ref[...]).astype(BF16)

    for g in range(SG_GROUPS):
        cs = slice(g * SG_GW, (g + 1) * SG_GW)
        u = _gelu_tanh(jnp.dot(h, win_ref[:, cs], preferred_element_type=F32))
        w_sp = ws_ref[g].astype(BF16)
        for c in range(tm // SG_CHUNK):
            rs = slice(c * SG_CHUNK, (c + 1) * SG_CHUNK)
            sp = jnp.dot(w_sp, v_scr[rs, cs], preferred_element_type=F32) + bst_ref[:, g:g + 1]
            m_scr[rs, cs] = (u[rs] * sp).astype(BF16)
    out = jnp.dot(m_scr[...], wout_ref[...], preferred_element_type=F32)
    _moe_input(x + gate_ref[...] * out, first, tail_in, tail_out, tail_scr)


def _sgu_layer(x, mods, g1, w_in, ln_g, ln_b, w_s, b_s, w_out, g2, router):
    tm = 512
    tail_in, out_specs, out_shape, tail_scr = _tail_specs(tm)
    const = lambda shape: pl.BlockSpec(shape, lambda i: (0,) * len(shape))
    held = lambda shape: pl.BlockSpec(shape, lambda i: (0,) * len(shape), pipeline_mode=pl.Buffered(1))
    return pl.pallas_call(
        functools.partial(_sgu_kernel, tm=tm), grid=(N_TOK // tm,),
        in_specs=[pl.BlockSpec((tm, D_MODEL), lambda i: (i, 0)),
                  const((1, D_MODEL)), _mod_spec(0, tm), _mod_spec(1, tm),
                  held((D_MODEL, 2 * SG_W)), const((1, SG_W)), const((1, SG_W)),
                  const((SG_GROUPS, SG_CHUNK, SG_CHUNK)), const((SG_CHUNK, SG_GROUPS)),
                  held((SG_W, D_MODEL))] + tail_in,
        out_specs=out_specs, out_shape=out_shape,
        scratch_shapes=[pltpu.VMEM((tm, SG_W), BF16), pltpu.VMEM((tm, SG_W), BF16)] + tail_scr,
        compiler_params=_params(("arbitrary",)), name="sgu_layer",
    )(x, g1.reshape(1, D_MODEL), mods, mods, w_in.astype(BF16), ln_g.reshape(1, SG_W), ln_b.reshape(1, SG_W),
      w_s, b_s.T, w_out.astype(BF16), mods, g2.reshape(1, D_MODEL), mods, mods, *router)


def _plan(rec, cnt):
    e_idx = rec[:, _R_E:_R_E + 2].astype(jnp.int32)
    rank = rec[:, _R_RANK:_R_RANK + 2].astype(jnp.int32)
    counts = cnt[0, _LOGIT0:_LOGIT0 + N_EXPERTS].astype(jnp.int32)
    padded = (counts + MOE_BLK - 1) // MOE_BLK * MOE_BLK
    pad_end = jnp.cumsum(padded)
    pad_start = pad_end - padded
    hit = e_idx[:, :, None] == jnp.arange(N_EXPERTS, dtype=jnp.int32)[None, None, :]
    dest = jnp.sum(jnp.where(hit, pad_start[None, None, :], 0), axis=-1) + rank
    blk0 = jnp.arange(MOE_NBLK, dtype=jnp.int32) * MOE_BLK
    blk_e = jnp.minimum(jnp.sum((pad_end[None, :] <= blk0[:, None]).astype(jnp.int32), axis=-1),
                        N_EXPERTS - 1)
    n_used = (pad_end[-1] // MOE_BLK).astype(jnp.int32).reshape(1)
    owns = counts > 0
    slot_of = (jnp.cumsum(owns.astype(jnp.int32)) - 1) % _W_SLOTS
    ids = jnp.arange(N_EXPERTS, dtype=jnp.int32)
    later = jnp.logical_and(owns[None, :], ids[None, :] > ids[:, None])
    next_of = jnp.min(jnp.where(later, ids[None, :], N_EXPERTS), axis=-1)
    next2_of = jnp.concatenate([next_of, jnp.full((1,), N_EXPERTS, jnp.int32)])[next_of]
    ahead = jnp.stack([next_of, next2_of], axis=0)
    ahead = jnp.where(ahead == N_EXPERTS, -1, ahead)
    return dest, blk_e, n_used, slot_of[blk_e], ahead[:, blk_e].reshape(-1)


_W_PARTS = 4
_W_SLOTS = 3


def _expert_kernel(blk_e_ref, n_used_ref, slot_ref, next_ref, x_ref, wg_hbm, wu_hbm, wd_hbm, o_ref,
                   wg_buf, wu_buf, wd_buf, wg_scr, wu_scr, wd_scr, sems, *, layer):
    j = pl.program_id(0)
    e = blk_e_ref[j]
    slot = slot_ref[j]
    fresh = jnp.logical_or(j == 0, e != blk_e_ref[jnp.maximum(j - 1, 0)])
    live = j < n_used_ref[0]

    def copies(expert, s):
        out = []
        for m, (hbm, buf) in enumerate(((wg_hbm, wg_buf), (wu_hbm, wu_buf), (wd_hbm, wd_buf))):
            rows = buf.shape[1] // _W_PARTS
            for part in range(_W_PARTS):
                band = pl.ds(part * rows, rows)
                out.append(pltpu.make_async_copy(hbm.at[layer, expert, band], buf.at[s, band],
                                                 sems.at[s, m, part]))
        return out

    def start_if_any(expert, s):
        @pl.when(expert >= 0)
        def _():
            for cp in copies(expert, s):
                cp.start()

    @pl.when(j == 0)
    def _():
        for cp in copies(e, slot):
            cp.start()
        start_if_any(next_ref[j], lax.rem(slot + 1, _W_SLOTS))

    @pl.when(jnp.logical_and(fresh, live))
    def _():
        for cp in copies(e, slot):
            cp.wait()
        start_if_any(next_ref[MOE_NBLK + j], lax.rem(slot + 2, _W_SLOTS))

        wg_scr[...] = wg_buf[slot].astype(BF16)
        wu_scr[...] = wu_buf[slot].astype(BF16)
        wd_scr[...] = wd_buf[slot].astype(BF16)

    @pl.when(live)
    def _():
        x = _unpack_rows(x_ref[...])
        gt = jnp.dot(x, wg_scr[...], preferred_element_type=F32)
        up = jnp.dot(x, wu_scr[...], preferred_element_type=F32)
        hb = (_silu(gt) * up).astype(BF16)
        o_ref[...] = jnp.dot(hb, wd_scr[...], preferred_element_type=F32).astype(o_ref.dtype)

    @pl.when(jnp.logical_not(live))
    def _():
        o_ref[...] = jnp.zeros(o_ref.shape, o_ref.dtype)


def _experts(x_pad, blk_e, n_used, slot, nxt, w_gate, w_up, w_down, layer):
    hbm = pl.BlockSpec(memory_space=pl.ANY)
    grid_spec = pltpu.PrefetchScalarGridSpec(
        num_scalar_prefetch=4, grid=(MOE_NBLK,),
        in_specs=[pl.BlockSpec((MOE_BLK, _PACK_W), lambda j, *_: (j, 0)), hbm, hbm, hbm],
        out_specs=pl.BlockSpec((MOE_BLK, D_MODEL), lambda j, *_: (j, 0)),
        scratch_shapes=[pltpu.VMEM((_W_SLOTS, D_MODEL, D_EXPERT), F32),
                        pltpu.VMEM((_W_SLOTS, D_MODEL, D_EXPERT), F32),
                        pltpu.VMEM((_W_SLOTS, D_EXPERT, D_MODEL), F32),
                        pltpu.VMEM((D_MODEL, D_EXPERT), BF16), pltpu.VMEM((D_MODEL, D_EXPERT), BF16),
                        pltpu.VMEM((D_EXPERT, D_MODEL), BF16),
                        pltpu.SemaphoreType.DMA((_W_SLOTS, 3, _W_PARTS))])
    return pl.pallas_call(
        functools.partial(_expert_kernel, layer=layer), grid_spec=grid_spec,
        out_shape=jax.ShapeDtypeStruct((MOE_NBLK * MOE_BLK, D_MODEL), BF16),
        compiler_params=_params(("arbitrary",)), name="experts",
    )(blk_e, n_used, slot, nxt, x_pad, w_gate, w_up, w_down)


def _combine_kernel(x_ref, ya_ref, yb_ref, rec_ref, gate_ref, fg_ref, o_ref, *, final):
    rec = rec_ref[...]
    y = (rec[:, _R_W:_R_W + 1] * ya_ref[...].astype(F32)
         + rec[:, _R_W + 1:_R_W + 2] * yb_ref[...].astype(F32))
    xn = x_ref[...] + gate_ref[...] * y
    o_ref[...] = _rms(xn, fg_ref[...]) if final else xn


def _combine(x, ya, yb, rec, mods, final_g, final):
    tm = 512
    blk = pl.BlockSpec((tm, D_MODEL), lambda i: (i, 0))
    return pl.pallas_call(
        functools.partial(_combine_kernel, final=final), grid=(N_TOK // tm,),
        in_specs=[blk, blk, blk, pl.BlockSpec((tm, LANES), lambda i: (i, 0)), _mod_spec(5, tm),
                  pl.BlockSpec((1, D_MODEL), lambda i: (0, 0))],
        out_specs=blk, out_shape=jax.ShapeDtypeStruct((N_TOK, D_MODEL), F32),
        compiler_params=_params(("parallel",)), name="moe_combine",
    )(x, ya, yb, rec, mods, final_g.reshape(1, D_MODEL))


_SC_WORKERS = 32
_SC_CORES = 2
_SC_ROWS = 64


def _dispatch_rows(hp, dest):
    n, width = hp.shape
    per_w = n // _SC_WORKERS
    n_ch = per_w // _SC_ROWS
    idx = dest.T.reshape(2, _SC_WORKERS, n_ch, _SC_ROWS)
    mesh = plsc.VectorSubcoreMesh(core_axis_name="c", subcore_axis_name="s")

    @functools.partial(
        pl.kernel, mesh=mesh, out_type=jax.ShapeDtypeStruct((MOE_NBLK * MOE_BLK, width), hp.dtype),
        scratch_types=[pltpu.VMEM((n_ch, _SC_ROWS), jnp.int32), pltpu.VMEM((n_ch, _SC_ROWS), jnp.int32),
                       pltpu.VMEM((_SC_ROWS, width), hp.dtype)], name="dispatch_rows")
    def scatter(h_hbm, idx_hbm, out_hbm, i0_v, i1_v, rows_v):
        wid = lax.axis_index("s") * _SC_CORES + lax.axis_index("c")
        pltpu.sync_copy(idx_hbm.at[0, wid], i0_v)
        pltpu.sync_copy(idx_hbm.at[1, wid], i1_v)

        @pl.loop(0, n_ch)
        def _(g):
            pltpu.sync_copy(h_hbm.at[pl.ds(wid * per_w + g * _SC_ROWS, _SC_ROWS)], rows_v)
            pltpu.sync_copy(rows_v, out_hbm.at[i0_v.at[g]])
            pltpu.sync_copy(rows_v, out_hbm.at[i1_v.at[g]])

    return scatter(hp, idx)


def _moe(x, h, rec, cnt, mods, w_gate, w_up, w_down, layer, final_g, final):
    dest, blk_e, n_used, slot, nxt = _plan(rec, cnt)
    y_pad = _experts(_dispatch_rows(h, dest), blk_e, n_used, slot, nxt, w_gate, w_up, w_down, layer)
    return _combine(x, y_pad[dest[:, 0]], y_pad[dest[:, 1]], rec, mods, final_g, final)


def kernel(x_prompt, x_sample, c, cache_k, cache_v, state_delta, c_ctx, ada_w, ada_b, norm1_g, norm2_g, final_g,
           ev_w_in, ev_w_out, ev_conv_w, ev_a_log, ev_dt_bias, ev_onorm_g, ev_rpb, od_w_in, od_ln_g, od_ln_b,
           od_w_s, od_b_s, od_w_out, moe_w_rg, moe_b_rg, moe_w_re, moe_b_re, moe_w_gate, moe_w_up, moe_w_down):
    x = jnp.concatenate([x_prompt.reshape(N_CTX, D_MODEL), x_sample.reshape(N_LAT, D_MODEL)], axis=0)
    cond = jnp.concatenate([c_ctx[None, :], c, jnp.zeros((N_COND - 1 - DEC_BATCH, D_MODEL), F32)], axis=0)
    mods_all = _ada_mods(cond, ada_w, ada_b)
    kctx_all = cache_k.reshape(DEC_BATCH, -1, PAST_LEN, NA_HEADS * NA_HD)
    vctx_all = cache_v.reshape(DEC_BATCH, -1, PAST_LEN, NA_HEADS * NA_HD)

    ks, vs, ss = [], [], []
    for l in range(DEPTH):
        mods = mods_all[l]
        router = _router_weights(moe_w_rg[l], moe_b_rg[l], moe_w_re[l], moe_b_re[l])
        if l % 2 == 0:
            e = l // 2
            proj, ab, kv = _even_proj(x, mods, norm1_g[l], ev_w_in[e])
            dn = (proj, ab, ev_conv_w[e], ev_a_log[e], ev_dt_bias[e], ev_onorm_g[e])
            oa_ctx, s_fin = _delta_heads(*dn, SEQ, BATCH, 0, None)
            oa_lat, _ = _delta_heads(*dn, DEC_SEQ, DEC_BATCH, N_CTX // DEC_SEQ, state_delta[:, e])
            ob_ctx = _ctx_attention(proj)
            ob_lat = _na_attention(proj, kctx_all[:, e], vctx_all[:, e], ev_rpb[e])
            x, h, rec, cnt = _even_out(oa_ctx, ob_ctx, oa_lat, ob_lat, x, ev_w_out[e], mods, norm2_g[l],
                                       router)
            na_w = NA_HEADS * NA_HD
            ks.append(kv[:N_CTX, :na_w].reshape(BATCH, SEQ, NA_HEADS, NA_HD))
            vs.append(kv[:N_CTX, na_w:].reshape(BATCH, SEQ, NA_HEADS, NA_HD))
            ss.append(s_fin)
        else:
            o = l // 2
            x, h, rec, cnt = _sgu_layer(x, mods, norm1_g[l], od_w_in[o], od_ln_g[o], od_ln_b[o], od_w_s[o],
                                        od_b_s[o], od_w_out[o], norm2_g[l], router)
        x = _moe(x, h, rec, cnt, mods, moe_w_gate, moe_w_up, moe_w_down, l, final_g, l == DEPTH - 1)

    y_prompt = x[:N_CTX].reshape(BATCH, SEQ, D_MODEL)
    y_sample = x[N_CTX:].reshape(DEC_BATCH, DEC_SEQ, D_MODEL)
    return (y_prompt, y_sample, jnp.stack(ks, axis=1), jnp.stack(vs, axis=1), jnp.stack(ss, axis=1))
```

```python
import functools

import jax
import jax.numpy as jnp
from jax import lax
from jax.experimental import pallas as pl
from jax.experimental.pallas import tpu as pltpu
from jax.experimental.pallas import tpu_sc as plsc

F32 = jnp.float32
BF16 = jnp.bfloat16

D_MODEL = 1024
BATCH = 16
SEQ = 256
DEPTH = 4
DEC_BATCH = 4
DEC_SEQ = 2048
PAST_LEN = 512
GRID_W = 64
EPS = 1e-6
NEG_INF = -1e30

DN_HEADS = 4
DN_DK = 128
DN_CHUNK = 64
NA_HEADS = 8
NA_HD = 64
NA_ROWS = 8
NA_COLS = 16
SG_CHUNK = 128
SG_GROUPS = 8
SG_W = 2 * D_MODEL
SG_GW = SG_W // SG_GROUPS
N_EGROUPS = 4
EXP_PER_GROUP = 8
N_EXPERTS = 32
D_EXPERT = 512

N_CTX = BATCH * SEQ
N_LAT = DEC_BATCH * DEC_SEQ
N_TOK = N_CTX + N_LAT
N_COND = 8
PROJ_W = 4096
LANES = 128
MOE_BLK = 256
MOE_NBLK = -(-(2 * N_TOK + N_EXPERTS * (MOE_BLK - 1)) // MOE_BLK)
VMEM_LIMIT = 56 * 1024 * 1024

_QA, _KA, _VA, _ZA, _QB, _KB, _VB = 0, 4, 8, 12, 16, 20, 24


def _params(sem):
    return pltpu.CompilerParams(dimension_semantics=sem, vmem_limit_bytes=VMEM_LIMIT)


def _bdot(a, b):
    return jnp.dot(a.astype(BF16), b.astype(BF16), preferred_element_type=F32)


def _bdot_nt(a, b):
    return lax.dot_general(a.astype(BF16), b.astype(BF16), (((1,), (1,)), ((), ())),
                           preferred_element_type=F32)


def _bdot_tn(a, b):
    return lax.dot_general(a.astype(BF16), b.astype(BF16), (((0,), (0,)), ((), ())),
                           preferred_element_type=F32)


def _split2(a):
    p0 = a.astype(BF16)
    return p0, (a - p0.astype(F32)).astype(BF16)


def _dot3(a, b):
    ah = a.astype(BF16)
    al = (a - ah.astype(F32)).astype(BF16)
    bh = b.astype(BF16)
    bl = (b - bh.astype(F32)).astype(BF16)
    return (jnp.dot(ah, bh, preferred_element_type=F32) + jnp.dot(ah, bl, preferred_element_type=F32)
            + jnp.dot(al, bh, preferred_element_type=F32))


def _mask_bf16(m01):
    return jnp.where(m01, 1.0, 0.0).astype(BF16)


def _xdot(m01, a):
    m = _mask_bf16(m01)
    p0, p1 = _split2(a)
    return jnp.dot(m, p0, preferred_element_type=F32) + jnp.dot(m, p1, preferred_element_type=F32)


def _xdot_r(a, m01):
    m = _mask_bf16(m01)
    p0, p1 = _split2(a)
    return jnp.dot(p0, m, preferred_element_type=F32) + jnp.dot(p1, m, preferred_element_type=F32)


def _sigmoid(x):
    return 0.5 * jnp.tanh(0.5 * x) + 0.5


def _silu(x):
    return x * _sigmoid(x)


def _rms(x, g):
    return x * lax.rsqrt(jnp.mean(x * x, axis=-1, keepdims=True) + EPS) * g


def _cond_index(row):
    return jnp.where(row < N_CTX, 0, 1 + (row - N_CTX) // DEC_SEQ)


def _mod_spec(k, tm):
    return pl.BlockSpec((None, None, 1, D_MODEL), lambda i, *_: (_cond_index(i * tm), k, 0, 0))


def _ada_kernel(c_ref, w_ref, b_ref, o_ref):
    o_ref[...] = _bdot(_silu(c_ref[...]), w_ref[...]) + b_ref[...]


def _ada_mods(cond, ada_w, ada_b):
    tn = 1536
    out = pl.pallas_call(
        _ada_kernel, grid=(DEPTH, 6 * D_MODEL // tn),
        in_specs=[pl.BlockSpec((N_COND, D_MODEL), lambda l, j: (0, 0)),
                  pl.BlockSpec((None, D_MODEL, tn), lambda l, j: (l, 0, j)),
                  pl.BlockSpec((None, 1, tn), lambda l, j: (l, 0, j))],
        out_specs=pl.BlockSpec((None, N_COND, tn), lambda l, j: (l, 0, j)),
        out_shape=jax.ShapeDtypeStruct((DEPTH, N_COND, 6 * D_MODEL), F32),
        compiler_params=_params(("parallel", "parallel")), name="ada_mods",
    )(cond, ada_w, ada_b.reshape(DEPTH, 1, 6 * D_MODEL))
    return out.reshape(DEPTH, N_COND, 6, 1, D_MODEL)


_EV_TN = 512
_EV_W = 7 * DN_HEADS * LANES
_KV_COL0 = _KB * LANES


def _with_pending(x_ref, pend):
    if not pend:
        return x_ref[...]
    ya_ref, yb_ref, rec_ref, gate_ref = pend
    rec = rec_ref[...]
    y = (rec[:, _R_W:_R_W + 1] * ya_ref[...].astype(F32)
         + rec[:, _R_W + 1:_R_W + 2] * yb_ref[...].astype(F32))
    return x_ref[...] + gate_ref[...] * y


def _pending_specs(tm):
    blk = pl.BlockSpec((tm, D_MODEL), lambda i: (i, 0))
    return [blk, blk, pl.BlockSpec((tm, LANES), lambda i: (i, 0)), _mod_spec(5, tm)]


def _even_proj_kernel(x_ref, *rest, n_pend):
    pend, (g_ref, sh_ref, sc_ref, w_ref, wab_ref, o_ref, ab_ref, kv_ref) = rest[:n_pend], rest[n_pend:n_pend + 8]
    x = _with_pending(x_ref, pend)
    if pend:
        rest[n_pend + 8][...] = x
    h = (_rms(x, g_ref[...]) * (1.0 + sc_ref[...]) + sh_ref[...]).astype(BF16)
    ab_ref[...] = jnp.dot(h, wab_ref[...], preferred_element_type=F32)
    for j in range(_EV_W // _EV_TN):
        c0 = j * _EV_TN
        y = jnp.dot(h, w_ref[:, c0:c0 + _EV_TN], preferred_element_type=F32)
        for c in range(_EV_TN // LANES):
            o_ref[c0 // LANES + c] = y[:, c * LANES:(c + 1) * LANES].astype(BF16)
        if c0 >= _KV_COL0:
            kv_ref[:, c0 - _KV_COL0:c0 - _KV_COL0 + _EV_TN] = y


def _even_proj(x, pend, mods, g, w_in):
    tm = 512
    pend = tuple(pend) if pend else ()
    n_ab = 4 * DN_HEADS
    ab0 = 4 * DN_HEADS * DN_DK
    w_main = jnp.concatenate([w_in[:, :ab0], w_in[:, ab0 + n_ab:]], axis=1).astype(BF16)
    w_ab = jnp.concatenate([w_in[:, ab0:ab0 + n_ab], jnp.zeros((D_MODEL, LANES - n_ab), F32)],
                           axis=1).astype(BF16)
    held = lambda shape: pl.BlockSpec(shape, lambda i: (0,) * len(shape), pipeline_mode=pl.Buffered(1))
    rows = pl.BlockSpec((tm, D_MODEL), lambda i: (i, 0))
    out_specs = [pl.BlockSpec((_EV_W // LANES, tm, LANES), lambda i: (0, i, 0)),
                 pl.BlockSpec((tm, LANES), lambda i: (i, 0)),
                 pl.BlockSpec((tm, 2 * NA_HEADS * NA_HD), lambda i: (i, 0))]
    out_shape = [jax.ShapeDtypeStruct((_EV_W // LANES, N_TOK, LANES), BF16),
                 jax.ShapeDtypeStruct((N_TOK, LANES), F32),
                 jax.ShapeDtypeStruct((N_TOK, 2 * NA_HEADS * NA_HD), F32)]
    if pend:
        out_specs.append(rows)
        out_shape.append(jax.ShapeDtypeStruct((N_TOK, D_MODEL), F32))
    res = pl.pallas_call(
        functools.partial(_even_proj_kernel, n_pend=len(pend)), grid=(N_TOK // tm,),
        in_specs=[rows] + (_pending_specs(tm) if pend else [])
        + [pl.BlockSpec((1, D_MODEL), lambda i: (0, 0)), _mod_spec(0, tm), _mod_spec(1, tm),
           held((D_MODEL, _EV_W)), held((D_MODEL, LANES))],
        out_specs=out_specs, out_shape=out_shape,
        compiler_params=_params(("parallel",)), name="even_proj",
    )(x, *pend, g.reshape(1, D_MODEL), mods, mods, w_main, w_ab)
    return (*res[:3], res[3] if pend else x)


_CHUNK_SHIFT = DN_CHUNK.bit_length() - 1
_CUM_ROWS = 256
_DN_CHAINS = 16
_DN_SHORT = 256
_SERIES_FINE = 3
_MQ_ROWS = DN_DK + DN_CHUNK


def _dn_kernel(*refs, T, HB, has_s0, want_state):
    it = iter(refs)
    q_ref, k_ref, v_ref, z_ref, ab_ref = (next(it) for _ in range(5))
    cwq_ref, cwk_ref, cwv_ref, alog_ref, dtb_ref, og_ref = (next(it) for _ in range(6))
    s0_ref = next(it) if has_s0 else None
    o_ref = next(it)
    sfin_ref = next(it) if want_state else None
    qc, kc, vc, gsc, bsc, osc, b_s, mq_s = (next(it) for _ in range(8))

    C = DN_CHUNK
    n = T // C
    h0 = pl.program_id(1) * HB

    row = lax.broadcasted_iota(jnp.int32, (T, 1), 0)

    def conv(x_ref, cw_ref, hh):
        x = x_ref[hh].astype(F32)
        cw = cw_ref[:, hh * LANES:(hh + 1) * LANES]
        xp = jnp.where(row == 0, 0.0, pltpu.roll(x, 1, 0))
        xn = jnp.where(row == T - 1, 0.0, pltpu.roll(x, T - 1, 0))
        return _silu(cw[0:1, :] * xp + cw[1:2, :] * x + cw[2:3, :] * xn)

    def l2n(x):
        return x * lax.rsqrt(jnp.sum(x * x, axis=-1, keepdims=True) + EPS)

    ab = ab_ref[...]
    sel_r = lax.broadcasted_iota(jnp.int32, (LANES, LANES), 0)
    for hh in range(HB):
        qc[hh] = l2n(conv(q_ref, cwq_ref, hh)) * (DN_DK ** -0.5)
        kc[hh] = l2n(conv(k_ref, cwk_ref, hh))
        vc[hh] = conv(v_ref, cwv_ref, hh)
        hd = h0 + hh
        for d in range(2):
            alpha = _xdot_r(ab, sel_r == d * DN_HEADS + hd)
            blog = _xdot_r(ab, sel_r == 2 * DN_HEADS + d * DN_HEADS + hd)
            x = alpha + dtb_ref[d, hd]
            sp = jnp.maximum(x, 0.0) + jnp.log1p(jnp.exp(-jnp.abs(x)))
            a = jnp.exp(jnp.full((1, LANES), alog_ref[d, hd], F32))
            gsc[hh, d] = -a * sp
            bsc[hh, d] = _sigmoid(blog)

    pr = lax.broadcasted_iota(jnp.int32, (_CUM_ROWS, _CUM_ROWS), 0)
    pc = lax.broadcasted_iota(jnp.int32, (_CUM_ROWS, _CUM_ROWS), 1)
    same = lax.shift_right_logical(pr, _CHUNK_SHIFT) == lax.shift_right_logical(pc, _CHUNK_SHIFT)
    cum_mask = (jnp.logical_and(same, pc <= pr), jnp.logical_and(same, pc >= pr))

    def cum_body(i, carry):
        sl = pl.ds(pl.multiple_of(i * _CUM_ROWS, _CUM_ROWS), _CUM_ROWS)
        for hh in range(HB):
            for d in range(2):
                gsc[hh, d, sl, :] = _xdot(cum_mask[d], gsc[hh, d, sl, :])
        return carry

    lax.fori_loop(0, T // _CUM_ROWS, cum_body, 0)

    ri = lax.broadcasted_iota(jnp.int32, (C, C), 0)
    ci = lax.broadcasted_iota(jnp.int32, (C, C), 1)
    eye = (ri == ci).astype(F32)

    def prepare(items):
        lows, decays = [], []
        kk, qk = {}, {}
        for hh, d, c, slot in items:
            sl = pl.ds(pl.multiple_of(c * C, C), C)
            gc = gsc[hh, d, sl, :]
            if (hh, slot) not in kk:
                k = kc[hh, sl, :]
                kk[hh, slot] = _bdot_nt(k, k)
                qk[hh, slot] = _bdot_nt(qc[hh, sl, :], k)
            incl = (ci <= ri) if d == 0 else (ci >= ri)
            strict = (ci < ri) if d == 0 else (ci > ri)
            gr = jnp.transpose(gc)[0:1, :C]
            decay = jnp.where(incl, jnp.exp(jnp.where(incl, gc[:, :C] - gr, 0.0)), 0.0)
            lows.append(jnp.where(strict, bsc[hh, d, sl, :C] * kk[hh, slot] * decay, 0.0))
            decays.append(decay)
        ts = [eye - low for low in lows]
        ps = lows
        for step in range(5):
            dot = _dot3 if step < _SERIES_FINE else _bdot
            ps = [dot(p, p) for p in ps]
            ts = [t + dot(t, p) for t, p in zip(ts, ps)]
        for (hh, d, c, slot), t, decay in zip(items, ts, decays):
            sl = pl.ds(pl.multiple_of(c * C, C), C)
            q, k, gc, beta = qc[hh, sl, :], kc[hh, sl, :], gsc[hh, d, sl, :], bsc[hh, d, sl, :]
            eg = jnp.exp(gc)
            uw = _bdot(t, jnp.concatenate([vc[hh, sl, :] * beta, k * beta * eg], axis=-1))
            last = gc[C - 1:C, :] if d == 0 else gc[0:1, :]
            wu = jnp.concatenate([uw[:, LANES:], uw[:, :LANES]], axis=-1).astype(BF16)
            kd = (k * jnp.exp(last - gc)).astype(BF16)
            attn = (qk[hh, slot] * decay).astype(BF16)
            kdwu = lax.dot_general(kd, wu, (((0,), (0,)), ((), ())), preferred_element_type=F32)
            awu = jnp.dot(attn, wu, preferred_element_type=F32)
            mq0 = pl.multiple_of(c * _MQ_ROWS, _MQ_ROWS)
            mq_s[hh, d, pl.ds(mq0, DN_DK), :] = kdwu[:, :LANES].astype(BF16)
            mq_s[hh, d, pl.ds(mq0 + DN_DK, C), :] = (q * eg - awu[:, :LANES]).astype(BF16)
            b_s[hh, d, pl.ds(pl.multiple_of(c * DN_DK, DN_DK), DN_DK), :] = kdwu[:, LANES:]
            osc[hh, d, sl, :] = awu[:, LANES:]

    n_prep = min(n, _DN_CHAINS // 2)
    h_prep = max(1, min(HB, _DN_CHAINS // (2 * n_prep)))

    def prep_body(i, carry):
        for hg in range(0, HB, h_prep):
            prepare([(hh, d, i * n_prep + j, j) for hh in range(hg, hg + h_prep) for j in range(n_prep)
                     for d in range(2)])
        return carry

    lax.fori_loop(0, n // n_prep, prep_body, 0)

    def advance(hh, d, c, S):
        sl = pl.ds(pl.multiple_of(c * C, C), C)
        ms = jnp.dot(mq_s[hh, d, pl.ds(pl.multiple_of(c * _MQ_ROWS, _MQ_ROWS), _MQ_ROWS), :], S.astype(BF16),
                     preferred_element_type=F32)
        osc[hh, d, sl, :] = osc[hh, d, sl, :] + ms[DN_DK:]
        last = gsc[hh, d, pl.ds(c * C + (C - 1 if d == 0 else 0), 1), :]
        return (S * jnp.exp(last) - ms[:DN_DK]
                + b_s[hh, d, pl.ds(pl.multiple_of(c * DN_DK, DN_DK), DN_DK), :])

    def body(i, carry):
        return tuple(advance(hh, d, i if d == 0 else n - 1 - i, carry[2 * hh + d])
                     for hh in range(HB) for d in range(2))

    if has_s0:
        init = tuple(s0_ref[d, hh] for hh in range(HB) for d in range(2))
    else:
        init = tuple(jnp.zeros((DN_DK, LANES), F32) for _ in range(2 * HB))
    fin = lax.fori_loop(0, n, body, init)
    for hh in range(HB):
        if want_state:
            sfin_ref[0, hh] = fin[2 * hh]
            sfin_ref[1, hh] = fin[2 * hh + 1]
        o = osc[hh, 0] + osc[hh, 1]
        o_ref[hh] = (_rms(o, og_ref[...]) * _silu(z_ref[hh].astype(F32))).astype(o_ref.dtype)


def _delta_heads(proj, ab, conv_w, a_log, dt_bias, onorm_g, T, n_seq, row0, s0):
    has_s0 = s0 is not None
    want_state = not has_s0
    hb = DN_HEADS if T <= _DN_SHORT else 1

    def col(cb):
        return pl.BlockSpec((hb, T, LANES), lambda s, h: (cb // hb + h, row0 + s, 0))

    def cw(cb):
        return pl.BlockSpec((3, hb * LANES), lambda s, h: (0, cb // hb + h))

    smem = pl.BlockSpec(memory_space=pltpu.SMEM)
    in_specs = [col(_QA), col(_KA), col(_VA), col(_ZA),
                pl.BlockSpec((T, LANES), lambda s, h: (row0 + s, 0)),
                cw(0), cw(4), cw(8), smem, smem,
                pl.BlockSpec((1, LANES), lambda s, h: (0, 0))]
    args = [proj, proj, proj, proj, ab, conv_w, conv_w, conv_w, a_log, dt_bias,
            onorm_g.reshape(1, LANES)]
    state_spec = pl.BlockSpec((None, 2, hb, DN_DK, LANES), lambda s, h: (s, 0, h, 0, 0))
    if has_s0:
        in_specs.append(state_spec)
        args.append(s0)
    out_shape = [jax.ShapeDtypeStruct((DN_HEADS, n_seq * T, LANES), BF16)]
    out_specs = [pl.BlockSpec((hb, T, LANES), lambda s, h: (h, s, 0))]
    if want_state:
        out_shape.append(jax.ShapeDtypeStruct((n_seq, 2, DN_HEADS, DN_DK, LANES), F32))
        out_specs.append(state_spec)
    res = pl.pallas_call(
        functools.partial(_dn_kernel, T=T, HB=hb, has_s0=has_s0, want_state=want_state),
        grid=(n_seq, DN_HEADS // hb), in_specs=in_specs, out_specs=out_specs, out_shape=out_shape,
        scratch_shapes=[pltpu.VMEM((hb, T, LANES), F32)] * 3
        + [pltpu.VMEM((hb, 2, T, LANES), F32)] * 3
        + [pltpu.VMEM((hb, 2, T // DN_CHUNK * DN_DK, LANES), F32),
           pltpu.VMEM((hb, 2, T // DN_CHUNK * _MQ_ROWS, LANES), BF16)],
        compiler_params=_params(("parallel", "parallel")), name="delta_heads_%d" % T,
    )(*args)
    return res if want_state else (res[0], None)


def _pair_queries(q, first):
    return jnp.concatenate([jnp.where(first, q, 0.0), jnp.where(first, 0.0, q)], axis=0).astype(BF16)


def _ctx_attn_kernel(q_ref, k_ref, v_ref, o_ref):
    first = lax.broadcasted_iota(jnp.int32, (SEQ, LANES), 1) < NA_HD
    qm = _pair_queries(q_ref[...] * (NA_HD ** -0.5), first)
    s = lax.dot_general(k_ref[...], qm, (((1,), (1,)), ((), ())), preferred_element_type=F32)
    e = jnp.exp(s - jnp.max(s, axis=0, keepdims=True))
    den = jnp.sum(e, axis=0, keepdims=True)
    o = lax.dot_general(e.astype(BF16), v_ref[...], (((0,), (0,)), ((), ())), preferred_element_type=F32)
    o = jnp.where(first, o[:SEQ], o[SEQ:])
    den_t = jnp.transpose(jnp.broadcast_to(den, (LANES, 2 * SEQ)))
    o_ref[...] = (o / jnp.where(first, den_t[:SEQ], den_t[SEQ:])).astype(o_ref.dtype)


def _ctx_attention(proj):
    def col(cb):
        return pl.BlockSpec((None, SEQ, LANES), lambda s, p: (cb + p, s, 0))

    return pl.pallas_call(
        _ctx_attn_kernel, grid=(BATCH, NA_HEADS // 2),
        in_specs=[col(_QB), col(_KB), col(_VB)],
        out_specs=pl.BlockSpec((None, SEQ, LANES), lambda s, p: (p, s, 0)),
        out_shape=jax.ShapeDtypeStruct((NA_HEADS // 2, N_CTX, LANES), BF16),
        compiler_params=_params(("parallel", "parallel")), name="ctx_attention",
    )(proj, proj, proj)


_NA_UNROLL = 4


def _na_kernel(q_ref, k_ref, v_ref, kc_ref, vc_ref, bias_ref, o_ref, kcb_scr, vcb_scr):
    rows = DEC_SEQ // GRID_W
    win = NA_ROWS * GRID_W
    scale = NA_HD ** -0.5
    dn_nt = (((1,), (1,)), ((), ()))
    dn_tn = (((0,), (0,)), ((), ()))

    kcb_scr[...] = kc_ref[...].astype(BF16)
    vcb_scr[...] = vc_ref[...].astype(BF16)
    first = lax.broadcasted_iota(jnp.int32, (GRID_W, LANES), 1) < NA_HD

    def body(it, carry):
        rr = [it * _NA_UNROLL + j for j in range(_NA_UNROLL)]
        rss = [jnp.clip(r - NA_ROWS // 2, 0, rows - NA_ROWS) for r in rr]
        qsls = [pl.ds(pl.multiple_of(r * GRID_W, GRID_W), GRID_W) for r in rr]
        wsls = [pl.ds(pl.multiple_of(rs * GRID_W, GRID_W), win) for rs in rss]
        qms, s_wins, s_ctxs = [], [], []
        for r, rs, qsl, wsl in zip(rr, rss, qsls, wsls):
            qm = _pair_queries(q_ref[qsl, :] * scale, first)
            bias = jnp.concatenate([bias_ref[NA_ROWS - 1 - (r - rs) + i] for i in range(NA_ROWS)], axis=0)
            s_wins.append(lax.dot_general(k_ref[wsl, :], qm, dn_nt, preferred_element_type=F32) + bias)
            s_ctxs.append(lax.dot_general(kcb_scr[...], qm, dn_nt, preferred_element_type=F32))
        ms = [jnp.maximum(jnp.max(sw, axis=0, keepdims=True), jnp.max(sc, axis=0, keepdims=True))
              for sw, sc in zip(s_wins, s_ctxs)]
        e_wins = [jnp.exp(sw - m) for sw, m in zip(s_wins, ms)]
        e_ctxs = [jnp.exp(sc - m) for sc, m in zip(s_ctxs, ms)]
        dens = [jnp.sum(ew, axis=0, keepdims=True) + jnp.sum(ec, axis=0, keepdims=True)
                for ew, ec in zip(e_wins, e_ctxs)]
        for qsl, wsl, ew, ec, den in zip(qsls, wsls, e_wins, e_ctxs, dens):
            o = (lax.dot_general(ew.astype(BF16), v_ref[wsl, :], dn_tn, preferred_element_type=F32)
                 + lax.dot_general(ec.astype(BF16), vcb_scr[...], dn_tn, preferred_element_type=F32))
            o = o / jnp.transpose(jnp.broadcast_to(den, (LANES, LANES)))
            o_ref[qsl, :] = jnp.where(first, o[:GRID_W], o[GRID_W:]).astype(o_ref.dtype)
        return carry

    lax.fori_loop(0, rows // _NA_UNROLL, body, 0)


def _na_bias_table(rpb):
    col = jnp.arange(GRID_W)
    cs = jnp.clip(col - NA_COLS // 2, 0, GRID_W - NA_COLS)
    col_ok = (col[None, :] >= cs[:, None]) & (col[None, :] < cs[:, None] + NA_COLS)
    dc = jnp.clip(col[None, :] - col[:, None] + NA_COLS - 1, 0, 2 * NA_COLS - 2)
    onehot = (dc.T[None, :, :] == jnp.arange(2 * NA_COLS - 1)[:, None, None]).astype(F32)
    t = jnp.einsum('hrd,dkq->hrkq', rpb.astype(F32), onehot, precision=lax.Precision.HIGHEST)
    t = jnp.where(col_ok.T[None, None], t, NEG_INF)
    t = t.reshape(NA_HEADS // 2, 2, 2 * NA_ROWS - 1, GRID_W, GRID_W)
    return jnp.concatenate([t[:, 0], t[:, 1]], axis=-1)


def _na_attention(proj, kctx, vctx, rpb):
    blk = N_CTX // DEC_SEQ

    def col(cb):
        return pl.BlockSpec((None, DEC_SEQ, LANES), lambda b, p: (cb + p, blk + b, 0))

    ctx = pl.BlockSpec((None, PAST_LEN, LANES), lambda b, p: (b, 0, p))
    return pl.pallas_call(
        _na_kernel, grid=(DEC_BATCH, NA_HEADS // 2),
        in_specs=[col(_QB), col(_KB), col(_VB), ctx, ctx,
                  pl.BlockSpec((None, 2 * NA_ROWS - 1, GRID_W, 2 * GRID_W), lambda b, p: (p, 0, 0, 0))],
        out_specs=pl.BlockSpec((None, DEC_SEQ, LANES), lambda b, p: (p, b, 0)),
        out_shape=jax.ShapeDtypeStruct((NA_HEADS // 2, N_LAT, LANES), BF16),
        scratch_shapes=[pltpu.VMEM((PAST_LEN, LANES), BF16), pltpu.VMEM((PAST_LEN, LANES), BF16)],
        compiler_params=_params(("parallel", "parallel")), name="na_attention",
    )(proj, proj, proj, kctx, vctx, _na_bias_table(rpb))


_LOGIT0 = N_EGROUPS
_R_E, _R_W, _R_RANK = 0, 2, 4


def _lane_min_where(mask, lane):
    return jnp.min(jnp.where(mask, lane, LANES), axis=-1, keepdims=True)


def _route_rows(lg, carry_ref, tri_ref):
    big = -3.0e38
    lane = lax.broadcasted_iota(jnp.int32, lg.shape, 1)
    is_g = lane < N_EGROUPS
    gmax = jnp.max(jnp.where(is_g, lg, big), axis=-1, keepdims=True)
    gsum = jnp.sum(jnp.where(is_g, jnp.exp(jnp.where(is_g, lg - gmax, 0.0)), 0.0), axis=-1, keepdims=True)
    pg_top = 1.0 / gsum
    g_idx = _lane_min_where(jnp.logical_and(is_g, lg == gmax), lane)
    in_g = jnp.logical_and(lane >= _LOGIT0, lax.shift_right_arithmetic(lane - _LOGIT0, 3) == g_idx)
    in_g = jnp.logical_and(in_g, lane < _LOGIT0 + N_EXPERTS)
    m1 = jnp.max(jnp.where(in_g, lg, big), axis=-1, keepdims=True)
    i1 = _lane_min_where(jnp.logical_and(in_g, lg == m1), lane)
    rest = jnp.logical_and(in_g, lane != i1)
    m2 = jnp.max(jnp.where(rest, lg, big), axis=-1, keepdims=True)
    i2 = _lane_min_where(jnp.logical_and(rest, lg == m2), lane)
    e2 = jnp.exp(m2 - m1)
    w1 = pg_top * (1.0 / (1.0 + e2))
    w2 = pg_top * (e2 / (1.0 + e2))
    hit1 = lane == i1
    hit2 = lane == i2
    picked = jnp.where(jnp.logical_or(hit1, hit2), 1.0, 0.0)
    before = jnp.dot(tri_ref[...], picked.astype(BF16), preferred_element_type=F32) + carry_ref[...]
    r1 = jnp.sum(jnp.where(hit1, before, 0.0), axis=-1, keepdims=True)
    r2 = jnp.sum(jnp.where(hit2, before, 0.0), axis=-1, keepdims=True)
    carry_ref[...] = carry_ref[...] + jnp.sum(picked, axis=0, keepdims=True)
    rec = jnp.zeros(lg.shape, F32)
    for ln, val in ((_R_E, (i1 - _LOGIT0).astype(F32)), (_R_E + 1, (i2 - _LOGIT0).astype(F32)),
                    (_R_W, w1), (_R_W + 1, w2), (_R_RANK, r1), (_R_RANK + 1, r2)):
        rec = jnp.where(lane == ln, val, rec)
    return rec


_PACK_W = D_MODEL // 2


def _pack_rows(hb):
    lo = lax.bitcast_convert_type(hb[:, :_PACK_W].astype(F32), jnp.int32)
    hi = lax.bitcast_convert_type(hb[:, _PACK_W:].astype(F32), jnp.int32)
    return jnp.bitwise_or(jnp.bitwise_and(hi, -65536), lax.shift_right_logical(lo, 16))


def _unpack_rows(w):
    lo = lax.bitcast_convert_type(lax.shift_left(w, 16), F32)
    hi = lax.bitcast_convert_type(jnp.bitwise_and(w, -65536), F32)
    return jnp.concatenate([lo, hi], axis=-1).astype(BF16)


def _moe_input(xnew, first, tail_in, tail_out, tail_scr):
    g2_ref, sc2_ref, sh2_ref, wr_ref, br_ref = tail_in
    x_out, h_out, rec_out, cnt_out = tail_out
    tri_scr, carry_scr = tail_scr

    @pl.when(first)
    def _():
        tm = tri_scr.shape[0]
        r = lax.broadcasted_iota(jnp.int32, (tm, tm), 0)
        c = lax.broadcasted_iota(jnp.int32, (tm, tm), 1)
        tri_scr[...] = jnp.where(c < r, 1.0, 0.0).astype(BF16)
        carry_scr[...] = jnp.zeros(carry_scr.shape, F32)

    x_out[...] = xnew
    h = _rms(xnew, g2_ref[...]) * (1.0 + sc2_ref[...]) + sh2_ref[...]
    hh = h.astype(BF16)
    h_out[...] = _pack_rows(hh)
    lg = jnp.dot(hh, wr_ref[...], preferred_element_type=F32) + br_ref[...]
    rec_out[...] = _route_rows(lg, carry_scr, tri_scr)
    cnt_out[...] = carry_scr[...]


def _even_out_kernel(oac_ref, obc_ref, oal_ref, obl_ref, x_ref, w_ref, gate_ref, *rest, ctx_tiles):
    tail_in, tail_out, (w_scr,), tail_scr = rest[:5], rest[5:9], rest[9:10], rest[10:]
    first = pl.program_id(0) == 0

    @pl.when(first)
    def _():
        w_scr[...] = w_ref[...].astype(BF16)

    is_ctx = pl.program_id(0) < ctx_tiles
    parts = [jnp.where(is_ctx, c_ref[hb], l_ref[hb])
             for c_ref, l_ref in ((oac_ref, oal_ref), (obc_ref, obl_ref)) for hb in range(DN_HEADS)]
    mix = jnp.concatenate(parts, axis=-1)
    out = jnp.dot(mix, w_scr[...], preferred_element_type=F32)
    _moe_input(x_ref[...] + gate_ref[...] * out, first, tail_in, tail_out, tail_scr)


def _tail_specs(tm):
    const = lambda shape: pl.BlockSpec(shape, lambda i: (0,) * len(shape))
    in_specs = [_mod_spec(2, tm), const((1, D_MODEL)), _mod_spec(4, tm), _mod_spec(3, tm),
                const((D_MODEL, LANES)), const((1, LANES))]
    out_specs = [pl.BlockSpec((tm, D_MODEL), lambda i: (i, 0)),
                 pl.BlockSpec((tm, _PACK_W), lambda i: (i, 0)),
                 pl.BlockSpec((tm, LANES), lambda i: (i, 0)),
                 const((1, LANES))]
    out_shape = [jax.ShapeDtypeStruct((N_TOK, D_MODEL), F32),
                 jax.ShapeDtypeStruct((N_TOK, _PACK_W), jnp.int32),
                 jax.ShapeDtypeStruct((N_TOK, LANES), F32),
                 jax.ShapeDtypeStruct((1, LANES), F32)]
    scratch = [pltpu.VMEM((tm, tm), BF16), pltpu.VMEM((1, LANES), F32)]
    return in_specs, out_specs, out_shape, scratch


def _router_weights(w_rg, b_rg, w_re, b_re):
    pad = LANES - N_EGROUPS - N_EXPERTS
    w = jnp.concatenate([w_rg, w_re, jnp.zeros((D_MODEL, pad), F32)], axis=1)
    b = jnp.concatenate([b_rg, b_re, jnp.zeros((pad,), F32)]).reshape(1, LANES)
    return w.astype(BF16), b


def _even_out(oa_ctx, ob_ctx, oa_lat, ob_lat, x, w_out, mods, g2, router):
    tm = 512
    ctx_tiles = N_CTX // tm
    tail_in, out_specs, out_shape, tail_scr = _tail_specs(tm)
    ctxblk = pl.BlockSpec((DN_HEADS, tm, LANES), lambda i: (0, jnp.minimum(i, ctx_tiles - 1), 0))
    latblk = pl.BlockSpec((DN_HEADS, tm, LANES), lambda i: (0, jnp.maximum(i - ctx_tiles, 0), 0))
    return pl.pallas_call(
        functools.partial(_even_out_kernel, ctx_tiles=ctx_tiles), grid=(N_TOK // tm,),
        in_specs=[ctxblk, ctxblk, latblk, latblk, pl.BlockSpec((tm, D_MODEL), lambda i: (i, 0)),
                  pl.BlockSpec((D_MODEL, D_MODEL), lambda i: (0, 0))] + tail_in,
        out_specs=out_specs, out_shape=out_shape,
        scratch_shapes=[pltpu.VMEM((D_MODEL, D_MODEL), BF16)] + tail_scr,
        compiler_params=_params(("arbitrary",)), name="even_out",
    )(oa_ctx, ob_ctx, oa_lat, ob_lat, x, w_out, mods, g2.reshape(1, D_MODEL), mods, mods, *router)


def _gelu_tanh(x):
    return x * (0.5 * (1.0 + jnp.tanh(0.7978845608028654 * (x + 0.044715 * (x * x * x)))))


def _sgu_kernel(x_ref, *rest, tm, n_pend):
    pend, rest = rest[:n_pend], rest[n_pend:]
    g1_ref, sh1_ref, sc1_ref, win_ref, lng_ref, lnb_ref, ws_ref, bst_ref, wout_ref, gate_ref = rest[:10]
    rest = rest[10:]
    tail_in, tail_out, (v_scr, m_scr), tail_scr = rest[:5], rest[5:9], rest[9:11], rest[11:]
    first = pl.program_id(0) == 0
    x = _with_pending(x_ref, pend)
    h = (_rms(x, g1_ref[...]) * (1.0 + sc1_ref[...]) + sh1_ref[...]).astype(BF16)

    v = _gelu_tanh(jnp.dot(h, win_ref[:, SG_W:], preferred_element_type=F32))
    mu = jnp.mean(v, axis=-1, keepdims=True)
    vc = v - mu
    var = jnp.mean(vc * vc, axis=-1, keepdims=True)
    v_scr[...] = (vc * lax.rsqrt(var + EPS) * lng_ref[...] + lnb_ref[...]).astype(BF16)

    for g in range(SG_GROUPS):
        cs = slice(g * SG_GW, (g + 1) * SG_GW)
        u = _gelu_tanh(jnp.dot(h, win_ref[:, cs], preferred_element_type=F32))
        w_sp = ws_ref[g].astype(BF16)
        for c in range(tm // SG_CHUNK):
            rs = slice(c * SG_CHUNK, (c + 1) * SG_CHUNK)
            sp = jnp.dot(w_sp, v_scr[rs, cs], preferred_element_type=F32) + bst_ref[:, g:g + 1]
            m_scr[rs, cs] = (u[rs] * sp).astype(BF16)
    out = jnp.dot(m_scr[...], wout_ref[...], preferred_element_type=F32)
    _moe_input(x + gate_ref[...] * out, first, tail_in, tail_out, tail_scr)


def _sgu_layer(x, pend, mods, g1, w_in, ln_g, ln_b, w_s, b_s, w_out, g2, router):
    tm = 512
    pend = tuple(pend) if pend else ()
    tail_in, out_specs, out_shape, tail_scr = _tail_specs(tm)
    const = lambda shape: pl.BlockSpec(shape, lambda i: (0,) * len(shape))
    held = lambda shape: pl.BlockSpec(shape, lambda i: (0,) * len(shape), pipeline_mode=pl.Buffered(1))
    return pl.pallas_call(
        functools.partial(_sgu_kernel, tm=tm, n_pend=len(pend)), grid=(N_TOK // tm,),
        in_specs=[pl.BlockSpec((tm, D_MODEL), lambda i: (i, 0))] + (_pending_specs(tm) if pend else [])
        + [const((1, D_MODEL)), _mod_spec(0, tm), _mod_spec(1, tm),
                  held((D_MODEL, 2 * SG_W)), const((1, SG_W)), const((1, SG_W)),
                  const((SG_GROUPS, SG_CHUNK, SG_CHUNK)), const((SG_CHUNK, SG_GROUPS)),
                  held((SG_W, D_MODEL))] + tail_in,
        out_specs=out_specs, out_shape=out_shape,
        scratch_shapes=[pltpu.VMEM((tm, SG_W), BF16), pltpu.VMEM((tm, SG_W), BF16)] + tail_scr,
        compiler_params=_params(("arbitrary",)), name="sgu_layer",
    )(x, *pend, g1.reshape(1, D_MODEL), mods, mods, w_in.astype(BF16), ln_g.reshape(1, SG_W), ln_b.reshape(1, SG_W),
      w_s, b_s.T, w_out.astype(BF16), mods, g2.reshape(1, D_MODEL), mods, mods, *router)


def _plan(rec, cnt):
    e_idx = rec[:, _R_E:_R_E + 2].astype(jnp.int32)
    rank = rec[:, _R_RANK:_R_RANK + 2].astype(jnp.int32)
    counts = cnt[0, _LOGIT0:_LOGIT0 + N_EXPERTS].astype(jnp.int32)
    padded = (counts + MOE_BLK - 1) // MOE_BLK * MOE_BLK
    pad_end = jnp.cumsum(padded)
    pad_start = pad_end - padded
    hit = e_idx[:, :, None] == jnp.arange(N_EXPERTS, dtype=jnp.int32)[None, None, :]
    dest = jnp.sum(jnp.where(hit, pad_start[None, None, :], 0), axis=-1) + rank
    blk0 = jnp.arange(MOE_NBLK, dtype=jnp.int32) * MOE_BLK
    blk_e = jnp.minimum(jnp.sum((pad_end[None, :] <= blk0[:, None]).astype(jnp.int32), axis=-1),
                        N_EXPERTS - 1)
    n_used = (pad_end[-1] // MOE_BLK).astype(jnp.int32).reshape(1)
    owns = counts > 0
    slot_of = (jnp.cumsum(owns.astype(jnp.int32)) - 1) % _W_SLOTS
    ids = jnp.arange(N_EXPERTS, dtype=jnp.int32)
    later = jnp.logical_and(owns[None, :], ids[None, :] > ids[:, None])
    next_of = jnp.min(jnp.where(later, ids[None, :], N_EXPERTS), axis=-1)
    next2_of = jnp.concatenate([next_of, jnp.full((1,), N_EXPERTS, jnp.int32)])[next_of]
    ahead = jnp.stack([next_of, next2_of], axis=0)
    ahead = jnp.where(ahead == N_EXPERTS, -1, ahead)
    return dest, blk_e, n_used, slot_of[blk_e], ahead[:, blk_e].reshape(-1)


_W_PARTS = 4
_W_SLOTS = 3


def _expert_kernel(blk_e_ref, n_used_ref, slot_ref, next_ref, x_ref, wg_hbm, wu_hbm, wd_hbm, o_ref,
                   wg_buf, wu_buf, wd_buf, wg_scr, wu_scr, wd_scr, sems, *, layer):
    j = pl.program_id(0)
    e = blk_e_ref[j]
    slot = slot_ref[j]
    fresh = jnp.logical_or(j == 0, e != blk_e_ref[jnp.maximum(j - 1, 0)])
    live = j < n_used_ref[0]

    def copies(expert, s):
        out = []
        for m, (hbm, buf) in enumerate(((wg_hbm, wg_buf), (wu_hbm, wu_buf), (wd_hbm, wd_buf))):
            rows = buf.shape[1] // _W_PARTS
            for part in range(_W_PARTS):
                band = pl.ds(part * rows, rows)
                out.append(pltpu.make_async_copy(hbm.at[layer, expert, band], buf.at[s, band],
                                                 sems.at[s, m, part]))
        return out

    def start_if_any(expert, s):
        @pl.when(expert >= 0)
        def _():
            for cp in copies(expert, s):
                cp.start()

    @pl.when(j == 0)
    def _():
        for cp in copies(e, slot):
            cp.start()
        start_if_any(next_ref[j], lax.rem(slot + 1, _W_SLOTS))

    @pl.when(jnp.logical_and(fresh, live))
    def _():
        for cp in copies(e, slot):
            cp.wait()
        start_if_any(next_ref[MOE_NBLK + j], lax.rem(slot + 2, _W_SLOTS))

        wg_scr[...] = wg_buf[slot].astype(BF16)
        wu_scr[...] = wu_buf[slot].astype(BF16)
        wd_scr[...] = wd_buf[slot].astype(BF16)

    @pl.when(live)
    def _():
        x = _unpack_rows(x_ref[...])
        gt = jnp.dot(x, wg_scr[...], preferred_element_type=F32)
        up = jnp.dot(x, wu_scr[...], preferred_element_type=F32)
        hb = (_silu(gt) * up).astype(BF16)
        o_ref[...] = jnp.dot(hb, wd_scr[...], preferred_element_type=F32).astype(o_ref.dtype)

    @pl.when(jnp.logical_not(live))
    def _():
        o_ref[...] = jnp.zeros(o_ref.shape, o_ref.dtype)


def _experts(x_pad, blk_e, n_used, slot, nxt, w_gate, w_up, w_down, layer):
    hbm = pl.BlockSpec(memory_space=pl.ANY)
    grid_spec = pltpu.PrefetchScalarGridSpec(
        num_scalar_prefetch=4, grid=(MOE_NBLK,),
        in_specs=[pl.BlockSpec((MOE_BLK, _PACK_W), lambda j, *_: (j, 0)), hbm, hbm, hbm],
        out_specs=pl.BlockSpec((MOE_BLK, D_MODEL), lambda j, *_: (j, 0)),
        scratch_shapes=[pltpu.VMEM((_W_SLOTS, D_MODEL, D_EXPERT), F32),
                        pltpu.VMEM((_W_SLOTS, D_MODEL, D_EXPERT), F32),
                        pltpu.VMEM((_W_SLOTS, D_EXPERT, D_MODEL), F32),
                        pltpu.VMEM((D_MODEL, D_EXPERT), BF16), pltpu.VMEM((D_MODEL, D_EXPERT), BF16),
                        pltpu.VMEM((D_EXPERT, D_MODEL), BF16),
                        pltpu.SemaphoreType.DMA((_W_SLOTS, 3, _W_PARTS))])
    return pl.pallas_call(
        functools.partial(_expert_kernel, layer=layer), grid_spec=grid_spec,
        out_shape=jax.ShapeDtypeStruct((MOE_NBLK * MOE_BLK, D_MODEL), BF16),
        compiler_params=_params(("arbitrary",)), name="experts",
    )(blk_e, n_used, slot, nxt, x_pad, w_gate, w_up, w_down)


def _final_kernel(x_ref, ya_ref, yb_ref, rec_ref, gate_ref, fg_ref, o_ref):
    o_ref[...] = _rms(_with_pending(x_ref, (ya_ref, yb_ref, rec_ref, gate_ref)), fg_ref[...])


def _final_norm(x, pend, final_g):
    tm = 512
    blk = pl.BlockSpec((tm, D_MODEL), lambda i: (i, 0))
    return pl.pallas_call(
        _final_kernel, grid=(N_TOK // tm,),
        in_specs=[blk] + _pending_specs(tm) + [pl.BlockSpec((1, D_MODEL), lambda i: (0, 0))],
        out_specs=blk, out_shape=jax.ShapeDtypeStruct((N_TOK, D_MODEL), F32),
        compiler_params=_params(("parallel",)), name="final_norm",
    )(x, *pend, final_g.reshape(1, D_MODEL))


_SC_WORKERS = 32
_SC_CORES = 2
_SC_ROWS = 64


def _dispatch_rows(hp, dest):
    n, width = hp.shape
    per_w = n // _SC_WORKERS
    n_ch = per_w // _SC_ROWS
    idx = dest.T.reshape(2, _SC_WORKERS, n_ch, _SC_ROWS)
    mesh = plsc.VectorSubcoreMesh(core_axis_name="c", subcore_axis_name="s")

    @functools.partial(
        pl.kernel, mesh=mesh, out_type=jax.ShapeDtypeStruct((MOE_NBLK * MOE_BLK, width), hp.dtype),
        scratch_types=[pltpu.VMEM((n_ch, _SC_ROWS), jnp.int32), pltpu.VMEM((n_ch, _SC_ROWS), jnp.int32),
                       pltpu.VMEM((_SC_ROWS, width), hp.dtype)], name="dispatch_rows")
    def scatter(h_hbm, idx_hbm, out_hbm, i0_v, i1_v, rows_v):
        wid = lax.axis_index("s") * _SC_CORES + lax.axis_index("c")
        pltpu.sync_copy(idx_hbm.at[0, wid], i0_v)
        pltpu.sync_copy(idx_hbm.at[1, wid], i1_v)

        @pl.loop(0, n_ch)
        def _(g):
            pltpu.sync_copy(h_hbm.at[pl.ds(wid * per_w + g * _SC_ROWS, _SC_ROWS)], rows_v)
            pltpu.sync_copy(rows_v, out_hbm.at[i0_v.at[g]])
            pltpu.sync_copy(rows_v, out_hbm.at[i1_v.at[g]])

    return scatter(hp, idx)


def _moe(h, rec, cnt, mods, w_gate, w_up, w_down, layer):
    dest, blk_e, n_used, slot, nxt = _plan(rec, cnt)
    y_pad = _experts(_dispatch_rows(h, dest), blk_e, n_used, slot, nxt, w_gate, w_up, w_down, layer)
    return y_pad[dest[:, 0]], y_pad[dest[:, 1]], rec, mods


def kernel(x_prompt, x_sample, c, cache_k, cache_v, state_delta, c_ctx, ada_w, ada_b, norm1_g, norm2_g, final_g,
           ev_w_in, ev_w_out, ev_conv_w, ev_a_log, ev_dt_bias, ev_onorm_g, ev_rpb, od_w_in, od_ln_g, od_ln_b,
           od_w_s, od_b_s, od_w_out, moe_w_rg, moe_b_rg, moe_w_re, moe_b_re, moe_w_gate, moe_w_up, moe_w_down):
    x = jnp.concatenate([x_prompt.reshape(N_CTX, D_MODEL), x_sample.reshape(N_LAT, D_MODEL)], axis=0)
    cond = jnp.concatenate([c_ctx[None, :], c, jnp.zeros((N_COND - 1 - DEC_BATCH, D_MODEL), F32)], axis=0)
    mods_all = _ada_mods(cond, ada_w, ada_b)
    kctx_all = cache_k.reshape(DEC_BATCH, -1, PAST_LEN, NA_HEADS * NA_HD)
    vctx_all = cache_v.reshape(DEC_BATCH, -1, PAST_LEN, NA_HEADS * NA_HD)

    ks, vs, ss = [], [], []
    pend = None
    for l in range(DEPTH):
        mods = mods_all[l]
        router = _router_weights(moe_w_rg[l], moe_b_rg[l], moe_w_re[l], moe_b_re[l])
        if l % 2 == 0:
            e = l // 2
            proj, ab, kv, x = _even_proj(x, pend, mods, norm1_g[l], ev_w_in[e])
            dn = (proj, ab, ev_conv_w[e], ev_a_log[e], ev_dt_bias[e], ev_onorm_g[e])
            oa_ctx, s_fin = _delta_heads(*dn, SEQ, BATCH, 0, None)
            oa_lat, _ = _delta_heads(*dn, DEC_SEQ, DEC_BATCH, N_CTX // DEC_SEQ, state_delta[:, e])
            ob_ctx = _ctx_attention(proj)
            ob_lat = _na_attention(proj, kctx_all[:, e], vctx_all[:, e], ev_rpb[e])
            x, h, rec, cnt = _even_out(oa_ctx, ob_ctx, oa_lat, ob_lat, x, ev_w_out[e], mods, norm2_g[l],
                                       router)
            na_w = NA_HEADS * NA_HD
            ks.append(kv[:N_CTX, :na_w].reshape(BATCH, SEQ, NA_HEADS, NA_HD))
            vs.append(kv[:N_CTX, na_w:].reshape(BATCH, SEQ, NA_HEADS, NA_HD))
            ss.append(s_fin)
        else:
            o = l // 2
            x, h, rec, cnt = _sgu_layer(x, pend, mods, norm1_g[l], od_w_in[o], od_ln_g[o], od_ln_b[o],
                                        od_w_s[o], od_b_s[o], od_w_out[o], norm2_g[l], router)
        pend = _moe(h, rec, cnt, mods, moe_w_gate, moe_w_up, moe_w_down, l)
    x = _final_norm(x, pend, final_g)

    y_prompt = x[:N_CTX].reshape(BATCH, SEQ, D_MODEL)
    y_sample = x[N_CTX:].reshape(DEC_BATCH, DEC_SEQ, D_MODEL)
    return (y_prompt, y_sample, jnp.stack(ks, axis=1), jnp.stack(vs, axis=1), jnp.stack(ss, axis=1))
```

```python
import functools

import jax
import jax.numpy as jnp
from jax import lax
from jax.experimental import pallas as pl
from jax.experimental.pallas import tpu as pltpu
from jax.experimental.pallas import tpu_sc as plsc

F32 = jnp.float32
BF16 = jnp.bfloat16

D_MODEL = 1024
BATCH = 16
SEQ = 256
DEPTH = 4
DEC_BATCH = 4
DEC_SEQ = 2048
PAST_LEN = 512
GRID_W = 64
EPS = 1e-6
NEG_INF = -1e30

DN_HEADS = 4
DN_DK = 128
DN_CHUNK = 64
NA_HEADS = 8
NA_HD = 64
NA_ROWS = 8
NA_COLS = 16
SG_CHUNK = 128
SG_GROUPS = 8
SG_W = 2 * D_MODEL
SG_GW = SG_W // SG_GROUPS
N_EGROUPS = 4
EXP_PER_GROUP = 8
N_EXPERTS = 32
D_EXPERT = 512

N_CTX = BATCH * SEQ
N_LAT = DEC_BATCH * DEC_SEQ
N_TOK = N_CTX + N_LAT
N_COND = 8
PROJ_W = 4096
LANES = 128
MOE_BLK = 256
MOE_NBLK = -(-(2 * N_TOK + N_EXPERTS * (MOE_BLK - 1)) // MOE_BLK)
VMEM_LIMIT = 56 * 1024 * 1024

_QA, _KA, _VA, _ZA, _QB, _KB, _VB = 0, 4, 8, 12, 16, 20, 24


def _params(sem):
    return pltpu.CompilerParams(dimension_semantics=sem, vmem_limit_bytes=VMEM_LIMIT)


def _bdot(a, b):
    return jnp.dot(a.astype(BF16), b.astype(BF16), preferred_element_type=F32)


def _bdot_nt(a, b):
    return lax.dot_general(a.astype(BF16), b.astype(BF16), (((1,), (1,)), ((), ())),
                           preferred_element_type=F32)


def _bdot_tn(a, b):
    return lax.dot_general(a.astype(BF16), b.astype(BF16), (((0,), (0,)), ((), ())),
                           preferred_element_type=F32)


def _split2(a):
    p0 = a.astype(BF16)
    return p0, (a - p0.astype(F32)).astype(BF16)


def _dot3(a, b):
    ah = a.astype(BF16)
    al = (a - ah.astype(F32)).astype(BF16)
    bh = b.astype(BF16)
    bl = (b - bh.astype(F32)).astype(BF16)
    return (jnp.dot(ah, bh, preferred_element_type=F32) + jnp.dot(ah, bl, preferred_element_type=F32)
            + jnp.dot(al, bh, preferred_element_type=F32))


def _mask_bf16(m01):
    return jnp.where(m01, 1.0, 0.0).astype(BF16)


def _xdot(m01, a):
    m = _mask_bf16(m01)
    p0, p1 = _split2(a)
    return jnp.dot(m, p0, preferred_element_type=F32) + jnp.dot(m, p1, preferred_element_type=F32)


def _xdot_r(a, m01):
    m = _mask_bf16(m01)
    p0, p1 = _split2(a)
    return jnp.dot(p0, m, preferred_element_type=F32) + jnp.dot(p1, m, preferred_element_type=F32)


def _sigmoid(x):
    return 0.5 * jnp.tanh(0.5 * x) + 0.5


def _silu(x):
    return x * _sigmoid(x)


def _rms(x, g):
    return x * lax.rsqrt(jnp.mean(x * x, axis=-1, keepdims=True) + EPS) * g


def _cond_index(row):
    return jnp.where(row < N_CTX, 0, 1 + (row - N_CTX) // DEC_SEQ)


def _mod_spec(k, tm, tile0=0):
    return pl.BlockSpec((None, None, 1, D_MODEL), lambda i, *_: (_cond_index((i + tile0) * tm), k, 0, 0))


def _ada_kernel(c_ref, w_ref, b_ref, o_ref):
    o_ref[...] = _bdot(_silu(c_ref[...]), w_ref[...]) + b_ref[...]


def _ada_mods(cond, ada_w, ada_b):
    tn = 1536
    out = pl.pallas_call(
        _ada_kernel, grid=(DEPTH, 6 * D_MODEL // tn),
        in_specs=[pl.BlockSpec((N_COND, D_MODEL), lambda l, j: (0, 0)),
                  pl.BlockSpec((None, D_MODEL, tn), lambda l, j: (l, 0, j)),
                  pl.BlockSpec((None, 1, tn), lambda l, j: (l, 0, j))],
        out_specs=pl.BlockSpec((None, N_COND, tn), lambda l, j: (l, 0, j)),
        out_shape=jax.ShapeDtypeStruct((DEPTH, N_COND, 6 * D_MODEL), F32),
        compiler_params=_params(("parallel", "parallel")), name="ada_mods",
    )(cond, ada_w, ada_b.reshape(DEPTH, 1, 6 * D_MODEL))
    return out.reshape(DEPTH, N_COND, 6, 1, D_MODEL)


_EV_TN = 512
_EV_W = 7 * DN_HEADS * LANES
_KV_COL0 = _KB * LANES


def _with_pending(x_ref, pend):
    if not pend:
        return x_ref[...]
    ya_ref, yb_ref, rec_ref, gate_ref = pend
    rec = rec_ref[...]
    y = (rec[:, _R_W:_R_W + 1] * ya_ref[...].astype(F32)
         + rec[:, _R_W + 1:_R_W + 2] * yb_ref[...].astype(F32))
    return x_ref[...] + gate_ref[...] * y


def _pending_specs(tm, tile0=0):
    blk = pl.BlockSpec((tm, D_MODEL), lambda i: (i + tile0, 0))
    return [blk, blk, pl.BlockSpec((tm, LANES), lambda i: (i + tile0, 0)), _mod_spec(5, tm, tile0)]


def _even_proj_kernel(x_ref, *rest, n_pend, ctx_tiles):
    pend, (g_ref, sh_ref, sc_ref, w_ref, wab_ref, o_ref, ab_ref, kv_ref) = rest[:n_pend], rest[n_pend:n_pend + 8]
    if n_pend == 1:
        x = jnp.where(pl.program_id(0) < ctx_tiles, x_ref[...], pend[0][...])
    else:
        x = _with_pending(x_ref, pend)
    if pend:
        rest[n_pend + 8][...] = x
    h = (_rms(x, g_ref[...]) * (1.0 + sc_ref[...]) + sh_ref[...]).astype(BF16)
    ab_ref[...] = jnp.dot(h, wab_ref[...], preferred_element_type=F32)
    for j in range(_EV_W // _EV_TN):
        c0 = j * _EV_TN
        y = jnp.dot(h, w_ref[:, c0:c0 + _EV_TN], preferred_element_type=F32)
        for c in range(_EV_TN // LANES):
            o_ref[c0 // LANES + c] = y[:, c * LANES:(c + 1) * LANES].astype(BF16)
        if c0 >= _KV_COL0:
            kv_ref[:, c0 - _KV_COL0:c0 - _KV_COL0 + _EV_TN] = y


def _even_proj(x, pend, mods, g, w_in):
    tm = 512
    ctx_tiles = N_CTX // tm
    rows = pl.BlockSpec((tm, D_MODEL), lambda i: (i, 0))
    if isinstance(x, tuple):
        x, pend = x[0], (x[1],)
        x_specs = [pl.BlockSpec((tm, D_MODEL), lambda i: (jnp.minimum(i, ctx_tiles - 1), 0)),
                   pl.BlockSpec((tm, D_MODEL), lambda i: (jnp.maximum(i - ctx_tiles, 0), 0))]
    else:
        pend = tuple(pend) if pend else ()
        x_specs = [rows] + (_pending_specs(tm) if pend else [])
    n_ab = 4 * DN_HEADS
    ab0 = 4 * DN_HEADS * DN_DK
    w_main = jnp.concatenate([w_in[:, :ab0], w_in[:, ab0 + n_ab:]], axis=1).astype(BF16)
    w_ab = jnp.concatenate([w_in[:, ab0:ab0 + n_ab], jnp.zeros((D_MODEL, LANES - n_ab), F32)],
                           axis=1).astype(BF16)
    held = lambda shape: pl.BlockSpec(shape, lambda i: (0,) * len(shape), pipeline_mode=pl.Buffered(1))
    out_specs = [pl.BlockSpec((_EV_W // LANES, tm, LANES), lambda i: (0, i, 0)),
                 pl.BlockSpec((tm, LANES), lambda i: (i, 0)),
                 pl.BlockSpec((tm, 2 * NA_HEADS * NA_HD), lambda i: (i, 0))]
    out_shape = [jax.ShapeDtypeStruct((_EV_W // LANES, N_TOK, LANES), BF16),
                 jax.ShapeDtypeStruct((N_TOK, LANES), F32),
                 jax.ShapeDtypeStruct((N_TOK, 2 * NA_HEADS * NA_HD), F32)]
    if pend:
        out_specs.append(rows)
        out_shape.append(jax.ShapeDtypeStruct((N_TOK, D_MODEL), F32))
    res = pl.pallas_call(
        functools.partial(_even_proj_kernel, n_pend=len(pend), ctx_tiles=ctx_tiles), grid=(N_TOK // tm,),
        in_specs=x_specs
        + [pl.BlockSpec((1, D_MODEL), lambda i: (0, 0)), _mod_spec(0, tm), _mod_spec(1, tm),
           held((D_MODEL, _EV_W)), held((D_MODEL, LANES))],
        out_specs=out_specs, out_shape=out_shape,
        compiler_params=_params(("parallel",)), name="even_proj",
    )(x, *pend, g.reshape(1, D_MODEL), mods, mods, w_main, w_ab)
    return (*res[:3], res[3] if pend else x)


_CHUNK_SHIFT = DN_CHUNK.bit_length() - 1
_CUM_ROWS = 256
_DN_CHAINS = 16
_DN_SHORT = 256
_SERIES_FINE = 3
_MQ_ROWS = DN_DK + DN_CHUNK


def _dn_kernel(*refs, T, HB, has_s0, want_state):
    it = iter(refs)
    q_ref, k_ref, v_ref, z_ref, ab_ref = (next(it) for _ in range(5))
    cwq_ref, cwk_ref, cwv_ref, alog_ref, dtb_ref, og_ref = (next(it) for _ in range(6))
    s0_ref = next(it) if has_s0 else None
    o_ref = next(it)
    sfin_ref = next(it) if want_state else None
    qc, kc, vc, gsc, bsc, osc, b_s, mq_s = (next(it) for _ in range(8))

    C = DN_CHUNK
    n = T // C
    h0 = pl.program_id(1) * HB

    row = lax.broadcasted_iota(jnp.int32, (T, 1), 0)

    def conv(x_ref, cw_ref, hh):
        x = x_ref[hh].astype(F32)
        cw = cw_ref[:, hh * LANES:(hh + 1) * LANES]
        xp = jnp.where(row == 0, 0.0, pltpu.roll(x, 1, 0))
        xn = jnp.where(row == T - 1, 0.0, pltpu.roll(x, T - 1, 0))
        return _silu(cw[0:1, :] * xp + cw[1:2, :] * x + cw[2:3, :] * xn)

    def l2n(x):
        return x * lax.rsqrt(jnp.sum(x * x, axis=-1, keepdims=True) + EPS)

    ab = ab_ref[...]
    lane = lax.broadcasted_iota(jnp.int32, (1, LANES), 1)
    dtb = jnp.zeros((1, LANES), F32)
    alog = jnp.zeros((1, LANES), F32)
    for d in range(2):
        for hq in range(DN_HEADS):
            dtb = jnp.where(lane == d * DN_HEADS + hq, dtb_ref[d, hq], dtb)
            alog = jnp.where(lane == d * DN_HEADS + hq, alog_ref[d, hq], alog)
    xs = ab + dtb
    g_all = -jnp.exp(alog) * (jnp.maximum(xs, 0.0) + jnp.log1p(jnp.exp(-jnp.abs(xs))))
    beta_all = _sigmoid(ab)

    sel_r = lax.broadcasted_iota(jnp.int32, (LANES, LANES), 0)
    for hh in range(HB):
        qc[hh] = l2n(conv(q_ref, cwq_ref, hh)) * (DN_DK ** -0.5)
        kc[hh] = l2n(conv(k_ref, cwk_ref, hh))
        vc[hh] = conv(v_ref, cwv_ref, hh)
        hd = h0 + hh
        for d in range(2):
            gsc[hh, d] = _xdot_r(g_all, sel_r == d * DN_HEADS + hd)
            bsc[hh, d] = _xdot_r(beta_all, sel_r == 2 * DN_HEADS + d * DN_HEADS + hd)

    pr = lax.broadcasted_iota(jnp.int32, (_CUM_ROWS, _CUM_ROWS), 0)
    pc = lax.broadcasted_iota(jnp.int32, (_CUM_ROWS, _CUM_ROWS), 1)
    same = lax.shift_right_logical(pr, _CHUNK_SHIFT) == lax.shift_right_logical(pc, _CHUNK_SHIFT)
    cum_mask = (jnp.logical_and(same, pc <= pr), jnp.logical_and(same, pc >= pr))

    def cum_body(i, carry):
        sl = pl.ds(pl.multiple_of(i * _CUM_ROWS, _CUM_ROWS), _CUM_ROWS)
        for hh in range(HB):
            for d in range(2):
                gsc[hh, d, sl, :] = _xdot(cum_mask[d], gsc[hh, d, sl, :])
        return carry

    lax.fori_loop(0, T // _CUM_ROWS, cum_body, 0)

    ri = lax.broadcasted_iota(jnp.int32, (C, C), 0)
    ci = lax.broadcasted_iota(jnp.int32, (C, C), 1)
    eye = (ri == ci).astype(F32)

    def prepare(items):
        lows, decays = [], []
        kk, qk = {}, {}
        for hh, d, c, slot in items:
            sl = pl.ds(pl.multiple_of(c * C, C), C)
            gc = gsc[hh, d, sl, :]
            if (hh, slot) not in kk:
                k = kc[hh, sl, :]
                kk[hh, slot] = _bdot_nt(k, k)
                qk[hh, slot] = _bdot_nt(qc[hh, sl, :], k)
            incl = (ci <= ri) if d == 0 else (ci >= ri)
            strict = (ci < ri) if d == 0 else (ci > ri)
            gr = jnp.transpose(gc)[0:1, :C]
            decay = jnp.where(incl, jnp.exp(jnp.where(incl, gc[:, :C] - gr, 0.0)), 0.0)
            lows.append(jnp.where(strict, bsc[hh, d, sl, :C] * kk[hh, slot] * decay, 0.0))
            decays.append(decay)
        ts = [eye - low for low in lows]
        ps = lows
        for step in range(5):
            dot = _dot3 if step < _SERIES_FINE else _bdot
            ps = [dot(p, p) for p in ps]
            ts = [t + dot(t, p) for t, p in zip(ts, ps)]
        for (hh, d, c, slot), t, decay in zip(items, ts, decays):
            sl = pl.ds(pl.multiple_of(c * C, C), C)
            q, k, gc, beta = qc[hh, sl, :], kc[hh, sl, :], gsc[hh, d, sl, :], bsc[hh, d, sl, :]
            eg = jnp.exp(gc)
            uw = _bdot(t, jnp.concatenate([vc[hh, sl, :] * beta, k * beta * eg], axis=-1))
            last = gc[C - 1:C, :] if d == 0 else gc[0:1, :]
            wu = jnp.concatenate([uw[:, LANES:], uw[:, :LANES]], axis=-1).astype(BF16)
            kd = (k * jnp.exp(last - gc)).astype(BF16)
            attn = (qk[hh, slot] * decay).astype(BF16)
            kdwu = lax.dot_general(kd, wu, (((0,), (0,)), ((), ())), preferred_element_type=F32)
            awu = jnp.dot(attn, wu, preferred_element_type=F32)
            mq0 = pl.multiple_of(c * _MQ_ROWS, _MQ_ROWS)
            mq_s[hh, d, pl.ds(mq0, DN_DK), :] = kdwu[:, :LANES].astype(BF16)
            mq_s[hh, d, pl.ds(mq0 + DN_DK, C), :] = (q * eg - awu[:, :LANES]).astype(BF16)
            b_s[hh, d, pl.ds(pl.multiple_of(c * DN_DK, DN_DK), DN_DK), :] = kdwu[:, LANES:]
            osc[hh, d, sl, :] = awu[:, LANES:]

    n_prep = min(n, _DN_CHAINS // 2)
    h_prep = max(1, min(HB, _DN_CHAINS // (2 * n_prep)))

    def prep_body(i, carry):
        for hg in range(0, HB, h_prep):
            prepare([(hh, d, i * n_prep + j, j) for hh in range(hg, hg + h_prep) for j in range(n_prep)
                     for d in range(2)])
        return carry

    lax.fori_loop(0, n // n_prep, prep_body, 0)

    def advance(hh, d, c, S):
        sl = pl.ds(pl.multiple_of(c * C, C), C)
        ms = jnp.dot(mq_s[hh, d, pl.ds(pl.multiple_of(c * _MQ_ROWS, _MQ_ROWS), _MQ_ROWS), :], S.astype(BF16),
                     preferred_element_type=F32)
        osc[hh, d, sl, :] = osc[hh, d, sl, :] + ms[DN_DK:]
        last = gsc[hh, d, pl.ds(c * C + (C - 1 if d == 0 else 0), 1), :]
        return (S * jnp.exp(last) - ms[:DN_DK]
                + b_s[hh, d, pl.ds(pl.multiple_of(c * DN_DK, DN_DK), DN_DK), :])

    def body(i, carry):
        return tuple(advance(hh, d, i if d == 0 else n - 1 - i, carry[2 * hh + d])
                     for hh in range(HB) for d in range(2))

    if has_s0:
        init = tuple(s0_ref[d, hh] for hh in range(HB) for d in range(2))
    else:
        init = tuple(jnp.zeros((DN_DK, LANES), F32) for _ in range(2 * HB))
    fin = lax.fori_loop(0, n, body, init)
    for hh in range(HB):
        if want_state:
            sfin_ref[0, hh] = fin[2 * hh]
            sfin_ref[1, hh] = fin[2 * hh + 1]
        o = osc[hh, 0] + osc[hh, 1]
        o_ref[hh] = (_rms(o, og_ref[...]) * _silu(z_ref[hh].astype(F32))).astype(o_ref.dtype)


def _delta_heads(proj, ab, conv_w, a_log, dt_bias, onorm_g, T, n_seq, row0, s0):
    has_s0 = s0 is not None
    want_state = not has_s0
    hb = DN_HEADS if T <= _DN_SHORT else 1

    def col(cb):
        return pl.BlockSpec((hb, T, LANES), lambda s, h: (cb // hb + h, row0 + s, 0))

    def cw(cb):
        return pl.BlockSpec((3, hb * LANES), lambda s, h: (0, cb // hb + h))

    smem = pl.BlockSpec(memory_space=pltpu.SMEM)
    in_specs = [col(_QA), col(_KA), col(_VA), col(_ZA),
                pl.BlockSpec((T, LANES), lambda s, h: (row0 + s, 0)),
                cw(0), cw(4), cw(8), smem, smem,
                pl.BlockSpec((1, LANES), lambda s, h: (0, 0))]
    args = [proj, proj, proj, proj, ab, conv_w, conv_w, conv_w, a_log, dt_bias,
            onorm_g.reshape(1, LANES)]
    state_spec = pl.BlockSpec((None, 2, hb, DN_DK, LANES), lambda s, h: (s, 0, h, 0, 0))
    if has_s0:
        in_specs.append(state_spec)
        args.append(s0)
    out_shape = [jax.ShapeDtypeStruct((DN_HEADS, n_seq * T, LANES), BF16)]
    out_specs = [pl.BlockSpec((hb, T, LANES), lambda s, h: (h, s, 0))]
    if want_state:
        out_shape.append(jax.ShapeDtypeStruct((n_seq, 2, DN_HEADS, DN_DK, LANES), F32))
        out_specs.append(state_spec)
    res = pl.pallas_call(
        functools.partial(_dn_kernel, T=T, HB=hb, has_s0=has_s0, want_state=want_state),
        grid=(n_seq, DN_HEADS // hb), in_specs=in_specs, out_specs=out_specs, out_shape=out_shape,
        scratch_shapes=[pltpu.VMEM((hb, T, LANES), F32)] * 3
        + [pltpu.VMEM((hb, 2, T, LANES), F32)] * 3
        + [pltpu.VMEM((hb, 2, T // DN_CHUNK * DN_DK, LANES), F32),
           pltpu.VMEM((hb, 2, T // DN_CHUNK * _MQ_ROWS, LANES), BF16)],
        compiler_params=_params(("parallel", "parallel")), name="delta_heads_%d" % T,
    )(*args)
    return res if want_state else (res[0], None)


def _pair_queries(q, first):
    return jnp.concatenate([jnp.where(first, q, 0.0), jnp.where(first, 0.0, q)], axis=0).astype(BF16)


def _ctx_attn_kernel(q_ref, k_ref, v_ref, o_ref):
    first = lax.broadcasted_iota(jnp.int32, (SEQ, LANES), 1) < NA_HD
    qm = _pair_queries(q_ref[...] * (NA_HD ** -0.5), first)
    s = lax.dot_general(k_ref[...], qm, (((1,), (1,)), ((), ())), preferred_element_type=F32)
    e = jnp.exp(s - jnp.max(s, axis=0, keepdims=True))
    den = jnp.sum(e, axis=0, keepdims=True)
    o = lax.dot_general(e.astype(BF16), v_ref[...], (((0,), (0,)), ((), ())), preferred_element_type=F32)
    o = jnp.where(first, o[:SEQ], o[SEQ:])
    den_t = jnp.transpose(jnp.broadcast_to(den, (LANES, 2 * SEQ)))
    o_ref[...] = (o / jnp.where(first, den_t[:SEQ], den_t[SEQ:])).astype(o_ref.dtype)


def _ctx_attention(proj):
    def col(cb):
        return pl.BlockSpec((None, SEQ, LANES), lambda s, p: (cb + p, s, 0))

    return pl.pallas_call(
        _ctx_attn_kernel, grid=(BATCH, NA_HEADS // 2),
        in_specs=[col(_QB), col(_KB), col(_VB)],
        out_specs=pl.BlockSpec((None, SEQ, LANES), lambda s, p: (p, s, 0)),
        out_shape=jax.ShapeDtypeStruct((NA_HEADS // 2, N_CTX, LANES), BF16),
        compiler_params=_params(("parallel", "parallel")), name="ctx_attention",
    )(proj, proj, proj)


_NA_UNROLL = 4


def _na_kernel(q_ref, k_ref, v_ref, kc_ref, vc_ref, bias_ref, o_ref, kcb_scr, vcb_scr):
    rows = DEC_SEQ // GRID_W
    win = NA_ROWS * GRID_W
    scale = NA_HD ** -0.5
    dn_nt = (((1,), (1,)), ((), ()))
    dn_tn = (((0,), (0,)), ((), ()))

    kcb_scr[...] = kc_ref[...].astype(BF16)
    vcb_scr[...] = vc_ref[...].astype(BF16)
    first = lax.broadcasted_iota(jnp.int32, (GRID_W, LANES), 1) < NA_HD

    def body(it, carry):
        rr = [it * _NA_UNROLL + j for j in range(_NA_UNROLL)]
        rss = [jnp.clip(r - NA_ROWS // 2, 0, rows - NA_ROWS) for r in rr]
        qsls = [pl.ds(pl.multiple_of(r * GRID_W, GRID_W), GRID_W) for r in rr]
        wsls = [pl.ds(pl.multiple_of(rs * GRID_W, GRID_W), win) for rs in rss]
        qms, s_wins, s_ctxs = [], [], []
        for r, rs, qsl, wsl in zip(rr, rss, qsls, wsls):
            qm = _pair_queries(q_ref[qsl, :] * scale, first)
            bias = jnp.concatenate([bias_ref[NA_ROWS - 1 - (r - rs) + i] for i in range(NA_ROWS)], axis=0)
            s_wins.append(lax.dot_general(k_ref[wsl, :], qm, dn_nt, preferred_element_type=F32) + bias)
            s_ctxs.append(lax.dot_general(kcb_scr[...], qm, dn_nt, preferred_element_type=F32))
        ms = [jnp.maximum(jnp.max(sw, axis=0, keepdims=True), jnp.max(sc, axis=0, keepdims=True))
              for sw, sc in zip(s_wins, s_ctxs)]
        e_wins = [jnp.exp(sw - m) for sw, m in zip(s_wins, ms)]
        e_ctxs = [jnp.exp(sc - m) for sc, m in zip(s_ctxs, ms)]
        dens = [jnp.sum(ew, axis=0, keepdims=True) + jnp.sum(ec, axis=0, keepdims=True)
                for ew, ec in zip(e_wins, e_ctxs)]
        for qsl, wsl, ew, ec, den in zip(qsls, wsls, e_wins, e_ctxs, dens):
            o = (lax.dot_general(ew.astype(BF16), v_ref[wsl, :], dn_tn, preferred_element_type=F32)
                 + lax.dot_general(ec.astype(BF16), vcb_scr[...], dn_tn, preferred_element_type=F32))
            o = o / jnp.transpose(jnp.broadcast_to(den, (LANES, LANES)))
            o_ref[qsl, :] = jnp.where(first, o[:GRID_W], o[GRID_W:]).astype(o_ref.dtype)
        return carry

    lax.fori_loop(0, rows // _NA_UNROLL, body, 0)


def _na_bias_table(rpb):
    col = jnp.arange(GRID_W)
    cs = jnp.clip(col - NA_COLS // 2, 0, GRID_W - NA_COLS)
    col_ok = (col[None, :] >= cs[:, None]) & (col[None, :] < cs[:, None] + NA_COLS)
    dc = jnp.clip(col[None, :] - col[:, None] + NA_COLS - 1, 0, 2 * NA_COLS - 2)
    onehot = (dc.T[None, :, :] == jnp.arange(2 * NA_COLS - 1)[:, None, None]).astype(F32)
    t = jnp.einsum('hrd,dkq->hrkq', rpb.astype(F32), onehot, precision=lax.Precision.HIGHEST)
    t = jnp.where(col_ok.T[None, None], t, NEG_INF)
    t = t.reshape(NA_HEADS // 2, 2, 2 * NA_ROWS - 1, GRID_W, GRID_W)
    return jnp.concatenate([t[:, 0], t[:, 1]], axis=-1)


def _na_attention(proj, kctx, vctx, rpb):
    blk = N_CTX // DEC_SEQ

    def col(cb):
        return pl.BlockSpec((None, DEC_SEQ, LANES), lambda b, p: (cb + p, blk + b, 0))

    ctx = pl.BlockSpec((None, PAST_LEN, LANES), lambda b, p: (b, 0, p))
    return pl.pallas_call(
        _na_kernel, grid=(DEC_BATCH, NA_HEADS // 2),
        in_specs=[col(_QB), col(_KB), col(_VB), ctx, ctx,
                  pl.BlockSpec((None, 2 * NA_ROWS - 1, GRID_W, 2 * GRID_W), lambda b, p: (p, 0, 0, 0))],
        out_specs=pl.BlockSpec((None, DEC_SEQ, LANES), lambda b, p: (p, b, 0)),
        out_shape=jax.ShapeDtypeStruct((NA_HEADS // 2, N_LAT, LANES), BF16),
        scratch_shapes=[pltpu.VMEM((PAST_LEN, LANES), BF16), pltpu.VMEM((PAST_LEN, LANES), BF16)],
        compiler_params=_params(("parallel", "parallel")), name="na_attention",
    )(proj, proj, proj, kctx, vctx, _na_bias_table(rpb))


_LOGIT0 = N_EGROUPS
_R_E, _R_W, _R_RANK = 0, 2, 4


def _lane_min_where(mask, lane):
    return jnp.min(jnp.where(mask, lane, LANES), axis=-1, keepdims=True)


def _route_rows(lg, carry_ref, tri_ref):
    big = -3.0e38
    lane = lax.broadcasted_iota(jnp.int32, lg.shape, 1)
    is_g = lane < N_EGROUPS
    gmax = jnp.max(jnp.where(is_g, lg, big), axis=-1, keepdims=True)
    gsum = jnp.sum(jnp.where(is_g, jnp.exp(jnp.where(is_g, lg - gmax, 0.0)), 0.0), axis=-1, keepdims=True)
    pg_top = 1.0 / gsum
    g_idx = _lane_min_where(jnp.logical_and(is_g, lg == gmax), lane)
    in_g = jnp.logical_and(lane >= _LOGIT0, lax.shift_right_arithmetic(lane - _LOGIT0, 3) == g_idx)
    in_g = jnp.logical_and(in_g, lane < _LOGIT0 + N_EXPERTS)
    m1 = jnp.max(jnp.where(in_g, lg, big), axis=-1, keepdims=True)
    i1 = _lane_min_where(jnp.logical_and(in_g, lg == m1), lane)
    rest = jnp.logical_and(in_g, lane != i1)
    m2 = jnp.max(jnp.where(rest, lg, big), axis=-1, keepdims=True)
    i2 = _lane_min_where(jnp.logical_and(rest, lg == m2), lane)
    e2 = jnp.exp(m2 - m1)
    w1 = pg_top * (1.0 / (1.0 + e2))
    w2 = pg_top * (e2 / (1.0 + e2))
    hit1 = lane == i1
    hit2 = lane == i2
    picked = jnp.where(jnp.logical_or(hit1, hit2), 1.0, 0.0)
    before = jnp.dot(tri_ref[...], picked.astype(BF16), preferred_element_type=F32) + carry_ref[...]
    r1 = jnp.sum(jnp.where(hit1, before, 0.0), axis=-1, keepdims=True)
    r2 = jnp.sum(jnp.where(hit2, before, 0.0), axis=-1, keepdims=True)
    carry_ref[...] = carry_ref[...] + jnp.sum(picked, axis=0, keepdims=True)
    rec = jnp.zeros(lg.shape, F32)
    for ln, val in ((_R_E, (i1 - _LOGIT0).astype(F32)), (_R_E + 1, (i2 - _LOGIT0).astype(F32)),
                    (_R_W, w1), (_R_W + 1, w2), (_R_RANK, r1), (_R_RANK + 1, r2)):
        rec = jnp.where(lane == ln, val, rec)
    return rec


_PACK_W = D_MODEL // 2


def _pack_rows(hb):
    lo = lax.bitcast_convert_type(hb[:, :_PACK_W].astype(F32), jnp.int32)
    hi = lax.bitcast_convert_type(hb[:, _PACK_W:].astype(F32), jnp.int32)
    return jnp.bitwise_or(jnp.bitwise_and(hi, -65536), lax.shift_right_logical(lo, 16))


def _unpack_rows(w):
    lo = lax.bitcast_convert_type(lax.shift_left(w, 16), F32)
    hi = lax.bitcast_convert_type(jnp.bitwise_and(w, -65536), F32)
    return jnp.concatenate([lo, hi], axis=-1).astype(BF16)


def _moe_input(xnew, first, tail_in, tail_out, tail_scr):
    g2_ref, sc2_ref, sh2_ref, wr_ref, br_ref = tail_in
    x_out, h_out, rec_out, cnt_out = tail_out
    tri_scr, carry_scr = tail_scr

    @pl.when(first)
    def _():
        tm = tri_scr.shape[0]
        r = lax.broadcasted_iota(jnp.int32, (tm, tm), 0)
        c = lax.broadcasted_iota(jnp.int32, (tm, tm), 1)
        tri_scr[...] = jnp.where(c < r, 1.0, 0.0).astype(BF16)
        carry_scr[...] = jnp.zeros(carry_scr.shape, F32)

    x_out[...] = xnew
    h = _rms(xnew, g2_ref[...]) * (1.0 + sc2_ref[...]) + sh2_ref[...]
    hh = h.astype(BF16)
    h_out[...] = _pack_rows(hh)
    lg = jnp.dot(hh, wr_ref[...], preferred_element_type=F32) + br_ref[...]
    rec_out[...] = _route_rows(lg, carry_scr, tri_scr)
    cnt_out[...] = carry_scr[...]


def _even_out_kernel(oac_ref, obc_ref, oal_ref, obl_ref, x_ref, w_ref, gate_ref, *rest, ctx_tiles):
    tail_in, tail_out, (w_scr,), tail_scr = rest[:5], rest[5:9], rest[9:10], rest[10:]
    first = pl.program_id(0) == 0

    @pl.when(first)
    def _():
        w_scr[...] = w_ref[...].astype(BF16)

    is_ctx = pl.program_id(0) < ctx_tiles
    parts = [jnp.where(is_ctx, c_ref[hb], l_ref[hb])
             for c_ref, l_ref in ((oac_ref, oal_ref), (obc_ref, obl_ref)) for hb in range(DN_HEADS)]
    mix = jnp.concatenate(parts, axis=-1)
    out = jnp.dot(mix, w_scr[...], preferred_element_type=F32)
    _moe_input(x_ref[...] + gate_ref[...] * out, first, tail_in, tail_out, tail_scr)


def _tail_specs(tm):
    const = lambda shape: pl.BlockSpec(shape, lambda i: (0,) * len(shape))
    in_specs = [_mod_spec(2, tm), const((1, D_MODEL)), _mod_spec(4, tm), _mod_spec(3, tm),
                const((D_MODEL, LANES)), const((1, LANES))]
    out_specs = [pl.BlockSpec((tm, D_MODEL), lambda i: (i, 0)),
                 pl.BlockSpec((tm, _PACK_W), lambda i: (i, 0)),
                 pl.BlockSpec((tm, LANES), lambda i: (i, 0)),
                 const((1, LANES))]
    out_shape = [jax.ShapeDtypeStruct((N_TOK, D_MODEL), F32),
                 jax.ShapeDtypeStruct((N_TOK, _PACK_W), jnp.int32),
                 jax.ShapeDtypeStruct((N_TOK, LANES), F32),
                 jax.ShapeDtypeStruct((1, LANES), F32)]
    scratch = [pltpu.VMEM((tm, tm), BF16), pltpu.VMEM((1, LANES), F32)]
    return in_specs, out_specs, out_shape, scratch


def _router_weights(w_rg, b_rg, w_re, b_re):
    pad = LANES - N_EGROUPS - N_EXPERTS
    w = jnp.concatenate([w_rg, w_re, jnp.zeros((D_MODEL, pad), F32)], axis=1)
    b = jnp.concatenate([b_rg, b_re, jnp.zeros((pad,), F32)]).reshape(1, LANES)
    return w.astype(BF16), b


def _even_out(oa_ctx, ob_ctx, oa_lat, ob_lat, x, w_out, mods, g2, router):
    tm = 512
    ctx_tiles = N_CTX // tm
    tail_in, out_specs, out_shape, tail_scr = _tail_specs(tm)
    ctxblk = pl.BlockSpec((DN_HEADS, tm, LANES), lambda i: (0, jnp.minimum(i, ctx_tiles - 1), 0))
    latblk = pl.BlockSpec((DN_HEADS, tm, LANES), lambda i: (0, jnp.maximum(i - ctx_tiles, 0), 0))
    return pl.pallas_call(
        functools.partial(_even_out_kernel, ctx_tiles=ctx_tiles), grid=(N_TOK // tm,),
        in_specs=[ctxblk, ctxblk, latblk, latblk, pl.BlockSpec((tm, D_MODEL), lambda i: (i, 0)),
                  pl.BlockSpec((D_MODEL, D_MODEL), lambda i: (0, 0))] + tail_in,
        out_specs=out_specs, out_shape=out_shape,
        scratch_shapes=[pltpu.VMEM((D_MODEL, D_MODEL), BF16)] + tail_scr,
        compiler_params=_params(("arbitrary",)), name="even_out",
    )(oa_ctx, ob_ctx, oa_lat, ob_lat, x, w_out, mods, g2.reshape(1, D_MODEL), mods, mods, *router)


def _gelu_tanh(x):
    return x * (0.5 * (1.0 + jnp.tanh(0.7978845608028654 * (x + 0.044715 * (x * x * x)))))


def _sgu_kernel(x_ref, *rest, tm, n_pend):
    pend, rest = rest[:n_pend], rest[n_pend:]
    g1_ref, sh1_ref, sc1_ref, win_ref, lng_ref, lnb_ref, ws_ref, bst_ref, wout_ref, gate_ref = rest[:10]
    rest = rest[10:]
    tail_in, tail_out, (v_scr, m_scr), tail_scr = rest[:5], rest[5:9], rest[9:11], rest[11:]
    first = pl.program_id(0) == 0
    x = _with_pending(x_ref, pend)
    h = (_rms(x, g1_ref[...]) * (1.0 + sc1_ref[...]) + sh1_ref[...]).astype(BF16)

    v = _gelu_tanh(jnp.dot(h, win_ref[:, SG_W:], preferred_element_type=F32))
    mu = jnp.mean(v, axis=-1, keepdims=True)
    vc = v - mu
    var = jnp.mean(vc * vc, axis=-1, keepdims=True)
    v_scr[...] = (vc * lax.rsqrt(var + EPS) * lng_ref[...] + lnb_ref[...]).astype(BF16)

    for g in range(SG_GROUPS):
        cs = slice(g * SG_GW, (g + 1) * SG_GW)
        u = _gelu_tanh(jnp.dot(h, win_ref[:, cs], preferred_element_type=F32))
        w_sp = ws_ref[g].astype(BF16)
        for c in range(tm // SG_CHUNK):
            rs = slice(c * SG_CHUNK, (c + 1) * SG_CHUNK)
            sp = jnp.dot(w_sp, v_scr[rs, cs], preferred_element_type=F32) + bst_ref[:, g:g + 1]
            m_scr[rs, cs] = (u[rs] * sp).astype(BF16)
    out = jnp.dot(m_scr[...], wout_ref[...], preferred_element_type=F32)
    _moe_input(x + gate_ref[...] * out, first, tail_in, tail_out, tail_scr)


def _sgu_layer(x, pend, mods, g1, w_in, ln_g, ln_b, w_s, b_s, w_out, g2, router):
    tm = 512
    pend = tuple(pend) if pend else ()
    tail_in, out_specs, out_shape, tail_scr = _tail_specs(tm)
    const = lambda shape: pl.BlockSpec(shape, lambda i: (0,) * len(shape))
    held = lambda shape: pl.BlockSpec(shape, lambda i: (0,) * len(shape), pipeline_mode=pl.Buffered(1))
    return pl.pallas_call(
        functools.partial(_sgu_kernel, tm=tm, n_pend=len(pend)), grid=(N_TOK // tm,),
        in_specs=[pl.BlockSpec((tm, D_MODEL), lambda i: (i, 0))] + (_pending_specs(tm) if pend else [])
        + [const((1, D_MODEL)), _mod_spec(0, tm), _mod_spec(1, tm),
                  held((D_MODEL, 2 * SG_W)), const((1, SG_W)), const((1, SG_W)),
                  const((SG_GROUPS, SG_CHUNK, SG_CHUNK)), const((SG_CHUNK, SG_GROUPS)),
                  held((SG_W, D_MODEL))] + tail_in,
        out_specs=out_specs, out_shape=out_shape,
        scratch_shapes=[pltpu.VMEM((tm, SG_W), BF16), pltpu.VMEM((tm, SG_W), BF16)] + tail_scr,
        compiler_params=_params(("arbitrary",)), name="sgu_layer",
    )(x, *pend, g1.reshape(1, D_MODEL), mods, mods, w_in.astype(BF16), ln_g.reshape(1, SG_W), ln_b.reshape(1, SG_W),
      w_s, b_s.T, w_out.astype(BF16), mods, g2.reshape(1, D_MODEL), mods, mods, *router)


def _plan(rec, cnt):
    e_idx = rec[:, _R_E:_R_E + 2].astype(jnp.int32)
    rank = rec[:, _R_RANK:_R_RANK + 2].astype(jnp.int32)
    counts = cnt[0, _LOGIT0:_LOGIT0 + N_EXPERTS].astype(jnp.int32)
    padded = (counts + MOE_BLK - 1) // MOE_BLK * MOE_BLK
    pad_end = jnp.cumsum(padded)
    pad_start = pad_end - padded
    hit = e_idx[:, :, None] == jnp.arange(N_EXPERTS, dtype=jnp.int32)[None, None, :]
    dest = jnp.sum(jnp.where(hit, pad_start[None, None, :], 0), axis=-1) + rank
    blk0 = jnp.arange(MOE_NBLK, dtype=jnp.int32) * MOE_BLK
    blk_e = jnp.minimum(jnp.sum((pad_end[None, :] <= blk0[:, None]).astype(jnp.int32), axis=-1),
                        N_EXPERTS - 1)
    n_used = (pad_end[-1] // MOE_BLK).astype(jnp.int32).reshape(1)
    owns = counts > 0
    slot_of = (jnp.cumsum(owns.astype(jnp.int32)) - 1) % _W_SLOTS
    ids = jnp.arange(N_EXPERTS, dtype=jnp.int32)
    later = jnp.logical_and(owns[None, :], ids[None, :] > ids[:, None])
    next_of = jnp.min(jnp.where(later, ids[None, :], N_EXPERTS), axis=-1)
    next2_of = jnp.concatenate([next_of, jnp.full((1,), N_EXPERTS, jnp.int32)])[next_of]
    ahead = jnp.stack([next_of, next2_of], axis=0)
    ahead = jnp.where(ahead == N_EXPERTS, -1, ahead)
    return dest, blk_e, n_used, slot_of[blk_e], ahead[:, blk_e].reshape(-1)


_W_PARTS = 4
_W_SLOTS = 3


def _expert_kernel(blk_e_ref, n_used_ref, slot_ref, next_ref, x_ref, wg_hbm, wu_hbm, wd_hbm, o_ref,
                   wg_buf, wu_buf, wd_buf, wg_scr, wu_scr, wd_scr, sems, *, layer):
    j = pl.program_id(0)
    e = blk_e_ref[j]
    slot = slot_ref[j]
    fresh = jnp.logical_or(j == 0, e != blk_e_ref[jnp.maximum(j - 1, 0)])
    live = j < n_used_ref[0]

    def copies(expert, s):
        out = []
        for m, (hbm, buf) in enumerate(((wg_hbm, wg_buf), (wu_hbm, wu_buf), (wd_hbm, wd_buf))):
            rows = buf.shape[1] // _W_PARTS
            for part in range(_W_PARTS):
                band = pl.ds(part * rows, rows)
                out.append(pltpu.make_async_copy(hbm.at[layer, expert, band], buf.at[s, band],
                                                 sems.at[s, m, part]))
        return out

    def start_if_any(expert, s):
        @pl.when(expert >= 0)
        def _():
            for cp in copies(expert, s):
                cp.start()

    @pl.when(j == 0)
    def _():
        for cp in copies(e, slot):
            cp.start()
        start_if_any(next_ref[j], lax.rem(slot + 1, _W_SLOTS))

    @pl.when(jnp.logical_and(fresh, live))
    def _():
        for cp in copies(e, slot):
            cp.wait()
        start_if_any(next_ref[MOE_NBLK + j], lax.rem(slot + 2, _W_SLOTS))

        wg_scr[...] = wg_buf[slot].astype(BF16)
        wu_scr[...] = wu_buf[slot].astype(BF16)
        wd_scr[...] = wd_buf[slot].astype(BF16)

    @pl.when(live)
    def _():
        x = _unpack_rows(x_ref[...])
        gt = jnp.dot(x, wg_scr[...], preferred_element_type=F32)
        up = jnp.dot(x, wu_scr[...], preferred_element_type=F32)
        hb = (_silu(gt) * up).astype(BF16)
        o_ref[...] = jnp.dot(hb, wd_scr[...], preferred_element_type=F32).astype(o_ref.dtype)

    @pl.when(jnp.logical_not(live))
    def _():
        o_ref[...] = jnp.zeros(o_ref.shape, o_ref.dtype)


def _experts(x_pad, blk_e, n_used, slot, nxt, w_gate, w_up, w_down, layer):
    hbm = pl.BlockSpec(memory_space=pl.ANY)
    grid_spec = pltpu.PrefetchScalarGridSpec(
        num_scalar_prefetch=4, grid=(MOE_NBLK,),
        in_specs=[pl.BlockSpec((MOE_BLK, _PACK_W), lambda j, *_: (j, 0)), hbm, hbm, hbm],
        out_specs=pl.BlockSpec((MOE_BLK, D_MODEL), lambda j, *_: (j, 0)),
        scratch_shapes=[pltpu.VMEM((_W_SLOTS, D_MODEL, D_EXPERT), F32),
                        pltpu.VMEM((_W_SLOTS, D_MODEL, D_EXPERT), F32),
                        pltpu.VMEM((_W_SLOTS, D_EXPERT, D_MODEL), F32),
                        pltpu.VMEM((D_MODEL, D_EXPERT), BF16), pltpu.VMEM((D_MODEL, D_EXPERT), BF16),
                        pltpu.VMEM((D_EXPERT, D_MODEL), BF16),
                        pltpu.SemaphoreType.DMA((_W_SLOTS, 3, _W_PARTS))])
    return pl.pallas_call(
        functools.partial(_expert_kernel, layer=layer), grid_spec=grid_spec,
        out_shape=jax.ShapeDtypeStruct((MOE_NBLK * MOE_BLK, D_MODEL), BF16),
        compiler_params=_params(("arbitrary",)), name="experts",
    )(blk_e, n_used, slot, nxt, x_pad, w_gate, w_up, w_down)


def _final_kernel(x_ref, ya_ref, yb_ref, rec_ref, gate_ref, fg_ref, o_ref):
    o_ref[...] = _rms(_with_pending(x_ref, (ya_ref, yb_ref, rec_ref, gate_ref)), fg_ref[...])


def _final_norm(x, pend, final_g, row0, n_rows):
    tm = 512
    tile0 = row0 // tm
    return pl.pallas_call(
        _final_kernel, grid=(n_rows // tm,),
        in_specs=[pl.BlockSpec((tm, D_MODEL), lambda i: (i + tile0, 0))] + _pending_specs(tm, tile0)
        + [pl.BlockSpec((1, D_MODEL), lambda i: (0, 0))],
        out_specs=pl.BlockSpec((tm, D_MODEL), lambda i: (i, 0)),
        out_shape=jax.ShapeDtypeStruct((n_rows, D_MODEL), F32),
        compiler_params=_params(("parallel",)), name="final_norm",
    )(x, *pend, final_g.reshape(1, D_MODEL))


_SC_WORKERS = 32
_SC_CORES = 2
_SC_ROWS = 64


def _dispatch_rows(hp, dest):
    n, width = hp.shape
    per_w = n // _SC_WORKERS
    n_ch = per_w // _SC_ROWS
    idx = dest.T.reshape(2, _SC_WORKERS, n_ch, _SC_ROWS)
    mesh = plsc.VectorSubcoreMesh(core_axis_name="c", subcore_axis_name="s")

    @functools.partial(
        pl.kernel, mesh=mesh, out_type=jax.ShapeDtypeStruct((MOE_NBLK * MOE_BLK, width), hp.dtype),
        scratch_types=[pltpu.VMEM((n_ch, _SC_ROWS), jnp.int32), pltpu.VMEM((n_ch, _SC_ROWS), jnp.int32),
                       pltpu.VMEM((_SC_ROWS, width), hp.dtype)], name="dispatch_rows")
    def scatter(h_hbm, idx_hbm, out_hbm, i0_v, i1_v, rows_v):
        wid = lax.axis_index("s") * _SC_CORES + lax.axis_index("c")
        pltpu.sync_copy(idx_hbm.at[0, wid], i0_v)
        pltpu.sync_copy(idx_hbm.at[1, wid], i1_v)

        @pl.loop(0, n_ch)
        def _(g):
            pltpu.sync_copy(h_hbm.at[pl.ds(wid * per_w + g * _SC_ROWS, _SC_ROWS)], rows_v)
            pltpu.sync_copy(rows_v, out_hbm.at[i0_v.at[g]])
            pltpu.sync_copy(rows_v, out_hbm.at[i1_v.at[g]])

    return scatter(hp, idx)


def _moe(h, rec, cnt, mods, w_gate, w_up, w_down, layer):
    dest, blk_e, n_used, slot, nxt = _plan(rec, cnt)
    y_pad = _experts(_dispatch_rows(h, dest), blk_e, n_used, slot, nxt, w_gate, w_up, w_down, layer)
    return y_pad[dest[:, 0]], y_pad[dest[:, 1]], rec, mods


def kernel(x_prompt, x_sample, c, cache_k, cache_v, state_delta, c_ctx, ada_w, ada_b, norm1_g, norm2_g, final_g,
           ev_w_in, ev_w_out, ev_conv_w, ev_a_log, ev_dt_bias, ev_onorm_g, ev_rpb, od_w_in, od_ln_g, od_ln_b,
           od_w_s, od_b_s, od_w_out, moe_w_rg, moe_b_rg, moe_w_re, moe_b_re, moe_w_gate, moe_w_up, moe_w_down):
    x = (x_prompt.reshape(N_CTX, D_MODEL), x_sample.reshape(N_LAT, D_MODEL))
    cond = jnp.concatenate([c_ctx[None, :], c, jnp.zeros((N_COND - 1 - DEC_BATCH, D_MODEL), F32)], axis=0)
    mods_all = _ada_mods(cond, ada_w, ada_b)
    kctx_all = cache_k.reshape(DEC_BATCH, -1, PAST_LEN, NA_HEADS * NA_HD)
    vctx_all = cache_v.reshape(DEC_BATCH, -1, PAST_LEN, NA_HEADS * NA_HD)

    ks, vs, ss = [], [], []
    pend = None
    for l in range(DEPTH):
        mods = mods_all[l]
        router = _router_weights(moe_w_rg[l], moe_b_rg[l], moe_w_re[l], moe_b_re[l])
        if l % 2 == 0:
            e = l // 2
            proj, ab, kv, x = _even_proj(x, pend, mods, norm1_g[l], ev_w_in[e])
            dn = (proj, ab, ev_conv_w[e], ev_a_log[e], ev_dt_bias[e], ev_onorm_g[e])
            oa_ctx, s_fin = _delta_heads(*dn, SEQ, BATCH, 0, None)
            oa_lat, _ = _delta_heads(*dn, DEC_SEQ, DEC_BATCH, N_CTX // DEC_SEQ, state_delta[:, e])
            ob_ctx = _ctx_attention(proj)
            ob_lat = _na_attention(proj, kctx_all[:, e], vctx_all[:, e], ev_rpb[e])
            x, h, rec, cnt = _even_out(oa_ctx, ob_ctx, oa_lat, ob_lat, x, ev_w_out[e], mods, norm2_g[l],
                                       router)
            na_w = NA_HEADS * NA_HD
            ks.append(kv[:N_CTX, :na_w].reshape(BATCH, SEQ, NA_HEADS, NA_HD))
            vs.append(kv[:N_CTX, na_w:].reshape(BATCH, SEQ, NA_HEADS, NA_HD))
            ss.append(s_fin)
        else:
            o = l // 2
            x, h, rec, cnt = _sgu_layer(x, pend, mods, norm1_g[l], od_w_in[o], od_ln_g[o], od_ln_b[o],
                                        od_w_s[o], od_b_s[o], od_w_out[o], norm2_g[l], router)
        pend = _moe(h, rec, cnt, mods, moe_w_gate, moe_w_up, moe_w_down, l)
    y_prompt = _final_norm(x, pend, final_g, 0, N_CTX).reshape(BATCH, SEQ, D_MODEL)
    y_sample = _final_norm(x, pend, final_g, N_CTX, N_LAT).reshape(DEC_BATCH, DEC_SEQ, D_MODEL)
    return (y_prompt, y_sample, jnp.stack(ks, axis=1), jnp.stack(vs, axis=1), jnp.stack(ss, axis=1))
```

```python
import functools

import jax
import jax.numpy as jnp
from jax import lax
from jax.experimental import pallas as pl
from jax.experimental.pallas import tpu as pltpu
from jax.experimental.pallas import tpu_sc as plsc

F32 = jnp.float32
BF16 = jnp.bfloat16

D_MODEL = 1024
BATCH = 16
SEQ = 256
DEPTH = 4
DEC_BATCH = 4
DEC_SEQ = 2048
PAST_LEN = 512
GRID_W = 64
EPS = 1e-6
NEG_INF = -1e30

DN_HEADS = 4
DN_DK = 128
DN_CHUNK = 64
NA_HEADS = 8
NA_HD = 64
NA_ROWS = 8
NA_COLS = 16
SG_CHUNK = 128
SG_GROUPS = 8
SG_W = 2 * D_MODEL
SG_GW = SG_W // SG_GROUPS
N_EGROUPS = 4
EXP_PER_GROUP = 8
N_EXPERTS = 32
D_EXPERT = 512

N_CTX = BATCH * SEQ
N_LAT = DEC_BATCH * DEC_SEQ
N_TOK = N_CTX + N_LAT
N_COND = 8
PROJ_W = 4096
LANES = 128
MOE_BLK = 256
MOE_NBLK = -(-(2 * N_TOK + N_EXPERTS * (MOE_BLK - 1)) // MOE_BLK)
VMEM_LIMIT = 56 * 1024 * 1024

_QA, _KA, _VA, _ZA, _QB, _KB, _VB = 0, 4, 8, 12, 16, 20, 24


def _params(sem):
    return pltpu.CompilerParams(dimension_semantics=sem, vmem_limit_bytes=VMEM_LIMIT)


def _bdot(a, b):
    return jnp.dot(a.astype(BF16), b.astype(BF16), preferred_element_type=F32)


def _bdot_nt(a, b):
    return lax.dot_general(a.astype(BF16), b.astype(BF16), (((1,), (1,)), ((), ())),
                           preferred_element_type=F32)


def _bdot_tn(a, b):
    return lax.dot_general(a.astype(BF16), b.astype(BF16), (((0,), (0,)), ((), ())),
                           preferred_element_type=F32)


def _split2(a):
    p0 = a.astype(BF16)
    return p0, (a - p0.astype(F32)).astype(BF16)


def _dot3(a, b):
    ah = a.astype(BF16)
    al = (a - ah.astype(F32)).astype(BF16)
    bh = b.astype(BF16)
    bl = (b - bh.astype(F32)).astype(BF16)
    return (jnp.dot(ah, bh, preferred_element_type=F32) + jnp.dot(ah, bl, preferred_element_type=F32)
            + jnp.dot(al, bh, preferred_element_type=F32))


def _mask_bf16(m01):
    return jnp.where(m01, 1.0, 0.0).astype(BF16)


def _xdot(m01, a):
    m = _mask_bf16(m01)
    p0, p1 = _split2(a)
    return jnp.dot(m, p0, preferred_element_type=F32) + jnp.dot(m, p1, preferred_element_type=F32)


def _xdot_r(a, m01):
    m = _mask_bf16(m01)
    p0, p1 = _split2(a)
    return jnp.dot(p0, m, preferred_element_type=F32) + jnp.dot(p1, m, preferred_element_type=F32)


def _sigmoid(x):
    return 0.5 * jnp.tanh(0.5 * x) + 0.5


def _silu(x):
    return x * _sigmoid(x)


def _rms(x, g):
    return x * lax.rsqrt(jnp.mean(x * x, axis=-1, keepdims=True) + EPS) * g


def _cond_index(row):
    return jnp.where(row < N_CTX, 0, 1 + (row - N_CTX) // DEC_SEQ)


def _mod_spec(k, tm, tile0=0):
    return pl.BlockSpec((None, None, 1, D_MODEL), lambda i, *_: (_cond_index((i + tile0) * tm), k, 0, 0))


def _ada_kernel(c_ref, w_ref, b_ref, o_ref):
    o_ref[...] = _bdot(_silu(c_ref[...]), w_ref[...]) + b_ref[...]


def _ada_mods(cond, ada_w, ada_b):
    tn = 1536
    out = pl.pallas_call(
        _ada_kernel, grid=(DEPTH, 6 * D_MODEL // tn),
        in_specs=[pl.BlockSpec((N_COND, D_MODEL), lambda l, j: (0, 0)),
                  pl.BlockSpec((None, D_MODEL, tn), lambda l, j: (l, 0, j)),
                  pl.BlockSpec((None, 1, tn), lambda l, j: (l, 0, j))],
        out_specs=pl.BlockSpec((None, N_COND, tn), lambda l, j: (l, 0, j)),
        out_shape=jax.ShapeDtypeStruct((DEPTH, N_COND, 6 * D_MODEL), F32),
        compiler_params=_params(("parallel", "parallel")), name="ada_mods",
    )(cond, ada_w, ada_b.reshape(DEPTH, 1, 6 * D_MODEL))
    return out.reshape(DEPTH, N_COND, 6, 1, D_MODEL)


_EV_TN = 512
_EV_W = 7 * DN_HEADS * LANES
_KV_COL0 = _KB * LANES


def _with_pending(x_ref, pend):
    if not pend:
        return x_ref[...]
    ya_ref, yb_ref, rec_ref, gate_ref = pend
    rec = rec_ref[...]
    y = (rec[:, _R_W:_R_W + 1] * ya_ref[...].astype(F32)
         + rec[:, _R_W + 1:_R_W + 2] * yb_ref[...].astype(F32))
    return x_ref[...] + gate_ref[...] * y


def _pending_specs(tm, tile0=0):
    blk = pl.BlockSpec((tm, D_MODEL), lambda i: (i + tile0, 0))
    return [blk, blk, pl.BlockSpec((tm, LANES), lambda i: (i + tile0, 0)), _mod_spec(5, tm, tile0)]


def _even_proj_kernel(x_ref, *rest, n_pend, ctx_tiles):
    pend, (g_ref, sh_ref, sc_ref, w_ref, wab_ref, o_ref, ab_ref, kv_ref) = rest[:n_pend], rest[n_pend:n_pend + 8]
    if n_pend == 1:
        x = jnp.where(pl.program_id(0) < ctx_tiles, x_ref[...], pend[0][...])
    else:
        x = _with_pending(x_ref, pend)
    if pend:
        rest[n_pend + 8][...] = x
    h = (_rms(x, g_ref[...]) * (1.0 + sc_ref[...]) + sh_ref[...]).astype(BF16)
    ab_ref[...] = jnp.dot(h, wab_ref[...], preferred_element_type=F32)
    for j in range(_EV_W // _EV_TN):
        c0 = j * _EV_TN
        y = jnp.dot(h, w_ref[:, c0:c0 + _EV_TN], preferred_element_type=F32)
        for c in range(_EV_TN // LANES):
            o_ref[c0 // LANES + c] = y[:, c * LANES:(c + 1) * LANES].astype(BF16)
        if c0 >= _KV_COL0:
            kv_ref[:, c0 - _KV_COL0:c0 - _KV_COL0 + _EV_TN] = y


def _even_proj(x, pend, mods, g, w_in):
    tm = 512
    ctx_tiles = N_CTX // tm
    rows = pl.BlockSpec((tm, D_MODEL), lambda i: (i, 0))
    if isinstance(x, tuple):
        x, pend = x[0], (x[1],)
        x_specs = [pl.BlockSpec((tm, D_MODEL), lambda i: (jnp.minimum(i, ctx_tiles - 1), 0)),
                   pl.BlockSpec((tm, D_MODEL), lambda i: (jnp.maximum(i - ctx_tiles, 0), 0))]
    else:
        pend = tuple(pend) if pend else ()
        x_specs = [rows] + (_pending_specs(tm) if pend else [])
    n_ab = 4 * DN_HEADS
    ab0 = 4 * DN_HEADS * DN_DK
    w_main = jnp.concatenate([w_in[:, :ab0], w_in[:, ab0 + n_ab:]], axis=1).astype(BF16)
    w_ab = jnp.concatenate([w_in[:, ab0:ab0 + n_ab], jnp.zeros((D_MODEL, LANES - n_ab), F32)],
                           axis=1).astype(BF16)
    held = lambda shape: pl.BlockSpec(shape, lambda i: (0,) * len(shape), pipeline_mode=pl.Buffered(1))
    out_specs = [pl.BlockSpec((_EV_W // LANES, tm, LANES), lambda i: (0, i, 0)),
                 pl.BlockSpec((tm, LANES), lambda i: (i, 0)),
                 pl.BlockSpec((tm, 2 * NA_HEADS * NA_HD), lambda i: (i, 0))]
    out_shape = [jax.ShapeDtypeStruct((_EV_W // LANES, N_TOK, LANES), BF16),
                 jax.ShapeDtypeStruct((N_TOK, LANES), F32),
                 jax.ShapeDtypeStruct((N_TOK, 2 * NA_HEADS * NA_HD), F32)]
    if pend:
        out_specs.append(rows)
        out_shape.append(jax.ShapeDtypeStruct((N_TOK, D_MODEL), F32))
    res = pl.pallas_call(
        functools.partial(_even_proj_kernel, n_pend=len(pend), ctx_tiles=ctx_tiles), grid=(N_TOK // tm,),
        in_specs=x_specs
        + [pl.BlockSpec((1, D_MODEL), lambda i: (0, 0)), _mod_spec(0, tm), _mod_spec(1, tm),
           held((D_MODEL, _EV_W)), held((D_MODEL, LANES))],
        out_specs=out_specs, out_shape=out_shape,
        compiler_params=_params(("parallel",)), name="even_proj",
    )(x, *pend, g.reshape(1, D_MODEL), mods, mods, w_main, w_ab)
    return (*res[:3], res[3] if pend else x)


_CHUNK_SHIFT = DN_CHUNK.bit_length() - 1
_CUM_ROWS = 256
_DN_CHAINS = 16
_DN_SHORT = 256
_SERIES_FINE = 3
_MQ_ROWS = DN_DK + DN_CHUNK


def _dn_kernel(*refs, T, HB, has_s0, want_state):
    it = iter(refs)
    q_ref, k_ref, v_ref, z_ref, ab_ref = (next(it) for _ in range(5))
    cwq_ref, cwk_ref, cwv_ref, alog_ref, dtb_ref, og_ref = (next(it) for _ in range(6))
    s0_ref = next(it) if has_s0 else None
    o_ref = next(it)
    sfin_ref = next(it) if want_state else None
    qc, kc, vc, gsc, bsc, osc, b_s, mq_s = (next(it) for _ in range(8))

    C = DN_CHUNK
    n = T // C
    h0 = pl.program_id(1) * HB

    row = lax.broadcasted_iota(jnp.int32, (T, 1), 0)

    def conv(x_ref, cw_ref, hh):
        x = x_ref[hh].astype(F32)
        cw = cw_ref[:, hh * LANES:(hh + 1) * LANES]
        xp = jnp.where(row == 0, 0.0, pltpu.roll(x, 1, 0))
        xn = jnp.where(row == T - 1, 0.0, pltpu.roll(x, T - 1, 0))
        return _silu(cw[0:1, :] * xp + cw[1:2, :] * x + cw[2:3, :] * xn)

    def l2n(x):
        return x * lax.rsqrt(jnp.sum(x * x, axis=-1, keepdims=True) + EPS)

    ab = ab_ref[...]
    lane = lax.broadcasted_iota(jnp.int32, (1, LANES), 1)
    dtb = jnp.zeros((1, LANES), F32)
    alog = jnp.zeros((1, LANES), F32)
    for d in range(2):
        for hq in range(DN_HEADS):
            dtb = jnp.where(lane == d * DN_HEADS + hq, dtb_ref[d, hq], dtb)
            alog = jnp.where(lane == d * DN_HEADS + hq, alog_ref[d, hq], alog)
    xs = ab + dtb
    g_all = -jnp.exp(alog) * (jnp.maximum(xs, 0.0) + jnp.log1p(jnp.exp(-jnp.abs(xs))))
    beta_all = _sigmoid(ab)

    sel_r = lax.broadcasted_iota(jnp.int32, (LANES, LANES), 0)
    for hh in range(HB):
        qc[hh] = l2n(conv(q_ref, cwq_ref, hh)) * (DN_DK ** -0.5)
        kc[hh] = l2n(conv(k_ref, cwk_ref, hh))
        vc[hh] = conv(v_ref, cwv_ref, hh)
        hd = h0 + hh
        for d in range(2):
            gsc[hh, d] = _xdot_r(g_all, sel_r == d * DN_HEADS + hd)
            bsc[hh, d] = _xdot_r(beta_all, sel_r == 2 * DN_HEADS + d * DN_HEADS + hd)

    pr = lax.broadcasted_iota(jnp.int32, (_CUM_ROWS, _CUM_ROWS), 0)
    pc = lax.broadcasted_iota(jnp.int32, (_CUM_ROWS, _CUM_ROWS), 1)
    same = lax.shift_right_logical(pr, _CHUNK_SHIFT) == lax.shift_right_logical(pc, _CHUNK_SHIFT)
    cum_mask = (jnp.logical_and(same, pc <= pr), jnp.logical_and(same, pc >= pr))

    def cum_body(i, carry):
        sl = pl.ds(pl.multiple_of(i * _CUM_ROWS, _CUM_ROWS), _CUM_ROWS)
        for hh in range(HB):
            for d in range(2):
                gsc[hh, d, sl, :] = _xdot(cum_mask[d], gsc[hh, d, sl, :])
        return carry

    lax.fori_loop(0, T // _CUM_ROWS, cum_body, 0)

    ri = lax.broadcasted_iota(jnp.int32, (C, C), 0)
    ci = lax.broadcasted_iota(jnp.int32, (C, C), 1)
    eye = (ri == ci).astype(F32)

    def prepare(items):
        lows, decays = [], []
        kk, qk = {}, {}
        for hh, d, c, slot in items:
            sl = pl.ds(pl.multiple_of(c * C, C), C)
            gc = gsc[hh, d, sl, :]
            if (hh, slot) not in kk:
                k = kc[hh, sl, :]
                kk[hh, slot] = _bdot_nt(k, k)
                qk[hh, slot] = _bdot_nt(qc[hh, sl, :], k)
            incl = (ci <= ri) if d == 0 else (ci >= ri)
            strict = (ci < ri) if d == 0 else (ci > ri)
            gr = jnp.transpose(gc)[0:1, :C]
            decay = jnp.where(incl, jnp.exp(jnp.where(incl, gc[:, :C] - gr, 0.0)), 0.0)
            lows.append(jnp.where(strict, bsc[hh, d, sl, :C] * kk[hh, slot] * decay, 0.0))
            decays.append(decay)
        ts = [eye - low for low in lows]
        ps = lows
        for step in range(5):
            dot = _dot3 if step < _SERIES_FINE else _bdot
            ps = [dot(p, p) for p in ps]
            ts = [t + dot(t, p) for t, p in zip(ts, ps)]
        for (hh, d, c, slot), t, decay in zip(items, ts, decays):
            sl = pl.ds(pl.multiple_of(c * C, C), C)
            q, k, gc, beta = qc[hh, sl, :], kc[hh, sl, :], gsc[hh, d, sl, :], bsc[hh, d, sl, :]
            eg = jnp.exp(gc)
            uw = _bdot(t, jnp.concatenate([vc[hh, sl, :] * beta, k * beta * eg], axis=-1))
            last = gc[C - 1:C, :] if d == 0 else gc[0:1, :]
            wu = jnp.concatenate([uw[:, LANES:], uw[:, :LANES]], axis=-1).astype(BF16)
            kd = (k * jnp.exp(last - gc)).astype(BF16)
            attn = (qk[hh, slot] * decay).astype(BF16)
            kdwu = lax.dot_general(kd, wu, (((0,), (0,)), ((), ())), preferred_element_type=F32)
            awu = jnp.dot(attn, wu, preferred_element_type=F32)
            mq0 = pl.multiple_of(c * _MQ_ROWS, _MQ_ROWS)
            mq_s[hh, d, pl.ds(mq0, DN_DK), :] = kdwu[:, :LANES].astype(BF16)
            mq_s[hh, d, pl.ds(mq0 + DN_DK, C), :] = (q * eg - awu[:, :LANES]).astype(BF16)
            b_s[hh, d, pl.ds(pl.multiple_of(c * DN_DK, DN_DK), DN_DK), :] = kdwu[:, LANES:]
            osc[hh, d, sl, :] = awu[:, LANES:]

    n_prep = min(n, _DN_CHAINS // 2)
    h_prep = max(1, min(HB, _DN_CHAINS // (2 * n_prep)))

    def prep_body(i, carry):
        for hg in range(0, HB, h_prep):
            prepare([(hh, d, i * n_prep + j, j) for hh in range(hg, hg + h_prep) for j in range(n_prep)
                     for d in range(2)])
        return carry

    lax.fori_loop(0, n // n_prep, prep_body, 0)

    def advance(hh, d, c, S):
        sl = pl.ds(pl.multiple_of(c * C, C), C)
        ms = jnp.dot(mq_s[hh, d, pl.ds(pl.multiple_of(c * _MQ_ROWS, _MQ_ROWS), _MQ_ROWS), :], S.astype(BF16),
                     preferred_element_type=F32)
        osc[hh, d, sl, :] = osc[hh, d, sl, :] + ms[DN_DK:]
        last = gsc[hh, d, pl.ds(c * C + (C - 1 if d == 0 else 0), 1), :]
        return (S * jnp.exp(last) - ms[:DN_DK]
                + b_s[hh, d, pl.ds(pl.multiple_of(c * DN_DK, DN_DK), DN_DK), :])

    def body(i, carry):
        return tuple(advance(hh, d, i if d == 0 else n - 1 - i, carry[2 * hh + d])
                     for hh in range(HB) for d in range(2))

    if has_s0:
        init = tuple(s0_ref[d, hh] for hh in range(HB) for d in range(2))
    else:
        init = tuple(jnp.zeros((DN_DK, LANES), F32) for _ in range(2 * HB))
    fin = lax.fori_loop(0, n, body, init)
    for hh in range(HB):
        if want_state:
            sfin_ref[0, hh] = fin[2 * hh]
            sfin_ref[1, hh] = fin[2 * hh + 1]
        o = osc[hh, 0] + osc[hh, 1]
        o_ref[hh] = (_rms(o, og_ref[...]) * _silu(z_ref[hh].astype(F32))).astype(o_ref.dtype)


def _delta_heads(proj, ab, conv_w, a_log, dt_bias, onorm_g, T, n_seq, row0, s0):
    has_s0 = s0 is not None
    want_state = not has_s0
    hb = DN_HEADS if T <= _DN_SHORT else 2

    def col(cb):
        return pl.BlockSpec((hb, T, LANES), lambda s, h: (cb // hb + h, row0 + s, 0))

    def cw(cb):
        return pl.BlockSpec((3, hb * LANES), lambda s, h: (0, cb // hb + h))

    smem = pl.BlockSpec(memory_space=pltpu.SMEM)
    in_specs = [col(_QA), col(_KA), col(_VA), col(_ZA),
                pl.BlockSpec((T, LANES), lambda s, h: (row0 + s, 0)),
                cw(0), cw(4), cw(8), smem, smem,
                pl.BlockSpec((1, LANES), lambda s, h: (0, 0))]
    args = [proj, proj, proj, proj, ab, conv_w, conv_w, conv_w, a_log, dt_bias,
            onorm_g.reshape(1, LANES)]
    state_spec = pl.BlockSpec((None, 2, hb, DN_DK, LANES), lambda s, h: (s, 0, h, 0, 0))
    if has_s0:
        in_specs.append(state_spec)
        args.append(s0)
    out_shape = [jax.ShapeDtypeStruct((DN_HEADS, n_seq * T, LANES), BF16)]
    out_specs = [pl.BlockSpec((hb, T, LANES), lambda s, h: (h, s, 0))]
    if want_state:
        out_shape.append(jax.ShapeDtypeStruct((n_seq, 2, DN_HEADS, DN_DK, LANES), F32))
        out_specs.append(state_spec)
    res = pl.pallas_call(
        functools.partial(_dn_kernel, T=T, HB=hb, has_s0=has_s0, want_state=want_state),
        grid=(n_seq, DN_HEADS // hb), in_specs=in_specs, out_specs=out_specs, out_shape=out_shape,
        scratch_shapes=[pltpu.VMEM((hb, T, LANES), F32)] * 3
        + [pltpu.VMEM((hb, 2, T, LANES), F32)] * 3
        + [pltpu.VMEM((hb, 2, T // DN_CHUNK * DN_DK, LANES), F32),
           pltpu.VMEM((hb, 2, T // DN_CHUNK * _MQ_ROWS, LANES), BF16)],
        compiler_params=_params(("parallel", "parallel")), name="delta_heads_%d" % T,
    )(*args)
    return res if want_state else (res[0], None)


def _pair_queries(q, first):
    return jnp.concatenate([jnp.where(first, q, 0.0), jnp.where(first, 0.0, q)], axis=0).astype(BF16)


def _ctx_attn_kernel(q_ref, k_ref, v_ref, o_ref):
    first = lax.broadcasted_iota(jnp.int32, (SEQ, LANES), 1) < NA_HD
    qm = _pair_queries(q_ref[...] * (NA_HD ** -0.5), first)
    s = lax.dot_general(k_ref[...], qm, (((1,), (1,)), ((), ())), preferred_element_type=F32)
    e = jnp.exp(s - jnp.max(s, axis=0, keepdims=True))
    den = jnp.sum(e, axis=0, keepdims=True)
    o = lax.dot_general(e.astype(BF16), v_ref[...], (((0,), (0,)), ((), ())), preferred_element_type=F32)
    o = jnp.where(first, o[:SEQ], o[SEQ:])
    den_t = jnp.transpose(jnp.broadcast_to(den, (LANES, 2 * SEQ)))
    o_ref[...] = (o / jnp.where(first, den_t[:SEQ], den_t[SEQ:])).astype(o_ref.dtype)


def _ctx_attention(proj):
    def col(cb):
        return pl.BlockSpec((None, SEQ, LANES), lambda s, p: (cb + p, s, 0))

    return pl.pallas_call(
        _ctx_attn_kernel, grid=(BATCH, NA_HEADS // 2),
        in_specs=[col(_QB), col(_KB), col(_VB)],
        out_specs=pl.BlockSpec((None, SEQ, LANES), lambda s, p: (p, s, 0)),
        out_shape=jax.ShapeDtypeStruct((NA_HEADS // 2, N_CTX, LANES), BF16),
        compiler_params=_params(("parallel", "parallel")), name="ctx_attention",
    )(proj, proj, proj)


_NA_UNROLL = 4


def _na_kernel(q_ref, k_ref, v_ref, kc_ref, vc_ref, bias_ref, o_ref, kcb_scr, vcb_scr):
    rows = DEC_SEQ // GRID_W
    win = NA_ROWS * GRID_W
    scale = NA_HD ** -0.5
    dn_nt = (((1,), (1,)), ((), ()))
    dn_tn = (((0,), (0,)), ((), ()))

    kcb_scr[...] = kc_ref[...].astype(BF16)
    vcb_scr[...] = vc_ref[...].astype(BF16)
    first = lax.broadcasted_iota(jnp.int32, (GRID_W, LANES), 1) < NA_HD

    def body(it, carry):
        rr = [it * _NA_UNROLL + j for j in range(_NA_UNROLL)]
        rss = [jnp.clip(r - NA_ROWS // 2, 0, rows - NA_ROWS) for r in rr]
        qsls = [pl.ds(pl.multiple_of(r * GRID_W, GRID_W), GRID_W) for r in rr]
        wsls = [pl.ds(pl.multiple_of(rs * GRID_W, GRID_W), win) for rs in rss]
        qms, s_wins, s_ctxs = [], [], []
        for r, rs, qsl, wsl in zip(rr, rss, qsls, wsls):
            qm = _pair_queries(q_ref[qsl, :] * scale, first)
            bias = jnp.concatenate([bias_ref[NA_ROWS - 1 - (r - rs) + i] for i in range(NA_ROWS)], axis=0)
            s_wins.append(lax.dot_general(k_ref[wsl, :], qm, dn_nt, preferred_element_type=F32) + bias)
            s_ctxs.append(lax.dot_general(kcb_scr[...], qm, dn_nt, preferred_element_type=F32))
        ms = [jnp.maximum(jnp.max(sw, axis=0, keepdims=True), jnp.max(sc, axis=0, keepdims=True))
              for sw, sc in zip(s_wins, s_ctxs)]
        e_wins = [jnp.exp(sw - m) for sw, m in zip(s_wins, ms)]
        e_ctxs = [jnp.exp(sc - m) for sc, m in zip(s_ctxs, ms)]
        dens = [jnp.sum(ew, axis=0, keepdims=True) + jnp.sum(ec, axis=0, keepdims=True)
                for ew, ec in zip(e_wins, e_ctxs)]
        for qsl, wsl, ew, ec, den in zip(qsls, wsls, e_wins, e_ctxs, dens):
            o = (lax.dot_general(ew.astype(BF16), v_ref[wsl, :], dn_tn, preferred_element_type=F32)
                 + lax.dot_general(ec.astype(BF16), vcb_scr[...], dn_tn, preferred_element_type=F32))
            o = o / jnp.transpose(jnp.broadcast_to(den, (LANES, LANES)))
            o_ref[qsl, :] = jnp.where(first, o[:GRID_W], o[GRID_W:]).astype(o_ref.dtype)
        return carry

    lax.fori_loop(0, rows // _NA_UNROLL, body, 0)


def _na_bias_table(rpb):
    col = jnp.arange(GRID_W)
    cs = jnp.clip(col - NA_COLS // 2, 0, GRID_W - NA_COLS)
    col_ok = (col[None, :] >= cs[:, None]) & (col[None, :] < cs[:, None] + NA_COLS)
    dc = jnp.clip(col[None, :] - col[:, None] + NA_COLS - 1, 0, 2 * NA_COLS - 2)
    onehot = (dc.T[None, :, :] == jnp.arange(2 * NA_COLS - 1)[:, None, None]).astype(F32)
    t = jnp.einsum('hrd,dkq->hrkq', rpb.astype(F32), onehot, precision=lax.Precision.HIGHEST)
    t = jnp.where(col_ok.T[None, None], t, NEG_INF)
    t = t.reshape(NA_HEADS // 2, 2, 2 * NA_ROWS - 1, GRID_W, GRID_W)
    return jnp.concatenate([t[:, 0], t[:, 1]], axis=-1)


def _na_attention(proj, kctx, vctx, rpb):
    blk = N_CTX // DEC_SEQ

    def col(cb):
        return pl.BlockSpec((None, DEC_SEQ, LANES), lambda b, p: (cb + p, blk + b, 0))

    ctx = pl.BlockSpec((None, PAST_LEN, LANES), lambda b, p: (b, 0, p))
    return pl.pallas_call(
        _na_kernel, grid=(DEC_BATCH, NA_HEADS // 2),
        in_specs=[col(_QB), col(_KB), col(_VB), ctx, ctx,
                  pl.BlockSpec((None, 2 * NA_ROWS - 1, GRID_W, 2 * GRID_W), lambda b, p: (p, 0, 0, 0))],
        out_specs=pl.BlockSpec((None, DEC_SEQ, LANES), lambda b, p: (p, b, 0)),
        out_shape=jax.ShapeDtypeStruct((NA_HEADS // 2, N_LAT, LANES), BF16),
        scratch_shapes=[pltpu.VMEM((PAST_LEN, LANES), BF16), pltpu.VMEM((PAST_LEN, LANES), BF16)],
        compiler_params=_params(("parallel", "parallel")), name="na_attention",
    )(proj, proj, proj, kctx, vctx, _na_bias_table(rpb))


_LOGIT0 = N_EGROUPS
_R_E, _R_W, _R_RANK = 0, 2, 4


def _lane_min_where(mask, lane):
    return jnp.min(jnp.where(mask, lane, LANES), axis=-1, keepdims=True)


def _route_rows(lg, carry_ref, tri_ref):
    big = -3.0e38
    lane = lax.broadcasted_iota(jnp.int32, lg.shape, 1)
    is_g = lane < N_EGROUPS
    gmax = jnp.max(jnp.where(is_g, lg, big), axis=-1, keepdims=True)
    gsum = jnp.sum(jnp.where(is_g, jnp.exp(jnp.where(is_g, lg - gmax, 0.0)), 0.0), axis=-1, keepdims=True)
    pg_top = 1.0 / gsum
    g_idx = _lane_min_where(jnp.logical_and(is_g, lg == gmax), lane)
    in_g = jnp.logical_and(lane >= _LOGIT0, lax.shift_right_arithmetic(lane - _LOGIT0, 3) == g_idx)
    in_g = jnp.logical_and(in_g, lane < _LOGIT0 + N_EXPERTS)
    m1 = jnp.max(jnp.where(in_g, lg, big), axis=-1, keepdims=True)
    i1 = _lane_min_where(jnp.logical_and(in_g, lg == m1), lane)
    rest = jnp.logical_and(in_g, lane != i1)
    m2 = jnp.max(jnp.where(rest, lg, big), axis=-1, keepdims=True)
    i2 = _lane_min_where(jnp.logical_and(rest, lg == m2), lane)
    e2 = jnp.exp(m2 - m1)
    w1 = pg_top * (1.0 / (1.0 + e2))
    w2 = pg_top * (e2 / (1.0 + e2))
    hit1 = lane == i1
    hit2 = lane == i2
    picked = jnp.where(jnp.logical_or(hit1, hit2), 1.0, 0.0)
    before = jnp.dot(tri_ref[...], picked.astype(BF16), preferred_element_type=F32) + carry_ref[...]
    r1 = jnp.sum(jnp.where(hit1, before, 0.0), axis=-1, keepdims=True)
    r2 = jnp.sum(jnp.where(hit2, before, 0.0), axis=-1, keepdims=True)
    carry_ref[...] = carry_ref[...] + jnp.sum(picked, axis=0, keepdims=True)
    rec = jnp.zeros(lg.shape, F32)
    for ln, val in ((_R_E, (i1 - _LOGIT0).astype(F32)), (_R_E + 1, (i2 - _LOGIT0).astype(F32)),
                    (_R_W, w1), (_R_W + 1, w2), (_R_RANK, r1), (_R_RANK + 1, r2)):
        rec = jnp.where(lane == ln, val, rec)
    return rec


_PACK_W = D_MODEL // 2


def _pack_rows(hb):
    lo = lax.bitcast_convert_type(hb[:, :_PACK_W].astype(F32), jnp.int32)
    hi = lax.bitcast_convert_type(hb[:, _PACK_W:].astype(F32), jnp.int32)
    return jnp.bitwise_or(jnp.bitwise_and(hi, -65536), lax.shift_right_logical(lo, 16))


def _unpack_rows(w):
    lo = lax.bitcast_convert_type(lax.shift_left(w, 16), F32)
    hi = lax.bitcast_convert_type(jnp.bitwise_and(w, -65536), F32)
    return jnp.concatenate([lo, hi], axis=-1).astype(BF16)


def _moe_input(xnew, first, tail_in, tail_out, tail_scr):
    g2_ref, sc2_ref, sh2_ref, wr_ref, br_ref = tail_in
    x_out, h_out, rec_out, cnt_out = tail_out
    tri_scr, carry_scr = tail_scr

    @pl.when(first)
    def _():
        tm = tri_scr.shape[0]
        r = lax.broadcasted_iota(jnp.int32, (tm, tm), 0)
        c = lax.broadcasted_iota(jnp.int32, (tm, tm), 1)
        tri_scr[...] = jnp.where(c < r, 1.0, 0.0).astype(BF16)
        carry_scr[...] = jnp.zeros(carry_scr.shape, F32)

    x_out[...] = xnew
    h = _rms(xnew, g2_ref[...]) * (1.0 + sc2_ref[...]) + sh2_ref[...]
    hh = h.astype(BF16)
    h_out[...] = _pack_rows(hh)
    lg = jnp.dot(hh, wr_ref[...], preferred_element_type=F32) + br_ref[...]
    rec_out[...] = _route_rows(lg, carry_scr, tri_scr)
    cnt_out[...] = carry_scr[...]


def _even_out_kernel(oac_ref, obc_ref, oal_ref, obl_ref, x_ref, w_ref, gate_ref, *rest, ctx_tiles):
    tail_in, tail_out, (w_scr,), tail_scr = rest[:5], rest[5:9], rest[9:10], rest[10:]
    first = pl.program_id(0) == 0

    @pl.when(first)
    def _():
        w_scr[...] = w_ref[...].astype(BF16)

    is_ctx = pl.program_id(0) < ctx_tiles
    parts = [jnp.where(is_ctx, c_ref[hb], l_ref[hb])
             for c_ref, l_ref in ((oac_ref, oal_ref), (obc_ref, obl_ref)) for hb in range(DN_HEADS)]
    mix = jnp.concatenate(parts, axis=-1)
    out = jnp.dot(mix, w_scr[...], preferred_element_type=F32)
    _moe_input(x_ref[...] + gate_ref[...] * out, first, tail_in, tail_out, tail_scr)


def _tail_specs(tm):
    const = lambda shape: pl.BlockSpec(shape, lambda i: (0,) * len(shape))
    in_specs = [_mod_spec(2, tm), const((1, D_MODEL)), _mod_spec(4, tm), _mod_spec(3, tm),
                const((D_MODEL, LANES)), const((1, LANES))]
    out_specs = [pl.BlockSpec((tm, D_MODEL), lambda i: (i, 0)),
                 pl.BlockSpec((tm, _PACK_W), lambda i: (i, 0)),
                 pl.BlockSpec((tm, LANES), lambda i: (i, 0)),
                 const((1, LANES))]
    out_shape = [jax.ShapeDtypeStruct((N_TOK, D_MODEL), F32),
                 jax.ShapeDtypeStruct((N_TOK, _PACK_W), jnp.int32),
                 jax.ShapeDtypeStruct((N_TOK, LANES), F32),
                 jax.ShapeDtypeStruct((1, LANES), F32)]
    scratch = [pltpu.VMEM((tm, tm), BF16), pltpu.VMEM((1, LANES), F32)]
    return in_specs, out_specs, out_shape, scratch


def _router_weights(w_rg, b_rg, w_re, b_re):
    pad = LANES - N_EGROUPS - N_EXPERTS
    w = jnp.concatenate([w_rg, w_re, jnp.zeros((D_MODEL, pad), F32)], axis=1)
    b = jnp.concatenate([b_rg, b_re, jnp.zeros((pad,), F32)]).reshape(1, LANES)
    return w.astype(BF16), b


def _even_out(oa_ctx, ob_ctx, oa_lat, ob_lat, x, w_out, mods, g2, router):
    tm = 512
    ctx_tiles = N_CTX // tm
    tail_in, out_specs, out_shape, tail_scr = _tail_specs(tm)
    ctxblk = pl.BlockSpec((DN_HEADS, tm, LANES), lambda i: (0, jnp.minimum(i, ctx_tiles - 1), 0))
    latblk = pl.BlockSpec((DN_HEADS, tm, LANES), lambda i: (0, jnp.maximum(i - ctx_tiles, 0), 0))
    return pl.pallas_call(
        functools.partial(_even_out_kernel, ctx_tiles=ctx_tiles), grid=(N_TOK // tm,),
        in_specs=[ctxblk, ctxblk, latblk, latblk, pl.BlockSpec((tm, D_MODEL), lambda i: (i, 0)),
                  pl.BlockSpec((D_MODEL, D_MODEL), lambda i: (0, 0))] + tail_in,
        out_specs=out_specs, out_shape=out_shape,
        scratch_shapes=[pltpu.VMEM((D_MODEL, D_MODEL), BF16)] + tail_scr,
        compiler_params=_params(("arbitrary",)), name="even_out",
    )(oa_ctx, ob_ctx, oa_lat, ob_lat, x, w_out, mods, g2.reshape(1, D_MODEL), mods, mods, *router)


def _gelu_tanh(x):
    return x * (0.5 * (1.0 + jnp.tanh(0.7978845608028654 * (x + 0.044715 * (x * x * x)))))


def _sgu_kernel(x_ref, *rest, tm, n_pend):
    pend, rest = rest[:n_pend], rest[n_pend:]
    g1_ref, sh1_ref, sc1_ref, win_ref, lng_ref, lnb_ref, ws_ref, bst_ref, wout_ref, gate_ref = rest[:10]
    rest = rest[10:]
    tail_in, tail_out, (v_scr, m_scr), tail_scr = rest[:5], rest[5:9], rest[9:11], rest[11:]
    first = pl.program_id(0) == 0
    x = _with_pending(x_ref, pend)
    h = (_rms(x, g1_ref[...]) * (1.0 + sc1_ref[...]) + sh1_ref[...]).astype(BF16)

    v = _gelu_tanh(jnp.dot(h, win_ref[:, SG_W:], preferred_element_type=F32))
    mu = jnp.mean(v, axis=-1, keepdims=True)
    vc = v - mu
    var = jnp.mean(vc * vc, axis=-1, keepdims=True)
    v_scr[...] = (vc * lax.rsqrt(var + EPS) * lng_ref[...] + lnb_ref[...]).astype(BF16)

    for g in range(SG_GROUPS):
        cs = slice(g * SG_GW, (g + 1) * SG_GW)
        u = _gelu_tanh(jnp.dot(h, win_ref[:, cs], preferred_element_type=F32))
        w_sp = ws_ref[g].astype(BF16)
        for c in range(tm // SG_CHUNK):
            rs = slice(c * SG_CHUNK, (c + 1) * SG_CHUNK)
            sp = jnp.dot(w_sp, v_scr[rs, cs], preferred_element_type=F32) + bst_ref[:, g:g + 1]
            m_scr[rs, cs] = (u[rs] * sp).astype(BF16)
    out = jnp.dot(m_scr[...], wout_ref[...], preferred_element_type=F32)
    _moe_input(x + gate_ref[...] * out, first, tail_in, tail_out, tail_scr)


def _sgu_layer(x, pend, mods, g1, w_in, ln_g, ln_b, w_s, b_s, w_out, g2, router):
    tm = 512
    pend = tuple(pend) if pend else ()
    tail_in, out_specs, out_shape, tail_scr = _tail_specs(tm)
    const = lambda shape: pl.BlockSpec(shape, lambda i: (0,) * len(shape))
    held = lambda shape: pl.BlockSpec(shape, lambda i: (0,) * len(shape), pipeline_mode=pl.Buffered(1))
    return pl.pallas_call(
        functools.partial(_sgu_kernel, tm=tm, n_pend=len(pend)), grid=(N_TOK // tm,),
        in_specs=[pl.BlockSpec((tm, D_MODEL), lambda i: (i, 0))] + (_pending_specs(tm) if pend else [])
        + [const((1, D_MODEL)), _mod_spec(0, tm), _mod_spec(1, tm),
                  held((D_MODEL, 2 * SG_W)), const((1, SG_W)), const((1, SG_W)),
                  const((SG_GROUPS, SG_CHUNK, SG_CHUNK)), const((SG_CHUNK, SG_GROUPS)),
                  held((SG_W, D_MODEL))] + tail_in,
        out_specs=out_specs, out_shape=out_shape,
        scratch_shapes=[pltpu.VMEM((tm, SG_W), BF16), pltpu.VMEM((tm, SG_W), BF16)] + tail_scr,
        compiler_params=_params(("arbitrary",)), name="sgu_layer",
    )(x, *pend, g1.reshape(1, D_MODEL), mods, mods, w_in.astype(BF16), ln_g.reshape(1, SG_W), ln_b.reshape(1, SG_W),
      w_s, b_s.T, w_out.astype(BF16), mods, g2.reshape(1, D_MODEL), mods, mods, *router)


def _plan(rec, cnt):
    e_idx = rec[:, _R_E:_R_E + 2].astype(jnp.int32)
    rank = rec[:, _R_RANK:_R_RANK + 2].astype(jnp.int32)
    counts = cnt[0, _LOGIT0:_LOGIT0 + N_EXPERTS].astype(jnp.int32)
    padded = (counts + MOE_BLK - 1) // MOE_BLK * MOE_BLK
    pad_end = jnp.cumsum(padded)
    pad_start = pad_end - padded
    hit = e_idx[:, :, None] == jnp.arange(N_EXPERTS, dtype=jnp.int32)[None, None, :]
    dest = jnp.sum(jnp.where(hit, pad_start[None, None, :], 0), axis=-1) + rank
    blk0 = jnp.arange(MOE_NBLK, dtype=jnp.int32) * MOE_BLK
    blk_e = jnp.minimum(jnp.sum((pad_end[None, :] <= blk0[:, None]).astype(jnp.int32), axis=-1),
                        N_EXPERTS - 1)
    n_used = (pad_end[-1] // MOE_BLK).astype(jnp.int32).reshape(1)
    owns = counts > 0
    slot_of = (jnp.cumsum(owns.astype(jnp.int32)) - 1) % _W_SLOTS
    ids = jnp.arange(N_EXPERTS, dtype=jnp.int32)
    later = jnp.logical_and(owns[None, :], ids[None, :] > ids[:, None])
    next_of = jnp.min(jnp.where(later, ids[None, :], N_EXPERTS), axis=-1)
    next2_of = jnp.concatenate([next_of, jnp.full((1,), N_EXPERTS, jnp.int32)])[next_of]
    ahead = jnp.stack([next_of, next2_of], axis=0)
    ahead = jnp.where(ahead == N_EXPERTS, -1, ahead)
    return dest, blk_e, n_used, slot_of[blk_e], ahead[:, blk_e].reshape(-1)


_W_PARTS = 4
_W_SLOTS = 3


def _expert_kernel(blk_e_ref, n_used_ref, slot_ref, next_ref, x_ref, wg_hbm, wu_hbm, wd_hbm, o_ref,
                   wg_buf, wu_buf, wd_buf, wg_scr, wu_scr, wd_scr, sems, *, layer):
    j = pl.program_id(0)
    e = blk_e_ref[j]
    slot = slot_ref[j]
    fresh = jnp.logical_or(j == 0, e != blk_e_ref[jnp.maximum(j - 1, 0)])
    live = j < n_used_ref[0]

    def copies(expert, s):
        out = []
        for m, (hbm, buf) in enumerate(((wg_hbm, wg_buf), (wu_hbm, wu_buf), (wd_hbm, wd_buf))):
            rows = buf.shape[1] // _W_PARTS
            for part in range(_W_PARTS):
                band = pl.ds(part * rows, rows)
                out.append(pltpu.make_async_copy(hbm.at[layer, expert, band], buf.at[s, band],
                                                 sems.at[s, m, part]))
        return out

    def start_if_any(expert, s):
        @pl.when(expert >= 0)
        def _():
            for cp in copies(expert, s):
                cp.start()

    @pl.when(j == 0)
    def _():
        for cp in copies(e, slot):
            cp.start()
        start_if_any(next_ref[j], lax.rem(slot + 1, _W_SLOTS))

    @pl.when(jnp.logical_and(fresh, live))
    def _():
        for cp in copies(e, slot):
            cp.wait()
        start_if_any(next_ref[MOE_NBLK + j], lax.rem(slot + 2, _W_SLOTS))

        wg_scr[...] = wg_buf[slot].astype(BF16)
        wu_scr[...] = wu_buf[slot].astype(BF16)
        wd_scr[...] = wd_buf[slot].astype(BF16)

    @pl.when(live)
    def _():
        x = _unpack_rows(x_ref[...])
        gt = jnp.dot(x, wg_scr[...], preferred_element_type=F32)
        up = jnp.dot(x, wu_scr[...], preferred_element_type=F32)
        hb = (_silu(gt) * up).astype(BF16)
        o_ref[...] = jnp.dot(hb, wd_scr[...], preferred_element_type=F32).astype(o_ref.dtype)

    @pl.when(jnp.logical_not(live))
    def _():
        o_ref[...] = jnp.zeros(o_ref.shape, o_ref.dtype)


def _experts(x_pad, blk_e, n_used, slot, nxt, w_gate, w_up, w_down, layer):
    hbm = pl.BlockSpec(memory_space=pl.ANY)
    grid_spec = pltpu.PrefetchScalarGridSpec(
        num_scalar_prefetch=4, grid=(MOE_NBLK,),
        in_specs=[pl.BlockSpec((MOE_BLK, _PACK_W), lambda j, *_: (j, 0)), hbm, hbm, hbm],
        out_specs=pl.BlockSpec((MOE_BLK, D_MODEL), lambda j, *_: (j, 0)),
        scratch_shapes=[pltpu.VMEM((_W_SLOTS, D_MODEL, D_EXPERT), F32),
                        pltpu.VMEM((_W_SLOTS, D_MODEL, D_EXPERT), F32),
                        pltpu.VMEM((_W_SLOTS, D_EXPERT, D_MODEL), F32),
                        pltpu.VMEM((D_MODEL, D_EXPERT), BF16), pltpu.VMEM((D_MODEL, D_EXPERT), BF16),
                        pltpu.VMEM((D_EXPERT, D_MODEL), BF16),
                        pltpu.SemaphoreType.DMA((_W_SLOTS, 3, _W_PARTS))])
    return pl.pallas_call(
        functools.partial(_expert_kernel, layer=layer), grid_spec=grid_spec,
        out_shape=jax.ShapeDtypeStruct((MOE_NBLK * MOE_BLK, D_MODEL), BF16),
        compiler_params=_params(("arbitrary",)), name="experts",
    )(blk_e, n_used, slot, nxt, x_pad, w_gate, w_up, w_down)


def _final_kernel(x_ref, ya_ref, yb_ref, rec_ref, gate_ref, fg_ref, o_ref):
    o_ref[...] = _rms(_with_pending(x_ref, (ya_ref, yb_ref, rec_ref, gate_ref)), fg_ref[...])


def _final_norm(x, pend, final_g, row0, n_rows):
    tm = 512
    tile0 = row0 // tm
    return pl.pallas_call(
        _final_kernel, grid=(n_rows // tm,),
        in_specs=[pl.BlockSpec((tm, D_MODEL), lambda i: (i + tile0, 0))] + _pending_specs(tm, tile0)
        + [pl.BlockSpec((1, D_MODEL), lambda i: (0, 0))],
        out_specs=pl.BlockSpec((tm, D_MODEL), lambda i: (i, 0)),
        out_shape=jax.ShapeDtypeStruct((n_rows, D_MODEL), F32),
        compiler_params=_params(("parallel",)), name="final_norm",
    )(x, *pend, final_g.reshape(1, D_MODEL))


_SC_WORKERS = 32
_SC_CORES = 2
_SC_ROWS = 64


def _dispatch_rows(hp, dest):
    n, width = hp.shape
    per_w = n // _SC_WORKERS
    n_ch = per_w // _SC_ROWS
    idx = dest.T.reshape(2, _SC_WORKERS, n_ch, _SC_ROWS)
    mesh = plsc.VectorSubcoreMesh(core_axis_name="c", subcore_axis_name="s")

    @functools.partial(
        pl.kernel, mesh=mesh, out_type=jax.ShapeDtypeStruct((MOE_NBLK * MOE_BLK, width), hp.dtype),
        scratch_types=[pltpu.VMEM((n_ch, _SC_ROWS), jnp.int32), pltpu.VMEM((n_ch, _SC_ROWS), jnp.int32),
                       pltpu.VMEM((_SC_ROWS, width), hp.dtype)], name="dispatch_rows")
    def scatter(h_hbm, idx_hbm, out_hbm, i0_v, i1_v, rows_v):
        wid = lax.axis_index("s") * _SC_CORES + lax.axis_index("c")
        pltpu.sync_copy(idx_hbm.at[0, wid], i0_v)
        pltpu.sync_copy(idx_hbm.at[1, wid], i1_v)

        @pl.loop(0, n_ch)
        def _(g):
            pltpu.sync_copy(h_hbm.at[pl.ds(wid * per_w + g * _SC_ROWS, _SC_ROWS)], rows_v)
            pltpu.sync_copy(rows_v, out_hbm.at[i0_v.at[g]])
            pltpu.sync_copy(rows_v, out_hbm.at[i1_v.at[g]])

    return scatter(hp, idx)


def _moe(h, rec, cnt, mods, w_gate, w_up, w_down, layer):
    dest, blk_e, n_used, slot, nxt = _plan(rec, cnt)
    y_pad = _experts(_dispatch_rows(h, dest), blk_e, n_used, slot, nxt, w_gate, w_up, w_down, layer)
    return y_pad[dest[:, 0]], y_pad[dest[:, 1]], rec, mods


def kernel(x_prompt, x_sample, c, cache_k, cache_v, state_delta, c_ctx, ada_w, ada_b, norm1_g, norm2_g, final_g,
           ev_w_in, ev_w_out, ev_conv_w, ev_a_log, ev_dt_bias, ev_onorm_g, ev_rpb, od_w_in, od_ln_g, od_ln_b,
           od_w_s, od_b_s, od_w_out, moe_w_rg, moe_b_rg, moe_w_re, moe_b_re, moe_w_gate, moe_w_up, moe_w_down):
    x = (x_prompt.reshape(N_CTX, D_MODEL), x_sample.reshape(N_LAT, D_MODEL))
    cond = jnp.concatenate([c_ctx[None, :], c, jnp.zeros((N_COND - 1 - DEC_BATCH, D_MODEL), F32)], axis=0)
    mods_all = _ada_mods(cond, ada_w, ada_b)
    kctx_all = cache_k.reshape(DEC_BATCH, -1, PAST_LEN, NA_HEADS * NA_HD)
    vctx_all = cache_v.reshape(DEC_BATCH, -1, PAST_LEN, NA_HEADS * NA_HD)

    ks, vs, ss = [], [], []
    pend = None
    for l in range(DEPTH):
        mods = mods_all[l]
        router = _router_weights(moe_w_rg[l], moe_b_rg[l], moe_w_re[l], moe_b_re[l])
        if l % 2 == 0:
            e = l // 2
            proj, ab, kv, x = _even_proj(x, pend, mods, norm1_g[l], ev_w_in[e])
            dn = (proj, ab, ev_conv_w[e], ev_a_log[e], ev_dt_bias[e], ev_onorm_g[e])
            oa_ctx, s_fin = _delta_heads(*dn, SEQ, BATCH, 0, None)
            oa_lat, _ = _delta_heads(*dn, DEC_SEQ, DEC_BATCH, N_CTX // DEC_SEQ, state_delta[:, e])
            ob_ctx = _ctx_attention(proj)
            ob_lat = _na_attention(proj, kctx_all[:, e], vctx_all[:, e], ev_rpb[e])
            x, h, rec, cnt = _even_out(oa_ctx, ob_ctx, oa_lat, ob_lat, x, ev_w_out[e], mods, norm2_g[l],
                                       router)
            na_w = NA_HEADS * NA_HD
            ks.append(kv[:N_CTX, :na_w].reshape(BATCH, SEQ, NA_HEADS, NA_HD))
            vs.append(kv[:N_CTX, na_w:].reshape(BATCH, SEQ, NA_HEADS, NA_HD))
            ss.append(s_fin)
        else:
            o = l // 2
            x, h, rec, cnt = _sgu_layer(x, pend, mods, norm1_g[l], od_w_in[o], od_ln_g[o], od_ln_b[o],
                                        od_w_s[o], od_b_s[o], od_w_out[o], norm2_g[l], router)
        pend = _moe(h, rec, cnt, mods, moe_w_gate, moe_w_up, moe_w_down, l)
    y_prompt = _final_norm(x, pend, final_g, 0, N_CTX).reshape(BATCH, SEQ, D_MODEL)
    y_sample = _final_norm(x, pend, final_g, N_CTX, N_LAT).reshape(DEC_BATCH, DEC_SEQ, D_MODEL)
    return (y_prompt, y_sample, jnp.stack(ks, axis=1), jnp.stack(vs, axis=1), jnp.stack(ss, axis=1))
```

```python
import functools

import jax
import jax.numpy as jnp
from jax import lax
from jax.experimental import pallas as pl
from jax.experimental.pallas import tpu as pltpu
from jax.experimental.pallas import tpu_sc as plsc

F32 = jnp.float32
BF16 = jnp.bfloat16

D_MODEL = 1024
BATCH = 16
SEQ = 256
DEPTH = 4
DEC_BATCH = 4
DEC_SEQ = 2048
PAST_LEN = 512
GRID_W = 64
EPS = 1e-6
NEG_INF = -1e30

DN_HEADS = 4
DN_DK = 128
DN_CHUNK = 64
NA_HEADS = 8
NA_HD = 64
NA_ROWS = 8
NA_COLS = 16
SG_CHUNK = 128
SG_GROUPS = 8
SG_W = 2 * D_MODEL
SG_GW = SG_W // SG_GROUPS
N_EGROUPS = 4
EXP_PER_GROUP = 8
N_EXPERTS = 32
D_EXPERT = 512

N_CTX = BATCH * SEQ
N_LAT = DEC_BATCH * DEC_SEQ
N_TOK = N_CTX + N_LAT
N_COND = 8
PROJ_W = 4096
LANES = 128
MOE_BLK = 256
MOE_NBLK = -(-(2 * N_TOK + N_EXPERTS * (MOE_BLK - 1)) // MOE_BLK)
VMEM_LIMIT = 56 * 1024 * 1024

_QA, _KA, _VA, _ZA, _QB, _KB, _VB = 0, 4, 8, 12, 16, 20, 24


def _params(sem):
    return pltpu.CompilerParams(dimension_semantics=sem, vmem_limit_bytes=VMEM_LIMIT)


def _bdot(a, b):
    return jnp.dot(a.astype(BF16), b.astype(BF16), preferred_element_type=F32)


def _bdot_nt(a, b):
    return lax.dot_general(a.astype(BF16), b.astype(BF16), (((1,), (1,)), ((), ())),
                           preferred_element_type=F32)


def _bdot_tn(a, b):
    return lax.dot_general(a.astype(BF16), b.astype(BF16), (((0,), (0,)), ((), ())),
                           preferred_element_type=F32)


def _split2(a):
    p0 = a.astype(BF16)
    return p0, (a - p0.astype(F32)).astype(BF16)


def _dot3(a, b):
    ah = a.astype(BF16)
    al = (a - ah.astype(F32)).astype(BF16)
    bh = b.astype(BF16)
    bl = (b - bh.astype(F32)).astype(BF16)
    return (jnp.dot(ah, bh, preferred_element_type=F32) + jnp.dot(ah, bl, preferred_element_type=F32)
            + jnp.dot(al, bh, preferred_element_type=F32))


def _mask_bf16(m01):
    return jnp.where(m01, 1.0, 0.0).astype(BF16)


def _xdot(m01, a):
    m = _mask_bf16(m01)
    p0, p1 = _split2(a)
    return jnp.dot(m, p0, preferred_element_type=F32) + jnp.dot(m, p1, preferred_element_type=F32)


def _xdot_r(a, m01):
    m = _mask_bf16(m01)
    p0, p1 = _split2(a)
    return jnp.dot(p0, m, preferred_element_type=F32) + jnp.dot(p1, m, preferred_element_type=F32)


def _sigmoid(x):
    return 0.5 * jnp.tanh(0.5 * x) + 0.5


def _silu(x):
    return x * _sigmoid(x)


def _rms(x, g):
    return x * lax.rsqrt(jnp.mean(x * x, axis=-1, keepdims=True) + EPS) * g


def _cond_index(row):
    return jnp.where(row < N_CTX, 0, 1 + (row - N_CTX) // DEC_SEQ)


def _mod_spec(k, tm, tile0=0):
    return pl.BlockSpec((None, None, 1, D_MODEL), lambda i, *_: (_cond_index((i + tile0) * tm), k, 0, 0))


def _ada_kernel(c_ref, w_ref, b_ref, o_ref):
    o_ref[...] = _bdot(_silu(c_ref[...]), w_ref[...]) + b_ref[...]


def _ada_mods(cond, ada_w, ada_b):
    tn = 1536
    out = pl.pallas_call(
        _ada_kernel, grid=(DEPTH, 6 * D_MODEL // tn),
        in_specs=[pl.BlockSpec((N_COND, D_MODEL), lambda l, j: (0, 0)),
                  pl.BlockSpec((None, D_MODEL, tn), lambda l, j: (l, 0, j)),
                  pl.BlockSpec((None, 1, tn), lambda l, j: (l, 0, j))],
        out_specs=pl.BlockSpec((None, N_COND, tn), lambda l, j: (l, 0, j)),
        out_shape=jax.ShapeDtypeStruct((DEPTH, N_COND, 6 * D_MODEL), F32),
        compiler_params=_params(("parallel", "parallel")), name="ada_mods",
    )(cond, ada_w, ada_b.reshape(DEPTH, 1, 6 * D_MODEL))
    return out.reshape(DEPTH, N_COND, 6, 1, D_MODEL)


_EV_TN = 512
_EV_W = 7 * DN_HEADS * LANES
_KV_COL0 = _KB * LANES


def _with_pending(x_ref, pend):
    if not pend:
        return x_ref[...]
    y2_ref, rec_ref, gate_ref = pend
    rec = rec_ref[...]
    y = (rec[:, _R_W:_R_W + 1] * y2_ref[:, :D_MODEL].astype(F32)
         + rec[:, _R_W + 1:_R_W + 2] * y2_ref[:, D_MODEL:].astype(F32))
    return x_ref[...] + gate_ref[...] * y


def _pending_specs(tm, tile0=0):
    return [pl.BlockSpec((tm, 2 * D_MODEL), lambda i: (i + tile0, 0)),
            pl.BlockSpec((tm, LANES), lambda i: (i + tile0, 0)), _mod_spec(5, tm, tile0)]


def _even_proj_kernel(x_ref, *rest, n_pend, ctx_tiles):
    pend, (g_ref, sh_ref, sc_ref, w_ref, wab_ref, o_ref, ab_ref, kv_ref) = rest[:n_pend], rest[n_pend:n_pend + 8]
    if n_pend == 1:
        x = jnp.where(pl.program_id(0) < ctx_tiles, x_ref[...], pend[0][...])
    else:
        x = _with_pending(x_ref, pend)
    if pend:
        rest[n_pend + 8][...] = x
    h = (_rms(x, g_ref[...]) * (1.0 + sc_ref[...]) + sh_ref[...]).astype(BF16)
    ab_ref[...] = jnp.dot(h, wab_ref[...], preferred_element_type=F32)
    for j in range(_EV_W // _EV_TN):
        c0 = j * _EV_TN
        y = jnp.dot(h, w_ref[:, c0:c0 + _EV_TN], preferred_element_type=F32)
        for c in range(_EV_TN // LANES):
            o_ref[c0 // LANES + c] = y[:, c * LANES:(c + 1) * LANES].astype(BF16)
        if c0 >= _KV_COL0:
            kv_ref[:, c0 - _KV_COL0:c0 - _KV_COL0 + _EV_TN] = y


def _even_proj(x, pend, mods, g, w_in):
    tm = 512
    ctx_tiles = N_CTX // tm
    rows = pl.BlockSpec((tm, D_MODEL), lambda i: (i, 0))
    if isinstance(x, tuple):
        x, pend = x[0], (x[1],)
        x_specs = [pl.BlockSpec((tm, D_MODEL), lambda i: (jnp.minimum(i, ctx_tiles - 1), 0)),
                   pl.BlockSpec((tm, D_MODEL), lambda i: (jnp.maximum(i - ctx_tiles, 0), 0))]
    else:
        pend = tuple(pend) if pend else ()
        x_specs = [rows] + (_pending_specs(tm) if pend else [])
    n_ab = 4 * DN_HEADS
    ab0 = 4 * DN_HEADS * DN_DK
    w_main = jnp.concatenate([w_in[:, :ab0], w_in[:, ab0 + n_ab:]], axis=1).astype(BF16)
    w_ab = jnp.concatenate([w_in[:, ab0:ab0 + n_ab], jnp.zeros((D_MODEL, LANES - n_ab), F32)],
                           axis=1).astype(BF16)
    held = lambda shape: pl.BlockSpec(shape, lambda i: (0,) * len(shape), pipeline_mode=pl.Buffered(1))
    out_specs = [pl.BlockSpec((_EV_W // LANES, tm, LANES), lambda i: (0, i, 0)),
                 pl.BlockSpec((tm, LANES), lambda i: (i, 0)),
                 pl.BlockSpec((tm, 2 * NA_HEADS * NA_HD), lambda i: (i, 0))]
    out_shape = [jax.ShapeDtypeStruct((_EV_W // LANES, N_TOK, LANES), BF16),
                 jax.ShapeDtypeStruct((N_TOK, LANES), F32),
                 jax.ShapeDtypeStruct((N_TOK, 2 * NA_HEADS * NA_HD), F32)]
    if pend:
        out_specs.append(rows)
        out_shape.append(jax.ShapeDtypeStruct((N_TOK, D_MODEL), F32))
    res = pl.pallas_call(
        functools.partial(_even_proj_kernel, n_pend=len(pend), ctx_tiles=ctx_tiles), grid=(N_TOK // tm,),
        in_specs=x_specs
        + [pl.BlockSpec((1, D_MODEL), lambda i: (0, 0)), _mod_spec(0, tm), _mod_spec(1, tm),
           held((D_MODEL, _EV_W)), held((D_MODEL, LANES))],
        out_specs=out_specs, out_shape=out_shape,
        compiler_params=_params(("parallel",)), name="even_proj",
    )(x, *pend, g.reshape(1, D_MODEL), mods, mods, w_main, w_ab)
    return (*res[:3], res[3] if pend else x)


_CHUNK_SHIFT = DN_CHUNK.bit_length() - 1
_CUM_ROWS = 256
_DN_CHAINS = 16
_DN_SHORT = 256
_SERIES_FINE = 3
_MQ_ROWS = DN_DK + DN_CHUNK


def _dn_kernel(*refs, T, HB, has_s0, want_state):
    it = iter(refs)
    q_ref, k_ref, v_ref, z_ref, ab_ref = (next(it) for _ in range(5))
    cwq_ref, cwk_ref, cwv_ref, alog_ref, dtb_ref, og_ref = (next(it) for _ in range(6))
    s0_ref = next(it) if has_s0 else None
    o_ref = next(it)
    sfin_ref = next(it) if want_state else None
    qc, kc, vc, gsc, bsc, osc, b_s, mq_s = (next(it) for _ in range(8))

    C = DN_CHUNK
    n = T // C
    h0 = pl.program_id(1) * HB

    row = lax.broadcasted_iota(jnp.int32, (T, 1), 0)

    def conv(x_ref, cw_ref, hh):
        x = x_ref[hh].astype(F32)
        cw = cw_ref[:, hh * LANES:(hh + 1) * LANES]
        xp = jnp.where(row == 0, 0.0, pltpu.roll(x, 1, 0))
        xn = jnp.where(row == T - 1, 0.0, pltpu.roll(x, T - 1, 0))
        return _silu(cw[0:1, :] * xp + cw[1:2, :] * x + cw[2:3, :] * xn)

    def l2n(x):
        return x * lax.rsqrt(jnp.sum(x * x, axis=-1, keepdims=True) + EPS)

    ab = ab_ref[...]
    lane = lax.broadcasted_iota(jnp.int32, (1, LANES), 1)
    dtb = jnp.zeros((1, LANES), F32)
    alog = jnp.zeros((1, LANES), F32)
    for d in range(2):
        for hq in range(DN_HEADS):
            dtb = jnp.where(lane == d * DN_HEADS + hq, dtb_ref[d, hq], dtb)
            alog = jnp.where(lane == d * DN_HEADS + hq, alog_ref[d, hq], alog)
    xs = ab + dtb
    g_all = -jnp.exp(alog) * (jnp.maximum(xs, 0.0) + jnp.log1p(jnp.exp(-jnp.abs(xs))))
    beta_all = _sigmoid(ab)

    sel_r = lax.broadcasted_iota(jnp.int32, (LANES, LANES), 0)
    for hh in range(HB):
        qc[hh] = l2n(conv(q_ref, cwq_ref, hh)) * (DN_DK ** -0.5)
        kc[hh] = l2n(conv(k_ref, cwk_ref, hh))
        vc[hh] = conv(v_ref, cwv_ref, hh)
        hd = h0 + hh
        for d in range(2):
            gsc[hh, d] = _xdot_r(g_all, sel_r == d * DN_HEADS + hd)
            bsc[hh, d] = _xdot_r(beta_all, sel_r == 2 * DN_HEADS + d * DN_HEADS + hd)

    pr = lax.broadcasted_iota(jnp.int32, (_CUM_ROWS, _CUM_ROWS), 0)
    pc = lax.broadcasted_iota(jnp.int32, (_CUM_ROWS, _CUM_ROWS), 1)
    same = lax.shift_right_logical(pr, _CHUNK_SHIFT) == lax.shift_right_logical(pc, _CHUNK_SHIFT)
    cum_mask = (jnp.logical_and(same, pc <= pr), jnp.logical_and(same, pc >= pr))

    def cum_body(i, carry):
        sl = pl.ds(pl.multiple_of(i * _CUM_ROWS, _CUM_ROWS), _CUM_ROWS)
        for hh in range(HB):
            for d in range(2):
                gsc[hh, d, sl, :] = _xdot(cum_mask[d], gsc[hh, d, sl, :])
        return carry

    lax.fori_loop(0, T // _CUM_ROWS, cum_body, 0)

    ri = lax.broadcasted_iota(jnp.int32, (C, C), 0)
    ci = lax.broadcasted_iota(jnp.int32, (C, C), 1)
    eye = (ri == ci).astype(F32)

    def prepare(items):
        lows, decays = [], []
        kk, qk = {}, {}
        for hh, d, c, slot in items:
            sl = pl.ds(pl.multiple_of(c * C, C), C)
            gc = gsc[hh, d, sl, :]
            if (hh, slot) not in kk:
                k = kc[hh, sl, :]
                kk[hh, slot] = _bdot_nt(k, k)
                qk[hh, slot] = _bdot_nt(qc[hh, sl, :], k)
            incl = (ci <= ri) if d == 0 else (ci >= ri)
            strict = (ci < ri) if d == 0 else (ci > ri)
            gr = jnp.transpose(gc)[0:1, :C]
            decay = jnp.where(incl, jnp.exp(jnp.where(incl, gc[:, :C] - gr, 0.0)), 0.0)
            lows.append(jnp.where(strict, bsc[hh, d, sl, :C] * kk[hh, slot] * decay, 0.0))
            decays.append(decay)
        ts = [eye - low for low in lows]
        ps = lows
        for step in range(5):
            dot = _dot3 if step < _SERIES_FINE else _bdot
            ps = [dot(p, p) for p in ps]
            ts = [t + dot(t, p) for t, p in zip(ts, ps)]
        for (hh, d, c, slot), t, decay in zip(items, ts, decays):
            sl = pl.ds(pl.multiple_of(c * C, C), C)
            q, k, gc, beta = qc[hh, sl, :], kc[hh, sl, :], gsc[hh, d, sl, :], bsc[hh, d, sl, :]
            eg = jnp.exp(gc)
            uw = _bdot(t, jnp.concatenate([vc[hh, sl, :] * beta, k * beta * eg], axis=-1))
            last = gc[C - 1:C, :] if d == 0 else gc[0:1, :]
            wu = jnp.concatenate([uw[:, LANES:], uw[:, :LANES]], axis=-1).astype(BF16)
            kd = (k * jnp.exp(last - gc)).astype(BF16)
            attn = (qk[hh, slot] * decay).astype(BF16)
            kdwu = lax.dot_general(kd, wu, (((0,), (0,)), ((), ())), preferred_element_type=F32)
            awu = jnp.dot(attn, wu, preferred_element_type=F32)
            mq0 = pl.multiple_of(c * _MQ_ROWS, _MQ_ROWS)
            mq_s[hh, d, pl.ds(mq0, DN_DK), :] = kdwu[:, :LANES].astype(BF16)
            mq_s[hh, d, pl.ds(mq0 + DN_DK, C), :] = (q * eg - awu[:, :LANES]).astype(BF16)
            b_s[hh, d, pl.ds(pl.multiple_of(c * DN_DK, DN_DK), DN_DK), :] = kdwu[:, LANES:]
            osc[hh, d, sl, :] = awu[:, LANES:]

    n_prep = min(n, _DN_CHAINS // 2)
    h_prep = max(1, min(HB, _DN_CHAINS // (2 * n_prep)))

    def prep_body(i, carry):
        for hg in range(0, HB, h_prep):
            prepare([(hh, d, i * n_prep + j, j) for hh in range(hg, hg + h_prep) for j in range(n_prep)
                     for d in range(2)])
        return carry

    lax.fori_loop(0, n // n_prep, prep_body, 0)

    def advance(hh, d, c, S):
        sl = pl.ds(pl.multiple_of(c * C, C), C)
        ms = jnp.dot(mq_s[hh, d, pl.ds(pl.multiple_of(c * _MQ_ROWS, _MQ_ROWS), _MQ_ROWS), :], S.astype(BF16),
                     preferred_element_type=F32)
        osc[hh, d, sl, :] = osc[hh, d, sl, :] + ms[DN_DK:]
        last = gsc[hh, d, pl.ds(c * C + (C - 1 if d == 0 else 0), 1), :]
        return (S * jnp.exp(last) - ms[:DN_DK]
                + b_s[hh, d, pl.ds(pl.multiple_of(c * DN_DK, DN_DK), DN_DK), :])

    def body(i, carry):
        return tuple(advance(hh, d, i if d == 0 else n - 1 - i, carry[2 * hh + d])
                     for hh in range(HB) for d in range(2))

    if has_s0:
        init = tuple(s0_ref[d, hh] for hh in range(HB) for d in range(2))
    else:
        init = tuple(jnp.zeros((DN_DK, LANES), F32) for _ in range(2 * HB))
    fin = lax.fori_loop(0, n, body, init)
    for hh in range(HB):
        if want_state:
            sfin_ref[0, hh] = fin[2 * hh]
            sfin_ref[1, hh] = fin[2 * hh + 1]
        o = osc[hh, 0] + osc[hh, 1]
        o_ref[hh] = (_rms(o, og_ref[...]) * _silu(z_ref[hh].astype(F32))).astype(o_ref.dtype)


def _delta_heads(proj, ab, conv_w, a_log, dt_bias, onorm_g, T, n_seq, row0, s0):
    has_s0 = s0 is not None
    want_state = not has_s0
    hb = DN_HEADS if T <= _DN_SHORT else 2

    def col(cb):
        return pl.BlockSpec((hb, T, LANES), lambda s, h: (cb // hb + h, row0 + s, 0))

    def cw(cb):
        return pl.BlockSpec((3, hb * LANES), lambda s, h: (0, cb // hb + h))

    smem = pl.BlockSpec(memory_space=pltpu.SMEM)
    in_specs = [col(_QA), col(_KA), col(_VA), col(_ZA),
                pl.BlockSpec((T, LANES), lambda s, h: (row0 + s, 0)),
                cw(0), cw(4), cw(8), smem, smem,
                pl.BlockSpec((1, LANES), lambda s, h: (0, 0))]
    args = [proj, proj, proj, proj, ab, conv_w, conv_w, conv_w, a_log, dt_bias,
            onorm_g.reshape(1, LANES)]
    state_spec = pl.BlockSpec((None, 2, hb, DN_DK, LANES), lambda s, h: (s, 0, h, 0, 0))
    if has_s0:
        in_specs.append(state_spec)
        args.append(s0)
    out_shape = [jax.ShapeDtypeStruct((DN_HEADS, n_seq * T, LANES), BF16)]
    out_specs = [pl.BlockSpec((hb, T, LANES), lambda s, h: (h, s, 0))]
    if want_state:
        out_shape.append(jax.ShapeDtypeStruct((n_seq, 2, DN_HEADS, DN_DK, LANES), F32))
        out_specs.append(state_spec)
    res = pl.pallas_call(
        functools.partial(_dn_kernel, T=T, HB=hb, has_s0=has_s0, want_state=want_state),
        grid=(n_seq, DN_HEADS // hb), in_specs=in_specs, out_specs=out_specs, out_shape=out_shape,
        scratch_shapes=[pltpu.VMEM((hb, T, LANES), F32)] * 3
        + [pltpu.VMEM((hb, 2, T, LANES), F32)] * 3
        + [pltpu.VMEM((hb, 2, T // DN_CHUNK * DN_DK, LANES), F32),
           pltpu.VMEM((hb, 2, T // DN_CHUNK * _MQ_ROWS, LANES), BF16)],
        compiler_params=_params(("parallel", "parallel")), name="delta_heads_%d" % T,
    )(*args)
    return res if want_state else (res[0], None)


def _pair_queries(q, first):
    return jnp.concatenate([jnp.where(first, q, 0.0), jnp.where(first, 0.0, q)], axis=0).astype(BF16)


def _ctx_attn_kernel(q_ref, k_ref, v_ref, o_ref):
    first = lax.broadcasted_iota(jnp.int32, (SEQ, LANES), 1) < NA_HD
    qm = _pair_queries(q_ref[...] * (NA_HD ** -0.5), first)
    s = lax.dot_general(k_ref[...], qm, (((1,), (1,)), ((), ())), preferred_element_type=F32)
    e = jnp.exp(s - jnp.max(s, axis=0, keepdims=True))
    den = jnp.sum(e, axis=0, keepdims=True)
    o = lax.dot_general(e.astype(BF16), v_ref[...], (((0,), (0,)), ((), ())), preferred_element_type=F32)
    o = jnp.where(first, o[:SEQ], o[SEQ:])
    den_t = jnp.transpose(jnp.broadcast_to(den, (LANES, 2 * SEQ)))
    o_ref[...] = (o / jnp.where(first, den_t[:SEQ], den_t[SEQ:])).astype(o_ref.dtype)


def _ctx_attention(proj):
    def col(cb):
        return pl.BlockSpec((None, SEQ, LANES), lambda s, p: (cb + p, s, 0))

    return pl.pallas_call(
        _ctx_attn_kernel, grid=(BATCH, NA_HEADS // 2),
        in_specs=[col(_QB), col(_KB), col(_VB)],
        out_specs=pl.BlockSpec((None, SEQ, LANES), lambda s, p: (p, s, 0)),
        out_shape=jax.ShapeDtypeStruct((NA_HEADS // 2, N_CTX, LANES), BF16),
        compiler_params=_params(("parallel", "parallel")), name="ctx_attention",
    )(proj, proj, proj)


_NA_UNROLL = 4


def _na_kernel(q_ref, k_ref, v_ref, kc_ref, vc_ref, bias_ref, o_ref, kcb_scr, vcb_scr):
    rows = DEC_SEQ // GRID_W
    win = NA_ROWS * GRID_W
    scale = NA_HD ** -0.5
    dn_nt = (((1,), (1,)), ((), ()))
    dn_tn = (((0,), (0,)), ((), ()))

    kcb_scr[...] = kc_ref[...].astype(BF16)
    vcb_scr[...] = vc_ref[...].astype(BF16)
    first = lax.broadcasted_iota(jnp.int32, (GRID_W, LANES), 1) < NA_HD

    def body(it, carry):
        rr = [it * _NA_UNROLL + j for j in range(_NA_UNROLL)]
        rss = [jnp.clip(r - NA_ROWS // 2, 0, rows - NA_ROWS) for r in rr]
        qsls = [pl.ds(pl.multiple_of(r * GRID_W, GRID_W), GRID_W) for r in rr]
        wsls = [pl.ds(pl.multiple_of(rs * GRID_W, GRID_W), win) for rs in rss]
        qms, s_wins, s_ctxs = [], [], []
        for r, rs, qsl, wsl in zip(rr, rss, qsls, wsls):
            qm = _pair_queries(q_ref[qsl, :] * scale, first)
            bias = jnp.concatenate([bias_ref[NA_ROWS - 1 - (r - rs) + i] for i in range(NA_ROWS)], axis=0)
            s_wins.append(lax.dot_general(k_ref[wsl, :], qm, dn_nt, preferred_element_type=F32) + bias)
            s_ctxs.append(lax.dot_general(kcb_scr[...], qm, dn_nt, preferred_element_type=F32))
        ms = [jnp.maximum(jnp.max(sw, axis=0, keepdims=True), jnp.max(sc, axis=0, keepdims=True))
              for sw, sc in zip(s_wins, s_ctxs)]
        e_wins = [jnp.exp(sw - m) for sw, m in zip(s_wins, ms)]
        e_ctxs = [jnp.exp(sc - m) for sc, m in zip(s_ctxs, ms)]
        dens = [jnp.sum(ew, axis=0, keepdims=True) + jnp.sum(ec, axis=0, keepdims=True)
                for ew, ec in zip(e_wins, e_ctxs)]
        for qsl, wsl, ew, ec, den in zip(qsls, wsls, e_wins, e_ctxs, dens):
            o = (lax.dot_general(ew.astype(BF16), v_ref[wsl, :], dn_tn, preferred_element_type=F32)
                 + lax.dot_general(ec.astype(BF16), vcb_scr[...], dn_tn, preferred_element_type=F32))
            o = o / jnp.transpose(jnp.broadcast_to(den, (LANES, LANES)))
            o_ref[qsl, :] = jnp.where(first, o[:GRID_W], o[GRID_W:]).astype(o_ref.dtype)
        return carry

    lax.fori_loop(0, rows // _NA_UNROLL, body, 0)


def _na_bias_table(rpb):
    col = jnp.arange(GRID_W)
    cs = jnp.clip(col - NA_COLS // 2, 0, GRID_W - NA_COLS)
    col_ok = (col[None, :] >= cs[:, None]) & (col[None, :] < cs[:, None] + NA_COLS)
    dc = jnp.clip(col[None, :] - col[:, None] + NA_COLS - 1, 0, 2 * NA_COLS - 2)
    onehot = (dc.T[None, :, :] == jnp.arange(2 * NA_COLS - 1)[:, None, None]).astype(F32)
    t = jnp.einsum('hrd,dkq->hrkq', rpb.astype(F32), onehot, precision=lax.Precision.HIGHEST)
    t = jnp.where(col_ok.T[None, None], t, NEG_INF)
    t = t.reshape(NA_HEADS // 2, 2, 2 * NA_ROWS - 1, GRID_W, GRID_W)
    return jnp.concatenate([t[:, 0], t[:, 1]], axis=-1)


def _na_attention(proj, kctx, vctx, rpb):
    blk = N_CTX // DEC_SEQ

    def col(cb):
        return pl.BlockSpec((None, DEC_SEQ, LANES), lambda b, p: (cb + p, blk + b, 0))

    ctx = pl.BlockSpec((None, PAST_LEN, LANES), lambda b, p: (b, 0, p))
    return pl.pallas_call(
        _na_kernel, grid=(DEC_BATCH, NA_HEADS // 2),
        in_specs=[col(_QB), col(_KB), col(_VB), ctx, ctx,
                  pl.BlockSpec((None, 2 * NA_ROWS - 1, GRID_W, 2 * GRID_W), lambda b, p: (p, 0, 0, 0))],
        out_specs=pl.BlockSpec((None, DEC_SEQ, LANES), lambda b, p: (p, b, 0)),
        out_shape=jax.ShapeDtypeStruct((NA_HEADS // 2, N_LAT, LANES), BF16),
        scratch_shapes=[pltpu.VMEM((PAST_LEN, LANES), BF16), pltpu.VMEM((PAST_LEN, LANES), BF16)],
        compiler_params=_params(("parallel", "parallel")), name="na_attention",
    )(proj, proj, proj, kctx, vctx, _na_bias_table(rpb))


_LOGIT0 = N_EGROUPS
_R_E, _R_W, _R_RANK = 0, 2, 4


def _lane_min_where(mask, lane):
    return jnp.min(jnp.where(mask, lane, LANES), axis=-1, keepdims=True)


def _route_rows(lg, carry_ref, tri_ref):
    big = -3.0e38
    lane = lax.broadcasted_iota(jnp.int32, lg.shape, 1)
    is_g = lane < N_EGROUPS
    gmax = jnp.max(jnp.where(is_g, lg, big), axis=-1, keepdims=True)
    gsum = jnp.sum(jnp.where(is_g, jnp.exp(jnp.where(is_g, lg - gmax, 0.0)), 0.0), axis=-1, keepdims=True)
    pg_top = 1.0 / gsum
    g_idx = _lane_min_where(jnp.logical_and(is_g, lg == gmax), lane)
    in_g = jnp.logical_and(lane >= _LOGIT0, lax.shift_right_arithmetic(lane - _LOGIT0, 3) == g_idx)
    in_g = jnp.logical_and(in_g, lane < _LOGIT0 + N_EXPERTS)
    m1 = jnp.max(jnp.where(in_g, lg, big), axis=-1, keepdims=True)
    i1 = _lane_min_where(jnp.logical_and(in_g, lg == m1), lane)
    rest = jnp.logical_and(in_g, lane != i1)
    m2 = jnp.max(jnp.where(rest, lg, big), axis=-1, keepdims=True)
    i2 = _lane_min_where(jnp.logical_and(rest, lg == m2), lane)
    e2 = jnp.exp(m2 - m1)
    w1 = pg_top * (1.0 / (1.0 + e2))
    w2 = pg_top * (e2 / (1.0 + e2))
    hit1 = lane == i1
    hit2 = lane == i2
    picked = jnp.where(jnp.logical_or(hit1, hit2), 1.0, 0.0)
    before = jnp.dot(tri_ref[...], picked.astype(BF16), preferred_element_type=F32) + carry_ref[...]
    r1 = jnp.sum(jnp.where(hit1, before, 0.0), axis=-1, keepdims=True)
    r2 = jnp.sum(jnp.where(hit2, before, 0.0), axis=-1, keepdims=True)
    carry_ref[...] = carry_ref[...] + jnp.sum(picked, axis=0, keepdims=True)
    rec = jnp.zeros(lg.shape, F32)
    for ln, val in ((_R_E, (i1 - _LOGIT0).astype(F32)), (_R_E + 1, (i2 - _LOGIT0).astype(F32)),
                    (_R_W, w1), (_R_W + 1, w2), (_R_RANK, r1), (_R_RANK + 1, r2)):
        rec = jnp.where(lane == ln, val, rec)
    return rec


_PACK_W = D_MODEL // 2


def _pack_rows(hb):
    lo = lax.bitcast_convert_type(hb[:, :_PACK_W].astype(F32), jnp.int32)
    hi = lax.bitcast_convert_type(hb[:, _PACK_W:].astype(F32), jnp.int32)
    return jnp.bitwise_or(jnp.bitwise_and(hi, -65536), lax.shift_right_logical(lo, 16))


def _unpack_rows(w):
    lo = lax.bitcast_convert_type(lax.shift_left(w, 16), F32)
    hi = lax.bitcast_convert_type(jnp.bitwise_and(w, -65536), F32)
    return jnp.concatenate([lo, hi], axis=-1).astype(BF16)


def _moe_input(xnew, first, tail_in, tail_out, tail_scr):
    g2_ref, sc2_ref, sh2_ref, wr_ref, br_ref = tail_in
    x_out, h_out, rec_out, cnt_out = tail_out
    tri_scr, carry_scr = tail_scr

    @pl.when(first)
    def _():
        tm = tri_scr.shape[0]
        r = lax.broadcasted_iota(jnp.int32, (tm, tm), 0)
        c = lax.broadcasted_iota(jnp.int32, (tm, tm), 1)
        tri_scr[...] = jnp.where(c < r, 1.0, 0.0).astype(BF16)
        carry_scr[...] = jnp.zeros(carry_scr.shape, F32)

    x_out[...] = xnew
    h = _rms(xnew, g2_ref[...]) * (1.0 + sc2_ref[...]) + sh2_ref[...]
    hh = h.astype(BF16)
    h_out[...] = _pack_rows(hh)
    lg = jnp.dot(hh, wr_ref[...], preferred_element_type=F32) + br_ref[...]
    rec_out[...] = _route_rows(lg, carry_scr, tri_scr)
    cnt_out[...] = carry_scr[...]


def _even_out_kernel(oac_ref, obc_ref, oal_ref, obl_ref, x_ref, w_ref, gate_ref, *rest, ctx_tiles):
    tail_in, tail_out, (w_scr,), tail_scr = rest[:5], rest[5:9], rest[9:10], rest[10:]
    first = pl.program_id(0) == 0

    @pl.when(first)
    def _():
        w_scr[...] = w_ref[...].astype(BF16)

    is_ctx = pl.program_id(0) < ctx_tiles
    parts = [jnp.where(is_ctx, c_ref[hb], l_ref[hb])
             for c_ref, l_ref in ((oac_ref, oal_ref), (obc_ref, obl_ref)) for hb in range(DN_HEADS)]
    mix = jnp.concatenate(parts, axis=-1)
    out = jnp.dot(mix, w_scr[...], preferred_element_type=F32)
    _moe_input(x_ref[...] + gate_ref[...] * out, first, tail_in, tail_out, tail_scr)


def _tail_specs(tm):
    const = lambda shape: pl.BlockSpec(shape, lambda i: (0,) * len(shape))
    in_specs = [_mod_spec(2, tm), const((1, D_MODEL)), _mod_spec(4, tm), _mod_spec(3, tm),
                const((D_MODEL, LANES)), const((1, LANES))]
    out_specs = [pl.BlockSpec((tm, D_MODEL), lambda i: (i, 0)),
                 pl.BlockSpec((tm, _PACK_W), lambda i: (i, 0)),
                 pl.BlockSpec((tm, LANES), lambda i: (i, 0)),
                 const((1, LANES))]
    out_shape = [jax.ShapeDtypeStruct((N_TOK, D_MODEL), F32),
                 jax.ShapeDtypeStruct((N_TOK, _PACK_W), jnp.int32),
                 jax.ShapeDtypeStruct((N_TOK, LANES), F32),
                 jax.ShapeDtypeStruct((1, LANES), F32)]
    scratch = [pltpu.VMEM((tm, tm), BF16), pltpu.VMEM((1, LANES), F32)]
    return in_specs, out_specs, out_shape, scratch


def _router_weights(w_rg, b_rg, w_re, b_re):
    pad = LANES - N_EGROUPS - N_EXPERTS
    w = jnp.concatenate([w_rg, w_re, jnp.zeros((D_MODEL, pad), F32)], axis=1)
    b = jnp.concatenate([b_rg, b_re, jnp.zeros((pad,), F32)]).reshape(1, LANES)
    return w.astype(BF16), b


def _even_out(oa_ctx, ob_ctx, oa_lat, ob_lat, x, w_out, mods, g2, router):
    tm = 512
    ctx_tiles = N_CTX // tm
    tail_in, out_specs, out_shape, tail_scr = _tail_specs(tm)
    ctxblk = pl.BlockSpec((DN_HEADS, tm, LANES), lambda i: (0, jnp.minimum(i, ctx_tiles - 1), 0))
    latblk = pl.BlockSpec((DN_HEADS, tm, LANES), lambda i: (0, jnp.maximum(i - ctx_tiles, 0), 0))
    return pl.pallas_call(
        functools.partial(_even_out_kernel, ctx_tiles=ctx_tiles), grid=(N_TOK // tm,),
        in_specs=[ctxblk, ctxblk, latblk, latblk, pl.BlockSpec((tm, D_MODEL), lambda i: (i, 0)),
                  pl.BlockSpec((D_MODEL, D_MODEL), lambda i: (0, 0))] + tail_in,
        out_specs=out_specs, out_shape=out_shape,
        scratch_shapes=[pltpu.VMEM((D_MODEL, D_MODEL), BF16)] + tail_scr,
        compiler_params=_params(("arbitrary",)), name="even_out",
    )(oa_ctx, ob_ctx, oa_lat, ob_lat, x, w_out, mods, g2.reshape(1, D_MODEL), mods, mods, *router)


def _gelu_tanh(x):
    c = 0.7978845608028654
    hx = 0.5 * x
    return hx + hx * jnp.tanh(x * (c + (c * 0.044715) * (x * x)))


def _sgu_kernel(x_ref, *rest, tm, n_pend):
    pend, rest = rest[:n_pend], rest[n_pend:]
    g1_ref, sh1_ref, sc1_ref, win_ref, lng_ref, lnb_ref, ws_ref, bst_ref, wout_ref, gate_ref = rest[:10]
    rest = rest[10:]
    tail_in, tail_out, (v_scr, m_scr), tail_scr = rest[:5], rest[5:9], rest[9:11], rest[11:]
    first = pl.program_id(0) == 0
    x = _with_pending(x_ref, pend)
    h = (_rms(x, g1_ref[...]) * (1.0 + sc1_ref[...]) + sh1_ref[...]).astype(BF16)

    v = _gelu_tanh(jnp.dot(h, win_ref[:, SG_W:], preferred_element_type=F32))
    mu = jnp.mean(v, axis=-1, keepdims=True)
    vc = v - mu
    var = jnp.mean(vc * vc, axis=-1, keepdims=True)
    v_scr[...] = (vc * lax.rsqrt(var + EPS) * lng_ref[...] + lnb_ref[...]).astype(BF16)

    for g in range(SG_GROUPS):
        cs = slice(g * SG_GW, (g + 1) * SG_GW)
        u = _gelu_tanh(jnp.dot(h, win_ref[:, cs], preferred_element_type=F32))
        w_sp = ws_ref[g].astype(BF16)
        for c in range(tm // SG_CHUNK):
            rs = slice(c * SG_CHUNK, (c + 1) * SG_CHUNK)
            sp = jnp.dot(w_sp, v_scr[rs, cs], preferred_element_type=F32) + bst_ref[:, g:g + 1]
            m_scr[rs, cs] = (u[rs] * sp).astype(BF16)
    out = jnp.dot(m_scr[...], wout_ref[...], preferred_element_type=F32)
    _moe_input(x + gate_ref[...] * out, first, tail_in, tail_out, tail_scr)


def _sgu_layer(x, pend, mods, g1, w_in, ln_g, ln_b, w_s, b_s, w_out, g2, router):
    tm = 512
    pend = tuple(pend) if pend else ()
    tail_in, out_specs, out_shape, tail_scr = _tail_specs(tm)
    const = lambda shape: pl.BlockSpec(shape, lambda i: (0,) * len(shape))
    held = lambda shape: pl.BlockSpec(shape, lambda i: (0,) * len(shape), pipeline_mode=pl.Buffered(1))
    return pl.pallas_call(
        functools.partial(_sgu_kernel, tm=tm, n_pend=len(pend)), grid=(N_TOK // tm,),
        in_specs=[pl.BlockSpec((tm, D_MODEL), lambda i: (i, 0))] + (_pending_specs(tm) if pend else [])
        + [const((1, D_MODEL)), _mod_spec(0, tm), _mod_spec(1, tm),
                  held((D_MODEL, 2 * SG_W)), const((1, SG_W)), const((1, SG_W)),
                  const((SG_GROUPS, SG_CHUNK, SG_CHUNK)), const((SG_CHUNK, SG_GROUPS)),
                  held((SG_W, D_MODEL))] + tail_in,
        out_specs=out_specs, out_shape=out_shape,
        scratch_shapes=[pltpu.VMEM((tm, SG_W), BF16), pltpu.VMEM((tm, SG_W), BF16)] + tail_scr,
        compiler_params=_params(("arbitrary",)), name="sgu_layer",
    )(x, *pend, g1.reshape(1, D_MODEL), mods, mods, w_in.astype(BF16), ln_g.reshape(1, SG_W), ln_b.reshape(1, SG_W),
      w_s, b_s.T, w_out.astype(BF16), mods, g2.reshape(1, D_MODEL), mods, mods, *router)


def _plan(rec, cnt):
    e_idx = rec[:, _R_E:_R_E + 2].astype(jnp.int32)
    rank = rec[:, _R_RANK:_R_RANK + 2].astype(jnp.int32)
    counts = cnt[0, _LOGIT0:_LOGIT0 + N_EXPERTS].astype(jnp.int32)
    padded = (counts + MOE_BLK - 1) // MOE_BLK * MOE_BLK
    pad_end = jnp.cumsum(padded)
    pad_start = pad_end - padded
    hit = e_idx[:, :, None] == jnp.arange(N_EXPERTS, dtype=jnp.int32)[None, None, :]
    dest = jnp.sum(jnp.where(hit, pad_start[None, None, :], 0), axis=-1) + rank
    blk0 = jnp.arange(MOE_NBLK, dtype=jnp.int32) * MOE_BLK
    blk_e = jnp.minimum(jnp.sum((pad_end[None, :] <= blk0[:, None]).astype(jnp.int32), axis=-1),
                        N_EXPERTS - 1)
    n_used = (pad_end[-1] // MOE_BLK).astype(jnp.int32).reshape(1)
    owns = counts > 0
    slot_of = (jnp.cumsum(owns.astype(jnp.int32)) - 1) % _W_SLOTS
    ids = jnp.arange(N_EXPERTS, dtype=jnp.int32)
    later = jnp.logical_and(owns[None, :], ids[None, :] > ids[:, None])
    next_of = jnp.min(jnp.where(later, ids[None, :], N_EXPERTS), axis=-1)
    next2_of = jnp.concatenate([next_of, jnp.full((1,), N_EXPERTS, jnp.int32)])[next_of]
    ahead = jnp.stack([next_of, next2_of], axis=0)
    ahead = jnp.where(ahead == N_EXPERTS, -1, ahead)
    return dest, blk_e, n_used, slot_of[blk_e], ahead[:, blk_e].reshape(-1)


_W_PARTS = 4
_W_SLOTS = 3


def _expert_kernel(blk_e_ref, n_used_ref, slot_ref, next_ref, x_ref, wg_hbm, wu_hbm, wd_hbm, o_ref,
                   wg_buf, wu_buf, wd_buf, wg_scr, wu_scr, wd_scr, sems, *, layer):
    j = pl.program_id(0)
    e = blk_e_ref[j]
    slot = slot_ref[j]
    fresh = jnp.logical_or(j == 0, e != blk_e_ref[jnp.maximum(j - 1, 0)])
    live = j < n_used_ref[0]

    def copies(expert, s):
        out = []
        for m, (hbm, buf) in enumerate(((wg_hbm, wg_buf), (wu_hbm, wu_buf), (wd_hbm, wd_buf))):
            rows = buf.shape[1] // _W_PARTS
            for part in range(_W_PARTS):
                band = pl.ds(part * rows, rows)
                out.append(pltpu.make_async_copy(hbm.at[layer, expert, band], buf.at[s, band],
                                                 sems.at[s, m, part]))
        return out

    def start_if_any(expert, s):
        @pl.when(expert >= 0)
        def _():
            for cp in copies(expert, s):
                cp.start()

    @pl.when(j == 0)
    def _():
        for cp in copies(e, slot):
            cp.start()
        start_if_any(next_ref[j], lax.rem(slot + 1, _W_SLOTS))

    @pl.when(jnp.logical_and(fresh, live))
    def _():
        for cp in copies(e, slot):
            cp.wait()
        start_if_any(next_ref[MOE_NBLK + j], lax.rem(slot + 2, _W_SLOTS))

        wg_scr[...] = wg_buf[slot].astype(BF16)
        wu_scr[...] = wu_buf[slot].astype(BF16)
        wd_scr[...] = wd_buf[slot].astype(BF16)

    @pl.when(live)
    def _():
        x = _unpack_rows(x_ref[...])
        gt = jnp.dot(x, wg_scr[...], preferred_element_type=F32)
        up = jnp.dot(x, wu_scr[...], preferred_element_type=F32)
        hb = (_silu(gt) * up).astype(BF16)
        o_ref[...] = jnp.dot(hb, wd_scr[...], preferred_element_type=F32).astype(o_ref.dtype)

    @pl.when(jnp.logical_not(live))
    def _():
        o_ref[...] = jnp.zeros(o_ref.shape, o_ref.dtype)


def _experts(x_pad, blk_e, n_used, slot, nxt, w_gate, w_up, w_down, layer):
    hbm = pl.BlockSpec(memory_space=pl.ANY)
    grid_spec = pltpu.PrefetchScalarGridSpec(
        num_scalar_prefetch=4, grid=(MOE_NBLK,),
        in_specs=[pl.BlockSpec((MOE_BLK, _PACK_W), lambda j, be, nu, *_: (jnp.minimum(j, nu[0] - 1), 0)),
                  hbm, hbm, hbm],
        out_specs=pl.BlockSpec((MOE_BLK, D_MODEL), lambda j, *_: (j, 0)),
        scratch_shapes=[pltpu.VMEM((_W_SLOTS, D_MODEL, D_EXPERT), F32),
                        pltpu.VMEM((_W_SLOTS, D_MODEL, D_EXPERT), F32),
                        pltpu.VMEM((_W_SLOTS, D_EXPERT, D_MODEL), F32),
                        pltpu.VMEM((D_MODEL, D_EXPERT), BF16), pltpu.VMEM((D_MODEL, D_EXPERT), BF16),
                        pltpu.VMEM((D_EXPERT, D_MODEL), BF16),
                        pltpu.SemaphoreType.DMA((_W_SLOTS, 3, _W_PARTS))])
    return pl.pallas_call(
        functools.partial(_expert_kernel, layer=layer), grid_spec=grid_spec,
        out_shape=jax.ShapeDtypeStruct((MOE_NBLK * MOE_BLK, D_MODEL), BF16),
        compiler_params=_params(("arbitrary",)), name="experts",
    )(blk_e, n_used, slot, nxt, x_pad, w_gate, w_up, w_down)


def _final_kernel(x_ref, y2_ref, rec_ref, gate_ref, fg_ref, o_ref):
    o_ref[...] = _rms(_with_pending(x_ref, (y2_ref, rec_ref, gate_ref)), fg_ref[...])


def _final_norm(x, pend, final_g, row0, n_rows):
    tm = 512
    tile0 = row0 // tm
    return pl.pallas_call(
        _final_kernel, grid=(n_rows // tm,),
        in_specs=[pl.BlockSpec((tm, D_MODEL), lambda i: (i + tile0, 0))] + _pending_specs(tm, tile0)
        + [pl.BlockSpec((1, D_MODEL), lambda i: (0, 0))],
        out_specs=pl.BlockSpec((tm, D_MODEL), lambda i: (i, 0)),
        out_shape=jax.ShapeDtypeStruct((n_rows, D_MODEL), F32),
        compiler_params=_params(("parallel",)), name="final_norm",
    )(x, *pend, final_g.reshape(1, D_MODEL))


_SC_WORKERS = 32
_SC_CORES = 2
_SC_ROWS = 64


def _dispatch_rows(hp, dest):
    n, width = hp.shape
    per_w = n // _SC_WORKERS
    n_ch = per_w // _SC_ROWS
    idx = dest.T.reshape(2, _SC_WORKERS, n_ch, _SC_ROWS)
    mesh = plsc.VectorSubcoreMesh(core_axis_name="c", subcore_axis_name="s")

    @functools.partial(
        pl.kernel, mesh=mesh, out_type=jax.ShapeDtypeStruct((MOE_NBLK * MOE_BLK, width), hp.dtype),
        scratch_types=[pltpu.VMEM((n_ch, _SC_ROWS), jnp.int32), pltpu.VMEM((n_ch, _SC_ROWS), jnp.int32),
                       pltpu.VMEM((_SC_ROWS, width), hp.dtype)], name="dispatch_rows")
    def scatter(h_hbm, idx_hbm, out_hbm, i0_v, i1_v, rows_v):
        wid = lax.axis_index("s") * _SC_CORES + lax.axis_index("c")
        pltpu.sync_copy(idx_hbm.at[0, wid], i0_v)
        pltpu.sync_copy(idx_hbm.at[1, wid], i1_v)

        @pl.loop(0, n_ch)
        def _(g):
            pltpu.sync_copy(h_hbm.at[pl.ds(wid * per_w + g * _SC_ROWS, _SC_ROWS)], rows_v)
            pltpu.sync_copy(rows_v, out_hbm.at[i0_v.at[g]])
            pltpu.sync_copy(rows_v, out_hbm.at[i1_v.at[g]])

    return scatter(hp, idx)


def _moe(h, rec, cnt, mods, w_gate, w_up, w_down, layer):
    dest, blk_e, n_used, slot, nxt = _plan(rec, cnt)
    y_pad = _experts(_dispatch_rows(h, dest), blk_e, n_used, slot, nxt, w_gate, w_up, w_down, layer)
    return y_pad[dest.reshape(-1)].reshape(N_TOK, 2 * D_MODEL), rec, mods


def kernel(x_prompt, x_sample, c, cache_k, cache_v, state_delta, c_ctx, ada_w, ada_b, norm1_g, norm2_g, final_g,
           ev_w_in, ev_w_out, ev_conv_w, ev_a_log, ev_dt_bias, ev_onorm_g, ev_rpb, od_w_in, od_ln_g, od_ln_b,
           od_w_s, od_b_s, od_w_out, moe_w_rg, moe_b_rg, moe_w_re, moe_b_re, moe_w_gate, moe_w_up, moe_w_down):
    x = (x_prompt.reshape(N_CTX, D_MODEL), x_sample.reshape(N_LAT, D_MODEL))
    cond = jnp.concatenate([c_ctx[None, :], c, jnp.zeros((N_COND - 1 - DEC_BATCH, D_MODEL), F32)], axis=0)
    mods_all = _ada_mods(cond, ada_w, ada_b)
    kctx_all = cache_k.reshape(DEC_BATCH, -1, PAST_LEN, NA_HEADS * NA_HD)
    vctx_all = cache_v.reshape(DEC_BATCH, -1, PAST_LEN, NA_HEADS * NA_HD)

    ks, vs, ss = [], [], []
    pend = None
    for l in range(DEPTH):
        mods = mods_all[l]
        router = _router_weights(moe_w_rg[l], moe_b_rg[l], moe_w_re[l], moe_b_re[l])
        if l % 2 == 0:
            e = l // 2
            proj, ab, kv, x = _even_proj(x, pend, mods, norm1_g[l], ev_w_in[e])
            dn = (proj, ab, ev_conv_w[e], ev_a_log[e], ev_dt_bias[e], ev_onorm_g[e])
            oa_ctx, s_fin = _delta_heads(*dn, SEQ, BATCH, 0, None)
            oa_lat, _ = _delta_heads(*dn, DEC_SEQ, DEC_BATCH, N_CTX // DEC_SEQ, state_delta[:, e])
            ob_ctx = _ctx_attention(proj)
            ob_lat = _na_attention(proj, kctx_all[:, e], vctx_all[:, e], ev_rpb[e])
            x, h, rec, cnt = _even_out(oa_ctx, ob_ctx, oa_lat, ob_lat, x, ev_w_out[e], mods, norm2_g[l],
                                       router)
            na_w = NA_HEADS * NA_HD
            ks.append(kv[:N_CTX, :na_w].reshape(BATCH, SEQ, NA_HEADS, NA_HD))
            vs.append(kv[:N_CTX, na_w:].reshape(BATCH, SEQ, NA_HEADS, NA_HD))
            ss.append(s_fin)
        else:
            o = l // 2
            x, h, rec, cnt = _sgu_layer(x, pend, mods, norm1_g[l], od_w_in[o], od_ln_g[o], od_ln_b[o],
                                        od_w_s[o], od_b_s[o], od_w_out[o], norm2_g[l], router)
        pend = _moe(h, rec, cnt, mods, moe_w_gate, moe_w_up, moe_w_down, l)
    y_prompt = _final_norm(x, pend, final_g, 0, N_CTX).reshape(BATCH, SEQ, D_MODEL)
    y_sample = _final_norm(x, pend, final_g, N_CTX, N_LAT).reshape(DEC_BATCH, DEC_SEQ, D_MODEL)
    return (y_prompt, y_sample, jnp.stack(ks, axis=1), jnp.stack(vs, axis=1), jnp.stack(ss, axis=1))
```

```python
import functools

import jax
import jax.numpy as jnp
from jax import lax
from jax.experimental import pallas as pl
from jax.experimental.pallas import tpu as pltpu
from jax.experimental.pallas import tpu_sc as plsc

F32 = jnp.float32
BF16 = jnp.bfloat16

D_MODEL = 1024
BATCH = 16
SEQ = 256
DEPTH = 4
DEC_BATCH = 4
DEC_SEQ = 2048
PAST_LEN = 512
GRID_W = 64
EPS = 1e-6
NEG_INF = -1e30

DN_HEADS = 4
DN_DK = 128
DN_CHUNK = 64
NA_HEADS = 8
NA_HD = 64
NA_ROWS = 8
NA_COLS = 16
SG_CHUNK = 128
SG_GROUPS = 8
SG_W = 2 * D_MODEL
SG_GW = SG_W // SG_GROUPS
N_EGROUPS = 4
EXP_PER_GROUP = 8
N_EXPERTS = 32
D_EXPERT = 512

N_CTX = BATCH * SEQ
N_LAT = DEC_BATCH * DEC_SEQ
N_TOK = N_CTX + N_LAT
N_COND = 8
PROJ_W = 4096
LANES = 128
MOE_BLK = 256
MOE_NBLK = -(-(2 * N_TOK + N_EXPERTS * (MOE_BLK - 1)) // MOE_BLK)
VMEM_LIMIT = 56 * 1024 * 1024

_QA, _KA, _VA, _ZA, _QB, _KB, _VB = 0, 4, 8, 12, 16, 20, 24


def _params(sem):
    return pltpu.CompilerParams(dimension_semantics=sem, vmem_limit_bytes=VMEM_LIMIT)


def _bdot(a, b):
    return jnp.dot(a.astype(BF16), b.astype(BF16), preferred_element_type=F32)


def _bdot_nt(a, b):
    return lax.dot_general(a.astype(BF16), b.astype(BF16), (((1,), (1,)), ((), ())),
                           preferred_element_type=F32)


def _bdot_tn(a, b):
    return lax.dot_general(a.astype(BF16), b.astype(BF16), (((0,), (0,)), ((), ())),
                           preferred_element_type=F32)


def _split2(a):
    p0 = a.astype(BF16)
    return p0, (a - p0.astype(F32)).astype(BF16)


def _dot3(a, b):
    ah = a.astype(BF16)
    al = (a - ah.astype(F32)).astype(BF16)
    bh = b.astype(BF16)
    bl = (b - bh.astype(F32)).astype(BF16)
    return (jnp.dot(ah, bh, preferred_element_type=F32) + jnp.dot(ah, bl, preferred_element_type=F32)
            + jnp.dot(al, bh, preferred_element_type=F32))


def _mask_bf16(m01):
    return jnp.where(m01, 1.0, 0.0).astype(BF16)


def _xdot(m01, a):
    m = _mask_bf16(m01)
    p0, p1 = _split2(a)
    return jnp.dot(m, p0, preferred_element_type=F32) + jnp.dot(m, p1, preferred_element_type=F32)


def _xdot_r(a, m01):
    m = _mask_bf16(m01)
    p0, p1 = _split2(a)
    return jnp.dot(p0, m, preferred_element_type=F32) + jnp.dot(p1, m, preferred_element_type=F32)


def _sigmoid(x):
    return 0.5 * jnp.tanh(0.5 * x) + 0.5


def _silu(x):
    return x * _sigmoid(x)


def _rms(x, g):
    return x * lax.rsqrt(jnp.mean(x * x, axis=-1, keepdims=True) + EPS) * g


def _cond_index(row):
    return jnp.where(row < N_CTX, 0, 1 + (row - N_CTX) // DEC_SEQ)


def _mod_spec(k, tm, tile0=0):
    return pl.BlockSpec((None, None, 1, D_MODEL), lambda i, *_: (_cond_index((i + tile0) * tm), k, 0, 0))


def _ada_kernel(c_ref, w_ref, b_ref, o_ref):
    o_ref[...] = _bdot(_silu(c_ref[...]), w_ref[...]) + b_ref[...]


def _ada_mods(cond, ada_w, ada_b):
    tn = 1536
    out = pl.pallas_call(
        _ada_kernel, grid=(DEPTH, 6 * D_MODEL // tn),
        in_specs=[pl.BlockSpec((N_COND, D_MODEL), lambda l, j: (0, 0)),
                  pl.BlockSpec((None, D_MODEL, tn), lambda l, j: (l, 0, j)),
                  pl.BlockSpec((None, 1, tn), lambda l, j: (l, 0, j))],
        out_specs=pl.BlockSpec((None, N_COND, tn), lambda l, j: (l, 0, j)),
        out_shape=jax.ShapeDtypeStruct((DEPTH, N_COND, 6 * D_MODEL), F32),
        compiler_params=_params(("parallel", "parallel")), name="ada_mods",
    )(cond, ada_w, ada_b.reshape(DEPTH, 1, 6 * D_MODEL))
    return out.reshape(DEPTH, N_COND, 6, 1, D_MODEL)


_EV_TN = 512
_EV_W = 7 * DN_HEADS * LANES
_KV_COL0 = _KB * LANES


def _with_pending(x_ref, pend):
    if not pend:
        return x_ref[...]
    y2_ref, rec_ref, gate_ref = pend
    rec = rec_ref[...]
    tm = rec.shape[0]
    y = (rec[:, _R_W:_R_W + 1] * y2_ref[:tm, :].astype(F32)
         + rec[:, _R_W + 1:_R_W + 2] * y2_ref[tm:, :].astype(F32))
    return x_ref[...] + gate_ref[...] * y


_PEND_TM = 512


def _pending_specs(tm, tile0=0):
    assert tm == _PEND_TM
    return [pl.BlockSpec((2 * tm, D_MODEL), lambda i: (i + tile0, 0)),
            pl.BlockSpec((tm, LANES), lambda i: (i + tile0, 0)), _mod_spec(5, tm, tile0)]


def _even_proj_kernel(x_ref, *rest, n_pend, ctx_tiles):
    pend, (g_ref, sh_ref, sc_ref, w_ref, wab_ref, o_ref, ab_ref, kv_ref) = rest[:n_pend], rest[n_pend:n_pend + 8]
    if n_pend == 1:
        x = jnp.where(pl.program_id(0) < ctx_tiles, x_ref[...], pend[0][...])
    else:
        x = _with_pending(x_ref, pend)
    if pend:
        rest[n_pend + 8][...] = x
    h = (_rms(x, g_ref[...]) * (1.0 + sc_ref[...]) + sh_ref[...]).astype(BF16)
    ab_ref[...] = jnp.dot(h, wab_ref[...], preferred_element_type=F32)
    for j in range(_EV_W // _EV_TN):
        c0 = j * _EV_TN
        y = jnp.dot(h, w_ref[:, c0:c0 + _EV_TN], preferred_element_type=F32)
        for c in range(_EV_TN // LANES):
            o_ref[c0 // LANES + c] = y[:, c * LANES:(c + 1) * LANES].astype(BF16)
        if c0 >= _KV_COL0:
            kv_ref[:, c0 - _KV_COL0:c0 - _KV_COL0 + _EV_TN] = y


def _even_proj(x, pend, mods, g, w_in):
    tm = 512
    ctx_tiles = N_CTX // tm
    rows = pl.BlockSpec((tm, D_MODEL), lambda i: (i, 0))
    if isinstance(x, tuple):
        x, pend = x[0], (x[1],)
        x_specs = [pl.BlockSpec((tm, D_MODEL), lambda i: (jnp.minimum(i, ctx_tiles - 1), 0)),
                   pl.BlockSpec((tm, D_MODEL), lambda i: (jnp.maximum(i - ctx_tiles, 0), 0))]
    else:
        pend = tuple(pend) if pend else ()
        x_specs = [rows] + (_pending_specs(tm) if pend else [])
    n_ab = 4 * DN_HEADS
    ab0 = 4 * DN_HEADS * DN_DK
    w_main = jnp.concatenate([w_in[:, :ab0], w_in[:, ab0 + n_ab:]], axis=1).astype(BF16)
    w_ab = jnp.concatenate([w_in[:, ab0:ab0 + n_ab], jnp.zeros((D_MODEL, LANES - n_ab), F32)],
                           axis=1).astype(BF16)
    held = lambda shape: pl.BlockSpec(shape, lambda i: (0,) * len(shape), pipeline_mode=pl.Buffered(1))
    out_specs = [pl.BlockSpec((_EV_W // LANES, tm, LANES), lambda i: (0, i, 0)),
                 pl.BlockSpec((tm, LANES), lambda i: (i, 0)),
                 pl.BlockSpec((tm, 2 * NA_HEADS * NA_HD), lambda i: (i, 0))]
    out_shape = [jax.ShapeDtypeStruct((_EV_W // LANES, N_TOK, LANES), BF16),
                 jax.ShapeDtypeStruct((N_TOK, LANES), F32),
                 jax.ShapeDtypeStruct((N_TOK, 2 * NA_HEADS * NA_HD), F32)]
    if pend:
        out_specs.append(rows)
        out_shape.append(jax.ShapeDtypeStruct((N_TOK, D_MODEL), F32))
    res = pl.pallas_call(
        functools.partial(_even_proj_kernel, n_pend=len(pend), ctx_tiles=ctx_tiles), grid=(N_TOK // tm,),
        in_specs=x_specs
        + [pl.BlockSpec((1, D_MODEL), lambda i: (0, 0)), _mod_spec(0, tm), _mod_spec(1, tm),
           held((D_MODEL, _EV_W)), held((D_MODEL, LANES))],
        out_specs=out_specs, out_shape=out_shape,
        compiler_params=_params(("parallel",)), name="even_proj",
    )(x, *pend, g.reshape(1, D_MODEL), mods, mods, w_main, w_ab)
    return (*res[:3], res[3] if pend else x)


_CHUNK_SHIFT = DN_CHUNK.bit_length() - 1
_CUM_ROWS = 256
_DN_CHAINS = 16
_DN_SHORT = 256
_SERIES_FINE = 3
_MQ_ROWS = DN_DK + DN_CHUNK


def _dn_kernel(*refs, T, HB, has_s0, want_state):
    it = iter(refs)
    q_ref, k_ref, v_ref, z_ref, ab_ref = (next(it) for _ in range(5))
    cwq_ref, cwk_ref, cwv_ref, alog_ref, dtb_ref, og_ref = (next(it) for _ in range(6))
    s0_ref = next(it) if has_s0 else None
    o_ref = next(it)
    sfin_ref = next(it) if want_state else None
    qc, kc, vc, gsc, bsc, osc, b_s, mq_s = (next(it) for _ in range(8))

    C = DN_CHUNK
    n = T // C
    h0 = pl.program_id(1) * HB

    row = lax.broadcasted_iota(jnp.int32, (T, 1), 0)

    def conv(x_ref, cw_ref, hh):
        x = x_ref[hh].astype(F32)
        cw = cw_ref[:, hh * LANES:(hh + 1) * LANES]
        xp = jnp.where(row == 0, 0.0, pltpu.roll(x, 1, 0))
        xn = jnp.where(row == T - 1, 0.0, pltpu.roll(x, T - 1, 0))
        return _silu(cw[0:1, :] * xp + cw[1:2, :] * x + cw[2:3, :] * xn)

    def l2n(x):
        return x * lax.rsqrt(jnp.sum(x * x, axis=-1, keepdims=True) + EPS)

    ab = ab_ref[...]
    lane = lax.broadcasted_iota(jnp.int32, (1, LANES), 1)
    dtb = jnp.zeros((1, LANES), F32)
    alog = jnp.zeros((1, LANES), F32)
    for d in range(2):
        for hq in range(DN_HEADS):
            dtb = jnp.where(lane == d * DN_HEADS + hq, dtb_ref[d, hq], dtb)
            alog = jnp.where(lane == d * DN_HEADS + hq, alog_ref[d, hq], alog)
    xs = ab + dtb
    g_all = -jnp.exp(alog) * (jnp.maximum(xs, 0.0) + jnp.log1p(jnp.exp(-jnp.abs(xs))))
    beta_all = _sigmoid(ab)

    sel_r = lax.broadcasted_iota(jnp.int32, (LANES, LANES), 0)
    for hh in range(HB):
        qc[hh] = l2n(conv(q_ref, cwq_ref, hh)) * (DN_DK ** -0.5)
        kc[hh] = l2n(conv(k_ref, cwk_ref, hh))
        vc[hh] = conv(v_ref, cwv_ref, hh)
        hd = h0 + hh
        for d in range(2):
            gsc[hh, d] = _xdot_r(g_all, sel_r == d * DN_HEADS + hd)
            bsc[hh, d] = _xdot_r(beta_all, sel_r == 2 * DN_HEADS + d * DN_HEADS + hd)

    pr = lax.broadcasted_iota(jnp.int32, (_CUM_ROWS, _CUM_ROWS), 0)
    pc = lax.broadcasted_iota(jnp.int32, (_CUM_ROWS, _CUM_ROWS), 1)
    same = lax.shift_right_logical(pr, _CHUNK_SHIFT) == lax.shift_right_logical(pc, _CHUNK_SHIFT)
    cum_mask = (jnp.logical_and(same, pc <= pr), jnp.logical_and(same, pc >= pr))

    def cum_body(i, carry):
        sl = pl.ds(pl.multiple_of(i * _CUM_ROWS, _CUM_ROWS), _CUM_ROWS)
        for hh in range(HB):
            for d in range(2):
                gsc[hh, d, sl, :] = _xdot(cum_mask[d], gsc[hh, d, sl, :])
        return carry

    lax.fori_loop(0, T // _CUM_ROWS, cum_body, 0)

    ri = lax.broadcasted_iota(jnp.int32, (C, C), 0)
    ci = lax.broadcasted_iota(jnp.int32, (C, C), 1)
    eye = (ri == ci).astype(F32)

    def prepare(items):
        lows, decays = [], []
        kk, qk = {}, {}
        for hh, d, c, slot in items:
            sl = pl.ds(pl.multiple_of(c * C, C), C)
            gc = gsc[hh, d, sl, :]
            if (hh, slot) not in kk:
                k = kc[hh, sl, :]
                kk[hh, slot] = _bdot_nt(k, k)
                qk[hh, slot] = _bdot_nt(qc[hh, sl, :], k)
            incl = (ci <= ri) if d == 0 else (ci >= ri)
            strict = (ci < ri) if d == 0 else (ci > ri)
            gr = jnp.transpose(gc)[0:1, :C]
            decay = jnp.where(incl, jnp.exp(jnp.where(incl, gc[:, :C] - gr, 0.0)), 0.0)
            lows.append(jnp.where(strict, bsc[hh, d, sl, :C] * kk[hh, slot] * decay, 0.0))
            decays.append(decay)
        ts = [eye - low for low in lows]
        ps = lows
        for step in range(5):
            dot = _dot3 if step < _SERIES_FINE else _bdot
            ps = [dot(p, p) for p in ps]
            ts = [t + dot(t, p) for t, p in zip(ts, ps)]
        for (hh, d, c, slot), t, decay in zip(items, ts, decays):
            sl = pl.ds(pl.multiple_of(c * C, C), C)
            q, k, gc, beta = qc[hh, sl, :], kc[hh, sl, :], gsc[hh, d, sl, :], bsc[hh, d, sl, :]
            eg = jnp.exp(gc)
            uw = _bdot(t, jnp.concatenate([vc[hh, sl, :] * beta, k * beta * eg], axis=-1))
            last = gc[C - 1:C, :] if d == 0 else gc[0:1, :]
            wu = jnp.concatenate([uw[:, LANES:], uw[:, :LANES]], axis=-1).astype(BF16)
            kd = (k * jnp.exp(last - gc)).astype(BF16)
            attn = (qk[hh, slot] * decay).astype(BF16)
            kdwu = lax.dot_general(kd, wu, (((0,), (0,)), ((), ())), preferred_element_type=F32)
            awu = jnp.dot(attn, wu, preferred_element_type=F32)
            mq0 = pl.multiple_of(c * _MQ_ROWS, _MQ_ROWS)
            mq_s[hh, d, pl.ds(mq0, DN_DK), :] = kdwu[:, :LANES].astype(BF16)
            mq_s[hh, d, pl.ds(mq0 + DN_DK, C), :] = (q * eg - awu[:, :LANES]).astype(BF16)
            b_s[hh, d, pl.ds(pl.multiple_of(c * DN_DK, DN_DK), DN_DK), :] = kdwu[:, LANES:]
            osc[hh, d, sl, :] = awu[:, LANES:]

    n_prep = min(n, _DN_CHAINS // 2)
    h_prep = max(1, min(HB, _DN_CHAINS // (2 * n_prep)))

    def prep_body(i, carry):
        for hg in range(0, HB, h_prep):
            prepare([(hh, d, i * n_prep + j, j) for hh in range(hg, hg + h_prep) for j in range(n_prep)
                     for d in range(2)])
        return carry

    lax.fori_loop(0, n // n_prep, prep_body, 0)

    def advance(hh, d, c, S):
        sl = pl.ds(pl.multiple_of(c * C, C), C)
        ms = jnp.dot(mq_s[hh, d, pl.ds(pl.multiple_of(c * _MQ_ROWS, _MQ_ROWS), _MQ_ROWS), :], S.astype(BF16),
                     preferred_element_type=F32)
        osc[hh, d, sl, :] = osc[hh, d, sl, :] + ms[DN_DK:]
        last = gsc[hh, d, pl.ds(c * C + (C - 1 if d == 0 else 0), 1), :]
        return (S * jnp.exp(last) - ms[:DN_DK]
                + b_s[hh, d, pl.ds(pl.multiple_of(c * DN_DK, DN_DK), DN_DK), :])

    def body(i, carry):
        return tuple(advance(hh, d, i if d == 0 else n - 1 - i, carry[2 * hh + d])
                     for hh in range(HB) for d in range(2))

    if has_s0:
        init = tuple(s0_ref[d, hh] for hh in range(HB) for d in range(2))
    else:
        init = tuple(jnp.zeros((DN_DK, LANES), F32) for _ in range(2 * HB))
    fin = lax.fori_loop(0, n, body, init)
    for hh in range(HB):
        if want_state:
            sfin_ref[0, hh] = fin[2 * hh]
            sfin_ref[1, hh] = fin[2 * hh + 1]
        o = osc[hh, 0] + osc[hh, 1]
        o_ref[hh] = (_rms(o, og_ref[...]) * _silu(z_ref[hh].astype(F32))).astype(o_ref.dtype)


def _delta_heads(proj, ab, conv_w, a_log, dt_bias, onorm_g, T, n_seq, row0, s0):
    has_s0 = s0 is not None
    want_state = not has_s0
    hb = DN_HEADS if T <= _DN_SHORT else 2

    def col(cb):
        return pl.BlockSpec((hb, T, LANES), lambda s, h: (cb // hb + h, row0 + s, 0))

    def cw(cb):
        return pl.BlockSpec((3, hb * LANES), lambda s, h: (0, cb // hb + h))

    smem = pl.BlockSpec(memory_space=pltpu.SMEM)
    in_specs = [col(_QA), col(_KA), col(_VA), col(_ZA),
                pl.BlockSpec((T, LANES), lambda s, h: (row0 + s, 0)),
                cw(0), cw(4), cw(8), smem, smem,
                pl.BlockSpec((1, LANES), lambda s, h: (0, 0))]
    args = [proj, proj, proj, proj, ab, conv_w, conv_w, conv_w, a_log, dt_bias,
            onorm_g.reshape(1, LANES)]
    state_spec = pl.BlockSpec((None, 2, hb, DN_DK, LANES), lambda s, h: (s, 0, h, 0, 0))
    if has_s0:
        in_specs.append(state_spec)
        args.append(s0)
    out_shape = [jax.ShapeDtypeStruct((DN_HEADS, n_seq * T, LANES), BF16)]
    out_specs = [pl.BlockSpec((hb, T, LANES), lambda s, h: (h, s, 0))]
    if want_state:
        out_shape.append(jax.ShapeDtypeStruct((n_seq, 2, DN_HEADS, DN_DK, LANES), F32))
        out_specs.append(state_spec)
    res = pl.pallas_call(
        functools.partial(_dn_kernel, T=T, HB=hb, has_s0=has_s0, want_state=want_state),
        grid=(n_seq, DN_HEADS // hb), in_specs=in_specs, out_specs=out_specs, out_shape=out_shape,
        scratch_shapes=[pltpu.VMEM((hb, T, LANES), F32)] * 3
        + [pltpu.VMEM((hb, 2, T, LANES), F32)] * 3
        + [pltpu.VMEM((hb, 2, T // DN_CHUNK * DN_DK, LANES), F32),
           pltpu.VMEM((hb, 2, T // DN_CHUNK * _MQ_ROWS, LANES), BF16)],
        compiler_params=_params(("parallel", "parallel")), name="delta_heads_%d" % T,
    )(*args)
    return res if want_state else (res[0], None)


def _pair_queries(q, first):
    return jnp.concatenate([jnp.where(first, q, 0.0), jnp.where(first, 0.0, q)], axis=0).astype(BF16)


def _ctx_attn_kernel(q_ref, k_ref, v_ref, o_ref):
    first = lax.broadcasted_iota(jnp.int32, (SEQ, LANES), 1) < NA_HD
    qm = _pair_queries(q_ref[...] * (NA_HD ** -0.5), first)
    s = lax.dot_general(k_ref[...], qm, (((1,), (1,)), ((), ())), preferred_element_type=F32)
    e = jnp.exp(s - jnp.max(s, axis=0, keepdims=True))
    den = jnp.sum(e, axis=0, keepdims=True)
    o = lax.dot_general(e.astype(BF16), v_ref[...], (((0,), (0,)), ((), ())), preferred_element_type=F32)
    o = jnp.where(first, o[:SEQ], o[SEQ:])
    den_t = jnp.transpose(jnp.broadcast_to(den, (LANES, 2 * SEQ)))
    o_ref[...] = (o / jnp.where(first, den_t[:SEQ], den_t[SEQ:])).astype(o_ref.dtype)


def _ctx_attention(proj):
    def col(cb):
        return pl.BlockSpec((None, SEQ, LANES), lambda s, p: (cb + p, s, 0))

    return pl.pallas_call(
        _ctx_attn_kernel, grid=(BATCH, NA_HEADS // 2),
        in_specs=[col(_QB), col(_KB), col(_VB)],
        out_specs=pl.BlockSpec((None, SEQ, LANES), lambda s, p: (p, s, 0)),
        out_shape=jax.ShapeDtypeStruct((NA_HEADS // 2, N_CTX, LANES), BF16),
        compiler_params=_params(("parallel", "parallel")), name="ctx_attention",
    )(proj, proj, proj)


_NA_UNROLL = 4


def _na_kernel(q_ref, k_ref, v_ref, kc_ref, vc_ref, bias_ref, o_ref, kcb_scr, vcb_scr):
    rows = DEC_SEQ // GRID_W
    win = NA_ROWS * GRID_W
    scale = NA_HD ** -0.5
    dn_nt = (((1,), (1,)), ((), ()))
    dn_tn = (((0,), (0,)), ((), ()))

    kcb_scr[...] = kc_ref[...].astype(BF16)
    vcb_scr[...] = vc_ref[...].astype(BF16)
    first = lax.broadcasted_iota(jnp.int32, (GRID_W, LANES), 1) < NA_HD

    def body(it, carry):
        rr = [it * _NA_UNROLL + j for j in range(_NA_UNROLL)]
        rss = [jnp.clip(r - NA_ROWS // 2, 0, rows - NA_ROWS) for r in rr]
        qsls = [pl.ds(pl.multiple_of(r * GRID_W, GRID_W), GRID_W) for r in rr]
        wsls = [pl.ds(pl.multiple_of(rs * GRID_W, GRID_W), win) for rs in rss]
        qms, s_wins, s_ctxs = [], [], []
        for r, rs, qsl, wsl in zip(rr, rss, qsls, wsls):
            qm = _pair_queries(q_ref[qsl, :] * scale, first)
            bias = jnp.concatenate([bias_ref[NA_ROWS - 1 - (r - rs) + i] for i in range(NA_ROWS)], axis=0)
            s_wins.append(lax.dot_general(k_ref[wsl, :], qm, dn_nt, preferred_element_type=F32) + bias)
            s_ctxs.append(lax.dot_general(kcb_scr[...], qm, dn_nt, preferred_element_type=F32))
        ms = [jnp.maximum(jnp.max(sw, axis=0, keepdims=True), jnp.max(sc, axis=0, keepdims=True))
              for sw, sc in zip(s_wins, s_ctxs)]
        e_wins = [jnp.exp(sw - m) for sw, m in zip(s_wins, ms)]
        e_ctxs = [jnp.exp(sc - m) for sc, m in zip(s_ctxs, ms)]
        dens = [jnp.sum(ew, axis=0, keepdims=True) + jnp.sum(ec, axis=0, keepdims=True)
                for ew, ec in zip(e_wins, e_ctxs)]
        for qsl, wsl, ew, ec, den in zip(qsls, wsls, e_wins, e_ctxs, dens):
            o = (lax.dot_general(ew.astype(BF16), v_ref[wsl, :], dn_tn, preferred_element_type=F32)
                 + lax.dot_general(ec.astype(BF16), vcb_scr[...], dn_tn, preferred_element_type=F32))
            o = o / jnp.transpose(jnp.broadcast_to(den, (LANES, LANES)))
            o_ref[qsl, :] = jnp.where(first, o[:GRID_W], o[GRID_W:]).astype(o_ref.dtype)
        return carry

    lax.fori_loop(0, rows // _NA_UNROLL, body, 0)


def _na_bias_table(rpb):
    col = jnp.arange(GRID_W)
    cs = jnp.clip(col - NA_COLS // 2, 0, GRID_W - NA_COLS)
    col_ok = (col[None, :] >= cs[:, None]) & (col[None, :] < cs[:, None] + NA_COLS)
    dc = jnp.clip(col[None, :] - col[:, None] + NA_COLS - 1, 0, 2 * NA_COLS - 2)
    onehot = (dc.T[None, :, :] == jnp.arange(2 * NA_COLS - 1)[:, None, None]).astype(F32)
    t = jnp.einsum('hrd,dkq->hrkq', rpb.astype(F32), onehot, precision=lax.Precision.HIGHEST)
    t = jnp.where(col_ok.T[None, None], t, NEG_INF)
    t = t.reshape(NA_HEADS // 2, 2, 2 * NA_ROWS - 1, GRID_W, GRID_W)
    return jnp.concatenate([t[:, 0], t[:, 1]], axis=-1)


def _na_attention(proj, kctx, vctx, rpb):
    blk = N_CTX // DEC_SEQ

    def col(cb):
        return pl.BlockSpec((None, DEC_SEQ, LANES), lambda b, p: (cb + p, blk + b, 0))

    ctx = pl.BlockSpec((None, PAST_LEN, LANES), lambda b, p: (b, 0, p))
    return pl.pallas_call(
        _na_kernel, grid=(DEC_BATCH, NA_HEADS // 2),
        in_specs=[col(_QB), col(_KB), col(_VB), ctx, ctx,
                  pl.BlockSpec((None, 2 * NA_ROWS - 1, GRID_W, 2 * GRID_W), lambda b, p: (p, 0, 0, 0))],
        out_specs=pl.BlockSpec((None, DEC_SEQ, LANES), lambda b, p: (p, b, 0)),
        out_shape=jax.ShapeDtypeStruct((NA_HEADS // 2, N_LAT, LANES), BF16),
        scratch_shapes=[pltpu.VMEM((PAST_LEN, LANES), BF16), pltpu.VMEM((PAST_LEN, LANES), BF16)],
        compiler_params=_params(("parallel", "parallel")), name="na_attention",
    )(proj, proj, proj, kctx, vctx, _na_bias_table(rpb))


_LOGIT0 = N_EGROUPS
_R_E, _R_W, _R_RANK = 0, 2, 4


def _lane_min_where(mask, lane):
    return jnp.min(jnp.where(mask, lane, LANES), axis=-1, keepdims=True)


def _route_rows(lg, carry_ref, tri_ref):
    big = -3.0e38
    lane = lax.broadcasted_iota(jnp.int32, lg.shape, 1)
    is_g = lane < N_EGROUPS
    gmax = jnp.max(jnp.where(is_g, lg, big), axis=-1, keepdims=True)
    gsum = jnp.sum(jnp.where(is_g, jnp.exp(jnp.where(is_g, lg - gmax, 0.0)), 0.0), axis=-1, keepdims=True)
    pg_top = 1.0 / gsum
    g_idx = _lane_min_where(jnp.logical_and(is_g, lg == gmax), lane)
    in_g = jnp.logical_and(lane >= _LOGIT0, lax.shift_right_arithmetic(lane - _LOGIT0, 3) == g_idx)
    in_g = jnp.logical_and(in_g, lane < _LOGIT0 + N_EXPERTS)
    m1 = jnp.max(jnp.where(in_g, lg, big), axis=-1, keepdims=True)
    i1 = _lane_min_where(jnp.logical_and(in_g, lg == m1), lane)
    rest = jnp.logical_and(in_g, lane != i1)
    m2 = jnp.max(jnp.where(rest, lg, big), axis=-1, keepdims=True)
    i2 = _lane_min_where(jnp.logical_and(rest, lg == m2), lane)
    e2 = jnp.exp(m2 - m1)
    w1 = pg_top * (1.0 / (1.0 + e2))
    w2 = pg_top * (e2 / (1.0 + e2))
    hit1 = lane == i1
    hit2 = lane == i2
    picked = jnp.where(jnp.logical_or(hit1, hit2), 1.0, 0.0)
    before = jnp.dot(tri_ref[...], picked.astype(BF16), preferred_element_type=F32) + carry_ref[...]
    r1 = jnp.sum(jnp.where(hit1, before, 0.0), axis=-1, keepdims=True)
    r2 = jnp.sum(jnp.where(hit2, before, 0.0), axis=-1, keepdims=True)
    carry_ref[...] = carry_ref[...] + jnp.sum(picked, axis=0, keepdims=True)
    rec = jnp.zeros(lg.shape, F32)
    for ln, val in ((_R_E, (i1 - _LOGIT0).astype(F32)), (_R_E + 1, (i2 - _LOGIT0).astype(F32)),
                    (_R_W, w1), (_R_W + 1, w2), (_R_RANK, r1), (_R_RANK + 1, r2)):
        rec = jnp.where(lane == ln, val, rec)
    return rec


_PACK_W = D_MODEL // 2


def _pack_rows(hb):
    lo = lax.bitcast_convert_type(hb[:, :_PACK_W].astype(F32), jnp.int32)
    hi = lax.bitcast_convert_type(hb[:, _PACK_W:].astype(F32), jnp.int32)
    return jnp.bitwise_or(jnp.bitwise_and(hi, -65536), lax.shift_right_logical(lo, 16))


def _unpack_rows(w):
    lo = lax.bitcast_convert_type(lax.shift_left(w, 16), F32)
    hi = lax.bitcast_convert_type(jnp.bitwise_and(w, -65536), F32)
    return jnp.concatenate([lo, hi], axis=-1).astype(BF16)


def _moe_input(xnew, first, tail_in, tail_out, tail_scr):
    g2_ref, sc2_ref, sh2_ref, wr_ref, br_ref = tail_in
    x_out, h_out, rec_out, cnt_out = tail_out
    tri_scr, carry_scr = tail_scr

    @pl.when(first)
    def _():
        tm = tri_scr.shape[0]
        r = lax.broadcasted_iota(jnp.int32, (tm, tm), 0)
        c = lax.broadcasted_iota(jnp.int32, (tm, tm), 1)
        tri_scr[...] = jnp.where(c < r, 1.0, 0.0).astype(BF16)
        carry_scr[...] = jnp.zeros(carry_scr.shape, F32)

    x_out[...] = xnew
    h = _rms(xnew, g2_ref[...]) * (1.0 + sc2_ref[...]) + sh2_ref[...]
    hh = h.astype(BF16)
    h_out[...] = _pack_rows(hh)
    lg = jnp.dot(hh, wr_ref[...], preferred_element_type=F32) + br_ref[...]
    rec_out[...] = _route_rows(lg, carry_scr, tri_scr)
    cnt_out[...] = carry_scr[...]


def _even_out_kernel(oac_ref, obc_ref, oal_ref, obl_ref, x_ref, w_ref, gate_ref, *rest, ctx_tiles):
    tail_in, tail_out, (w_scr,), tail_scr = rest[:5], rest[5:9], rest[9:10], rest[10:]
    first = pl.program_id(0) == 0

    @pl.when(first)
    def _():
        w_scr[...] = w_ref[...].astype(BF16)

    is_ctx = pl.program_id(0) < ctx_tiles
    parts = [jnp.where(is_ctx, c_ref[hb], l_ref[hb])
             for c_ref, l_ref in ((oac_ref, oal_ref), (obc_ref, obl_ref)) for hb in range(DN_HEADS)]
    mix = jnp.concatenate(parts, axis=-1)
    out = jnp.dot(mix, w_scr[...], preferred_element_type=F32)
    _moe_input(x_ref[...] + gate_ref[...] * out, first, tail_in, tail_out, tail_scr)


def _tail_specs(tm):
    const = lambda shape: pl.BlockSpec(shape, lambda i: (0,) * len(shape))
    in_specs = [_mod_spec(2, tm), const((1, D_MODEL)), _mod_spec(4, tm), _mod_spec(3, tm),
                const((D_MODEL, LANES)), const((1, LANES))]
    out_specs = [pl.BlockSpec((tm, D_MODEL), lambda i: (i, 0)),
                 pl.BlockSpec((tm, _PACK_W), lambda i: (i, 0)),
                 pl.BlockSpec((tm, LANES), lambda i: (i, 0)),
                 const((1, LANES))]
    out_shape = [jax.ShapeDtypeStruct((N_TOK, D_MODEL), F32),
                 jax.ShapeDtypeStruct((N_TOK, _PACK_W), jnp.int32),
                 jax.ShapeDtypeStruct((N_TOK, LANES), F32),
                 jax.ShapeDtypeStruct((1, LANES), F32)]
    scratch = [pltpu.VMEM((tm, tm), BF16), pltpu.VMEM((1, LANES), F32)]
    return in_specs, out_specs, out_shape, scratch


def _router_weights(w_rg, b_rg, w_re, b_re):
    pad = LANES - N_EGROUPS - N_EXPERTS
    w = jnp.concatenate([w_rg, w_re, jnp.zeros((D_MODEL, pad), F32)], axis=1)
    b = jnp.concatenate([b_rg, b_re, jnp.zeros((pad,), F32)]).reshape(1, LANES)
    return w.astype(BF16), b


def _even_out(oa_ctx, ob_ctx, oa_lat, ob_lat, x, w_out, mods, g2, router):
    tm = 512
    ctx_tiles = N_CTX // tm
    tail_in, out_specs, out_shape, tail_scr = _tail_specs(tm)
    ctxblk = pl.BlockSpec((DN_HEADS, tm, LANES), lambda i: (0, jnp.minimum(i, ctx_tiles - 1), 0))
    latblk = pl.BlockSpec((DN_HEADS, tm, LANES), lambda i: (0, jnp.maximum(i - ctx_tiles, 0), 0))
    return pl.pallas_call(
        functools.partial(_even_out_kernel, ctx_tiles=ctx_tiles), grid=(N_TOK // tm,),
        in_specs=[ctxblk, ctxblk, latblk, latblk, pl.BlockSpec((tm, D_MODEL), lambda i: (i, 0)),
                  pl.BlockSpec((D_MODEL, D_MODEL), lambda i: (0, 0))] + tail_in,
        out_specs=out_specs, out_shape=out_shape,
        scratch_shapes=[pltpu.VMEM((D_MODEL, D_MODEL), BF16)] + tail_scr,
        compiler_params=_params(("arbitrary",)), name="even_out",
    )(oa_ctx, ob_ctx, oa_lat, ob_lat, x, w_out, mods, g2.reshape(1, D_MODEL), mods, mods, *router)


def _gelu_tanh(x):
    c = 0.7978845608028654
    hx = 0.5 * x
    return hx + hx * jnp.tanh(x * (c + (c * 0.044715) * (x * x)))


def _sgu_kernel(x_ref, *rest, tm, n_pend):
    pend, rest = rest[:n_pend], rest[n_pend:]
    g1_ref, sh1_ref, sc1_ref, win_ref, lng_ref, lnb_ref, ws_ref, bst_ref, wout_ref, gate_ref = rest[:10]
    rest = rest[10:]
    tail_in, tail_out, (v_scr, m_scr), tail_scr = rest[:5], rest[5:9], rest[9:11], rest[11:]
    first = pl.program_id(0) == 0
    x = _with_pending(x_ref, pend)
    h = (_rms(x, g1_ref[...]) * (1.0 + sc1_ref[...]) + sh1_ref[...]).astype(BF16)

    v = _gelu_tanh(jnp.dot(h, win_ref[:, SG_W:], preferred_element_type=F32))
    mu = jnp.mean(v, axis=-1, keepdims=True)
    vc = v - mu
    var = jnp.mean(vc * vc, axis=-1, keepdims=True)
    v_scr[...] = (vc * lax.rsqrt(var + EPS) * lng_ref[...] + lnb_ref[...]).astype(BF16)

    for g in range(SG_GROUPS):
        cs = slice(g * SG_GW, (g + 1) * SG_GW)
        u = _gelu_tanh(jnp.dot(h, win_ref[:, cs], preferred_element_type=F32))
        w_sp = ws_ref[g].astype(BF16)
        for c in range(tm // SG_CHUNK):
            rs = slice(c * SG_CHUNK, (c + 1) * SG_CHUNK)
            sp = jnp.dot(w_sp, v_scr[rs, cs], preferred_element_type=F32) + bst_ref[:, g:g + 1]
            m_scr[rs, cs] = (u[rs] * sp).astype(BF16)
    out = jnp.dot(m_scr[...], wout_ref[...], preferred_element_type=F32)
    _moe_input(x + gate_ref[...] * out, first, tail_in, tail_out, tail_scr)


def _sgu_layer(x, pend, mods, g1, w_in, ln_g, ln_b, w_s, b_s, w_out, g2, router):
    tm = 512
    pend = tuple(pend) if pend else ()
    tail_in, out_specs, out_shape, tail_scr = _tail_specs(tm)
    const = lambda shape: pl.BlockSpec(shape, lambda i: (0,) * len(shape))
    held = lambda shape: pl.BlockSpec(shape, lambda i: (0,) * len(shape), pipeline_mode=pl.Buffered(1))
    return pl.pallas_call(
        functools.partial(_sgu_kernel, tm=tm, n_pend=len(pend)), grid=(N_TOK // tm,),
        in_specs=[pl.BlockSpec((tm, D_MODEL), lambda i: (i, 0))] + (_pending_specs(tm) if pend else [])
        + [const((1, D_MODEL)), _mod_spec(0, tm), _mod_spec(1, tm),
                  held((D_MODEL, 2 * SG_W)), const((1, SG_W)), const((1, SG_W)),
                  const((SG_GROUPS, SG_CHUNK, SG_CHUNK)), const((SG_CHUNK, SG_GROUPS)),
                  held((SG_W, D_MODEL))] + tail_in,
        out_specs=out_specs, out_shape=out_shape,
        scratch_shapes=[pltpu.VMEM((tm, SG_W), BF16), pltpu.VMEM((tm, SG_W), BF16)] + tail_scr,
        compiler_params=_params(("arbitrary",)), name="sgu_layer",
    )(x, *pend, g1.reshape(1, D_MODEL), mods, mods, w_in.astype(BF16), ln_g.reshape(1, SG_W), ln_b.reshape(1, SG_W),
      w_s, b_s.T, w_out.astype(BF16), mods, g2.reshape(1, D_MODEL), mods, mods, *router)


def _plan(rec, cnt):
    e_idx = rec[:, _R_E:_R_E + 2].astype(jnp.int32)
    rank = rec[:, _R_RANK:_R_RANK + 2].astype(jnp.int32)
    counts = cnt[0, _LOGIT0:_LOGIT0 + N_EXPERTS].astype(jnp.int32)
    padded = (counts + MOE_BLK - 1) // MOE_BLK * MOE_BLK
    pad_end = jnp.cumsum(padded)
    pad_start = pad_end - padded
    hit = e_idx[:, :, None] == jnp.arange(N_EXPERTS, dtype=jnp.int32)[None, None, :]
    dest = jnp.sum(jnp.where(hit, pad_start[None, None, :], 0), axis=-1) + rank
    blk0 = jnp.arange(MOE_NBLK, dtype=jnp.int32) * MOE_BLK
    blk_e = jnp.minimum(jnp.sum((pad_end[None, :] <= blk0[:, None]).astype(jnp.int32), axis=-1),
                        N_EXPERTS - 1)
    n_used = (pad_end[-1] // MOE_BLK).astype(jnp.int32).reshape(1)
    owns = counts > 0
    slot_of = (jnp.cumsum(owns.astype(jnp.int32)) - 1) % _W_SLOTS
    ids = jnp.arange(N_EXPERTS, dtype=jnp.int32)
    later = jnp.logical_and(owns[None, :], ids[None, :] > ids[:, None])
    next_of = jnp.min(jnp.where(later, ids[None, :], N_EXPERTS), axis=-1)
    next2_of = jnp.concatenate([next_of, jnp.full((1,), N_EXPERTS, jnp.int32)])[next_of]
    ahead = jnp.stack([next_of, next2_of], axis=0)
    ahead = jnp.where(ahead == N_EXPERTS, -1, ahead)
    return dest, blk_e, n_used, slot_of[blk_e], ahead[:, blk_e].reshape(-1)


_W_PARTS = 4
_W_SLOTS = 3


def _expert_kernel(blk_e_ref, n_used_ref, slot_ref, next_ref, x_ref, wg_hbm, wu_hbm, wd_hbm, o_ref,
                   wg_buf, wu_buf, wd_buf, wg_scr, wu_scr, wd_scr, sems, *, layer):
    j = pl.program_id(0)
    e = blk_e_ref[j]
    slot = slot_ref[j]
    fresh = jnp.logical_or(j == 0, e != blk_e_ref[jnp.maximum(j - 1, 0)])
    live = j < n_used_ref[0]

    def copies(expert, s):
        out = []
        for m, (hbm, buf) in enumerate(((wg_hbm, wg_buf), (wu_hbm, wu_buf), (wd_hbm, wd_buf))):
            rows = buf.shape[1] // _W_PARTS
            for part in range(_W_PARTS):
                band = pl.ds(part * rows, rows)
                out.append(pltpu.make_async_copy(hbm.at[layer, expert, band], buf.at[s, band],
                                                 sems.at[s, m, part]))
        return out

    def start_if_any(expert, s):
        @pl.when(expert >= 0)
        def _():
            for cp in copies(expert, s):
                cp.start()

    @pl.when(j == 0)
    def _():
        for cp in copies(e, slot):
            cp.start()
        start_if_any(next_ref[j], lax.rem(slot + 1, _W_SLOTS))

    @pl.when(jnp.logical_and(fresh, live))
    def _():
        for cp in copies(e, slot):
            cp.wait()
        start_if_any(next_ref[MOE_NBLK + j], lax.rem(slot + 2, _W_SLOTS))

        wg_scr[...] = wg_buf[slot].astype(BF16)
        wu_scr[...] = wu_buf[slot].astype(BF16)
        wd_scr[...] = wd_buf[slot].astype(BF16)

    @pl.when(live)
    def _():
        x = _unpack_rows(x_ref[...])
        gt = jnp.dot(x, wg_scr[...], preferred_element_type=F32)
        up = jnp.dot(x, wu_scr[...], preferred_element_type=F32)
        hb = (_silu(gt) * up).astype(BF16)
        o_ref[...] = jnp.dot(hb, wd_scr[...], preferred_element_type=F32).astype(o_ref.dtype)

    @pl.when(jnp.logical_not(live))
    def _():
        o_ref[...] = jnp.zeros(o_ref.shape, o_ref.dtype)


def _experts(x_pad, blk_e, n_used, slot, nxt, w_gate, w_up, w_down, layer):
    hbm = pl.BlockSpec(memory_space=pl.ANY)
    grid_spec = pltpu.PrefetchScalarGridSpec(
        num_scalar_prefetch=4, grid=(MOE_NBLK,),
        in_specs=[pl.BlockSpec((MOE_BLK, _PACK_W), lambda j, be, nu, *_: (jnp.minimum(j, nu[0] - 1), 0)),
                  hbm, hbm, hbm],
        out_specs=pl.BlockSpec((MOE_BLK, D_MODEL), lambda j, *_: (j, 0)),
        scratch_shapes=[pltpu.VMEM((_W_SLOTS, D_MODEL, D_EXPERT), F32),
                        pltpu.VMEM((_W_SLOTS, D_MODEL, D_EXPERT), F32),
                        pltpu.VMEM((_W_SLOTS, D_EXPERT, D_MODEL), F32),
                        pltpu.VMEM((D_MODEL, D_EXPERT), BF16), pltpu.VMEM((D_MODEL, D_EXPERT), BF16),
                        pltpu.VMEM((D_EXPERT, D_MODEL), BF16),
                        pltpu.SemaphoreType.DMA((_W_SLOTS, 3, _W_PARTS))])
    return pl.pallas_call(
        functools.partial(_expert_kernel, layer=layer), grid_spec=grid_spec,
        out_shape=jax.ShapeDtypeStruct((MOE_NBLK * MOE_BLK, D_MODEL), BF16),
        compiler_params=_params(("arbitrary",)), name="experts",
    )(blk_e, n_used, slot, nxt, x_pad, w_gate, w_up, w_down)


def _final_kernel(x_ref, y2_ref, rec_ref, gate_ref, fg_ref, o_ref):
    o_ref[...] = _rms(_with_pending(x_ref, (y2_ref, rec_ref, gate_ref)), fg_ref[...])


def _final_norm(x, pend, final_g, row0, n_rows):
    tm = 512
    tile0 = row0 // tm
    return pl.pallas_call(
        _final_kernel, grid=(n_rows // tm,),
        in_specs=[pl.BlockSpec((tm, D_MODEL), lambda i: (i + tile0, 0))] + _pending_specs(tm, tile0)
        + [pl.BlockSpec((1, D_MODEL), lambda i: (0, 0))],
        out_specs=pl.BlockSpec((tm, D_MODEL), lambda i: (i, 0)),
        out_shape=jax.ShapeDtypeStruct((n_rows, D_MODEL), F32),
        compiler_params=_params(("parallel",)), name="final_norm",
    )(x, *pend, final_g.reshape(1, D_MODEL))


_SC_WORKERS = 32
_SC_CORES = 2
_SC_ROWS = 64


def _dispatch_rows(hp, dest):
    n, width = hp.shape
    per_w = n // _SC_WORKERS
    n_ch = per_w // _SC_ROWS
    idx = dest.T.reshape(2, _SC_WORKERS, n_ch, _SC_ROWS)
    mesh = plsc.VectorSubcoreMesh(core_axis_name="c", subcore_axis_name="s")

    @functools.partial(
        pl.kernel, mesh=mesh, out_type=jax.ShapeDtypeStruct((MOE_NBLK * MOE_BLK, width), hp.dtype),
        scratch_types=[pltpu.VMEM((n_ch, _SC_ROWS), jnp.int32), pltpu.VMEM((n_ch, _SC_ROWS), jnp.int32),
                       pltpu.VMEM((_SC_ROWS, width), hp.dtype)], name="dispatch_rows")
    def scatter(h_hbm, idx_hbm, out_hbm, i0_v, i1_v, rows_v):
        wid = lax.axis_index("s") * _SC_CORES + lax.axis_index("c")
        pltpu.sync_copy(idx_hbm.at[0, wid], i0_v)
        pltpu.sync_copy(idx_hbm.at[1, wid], i1_v)

        @pl.loop(0, n_ch)
        def _(g):
            pltpu.sync_copy(h_hbm.at[pl.ds(wid * per_w + g * _SC_ROWS, _SC_ROWS)], rows_v)
            pltpu.sync_copy(rows_v, out_hbm.at[i0_v.at[g]])
            pltpu.sync_copy(rows_v, out_hbm.at[i1_v.at[g]])

    return scatter(hp, idx)


def _moe(h, rec, cnt, mods, w_gate, w_up, w_down, layer):
    dest, blk_e, n_used, slot, nxt = _plan(rec, cnt)
    y_pad = _experts(_dispatch_rows(h, dest), blk_e, n_used, slot, nxt, w_gate, w_up, w_down, layer)
    order = dest.reshape(N_TOK // _PEND_TM, _PEND_TM, 2).transpose(0, 2, 1).reshape(-1)
    return y_pad[order], rec, mods


def kernel(x_prompt, x_sample, c, cache_k, cache_v, state_delta, c_ctx, ada_w, ada_b, norm1_g, norm2_g, final_g,
           ev_w_in, ev_w_out, ev_conv_w, ev_a_log, ev_dt_bias, ev_onorm_g, ev_rpb, od_w_in, od_ln_g, od_ln_b,
           od_w_s, od_b_s, od_w_out, moe_w_rg, moe_b_rg, moe_w_re, moe_b_re, moe_w_gate, moe_w_up, moe_w_down):
    x = (x_prompt.reshape(N_CTX, D_MODEL), x_sample.reshape(N_LAT, D_MODEL))
    cond = jnp.concatenate([c_ctx[None, :], c, jnp.zeros((N_COND - 1 - DEC_BATCH, D_MODEL), F32)], axis=0)
    mods_all = _ada_mods(cond, ada_w, ada_b)
    kctx_all = cache_k.reshape(DEC_BATCH, -1, PAST_LEN, NA_HEADS * NA_HD)
    vctx_all = cache_v.reshape(DEC_BATCH, -1, PAST_LEN, NA_HEADS * NA_HD)

    ks, vs, ss = [], [], []
    pend = None
    for l in range(DEPTH):
        mods = mods_all[l]
        router = _router_weights(moe_w_rg[l], moe_b_rg[l], moe_w_re[l], moe_b_re[l])
        if l % 2 == 0:
            e = l // 2
            proj, ab, kv, x = _even_proj(x, pend, mods, norm1_g[l], ev_w_in[e])
            dn = (proj, ab, ev_conv_w[e], ev_a_log[e], ev_dt_bias[e], ev_onorm_g[e])
            oa_ctx, s_fin = _delta_heads(*dn, SEQ, BATCH, 0, None)
            oa_lat, _ = _delta_heads(*dn, DEC_SEQ, DEC_BATCH, N_CTX // DEC_SEQ, state_delta[:, e])
            ob_ctx = _ctx_attention(proj)
            ob_lat = _na_attention(proj, kctx_all[:, e], vctx_all[:, e], ev_rpb[e])
            x, h, rec, cnt = _even_out(oa_ctx, ob_ctx, oa_lat, ob_lat, x, ev_w_out[e], mods, norm2_g[l],
                                       router)
            na_w = NA_HEADS * NA_HD
            ks.append(kv[:N_CTX, :na_w].reshape(BATCH, SEQ, NA_HEADS, NA_HD))
            vs.append(kv[:N_CTX, na_w:].reshape(BATCH, SEQ, NA_HEADS, NA_HD))
            ss.append(s_fin)
        else:
            o = l // 2
            x, h, rec, cnt = _sgu_layer(x, pend, mods, norm1_g[l], od_w_in[o], od_ln_g[o], od_ln_b[o],
                                        od_w_s[o], od_b_s[o], od_w_out[o], norm2_g[l], router)
        pend = _moe(h, rec, cnt, mods, moe_w_gate, moe_w_up, moe_w_down, l)
    y_prompt = _final_norm(x, pend, final_g, 0, N_CTX).reshape(BATCH, SEQ, D_MODEL)
    y_sample = _final_norm(x, pend, final_g, N_CTX, N_LAT).reshape(DEC_BATCH, DEC_SEQ, D_MODEL)
    return (y_prompt, y_sample, jnp.stack(ks, axis=1), jnp.stack(vs, axis=1), jnp.stack(ss, axis=1))
```

```python
import functools

import jax
import jax.numpy as jnp
from jax import lax
from jax.experimental import pallas as pl
from jax.experimental.pallas import tpu as pltpu
from jax.experimental.pallas import tpu_sc as plsc

F32 = jnp.float32
BF16 = jnp.bfloat16

D_MODEL = 1024
BATCH = 16
SEQ = 256
DEPTH = 4
DEC_BATCH = 4
DEC_SEQ = 2048
PAST_LEN = 512
GRID_W = 64
EPS = 1e-6
NEG_INF = -1e30

DN_HEADS = 4
DN_DK = 128
DN_CHUNK = 64
NA_HEADS = 8
NA_HD = 64
NA_ROWS = 8
NA_COLS = 16
SG_CHUNK = 128
SG_GROUPS = 8
SG_W = 2 * D_MODEL
SG_GW = SG_W // SG_GROUPS
N_EGROUPS = 4
EXP_PER_GROUP = 8
N_EXPERTS = 32
D_EXPERT = 512

N_CTX = BATCH * SEQ
N_LAT = DEC_BATCH * DEC_SEQ
N_TOK = N_CTX + N_LAT
N_COND = 8
PROJ_W = 4096
LANES = 128
MOE_BLK = 256
MOE_NBLK = -(-(2 * N_TOK + N_EXPERTS * (MOE_BLK - 1)) // MOE_BLK)
VMEM_LIMIT = 56 * 1024 * 1024

_QA, _KA, _VA, _ZA, _QB, _KB, _VB = 0, 4, 8, 12, 16, 20, 24


def _params(sem):
    return pltpu.CompilerParams(dimension_semantics=sem, vmem_limit_bytes=VMEM_LIMIT)


def _bdot(a, b):
    return jnp.dot(a.astype(BF16), b.astype(BF16), preferred_element_type=F32)


def _bdot_nt(a, b):
    return lax.dot_general(a.astype(BF16), b.astype(BF16), (((1,), (1,)), ((), ())),
                           preferred_element_type=F32)


def _bdot_tn(a, b):
    return lax.dot_general(a.astype(BF16), b.astype(BF16), (((0,), (0,)), ((), ())),
                           preferred_element_type=F32)


def _split2(a):
    p0 = a.astype(BF16)
    return p0, (a - p0.astype(F32)).astype(BF16)


def _dot3(a, b):
    ah = a.astype(BF16)
    al = (a - ah.astype(F32)).astype(BF16)
    bh = b.astype(BF16)
    bl = (b - bh.astype(F32)).astype(BF16)
    return (jnp.dot(ah, bh, preferred_element_type=F32) + jnp.dot(ah, bl, preferred_element_type=F32)
            + jnp.dot(al, bh, preferred_element_type=F32))


def _mask_bf16(m01):
    return jnp.where(m01, 1.0, 0.0).astype(BF16)


def _xdot(m01, a):
    m = _mask_bf16(m01)
    p0, p1 = _split2(a)
    return jnp.dot(m, p0, preferred_element_type=F32) + jnp.dot(m, p1, preferred_element_type=F32)


def _xdot_r(a, m01):
    m = _mask_bf16(m01)
    p0, p1 = _split2(a)
    return jnp.dot(p0, m, preferred_element_type=F32) + jnp.dot(p1, m, preferred_element_type=F32)


def _sigmoid(x):
    return 0.5 * jnp.tanh(0.5 * x) + 0.5


def _silu(x):
    return x * _sigmoid(x)


def _rms(x, g):
    return x * lax.rsqrt(jnp.mean(x * x, axis=-1, keepdims=True) + EPS) * g


def _cond_index(row):
    return jnp.where(row < N_CTX, 0, 1 + (row - N_CTX) // DEC_SEQ)


def _mod_spec(k, tm, tile0=0):
    return pl.BlockSpec((None, None, 1, D_MODEL), lambda i, *_: (_cond_index((i + tile0) * tm), k, 0, 0))


def _ada_kernel(c_ref, w_ref, b_ref, o_ref):
    o_ref[...] = _bdot(_silu(c_ref[...]), w_ref[...]) + b_ref[...]


def _ada_mods(cond, ada_w, ada_b):
    tn = 1536
    out = pl.pallas_call(
        _ada_kernel, grid=(DEPTH, 6 * D_MODEL // tn),
        in_specs=[pl.BlockSpec((N_COND, D_MODEL), lambda l, j: (0, 0)),
                  pl.BlockSpec((None, D_MODEL, tn), lambda l, j: (l, 0, j)),
                  pl.BlockSpec((None, 1, tn), lambda l, j: (l, 0, j))],
        out_specs=pl.BlockSpec((None, N_COND, tn), lambda l, j: (l, 0, j)),
        out_shape=jax.ShapeDtypeStruct((DEPTH, N_COND, 6 * D_MODEL), F32),
        compiler_params=_params(("parallel", "parallel")), name="ada_mods",
    )(cond, ada_w, ada_b.reshape(DEPTH, 1, 6 * D_MODEL))
    return out.reshape(DEPTH, N_COND, 6, 1, D_MODEL)


_EV_TN = 512
_EV_W = 7 * DN_HEADS * LANES
_KV_COL0 = _KB * LANES


def _with_pending(x_ref, pend):
    if not pend:
        return x_ref[...]
    y2_ref, rec_ref, gate_ref = pend
    rec = rec_ref[...]
    tm = rec.shape[0]
    y = (rec[:, _R_W:_R_W + 1] * y2_ref[:tm, :].astype(F32)
         + rec[:, _R_W + 1:_R_W + 2] * y2_ref[tm:, :].astype(F32))
    return x_ref[...] + gate_ref[...] * y


_PEND_TM = 512


def _pending_specs(tm, tile0=0):
    assert tm == _PEND_TM
    return [pl.BlockSpec((2 * tm, D_MODEL), lambda i: (i + tile0, 0)),
            pl.BlockSpec((tm, LANES), lambda i: (i + tile0, 0)), _mod_spec(5, tm, tile0)]


def _even_proj_kernel(x_ref, *rest, n_pend, ctx_tiles):
    pend, (g_ref, sh_ref, sc_ref, w_ref, wab_ref, o_ref, ab_ref, kv_ref) = rest[:n_pend], rest[n_pend:n_pend + 8]
    if n_pend == 1:
        x = jnp.where(pl.program_id(0) < ctx_tiles, x_ref[...], pend[0][...])
    else:
        x = _with_pending(x_ref, pend)
    if pend:
        rest[n_pend + 8][...] = x
    h = (_rms(x, g_ref[...]) * (1.0 + sc_ref[...]) + sh_ref[...]).astype(BF16)
    ab_ref[...] = jnp.dot(h, wab_ref[...], preferred_element_type=F32)
    for j in range(_EV_W // _EV_TN):
        c0 = j * _EV_TN
        y = jnp.dot(h, w_ref[:, c0:c0 + _EV_TN], preferred_element_type=F32)
        for c in range(_EV_TN // LANES):
            o_ref[c0 // LANES + c] = y[:, c * LANES:(c + 1) * LANES].astype(BF16)
        if c0 >= _KV_COL0:
            kv_ref[:, c0 - _KV_COL0:c0 - _KV_COL0 + _EV_TN] = y


def _even_proj(x, pend, mods, g, w_in):
    tm = 512
    ctx_tiles = N_CTX // tm
    rows = pl.BlockSpec((tm, D_MODEL), lambda i: (i, 0))
    if isinstance(x, tuple):
        x, pend = x[0], (x[1],)
        x_specs = [pl.BlockSpec((tm, D_MODEL), lambda i: (jnp.minimum(i, ctx_tiles - 1), 0)),
                   pl.BlockSpec((tm, D_MODEL), lambda i: (jnp.maximum(i - ctx_tiles, 0), 0))]
    else:
        pend = tuple(pend) if pend else ()
        x_specs = [rows] + (_pending_specs(tm) if pend else [])
    n_ab = 4 * DN_HEADS
    ab0 = 4 * DN_HEADS * DN_DK
    w_main = jnp.concatenate([w_in[:, :ab0], w_in[:, ab0 + n_ab:]], axis=1).astype(BF16)
    w_ab = jnp.concatenate([w_in[:, ab0:ab0 + n_ab], jnp.zeros((D_MODEL, LANES - n_ab), F32)],
                           axis=1).astype(BF16)
    held = lambda shape: pl.BlockSpec(shape, lambda i: (0,) * len(shape), pipeline_mode=pl.Buffered(1))
    out_specs = [pl.BlockSpec((_EV_W // LANES, tm, LANES), lambda i: (0, i, 0)),
                 pl.BlockSpec((tm, LANES), lambda i: (i, 0)),
                 pl.BlockSpec((tm, 2 * NA_HEADS * NA_HD), lambda i: (i, 0))]
    out_shape = [jax.ShapeDtypeStruct((_EV_W // LANES, N_TOK, LANES), BF16),
                 jax.ShapeDtypeStruct((N_TOK, LANES), F32),
                 jax.ShapeDtypeStruct((N_TOK, 2 * NA_HEADS * NA_HD), F32)]
    if pend:
        out_specs.append(rows)
        out_shape.append(jax.ShapeDtypeStruct((N_TOK, D_MODEL), F32))
    res = pl.pallas_call(
        functools.partial(_even_proj_kernel, n_pend=len(pend), ctx_tiles=ctx_tiles), grid=(N_TOK // tm,),
        in_specs=x_specs
        + [pl.BlockSpec((1, D_MODEL), lambda i: (0, 0)), _mod_spec(0, tm), _mod_spec(1, tm),
           held((D_MODEL, _EV_W)), held((D_MODEL, LANES))],
        out_specs=out_specs, out_shape=out_shape,
        compiler_params=_params(("parallel",)), name="even_proj",
    )(x, *pend, g.reshape(1, D_MODEL), mods, mods, w_main, w_ab)
    return (*res[:3], res[3] if pend else x)


_CHUNK_SHIFT = DN_CHUNK.bit_length() - 1
_CUM_ROWS = 256
_DN_CHAINS = 16
_DN_SHORT = 256
_MQ_ROWS = DN_DK + DN_CHUNK


def _dn_kernel(*refs, T, HB, has_s0, want_state):
    it = iter(refs)
    q_ref, k_ref, v_ref, z_ref, ab_ref = (next(it) for _ in range(5))
    cwq_ref, cwk_ref, cwv_ref, alog_ref, dtb_ref, og_ref = (next(it) for _ in range(6))
    s0_ref = next(it) if has_s0 else None
    o_ref = next(it)
    sfin_ref = next(it) if want_state else None
    qc, kc, vc, gsc, bsc, osc, b_s, mq_s = (next(it) for _ in range(8))

    C = DN_CHUNK
    n = T // C
    h0 = pl.program_id(1) * HB

    row = lax.broadcasted_iota(jnp.int32, (T, 1), 0)

    def conv(x_ref, cw_ref, hh):
        x = x_ref[hh].astype(F32)
        cw = cw_ref[:, hh * LANES:(hh + 1) * LANES]
        xp = jnp.where(row == 0, 0.0, pltpu.roll(x, 1, 0))
        xn = jnp.where(row == T - 1, 0.0, pltpu.roll(x, T - 1, 0))
        return _silu(cw[0:1, :] * xp + cw[1:2, :] * x + cw[2:3, :] * xn)

    def l2n(x):
        return x * lax.rsqrt(jnp.sum(x * x, axis=-1, keepdims=True) + EPS)

    ab = ab_ref[...]
    lane = lax.broadcasted_iota(jnp.int32, (1, LANES), 1)
    dtb = jnp.zeros((1, LANES), F32)
    alog = jnp.zeros((1, LANES), F32)
    for d in range(2):
        for hq in range(DN_HEADS):
            dtb = jnp.where(lane == d * DN_HEADS + hq, dtb_ref[d, hq], dtb)
            alog = jnp.where(lane == d * DN_HEADS + hq, alog_ref[d, hq], alog)
    xs = ab + dtb
    g_all = -jnp.exp(alog) * (jnp.maximum(xs, 0.0) + jnp.log1p(jnp.exp(-jnp.abs(xs))))
    beta_all = _sigmoid(ab)

    sel_r = lax.broadcasted_iota(jnp.int32, (LANES, LANES), 0)
    for hh in range(HB):
        qc[hh] = l2n(conv(q_ref, cwq_ref, hh)) * (DN_DK ** -0.5)
        kc[hh] = l2n(conv(k_ref, cwk_ref, hh))
        vc[hh] = conv(v_ref, cwv_ref, hh)
        hd = h0 + hh
        for d in range(2):
            gsc[hh, d] = _xdot_r(g_all, sel_r == d * DN_HEADS + hd)
            bsc[hh, d] = _xdot_r(beta_all, sel_r == 2 * DN_HEADS + d * DN_HEADS + hd)

    pr = lax.broadcasted_iota(jnp.int32, (_CUM_ROWS, _CUM_ROWS), 0)
    pc = lax.broadcasted_iota(jnp.int32, (_CUM_ROWS, _CUM_ROWS), 1)
    same = lax.shift_right_logical(pr, _CHUNK_SHIFT) == lax.shift_right_logical(pc, _CHUNK_SHIFT)
    cum_mask = (jnp.logical_and(same, pc <= pr), jnp.logical_and(same, pc >= pr))

    def cum_body(i, carry):
        sl = pl.ds(pl.multiple_of(i * _CUM_ROWS, _CUM_ROWS), _CUM_ROWS)
        for hh in range(HB):
            for d in range(2):
                gsc[hh, d, sl, :] = _xdot(cum_mask[d], gsc[hh, d, sl, :])
        return carry

    lax.fori_loop(0, T // _CUM_ROWS, cum_body, 0)

    ri = lax.broadcasted_iota(jnp.int32, (C, C), 0)
    ci = lax.broadcasted_iota(jnp.int32, (C, C), 1)
    eye = (ri == ci).astype(F32)

    def prepare(items):
        lows, decays = [], []
        kk, qk = {}, {}
        for hh, d, c, slot in items:
            sl = pl.ds(pl.multiple_of(c * C, C), C)
            gc = gsc[hh, d, sl, :]
            if (hh, slot) not in kk:
                k = kc[hh, sl, :]
                kk[hh, slot] = _bdot_nt(k, k)
                qk[hh, slot] = _bdot_nt(qc[hh, sl, :], k)
            incl = (ci <= ri) if d == 0 else (ci >= ri)
            strict = (ci < ri) if d == 0 else (ci > ri)
            gr = jnp.transpose(gc)[0:1, :C]
            decay = jnp.where(incl, jnp.exp(jnp.where(incl, gc[:, :C] - gr, 0.0)), 0.0)
            lows.append(jnp.where(strict, bsc[hh, d, sl, :C] * kk[hh, slot] * decay, 0.0))
            decays.append(decay)
        ts = [eye - low for low in lows]
        ps = lows
        for step in range(_CHUNK_SHIFT - 1):
            ps = [_dot3(p, p) for p in ps]
            ts = [t + _dot3(t, p) for t, p in zip(ts, ps)]
        for (hh, d, c, slot), t, decay in zip(items, ts, decays):
            sl = pl.ds(pl.multiple_of(c * C, C), C)
            q, k, gc, beta = qc[hh, sl, :], kc[hh, sl, :], gsc[hh, d, sl, :], bsc[hh, d, sl, :]
            eg = jnp.exp(gc)
            uw = _bdot(t, jnp.concatenate([vc[hh, sl, :] * beta, k * beta * eg], axis=-1))
            last = gc[C - 1:C, :] if d == 0 else gc[0:1, :]
            wu = jnp.concatenate([uw[:, LANES:], uw[:, :LANES]], axis=-1).astype(BF16)
            kd = (k * jnp.exp(last - gc)).astype(BF16)
            attn = (qk[hh, slot] * decay).astype(BF16)
            kdwu = lax.dot_general(kd, wu, (((0,), (0,)), ((), ())), preferred_element_type=F32)
            awu = jnp.dot(attn, wu, preferred_element_type=F32)
            mq0 = pl.multiple_of(c * _MQ_ROWS, _MQ_ROWS)
            mq_s[hh, d, pl.ds(mq0, DN_DK), :] = kdwu[:, :LANES].astype(BF16)
            mq_s[hh, d, pl.ds(mq0 + DN_DK, C), :] = (q * eg - awu[:, :LANES]).astype(BF16)
            b_s[hh, d, pl.ds(pl.multiple_of(c * DN_DK, DN_DK), DN_DK), :] = kdwu[:, LANES:]
            osc[hh, d, sl, :] = awu[:, LANES:]

    n_prep = min(n, _DN_CHAINS // 2)
    h_prep = max(1, min(HB, _DN_CHAINS // (2 * n_prep)))

    def prep_body(i, carry):
        for hg in range(0, HB, h_prep):
            prepare([(hh, d, i * n_prep + j, j) for hh in range(hg, hg + h_prep) for j in range(n_prep)
                     for d in range(2)])
        return carry

    lax.fori_loop(0, n // n_prep, prep_body, 0)

    def advance(hh, d, c, S):
        sl = pl.ds(pl.multiple_of(c * C, C), C)
        ms = jnp.dot(mq_s[hh, d, pl.ds(pl.multiple_of(c * _MQ_ROWS, _MQ_ROWS), _MQ_ROWS), :], S.astype(BF16),
                     preferred_element_type=F32)
        osc[hh, d, sl, :] = osc[hh, d, sl, :] + ms[DN_DK:]
        last = gsc[hh, d, pl.ds(c * C + (C - 1 if d == 0 else 0), 1), :]
        return (S * jnp.exp(last) - ms[:DN_DK]
                + b_s[hh, d, pl.ds(pl.multiple_of(c * DN_DK, DN_DK), DN_DK), :])

    def body(i, carry):
        return tuple(advance(hh, d, i if d == 0 else n - 1 - i, carry[2 * hh + d])
                     for hh in range(HB) for d in range(2))

    if has_s0:
        init = tuple(s0_ref[d, hh] for hh in range(HB) for d in range(2))
    else:
        init = tuple(jnp.zeros((DN_DK, LANES), F32) for _ in range(2 * HB))
    fin = lax.fori_loop(0, n, body, init)
    for hh in range(HB):
        if want_state:
            sfin_ref[0, hh] = fin[2 * hh]
            sfin_ref[1, hh] = fin[2 * hh + 1]
        o = osc[hh, 0] + osc[hh, 1]
        o_ref[hh] = (_rms(o, og_ref[...]) * _silu(z_ref[hh].astype(F32))).astype(o_ref.dtype)


def _delta_heads(proj, ab, conv_w, a_log, dt_bias, onorm_g, T, n_seq, row0, s0):
    has_s0 = s0 is not None
    want_state = not has_s0
    hb = DN_HEADS if T <= _DN_SHORT else 2

    def col(cb):
        return pl.BlockSpec((hb, T, LANES), lambda s, h: (cb // hb + h, row0 + s, 0))

    def cw(cb):
        return pl.BlockSpec((3, hb * LANES), lambda s, h: (0, cb // hb + h))

    smem = pl.BlockSpec(memory_space=pltpu.SMEM)
    in_specs = [col(_QA), col(_KA), col(_VA), col(_ZA),
                pl.BlockSpec((T, LANES), lambda s, h: (row0 + s, 0)),
                cw(0), cw(4), cw(8), smem, smem,
                pl.BlockSpec((1, LANES), lambda s, h: (0, 0))]
    args = [proj, proj, proj, proj, ab, conv_w, conv_w, conv_w, a_log, dt_bias,
            onorm_g.reshape(1, LANES)]
    state_spec = pl.BlockSpec((None, 2, hb, DN_DK, LANES), lambda s, h: (s, 0, h, 0, 0))
    if has_s0:
        states, layer = s0
        in_specs.append(pl.BlockSpec((None, None, 2, hb, DN_DK, LANES), lambda s, h: (s, layer, 0, h, 0, 0)))
        args.append(states)
    out_shape = [jax.ShapeDtypeStruct((DN_HEADS, n_seq * T, LANES), BF16)]
    out_specs = [pl.BlockSpec((hb, T, LANES), lambda s, h: (h, s, 0))]
    if want_state:
        out_shape.append(jax.ShapeDtypeStruct((n_seq, 2, DN_HEADS, DN_DK, LANES), F32))
        out_specs.append(state_spec)
    res = pl.pallas_call(
        functools.partial(_dn_kernel, T=T, HB=hb, has_s0=has_s0, want_state=want_state),
        grid=(n_seq, DN_HEADS // hb), in_specs=in_specs, out_specs=out_specs, out_shape=out_shape,
        scratch_shapes=[pltpu.VMEM((hb, T, LANES), F32)] * 3
        + [pltpu.VMEM((hb, 2, T, LANES), F32)] * 3
        + [pltpu.VMEM((hb, 2, T // DN_CHUNK * DN_DK, LANES), F32),
           pltpu.VMEM((hb, 2, T // DN_CHUNK * _MQ_ROWS, LANES), BF16)],
        compiler_params=_params(("parallel", "parallel")), name="delta_heads_%d" % T,
    )(*args)
    return res if want_state else (res[0], None)


def _pair_queries(q, first):
    return jnp.concatenate([jnp.where(first, q, 0.0), jnp.where(first, 0.0, q)], axis=0).astype(BF16)


def _ctx_attn_kernel(q_ref, k_ref, v_ref, o_ref):
    first = lax.broadcasted_iota(jnp.int32, (SEQ, LANES), 1) < NA_HD
    qm = _pair_queries(q_ref[...] * (NA_HD ** -0.5), first)
    s = lax.dot_general(k_ref[...], qm, (((1,), (1,)), ((), ())), preferred_element_type=F32)
    e = jnp.exp(s - jnp.max(s, axis=0, keepdims=True))
    den = jnp.sum(e, axis=0, keepdims=True)
    o = lax.dot_general(e.astype(BF16), v_ref[...], (((0,), (0,)), ((), ())), preferred_element_type=F32)
    o = jnp.where(first, o[:SEQ], o[SEQ:])
    den_t = jnp.transpose(jnp.broadcast_to(den, (LANES, 2 * SEQ)))
    o_ref[...] = (o / jnp.where(first, den_t[:SEQ], den_t[SEQ:])).astype(o_ref.dtype)


def _ctx_attention(proj):
    def col(cb):
        return pl.BlockSpec((None, SEQ, LANES), lambda s, p: (cb + p, s, 0))

    return pl.pallas_call(
        _ctx_attn_kernel, grid=(BATCH, NA_HEADS // 2),
        in_specs=[col(_QB), col(_KB), col(_VB)],
        out_specs=pl.BlockSpec((None, SEQ, LANES), lambda s, p: (p, s, 0)),
        out_shape=jax.ShapeDtypeStruct((NA_HEADS // 2, N_CTX, LANES), BF16),
        compiler_params=_params(("parallel", "parallel")), name="ctx_attention",
    )(proj, proj, proj)


_NA_UNROLL = 4


def _na_kernel(q_ref, k_ref, v_ref, kc_ref, vc_ref, bias_ref, o_ref, kcb_scr, vcb_scr):
    rows = DEC_SEQ // GRID_W
    win = NA_ROWS * GRID_W
    scale = NA_HD ** -0.5
    dn_nt = (((1,), (1,)), ((), ()))
    dn_tn = (((0,), (0,)), ((), ()))

    kcb_scr[...] = kc_ref[...].astype(BF16)
    vcb_scr[...] = vc_ref[...].astype(BF16)
    first = lax.broadcasted_iota(jnp.int32, (GRID_W, LANES), 1) < NA_HD

    def body(it, carry):
        rr = [it * _NA_UNROLL + j for j in range(_NA_UNROLL)]
        rss = [jnp.clip(r - NA_ROWS // 2, 0, rows - NA_ROWS) for r in rr]
        qsls = [pl.ds(pl.multiple_of(r * GRID_W, GRID_W), GRID_W) for r in rr]
        wsls = [pl.ds(pl.multiple_of(rs * GRID_W, GRID_W), win) for rs in rss]
        qms, s_wins, s_ctxs = [], [], []
        for r, rs, qsl, wsl in zip(rr, rss, qsls, wsls):
            qm = _pair_queries(q_ref[qsl, :] * scale, first)
            bias = jnp.concatenate([bias_ref[NA_ROWS - 1 - (r - rs) + i] for i in range(NA_ROWS)], axis=0)
            s_wins.append(lax.dot_general(k_ref[wsl, :], qm, dn_nt, preferred_element_type=F32) + bias)
            s_ctxs.append(lax.dot_general(kcb_scr[...], qm, dn_nt, preferred_element_type=F32))
        ms = [jnp.maximum(jnp.max(sw, axis=0, keepdims=True), jnp.max(sc, axis=0, keepdims=True))
              for sw, sc in zip(s_wins, s_ctxs)]
        e_wins = [jnp.exp(sw - m) for sw, m in zip(s_wins, ms)]
        e_ctxs = [jnp.exp(sc - m) for sc, m in zip(s_ctxs, ms)]
        dens = [jnp.sum(ew, axis=0, keepdims=True) + jnp.sum(ec, axis=0, keepdims=True)
                for ew, ec in zip(e_wins, e_ctxs)]
        for qsl, wsl, ew, ec, den in zip(qsls, wsls, e_wins, e_ctxs, dens):
            o = (lax.dot_general(ew.astype(BF16), v_ref[wsl, :], dn_tn, preferred_element_type=F32)
                 + lax.dot_general(ec.astype(BF16), vcb_scr[...], dn_tn, preferred_element_type=F32))
            o = o / jnp.transpose(jnp.broadcast_to(den, (LANES, LANES)))
            o_ref[qsl, :] = jnp.where(first, o[:GRID_W], o[GRID_W:]).astype(o_ref.dtype)
        return carry

    lax.fori_loop(0, rows // _NA_UNROLL, body, 0)


def _na_bias_table(rpb):
    col = jnp.arange(GRID_W)
    cs = jnp.clip(col - NA_COLS // 2, 0, GRID_W - NA_COLS)
    col_ok = (col[None, :] >= cs[:, None]) & (col[None, :] < cs[:, None] + NA_COLS)
    dc = jnp.clip(col[None, :] - col[:, None] + NA_COLS - 1, 0, 2 * NA_COLS - 2)
    onehot = (dc.T[None, :, :] == jnp.arange(2 * NA_COLS - 1)[:, None, None]).astype(F32)
    t = jnp.einsum('hrd,dkq->hrkq', rpb.astype(F32), onehot, precision=lax.Precision.HIGHEST)
    t = jnp.where(col_ok.T[None, None], t, NEG_INF)
    t = t.reshape(NA_HEADS // 2, 2, 2 * NA_ROWS - 1, GRID_W, GRID_W)
    return jnp.concatenate([t[:, 0], t[:, 1]], axis=-1)


def _na_attention(proj, kctx, vctx, layer, rpb):
    blk = N_CTX // DEC_SEQ

    def col(cb):
        return pl.BlockSpec((None, DEC_SEQ, LANES), lambda b, p: (cb + p, blk + b, 0))

    ctx = pl.BlockSpec((None, None, PAST_LEN, LANES), lambda b, p: (b, layer, 0, p))
    return pl.pallas_call(
        _na_kernel, grid=(DEC_BATCH, NA_HEADS // 2),
        in_specs=[col(_QB), col(_KB), col(_VB), ctx, ctx,
                  pl.BlockSpec((None, 2 * NA_ROWS - 1, GRID_W, 2 * GRID_W), lambda b, p: (p, 0, 0, 0))],
        out_specs=pl.BlockSpec((None, DEC_SEQ, LANES), lambda b, p: (p, b, 0)),
        out_shape=jax.ShapeDtypeStruct((NA_HEADS // 2, N_LAT, LANES), BF16),
        scratch_shapes=[pltpu.VMEM((PAST_LEN, LANES), BF16), pltpu.VMEM((PAST_LEN, LANES), BF16)],
        compiler_params=_params(("parallel", "parallel")), name="na_attention",
    )(proj, proj, proj, kctx, vctx, _na_bias_table(rpb))


_LOGIT0 = N_EGROUPS
_R_E, _R_W, _R_RANK = 0, 2, 4


def _lane_min_where(mask, lane):
    return jnp.min(jnp.where(mask, lane, LANES), axis=-1, keepdims=True)


def _route_rows(lg, carry_ref, tri_ref):
    big = -3.0e38
    lane = lax.broadcasted_iota(jnp.int32, lg.shape, 1)
    is_g = lane < N_EGROUPS
    gmax = jnp.max(jnp.where(is_g, lg, big), axis=-1, keepdims=True)
    gsum = jnp.sum(jnp.where(is_g, jnp.exp(jnp.where(is_g, lg - gmax, 0.0)), 0.0), axis=-1, keepdims=True)
    pg_top = 1.0 / gsum
    g_idx = _lane_min_where(jnp.logical_and(is_g, lg == gmax), lane)
    in_g = jnp.logical_and(lane >= _LOGIT0, lax.shift_right_arithmetic(lane - _LOGIT0, 3) == g_idx)
    in_g = jnp.logical_and(in_g, lane < _LOGIT0 + N_EXPERTS)
    m1 = jnp.max(jnp.where(in_g, lg, big), axis=-1, keepdims=True)
    i1 = _lane_min_where(jnp.logical_and(in_g, lg == m1), lane)
    rest = jnp.logical_and(in_g, lane != i1)
    m2 = jnp.max(jnp.where(rest, lg, big), axis=-1, keepdims=True)
    i2 = _lane_min_where(jnp.logical_and(rest, lg == m2), lane)
    e2 = jnp.exp(m2 - m1)
    w1 = pg_top * (1.0 / (1.0 + e2))
    w2 = pg_top * (e2 / (1.0 + e2))
    hit1 = lane == i1
    hit2 = lane == i2
    picked = jnp.where(jnp.logical_or(hit1, hit2), 1.0, 0.0)
    before = jnp.dot(tri_ref[...], picked.astype(BF16), preferred_element_type=F32) + carry_ref[...]
    r1 = jnp.sum(jnp.where(hit1, before, 0.0), axis=-1, keepdims=True)
    r2 = jnp.sum(jnp.where(hit2, before, 0.0), axis=-1, keepdims=True)
    carry_ref[...] = carry_ref[...] + jnp.sum(picked, axis=0, keepdims=True)
    rec = jnp.zeros(lg.shape, F32)
    for ln, val in ((_R_E, (i1 - _LOGIT0).astype(F32)), (_R_E + 1, (i2 - _LOGIT0).astype(F32)),
                    (_R_W, w1), (_R_W + 1, w2), (_R_RANK, r1), (_R_RANK + 1, r2)):
        rec = jnp.where(lane == ln, val, rec)
    return rec


_PACK_W = D_MODEL // 2


def _pack_rows(hb):
    lo = lax.bitcast_convert_type(hb[:, :_PACK_W].astype(F32), jnp.int32)
    hi = lax.bitcast_convert_type(hb[:, _PACK_W:].astype(F32), jnp.int32)
    return jnp.bitwise_or(jnp.bitwise_and(hi, -65536), lax.shift_right_logical(lo, 16))


def _unpack_rows(w):
    lo = lax.bitcast_convert_type(lax.shift_left(w, 16), F32)
    hi = lax.bitcast_convert_type(jnp.bitwise_and(w, -65536), F32)
    return jnp.concatenate([lo, hi], axis=-1).astype(BF16)


def _moe_input(xnew, first, tail_in, tail_out, tail_scr):
    g2_ref, sc2_ref, sh2_ref, wr_ref, br_ref = tail_in
    x_out, h_out, rec_out, cnt_out = tail_out
    tri_scr, carry_scr = tail_scr

    @pl.when(first)
    def _():
        tm = tri_scr.shape[0]
        r = lax.broadcasted_iota(jnp.int32, (tm, tm), 0)
        c = lax.broadcasted_iota(jnp.int32, (tm, tm), 1)
        tri_scr[...] = jnp.where(c < r, 1.0, 0.0).astype(BF16)
        carry_scr[...] = jnp.zeros(carry_scr.shape, F32)

    x_out[...] = xnew
    h = _rms(xnew, g2_ref[...]) * (1.0 + sc2_ref[...]) + sh2_ref[...]
    hh = h.astype(BF16)
    h_out[...] = _pack_rows(hh)
    lg = jnp.dot(hh, wr_ref[...], preferred_element_type=F32) + br_ref[...]
    rec_out[...] = _route_rows(lg, carry_scr, tri_scr)
    cnt_out[...] = carry_scr[...]


def _even_out_kernel(oac_ref, obc_ref, oal_ref, obl_ref, x_ref, w_ref, gate_ref, *rest, ctx_tiles):
    tail_in, tail_out, (w_scr,), tail_scr = rest[:5], rest[5:9], rest[9:10], rest[10:]
    first = pl.program_id(0) == 0

    @pl.when(first)
    def _():
        w_scr[...] = w_ref[...].astype(BF16)

    is_ctx = pl.program_id(0) < ctx_tiles
    parts = [jnp.where(is_ctx, c_ref[hb], l_ref[hb])
             for c_ref, l_ref in ((oac_ref, oal_ref), (obc_ref, obl_ref)) for hb in range(DN_HEADS)]
    mix = jnp.concatenate(parts, axis=-1)
    out = jnp.dot(mix, w_scr[...], preferred_element_type=F32)
    _moe_input(x_ref[...] + gate_ref[...] * out, first, tail_in, tail_out, tail_scr)


def _tail_specs(tm):
    const = lambda shape: pl.BlockSpec(shape, lambda i: (0,) * len(shape))
    in_specs = [_mod_spec(2, tm), const((1, D_MODEL)), _mod_spec(4, tm), _mod_spec(3, tm),
                const((D_MODEL, LANES)), const((1, LANES))]
    out_specs = [pl.BlockSpec((tm, D_MODEL), lambda i: (i, 0)),
                 pl.BlockSpec((tm, _PACK_W), lambda i: (i, 0)),
                 pl.BlockSpec((tm, LANES), lambda i: (i, 0)),
                 const((1, LANES))]
    out_shape = [jax.ShapeDtypeStruct((N_TOK, D_MODEL), F32),
                 jax.ShapeDtypeStruct((N_TOK, _PACK_W), jnp.int32),
                 jax.ShapeDtypeStruct((N_TOK, LANES), F32),
                 jax.ShapeDtypeStruct((1, LANES), F32)]
    scratch = [pltpu.VMEM((tm, tm), BF16), pltpu.VMEM((1, LANES), F32)]
    return in_specs, out_specs, out_shape, scratch


def _router_weights(w_rg, b_rg, w_re, b_re):
    pad = LANES - N_EGROUPS - N_EXPERTS
    w = jnp.concatenate([w_rg, w_re, jnp.zeros((D_MODEL, pad), F32)], axis=1)
    b = jnp.concatenate([b_rg, b_re, jnp.zeros((pad,), F32)]).reshape(1, LANES)
    return w.astype(BF16), b


def _even_out(oa_ctx, ob_ctx, oa_lat, ob_lat, x, w_out, layer, mods, g2, router):
    tm = 512
    ctx_tiles = N_CTX // tm
    tail_in, out_specs, out_shape, tail_scr = _tail_specs(tm)
    ctxblk = pl.BlockSpec((DN_HEADS, tm, LANES), lambda i: (0, jnp.minimum(i, ctx_tiles - 1), 0))
    latblk = pl.BlockSpec((DN_HEADS, tm, LANES), lambda i: (0, jnp.maximum(i - ctx_tiles, 0), 0))
    return pl.pallas_call(
        functools.partial(_even_out_kernel, ctx_tiles=ctx_tiles), grid=(N_TOK // tm,),
        in_specs=[ctxblk, ctxblk, latblk, latblk, pl.BlockSpec((tm, D_MODEL), lambda i: (i, 0)),
                  pl.BlockSpec((None, D_MODEL, D_MODEL), lambda i: (layer, 0, 0))] + tail_in,
        out_specs=out_specs, out_shape=out_shape,
        scratch_shapes=[pltpu.VMEM((D_MODEL, D_MODEL), BF16)] + tail_scr,
        compiler_params=_params(("arbitrary",)), name="even_out",
    )(oa_ctx, ob_ctx, oa_lat, ob_lat, x, w_out, mods, g2.reshape(1, D_MODEL), mods, mods, *router)


def _gelu_tanh(x):
    c = 0.7978845608028654
    hx = 0.5 * x
    return hx + hx * jnp.tanh(x * (c + (c * 0.044715) * (x * x)))


def _sgu_kernel(x_ref, *rest, tm, n_pend):
    pend, rest = rest[:n_pend], rest[n_pend:]
    g1_ref, sh1_ref, sc1_ref, win_ref, lng_ref, lnb_ref, ws_ref, bst_ref, wout_ref, gate_ref = rest[:10]
    rest = rest[10:]
    tail_in, tail_out, (v_scr, m_scr), tail_scr = rest[:5], rest[5:9], rest[9:11], rest[11:]
    first = pl.program_id(0) == 0
    x = _with_pending(x_ref, pend)
    h = (_rms(x, g1_ref[...]) * (1.0 + sc1_ref[...]) + sh1_ref[...]).astype(BF16)

    def proj_u(g):
        return jnp.dot(h, win_ref[:, g * SG_GW:(g + 1) * SG_GW], preferred_element_type=F32)

    v = _gelu_tanh(jnp.dot(h, win_ref[:, SG_W:], preferred_element_type=F32))
    mu = jnp.mean(v, axis=-1, keepdims=True)
    vc = v - mu
    var = jnp.mean(vc * vc, axis=-1, keepdims=True)
    v_scr[...] = (vc * lax.rsqrt(var + EPS) * lng_ref[...] + lnb_ref[...]).astype(BF16)

    u_next = proj_u(0)
    for g in range(SG_GROUPS):
        cs = slice(g * SG_GW, (g + 1) * SG_GW)
        u_raw, u_next = u_next, (proj_u(g + 1) if g + 1 < SG_GROUPS else None)
        w_sp = ws_ref[g].astype(BF16)
        chunks = [slice(c * SG_CHUNK, (c + 1) * SG_CHUNK) for c in range(tm // SG_CHUNK)]
        sps = [jnp.dot(w_sp, v_scr[rs, cs], preferred_element_type=F32) for rs in chunks]
        u = _gelu_tanh(u_raw)
        for rs, sp in zip(chunks, sps):
            m_scr[rs, cs] = (u[rs] * (sp + bst_ref[:, g:g + 1])).astype(BF16)
    out = jnp.dot(m_scr[...], wout_ref[...], preferred_element_type=F32)
    _moe_input(x + gate_ref[...] * out, first, tail_in, tail_out, tail_scr)


def _sgu_layer(x, pend, mods, g1, w_in, ln_g, ln_b, w_s, b_s, w_out, g2, router):
    tm = 512
    pend = tuple(pend) if pend else ()
    tail_in, out_specs, out_shape, tail_scr = _tail_specs(tm)
    const = lambda shape: pl.BlockSpec(shape, lambda i: (0,) * len(shape))
    held = lambda shape: pl.BlockSpec(shape, lambda i: (0,) * len(shape), pipeline_mode=pl.Buffered(1))
    return pl.pallas_call(
        functools.partial(_sgu_kernel, tm=tm, n_pend=len(pend)), grid=(N_TOK // tm,),
        in_specs=[pl.BlockSpec((tm, D_MODEL), lambda i: (i, 0))] + (_pending_specs(tm) if pend else [])
        + [const((1, D_MODEL)), _mod_spec(0, tm), _mod_spec(1, tm),
                  held((D_MODEL, 2 * SG_W)), const((1, SG_W)), const((1, SG_W)),
                  const((SG_GROUPS, SG_CHUNK, SG_CHUNK)), const((SG_CHUNK, SG_GROUPS)),
                  held((SG_W, D_MODEL))] + tail_in,
        out_specs=out_specs, out_shape=out_shape,
        scratch_shapes=[pltpu.VMEM((tm, SG_W), BF16), pltpu.VMEM((tm, SG_W), BF16)] + tail_scr,
        compiler_params=_params(("arbitrary",)), name="sgu_layer",
    )(x, *pend, g1.reshape(1, D_MODEL), mods, mods, w_in.astype(BF16), ln_g.reshape(1, SG_W), ln_b.reshape(1, SG_W),
      w_s, b_s.T, w_out.astype(BF16), mods, g2.reshape(1, D_MODEL), mods, mods, *router)


def _plan(rec, cnt):
    e_idx = rec[:, _R_E:_R_E + 2].astype(jnp.int32)
    rank = rec[:, _R_RANK:_R_RANK + 2].astype(jnp.int32)
    counts = cnt[0, _LOGIT0:_LOGIT0 + N_EXPERTS].astype(jnp.int32)
    padded = (counts + MOE_BLK - 1) // MOE_BLK * MOE_BLK
    pad_end = jnp.cumsum(padded)
    pad_start = pad_end - padded
    hit = e_idx[:, :, None] == jnp.arange(N_EXPERTS, dtype=jnp.int32)[None, None, :]
    dest = jnp.sum(jnp.where(hit, pad_start[None, None, :], 0), axis=-1) + rank
    blk0 = jnp.arange(MOE_NBLK, dtype=jnp.int32) * MOE_BLK
    blk_e = jnp.minimum(jnp.sum((pad_end[None, :] <= blk0[:, None]).astype(jnp.int32), axis=-1),
                        N_EXPERTS - 1)
    n_used = (pad_end[-1] // MOE_BLK).astype(jnp.int32).reshape(1)
    owns = counts > 0
    slot_of = (jnp.cumsum(owns.astype(jnp.int32)) - 1) % _W_SLOTS
    ids = jnp.arange(N_EXPERTS, dtype=jnp.int32)
    later = jnp.logical_and(owns[None, :], ids[None, :] > ids[:, None])
    next_of = jnp.min(jnp.where(later, ids[None, :], N_EXPERTS), axis=-1)
    next2_of = jnp.concatenate([next_of, jnp.full((1,), N_EXPERTS, jnp.int32)])[next_of]
    ahead = jnp.stack([next_of, next2_of], axis=0)
    ahead = jnp.where(ahead == N_EXPERTS, -1, ahead)
    return dest, blk_e, n_used, slot_of[blk_e], ahead[:, blk_e].reshape(-1)


_W_PARTS = 4
_W_SLOTS = 3


def _expert_kernel(blk_e_ref, n_used_ref, slot_ref, next_ref, x_ref, wg_hbm, wu_hbm, wd_hbm, o_ref,
                   wg_buf, wu_buf, wd_buf, wg_scr, wu_scr, wd_scr, sems, *, layer):
    j = pl.program_id(0)
    e = blk_e_ref[j]
    slot = slot_ref[j]
    fresh = jnp.logical_or(j == 0, e != blk_e_ref[jnp.maximum(j - 1, 0)])
    live = j < n_used_ref[0]

    def copies(expert, s):
        out = []
        for m, (hbm, buf) in enumerate(((wg_hbm, wg_buf), (wu_hbm, wu_buf), (wd_hbm, wd_buf))):
            rows = buf.shape[1] // _W_PARTS
            for part in range(_W_PARTS):
                band = pl.ds(part * rows, rows)
                out.append(pltpu.make_async_copy(hbm.at[layer, expert, band], buf.at[s, band],
                                                 sems.at[s, m, part]))
        return out

    def start_if_any(expert, s):
        @pl.when(expert >= 0)
        def _():
            for cp in copies(expert, s):
                cp.start()

    @pl.when(j == 0)
    def _():
        for cp in copies(e, slot):
            cp.start()
        start_if_any(next_ref[j], lax.rem(slot + 1, _W_SLOTS))

    @pl.when(jnp.logical_and(fresh, live))
    def _():
        for cp in copies(e, slot):
            cp.wait()
        start_if_any(next_ref[MOE_NBLK + j], lax.rem(slot + 2, _W_SLOTS))

        wg_scr[...] = wg_buf[slot].astype(BF16)
        wu_scr[...] = wu_buf[slot].astype(BF16)
        wd_scr[...] = wd_buf[slot].astype(BF16)

    @pl.when(live)
    def _():
        x = _unpack_rows(x_ref[...])
        gt = jnp.dot(x, wg_scr[...], preferred_element_type=F32)
        up = jnp.dot(x, wu_scr[...], preferred_element_type=F32)
        hb = (_silu(gt) * up).astype(BF16)
        o_ref[...] = jnp.dot(hb, wd_scr[...], preferred_element_type=F32).astype(o_ref.dtype)

    @pl.when(jnp.logical_not(live))
    def _():
        o_ref[...] = jnp.zeros(o_ref.shape, o_ref.dtype)


def _experts(x_pad, blk_e, n_used, slot, nxt, w_gate, w_up, w_down, layer):
    hbm = pl.BlockSpec(memory_space=pl.ANY)
    grid_spec = pltpu.PrefetchScalarGridSpec(
        num_scalar_prefetch=4, grid=(MOE_NBLK,),
        in_specs=[pl.BlockSpec((MOE_BLK, _PACK_W), lambda j, be, nu, *_: (jnp.minimum(j, nu[0] - 1), 0)),
                  hbm, hbm, hbm],
        out_specs=pl.BlockSpec((MOE_BLK, D_MODEL), lambda j, *_: (j, 0)),
        scratch_shapes=[pltpu.VMEM((_W_SLOTS, D_MODEL, D_EXPERT), F32),
                        pltpu.VMEM((_W_SLOTS, D_MODEL, D_EXPERT), F32),
                        pltpu.VMEM((_W_SLOTS, D_EXPERT, D_MODEL), F32),
                        pltpu.VMEM((D_MODEL, D_EXPERT), BF16), pltpu.VMEM((D_MODEL, D_EXPERT), BF16),
                        pltpu.VMEM((D_EXPERT, D_MODEL), BF16),
                        pltpu.SemaphoreType.DMA((_W_SLOTS, 3, _W_PARTS))])
    return pl.pallas_call(
        functools.partial(_expert_kernel, layer=layer), grid_spec=grid_spec,
        out_shape=jax.ShapeDtypeStruct((MOE_NBLK * MOE_BLK, D_MODEL), BF16),
        compiler_params=_params(("arbitrary",)), name="experts",
    )(blk_e, n_used, slot, nxt, x_pad, w_gate, w_up, w_down)


def _final_kernel(x_ref, y2_ref, rec_ref, gate_ref, fg_ref, o_ref):
    o_ref[...] = _rms(_with_pending(x_ref, (y2_ref, rec_ref, gate_ref)), fg_ref[...])


def _final_norm(x, pend, final_g, row0, n_rows):
    tm = 512
    tile0 = row0 // tm
    return pl.pallas_call(
        _final_kernel, grid=(n_rows // tm,),
        in_specs=[pl.BlockSpec((tm, D_MODEL), lambda i: (i + tile0, 0))] + _pending_specs(tm, tile0)
        + [pl.BlockSpec((1, D_MODEL), lambda i: (0, 0))],
        out_specs=pl.BlockSpec((tm, D_MODEL), lambda i: (i, 0)),
        out_shape=jax.ShapeDtypeStruct((n_rows, D_MODEL), F32),
        compiler_params=_params(("parallel",)), name="final_norm",
    )(x, *pend, final_g.reshape(1, D_MODEL))


_SC_WORKERS = 32
_SC_CORES = 2
_SC_ROWS = 64


def _dispatch_rows(hp, dest):
    n, width = hp.shape
    per_w = n // _SC_WORKERS
    n_ch = per_w // _SC_ROWS
    idx = dest.T.reshape(2, _SC_WORKERS, n_ch, _SC_ROWS)
    mesh = plsc.VectorSubcoreMesh(core_axis_name="c", subcore_axis_name="s")

    @functools.partial(
        pl.kernel, mesh=mesh, out_type=jax.ShapeDtypeStruct((MOE_NBLK * MOE_BLK, width), hp.dtype),
        scratch_types=[pltpu.VMEM((n_ch, _SC_ROWS), jnp.int32), pltpu.VMEM((n_ch, _SC_ROWS), jnp.int32),
                       pltpu.VMEM((_SC_ROWS, width), hp.dtype)], name="dispatch_rows")
    def scatter(h_hbm, idx_hbm, out_hbm, i0_v, i1_v, rows_v):
        wid = lax.axis_index("s") * _SC_CORES + lax.axis_index("c")
        pltpu.sync_copy(idx_hbm.at[0, wid], i0_v)
        pltpu.sync_copy(idx_hbm.at[1, wid], i1_v)

        @pl.loop(0, n_ch)
        def _(g):
            pltpu.sync_copy(h_hbm.at[pl.ds(wid * per_w + g * _SC_ROWS, _SC_ROWS)], rows_v)
            pltpu.sync_copy(rows_v, out_hbm.at[i0_v.at[g]])
            pltpu.sync_copy(rows_v, out_hbm.at[i1_v.at[g]])

    return scatter(hp, idx)


def _moe(h, rec, cnt, mods, w_gate, w_up, w_down, layer):
    dest, blk_e, n_used, slot, nxt = _plan(rec, cnt)
    y_pad = _experts(_dispatch_rows(h, dest), blk_e, n_used, slot, nxt, w_gate, w_up, w_down, layer)
    order = dest.reshape(N_TOK // _PEND_TM, _PEND_TM, 2).transpose(0, 2, 1).reshape(-1)
    return y_pad[order], rec, mods


def kernel(x_prompt, x_sample, c, cache_k, cache_v, state_delta, c_ctx, ada_w, ada_b, norm1_g, norm2_g, final_g,
           ev_w_in, ev_w_out, ev_conv_w, ev_a_log, ev_dt_bias, ev_onorm_g, ev_rpb, od_w_in, od_ln_g, od_ln_b,
           od_w_s, od_b_s, od_w_out, moe_w_rg, moe_b_rg, moe_w_re, moe_b_re, moe_w_gate, moe_w_up, moe_w_down):
    x = (x_prompt.reshape(N_CTX, D_MODEL), x_sample.reshape(N_LAT, D_MODEL))
    cond = jnp.concatenate([c_ctx[None, :], c, jnp.zeros((N_COND - 1 - DEC_BATCH, D_MODEL), F32)], axis=0)
    mods_all = _ada_mods(cond, ada_w, ada_b)
    kctx_all = cache_k.reshape(DEC_BATCH, -1, PAST_LEN, NA_HEADS * NA_HD)
    vctx_all = cache_v.reshape(DEC_BATCH, -1, PAST_LEN, NA_HEADS * NA_HD)

    ks, vs, ss = [], [], []
    pend = None
    for l in range(DEPTH):
        mods = mods_all[l]
        router = _router_weights(moe_w_rg[l], moe_b_rg[l], moe_w_re[l], moe_b_re[l])
        if l % 2 == 0:
            e = l // 2
            proj, ab, kv, x = _even_proj(x, pend, mods, norm1_g[l], ev_w_in[e])
            dn = (proj, ab, ev_conv_w[e], ev_a_log[e], ev_dt_bias[e], ev_onorm_g[e])
            oa_ctx, s_fin = _delta_heads(*dn, SEQ, BATCH, 0, None)
            oa_lat, _ = _delta_heads(*dn, DEC_SEQ, DEC_BATCH, N_CTX // DEC_SEQ, (state_delta, e))
            ob_ctx = _ctx_attention(proj)
            ob_lat = _na_attention(proj, kctx_all, vctx_all, e, ev_rpb[e])
            x, h, rec, cnt = _even_out(oa_ctx, ob_ctx, oa_lat, ob_lat, x, ev_w_out, e, mods, norm2_g[l],
                                       router)
            na_w = NA_HEADS * NA_HD
            ks.append(kv[:N_CTX, :na_w].reshape(BATCH, SEQ, NA_HEADS, NA_HD))
            vs.append(kv[:N_CTX, na_w:].reshape(BATCH, SEQ, NA_HEADS, NA_HD))
            ss.append(s_fin)
        else:
            o = l // 2
            x, h, rec, cnt = _sgu_layer(x, pend, mods, norm1_g[l], od_w_in[o], od_ln_g[o], od_ln_b[o],
                                        od_w_s[o], od_b_s[o], od_w_out[o], norm2_g[l], router)
        pend = _moe(h, rec, cnt, mods, moe_w_gate, moe_w_up, moe_w_down, l)
    y_prompt = _final_norm(x, pend, final_g, 0, N_CTX).reshape(BATCH, SEQ, D_MODEL)
    y_sample = _final_norm(x, pend, final_g, N_CTX, N_LAT).reshape(DEC_BATCH, DEC_SEQ, D_MODEL)
    return (y_prompt, y_sample, jnp.stack(ks, axis=1), jnp.stack(vs, axis=1), jnp.stack(ss, axis=1))
```

```python
import functools

import jax
import jax.numpy as jnp
from jax import lax
from jax.experimental import pallas as pl
from jax.experimental.pallas import tpu as pltpu
from jax.experimental.pallas import tpu_sc as plsc

F32 = jnp.float32
BF16 = jnp.bfloat16

D_MODEL = 1024
BATCH = 16
SEQ = 256
DEPTH = 4
DEC_BATCH = 4
DEC_SEQ = 2048
PAST_LEN = 512
GRID_W = 64
EPS = 1e-6
NEG_INF = -1e30

DN_HEADS = 4
DN_DK = 128
DN_CHUNK = 64
NA_HEADS = 8
NA_HD = 64
NA_ROWS = 8
NA_COLS = 16
SG_CHUNK = 128
SG_GROUPS = 8
SG_W = 2 * D_MODEL
SG_GW = SG_W // SG_GROUPS
N_EGROUPS = 4
EXP_PER_GROUP = 8
N_EXPERTS = 32
D_EXPERT = 512

N_CTX = BATCH * SEQ
N_LAT = DEC_BATCH * DEC_SEQ
N_TOK = N_CTX + N_LAT
N_COND = 8
PROJ_W = 4096
LANES = 128
MOE_BLK = 256
MOE_NBLK = -(-(2 * N_TOK + N_EXPERTS * (MOE_BLK - 1)) // MOE_BLK)
VMEM_LIMIT = 56 * 1024 * 1024

_QA, _KA, _VA, _ZA, _QB, _KB, _VB = 0, 4, 8, 12, 16, 20, 24


def _params(sem):
    return pltpu.CompilerParams(dimension_semantics=sem, vmem_limit_bytes=VMEM_LIMIT)


def _bdot(a, b):
    return jnp.dot(a.astype(BF16), b.astype(BF16), preferred_element_type=F32)


def _bdot_nt(a, b):
    return lax.dot_general(a.astype(BF16), b.astype(BF16), (((1,), (1,)), ((), ())),
                           preferred_element_type=F32)


def _bdot_tn(a, b):
    return lax.dot_general(a.astype(BF16), b.astype(BF16), (((0,), (0,)), ((), ())),
                           preferred_element_type=F32)


def _split2(a):
    p0 = a.astype(BF16)
    return p0, (a - p0.astype(F32)).astype(BF16)


def _dot3(a, b):
    ah = a.astype(BF16)
    al = (a - ah.astype(F32)).astype(BF16)
    bh = b.astype(BF16)
    bl = (b - bh.astype(F32)).astype(BF16)
    return (jnp.dot(ah, bh, preferred_element_type=F32) + jnp.dot(ah, bl, preferred_element_type=F32)
            + jnp.dot(al, bh, preferred_element_type=F32))


def _mask_bf16(m01):
    return jnp.where(m01, 1.0, 0.0).astype(BF16)


def _xdot(m01, a):
    m = _mask_bf16(m01)
    p0, p1 = _split2(a)
    return jnp.dot(m, p0, preferred_element_type=F32) + jnp.dot(m, p1, preferred_element_type=F32)


def _xdot_r(a, m01):
    m = _mask_bf16(m01)
    p0, p1 = _split2(a)
    return jnp.dot(p0, m, preferred_element_type=F32) + jnp.dot(p1, m, preferred_element_type=F32)


def _sigmoid(x):
    return 0.5 * jnp.tanh(0.5 * x) + 0.5


def _silu(x):
    return x * _sigmoid(x)


def _rms(x, g):
    return x * lax.rsqrt(jnp.mean(x * x, axis=-1, keepdims=True) + EPS) * g


def _cond_index(row):
    return jnp.where(row < N_CTX, 0, 1 + (row - N_CTX) // DEC_SEQ)


def _mod_spec(k, tm, tile0=0):
    return pl.BlockSpec((None, None, 1, D_MODEL), lambda i, *_: (_cond_index((i + tile0) * tm), k, 0, 0))


def _ada_kernel(c_ref, w_ref, b_ref, o_ref):
    o_ref[...] = _bdot(_silu(c_ref[...]), w_ref[...]) + b_ref[...]


def _ada_mods(cond, ada_w, ada_b):
    tn = 1536
    out = pl.pallas_call(
        _ada_kernel, grid=(DEPTH, 6 * D_MODEL // tn),
        in_specs=[pl.BlockSpec((N_COND, D_MODEL), lambda l, j: (0, 0)),
                  pl.BlockSpec((None, D_MODEL, tn), lambda l, j: (l, 0, j)),
                  pl.BlockSpec((None, 1, tn), lambda l, j: (l, 0, j))],
        out_specs=pl.BlockSpec((None, N_COND, tn), lambda l, j: (l, 0, j)),
        out_shape=jax.ShapeDtypeStruct((DEPTH, N_COND, 6 * D_MODEL), F32),
        compiler_params=_params(("parallel", "parallel")), name="ada_mods",
    )(cond, ada_w, ada_b.reshape(DEPTH, 1, 6 * D_MODEL))
    return out.reshape(DEPTH, N_COND, 6, 1, D_MODEL)


_EV_TN = 512
_EV_W = 7 * DN_HEADS * LANES
_KV_COL0 = _KB * LANES


def _with_pending(x_ref, pend):
    if not pend:
        return x_ref[...]
    y2_ref, rec_ref, gate_ref = pend
    rec = rec_ref[...]
    tm = rec.shape[0]
    y = (rec[:, _R_W:_R_W + 1] * y2_ref[:tm, :].astype(F32)
         + rec[:, _R_W + 1:_R_W + 2] * y2_ref[tm:, :].astype(F32))
    return x_ref[...] + gate_ref[...] * y


_PEND_TM = 512


def _pending_specs(tm, tile0=0):
    assert tm == _PEND_TM
    return [pl.BlockSpec((2 * tm, D_MODEL), lambda i: (i + tile0, 0)),
            pl.BlockSpec((tm, LANES), lambda i: (i + tile0, 0)), _mod_spec(5, tm, tile0)]


def _even_proj_kernel(x_ref, *rest, n_pend, ctx_tiles):
    pend, (g_ref, sh_ref, sc_ref, w_ref, wab_ref, o_ref, ab_ref, kv_ref) = rest[:n_pend], rest[n_pend:n_pend + 8]
    if n_pend == 1:
        x = jnp.where(pl.program_id(0) < ctx_tiles, x_ref[...], pend[0][...])
    else:
        x = _with_pending(x_ref, pend)
    if pend:
        rest[n_pend + 8][...] = x
    h = (_rms(x, g_ref[...]) * (1.0 + sc_ref[...]) + sh_ref[...]).astype(BF16)
    ab_ref[...] = jnp.dot(h, wab_ref[...], preferred_element_type=F32)
    for j in range(_EV_W // _EV_TN):
        c0 = j * _EV_TN
        y = jnp.dot(h, w_ref[:, c0:c0 + _EV_TN], preferred_element_type=F32)
        for c in range(_EV_TN // LANES):
            o_ref[c0 // LANES + c] = y[:, c * LANES:(c + 1) * LANES].astype(BF16)
        if c0 >= _KV_COL0:
            kv_ref[:, c0 - _KV_COL0:c0 - _KV_COL0 + _EV_TN] = y


def _even_proj(x, pend, mods, g, w_in):
    tm = 512
    ctx_tiles = N_CTX // tm
    rows = pl.BlockSpec((tm, D_MODEL), lambda i: (i, 0))
    if isinstance(x, tuple):
        x, pend = x[0], (x[1],)
        x_specs = [pl.BlockSpec((tm, D_MODEL), lambda i: (jnp.minimum(i, ctx_tiles - 1), 0)),
                   pl.BlockSpec((tm, D_MODEL), lambda i: (jnp.maximum(i - ctx_tiles, 0), 0))]
    else:
        pend = tuple(pend) if pend else ()
        x_specs = [rows] + (_pending_specs(tm) if pend else [])
    n_ab = 4 * DN_HEADS
    ab0 = 4 * DN_HEADS * DN_DK
    w_main = jnp.concatenate([w_in[:, :ab0], w_in[:, ab0 + n_ab:]], axis=1).astype(BF16)
    w_ab = jnp.concatenate([w_in[:, ab0:ab0 + n_ab], jnp.zeros((D_MODEL, LANES - n_ab), F32)],
                           axis=1).astype(BF16)
    held = lambda shape: pl.BlockSpec(shape, lambda i: (0,) * len(shape), pipeline_mode=pl.Buffered(1))
    out_specs = [pl.BlockSpec((_EV_W // LANES, tm, LANES), lambda i: (0, i, 0)),
                 pl.BlockSpec((tm, LANES), lambda i: (i, 0)),
                 pl.BlockSpec((tm, 2 * NA_HEADS * NA_HD), lambda i: (i, 0))]
    out_shape = [jax.ShapeDtypeStruct((_EV_W // LANES, N_TOK, LANES), BF16),
                 jax.ShapeDtypeStruct((N_TOK, LANES), F32),
                 jax.ShapeDtypeStruct((N_TOK, 2 * NA_HEADS * NA_HD), F32)]
    if pend:
        out_specs.append(rows)
        out_shape.append(jax.ShapeDtypeStruct((N_TOK, D_MODEL), F32))
    res = pl.pallas_call(
        functools.partial(_even_proj_kernel, n_pend=len(pend), ctx_tiles=ctx_tiles), grid=(N_TOK // tm,),
        in_specs=x_specs
        + [pl.BlockSpec((1, D_MODEL), lambda i: (0, 0)), _mod_spec(0, tm), _mod_spec(1, tm),
           held((D_MODEL, _EV_W)), held((D_MODEL, LANES))],
        out_specs=out_specs, out_shape=out_shape,
        compiler_params=_params(("parallel",)), name="even_proj",
    )(x, *pend, g.reshape(1, D_MODEL), mods, mods, w_main, w_ab)
    return (*res[:3], res[3] if pend else x)


_CHUNK_SHIFT = DN_CHUNK.bit_length() - 1
_CUM_ROWS = 256
_DN_CHAINS = 16
_DN_SHORT = 256
_MQ_ROWS = DN_DK + DN_CHUNK


def _dn_kernel(*refs, T, HB, has_s0, want_state):
    it = iter(refs)
    q_ref, k_ref, v_ref, z_ref, ab_ref = (next(it) for _ in range(5))
    cwq_ref, cwk_ref, cwv_ref, alog_ref, dtb_ref, og_ref = (next(it) for _ in range(6))
    s0_ref = next(it) if has_s0 else None
    o_ref = next(it)
    sfin_ref = next(it) if want_state else None
    qc, kc, vc, gsc, bsc, osc, b_s, mq_s = (next(it) for _ in range(8))

    C = DN_CHUNK
    n = T // C
    h0 = pl.program_id(1) * HB

    row = lax.broadcasted_iota(jnp.int32, (T, 1), 0)

    def conv(x_ref, cw_ref, hh):
        x = x_ref[hh].astype(F32)
        cw = cw_ref[:, hh * LANES:(hh + 1) * LANES]
        xp = jnp.where(row == 0, 0.0, pltpu.roll(x, 1, 0))
        xn = jnp.where(row == T - 1, 0.0, pltpu.roll(x, T - 1, 0))
        return _silu(cw[0:1, :] * xp + cw[1:2, :] * x + cw[2:3, :] * xn)

    def l2n(x):
        return x * lax.rsqrt(jnp.sum(x * x, axis=-1, keepdims=True) + EPS)

    ab = ab_ref[...]
    lane = lax.broadcasted_iota(jnp.int32, (1, LANES), 1)
    dtb = jnp.zeros((1, LANES), F32)
    alog = jnp.zeros((1, LANES), F32)
    for d in range(2):
        for hq in range(DN_HEADS):
            dtb = jnp.where(lane == d * DN_HEADS + hq, dtb_ref[d, hq], dtb)
            alog = jnp.where(lane == d * DN_HEADS + hq, alog_ref[d, hq], alog)
    xs = ab + dtb
    g_all = -jnp.exp(alog) * (jnp.maximum(xs, 0.0) + jnp.log1p(jnp.exp(-jnp.abs(xs))))
    beta_all = _sigmoid(ab)

    sel_r = lax.broadcasted_iota(jnp.int32, (LANES, LANES), 0)
    for hh in range(HB):
        qc[hh] = l2n(conv(q_ref, cwq_ref, hh)) * (DN_DK ** -0.5)
        kc[hh] = l2n(conv(k_ref, cwk_ref, hh))
        vc[hh] = conv(v_ref, cwv_ref, hh)
        hd = h0 + hh
        for d in range(2):
            gsc[hh, d] = _xdot_r(g_all, sel_r == d * DN_HEADS + hd)
            bsc[hh, d] = _xdot_r(beta_all, sel_r == 2 * DN_HEADS + d * DN_HEADS + hd)

    pr = lax.broadcasted_iota(jnp.int32, (_CUM_ROWS, _CUM_ROWS), 0)
    pc = lax.broadcasted_iota(jnp.int32, (_CUM_ROWS, _CUM_ROWS), 1)
    same = lax.shift_right_logical(pr, _CHUNK_SHIFT) == lax.shift_right_logical(pc, _CHUNK_SHIFT)
    cum_mask = (jnp.logical_and(same, pc <= pr), jnp.logical_and(same, pc >= pr))

    def cum_body(i, carry):
        sl = pl.ds(pl.multiple_of(i * _CUM_ROWS, _CUM_ROWS), _CUM_ROWS)
        for hh in range(HB):
            for d in range(2):
                gsc[hh, d, sl, :] = _xdot(cum_mask[d], gsc[hh, d, sl, :])
        return carry

    lax.fori_loop(0, T // _CUM_ROWS, cum_body, 0)

    ri = lax.broadcasted_iota(jnp.int32, (C, C), 0)
    ci = lax.broadcasted_iota(jnp.int32, (C, C), 1)
    eye = (ri == ci).astype(F32)

    def prepare(items):
        lows, decays = [], []
        kk, qk = {}, {}
        for hh, d, c, slot in items:
            sl = pl.ds(pl.multiple_of(c * C, C), C)
            gc = gsc[hh, d, sl, :]
            if (hh, slot) not in kk:
                k = kc[hh, sl, :]
                kk[hh, slot] = _bdot_nt(k, k)
                qk[hh, slot] = _bdot_nt(qc[hh, sl, :], k)
            incl = (ci <= ri) if d == 0 else (ci >= ri)
            strict = (ci < ri) if d == 0 else (ci > ri)
            gr = jnp.transpose(gc)[0:1, :C]
            decay = jnp.where(incl, jnp.exp(jnp.where(incl, gc[:, :C] - gr, 0.0)), 0.0)
            lows.append(jnp.where(strict, bsc[hh, d, sl, :C] * kk[hh, slot] * decay, 0.0))
            decays.append(decay)
        ts = [eye - low for low in lows]
        ps = lows
        for step in range(_CHUNK_SHIFT - 1):
            ps = [_dot3(p, p) for p in ps]
            ts = [t + _dot3(t, p) for t, p in zip(ts, ps)]
        wus, kds, attns, qds = [], [], [], []
        for (hh, d, c, slot), t, decay in zip(items, ts, decays):
            sl = pl.ds(pl.multiple_of(c * C, C), C)
            k, gc, beta = kc[hh, sl, :], gsc[hh, d, sl, :], bsc[hh, d, sl, :]
            eg = jnp.exp(gc)
            uw = _bdot(t, jnp.concatenate([vc[hh, sl, :] * beta, k * beta * eg], axis=-1))
            last = gc[C - 1:C, :] if d == 0 else gc[0:1, :]
            wus.append(jnp.concatenate([uw[:, LANES:], uw[:, :LANES]], axis=-1).astype(BF16))
            kds.append((k * jnp.exp(last - gc)).astype(BF16))
            attns.append((qk[hh, slot] * decay).astype(BF16))
            qds.append(qc[hh, sl, :] * eg)
        kdwus = [lax.dot_general(kd, wu, (((0,), (0,)), ((), ())), preferred_element_type=F32)
                 for kd, wu in zip(kds, wus)]
        awus = [jnp.dot(attn, wu, preferred_element_type=F32) for attn, wu in zip(attns, wus)]
        for (hh, d, c, slot), kdwu, awu, qd in zip(items, kdwus, awus, qds):
            mq0 = pl.multiple_of(c * _MQ_ROWS, _MQ_ROWS)
            mq_s[hh, d, pl.ds(mq0, DN_DK), :] = kdwu[:, :LANES].astype(BF16)
            mq_s[hh, d, pl.ds(mq0 + DN_DK, C), :] = (qd - awu[:, :LANES]).astype(BF16)
            b_s[hh, d, pl.ds(pl.multiple_of(c * DN_DK, DN_DK), DN_DK), :] = kdwu[:, LANES:]
            osc[hh, d, pl.ds(pl.multiple_of(c * C, C), C), :] = awu[:, LANES:]

    n_prep = min(n, _DN_CHAINS // 2)
    h_prep = max(1, min(HB, _DN_CHAINS // (2 * n_prep)))

    def prep_body(i, carry):
        for hg in range(0, HB, h_prep):
            prepare([(hh, d, i * n_prep + j, j) for hh in range(hg, hg + h_prep) for j in range(n_prep)
                     for d in range(2)])
        return carry

    lax.fori_loop(0, n // n_prep, prep_body, 0)

    def advance(hh, d, c, S):
        sl = pl.ds(pl.multiple_of(c * C, C), C)
        ms = jnp.dot(mq_s[hh, d, pl.ds(pl.multiple_of(c * _MQ_ROWS, _MQ_ROWS), _MQ_ROWS), :], S.astype(BF16),
                     preferred_element_type=F32)
        osc[hh, d, sl, :] = osc[hh, d, sl, :] + ms[DN_DK:]
        last = gsc[hh, d, pl.ds(c * C + (C - 1 if d == 0 else 0), 1), :]
        return (S * jnp.exp(last) - ms[:DN_DK]
                + b_s[hh, d, pl.ds(pl.multiple_of(c * DN_DK, DN_DK), DN_DK), :])

    def body(i, carry):
        return tuple(advance(hh, d, i if d == 0 else n - 1 - i, carry[2 * hh + d])
                     for hh in range(HB) for d in range(2))

    if has_s0:
        init = tuple(s0_ref[d, hh] for hh in range(HB) for d in range(2))
    else:
        init = tuple(jnp.zeros((DN_DK, LANES), F32) for _ in range(2 * HB))
    fin = lax.fori_loop(0, n, body, init)
    for hh in range(HB):
        if want_state:
            sfin_ref[0, hh] = fin[2 * hh]
            sfin_ref[1, hh] = fin[2 * hh + 1]
        o = osc[hh, 0] + osc[hh, 1]
        o_ref[hh] = (_rms(o, og_ref[...]) * _silu(z_ref[hh].astype(F32))).astype(o_ref.dtype)


def _delta_heads(proj, ab, conv_w, a_log, dt_bias, onorm_g, T, n_seq, row0, s0):
    has_s0 = s0 is not None
    want_state = not has_s0
    hb = DN_HEADS if T <= _DN_SHORT else 2

    def col(cb):
        return pl.BlockSpec((hb, T, LANES), lambda s, h: (cb // hb + h, row0 + s, 0))

    def cw(cb):
        return pl.BlockSpec((3, hb * LANES), lambda s, h: (0, cb // hb + h))

    smem = pl.BlockSpec(memory_space=pltpu.SMEM)
    in_specs = [col(_QA), col(_KA), col(_VA), col(_ZA),
                pl.BlockSpec((T, LANES), lambda s, h: (row0 + s, 0)),
                cw(0), cw(4), cw(8), smem, smem,
                pl.BlockSpec((1, LANES), lambda s, h: (0, 0))]
    args = [proj, proj, proj, proj, ab, conv_w, conv_w, conv_w, a_log, dt_bias,
            onorm_g.reshape(1, LANES)]
    state_spec = pl.BlockSpec((None, 2, hb, DN_DK, LANES), lambda s, h: (s, 0, h, 0, 0))
    if has_s0:
        states, layer = s0
        in_specs.append(pl.BlockSpec((None, None, 2, hb, DN_DK, LANES), lambda s, h: (s, layer, 0, h, 0, 0)))
        args.append(states)
    out_shape = [jax.ShapeDtypeStruct((DN_HEADS, n_seq * T, LANES), BF16)]
    out_specs = [pl.BlockSpec((hb, T, LANES), lambda s, h: (h, s, 0))]
    if want_state:
        out_shape.append(jax.ShapeDtypeStruct((n_seq, 2, DN_HEADS, DN_DK, LANES), F32))
        out_specs.append(state_spec)
    res = pl.pallas_call(
        functools.partial(_dn_kernel, T=T, HB=hb, has_s0=has_s0, want_state=want_state),
        grid=(n_seq, DN_HEADS // hb), in_specs=in_specs, out_specs=out_specs, out_shape=out_shape,
        scratch_shapes=[pltpu.VMEM((hb, T, LANES), F32)] * 3
        + [pltpu.VMEM((hb, 2, T, LANES), F32)] * 3
        + [pltpu.VMEM((hb, 2, T // DN_CHUNK * DN_DK, LANES), F32),
           pltpu.VMEM((hb, 2, T // DN_CHUNK * _MQ_ROWS, LANES), BF16)],
        compiler_params=_params(("parallel", "parallel")), name="delta_heads_%d" % T,
    )(*args)
    return res if want_state else (res[0], None)


def _pair_queries(q, first):
    return jnp.concatenate([jnp.where(first, q, 0.0), jnp.where(first, 0.0, q)], axis=0).astype(BF16)


def _ctx_attn_kernel(q_ref, k_ref, v_ref, o_ref):
    first = lax.broadcasted_iota(jnp.int32, (SEQ, LANES), 1) < NA_HD
    qm = _pair_queries(q_ref[...] * (NA_HD ** -0.5), first)
    s = lax.dot_general(k_ref[...], qm, (((1,), (1,)), ((), ())), preferred_element_type=F32)
    e = jnp.exp(s - jnp.max(s, axis=0, keepdims=True))
    den = jnp.sum(e, axis=0, keepdims=True)
    o = lax.dot_general(e.astype(BF16), v_ref[...], (((0,), (0,)), ((), ())), preferred_element_type=F32)
    o = jnp.where(first, o[:SEQ], o[SEQ:])
    den_t = jnp.transpose(jnp.broadcast_to(den, (LANES, 2 * SEQ)))
    o_ref[...] = (o / jnp.where(first, den_t[:SEQ], den_t[SEQ:])).astype(o_ref.dtype)


def _ctx_attention(proj):
    def col(cb):
        return pl.BlockSpec((None, SEQ, LANES), lambda s, p: (cb + p, s, 0))

    return pl.pallas_call(
        _ctx_attn_kernel, grid=(BATCH, NA_HEADS // 2),
        in_specs=[col(_QB), col(_KB), col(_VB)],
        out_specs=pl.BlockSpec((None, SEQ, LANES), lambda s, p: (p, s, 0)),
        out_shape=jax.ShapeDtypeStruct((NA_HEADS // 2, N_CTX, LANES), BF16),
        compiler_params=_params(("parallel", "parallel")), name="ctx_attention",
    )(proj, proj, proj)


_NA_UNROLL = 4


def _na_kernel(q_ref, k_ref, v_ref, kc_ref, vc_ref, bias_ref, o_ref, kcb_scr, vcb_scr):
    rows = DEC_SEQ // GRID_W
    win = NA_ROWS * GRID_W
    scale = NA_HD ** -0.5
    dn_nt = (((1,), (1,)), ((), ()))
    dn_tn = (((0,), (0,)), ((), ()))

    kcb_scr[...] = kc_ref[...].astype(BF16)
    vcb_scr[...] = vc_ref[...].astype(BF16)
    first = lax.broadcasted_iota(jnp.int32, (GRID_W, LANES), 1) < NA_HD

    def body(it, carry):
        rr = [it * _NA_UNROLL + j for j in range(_NA_UNROLL)]
        rss = [jnp.clip(r - NA_ROWS // 2, 0, rows - NA_ROWS) for r in rr]
        qsls = [pl.ds(pl.multiple_of(r * GRID_W, GRID_W), GRID_W) for r in rr]
        wsls = [pl.ds(pl.multiple_of(rs * GRID_W, GRID_W), win) for rs in rss]
        qms, s_wins, s_ctxs = [], [], []
        for r, rs, qsl, wsl in zip(rr, rss, qsls, wsls):
            qm = _pair_queries(q_ref[qsl, :] * scale, first)
            bias = jnp.concatenate([bias_ref[NA_ROWS - 1 - (r - rs) + i] for i in range(NA_ROWS)], axis=0)
            s_wins.append(lax.dot_general(k_ref[wsl, :], qm, dn_nt, preferred_element_type=F32) + bias)
            s_ctxs.append(lax.dot_general(kcb_scr[...], qm, dn_nt, preferred_element_type=F32))
        ms = [jnp.maximum(jnp.max(sw, axis=0, keepdims=True), jnp.max(sc, axis=0, keepdims=True))
              for sw, sc in zip(s_wins, s_ctxs)]
        e_wins = [jnp.exp(sw - m) for sw, m in zip(s_wins, ms)]
        e_ctxs = [jnp.exp(sc - m) for sc, m in zip(s_ctxs, ms)]
        dens = [jnp.sum(ew, axis=0, keepdims=True) + jnp.sum(ec, axis=0, keepdims=True)
                for ew, ec in zip(e_wins, e_ctxs)]
        for qsl, wsl, ew, ec, den in zip(qsls, wsls, e_wins, e_ctxs, dens):
            o = (lax.dot_general(ew.astype(BF16), v_ref[wsl, :], dn_tn, preferred_element_type=F32)
                 + lax.dot_general(ec.astype(BF16), vcb_scr[...], dn_tn, preferred_element_type=F32))
            o = o / jnp.transpose(jnp.broadcast_to(den, (LANES, LANES)))
            o_ref[qsl, :] = jnp.where(first, o[:GRID_W], o[GRID_W:]).astype(o_ref.dtype)
        return carry

    lax.fori_loop(0, rows // _NA_UNROLL, body, 0)


def _na_bias_table(rpb):
    col = jnp.arange(GRID_W)
    cs = jnp.clip(col - NA_COLS // 2, 0, GRID_W - NA_COLS)
    col_ok = (col[None, :] >= cs[:, None]) & (col[None, :] < cs[:, None] + NA_COLS)
    dc = jnp.clip(col[None, :] - col[:, None] + NA_COLS - 1, 0, 2 * NA_COLS - 2)
    onehot = (dc.T[None, :, :] == jnp.arange(2 * NA_COLS - 1)[:, None, None]).astype(F32)
    t = jnp.einsum('hrd,dkq->hrkq', rpb.astype(F32), onehot, precision=lax.Precision.HIGHEST)
    t = jnp.where(col_ok.T[None, None], t, NEG_INF)
    t = t.reshape(NA_HEADS // 2, 2, 2 * NA_ROWS - 1, GRID_W, GRID_W)
    return jnp.concatenate([t[:, 0], t[:, 1]], axis=-1)


def _na_attention(proj, kctx, vctx, layer, rpb):
    blk = N_CTX // DEC_SEQ

    def col(cb):
        return pl.BlockSpec((None, DEC_SEQ, LANES), lambda b, p: (cb + p, blk + b, 0))

    ctx = pl.BlockSpec((None, None, PAST_LEN, LANES), lambda b, p: (b, layer, 0, p))
    return pl.pallas_call(
        _na_kernel, grid=(DEC_BATCH, NA_HEADS // 2),
        in_specs=[col(_QB), col(_KB), col(_VB), ctx, ctx,
                  pl.BlockSpec((None, 2 * NA_ROWS - 1, GRID_W, 2 * GRID_W), lambda b, p: (p, 0, 0, 0))],
        out_specs=pl.BlockSpec((None, DEC_SEQ, LANES), lambda b, p: (p, b, 0)),
        out_shape=jax.ShapeDtypeStruct((NA_HEADS // 2, N_LAT, LANES), BF16),
        scratch_shapes=[pltpu.VMEM((PAST_LEN, LANES), BF16), pltpu.VMEM((PAST_LEN, LANES), BF16)],
        compiler_params=_params(("parallel", "parallel")), name="na_attention",
    )(proj, proj, proj, kctx, vctx, _na_bias_table(rpb))


_LOGIT0 = N_EGROUPS
_R_E, _R_W, _R_RANK = 0, 2, 4


def _lane_min_where(mask, lane):
    return jnp.min(jnp.where(mask, lane, LANES), axis=-1, keepdims=True)


def _route_rows(lg, carry_ref, tri_ref):
    big = -3.0e38
    lane = lax.broadcasted_iota(jnp.int32, lg.shape, 1)
    is_g = lane < N_EGROUPS
    gmax = jnp.max(jnp.where(is_g, lg, big), axis=-1, keepdims=True)
    gsum = jnp.sum(jnp.where(is_g, jnp.exp(jnp.where(is_g, lg - gmax, 0.0)), 0.0), axis=-1, keepdims=True)
    pg_top = 1.0 / gsum
    g_idx = _lane_min_where(jnp.logical_and(is_g, lg == gmax), lane)
    in_g = jnp.logical_and(lane >= _LOGIT0, lax.shift_right_arithmetic(lane - _LOGIT0, 3) == g_idx)
    in_g = jnp.logical_and(in_g, lane < _LOGIT0 + N_EXPERTS)
    m1 = jnp.max(jnp.where(in_g, lg, big), axis=-1, keepdims=True)
    i1 = _lane_min_where(jnp.logical_and(in_g, lg == m1), lane)
    rest = jnp.logical_and(in_g, lane != i1)
    m2 = jnp.max(jnp.where(rest, lg, big), axis=-1, keepdims=True)
    i2 = _lane_min_where(jnp.logical_and(rest, lg == m2), lane)
    e2 = jnp.exp(m2 - m1)
    w1 = pg_top * (1.0 / (1.0 + e2))
    w2 = pg_top * (e2 / (1.0 + e2))
    hit1 = lane == i1
    hit2 = lane == i2
    picked = jnp.where(jnp.logical_or(hit1, hit2), 1.0, 0.0)
    before = jnp.dot(tri_ref[...], picked.astype(BF16), preferred_element_type=F32) + carry_ref[...]
    r1 = jnp.sum(jnp.where(hit1, before, 0.0), axis=-1, keepdims=True)
    r2 = jnp.sum(jnp.where(hit2, before, 0.0), axis=-1, keepdims=True)
    carry_ref[...] = carry_ref[...] + jnp.sum(picked, axis=0, keepdims=True)
    rec = jnp.zeros(lg.shape, F32)
    for ln, val in ((_R_E, (i1 - _LOGIT0).astype(F32)), (_R_E + 1, (i2 - _LOGIT0).astype(F32)),
                    (_R_W, w1), (_R_W + 1, w2), (_R_RANK, r1), (_R_RANK + 1, r2)):
        rec = jnp.where(lane == ln, val, rec)
    return rec


_PACK_W = D_MODEL // 2


def _pack_rows(hb):
    lo = lax.bitcast_convert_type(hb[:, :_PACK_W].astype(F32), jnp.int32)
    hi = lax.bitcast_convert_type(hb[:, _PACK_W:].astype(F32), jnp.int32)
    return jnp.bitwise_or(jnp.bitwise_and(hi, -65536), lax.shift_right_logical(lo, 16))


def _unpack_rows(w):
    lo = lax.bitcast_convert_type(lax.shift_left(w, 16), F32)
    hi = lax.bitcast_convert_type(jnp.bitwise_and(w, -65536), F32)
    return jnp.concatenate([lo, hi], axis=-1).astype(BF16)


def _moe_input(xnew, first, tail_in, tail_out, tail_scr):
    g2_ref, sc2_ref, sh2_ref, wr_ref, br_ref = tail_in
    x_out, h_out, rec_out, cnt_out = tail_out
    tri_scr, carry_scr = tail_scr

    @pl.when(first)
    def _():
        tm = tri_scr.shape[0]
        r = lax.broadcasted_iota(jnp.int32, (tm, tm), 0)
        c = lax.broadcasted_iota(jnp.int32, (tm, tm), 1)
        tri_scr[...] = jnp.where(c < r, 1.0, 0.0).astype(BF16)
        carry_scr[...] = jnp.zeros(carry_scr.shape, F32)

    x_out[...] = xnew
    h = _rms(xnew, g2_ref[...]) * (1.0 + sc2_ref[...]) + sh2_ref[...]
    hh = h.astype(BF16)
    h_out[...] = _pack_rows(hh)
    lg = jnp.dot(hh, wr_ref[...], preferred_element_type=F32) + br_ref[...]
    rec_out[...] = _route_rows(lg, carry_scr, tri_scr)
    cnt_out[...] = carry_scr[...]


def _even_out_kernel(oac_ref, obc_ref, oal_ref, obl_ref, x_ref, w_ref, gate_ref, *rest, ctx_tiles):
    tail_in, tail_out, (w_scr,), tail_scr = rest[:5], rest[5:9], rest[9:10], rest[10:]
    first = pl.program_id(0) == 0

    @pl.when(first)
    def _():
        w_scr[...] = w_ref[...].astype(BF16)

    is_ctx = pl.program_id(0) < ctx_tiles
    parts = [jnp.where(is_ctx, c_ref[hb], l_ref[hb])
             for c_ref, l_ref in ((oac_ref, oal_ref), (obc_ref, obl_ref)) for hb in range(DN_HEADS)]
    mix = jnp.concatenate(parts, axis=-1)
    out = jnp.dot(mix, w_scr[...], preferred_element_type=F32)
    _moe_input(x_ref[...] + gate_ref[...] * out, first, tail_in, tail_out, tail_scr)


def _tail_specs(tm):
    const = lambda shape: pl.BlockSpec(shape, lambda i: (0,) * len(shape))
    in_specs = [_mod_spec(2, tm), const((1, D_MODEL)), _mod_spec(4, tm), _mod_spec(3, tm),
                const((D_MODEL, LANES)), const((1, LANES))]
    out_specs = [pl.BlockSpec((tm, D_MODEL), lambda i: (i, 0)),
                 pl.BlockSpec((tm, _PACK_W), lambda i: (i, 0)),
                 pl.BlockSpec((tm, LANES), lambda i: (i, 0)),
                 const((1, LANES))]
    out_shape = [jax.ShapeDtypeStruct((N_TOK, D_MODEL), F32),
                 jax.ShapeDtypeStruct((N_TOK, _PACK_W), jnp.int32),
                 jax.ShapeDtypeStruct((N_TOK, LANES), F32),
                 jax.ShapeDtypeStruct((1, LANES), F32)]
    scratch = [pltpu.VMEM((tm, tm), BF16), pltpu.VMEM((1, LANES), F32)]
    return in_specs, out_specs, out_shape, scratch


def _router_weights(w_rg, b_rg, w_re, b_re):
    pad = LANES - N_EGROUPS - N_EXPERTS
    w = jnp.concatenate([w_rg, w_re, jnp.zeros((D_MODEL, pad), F32)], axis=1)
    b = jnp.concatenate([b_rg, b_re, jnp.zeros((pad,), F32)]).reshape(1, LANES)
    return w.astype(BF16), b


def _even_out(oa_ctx, ob_ctx, oa_lat, ob_lat, x, w_out, layer, mods, g2, router):
    tm = 512
    ctx_tiles = N_CTX // tm
    tail_in, out_specs, out_shape, tail_scr = _tail_specs(tm)
    ctxblk = pl.BlockSpec((DN_HEADS, tm, LANES), lambda i: (0, jnp.minimum(i, ctx_tiles - 1), 0))
    latblk = pl.BlockSpec((DN_HEADS, tm, LANES), lambda i: (0, jnp.maximum(i - ctx_tiles, 0), 0))
    return pl.pallas_call(
        functools.partial(_even_out_kernel, ctx_tiles=ctx_tiles), grid=(N_TOK // tm,),
        in_specs=[ctxblk, ctxblk, latblk, latblk, pl.BlockSpec((tm, D_MODEL), lambda i: (i, 0)),
                  pl.BlockSpec((None, D_MODEL, D_MODEL), lambda i: (layer, 0, 0))] + tail_in,
        out_specs=out_specs, out_shape=out_shape,
        scratch_shapes=[pltpu.VMEM((D_MODEL, D_MODEL), BF16)] + tail_scr,
        compiler_params=_params(("arbitrary",)), name="even_out",
    )(oa_ctx, ob_ctx, oa_lat, ob_lat, x, w_out, mods, g2.reshape(1, D_MODEL), mods, mods, *router)


def _gelu_tanh(x):
    c = 0.7978845608028654
    hx = 0.5 * x
    return hx + hx * jnp.tanh(x * (c + (c * 0.044715) * (x * x)))


def _sgu_kernel(x_ref, *rest, tm, n_pend):
    pend, rest = rest[:n_pend], rest[n_pend:]
    g1_ref, sh1_ref, sc1_ref, win_ref, lng_ref, lnb_ref, ws_ref, bst_ref, wout_ref, gate_ref = rest[:10]
    rest = rest[10:]
    tail_in, tail_out, (v_scr, m_scr), tail_scr = rest[:5], rest[5:9], rest[9:11], rest[11:]
    first = pl.program_id(0) == 0
    x = _with_pending(x_ref, pend)
    h = (_rms(x, g1_ref[...]) * (1.0 + sc1_ref[...]) + sh1_ref[...]).astype(BF16)

    def proj_u(g):
        return jnp.dot(h, win_ref[:, g * SG_GW:(g + 1) * SG_GW], preferred_element_type=F32)

    v = _gelu_tanh(jnp.dot(h, win_ref[:, SG_W:], preferred_element_type=F32))
    mu = jnp.mean(v, axis=-1, keepdims=True)
    vc = v - mu
    var = jnp.mean(vc * vc, axis=-1, keepdims=True)
    v_scr[...] = (vc * lax.rsqrt(var + EPS) * lng_ref[...] + lnb_ref[...]).astype(BF16)

    u_next = proj_u(0)
    for g in range(SG_GROUPS):
        cs = slice(g * SG_GW, (g + 1) * SG_GW)
        u_raw, u_next = u_next, (proj_u(g + 1) if g + 1 < SG_GROUPS else None)
        w_sp = ws_ref[g].astype(BF16)
        chunks = [slice(c * SG_CHUNK, (c + 1) * SG_CHUNK) for c in range(tm // SG_CHUNK)]
        sps = [jnp.dot(w_sp, v_scr[rs, cs], preferred_element_type=F32) for rs in chunks]
        u = _gelu_tanh(u_raw)
        for rs, sp in zip(chunks, sps):
            m_scr[rs, cs] = (u[rs] * (sp + bst_ref[:, g:g + 1])).astype(BF16)
    out = jnp.dot(m_scr[...], wout_ref[...], preferred_element_type=F32)
    _moe_input(x + gate_ref[...] * out, first, tail_in, tail_out, tail_scr)


def _sgu_layer(x, pend, mods, g1, w_in, ln_g, ln_b, w_s, b_s, w_out, g2, router):
    tm = 512
    pend = tuple(pend) if pend else ()
    tail_in, out_specs, out_shape, tail_scr = _tail_specs(tm)
    const = lambda shape: pl.BlockSpec(shape, lambda i: (0,) * len(shape))
    held = lambda shape: pl.BlockSpec(shape, lambda i: (0,) * len(shape), pipeline_mode=pl.Buffered(1))
    return pl.pallas_call(
        functools.partial(_sgu_kernel, tm=tm, n_pend=len(pend)), grid=(N_TOK // tm,),
        in_specs=[pl.BlockSpec((tm, D_MODEL), lambda i: (i, 0))] + (_pending_specs(tm) if pend else [])
        + [const((1, D_MODEL)), _mod_spec(0, tm), _mod_spec(1, tm),
                  held((D_MODEL, 2 * SG_W)), const((1, SG_W)), const((1, SG_W)),
                  const((SG_GROUPS, SG_CHUNK, SG_CHUNK)), const((SG_CHUNK, SG_GROUPS)),
                  held((SG_W, D_MODEL))] + tail_in,
        out_specs=out_specs, out_shape=out_shape,
        scratch_shapes=[pltpu.VMEM((tm, SG_W), BF16), pltpu.VMEM((tm, SG_W), BF16)] + tail_scr,
        compiler_params=_params(("arbitrary",)), name="sgu_layer",
    )(x, *pend, g1.reshape(1, D_MODEL), mods, mods, w_in.astype(BF16), ln_g.reshape(1, SG_W), ln_b.reshape(1, SG_W),
      w_s, b_s.T, w_out.astype(BF16), mods, g2.reshape(1, D_MODEL), mods, mods, *router)


def _plan(rec, cnt):
    e_idx = rec[:, _R_E:_R_E + 2].astype(jnp.int32)
    rank = rec[:, _R_RANK:_R_RANK + 2].astype(jnp.int32)
    counts = cnt[0, _LOGIT0:_LOGIT0 + N_EXPERTS].astype(jnp.int32)
    padded = (counts + MOE_BLK - 1) // MOE_BLK * MOE_BLK
    pad_end = jnp.cumsum(padded)
    pad_start = pad_end - padded
    hit = e_idx[:, :, None] == jnp.arange(N_EXPERTS, dtype=jnp.int32)[None, None, :]
    dest = jnp.sum(jnp.where(hit, pad_start[None, None, :], 0), axis=-1) + rank
    blk0 = jnp.arange(MOE_NBLK, dtype=jnp.int32) * MOE_BLK
    blk_e = jnp.minimum(jnp.sum((pad_end[None, :] <= blk0[:, None]).astype(jnp.int32), axis=-1),
                        N_EXPERTS - 1)
    n_used = (pad_end[-1] // MOE_BLK).astype(jnp.int32).reshape(1)
    owns = counts > 0
    slot_of = (jnp.cumsum(owns.astype(jnp.int32)) - 1) % _W_SLOTS
    ids = jnp.arange(N_EXPERTS, dtype=jnp.int32)
    later = jnp.logical_and(owns[None, :], ids[None, :] > ids[:, None])
    next_of = jnp.min(jnp.where(later, ids[None, :], N_EXPERTS), axis=-1)
    next2_of = jnp.concatenate([next_of, jnp.full((1,), N_EXPERTS, jnp.int32)])[next_of]
    ahead = jnp.stack([next_of, next2_of], axis=0)
    ahead = jnp.where(ahead == N_EXPERTS, -1, ahead)
    return dest, blk_e, n_used, slot_of[blk_e], ahead[:, blk_e].reshape(-1)


_W_PARTS = 4
_W_SLOTS = 3


def _expert_kernel(blk_e_ref, n_used_ref, slot_ref, next_ref, x_ref, wg_hbm, wu_hbm, wd_hbm, o_ref,
                   wg_buf, wu_buf, wd_buf, wg_scr, wu_scr, wd_scr, sems, *, layer):
    j = pl.program_id(0)
    e = blk_e_ref[j]
    slot = slot_ref[j]
    fresh = jnp.logical_or(j == 0, e != blk_e_ref[jnp.maximum(j - 1, 0)])
    live = j < n_used_ref[0]

    def copies(expert, s):
        out = []
        for m, (hbm, buf) in enumerate(((wg_hbm, wg_buf), (wu_hbm, wu_buf), (wd_hbm, wd_buf))):
            rows = buf.shape[1] // _W_PARTS
            for part in range(_W_PARTS):
                band = pl.ds(part * rows, rows)
                out.append(pltpu.make_async_copy(hbm.at[layer, expert, band], buf.at[s, band],
                                                 sems.at[s, m, part]))
        return out

    def start_if_any(expert, s):
        @pl.when(expert >= 0)
        def _():
            for cp in copies(expert, s):
                cp.start()

    @pl.when(j == 0)
    def _():
        for cp in copies(e, slot):
            cp.start()
        start_if_any(next_ref[j], lax.rem(slot + 1, _W_SLOTS))

    @pl.when(jnp.logical_and(fresh, live))
    def _():
        for cp in copies(e, slot):
            cp.wait()
        start_if_any(next_ref[MOE_NBLK + j], lax.rem(slot + 2, _W_SLOTS))

        wg_scr[...] = wg_buf[slot].astype(BF16)
        wu_scr[...] = wu_buf[slot].astype(BF16)
        wd_scr[...] = wd_buf[slot].astype(BF16)

    @pl.when(live)
    def _():
        x = _unpack_rows(x_ref[...])
        gt = jnp.dot(x, wg_scr[...], preferred_element_type=F32)
        up = jnp.dot(x, wu_scr[...], preferred_element_type=F32)
        hb = (_silu(gt) * up).astype(BF16)
        o_ref[...] = jnp.dot(hb, wd_scr[...], preferred_element_type=F32).astype(o_ref.dtype)

    @pl.when(jnp.logical_not(live))
    def _():
        o_ref[...] = jnp.zeros(o_ref.shape, o_ref.dtype)


def _experts(x_pad, blk_e, n_used, slot, nxt, w_gate, w_up, w_down, layer):
    hbm = pl.BlockSpec(memory_space=pl.ANY)
    grid_spec = pltpu.PrefetchScalarGridSpec(
        num_scalar_prefetch=4, grid=(MOE_NBLK,),
        in_specs=[pl.BlockSpec((MOE_BLK, _PACK_W), lambda j, be, nu, *_: (jnp.minimum(j, nu[0] - 1), 0)),
                  hbm, hbm, hbm],
        out_specs=pl.BlockSpec((MOE_BLK, D_MODEL), lambda j, *_: (j, 0)),
        scratch_shapes=[pltpu.VMEM((_W_SLOTS, D_MODEL, D_EXPERT), F32),
                        pltpu.VMEM((_W_SLOTS, D_MODEL, D_EXPERT), F32),
                        pltpu.VMEM((_W_SLOTS, D_EXPERT, D_MODEL), F32),
                        pltpu.VMEM((D_MODEL, D_EXPERT), BF16), pltpu.VMEM((D_MODEL, D_EXPERT), BF16),
                        pltpu.VMEM((D_EXPERT, D_MODEL), BF16),
                        pltpu.SemaphoreType.DMA((_W_SLOTS, 3, _W_PARTS))])
    return pl.pallas_call(
        functools.partial(_expert_kernel, layer=layer), grid_spec=grid_spec,
        out_shape=jax.ShapeDtypeStruct((MOE_NBLK * MOE_BLK, D_MODEL), BF16),
        compiler_params=_params(("arbitrary",)), name="experts",
    )(blk_e, n_used, slot, nxt, x_pad, w_gate, w_up, w_down)


def _final_kernel(x_ref, y2_ref, rec_ref, gate_ref, fg_ref, o_ref):
    o_ref[...] = _rms(_with_pending(x_ref, (y2_ref, rec_ref, gate_ref)), fg_ref[...])


def _final_norm(x, pend, final_g, row0, n_rows):
    tm = 512
    tile0 = row0 // tm
    return pl.pallas_call(
        _final_kernel, grid=(n_rows // tm,),
        in_specs=[pl.BlockSpec((tm, D_MODEL), lambda i: (i + tile0, 0))] + _pending_specs(tm, tile0)
        + [pl.BlockSpec((1, D_MODEL), lambda i: (0, 0))],
        out_specs=pl.BlockSpec((tm, D_MODEL), lambda i: (i, 0)),
        out_shape=jax.ShapeDtypeStruct((n_rows, D_MODEL), F32),
        compiler_params=_params(("parallel",)), name="final_norm",
    )(x, *pend, final_g.reshape(1, D_MODEL))


_SC_WORKERS = 32
_SC_CORES = 2
_SC_ROWS = 64


def _dispatch_rows(hp, dest):
    n, width = hp.shape
    per_w = n // _SC_WORKERS
    n_ch = per_w // _SC_ROWS
    idx = dest.T.reshape(2, _SC_WORKERS, n_ch, _SC_ROWS)
    mesh = plsc.VectorSubcoreMesh(core_axis_name="c", subcore_axis_name="s")

    @functools.partial(
        pl.kernel, mesh=mesh, out_type=jax.ShapeDtypeStruct((MOE_NBLK * MOE_BLK, width), hp.dtype),
        scratch_types=[pltpu.VMEM((n_ch, _SC_ROWS), jnp.int32), pltpu.VMEM((n_ch, _SC_ROWS), jnp.int32),
                       pltpu.VMEM((_SC_ROWS, width), hp.dtype)], name="dispatch_rows")
    def scatter(h_hbm, idx_hbm, out_hbm, i0_v, i1_v, rows_v):
        wid = lax.axis_index("s") * _SC_CORES + lax.axis_index("c")
        pltpu.sync_copy(idx_hbm.at[0, wid], i0_v)
        pltpu.sync_copy(idx_hbm.at[1, wid], i1_v)

        @pl.loop(0, n_ch)
        def _(g):
            pltpu.sync_copy(h_hbm.at[pl.ds(wid * per_w + g * _SC_ROWS, _SC_ROWS)], rows_v)
            pltpu.sync_copy(rows_v, out_hbm.at[i0_v.at[g]])
            pltpu.sync_copy(rows_v, out_hbm.at[i1_v.at[g]])

    return scatter(hp, idx)


def _moe(h, rec, cnt, mods, w_gate, w_up, w_down, layer):
    dest, blk_e, n_used, slot, nxt = _plan(rec, cnt)
    y_pad = _experts(_dispatch_rows(h, dest), blk_e, n_used, slot, nxt, w_gate, w_up, w_down, layer)
    order = dest.reshape(N_TOK // _PEND_TM, _PEND_TM, 2).transpose(0, 2, 1).reshape(-1)
    return y_pad[order], rec, mods


def kernel(x_prompt, x_sample, c, cache_k, cache_v, state_delta, c_ctx, ada_w, ada_b, norm1_g, norm2_g, final_g,
           ev_w_in, ev_w_out, ev_conv_w, ev_a_log, ev_dt_bias, ev_onorm_g, ev_rpb, od_w_in, od_ln_g, od_ln_b,
           od_w_s, od_b_s, od_w_out, moe_w_rg, moe_b_rg, moe_w_re, moe_b_re, moe_w_gate, moe_w_up, moe_w_down):
    x = (x_prompt.reshape(N_CTX, D_MODEL), x_sample.reshape(N_LAT, D_MODEL))
    cond = jnp.concatenate([c_ctx[None, :], c, jnp.zeros((N_COND - 1 - DEC_BATCH, D_MODEL), F32)], axis=0)
    mods_all = _ada_mods(cond, ada_w, ada_b)
    kctx_all = cache_k.reshape(DEC_BATCH, -1, PAST_LEN, NA_HEADS * NA_HD)
    vctx_all = cache_v.reshape(DEC_BATCH, -1, PAST_LEN, NA_HEADS * NA_HD)

    ks, vs, ss = [], [], []
    pend = None
    for l in range(DEPTH):
        mods = mods_all[l]
        router = _router_weights(moe_w_rg[l], moe_b_rg[l], moe_w_re[l], moe_b_re[l])
        if l % 2 == 0:
            e = l // 2
            proj, ab, kv, x = _even_proj(x, pend, mods, norm1_g[l], ev_w_in[e])
            dn = (proj, ab, ev_conv_w[e], ev_a_log[e], ev_dt_bias[e], ev_onorm_g[e])
            oa_ctx, s_fin = _delta_heads(*dn, SEQ, BATCH, 0, None)
            oa_lat, _ = _delta_heads(*dn, DEC_SEQ, DEC_BATCH, N_CTX // DEC_SEQ, (state_delta, e))
            ob_ctx = _ctx_attention(proj)
            ob_lat = _na_attention(proj, kctx_all, vctx_all, e, ev_rpb[e])
            x, h, rec, cnt = _even_out(oa_ctx, ob_ctx, oa_lat, ob_lat, x, ev_w_out, e, mods, norm2_g[l],
                                       router)
            na_w = NA_HEADS * NA_HD
            ks.append(kv[:N_CTX, :na_w].reshape(BATCH, SEQ, NA_HEADS, NA_HD))
            vs.append(kv[:N_CTX, na_w:].reshape(BATCH, SEQ, NA_HEADS, NA_HD))
            ss.append(s_fin)
        else:
            o = l // 2
            x, h, rec, cnt = _sgu_layer(x, pend, mods, norm1_g[l], od_w_in[o], od_ln_g[o], od_ln_b[o],
                                        od_w_s[o], od_b_s[o], od_w_out[o], norm2_g[l], router)
        pend = _moe(h, rec, cnt, mods, moe_w_gate, moe_w_up, moe_w_down, l)
    y_prompt = _final_norm(x, pend, final_g, 0, N_CTX).reshape(BATCH, SEQ, D_MODEL)
    y_sample = _final_norm(x, pend, final_g, N_CTX, N_LAT).reshape(DEC_BATCH, DEC_SEQ, D_MODEL)
    return (y_prompt, y_sample, jnp.stack(ks, axis=1), jnp.stack(vs, axis=1), jnp.stack(ss, axis=1))
```

```python
import functools

import jax
import jax.numpy as jnp
from jax import lax
from jax.experimental import pallas as pl
from jax.experimental.pallas import tpu as pltpu
from jax.experimental.pallas import tpu_sc as plsc

F32 = jnp.float32
BF16 = jnp.bfloat16

D_MODEL = 1024
BATCH = 16
SEQ = 256
DEPTH = 4
DEC_BATCH = 4
DEC_SEQ = 2048
PAST_LEN = 512
GRID_W = 64
EPS = 1e-6
NEG_INF = -1e30

DN_HEADS = 4
DN_DK = 128
DN_CHUNK = 64
NA_HEADS = 8
NA_HD = 64
NA_ROWS = 8
NA_COLS = 16
SG_CHUNK = 128
SG_GROUPS = 8
SG_W = 2 * D_MODEL
SG_GW = SG_W // SG_GROUPS
N_EGROUPS = 4
EXP_PER_GROUP = 8
N_EXPERTS = 32
D_EXPERT = 512

N_CTX = BATCH * SEQ
N_LAT = DEC_BATCH * DEC_SEQ
N_TOK = N_CTX + N_LAT
N_COND = 8
PROJ_W = 4096
LANES = 128
MOE_BLK = 256
MOE_NBLK = -(-(2 * N_TOK + N_EXPERTS * (MOE_BLK - 1)) // MOE_BLK)
VMEM_LIMIT = 56 * 1024 * 1024

_QA, _KA, _VA, _ZA, _QB, _KB, _VB = 0, 4, 8, 12, 16, 20, 24


def _params(sem):
    return pltpu.CompilerParams(dimension_semantics=sem, vmem_limit_bytes=VMEM_LIMIT)


def _bdot(a, b):
    return jnp.dot(a.astype(BF16), b.astype(BF16), preferred_element_type=F32)


def _bdot_nt(a, b):
    return lax.dot_general(a.astype(BF16), b.astype(BF16), (((1,), (1,)), ((), ())),
                           preferred_element_type=F32)


def _bdot_tn(a, b):
    return lax.dot_general(a.astype(BF16), b.astype(BF16), (((0,), (0,)), ((), ())),
                           preferred_element_type=F32)


def _split2(a):
    p0 = a.astype(BF16)
    return p0, (a - p0.astype(F32)).astype(BF16)


def _dot3(a, b):
    ah = a.astype(BF16)
    al = (a - ah.astype(F32)).astype(BF16)
    bh = b.astype(BF16)
    bl = (b - bh.astype(F32)).astype(BF16)
    return (jnp.dot(ah, bh, preferred_element_type=F32) + jnp.dot(ah, bl, preferred_element_type=F32)
            + jnp.dot(al, bh, preferred_element_type=F32))


def _mask_bf16(m01):
    return jnp.where(m01, 1.0, 0.0).astype(BF16)


def _xdot(m01, a):
    m = _mask_bf16(m01)
    p0, p1 = _split2(a)
    return jnp.dot(m, p0, preferred_element_type=F32) + jnp.dot(m, p1, preferred_element_type=F32)


def _xdot_r(a, m01):
    m = _mask_bf16(m01)
    p0, p1 = _split2(a)
    return jnp.dot(p0, m, preferred_element_type=F32) + jnp.dot(p1, m, preferred_element_type=F32)


def _sigmoid(x):
    return 0.5 * jnp.tanh(0.5 * x) + 0.5


def _silu(x):
    hx = 0.5 * x
    return hx + hx * jnp.tanh(hx)


def _rms(x, g):
    return x * lax.rsqrt(jnp.mean(x * x, axis=-1, keepdims=True) + EPS) * g


def _cond_index(row):
    return jnp.where(row < N_CTX, 0, 1 + (row - N_CTX) // DEC_SEQ)


def _mod_spec(k, tm, tile0=0):
    return pl.BlockSpec((None, None, 1, D_MODEL), lambda i, *_: (_cond_index((i + tile0) * tm), k, 0, 0))


def _ada_kernel(c_ref, w_ref, b_ref, o_ref):
    o_ref[...] = _bdot(_silu(c_ref[...]), w_ref[...]) + b_ref[...]


def _ada_mods(cond, ada_w, ada_b):
    tn = 1536
    out = pl.pallas_call(
        _ada_kernel, grid=(DEPTH, 6 * D_MODEL // tn),
        in_specs=[pl.BlockSpec((N_COND, D_MODEL), lambda l, j: (0, 0)),
                  pl.BlockSpec((None, D_MODEL, tn), lambda l, j: (l, 0, j)),
                  pl.BlockSpec((None, 1, tn), lambda l, j: (l, 0, j))],
        out_specs=pl.BlockSpec((None, N_COND, tn), lambda l, j: (l, 0, j)),
        out_shape=jax.ShapeDtypeStruct((DEPTH, N_COND, 6 * D_MODEL), F32),
        compiler_params=_params(("parallel", "parallel")), name="ada_mods",
    )(cond, ada_w, ada_b.reshape(DEPTH, 1, 6 * D_MODEL))
    return out.reshape(DEPTH, N_COND, 6, 1, D_MODEL)


_EV_TN = 512
_EV_W = 7 * DN_HEADS * LANES
_KV_COL0 = _KB * LANES


def _with_pending(x_ref, pend):
    if not pend:
        return x_ref[...]
    y2_ref, rec_ref, gate_ref = pend
    rec = rec_ref[...]
    tm = rec.shape[0]
    y = (rec[:, _R_W:_R_W + 1] * y2_ref[:tm, :].astype(F32)
         + rec[:, _R_W + 1:_R_W + 2] * y2_ref[tm:, :].astype(F32))
    return x_ref[...] + gate_ref[...] * y


_PEND_TM = 512


def _pending_specs(tm, tile0=0):
    assert tm == _PEND_TM
    return [pl.BlockSpec((2 * tm, D_MODEL), lambda i: (i + tile0, 0)),
            pl.BlockSpec((tm, LANES), lambda i: (i + tile0, 0)), _mod_spec(5, tm, tile0)]


def _even_proj_kernel(x_ref, *rest, n_pend, ctx_tiles):
    pend, (g_ref, sh_ref, sc_ref, w_ref, wab_ref, o_ref, ab_ref, kv_ref) = rest[:n_pend], rest[n_pend:n_pend + 8]
    if n_pend == 1:
        x = jnp.where(pl.program_id(0) < ctx_tiles, x_ref[...], pend[0][...])
    else:
        x = _with_pending(x_ref, pend)
    if pend:
        rest[n_pend + 8][...] = x
    h = (_rms(x, g_ref[...]) * (1.0 + sc_ref[...]) + sh_ref[...]).astype(BF16)
    ab_ref[...] = jnp.dot(h, wab_ref[...], preferred_element_type=F32)
    for j in range(_EV_W // _EV_TN):
        c0 = j * _EV_TN
        y = jnp.dot(h, w_ref[:, c0:c0 + _EV_TN], preferred_element_type=F32)
        for c in range(_EV_TN // LANES):
            o_ref[c0 // LANES + c] = y[:, c * LANES:(c + 1) * LANES].astype(BF16)
        if c0 >= _KV_COL0:
            kv_ref[:, c0 - _KV_COL0:c0 - _KV_COL0 + _EV_TN] = y


def _even_proj(x, pend, mods, g, w_in):
    tm = 512
    ctx_tiles = N_CTX // tm
    rows = pl.BlockSpec((tm, D_MODEL), lambda i: (i, 0))
    if isinstance(x, tuple):
        x, pend = x[0], (x[1],)
        x_specs = [pl.BlockSpec((tm, D_MODEL), lambda i: (jnp.minimum(i, ctx_tiles - 1), 0)),
                   pl.BlockSpec((tm, D_MODEL), lambda i: (jnp.maximum(i - ctx_tiles, 0), 0))]
    else:
        pend = tuple(pend) if pend else ()
        x_specs = [rows] + (_pending_specs(tm) if pend else [])
    n_ab = 4 * DN_HEADS
    ab0 = 4 * DN_HEADS * DN_DK
    w_main = jnp.concatenate([w_in[:, :ab0], w_in[:, ab0 + n_ab:]], axis=1).astype(BF16)
    w_ab = jnp.concatenate([w_in[:, ab0:ab0 + n_ab], jnp.zeros((D_MODEL, LANES - n_ab), F32)],
                           axis=1).astype(BF16)
    held = lambda shape: pl.BlockSpec(shape, lambda i: (0,) * len(shape), pipeline_mode=pl.Buffered(1))
    out_specs = [pl.BlockSpec((_EV_W // LANES, tm, LANES), lambda i: (0, i, 0)),
                 pl.BlockSpec((tm, LANES), lambda i: (i, 0)),
                 pl.BlockSpec((tm, 2 * NA_HEADS * NA_HD), lambda i: (i, 0))]
    out_shape = [jax.ShapeDtypeStruct((_EV_W // LANES, N_TOK, LANES), BF16),
                 jax.ShapeDtypeStruct((N_TOK, LANES), F32),
                 jax.ShapeDtypeStruct((N_TOK, 2 * NA_HEADS * NA_HD), F32)]
    if pend:
        out_specs.append(rows)
        out_shape.append(jax.ShapeDtypeStruct((N_TOK, D_MODEL), F32))
    res = pl.pallas_call(
        functools.partial(_even_proj_kernel, n_pend=len(pend), ctx_tiles=ctx_tiles), grid=(N_TOK // tm,),
        in_specs=x_specs
        + [pl.BlockSpec((1, D_MODEL), lambda i: (0, 0)), _mod_spec(0, tm), _mod_spec(1, tm),
           held((D_MODEL, _EV_W)), held((D_MODEL, LANES))],
        out_specs=out_specs, out_shape=out_shape,
        compiler_params=_params(("parallel",)), name="even_proj",
    )(x, *pend, g.reshape(1, D_MODEL), mods, mods, w_main, w_ab)
    return (*res[:3], res[3] if pend else x)


_CHUNK_SHIFT = DN_CHUNK.bit_length() - 1
_CUM_ROWS = 256
_DN_CHAINS = 16
_DN_SHORT = 256
_MQ_ROWS = DN_DK + DN_CHUNK


def _dn_kernel(*refs, T, HB, has_s0, want_state):
    it = iter(refs)
    q_ref, k_ref, v_ref, z_ref, ab_ref = (next(it) for _ in range(5))
    cwq_ref, cwk_ref, cwv_ref, alog_ref, dtb_ref, og_ref = (next(it) for _ in range(6))
    s0_ref = next(it) if has_s0 else None
    o_ref = next(it)
    sfin_ref = next(it) if want_state else None
    qc, kc, vc, gsc, bsc, osc, b_s, mq_s = (next(it) for _ in range(8))

    C = DN_CHUNK
    n = T // C
    h0 = pl.program_id(1) * HB

    row = lax.broadcasted_iota(jnp.int32, (T, 1), 0)

    def conv(x_ref, cw_ref, hh):
        x = x_ref[hh].astype(F32)
        cw = cw_ref[:, hh * LANES:(hh + 1) * LANES]
        xp = jnp.where(row == 0, 0.0, pltpu.roll(x, 1, 0))
        xn = jnp.where(row == T - 1, 0.0, pltpu.roll(x, T - 1, 0))
        return _silu(cw[0:1, :] * xp + cw[1:2, :] * x + cw[2:3, :] * xn)

    def l2n(x):
        return x * lax.rsqrt(jnp.sum(x * x, axis=-1, keepdims=True) + EPS)

    ab = ab_ref[...]
    lane = lax.broadcasted_iota(jnp.int32, (1, LANES), 1)
    dtb = jnp.zeros((1, LANES), F32)
    alog = jnp.zeros((1, LANES), F32)
    for d in range(2):
        for hq in range(DN_HEADS):
            dtb = jnp.where(lane == d * DN_HEADS + hq, dtb_ref[d, hq], dtb)
            alog = jnp.where(lane == d * DN_HEADS + hq, alog_ref[d, hq], alog)
    xs = ab + dtb
    g_all = -jnp.exp(alog) * (jnp.maximum(xs, 0.0) + jnp.log(1.0 + jnp.exp(-jnp.abs(xs))))
    beta_all = _sigmoid(ab)

    sel_r = lax.broadcasted_iota(jnp.int32, (LANES, LANES), 0)
    for hh in range(HB):
        qc[hh] = l2n(conv(q_ref, cwq_ref, hh)) * (DN_DK ** -0.5)
        kc[hh] = l2n(conv(k_ref, cwk_ref, hh))
        vc[hh] = conv(v_ref, cwv_ref, hh)
        hd = h0 + hh
        for d in range(2):
            gsc[hh, d] = _xdot_r(g_all, sel_r == d * DN_HEADS + hd)
            bsc[hh, d] = _xdot_r(beta_all, sel_r == 2 * DN_HEADS + d * DN_HEADS + hd)

    pr = lax.broadcasted_iota(jnp.int32, (_CUM_ROWS, _CUM_ROWS), 0)
    pc = lax.broadcasted_iota(jnp.int32, (_CUM_ROWS, _CUM_ROWS), 1)
    same = lax.shift_right_logical(pr, _CHUNK_SHIFT) == lax.shift_right_logical(pc, _CHUNK_SHIFT)
    cum_mask = (jnp.logical_and(same, pc <= pr), jnp.logical_and(same, pc >= pr))

    def cum_body(i, carry):
        sl = pl.ds(pl.multiple_of(i * _CUM_ROWS, _CUM_ROWS), _CUM_ROWS)
        for hh in range(HB):
            for d in range(2):
                gsc[hh, d, sl, :] = _xdot(cum_mask[d], gsc[hh, d, sl, :])
        return carry

    lax.fori_loop(0, T // _CUM_ROWS, cum_body, 0)

    ri = lax.broadcasted_iota(jnp.int32, (C, C), 0)
    ci = lax.broadcasted_iota(jnp.int32, (C, C), 1)
    eye = (ri == ci).astype(F32)

    def prepare(items):
        lows, decays = [], []
        kk, qk = {}, {}
        for hh, d, c, slot in items:
            if (hh, slot) not in kk:
                sl = pl.ds(pl.multiple_of(c * C, C), C)
                k = kc[hh, sl, :]
                kk[hh, slot] = _bdot_nt(k, k)
                qk[hh, slot] = _bdot_nt(qc[hh, sl, :], k)
        for hh, d, c, slot in items:
            sl = pl.ds(pl.multiple_of(c * C, C), C)
            gc = gsc[hh, d, sl, :]
            incl = (ci <= ri) if d == 0 else (ci >= ri)
            strict = (ci < ri) if d == 0 else (ci > ri)
            gr = jnp.transpose(gc)[0:1, :C]
            decay = jnp.where(incl, jnp.exp(jnp.where(incl, gc[:, :C] - gr, 0.0)), 0.0)
            lows.append(jnp.where(strict, bsc[hh, d, sl, :C] * kk[hh, slot] * decay, 0.0))
            decays.append(decay)
        ts = [eye - low for low in lows]
        ps = lows
        for step in range(_CHUNK_SHIFT - 1):
            ps = [_dot3(p, p) for p in ps]
            ts = [t + _dot3(t, p) for t, p in zip(ts, ps)]
        wus, kds, attns, qds = [], [], [], []
        for (hh, d, c, slot), t, decay in zip(items, ts, decays):
            sl = pl.ds(pl.multiple_of(c * C, C), C)
            k, gc, beta = kc[hh, sl, :], gsc[hh, d, sl, :], bsc[hh, d, sl, :]
            eg = jnp.exp(gc)
            uw = _bdot(t, jnp.concatenate([vc[hh, sl, :] * beta, k * beta * eg], axis=-1))
            last = gc[C - 1:C, :] if d == 0 else gc[0:1, :]
            wus.append(jnp.concatenate([uw[:, LANES:], uw[:, :LANES]], axis=-1).astype(BF16))
            kds.append((k * jnp.exp(last - gc)).astype(BF16))
            attns.append((qk[hh, slot] * decay).astype(BF16))
            qds.append(qc[hh, sl, :] * eg)
        kdwus = [lax.dot_general(kd, wu, (((0,), (0,)), ((), ())), preferred_element_type=F32)
                 for kd, wu in zip(kds, wus)]
        awus = [jnp.dot(attn, wu, preferred_element_type=F32) for attn, wu in zip(attns, wus)]
        for (hh, d, c, slot), kdwu, awu, qd in zip(items, kdwus, awus, qds):
            mq0 = pl.multiple_of(c * _MQ_ROWS, _MQ_ROWS)
            mq_s[hh, d, pl.ds(mq0, DN_DK), :] = kdwu[:, :LANES].astype(BF16)
            mq_s[hh, d, pl.ds(mq0 + DN_DK, C), :] = (qd - awu[:, :LANES]).astype(BF16)
            b_s[hh, d, pl.ds(pl.multiple_of(c * DN_DK, DN_DK), DN_DK), :] = kdwu[:, LANES:]
            osc[hh, d, pl.ds(pl.multiple_of(c * C, C), C), :] = awu[:, LANES:]

    n_prep = min(n, _DN_CHAINS // 2)
    h_prep = max(1, min(HB, _DN_CHAINS // (2 * n_prep)))

    def prep_body(i, carry):
        for hg in range(0, HB, h_prep):
            prepare([(hh, d, i * n_prep + j, j) for hh in range(hg, hg + h_prep) for j in range(n_prep)
                     for d in range(2)])
        return carry

    lax.fori_loop(0, n // n_prep, prep_body, 0)

    def body(i, carry):
        chains = [(hh, d, i if d == 0 else n - 1 - i) for hh in range(HB) for d in range(2)]
        mss = [jnp.dot(mq_s[hh, d, pl.ds(pl.multiple_of(c * _MQ_ROWS, _MQ_ROWS), _MQ_ROWS), :],
                       S.astype(BF16), preferred_element_type=F32)
               for (hh, d, c), S in zip(chains, carry)]
        new = []
        for (hh, d, c), S, ms in zip(chains, carry, mss):
            sl = pl.ds(pl.multiple_of(c * C, C), C)
            osc[hh, d, sl, :] = osc[hh, d, sl, :] + ms[DN_DK:]
            last = gsc[hh, d, pl.ds(c * C + (C - 1 if d == 0 else 0), 1), :]
            new.append(S * jnp.exp(last) - ms[:DN_DK]
                       + b_s[hh, d, pl.ds(pl.multiple_of(c * DN_DK, DN_DK), DN_DK), :])
        return tuple(new)

    if has_s0:
        init = tuple(s0_ref[d, hh] for hh in range(HB) for d in range(2))
    else:
        init = tuple(jnp.zeros((DN_DK, LANES), F32) for _ in range(2 * HB))
    fin = lax.fori_loop(0, n, body, init)
    for hh in range(HB):
        if want_state:
            sfin_ref[0, hh] = fin[2 * hh]
            sfin_ref[1, hh] = fin[2 * hh + 1]
        o = osc[hh, 0] + osc[hh, 1]
        o_ref[hh] = (_rms(o, og_ref[...]) * _silu(z_ref[hh].astype(F32))).astype(o_ref.dtype)


def _delta_heads(proj, ab, conv_w, a_log, dt_bias, onorm_g, T, n_seq, row0, s0):
    has_s0 = s0 is not None
    want_state = not has_s0
    hb = DN_HEADS if T <= _DN_SHORT else 2

    def col(cb):
        return pl.BlockSpec((hb, T, LANES), lambda s, h: (cb // hb + h, row0 + s, 0))

    def cw(cb):
        return pl.BlockSpec((3, hb * LANES), lambda s, h: (0, cb // hb + h))

    smem = pl.BlockSpec(memory_space=pltpu.SMEM)
    in_specs = [col(_QA), col(_KA), col(_VA), col(_ZA),
                pl.BlockSpec((T, LANES), lambda s, h: (row0 + s, 0)),
                cw(0), cw(4), cw(8), smem, smem,
                pl.BlockSpec((1, LANES), lambda s, h: (0, 0))]
    args = [proj, proj, proj, proj, ab, conv_w, conv_w, conv_w, a_log, dt_bias,
            onorm_g.reshape(1, LANES)]
    state_spec = pl.BlockSpec((None, 2, hb, DN_DK, LANES), lambda s, h: (s, 0, h, 0, 0))
    if has_s0:
        states, layer = s0
        in_specs.append(pl.BlockSpec((None, None, 2, hb, DN_DK, LANES), lambda s, h: (s, layer, 0, h, 0, 0)))
        args.append(states)
    out_shape = [jax.ShapeDtypeStruct((DN_HEADS, n_seq * T, LANES), BF16)]
    out_specs = [pl.BlockSpec((hb, T, LANES), lambda s, h: (h, s, 0))]
    if want_state:
        out_shape.append(jax.ShapeDtypeStruct((n_seq, 2, DN_HEADS, DN_DK, LANES), F32))
        out_specs.append(state_spec)
    res = pl.pallas_call(
        functools.partial(_dn_kernel, T=T, HB=hb, has_s0=has_s0, want_state=want_state),
        grid=(n_seq, DN_HEADS // hb), in_specs=in_specs, out_specs=out_specs, out_shape=out_shape,
        scratch_shapes=[pltpu.VMEM((hb, T, LANES), F32)] * 3
        + [pltpu.VMEM((hb, 2, T, LANES), F32)] * 3
        + [pltpu.VMEM((hb, 2, T // DN_CHUNK * DN_DK, LANES), F32),
           pltpu.VMEM((hb, 2, T // DN_CHUNK * _MQ_ROWS, LANES), BF16)],
        compiler_params=_params(("parallel", "parallel")), name="delta_heads_%d" % T,
    )(*args)
    return res if want_state else (res[0], None)


def _pair_queries(q, first):
    return jnp.concatenate([jnp.where(first, q, 0.0), jnp.where(first, 0.0, q)], axis=0).astype(BF16)


def _ctx_attn_kernel(q_ref, k_ref, v_ref, o_ref):
    first = lax.broadcasted_iota(jnp.int32, (SEQ, LANES), 1) < NA_HD
    qm = _pair_queries(q_ref[...] * (NA_HD ** -0.5), first)
    s = lax.dot_general(k_ref[...], qm, (((1,), (1,)), ((), ())), preferred_element_type=F32)
    e = jnp.exp(s - jnp.max(s, axis=0, keepdims=True))
    den = jnp.sum(e, axis=0, keepdims=True)
    o = lax.dot_general(e.astype(BF16), v_ref[...], (((0,), (0,)), ((), ())), preferred_element_type=F32)
    o = jnp.where(first, o[:SEQ], o[SEQ:])
    den_t = jnp.transpose(jnp.broadcast_to(den, (LANES, 2 * SEQ)))
    o_ref[...] = (o / jnp.where(first, den_t[:SEQ], den_t[SEQ:])).astype(o_ref.dtype)


def _ctx_attention(proj):
    def col(cb):
        return pl.BlockSpec((None, SEQ, LANES), lambda s, p: (cb + p, s, 0))

    return pl.pallas_call(
        _ctx_attn_kernel, grid=(BATCH, NA_HEADS // 2),
        in_specs=[col(_QB), col(_KB), col(_VB)],
        out_specs=pl.BlockSpec((None, SEQ, LANES), lambda s, p: (p, s, 0)),
        out_shape=jax.ShapeDtypeStruct((NA_HEADS // 2, N_CTX, LANES), BF16),
        compiler_params=_params(("parallel", "parallel")), name="ctx_attention",
    )(proj, proj, proj)


_NA_UNROLL = 4


def _na_kernel(q_ref, k_ref, v_ref, kc_ref, vc_ref, bias_ref, o_ref, kcb_scr, vcb_scr):
    rows = DEC_SEQ // GRID_W
    win = NA_ROWS * GRID_W
    scale = NA_HD ** -0.5
    dn_nt = (((1,), (1,)), ((), ()))
    dn_tn = (((0,), (0,)), ((), ()))

    kcb_scr[...] = kc_ref[...].astype(BF16)
    vcb_scr[...] = vc_ref[...].astype(BF16)
    first = lax.broadcasted_iota(jnp.int32, (GRID_W, LANES), 1) < NA_HD

    def body(it, carry):
        rr = [it * _NA_UNROLL + j for j in range(_NA_UNROLL)]
        rss = [jnp.clip(r - NA_ROWS // 2, 0, rows - NA_ROWS) for r in rr]
        qsls = [pl.ds(pl.multiple_of(r * GRID_W, GRID_W), GRID_W) for r in rr]
        wsls = [pl.ds(pl.multiple_of(rs * GRID_W, GRID_W), win) for rs in rss]
        qms, s_wins, s_ctxs = [], [], []
        for r, rs, qsl, wsl in zip(rr, rss, qsls, wsls):
            qm = _pair_queries(q_ref[qsl, :] * scale, first)
            bias = jnp.concatenate([bias_ref[NA_ROWS - 1 - (r - rs) + i] for i in range(NA_ROWS)], axis=0)
            s_wins.append(lax.dot_general(k_ref[wsl, :], qm, dn_nt, preferred_element_type=F32) + bias)
            s_ctxs.append(lax.dot_general(kcb_scr[...], qm, dn_nt, preferred_element_type=F32))
        ms = [jnp.maximum(jnp.max(sw, axis=0, keepdims=True), jnp.max(sc, axis=0, keepdims=True))
              for sw, sc in zip(s_wins, s_ctxs)]
        e_wins = [jnp.exp(sw - m) for sw, m in zip(s_wins, ms)]
        e_ctxs = [jnp.exp(sc - m) for sc, m in zip(s_ctxs, ms)]
        dens = [jnp.sum(ew, axis=0, keepdims=True) + jnp.sum(ec, axis=0, keepdims=True)
                for ew, ec in zip(e_wins, e_ctxs)]
        for qsl, wsl, ew, ec, den in zip(qsls, wsls, e_wins, e_ctxs, dens):
            o = (lax.dot_general(ew.astype(BF16), v_ref[wsl, :], dn_tn, preferred_element_type=F32)
                 + lax.dot_general(ec.astype(BF16), vcb_scr[...], dn_tn, preferred_element_type=F32))
            o = o / jnp.transpose(jnp.broadcast_to(den, (LANES, LANES)))
            o_ref[qsl, :] = jnp.where(first, o[:GRID_W], o[GRID_W:]).astype(o_ref.dtype)
        return carry

    lax.fori_loop(0, rows // _NA_UNROLL, body, 0)


def _na_bias_table(rpb):
    col = jnp.arange(GRID_W)
    cs = jnp.clip(col - NA_COLS // 2, 0, GRID_W - NA_COLS)
    col_ok = (col[None, :] >= cs[:, None]) & (col[None, :] < cs[:, None] + NA_COLS)
    dc = jnp.clip(col[None, :] - col[:, None] + NA_COLS - 1, 0, 2 * NA_COLS - 2)
    onehot = (dc.T[None, :, :] == jnp.arange(2 * NA_COLS - 1)[:, None, None]).astype(F32)
    t = jnp.einsum('hrd,dkq->hrkq', rpb.astype(F32), onehot, precision=lax.Precision.HIGHEST)
    t = jnp.where(col_ok.T[None, None], t, NEG_INF)
    t = t.reshape(NA_HEADS // 2, 2, 2 * NA_ROWS - 1, GRID_W, GRID_W)
    return jnp.concatenate([t[:, 0], t[:, 1]], axis=-1)


def _na_attention(proj, kctx, vctx, layer, rpb):
    blk = N_CTX // DEC_SEQ

    def col(cb):
        return pl.BlockSpec((None, DEC_SEQ, LANES), lambda b, p: (cb + p, blk + b, 0))

    ctx = pl.BlockSpec((None, None, PAST_LEN, LANES), lambda b, p: (b, layer, 0, p))
    return pl.pallas_call(
        _na_kernel, grid=(DEC_BATCH, NA_HEADS // 2),
        in_specs=[col(_QB), col(_KB), col(_VB), ctx, ctx,
                  pl.BlockSpec((None, 2 * NA_ROWS - 1, GRID_W, 2 * GRID_W), lambda b, p: (p, 0, 0, 0))],
        out_specs=pl.BlockSpec((None, DEC_SEQ, LANES), lambda b, p: (p, b, 0)),
        out_shape=jax.ShapeDtypeStruct((NA_HEADS // 2, N_LAT, LANES), BF16),
        scratch_shapes=[pltpu.VMEM((PAST_LEN, LANES), BF16), pltpu.VMEM((PAST_LEN, LANES), BF16)],
        compiler_params=_params(("parallel", "parallel")), name="na_attention",
    )(proj, proj, proj, kctx, vctx, _na_bias_table(rpb))


_LOGIT0 = N_EGROUPS
_R_E, _R_W, _R_RANK = 0, 2, 4


def _lane_min_where(mask, lane):
    return jnp.min(jnp.where(mask, lane, LANES), axis=-1, keepdims=True)


def _route_rows(lg, carry_ref, tri_ref):
    big = -3.0e38
    lane = lax.broadcasted_iota(jnp.int32, lg.shape, 1)
    is_g = lane < N_EGROUPS
    gmax = jnp.max(jnp.where(is_g, lg, big), axis=-1, keepdims=True)
    gsum = jnp.sum(jnp.where(is_g, jnp.exp(jnp.where(is_g, lg - gmax, 0.0)), 0.0), axis=-1, keepdims=True)
    pg_top = 1.0 / gsum
    g_idx = _lane_min_where(jnp.logical_and(is_g, lg == gmax), lane)
    in_g = jnp.logical_and(lane >= _LOGIT0, lax.shift_right_arithmetic(lane - _LOGIT0, 3) == g_idx)
    in_g = jnp.logical_and(in_g, lane < _LOGIT0 + N_EXPERTS)
    m1 = jnp.max(jnp.where(in_g, lg, big), axis=-1, keepdims=True)
    i1 = _lane_min_where(jnp.logical_and(in_g, lg == m1), lane)
    rest = jnp.logical_and(in_g, lane != i1)
    m2 = jnp.max(jnp.where(rest, lg, big), axis=-1, keepdims=True)
    i2 = _lane_min_where(jnp.logical_and(rest, lg == m2), lane)
    e2 = jnp.exp(m2 - m1)
    w1 = pg_top * (1.0 / (1.0 + e2))
    w2 = pg_top * (e2 / (1.0 + e2))
    hit1 = lane == i1
    hit2 = lane == i2
    picked = jnp.where(jnp.logical_or(hit1, hit2), 1.0, 0.0)
    before = jnp.dot(tri_ref[...], picked.astype(BF16), preferred_element_type=F32) + carry_ref[...]
    r1 = jnp.sum(jnp.where(hit1, before, 0.0), axis=-1, keepdims=True)
    r2 = jnp.sum(jnp.where(hit2, before, 0.0), axis=-1, keepdims=True)
    carry_ref[...] = carry_ref[...] + jnp.sum(picked, axis=0, keepdims=True)
    rec = jnp.zeros(lg.shape, F32)
    for ln, val in ((_R_E, (i1 - _LOGIT0).astype(F32)), (_R_E + 1, (i2 - _LOGIT0).astype(F32)),
                    (_R_W, w1), (_R_W + 1, w2), (_R_RANK, r1), (_R_RANK + 1, r2)):
        rec = jnp.where(lane == ln, val, rec)
    return rec


_PACK_W = D_MODEL // 2


def _pack_rows(hb):
    lo = lax.bitcast_convert_type(hb[:, :_PACK_W].astype(F32), jnp.int32)
    hi = lax.bitcast_convert_type(hb[:, _PACK_W:].astype(F32), jnp.int32)
    return jnp.bitwise_or(jnp.bitwise_and(hi, -65536), lax.shift_right_logical(lo, 16))


def _unpack_rows(w):
    lo = lax.bitcast_convert_type(lax.shift_left(w, 16), F32)
    hi = lax.bitcast_convert_type(jnp.bitwise_and(w, -65536), F32)
    return jnp.concatenate([lo, hi], axis=-1).astype(BF16)


def _moe_input(xnew, first, tail_in, tail_out, tail_scr):
    g2_ref, sc2_ref, sh2_ref, wr_ref, br_ref = tail_in
    x_out, h_out, rec_out, cnt_out = tail_out
    tri_scr, carry_scr = tail_scr

    @pl.when(first)
    def _():
        tm = tri_scr.shape[0]
        r = lax.broadcasted_iota(jnp.int32, (tm, tm), 0)
        c = lax.broadcasted_iota(jnp.int32, (tm, tm), 1)
        tri_scr[...] = jnp.where(c < r, 1.0, 0.0).astype(BF16)
        carry_scr[...] = jnp.zeros(carry_scr.shape, F32)

    x_out[...] = xnew
    h = _rms(xnew, g2_ref[...]) * (1.0 + sc2_ref[...]) + sh2_ref[...]
    hh = h.astype(BF16)
    h_out[...] = _pack_rows(hh)
    lg = jnp.dot(hh, wr_ref[...], preferred_element_type=F32) + br_ref[...]
    rec_out[...] = _route_rows(lg, carry_scr, tri_scr)
    cnt_out[...] = carry_scr[...]


def _even_out_kernel(oac_ref, obc_ref, oal_ref, obl_ref, x_ref, w_ref, gate_ref, *rest, ctx_tiles):
    tail_in, tail_out, (w_scr,), tail_scr = rest[:5], rest[5:9], rest[9:10], rest[10:]
    first = pl.program_id(0) == 0

    @pl.when(first)
    def _():
        w_scr[...] = w_ref[...].astype(BF16)

    is_ctx = pl.program_id(0) < ctx_tiles
    parts = [jnp.where(is_ctx, c_ref[hb], l_ref[hb])
             for c_ref, l_ref in ((oac_ref, oal_ref), (obc_ref, obl_ref)) for hb in range(DN_HEADS)]
    mix = jnp.concatenate(parts, axis=-1)
    out = jnp.dot(mix, w_scr[...], preferred_element_type=F32)
    _moe_input(x_ref[...] + gate_ref[...] * out, first, tail_in, tail_out, tail_scr)


def _tail_specs(tm):
    const = lambda shape: pl.BlockSpec(shape, lambda i: (0,) * len(shape))
    in_specs = [_mod_spec(2, tm), const((1, D_MODEL)), _mod_spec(4, tm), _mod_spec(3, tm),
                const((D_MODEL, LANES)), const((1, LANES))]
    out_specs = [pl.BlockSpec((tm, D_MODEL), lambda i: (i, 0)),
                 pl.BlockSpec((tm, _PACK_W), lambda i: (i, 0)),
                 pl.BlockSpec((tm, LANES), lambda i: (i, 0)),
                 const((1, LANES))]
    out_shape = [jax.ShapeDtypeStruct((N_TOK, D_MODEL), F32),
                 jax.ShapeDtypeStruct((N_TOK, _PACK_W), jnp.int32),
                 jax.ShapeDtypeStruct((N_TOK, LANES), F32),
                 jax.ShapeDtypeStruct((1, LANES), F32)]
    scratch = [pltpu.VMEM((tm, tm), BF16), pltpu.VMEM((1, LANES), F32)]
    return in_specs, out_specs, out_shape, scratch


def _router_weights(w_rg, b_rg, w_re, b_re):
    pad = LANES - N_EGROUPS - N_EXPERTS
    w = jnp.concatenate([w_rg, w_re, jnp.zeros((D_MODEL, pad), F32)], axis=1)
    b = jnp.concatenate([b_rg, b_re, jnp.zeros((pad,), F32)]).reshape(1, LANES)
    return w.astype(BF16), b


def _even_out(oa_ctx, ob_ctx, oa_lat, ob_lat, x, w_out, layer, mods, g2, router):
    tm = 512
    ctx_tiles = N_CTX // tm
    tail_in, out_specs, out_shape, tail_scr = _tail_specs(tm)
    ctxblk = pl.BlockSpec((DN_HEADS, tm, LANES), lambda i: (0, jnp.minimum(i, ctx_tiles - 1), 0))
    latblk = pl.BlockSpec((DN_HEADS, tm, LANES), lambda i: (0, jnp.maximum(i - ctx_tiles, 0), 0))
    return pl.pallas_call(
        functools.partial(_even_out_kernel, ctx_tiles=ctx_tiles), grid=(N_TOK // tm,),
        in_specs=[ctxblk, ctxblk, latblk, latblk, pl.BlockSpec((tm, D_MODEL), lambda i: (i, 0)),
                  pl.BlockSpec((None, D_MODEL, D_MODEL), lambda i: (layer, 0, 0))] + tail_in,
        out_specs=out_specs, out_shape=out_shape,
        scratch_shapes=[pltpu.VMEM((D_MODEL, D_MODEL), BF16)] + tail_scr,
        compiler_params=_params(("arbitrary",)), name="even_out",
    )(oa_ctx, ob_ctx, oa_lat, ob_lat, x, w_out, mods, g2.reshape(1, D_MODEL), mods, mods, *router)


def _gelu_tanh(x):
    c = 0.7978845608028654
    hx = 0.5 * x
    return hx + hx * jnp.tanh(x * (c + (c * 0.044715) * (x * x)))


def _sgu_kernel(x_ref, *rest, tm, n_pend):
    pend, rest = rest[:n_pend], rest[n_pend:]
    g1_ref, sh1_ref, sc1_ref, win_ref, lng_ref, lnb_ref, ws_ref, bst_ref, wout_ref, gate_ref = rest[:10]
    rest = rest[10:]
    tail_in, tail_out, (v_scr, m_scr), tail_scr = rest[:5], rest[5:9], rest[9:11], rest[11:]
    first = pl.program_id(0) == 0
    x = _with_pending(x_ref, pend)
    h = (_rms(x, g1_ref[...]) * (1.0 + sc1_ref[...]) + sh1_ref[...]).astype(BF16)

    def proj_u(g):
        return jnp.dot(h, win_ref[:, g * SG_GW:(g + 1) * SG_GW], preferred_element_type=F32)

    v = _gelu_tanh(jnp.dot(h, win_ref[:, SG_W:], preferred_element_type=F32))
    mu = jnp.mean(v, axis=-1, keepdims=True)
    vc = v - mu
    var = jnp.mean(vc * vc, axis=-1, keepdims=True)
    v_scr[...] = (vc * lax.rsqrt(var + EPS) * lng_ref[...] + lnb_ref[...]).astype(BF16)

    u_next = proj_u(0)
    for g in range(SG_GROUPS):
        cs = slice(g * SG_GW, (g + 1) * SG_GW)
        u_raw, u_next = u_next, (proj_u(g + 1) if g + 1 < SG_GROUPS else None)
        w_sp = ws_ref[g].astype(BF16)
        chunks = [slice(c * SG_CHUNK, (c + 1) * SG_CHUNK) for c in range(tm // SG_CHUNK)]
        sps = [jnp.dot(w_sp, v_scr[rs, cs], preferred_element_type=F32) for rs in chunks]
        u = _gelu_tanh(u_raw)
        for rs, sp in zip(chunks, sps):
            m_scr[rs, cs] = (u[rs] * (sp + bst_ref[:, g:g + 1])).astype(BF16)
    out = jnp.dot(m_scr[...], wout_ref[...], preferred_element_type=F32)
    _moe_input(x + gate_ref[...] * out, first, tail_in, tail_out, tail_scr)


def _sgu_layer(x, pend, mods, g1, w_in, ln_g, ln_b, w_s, b_s, w_out, g2, router):
    tm = 512
    pend = tuple(pend) if pend else ()
    tail_in, out_specs, out_shape, tail_scr = _tail_specs(tm)
    const = lambda shape: pl.BlockSpec(shape, lambda i: (0,) * len(shape))
    held = lambda shape: pl.BlockSpec(shape, lambda i: (0,) * len(shape), pipeline_mode=pl.Buffered(1))
    return pl.pallas_call(
        functools.partial(_sgu_kernel, tm=tm, n_pend=len(pend)), grid=(N_TOK // tm,),
        in_specs=[pl.BlockSpec((tm, D_MODEL), lambda i: (i, 0))] + (_pending_specs(tm) if pend else [])
        + [const((1, D_MODEL)), _mod_spec(0, tm), _mod_spec(1, tm),
                  held((D_MODEL, 2 * SG_W)), const((1, SG_W)), const((1, SG_W)),
                  const((SG_GROUPS, SG_CHUNK, SG_CHUNK)), const((SG_CHUNK, SG_GROUPS)),
                  held((SG_W, D_MODEL))] + tail_in,
        out_specs=out_specs, out_shape=out_shape,
        scratch_shapes=[pltpu.VMEM((tm, SG_W), BF16), pltpu.VMEM((tm, SG_W), BF16)] + tail_scr,
        compiler_params=_params(("arbitrary",)), name="sgu_layer",
    )(x, *pend, g1.reshape(1, D_MODEL), mods, mods, w_in.astype(BF16), ln_g.reshape(1, SG_W), ln_b.reshape(1, SG_W),
      w_s, b_s.T, w_out.astype(BF16), mods, g2.reshape(1, D_MODEL), mods, mods, *router)


def _plan(rec, cnt):
    e_idx = rec[:, _R_E:_R_E + 2].astype(jnp.int32)
    rank = rec[:, _R_RANK:_R_RANK + 2].astype(jnp.int32)
    counts = cnt[0, _LOGIT0:_LOGIT0 + N_EXPERTS].astype(jnp.int32)
    padded = (counts + MOE_BLK - 1) // MOE_BLK * MOE_BLK
    pad_end = jnp.cumsum(padded)
    pad_start = pad_end - padded
    hit = e_idx[:, :, None] == jnp.arange(N_EXPERTS, dtype=jnp.int32)[None, None, :]
    dest = jnp.sum(jnp.where(hit, pad_start[None, None, :], 0), axis=-1) + rank
    blk0 = jnp.arange(MOE_NBLK, dtype=jnp.int32) * MOE_BLK
    blk_e = jnp.minimum(jnp.sum((pad_end[None, :] <= blk0[:, None]).astype(jnp.int32), axis=-1),
                        N_EXPERTS - 1)
    n_used = (pad_end[-1] // MOE_BLK).astype(jnp.int32).reshape(1)
    owns = counts > 0
    slot_of = (jnp.cumsum(owns.astype(jnp.int32)) - 1) % _W_SLOTS
    ids = jnp.arange(N_EXPERTS, dtype=jnp.int32)
    later = jnp.logical_and(owns[None, :], ids[None, :] > ids[:, None])
    next_of = jnp.min(jnp.where(later, ids[None, :], N_EXPERTS), axis=-1)
    next2_of = jnp.concatenate([next_of, jnp.full((1,), N_EXPERTS, jnp.int32)])[next_of]
    ahead = jnp.stack([next_of, next2_of], axis=0)
    ahead = jnp.where(ahead == N_EXPERTS, -1, ahead)
    return dest, blk_e, n_used, slot_of[blk_e], ahead[:, blk_e].reshape(-1)


_W_PARTS = 4
_W_SLOTS = 3


def _expert_kernel(blk_e_ref, n_used_ref, slot_ref, next_ref, x_ref, wg_hbm, wu_hbm, wd_hbm, o_ref,
                   wg_buf, wu_buf, wd_buf, wg_scr, wu_scr, wd_scr, sems, *, layer):
    j = pl.program_id(0)
    e = blk_e_ref[j]
    slot = slot_ref[j]
    fresh = jnp.logical_or(j == 0, e != blk_e_ref[jnp.maximum(j - 1, 0)])
    live = j < n_used_ref[0]

    def copies(expert, s):
        out = []
        for m, (hbm, buf) in enumerate(((wg_hbm, wg_buf), (wu_hbm, wu_buf), (wd_hbm, wd_buf))):
            rows = buf.shape[1] // _W_PARTS
            for part in range(_W_PARTS):
                band = pl.ds(part * rows, rows)
                out.append(pltpu.make_async_copy(hbm.at[layer, expert, band], buf.at[s, band],
                                                 sems.at[s, m, part]))
        return out

    def start_if_any(expert, s):
        @pl.when(expert >= 0)
        def _():
            for cp in copies(expert, s):
                cp.start()

    @pl.when(j == 0)
    def _():
        for cp in copies(e, slot):
            cp.start()
        start_if_any(next_ref[j], lax.rem(slot + 1, _W_SLOTS))

    @pl.when(jnp.logical_and(fresh, live))
    def _():
        for cp in copies(e, slot):
            cp.wait()
        start_if_any(next_ref[MOE_NBLK + j], lax.rem(slot + 2, _W_SLOTS))

        wg_scr[...] = wg_buf[slot].astype(BF16)
        wu_scr[...] = wu_buf[slot].astype(BF16)
        wd_scr[...] = wd_buf[slot].astype(BF16)

    @pl.when(live)
    def _():
        x = _unpack_rows(x_ref[...])
        gt = jnp.dot(x, wg_scr[...], preferred_element_type=F32)
        up = jnp.dot(x, wu_scr[...], preferred_element_type=F32)
        hb = (_silu(gt) * up).astype(BF16)
        o_ref[...] = jnp.dot(hb, wd_scr[...], preferred_element_type=F32).astype(o_ref.dtype)

    @pl.when(jnp.logical_not(live))
    def _():
        o_ref[...] = jnp.zeros(o_ref.shape, o_ref.dtype)


def _experts(x_pad, blk_e, n_used, slot, nxt, w_gate, w_up, w_down, layer):
    hbm = pl.BlockSpec(memory_space=pl.ANY)
    grid_spec = pltpu.PrefetchScalarGridSpec(
        num_scalar_prefetch=4, grid=(MOE_NBLK,),
        in_specs=[pl.BlockSpec((MOE_BLK, _PACK_W), lambda j, be, nu, *_: (jnp.minimum(j, nu[0] - 1), 0)),
                  hbm, hbm, hbm],
        out_specs=pl.BlockSpec((MOE_BLK, D_MODEL), lambda j, *_: (j, 0)),
        scratch_shapes=[pltpu.VMEM((_W_SLOTS, D_MODEL, D_EXPERT), F32),
                        pltpu.VMEM((_W_SLOTS, D_MODEL, D_EXPERT), F32),
                        pltpu.VMEM((_W_SLOTS, D_EXPERT, D_MODEL), F32),
                        pltpu.VMEM((D_MODEL, D_EXPERT), BF16), pltpu.VMEM((D_MODEL, D_EXPERT), BF16),
                        pltpu.VMEM((D_EXPERT, D_MODEL), BF16),
                        pltpu.SemaphoreType.DMA((_W_SLOTS, 3, _W_PARTS))])
    return pl.pallas_call(
        functools.partial(_expert_kernel, layer=layer), grid_spec=grid_spec,
        out_shape=jax.ShapeDtypeStruct((MOE_NBLK * MOE_BLK, D_MODEL), BF16),
        compiler_params=_params(("arbitrary",)), name="experts",
    )(blk_e, n_used, slot, nxt, x_pad, w_gate, w_up, w_down)


def _final_kernel(x_ref, y2_ref, rec_ref, gate_ref, fg_ref, o_ref):
    o_ref[...] = _rms(_with_pending(x_ref, (y2_ref, rec_ref, gate_ref)), fg_ref[...])


def _final_norm(x, pend, final_g, row0, n_rows):
    tm = 512
    tile0 = row0 // tm
    return pl.pallas_call(
        _final_kernel, grid=(n_rows // tm,),
        in_specs=[pl.BlockSpec((tm, D_MODEL), lambda i: (i + tile0, 0))] + _pending_specs(tm, tile0)
        + [pl.BlockSpec((1, D_MODEL), lambda i: (0, 0))],
        out_specs=pl.BlockSpec((tm, D_MODEL), lambda i: (i, 0)),
        out_shape=jax.ShapeDtypeStruct((n_rows, D_MODEL), F32),
        compiler_params=_params(("parallel",)), name="final_norm",
    )(x, *pend, final_g.reshape(1, D_MODEL))


_SC_WORKERS = 32
_SC_CORES = 2
_SC_ROWS = 64


def _dispatch_rows(hp, dest):
    n, width = hp.shape
    per_w = n // _SC_WORKERS
    n_ch = per_w // _SC_ROWS
    idx = dest.T.reshape(2, _SC_WORKERS, n_ch, _SC_ROWS)
    mesh = plsc.VectorSubcoreMesh(core_axis_name="c", subcore_axis_name="s")

    @functools.partial(
        pl.kernel, mesh=mesh, out_type=jax.ShapeDtypeStruct((MOE_NBLK * MOE_BLK, width), hp.dtype),
        scratch_types=[pltpu.VMEM((n_ch, _SC_ROWS), jnp.int32), pltpu.VMEM((n_ch, _SC_ROWS), jnp.int32),
                       pltpu.VMEM((_SC_ROWS, width), hp.dtype)], name="dispatch_rows")
    def scatter(h_hbm, idx_hbm, out_hbm, i0_v, i1_v, rows_v):
        wid = lax.axis_index("s") * _SC_CORES + lax.axis_index("c")
        pltpu.sync_copy(idx_hbm.at[0, wid], i0_v)
        pltpu.sync_copy(idx_hbm.at[1, wid], i1_v)

        @pl.loop(0, n_ch)
        def _(g):
            pltpu.sync_copy(h_hbm.at[pl.ds(wid * per_w + g * _SC_ROWS, _SC_ROWS)], rows_v)
            pltpu.sync_copy(rows_v, out_hbm.at[i0_v.at[g]])
            pltpu.sync_copy(rows_v, out_hbm.at[i1_v.at[g]])

    return scatter(hp, idx)


def _moe(h, rec, cnt, mods, w_gate, w_up, w_down, layer):
    dest, blk_e, n_used, slot, nxt = _plan(rec, cnt)
    y_pad = _experts(_dispatch_rows(h, dest), blk_e, n_used, slot, nxt, w_gate, w_up, w_down, layer)
    order = dest.reshape(N_TOK // _PEND_TM, _PEND_TM, 2).transpose(0, 2, 1).reshape(-1)
    return y_pad[order], rec, mods


def kernel(x_prompt, x_sample, c, cache_k, cache_v, state_delta, c_ctx, ada_w, ada_b, norm1_g, norm2_g, final_g,
           ev_w_in, ev_w_out, ev_conv_w, ev_a_log, ev_dt_bias, ev_onorm_g, ev_rpb, od_w_in, od_ln_g, od_ln_b,
           od_w_s, od_b_s, od_w_out, moe_w_rg, moe_b_rg, moe_w_re, moe_b_re, moe_w_gate, moe_w_up, moe_w_down):
    x = (x_prompt.reshape(N_CTX, D_MODEL), x_sample.reshape(N_LAT, D_MODEL))
    cond = jnp.concatenate([c_ctx[None, :], c, jnp.zeros((N_COND - 1 - DEC_BATCH, D_MODEL), F32)], axis=0)
    mods_all = _ada_mods(cond, ada_w, ada_b)
    kctx_all = cache_k.reshape(DEC_BATCH, -1, PAST_LEN, NA_HEADS * NA_HD)
    vctx_all = cache_v.reshape(DEC_BATCH, -1, PAST_LEN, NA_HEADS * NA_HD)

    ks, vs, ss = [], [], []
    pend = None
    for l in range(DEPTH):
        mods = mods_all[l]
        router = _router_weights(moe_w_rg[l], moe_b_rg[l], moe_w_re[l], moe_b_re[l])
        if l % 2 == 0:
            e = l // 2
            proj, ab, kv, x = _even_proj(x, pend, mods, norm1_g[l], ev_w_in[e])
            dn = (proj, ab, ev_conv_w[e], ev_a_log[e], ev_dt_bias[e], ev_onorm_g[e])
            oa_ctx, s_fin = _delta_heads(*dn, SEQ, BATCH, 0, None)
            oa_lat, _ = _delta_heads(*dn, DEC_SEQ, DEC_BATCH, N_CTX // DEC_SEQ, (state_delta, e))
            ob_ctx = _ctx_attention(proj)
            ob_lat = _na_attention(proj, kctx_all, vctx_all, e, ev_rpb[e])
            x, h, rec, cnt = _even_out(oa_ctx, ob_ctx, oa_lat, ob_lat, x, ev_w_out, e, mods, norm2_g[l],
                                       router)
            na_w = NA_HEADS * NA_HD
            ks.append(kv[:N_CTX, :na_w].reshape(BATCH, SEQ, NA_HEADS, NA_HD))
            vs.append(kv[:N_CTX, na_w:].reshape(BATCH, SEQ, NA_HEADS, NA_HD))
            ss.append(s_fin)
        else:
            o = l // 2
            x, h, rec, cnt = _sgu_layer(x, pend, mods, norm1_g[l], od_w_in[o], od_ln_g[o], od_ln_b[o],
                                        od_w_s[o], od_b_s[o], od_w_out[o], norm2_g[l], router)
        pend = _moe(h, rec, cnt, mods, moe_w_gate, moe_w_up, moe_w_down, l)
    y_prompt = _final_norm(x, pend, final_g, 0, N_CTX).reshape(BATCH, SEQ, D_MODEL)
    y_sample = _final_norm(x, pend, final_g, N_CTX, N_LAT).reshape(DEC_BATCH, DEC_SEQ, D_MODEL)
    return (y_prompt, y_sample, jnp.stack(ks, axis=1), jnp.stack(vs, axis=1), jnp.stack(ss, axis=1))
```

```python
import functools

import jax
import jax.numpy as jnp
from jax import lax
from jax.experimental import pallas as pl
from jax.experimental.pallas import tpu as pltpu
from jax.experimental.pallas import tpu_sc as plsc

F32 = jnp.float32
BF16 = jnp.bfloat16

D_MODEL = 1024
BATCH = 16
SEQ = 256
DEPTH = 4
DEC_BATCH = 4
DEC_SEQ = 2048
PAST_LEN = 512
GRID_W = 64
EPS = 1e-6
NEG_INF = -1e30

DN_HEADS = 4
DN_DK = 128
DN_CHUNK = 64
NA_HEADS = 8
NA_HD = 64
NA_ROWS = 8
NA_COLS = 16
SG_CHUNK = 128
SG_GROUPS = 8
SG_W = 2 * D_MODEL
SG_GW = SG_W // SG_GROUPS
N_EGROUPS = 4
EXP_PER_GROUP = 8
N_EXPERTS = 32
D_EXPERT = 512

N_CTX = BATCH * SEQ
N_LAT = DEC_BATCH * DEC_SEQ
N_TOK = N_CTX + N_LAT
N_COND = 8
PROJ_W = 4096
LANES = 128
MOE_BLK = 256
MOE_NBLK = -(-(2 * N_TOK + N_EXPERTS * (MOE_BLK - 1)) // MOE_BLK)
VMEM_LIMIT = 56 * 1024 * 1024

_QA, _KA, _VA, _ZA, _QB, _KB, _VB = 0, 4, 8, 12, 16, 20, 24


def _params(sem):
    return pltpu.CompilerParams(dimension_semantics=sem, vmem_limit_bytes=VMEM_LIMIT)


def _bdot(a, b):
    return jnp.dot(a.astype(BF16), b.astype(BF16), preferred_element_type=F32)


def _bdot_nt(a, b):
    return lax.dot_general(a.astype(BF16), b.astype(BF16), (((1,), (1,)), ((), ())),
                           preferred_element_type=F32)


def _bdot_tn(a, b):
    return lax.dot_general(a.astype(BF16), b.astype(BF16), (((0,), (0,)), ((), ())),
                           preferred_element_type=F32)


def _split2(a):
    p0 = a.astype(BF16)
    return p0, (a - p0.astype(F32)).astype(BF16)


def _dot3(a, b):
    ah = a.astype(BF16)
    al = (a - ah.astype(F32)).astype(BF16)
    bh = b.astype(BF16)
    bl = (b - bh.astype(F32)).astype(BF16)
    return (jnp.dot(ah, bh, preferred_element_type=F32) + jnp.dot(ah, bl, preferred_element_type=F32)
            + jnp.dot(al, bh, preferred_element_type=F32))


def _mask_bf16(m01):
    return jnp.where(m01, 1.0, 0.0).astype(BF16)


def _xdot(m01, a):
    m = _mask_bf16(m01)
    p0, p1 = _split2(a)
    return jnp.dot(m, p0, preferred_element_type=F32) + jnp.dot(m, p1, preferred_element_type=F32)


def _xdot_r(a, m01):
    m = _mask_bf16(m01)
    p0, p1 = _split2(a)
    return jnp.dot(p0, m, preferred_element_type=F32) + jnp.dot(p1, m, preferred_element_type=F32)


def _sigmoid(x):
    return 0.5 * jnp.tanh(0.5 * x) + 0.5


def _silu(x):
    hx = 0.5 * x
    return hx + hx * jnp.tanh(hx)


def _rms(x, g):
    return x * lax.rsqrt(jnp.mean(x * x, axis=-1, keepdims=True) + EPS) * g


def _cond_index(row):
    return jnp.where(row < N_CTX, 0, 1 + (row - N_CTX) // DEC_SEQ)


def _mod_spec(k, tm, tile0=0):
    return pl.BlockSpec((None, None, 1, D_MODEL), lambda i, *_: (_cond_index((i + tile0) * tm), k, 0, 0))


def _ada_kernel(c_ref, w_ref, b_ref, o_ref):
    o_ref[...] = _bdot(_silu(c_ref[...]), w_ref[...]) + b_ref[...]


def _ada_mods(cond, ada_w, ada_b):
    tn = 1536
    out = pl.pallas_call(
        _ada_kernel, grid=(DEPTH, 6 * D_MODEL // tn),
        in_specs=[pl.BlockSpec((N_COND, D_MODEL), lambda l, j: (0, 0)),
                  pl.BlockSpec((None, D_MODEL, tn), lambda l, j: (l, 0, j)),
                  pl.BlockSpec((None, 1, tn), lambda l, j: (l, 0, j))],
        out_specs=pl.BlockSpec((None, N_COND, tn), lambda l, j: (l, 0, j)),
        out_shape=jax.ShapeDtypeStruct((DEPTH, N_COND, 6 * D_MODEL), F32),
        compiler_params=_params(("parallel", "parallel")), name="ada_mods",
    )(cond, ada_w, ada_b.reshape(DEPTH, 1, 6 * D_MODEL))
    return out.reshape(DEPTH, N_COND, 6, 1, D_MODEL)


_EV_TN = 512
_EV_W = 7 * DN_HEADS * LANES
_KV_COL0 = _KB * LANES


def _with_pending(x_ref, pend):
    if not pend:
        return x_ref[...]
    y2_ref, rec_ref, gate_ref = pend
    rec = rec_ref[...]
    tm = rec.shape[0]
    y = (rec[:, _R_W:_R_W + 1] * y2_ref[:tm, :].astype(F32)
         + rec[:, _R_W + 1:_R_W + 2] * y2_ref[tm:, :].astype(F32))
    return x_ref[...] + gate_ref[...] * y


_PEND_TM = 512


def _pending_specs(tm, tile0=0):
    assert tm == _PEND_TM
    return [pl.BlockSpec((2 * tm, D_MODEL), lambda i: (i + tile0, 0)),
            pl.BlockSpec((tm, LANES), lambda i: (i + tile0, 0)), _mod_spec(5, tm, tile0)]


def _even_proj_kernel(x_ref, *rest, n_pend, ctx_tiles):
    pend, (g_ref, sh_ref, sc_ref, w_ref, wab_ref, o_ref, ab_ref, kv_ref) = rest[:n_pend], rest[n_pend:n_pend + 8]
    if n_pend == 1:
        x = jnp.where(pl.program_id(0) < ctx_tiles, x_ref[...], pend[0][...])
    else:
        x = _with_pending(x_ref, pend)
    if pend:
        rest[n_pend + 8][...] = x
    h = (_rms(x, g_ref[...]) * (1.0 + sc_ref[...]) + sh_ref[...]).astype(BF16)
    ab_ref[...] = jnp.dot(h, wab_ref[...], preferred_element_type=F32)
    for j in range(_EV_W // _EV_TN):
        c0 = j * _EV_TN
        y = jnp.dot(h, w_ref[:, c0:c0 + _EV_TN], preferred_element_type=F32)
        for c in range(_EV_TN // LANES):
            o_ref[c0 // LANES + c] = y[:, c * LANES:(c + 1) * LANES].astype(BF16)
        if c0 >= _KV_COL0:
            kv_ref[:, c0 - _KV_COL0:c0 - _KV_COL0 + _EV_TN] = y


def _even_proj(x, pend, mods, g, w_in):
    tm = 512
    ctx_tiles = N_CTX // tm
    rows = pl.BlockSpec((tm, D_MODEL), lambda i: (i, 0))
    if isinstance(x, tuple):
        x, pend = x[0], (x[1],)
        x_specs = [pl.BlockSpec((tm, D_MODEL), lambda i: (jnp.minimum(i, ctx_tiles - 1), 0)),
                   pl.BlockSpec((tm, D_MODEL), lambda i: (jnp.maximum(i - ctx_tiles, 0), 0))]
    else:
        pend = tuple(pend) if pend else ()
        x_specs = [rows] + (_pending_specs(tm) if pend else [])
    n_ab = 4 * DN_HEADS
    ab0 = 4 * DN_HEADS * DN_DK
    w_main = jnp.concatenate([w_in[:, :ab0], w_in[:, ab0 + n_ab:]], axis=1).astype(BF16)
    w_ab = jnp.concatenate([w_in[:, ab0:ab0 + n_ab], jnp.zeros((D_MODEL, LANES - n_ab), F32)],
                           axis=1).astype(BF16)
    held = lambda shape: pl.BlockSpec(shape, lambda i: (0,) * len(shape), pipeline_mode=pl.Buffered(1))
    out_specs = [pl.BlockSpec((_EV_W // LANES, tm, LANES), lambda i: (0, i, 0)),
                 pl.BlockSpec((tm, LANES), lambda i: (i, 0)),
                 pl.BlockSpec((tm, 2 * NA_HEADS * NA_HD), lambda i: (i, 0))]
    out_shape = [jax.ShapeDtypeStruct((_EV_W // LANES, N_TOK, LANES), BF16),
                 jax.ShapeDtypeStruct((N_TOK, LANES), F32),
                 jax.ShapeDtypeStruct((N_TOK, 2 * NA_HEADS * NA_HD), F32)]
    if pend:
        out_specs.append(rows)
        out_shape.append(jax.ShapeDtypeStruct((N_TOK, D_MODEL), F32))
    res = pl.pallas_call(
        functools.partial(_even_proj_kernel, n_pend=len(pend), ctx_tiles=ctx_tiles), grid=(N_TOK // tm,),
        in_specs=x_specs
        + [pl.BlockSpec((1, D_MODEL), lambda i: (0, 0)), _mod_spec(0, tm), _mod_spec(1, tm),
           held((D_MODEL, _EV_W)), held((D_MODEL, LANES))],
        out_specs=out_specs, out_shape=out_shape,
        compiler_params=_params(("parallel",)), name="even_proj",
    )(x, *pend, g.reshape(1, D_MODEL), mods, mods, w_main, w_ab)
    return (*res[:3], res[3] if pend else x)


_CHUNK_SHIFT = DN_CHUNK.bit_length() - 1
_CUM_ROWS = 256
_DN_CHAINS = 16
_DN_SHORT = 256
_MQ_ROWS = DN_DK + DN_CHUNK


def _dn_kernel(*refs, T, HB, has_s0, want_state):
    it = iter(refs)
    q_ref, k_ref, v_ref, z_ref, ab_ref = (next(it) for _ in range(5))
    cwq_ref, cwk_ref, cwv_ref, alog_ref, dtb_ref, og_ref = (next(it) for _ in range(6))
    s0_ref = next(it) if has_s0 else None
    o_ref = next(it)
    sfin_ref = next(it) if want_state else None
    qc, kc, vc, gsc, bsc, osc, b_s, mq_s = (next(it) for _ in range(8))

    C = DN_CHUNK
    n = T // C
    h0 = pl.program_id(1) * HB

    row = lax.broadcasted_iota(jnp.int32, (T, 1), 0)

    def conv(x_ref, cw_ref, hh):
        x = x_ref[hh].astype(F32)
        cw = cw_ref[:, hh * LANES:(hh + 1) * LANES]
        xp = jnp.where(row == 0, 0.0, pltpu.roll(x, 1, 0))
        xn = jnp.where(row == T - 1, 0.0, pltpu.roll(x, T - 1, 0))
        return _silu(cw[0:1, :] * xp + cw[1:2, :] * x + cw[2:3, :] * xn)

    def l2n(x):
        return x * lax.rsqrt(jnp.sum(x * x, axis=-1, keepdims=True) + EPS)

    ab = ab_ref[...]
    lane = lax.broadcasted_iota(jnp.int32, (1, LANES), 1)
    dtb = jnp.zeros((1, LANES), F32)
    alog = jnp.zeros((1, LANES), F32)
    for d in range(2):
        for hq in range(DN_HEADS):
            dtb = jnp.where(lane == d * DN_HEADS + hq, dtb_ref[d, hq], dtb)
            alog = jnp.where(lane == d * DN_HEADS + hq, alog_ref[d, hq], alog)
    xs = ab + dtb
    g_all = -jnp.exp(alog) * (jnp.maximum(xs, 0.0) + jnp.log(1.0 + jnp.exp(-jnp.abs(xs))))
    beta_all = _sigmoid(ab)

    sel_r = lax.broadcasted_iota(jnp.int32, (LANES, LANES), 0)
    for hh in range(HB):
        qc[hh] = l2n(conv(q_ref, cwq_ref, hh)) * (DN_DK ** -0.5)
        kc[hh] = l2n(conv(k_ref, cwk_ref, hh))
        vc[hh] = conv(v_ref, cwv_ref, hh)
        hd = h0 + hh
        for d in range(2):
            gsc[hh, d] = _xdot_r(g_all, sel_r == d * DN_HEADS + hd)
            bsc[hh, d] = _xdot_r(beta_all, sel_r == 2 * DN_HEADS + d * DN_HEADS + hd)

    pr = lax.broadcasted_iota(jnp.int32, (_CUM_ROWS, _CUM_ROWS), 0)
    pc = lax.broadcasted_iota(jnp.int32, (_CUM_ROWS, _CUM_ROWS), 1)
    same = lax.shift_right_logical(pr, _CHUNK_SHIFT) == lax.shift_right_logical(pc, _CHUNK_SHIFT)
    cum_mask = (jnp.logical_and(same, pc <= pr), jnp.logical_and(same, pc >= pr))

    def cum_body(i, carry):
        sl = pl.ds(pl.multiple_of(i * _CUM_ROWS, _CUM_ROWS), _CUM_ROWS)
        for hh in range(HB):
            for d in range(2):
                gsc[hh, d, sl, :] = _xdot(cum_mask[d], gsc[hh, d, sl, :])
        return carry

    lax.fori_loop(0, T // _CUM_ROWS, cum_body, 0)

    ri = lax.broadcasted_iota(jnp.int32, (C, C), 0)
    ci = lax.broadcasted_iota(jnp.int32, (C, C), 1)
    eye = (ri == ci).astype(F32)

    def prepare(items):
        lows, decays = [], []
        kk, qk = {}, {}
        for hh, d, c, slot in items:
            if (hh, slot) not in kk:
                sl = pl.ds(pl.multiple_of(c * C, C), C)
                k = kc[hh, sl, :]
                kk[hh, slot] = _bdot_nt(k, k)
                qk[hh, slot] = _bdot_nt(qc[hh, sl, :], k)
        for hh, d, c, slot in items:
            sl = pl.ds(pl.multiple_of(c * C, C), C)
            gc = gsc[hh, d, sl, :]
            incl = (ci <= ri) if d == 0 else (ci >= ri)
            strict = (ci < ri) if d == 0 else (ci > ri)
            gr = jnp.transpose(gc)[0:1, :C]
            decay = jnp.where(incl, jnp.exp(jnp.where(incl, gc[:, :C] - gr, 0.0)), 0.0)
            lows.append(jnp.where(strict, bsc[hh, d, sl, :C] * kk[hh, slot] * decay, 0.0))
            decays.append(decay)
        ts = [eye - low for low in lows]
        ps = lows
        for step in range(_CHUNK_SHIFT - 1):
            ps = [_dot3(p, p) for p in ps]
            ts = [t + _dot3(t, p) for t, p in zip(ts, ps)]
        wus, kds, attns, qds = [], [], [], []
        for (hh, d, c, slot), t, decay in zip(items, ts, decays):
            sl = pl.ds(pl.multiple_of(c * C, C), C)
            k, gc, beta = kc[hh, sl, :], gsc[hh, d, sl, :], bsc[hh, d, sl, :]
            eg = jnp.exp(gc)
            uw = _bdot(t, jnp.concatenate([vc[hh, sl, :] * beta, k * beta * eg], axis=-1))
            last = gc[C - 1:C, :] if d == 0 else gc[0:1, :]
            wus.append(jnp.concatenate([uw[:, LANES:], uw[:, :LANES]], axis=-1).astype(BF16))
            kds.append((k * jnp.exp(last - gc)).astype(BF16))
            attns.append((qk[hh, slot] * decay).astype(BF16))
            qds.append(qc[hh, sl, :] * eg)
        kdwus = [lax.dot_general(kd, wu, (((0,), (0,)), ((), ())), preferred_element_type=F32)
                 for kd, wu in zip(kds, wus)]
        awus = [jnp.dot(attn, wu, preferred_element_type=F32) for attn, wu in zip(attns, wus)]
        for (hh, d, c, slot), kdwu, awu, qd in zip(items, kdwus, awus, qds):
            mq0 = pl.multiple_of(c * _MQ_ROWS, _MQ_ROWS)
            mq_s[hh, d, pl.ds(mq0, DN_DK), :] = kdwu[:, :LANES].astype(BF16)
            mq_s[hh, d, pl.ds(mq0 + DN_DK, C), :] = (qd - awu[:, :LANES]).astype(BF16)
            b_s[hh, d, pl.ds(pl.multiple_of(c * DN_DK, DN_DK), DN_DK), :] = kdwu[:, LANES:]
            osc[hh, d, pl.ds(pl.multiple_of(c * C, C), C), :] = awu[:, LANES:]

    n_prep = min(n, _DN_CHAINS // 2)
    h_prep = max(1, min(HB, _DN_CHAINS // (2 * n_prep)))

    def prep_body(i, carry):
        for hg in range(0, HB, h_prep):
            prepare([(hh, d, i * n_prep + j, j) for hh in range(hg, hg + h_prep) for j in range(n_prep)
                     for d in range(2)])
        return carry

    lax.fori_loop(0, n // n_prep, prep_body, 0)

    def body(i, carry):
        chains = [(hh, d, i if d == 0 else n - 1 - i) for hh in range(HB) for d in range(2)]
        mss = [jnp.dot(mq_s[hh, d, pl.ds(pl.multiple_of(c * _MQ_ROWS, _MQ_ROWS), _MQ_ROWS), :],
                       S.astype(BF16), preferred_element_type=F32)
               for (hh, d, c), S in zip(chains, carry)]
        new = []
        for (hh, d, c), S, ms in zip(chains, carry, mss):
            sl = pl.ds(pl.multiple_of(c * C, C), C)
            osc[hh, d, sl, :] = osc[hh, d, sl, :] + ms[DN_DK:]
            last = gsc[hh, d, pl.ds(c * C + (C - 1 if d == 0 else 0), 1), :]
            new.append(S * jnp.exp(last) - ms[:DN_DK]
                       + b_s[hh, d, pl.ds(pl.multiple_of(c * DN_DK, DN_DK), DN_DK), :])
        return tuple(new)

    if has_s0:
        init = tuple(s0_ref[d, hh] for hh in range(HB) for d in range(2))
    else:
        init = tuple(jnp.zeros((DN_DK, LANES), F32) for _ in range(2 * HB))
    fin = lax.fori_loop(0, n, body, init)
    for hh in range(HB):
        if want_state:
            sfin_ref[0, hh] = fin[2 * hh]
            sfin_ref[1, hh] = fin[2 * hh + 1]
        o = osc[hh, 0] + osc[hh, 1]
        o_ref[hh] = (_rms(o, og_ref[...]) * _silu(z_ref[hh].astype(F32))).astype(o_ref.dtype)


def _delta_heads(proj, ab, conv_w, a_log, dt_bias, onorm_g, T, n_seq, row0, s0):
    has_s0 = s0 is not None
    want_state = not has_s0
    hb = DN_HEADS if T <= _DN_SHORT else 2

    def col(cb):
        return pl.BlockSpec((hb, T, LANES), lambda s, h: (cb // hb + h, row0 + s, 0))

    def cw(cb):
        return pl.BlockSpec((3, hb * LANES), lambda s, h: (0, cb // hb + h))

    smem = pl.BlockSpec(memory_space=pltpu.SMEM)
    in_specs = [col(_QA), col(_KA), col(_VA), col(_ZA),
                pl.BlockSpec((T, LANES), lambda s, h: (row0 + s, 0)),
                cw(0), cw(4), cw(8), smem, smem,
                pl.BlockSpec((1, LANES), lambda s, h: (0, 0))]
    args = [proj, proj, proj, proj, ab, conv_w, conv_w, conv_w, a_log, dt_bias,
            onorm_g.reshape(1, LANES)]
    state_spec = pl.BlockSpec((None, 2, hb, DN_DK, LANES), lambda s, h: (s, 0, h, 0, 0))
    if has_s0:
        states, layer = s0
        in_specs.append(pl.BlockSpec((None, None, 2, hb, DN_DK, LANES), lambda s, h: (s, layer, 0, h, 0, 0)))
        args.append(states)
    out_shape = [jax.ShapeDtypeStruct((DN_HEADS, n_seq * T, LANES), BF16)]
    out_specs = [pl.BlockSpec((hb, T, LANES), lambda s, h: (h, s, 0))]
    if want_state:
        out_shape.append(jax.ShapeDtypeStruct((n_seq, 2, DN_HEADS, DN_DK, LANES), F32))
        out_specs.append(state_spec)
    res = pl.pallas_call(
        functools.partial(_dn_kernel, T=T, HB=hb, has_s0=has_s0, want_state=want_state),
        grid=(n_seq, DN_HEADS // hb), in_specs=in_specs, out_specs=out_specs, out_shape=out_shape,
        scratch_shapes=[pltpu.VMEM((hb, T, LANES), F32)] * 3
        + [pltpu.VMEM((hb, 2, T, LANES), F32)] * 3
        + [pltpu.VMEM((hb, 2, T // DN_CHUNK * DN_DK, LANES), F32),
           pltpu.VMEM((hb, 2, T // DN_CHUNK * _MQ_ROWS, LANES), BF16)],
        compiler_params=_params(("parallel", "parallel")), name="delta_heads_%d" % T,
    )(*args)
    return res if want_state else (res[0], None)


def _pair_queries(q, first):
    return jnp.concatenate([jnp.where(first, q, 0.0), jnp.where(first, 0.0, q)], axis=0).astype(BF16)


def _ctx_attn_kernel(q_ref, k_ref, v_ref, o_ref):
    first = lax.broadcasted_iota(jnp.int32, (SEQ, LANES), 1) < NA_HD
    qm = _pair_queries(q_ref[...] * (NA_HD ** -0.5), first)
    s = lax.dot_general(k_ref[...], qm, (((1,), (1,)), ((), ())), preferred_element_type=F32)
    e = jnp.exp(s - jnp.max(s, axis=0, keepdims=True))
    den = jnp.sum(e, axis=0, keepdims=True)
    o = lax.dot_general(e.astype(BF16), v_ref[...], (((0,), (0,)), ((), ())), preferred_element_type=F32)
    o = jnp.where(first, o[:SEQ], o[SEQ:])
    den_t = jnp.transpose(jnp.broadcast_to(den, (LANES, 2 * SEQ)))
    o_ref[...] = (o / jnp.where(first, den_t[:SEQ], den_t[SEQ:])).astype(o_ref.dtype)


def _ctx_attention(proj):
    def col(cb):
        return pl.BlockSpec((None, SEQ, LANES), lambda s, p: (cb + p, s, 0))

    return pl.pallas_call(
        _ctx_attn_kernel, grid=(BATCH, NA_HEADS // 2),
        in_specs=[col(_QB), col(_KB), col(_VB)],
        out_specs=pl.BlockSpec((None, SEQ, LANES), lambda s, p: (p, s, 0)),
        out_shape=jax.ShapeDtypeStruct((NA_HEADS // 2, N_CTX, LANES), BF16),
        compiler_params=_params(("parallel", "parallel")), name="ctx_attention",
    )(proj, proj, proj)


_NA_UNROLL = 4


def _na_kernel(q_ref, k_ref, v_ref, kc_ref, vc_ref, bias_ref, o_ref, kcb_scr, vcb_scr):
    rows = DEC_SEQ // GRID_W
    win = NA_ROWS * GRID_W
    scale = NA_HD ** -0.5
    dn_nt = (((1,), (1,)), ((), ()))
    dn_tn = (((0,), (0,)), ((), ()))

    kcb_scr[...] = kc_ref[...].astype(BF16)
    vcb_scr[...] = vc_ref[...].astype(BF16)
    first = lax.broadcasted_iota(jnp.int32, (GRID_W, LANES), 1) < NA_HD

    def scores(group):
        out = []
        for r in range(group * _NA_UNROLL, (group + 1) * _NA_UNROLL):
            rs = min(max(r - NA_ROWS // 2, 0), rows - NA_ROWS)
            qsl, wsl = pl.ds(r * GRID_W, GRID_W), pl.ds(rs * GRID_W, win)
            qm = _pair_queries(q_ref[qsl, :] * scale, first)
            bias = jnp.concatenate([bias_ref[NA_ROWS - 1 - (r - rs) + i] for i in range(NA_ROWS)], axis=0)
            s_win = lax.dot_general(k_ref[wsl, :], qm, dn_nt, preferred_element_type=F32) + bias
            s_ctx = lax.dot_general(kcb_scr[...], qm, dn_nt, preferred_element_type=F32)
            out.append((qsl, wsl, s_win, s_ctx))
        return out

    n_groups = rows // _NA_UNROLL
    nxt = scores(0)
    for group in range(n_groups):
        cur, nxt = nxt, (scores(group + 1) if group + 1 < n_groups else None)
        ms = [jnp.maximum(jnp.max(sw, axis=0, keepdims=True), jnp.max(sc, axis=0, keepdims=True))
              for _, _, sw, sc in cur]
        e_wins = [jnp.exp(sw - m) for (_, _, sw, _), m in zip(cur, ms)]
        e_ctxs = [jnp.exp(sc - m) for (_, _, _, sc), m in zip(cur, ms)]
        dens = [jnp.sum(ew, axis=0, keepdims=True) + jnp.sum(ec, axis=0, keepdims=True)
                for ew, ec in zip(e_wins, e_ctxs)]
        for (qsl, wsl, _, _), ew, ec, den in zip(cur, e_wins, e_ctxs, dens):
            o = (lax.dot_general(ew.astype(BF16), v_ref[wsl, :], dn_tn, preferred_element_type=F32)
                 + lax.dot_general(ec.astype(BF16), vcb_scr[...], dn_tn, preferred_element_type=F32))
            o = o / jnp.transpose(jnp.broadcast_to(den, (LANES, LANES)))
            o_ref[qsl, :] = jnp.where(first, o[:GRID_W], o[GRID_W:]).astype(o_ref.dtype)


def _na_bias_table(rpb):
    col = jnp.arange(GRID_W)
    cs = jnp.clip(col - NA_COLS // 2, 0, GRID_W - NA_COLS)
    col_ok = (col[None, :] >= cs[:, None]) & (col[None, :] < cs[:, None] + NA_COLS)
    dc = jnp.clip(col[None, :] - col[:, None] + NA_COLS - 1, 0, 2 * NA_COLS - 2)
    onehot = (dc.T[None, :, :] == jnp.arange(2 * NA_COLS - 1)[:, None, None]).astype(F32)
    t = jnp.einsum('hrd,dkq->hrkq', rpb.astype(F32), onehot, precision=lax.Precision.HIGHEST)
    t = jnp.where(col_ok.T[None, None], t, NEG_INF)
    t = t.reshape(NA_HEADS // 2, 2, 2 * NA_ROWS - 1, GRID_W, GRID_W)
    return jnp.concatenate([t[:, 0], t[:, 1]], axis=-1)


def _na_attention(proj, kctx, vctx, layer, rpb):
    blk = N_CTX // DEC_SEQ

    def col(cb):
        return pl.BlockSpec((None, DEC_SEQ, LANES), lambda b, p: (cb + p, blk + b, 0))

    ctx = pl.BlockSpec((None, None, PAST_LEN, LANES), lambda b, p: (b, layer, 0, p))
    return pl.pallas_call(
        _na_kernel, grid=(DEC_BATCH, NA_HEADS // 2),
        in_specs=[col(_QB), col(_KB), col(_VB), ctx, ctx,
                  pl.BlockSpec((None, 2 * NA_ROWS - 1, GRID_W, 2 * GRID_W), lambda b, p: (p, 0, 0, 0))],
        out_specs=pl.BlockSpec((None, DEC_SEQ, LANES), lambda b, p: (p, b, 0)),
        out_shape=jax.ShapeDtypeStruct((NA_HEADS // 2, N_LAT, LANES), BF16),
        scratch_shapes=[pltpu.VMEM((PAST_LEN, LANES), BF16), pltpu.VMEM((PAST_LEN, LANES), BF16)],
        compiler_params=_params(("parallel", "parallel")), name="na_attention",
    )(proj, proj, proj, kctx, vctx, _na_bias_table(rpb))


_LOGIT0 = N_EGROUPS
_R_E, _R_W, _R_RANK = 0, 2, 4


def _lane_min_where(mask, lane):
    return jnp.min(jnp.where(mask, lane, LANES), axis=-1, keepdims=True)


def _route_rows(lg, carry_ref, tri_ref):
    big = -3.0e38
    lane = lax.broadcasted_iota(jnp.int32, lg.shape, 1)
    is_g = lane < N_EGROUPS
    gmax = jnp.max(jnp.where(is_g, lg, big), axis=-1, keepdims=True)
    gsum = jnp.sum(jnp.where(is_g, jnp.exp(jnp.where(is_g, lg - gmax, 0.0)), 0.0), axis=-1, keepdims=True)
    pg_top = 1.0 / gsum
    g_idx = _lane_min_where(jnp.logical_and(is_g, lg == gmax), lane)
    in_g = jnp.logical_and(lane >= _LOGIT0, lax.shift_right_arithmetic(lane - _LOGIT0, 3) == g_idx)
    in_g = jnp.logical_and(in_g, lane < _LOGIT0 + N_EXPERTS)
    m1 = jnp.max(jnp.where(in_g, lg, big), axis=-1, keepdims=True)
    i1 = _lane_min_where(jnp.logical_and(in_g, lg == m1), lane)
    rest = jnp.logical_and(in_g, lane != i1)
    m2 = jnp.max(jnp.where(rest, lg, big), axis=-1, keepdims=True)
    i2 = _lane_min_where(jnp.logical_and(rest, lg == m2), lane)
    e2 = jnp.exp(m2 - m1)
    w1 = pg_top * (1.0 / (1.0 + e2))
    w2 = pg_top * (e2 / (1.0 + e2))
    hit1 = lane == i1
    hit2 = lane == i2
    picked = jnp.where(jnp.logical_or(hit1, hit2), 1.0, 0.0)
    before = jnp.dot(tri_ref[...], picked.astype(BF16), preferred_element_type=F32) + carry_ref[...]
    r1 = jnp.sum(jnp.where(hit1, before, 0.0), axis=-1, keepdims=True)
    r2 = jnp.sum(jnp.where(hit2, before, 0.0), axis=-1, keepdims=True)
    carry_ref[...] = carry_ref[...] + jnp.sum(picked, axis=0, keepdims=True)
    rec = jnp.zeros(lg.shape, F32)
    for ln, val in ((_R_E, (i1 - _LOGIT0).astype(F32)), (_R_E + 1, (i2 - _LOGIT0).astype(F32)),
                    (_R_W, w1), (_R_W + 1, w2), (_R_RANK, r1), (_R_RANK + 1, r2)):
        rec = jnp.where(lane == ln, val, rec)
    return rec


_PACK_W = D_MODEL // 2


def _pack_rows(hb):
    lo = lax.bitcast_convert_type(hb[:, :_PACK_W].astype(F32), jnp.int32)
    hi = lax.bitcast_convert_type(hb[:, _PACK_W:].astype(F32), jnp.int32)
    return jnp.bitwise_or(jnp.bitwise_and(hi, -65536), lax.shift_right_logical(lo, 16))


def _unpack_rows(w):
    lo = lax.bitcast_convert_type(lax.shift_left(w, 16), F32)
    hi = lax.bitcast_convert_type(jnp.bitwise_and(w, -65536), F32)
    return jnp.concatenate([lo, hi], axis=-1).astype(BF16)


def _moe_input(xnew, first, tail_in, tail_out, tail_scr):
    g2_ref, sc2_ref, sh2_ref, wr_ref, br_ref = tail_in
    x_out, h_out, rec_out, cnt_out = tail_out
    tri_scr, carry_scr = tail_scr

    @pl.when(first)
    def _():
        tm = tri_scr.shape[0]
        r = lax.broadcasted_iota(jnp.int32, (tm, tm), 0)
        c = lax.broadcasted_iota(jnp.int32, (tm, tm), 1)
        tri_scr[...] = jnp.where(c < r, 1.0, 0.0).astype(BF16)
        carry_scr[...] = jnp.zeros(carry_scr.shape, F32)

    x_out[...] = xnew
    h = _rms(xnew, g2_ref[...]) * (1.0 + sc2_ref[...]) + sh2_ref[...]
    hh = h.astype(BF16)
    h_out[...] = _pack_rows(hh)
    lg = jnp.dot(hh, wr_ref[...], preferred_element_type=F32) + br_ref[...]
    rec_out[...] = _route_rows(lg, carry_scr, tri_scr)
    cnt_out[...] = carry_scr[...]


def _even_out_kernel(oac_ref, obc_ref, oal_ref, obl_ref, x_ref, w_ref, gate_ref, *rest, ctx_tiles):
    tail_in, tail_out, (w_scr,), tail_scr = rest[:5], rest[5:9], rest[9:10], rest[10:]
    first = pl.program_id(0) == 0

    @pl.when(first)
    def _():
        w_scr[...] = w_ref[...].astype(BF16)

    is_ctx = pl.program_id(0) < ctx_tiles
    parts = [jnp.where(is_ctx, c_ref[hb], l_ref[hb])
             for c_ref, l_ref in ((oac_ref, oal_ref), (obc_ref, obl_ref)) for hb in range(DN_HEADS)]
    mix = jnp.concatenate(parts, axis=-1)
    out = jnp.dot(mix, w_scr[...], preferred_element_type=F32)
    _moe_input(x_ref[...] + gate_ref[...] * out, first, tail_in, tail_out, tail_scr)


def _tail_specs(tm):
    const = lambda shape: pl.BlockSpec(shape, lambda i: (0,) * len(shape))
    in_specs = [_mod_spec(2, tm), const((1, D_MODEL)), _mod_spec(4, tm), _mod_spec(3, tm),
                const((D_MODEL, LANES)), const((1, LANES))]
    out_specs = [pl.BlockSpec((tm, D_MODEL), lambda i: (i, 0)),
                 pl.BlockSpec((tm, _PACK_W), lambda i: (i, 0)),
                 pl.BlockSpec((tm, LANES), lambda i: (i, 0)),
                 const((1, LANES))]
    out_shape = [jax.ShapeDtypeStruct((N_TOK, D_MODEL), F32),
                 jax.ShapeDtypeStruct((N_TOK, _PACK_W), jnp.int32),
                 jax.ShapeDtypeStruct((N_TOK, LANES), F32),
                 jax.ShapeDtypeStruct((1, LANES), F32)]
    scratch = [pltpu.VMEM((tm, tm), BF16), pltpu.VMEM((1, LANES), F32)]
    return in_specs, out_specs, out_shape, scratch


def _router_weights(w_rg, b_rg, w_re, b_re):
    pad = LANES - N_EGROUPS - N_EXPERTS
    w = jnp.concatenate([w_rg, w_re, jnp.zeros((D_MODEL, pad), F32)], axis=1)
    b = jnp.concatenate([b_rg, b_re, jnp.zeros((pad,), F32)]).reshape(1, LANES)
    return w.astype(BF16), b


def _even_out(oa_ctx, ob_ctx, oa_lat, ob_lat, x, w_out, layer, mods, g2, router):
    tm = 512
    ctx_tiles = N_CTX // tm
    tail_in, out_specs, out_shape, tail_scr = _tail_specs(tm)
    ctxblk = pl.BlockSpec((DN_HEADS, tm, LANES), lambda i: (0, jnp.minimum(i, ctx_tiles - 1), 0))
    latblk = pl.BlockSpec((DN_HEADS, tm, LANES), lambda i: (0, jnp.maximum(i - ctx_tiles, 0), 0))
    return pl.pallas_call(
        functools.partial(_even_out_kernel, ctx_tiles=ctx_tiles), grid=(N_TOK // tm,),
        in_specs=[ctxblk, ctxblk, latblk, latblk, pl.BlockSpec((tm, D_MODEL), lambda i: (i, 0)),
                  pl.BlockSpec((None, D_MODEL, D_MODEL), lambda i: (layer, 0, 0))] + tail_in,
        out_specs=out_specs, out_shape=out_shape,
        scratch_shapes=[pltpu.VMEM((D_MODEL, D_MODEL), BF16)] + tail_scr,
        compiler_params=_params(("arbitrary",)), name="even_out",
    )(oa_ctx, ob_ctx, oa_lat, ob_lat, x, w_out, mods, g2.reshape(1, D_MODEL), mods, mods, *router)


def _gelu_tanh(x):
    c = 0.7978845608028654
    hx = 0.5 * x
    return hx + hx * jnp.tanh(x * (c + (c * 0.044715) * (x * x)))


def _sgu_kernel(x_ref, *rest, tm, n_pend):
    pend, rest = rest[:n_pend], rest[n_pend:]
    g1_ref, sh1_ref, sc1_ref, win_ref, lng_ref, lnb_ref, ws_ref, bst_ref, wout_ref, gate_ref = rest[:10]
    rest = rest[10:]
    tail_in, tail_out, (v_scr, m_scr), tail_scr = rest[:5], rest[5:9], rest[9:11], rest[11:]
    first = pl.program_id(0) == 0
    x = _with_pending(x_ref, pend)
    h = (_rms(x, g1_ref[...]) * (1.0 + sc1_ref[...]) + sh1_ref[...]).astype(BF16)

    def proj_u(g):
        return jnp.dot(h, win_ref[:, g * SG_GW:(g + 1) * SG_GW], preferred_element_type=F32)

    v = _gelu_tanh(jnp.dot(h, win_ref[:, SG_W:], preferred_element_type=F32))
    mu = jnp.mean(v, axis=-1, keepdims=True)
    vc = v - mu
    var = jnp.mean(vc * vc, axis=-1, keepdims=True)
    v_scr[...] = (vc * lax.rsqrt(var + EPS) * lng_ref[...] + lnb_ref[...]).astype(BF16)

    u_next = proj_u(0)
    for g in range(SG_GROUPS):
        cs = slice(g * SG_GW, (g + 1) * SG_GW)
        u_raw, u_next = u_next, (proj_u(g + 1) if g + 1 < SG_GROUPS else None)
        w_sp = ws_ref[g].astype(BF16)
        chunks = [slice(c * SG_CHUNK, (c + 1) * SG_CHUNK) for c in range(tm // SG_CHUNK)]
        sps = [jnp.dot(w_sp, v_scr[rs, cs], preferred_element_type=F32) for rs in chunks]
        u = _gelu_tanh(u_raw)
        for rs, sp in zip(chunks, sps):
            m_scr[rs, cs] = (u[rs] * (sp + bst_ref[:, g:g + 1])).astype(BF16)
    out = jnp.dot(m_scr[...], wout_ref[...], preferred_element_type=F32)
    _moe_input(x + gate_ref[...] * out, first, tail_in, tail_out, tail_scr)


def _sgu_layer(x, pend, mods, g1, w_in, ln_g, ln_b, w_s, b_s, w_out, g2, router):
    tm = 512
    pend = tuple(pend) if pend else ()
    tail_in, out_specs, out_shape, tail_scr = _tail_specs(tm)
    const = lambda shape: pl.BlockSpec(shape, lambda i: (0,) * len(shape))
    held = lambda shape: pl.BlockSpec(shape, lambda i: (0,) * len(shape), pipeline_mode=pl.Buffered(1))
    return pl.pallas_call(
        functools.partial(_sgu_kernel, tm=tm, n_pend=len(pend)), grid=(N_TOK // tm,),
        in_specs=[pl.BlockSpec((tm, D_MODEL), lambda i: (i, 0))] + (_pending_specs(tm) if pend else [])
        + [const((1, D_MODEL)), _mod_spec(0, tm), _mod_spec(1, tm),
                  held((D_MODEL, 2 * SG_W)), const((1, SG_W)), const((1, SG_W)),
                  const((SG_GROUPS, SG_CHUNK, SG_CHUNK)), const((SG_CHUNK, SG_GROUPS)),
                  held((SG_W, D_MODEL))] + tail_in,
        out_specs=out_specs, out_shape=out_shape,
        scratch_shapes=[pltpu.VMEM((tm, SG_W), BF16), pltpu.VMEM((tm, SG_W), BF16)] + tail_scr,
        compiler_params=_params(("arbitrary",)), name="sgu_layer",
    )(x, *pend, g1.reshape(1, D_MODEL), mods, mods, w_in.astype(BF16), ln_g.reshape(1, SG_W), ln_b.reshape(1, SG_W),
      w_s, b_s.T, w_out.astype(BF16), mods, g2.reshape(1, D_MODEL), mods, mods, *router)


def _plan(rec, cnt):
    e_idx = rec[:, _R_E:_R_E + 2].astype(jnp.int32)
    rank = rec[:, _R_RANK:_R_RANK + 2].astype(jnp.int32)
    counts = cnt[0, _LOGIT0:_LOGIT0 + N_EXPERTS].astype(jnp.int32)
    padded = (counts + MOE_BLK - 1) // MOE_BLK * MOE_BLK
    pad_end = jnp.cumsum(padded)
    pad_start = pad_end - padded
    hit = e_idx[:, :, None] == jnp.arange(N_EXPERTS, dtype=jnp.int32)[None, None, :]
    dest = jnp.sum(jnp.where(hit, pad_start[None, None, :], 0), axis=-1) + rank
    blk0 = jnp.arange(MOE_NBLK, dtype=jnp.int32) * MOE_BLK
    blk_e = jnp.minimum(jnp.sum((pad_end[None, :] <= blk0[:, None]).astype(jnp.int32), axis=-1),
                        N_EXPERTS - 1)
    n_used = (pad_end[-1] // MOE_BLK).astype(jnp.int32).reshape(1)
    owns = counts > 0
    slot_of = (jnp.cumsum(owns.astype(jnp.int32)) - 1) % _W_SLOTS
    ids = jnp.arange(N_EXPERTS, dtype=jnp.int32)
    later = jnp.logical_and(owns[None, :], ids[None, :] > ids[:, None])
    next_of = jnp.min(jnp.where(later, ids[None, :], N_EXPERTS), axis=-1)
    next2_of = jnp.concatenate([next_of, jnp.full((1,), N_EXPERTS, jnp.int32)])[next_of]
    ahead = jnp.stack([next_of, next2_of], axis=0)
    ahead = jnp.where(ahead == N_EXPERTS, -1, ahead)
    return dest, blk_e, n_used, slot_of[blk_e], ahead[:, blk_e].reshape(-1)


_W_PARTS = 4
_W_SLOTS = 3


def _expert_kernel(blk_e_ref, n_used_ref, slot_ref, next_ref, x_ref, wg_hbm, wu_hbm, wd_hbm, o_ref,
                   wg_buf, wu_buf, wd_buf, wg_scr, wu_scr, wd_scr, sems, *, layer):
    j = pl.program_id(0)
    e = blk_e_ref[j]
    slot = slot_ref[j]
    fresh = jnp.logical_or(j == 0, e != blk_e_ref[jnp.maximum(j - 1, 0)])
    live = j < n_used_ref[0]

    def copies(expert, s):
        out = []
        for m, (hbm, buf) in enumerate(((wg_hbm, wg_buf), (wu_hbm, wu_buf), (wd_hbm, wd_buf))):
            rows = buf.shape[1] // _W_PARTS
            for part in range(_W_PARTS):
                band = pl.ds(part * rows, rows)
                out.append(pltpu.make_async_copy(hbm.at[layer, expert, band], buf.at[s, band],
                                                 sems.at[s, m, part]))
        return out

    def start_if_any(expert, s):
        @pl.when(expert >= 0)
        def _():
            for cp in copies(expert, s):
                cp.start()

    @pl.when(j == 0)
    def _():
        for cp in copies(e, slot):
            cp.start()
        start_if_any(next_ref[j], lax.rem(slot + 1, _W_SLOTS))

    @pl.when(jnp.logical_and(fresh, live))
    def _():
        for cp in copies(e, slot):
            cp.wait()
        start_if_any(next_ref[MOE_NBLK + j], lax.rem(slot + 2, _W_SLOTS))

        wg_scr[...] = wg_buf[slot].astype(BF16)
        wu_scr[...] = wu_buf[slot].astype(BF16)
        wd_scr[...] = wd_buf[slot].astype(BF16)

    @pl.when(live)
    def _():
        x = _unpack_rows(x_ref[...])
        gt = jnp.dot(x, wg_scr[...], preferred_element_type=F32)
        up = jnp.dot(x, wu_scr[...], preferred_element_type=F32)
        hb = (_silu(gt) * up).astype(BF16)
        o_ref[...] = jnp.dot(hb, wd_scr[...], preferred_element_type=F32).astype(o_ref.dtype)

    @pl.when(jnp.logical_not(live))
    def _():
        o_ref[...] = jnp.zeros(o_ref.shape, o_ref.dtype)


def _experts(x_pad, blk_e, n_used, slot, nxt, w_gate, w_up, w_down, layer):
    hbm = pl.BlockSpec(memory_space=pl.ANY)
    grid_spec = pltpu.PrefetchScalarGridSpec(
        num_scalar_prefetch=4, grid=(MOE_NBLK,),
        in_specs=[pl.BlockSpec((MOE_BLK, _PACK_W), lambda j, be, nu, *_: (jnp.minimum(j, nu[0] - 1), 0)),
                  hbm, hbm, hbm],
        out_specs=pl.BlockSpec((MOE_BLK, D_MODEL), lambda j, *_: (j, 0)),
        scratch_shapes=[pltpu.VMEM((_W_SLOTS, D_MODEL, D_EXPERT), F32),
                        pltpu.VMEM((_W_SLOTS, D_MODEL, D_EXPERT), F32),
                        pltpu.VMEM((_W_SLOTS, D_EXPERT, D_MODEL), F32),
                        pltpu.VMEM((D_MODEL, D_EXPERT), BF16), pltpu.VMEM((D_MODEL, D_EXPERT), BF16),
                        pltpu.VMEM((D_EXPERT, D_MODEL), BF16),
                        pltpu.SemaphoreType.DMA((_W_SLOTS, 3, _W_PARTS))])
    return pl.pallas_call(
        functools.partial(_expert_kernel, layer=layer), grid_spec=grid_spec,
        out_shape=jax.ShapeDtypeStruct((MOE_NBLK * MOE_BLK, D_MODEL), BF16),
        compiler_params=_params(("arbitrary",)), name="experts",
    )(blk_e, n_used, slot, nxt, x_pad, w_gate, w_up, w_down)


def _final_kernel(x_ref, y2_ref, rec_ref, gate_ref, fg_ref, o_ref):
    o_ref[...] = _rms(_with_pending(x_ref, (y2_ref, rec_ref, gate_ref)), fg_ref[...])


def _final_norm(x, pend, final_g, row0, n_rows):
    tm = 512
    tile0 = row0 // tm
    return pl.pallas_call(
        _final_kernel, grid=(n_rows // tm,),
        in_specs=[pl.BlockSpec((tm, D_MODEL), lambda i: (i + tile0, 0))] + _pending_specs(tm, tile0)
        + [pl.BlockSpec((1, D_MODEL), lambda i: (0, 0))],
        out_specs=pl.BlockSpec((tm, D_MODEL), lambda i: (i, 0)),
        out_shape=jax.ShapeDtypeStruct((n_rows, D_MODEL), F32),
        compiler_params=_params(("parallel",)), name="final_norm",
    )(x, *pend, final_g.reshape(1, D_MODEL))


_SC_WORKERS = 32
_SC_CORES = 2
_SC_ROWS = 64


def _dispatch_rows(hp, dest):
    n, width = hp.shape
    per_w = n // _SC_WORKERS
    n_ch = per_w // _SC_ROWS
    idx = dest.T.reshape(2, _SC_WORKERS, n_ch, _SC_ROWS)
    mesh = plsc.VectorSubcoreMesh(core_axis_name="c", subcore_axis_name="s")

    @functools.partial(
        pl.kernel, mesh=mesh, out_type=jax.ShapeDtypeStruct((MOE_NBLK * MOE_BLK, width), hp.dtype),
        scratch_types=[pltpu.VMEM((n_ch, _SC_ROWS), jnp.int32), pltpu.VMEM((n_ch, _SC_ROWS), jnp.int32),
                       pltpu.VMEM((_SC_ROWS, width), hp.dtype)], name="dispatch_rows")
    def scatter(h_hbm, idx_hbm, out_hbm, i0_v, i1_v, rows_v):
        wid = lax.axis_index("s") * _SC_CORES + lax.axis_index("c")
        pltpu.sync_copy(idx_hbm.at[0, wid], i0_v)
        pltpu.sync_copy(idx_hbm.at[1, wid], i1_v)

        @pl.loop(0, n_ch)
        def _(g):
            pltpu.sync_copy(h_hbm.at[pl.ds(wid * per_w + g * _SC_ROWS, _SC_ROWS)], rows_v)
            pltpu.sync_copy(rows_v, out_hbm.at[i0_v.at[g]])
            pltpu.sync_copy(rows_v, out_hbm.at[i1_v.at[g]])

    return scatter(hp, idx)


def _moe(h, rec, cnt, mods, w_gate, w_up, w_down, layer):
    dest, blk_e, n_used, slot, nxt = _plan(rec, cnt)
    y_pad = _experts(_dispatch_rows(h, dest), blk_e, n_used, slot, nxt, w_gate, w_up, w_down, layer)
    order = dest.reshape(N_TOK // _PEND_TM, _PEND_TM, 2).transpose(0, 2, 1).reshape(-1)
    return y_pad[order], rec, mods


def kernel(x_prompt, x_sample, c, cache_k, cache_v, state_delta, c_ctx, ada_w, ada_b, norm1_g, norm2_g, final_g,
           ev_w_in, ev_w_out, ev_conv_w, ev_a_log, ev_dt_bias, ev_onorm_g, ev_rpb, od_w_in, od_ln_g, od_ln_b,
           od_w_s, od_b_s, od_w_out, moe_w_rg, moe_b_rg, moe_w_re, moe_b_re, moe_w_gate, moe_w_up, moe_w_down):
    x = (x_prompt.reshape(N_CTX, D_MODEL), x_sample.reshape(N_LAT, D_MODEL))
    cond = jnp.concatenate([c_ctx[None, :], c, jnp.zeros((N_COND - 1 - DEC_BATCH, D_MODEL), F32)], axis=0)
    mods_all = _ada_mods(cond, ada_w, ada_b)
    kctx_all = cache_k.reshape(DEC_BATCH, -1, PAST_LEN, NA_HEADS * NA_HD)
    vctx_all = cache_v.reshape(DEC_BATCH, -1, PAST_LEN, NA_HEADS * NA_HD)

    ks, vs, ss = [], [], []
    pend = None
    for l in range(DEPTH):
        mods = mods_all[l]
        router = _router_weights(moe_w_rg[l], moe_b_rg[l], moe_w_re[l], moe_b_re[l])
        if l % 2 == 0:
            e = l // 2
            proj, ab, kv, x = _even_proj(x, pend, mods, norm1_g[l], ev_w_in[e])
            dn = (proj, ab, ev_conv_w[e], ev_a_log[e], ev_dt_bias[e], ev_onorm_g[e])
            oa_ctx, s_fin = _delta_heads(*dn, SEQ, BATCH, 0, None)
            oa_lat, _ = _delta_heads(*dn, DEC_SEQ, DEC_BATCH, N_CTX // DEC_SEQ, (state_delta, e))
            ob_ctx = _ctx_attention(proj)
            ob_lat = _na_attention(proj, kctx_all, vctx_all, e, ev_rpb[e])
            x, h, rec, cnt = _even_out(oa_ctx, ob_ctx, oa_lat, ob_lat, x, ev_w_out, e, mods, norm2_g[l],
                                       router)
            na_w = NA_HEADS * NA_HD
            ks.append(kv[:N_CTX, :na_w].reshape(BATCH, SEQ, NA_HEADS, NA_HD))
            vs.append(kv[:N_CTX, na_w:].reshape(BATCH, SEQ, NA_HEADS, NA_HD))
            ss.append(s_fin)
        else:
            o = l // 2
            x, h, rec, cnt = _sgu_layer(x, pend, mods, norm1_g[l], od_w_in[o], od_ln_g[o], od_ln_b[o],
                                        od_w_s[o], od_b_s[o], od_w_out[o], norm2_g[l], router)
        pend = _moe(h, rec, cnt, mods, moe_w_gate, moe_w_up, moe_w_down, l)
    y_prompt = _final_norm(x, pend, final_g, 0, N_CTX).reshape(BATCH, SEQ, D_MODEL)
    y_sample = _final_norm(x, pend, final_g, N_CTX, N_LAT).reshape(DEC_BATCH, DEC_SEQ, D_MODEL)
    return (y_prompt, y_sample, jnp.stack(ks, axis=1), jnp.stack(vs, axis=1), jnp.stack(ss, axis=1))
```

```python
import functools

import jax
import jax.numpy as jnp
from jax import lax
from jax.experimental import pallas as pl
from jax.experimental.pallas import tpu as pltpu
from jax.experimental.pallas import tpu_sc as plsc

F32 = jnp.float32
BF16 = jnp.bfloat16

D_MODEL = 1024
BATCH = 16
SEQ = 256
DEPTH = 4
DEC_BATCH = 4
DEC_SEQ = 2048
PAST_LEN = 512
GRID_W = 64
EPS = 1e-6
NEG_INF = -1e30

DN_HEADS = 4
DN_DK = 128
DN_CHUNK = 64
NA_HEADS = 8
NA_HD = 64
NA_ROWS = 8
NA_COLS = 16
SG_CHUNK = 128
SG_GROUPS = 8
SG_W = 2 * D_MODEL
SG_GW = SG_W // SG_GROUPS
N_EGROUPS = 4
EXP_PER_GROUP = 8
N_EXPERTS = 32
D_EXPERT = 512

N_CTX = BATCH * SEQ
N_LAT = DEC_BATCH * DEC_SEQ
N_TOK = N_CTX + N_LAT
N_COND = 8
PROJ_W = 4096
LANES = 128
MOE_BLK = 256
MOE_NBLK = -(-(2 * N_TOK + N_EXPERTS * (MOE_BLK - 1)) // MOE_BLK)
VMEM_LIMIT = 56 * 1024 * 1024

_QA, _KA, _VA, _ZA, _QB, _KB, _VB = 0, 4, 8, 12, 16, 20, 24


def _params(sem):
    return pltpu.CompilerParams(dimension_semantics=sem, vmem_limit_bytes=VMEM_LIMIT)


def _bdot(a, b):
    return jnp.dot(a.astype(BF16), b.astype(BF16), preferred_element_type=F32)


def _bdot_nt(a, b):
    return lax.dot_general(a.astype(BF16), b.astype(BF16), (((1,), (1,)), ((), ())),
                           preferred_element_type=F32)


def _bdot_tn(a, b):
    return lax.dot_general(a.astype(BF16), b.astype(BF16), (((0,), (0,)), ((), ())),
                           preferred_element_type=F32)


def _split2(a):
    p0 = a.astype(BF16)
    return p0, (a - p0.astype(F32)).astype(BF16)


def _dot3(a, b):
    ah = a.astype(BF16)
    al = (a - ah.astype(F32)).astype(BF16)
    bh = b.astype(BF16)
    bl = (b - bh.astype(F32)).astype(BF16)
    return (jnp.dot(ah, bh, preferred_element_type=F32) + jnp.dot(ah, bl, preferred_element_type=F32)
            + jnp.dot(al, bh, preferred_element_type=F32))


def _mask_bf16(m01):
    return jnp.where(m01, 1.0, 0.0).astype(BF16)


def _xdot(m01, a):
    m = _mask_bf16(m01)
    p0, p1 = _split2(a)
    return jnp.dot(m, p0, preferred_element_type=F32) + jnp.dot(m, p1, preferred_element_type=F32)


def _xdot_r(a, m01):
    m = _mask_bf16(m01)
    p0, p1 = _split2(a)
    return jnp.dot(p0, m, preferred_element_type=F32) + jnp.dot(p1, m, preferred_element_type=F32)


def _sigmoid(x):
    return 0.5 * jnp.tanh(0.5 * x) + 0.5


def _silu(x):
    hx = 0.5 * x
    return hx + hx * jnp.tanh(hx)


def _rms(x, g):
    return x * lax.rsqrt(jnp.mean(x * x, axis=-1, keepdims=True) + EPS) * g


def _cond_index(row):
    return jnp.where(row < N_CTX, 0, 1 + (row - N_CTX) // DEC_SEQ)


def _mod_spec(k, tm, tile0=0):
    return pl.BlockSpec((None, None, 1, D_MODEL), lambda i, *_: (_cond_index((i + tile0) * tm), k, 0, 0))


def _ada_kernel(c_ref, w_ref, b_ref, o_ref):
    o_ref[...] = _bdot(_silu(c_ref[...]), w_ref[...]) + b_ref[...]


def _ada_mods(cond, ada_w, ada_b):
    tn = 1536
    out = pl.pallas_call(
        _ada_kernel, grid=(DEPTH, 6 * D_MODEL // tn),
        in_specs=[pl.BlockSpec((N_COND, D_MODEL), lambda l, j: (0, 0)),
                  pl.BlockSpec((None, D_MODEL, tn), lambda l, j: (l, 0, j)),
                  pl.BlockSpec((None, 1, tn), lambda l, j: (l, 0, j))],
        out_specs=pl.BlockSpec((None, N_COND, tn), lambda l, j: (l, 0, j)),
        out_shape=jax.ShapeDtypeStruct((DEPTH, N_COND, 6 * D_MODEL), F32),
        compiler_params=_params(("parallel", "parallel")), name="ada_mods",
    )(cond, ada_w, ada_b.reshape(DEPTH, 1, 6 * D_MODEL))
    return out.reshape(DEPTH, N_COND, 6, 1, D_MODEL)


_EV_TN = 512
_EV_W = 7 * DN_HEADS * LANES
_KV_COL0 = _KB * LANES


def _with_pending(x_ref, pend):
    if not pend:
        return x_ref[...]
    y2_ref, rec_ref, gate_ref = pend
    rec = rec_ref[...]
    tm = rec.shape[0]
    y = (rec[:, _R_W:_R_W + 1] * y2_ref[:tm, :].astype(F32)
         + rec[:, _R_W + 1:_R_W + 2] * y2_ref[tm:, :].astype(F32))
    return x_ref[...] + gate_ref[...] * y


_PEND_TM = 512


def _pending_specs(tm, tile0=0):
    assert tm == _PEND_TM
    return [pl.BlockSpec((2 * tm, D_MODEL), lambda i: (i + tile0, 0)),
            pl.BlockSpec((tm, LANES), lambda i: (i + tile0, 0)), _mod_spec(5, tm, tile0)]


def _even_proj_kernel(x_ref, *rest, n_pend, ctx_tiles):
    pend, (g_ref, sh_ref, sc_ref, w_ref, wab_ref, o_ref, ab_ref, kv_ref) = rest[:n_pend], rest[n_pend:n_pend + 8]
    if n_pend == 1:
        x = jnp.where(pl.program_id(0) < ctx_tiles, x_ref[...], pend[0][...])
    else:
        x = _with_pending(x_ref, pend)
    if pend:
        rest[n_pend + 8][...] = x
    h = (_rms(x, g_ref[...]) * (1.0 + sc_ref[...]) + sh_ref[...]).astype(BF16)
    ab_ref[...] = jnp.dot(h, wab_ref[...], preferred_element_type=F32)
    for j in range(_EV_W // _EV_TN):
        c0 = j * _EV_TN
        y = jnp.dot(h, w_ref[:, c0:c0 + _EV_TN], preferred_element_type=F32)
        for c in range(_EV_TN // LANES):
            o_ref[c0 // LANES + c] = y[:, c * LANES:(c + 1) * LANES].astype(BF16)
        if c0 >= _KV_COL0:
            kv_ref[:, c0 - _KV_COL0:c0 - _KV_COL0 + _EV_TN] = y


def _even_proj(x, pend, mods, g, w_in):
    tm = 512
    ctx_tiles = N_CTX // tm
    rows = pl.BlockSpec((tm, D_MODEL), lambda i: (i, 0))
    if isinstance(x, tuple):
        x, pend = x[0], (x[1],)
        x_specs = [pl.BlockSpec((tm, D_MODEL), lambda i: (jnp.minimum(i, ctx_tiles - 1), 0)),
                   pl.BlockSpec((tm, D_MODEL), lambda i: (jnp.maximum(i - ctx_tiles, 0), 0))]
    else:
        pend = tuple(pend) if pend else ()
        x_specs = [rows] + (_pending_specs(tm) if pend else [])
    n_ab = 4 * DN_HEADS
    ab0 = 4 * DN_HEADS * DN_DK
    w_main = jnp.concatenate([w_in[:, :ab0], w_in[:, ab0 + n_ab:]], axis=1).astype(BF16)
    w_ab = jnp.concatenate([w_in[:, ab0:ab0 + n_ab], jnp.zeros((D_MODEL, LANES - n_ab), F32)],
                           axis=1).astype(BF16)
    held = lambda shape: pl.BlockSpec(shape, lambda i: (0,) * len(shape), pipeline_mode=pl.Buffered(1))
    out_specs = [pl.BlockSpec((_EV_W // LANES, tm, LANES), lambda i: (0, i, 0)),
                 pl.BlockSpec((tm, LANES), lambda i: (i, 0)),
                 pl.BlockSpec((tm, 2 * NA_HEADS * NA_HD), lambda i: (i, 0))]
    out_shape = [jax.ShapeDtypeStruct((_EV_W // LANES, N_TOK, LANES), BF16),
                 jax.ShapeDtypeStruct((N_TOK, LANES), F32),
                 jax.ShapeDtypeStruct((N_TOK, 2 * NA_HEADS * NA_HD), F32)]
    if pend:
        out_specs.append(rows)
        out_shape.append(jax.ShapeDtypeStruct((N_TOK, D_MODEL), F32))
    res = pl.pallas_call(
        functools.partial(_even_proj_kernel, n_pend=len(pend), ctx_tiles=ctx_tiles), grid=(N_TOK // tm,),
        in_specs=x_specs
        + [pl.BlockSpec((1, D_MODEL), lambda i: (0, 0)), _mod_spec(0, tm), _mod_spec(1, tm),
           held((D_MODEL, _EV_W)), held((D_MODEL, LANES))],
        out_specs=out_specs, out_shape=out_shape,
        compiler_params=_params(("parallel",)), name="even_proj",
    )(x, *pend, g.reshape(1, D_MODEL), mods, mods, w_main, w_ab)
    return (*res[:3], res[3] if pend else x)


_CHUNK_SHIFT = DN_CHUNK.bit_length() - 1
_CUM_ROWS = 256
_DN_CHAINS = 16
_DN_SHORT = 256
_MQ_ROWS = DN_DK + DN_CHUNK


def _dn_kernel(*refs, T, HB, has_s0, want_state):
    it = iter(refs)
    q_ref, k_ref, v_ref, z_ref, ab_ref = (next(it) for _ in range(5))
    cwq_ref, cwk_ref, cwv_ref, alog_ref, dtb_ref, og_ref = (next(it) for _ in range(6))
    s0_ref = next(it) if has_s0 else None
    o_ref = next(it)
    sfin_ref = next(it) if want_state else None
    qc, kc, vc, gsc, bsc, osc, b_s, mq_s = (next(it) for _ in range(8))

    C = DN_CHUNK
    n = T // C
    h0 = pl.program_id(1) * HB

    row = lax.broadcasted_iota(jnp.int32, (T, 1), 0)

    def conv(x_ref, cw_ref, hh):
        x = x_ref[hh].astype(F32)
        cw = cw_ref[:, hh * LANES:(hh + 1) * LANES]
        xp = jnp.where(row == 0, 0.0, pltpu.roll(x, 1, 0))
        xn = jnp.where(row == T - 1, 0.0, pltpu.roll(x, T - 1, 0))
        return _silu(cw[0:1, :] * xp + cw[1:2, :] * x + cw[2:3, :] * xn)

    def l2n(x):
        return x * lax.rsqrt(jnp.sum(x * x, axis=-1, keepdims=True) + EPS)

    ab = ab_ref[...]
    lane = lax.broadcasted_iota(jnp.int32, (1, LANES), 1)
    dtb = jnp.zeros((1, LANES), F32)
    alog = jnp.zeros((1, LANES), F32)
    for d in range(2):
        for hq in range(DN_HEADS):
            dtb = jnp.where(lane == d * DN_HEADS + hq, dtb_ref[d, hq], dtb)
            alog = jnp.where(lane == d * DN_HEADS + hq, alog_ref[d, hq], alog)
    xs = ab + dtb
    g_all = -jnp.exp(alog) * (jnp.maximum(xs, 0.0) + jnp.log(1.0 + jnp.exp(-jnp.abs(xs))))
    beta_all = _sigmoid(ab)

    sel_r = lax.broadcasted_iota(jnp.int32, (LANES, LANES), 0)
    for hh in range(HB):
        qc[hh] = l2n(conv(q_ref, cwq_ref, hh)) * (DN_DK ** -0.5)
        kc[hh] = l2n(conv(k_ref, cwk_ref, hh))
        vc[hh] = conv(v_ref, cwv_ref, hh)
        hd = h0 + hh
        for d in range(2):
            gsc[hh, d] = _xdot_r(g_all, sel_r == d * DN_HEADS + hd)
            bsc[hh, d] = _xdot_r(beta_all, sel_r == 2 * DN_HEADS + d * DN_HEADS + hd)

    pr = lax.broadcasted_iota(jnp.int32, (_CUM_ROWS, _CUM_ROWS), 0)
    pc = lax.broadcasted_iota(jnp.int32, (_CUM_ROWS, _CUM_ROWS), 1)
    same = lax.shift_right_logical(pr, _CHUNK_SHIFT) == lax.shift_right_logical(pc, _CHUNK_SHIFT)
    cum_mask = (jnp.logical_and(same, pc <= pr), jnp.logical_and(same, pc >= pr))

    def cum_body(i, carry):
        sl = pl.ds(pl.multiple_of(i * _CUM_ROWS, _CUM_ROWS), _CUM_ROWS)
        for hh in range(HB):
            for d in range(2):
                gsc[hh, d, sl, :] = _xdot(cum_mask[d], gsc[hh, d, sl, :])
        return carry

    lax.fori_loop(0, T // _CUM_ROWS, cum_body, 0, unroll=True)

    ri = lax.broadcasted_iota(jnp.int32, (C, C), 0)
    ci = lax.broadcasted_iota(jnp.int32, (C, C), 1)
    eye = (ri == ci).astype(F32)

    def prepare(items):
        lows, decays = [], []
        kk, qk = {}, {}
        for hh, d, c, slot in items:
            if (hh, slot) not in kk:
                sl = pl.ds(pl.multiple_of(c * C, C), C)
                k = kc[hh, sl, :]
                kk[hh, slot] = _bdot_nt(k, k)
                qk[hh, slot] = _bdot_nt(qc[hh, sl, :], k)
        for hh, d, c, slot in items:
            sl = pl.ds(pl.multiple_of(c * C, C), C)
            gc = gsc[hh, d, sl, :]
            incl = (ci <= ri) if d == 0 else (ci >= ri)
            strict = (ci < ri) if d == 0 else (ci > ri)
            gr = jnp.transpose(gc)[0:1, :C]
            decay = jnp.where(incl, jnp.exp(jnp.where(incl, gc[:, :C] - gr, 0.0)), 0.0)
            lows.append(jnp.where(strict, bsc[hh, d, sl, :C] * kk[hh, slot] * decay, 0.0))
            decays.append(decay)
        ts = [eye - low for low in lows]
        ps = lows
        for step in range(_CHUNK_SHIFT - 1):
            ps = [_dot3(p, p) for p in ps]
            ts = [t + _dot3(t, p) for t, p in zip(ts, ps)]
        wus, kds, attns, qds = [], [], [], []
        for (hh, d, c, slot), t, decay in zip(items, ts, decays):
            sl = pl.ds(pl.multiple_of(c * C, C), C)
            k, gc, beta = kc[hh, sl, :], gsc[hh, d, sl, :], bsc[hh, d, sl, :]
            eg = jnp.exp(gc)
            uw = _bdot(t, jnp.concatenate([vc[hh, sl, :] * beta, k * beta * eg], axis=-1))
            last = gc[C - 1:C, :] if d == 0 else gc[0:1, :]
            wus.append(jnp.concatenate([uw[:, LANES:], uw[:, :LANES]], axis=-1).astype(BF16))
            kds.append((k * jnp.exp(last - gc)).astype(BF16))
            attns.append((qk[hh, slot] * decay).astype(BF16))
            qds.append(qc[hh, sl, :] * eg)
        kdwus = [lax.dot_general(kd, wu, (((0,), (0,)), ((), ())), preferred_element_type=F32)
                 for kd, wu in zip(kds, wus)]
        awus = [jnp.dot(attn, wu, preferred_element_type=F32) for attn, wu in zip(attns, wus)]
        for (hh, d, c, slot), kdwu, awu, qd in zip(items, kdwus, awus, qds):
            mq0 = pl.multiple_of(c * _MQ_ROWS, _MQ_ROWS)
            mq_s[hh, d, pl.ds(mq0, DN_DK), :] = kdwu[:, :LANES].astype(BF16)
            mq_s[hh, d, pl.ds(mq0 + DN_DK, C), :] = (qd - awu[:, :LANES]).astype(BF16)
            b_s[hh, d, pl.ds(pl.multiple_of(c * DN_DK, DN_DK), DN_DK), :] = kdwu[:, LANES:]
            osc[hh, d, pl.ds(pl.multiple_of(c * C, C), C), :] = awu[:, LANES:]

    n_prep = min(n, _DN_CHAINS // 2)
    h_prep = max(1, min(HB, _DN_CHAINS // (2 * n_prep)))

    def prep_body(i, carry):
        for hg in range(0, HB, h_prep):
            prepare([(hh, d, i * n_prep + j, j) for hh in range(hg, hg + h_prep) for j in range(n_prep)
                     for d in range(2)])
        return carry

    lax.fori_loop(0, n // n_prep, prep_body, 0)

    def body(i, carry):
        chains = [(hh, d, i if d == 0 else n - 1 - i) for hh in range(HB) for d in range(2)]
        mss = [jnp.dot(mq_s[hh, d, pl.ds(pl.multiple_of(c * _MQ_ROWS, _MQ_ROWS), _MQ_ROWS), :],
                       S.astype(BF16), preferred_element_type=F32)
               for (hh, d, c), S in zip(chains, carry)]
        new = []
        for (hh, d, c), S, ms in zip(chains, carry, mss):
            sl = pl.ds(pl.multiple_of(c * C, C), C)
            osc[hh, d, sl, :] = osc[hh, d, sl, :] + ms[DN_DK:]
            last = gsc[hh, d, pl.ds(c * C + (C - 1 if d == 0 else 0), 1), :]
            new.append(S * jnp.exp(last) - ms[:DN_DK]
                       + b_s[hh, d, pl.ds(pl.multiple_of(c * DN_DK, DN_DK), DN_DK), :])
        return tuple(new)

    if has_s0:
        init = tuple(s0_ref[d, hh] for hh in range(HB) for d in range(2))
    else:
        init = tuple(jnp.zeros((DN_DK, LANES), F32) for _ in range(2 * HB))
    fin = lax.fori_loop(0, n, body, init)
    for hh in range(HB):
        if want_state:
            sfin_ref[0, hh] = fin[2 * hh]
            sfin_ref[1, hh] = fin[2 * hh + 1]
        o = osc[hh, 0] + osc[hh, 1]
        o_ref[hh] = (_rms(o, og_ref[...]) * _silu(z_ref[hh].astype(F32))).astype(o_ref.dtype)


def _delta_heads(proj, ab, conv_w, a_log, dt_bias, onorm_g, T, n_seq, row0, s0):
    has_s0 = s0 is not None
    want_state = not has_s0
    hb = DN_HEADS if T <= _DN_SHORT else 2

    def col(cb):
        return pl.BlockSpec((hb, T, LANES), lambda s, h: (cb // hb + h, row0 + s, 0))

    def cw(cb):
        return pl.BlockSpec((3, hb * LANES), lambda s, h: (0, cb // hb + h))

    smem = pl.BlockSpec(memory_space=pltpu.SMEM)
    in_specs = [col(_QA), col(_KA), col(_VA), col(_ZA),
                pl.BlockSpec((T, LANES), lambda s, h: (row0 + s, 0)),
                cw(0), cw(4), cw(8), smem, smem,
                pl.BlockSpec((1, LANES), lambda s, h: (0, 0))]
    args = [proj, proj, proj, proj, ab, conv_w, conv_w, conv_w, a_log, dt_bias,
            onorm_g.reshape(1, LANES)]
    state_spec = pl.BlockSpec((None, 2, hb, DN_DK, LANES), lambda s, h: (s, 0, h, 0, 0))
    if has_s0:
        states, layer = s0
        in_specs.append(pl.BlockSpec((None, None, 2, hb, DN_DK, LANES), lambda s, h: (s, layer, 0, h, 0, 0)))
        args.append(states)
    out_shape = [jax.ShapeDtypeStruct((DN_HEADS, n_seq * T, LANES), BF16)]
    out_specs = [pl.BlockSpec((hb, T, LANES), lambda s, h: (h, s, 0))]
    if want_state:
        out_shape.append(jax.ShapeDtypeStruct((n_seq, 2, DN_HEADS, DN_DK, LANES), F32))
        out_specs.append(state_spec)
    res = pl.pallas_call(
        functools.partial(_dn_kernel, T=T, HB=hb, has_s0=has_s0, want_state=want_state),
        grid=(n_seq, DN_HEADS // hb), in_specs=in_specs, out_specs=out_specs, out_shape=out_shape,
        scratch_shapes=[pltpu.VMEM((hb, T, LANES), F32)] * 3
        + [pltpu.VMEM((hb, 2, T, LANES), F32)] * 3
        + [pltpu.VMEM((hb, 2, T // DN_CHUNK * DN_DK, LANES), F32),
           pltpu.VMEM((hb, 2, T // DN_CHUNK * _MQ_ROWS, LANES), BF16)],
        compiler_params=_params(("parallel", "parallel")), name="delta_heads_%d" % T,
    )(*args)
    return res if want_state else (res[0], None)


def _pair_queries(q, first):
    return jnp.concatenate([jnp.where(first, q, 0.0), jnp.where(first, 0.0, q)], axis=0).astype(BF16)


def _ctx_attn_kernel(q_ref, k_ref, v_ref, o_ref):
    first = lax.broadcasted_iota(jnp.int32, (SEQ, LANES), 1) < NA_HD
    pairs = range(NA_HEADS // 2)
    ss = [lax.dot_general(k_ref[p], _pair_queries(q_ref[p] * (NA_HD ** -0.5), first),
                          (((1,), (1,)), ((), ())), preferred_element_type=F32) for p in pairs]
    es = [jnp.exp(s - jnp.max(s, axis=0, keepdims=True)) for s in ss]
    dens = [jnp.sum(e, axis=0, keepdims=True) for e in es]
    os = [lax.dot_general(e.astype(BF16), v_ref[p], (((0,), (0,)), ((), ())), preferred_element_type=F32)
          for p, e in zip(pairs, es)]
    for p, o, den in zip(pairs, os, dens):
        o = jnp.where(first, o[:SEQ], o[SEQ:])
        den_t = jnp.transpose(jnp.broadcast_to(den, (LANES, 2 * SEQ)))
        o_ref[p] = (o / jnp.where(first, den_t[:SEQ], den_t[SEQ:])).astype(o_ref.dtype)


def _ctx_attention(proj):
    n_pairs = NA_HEADS // 2

    def col(cb):
        return pl.BlockSpec((n_pairs, SEQ, LANES), lambda s: (cb // n_pairs, s, 0))

    return pl.pallas_call(
        _ctx_attn_kernel, grid=(BATCH,),
        in_specs=[col(_QB), col(_KB), col(_VB)],
        out_specs=pl.BlockSpec((n_pairs, SEQ, LANES), lambda s: (0, s, 0)),
        out_shape=jax.ShapeDtypeStruct((n_pairs, N_CTX, LANES), BF16),
        compiler_params=_params(("parallel",)), name="ctx_attention",
    )(proj, proj, proj)


_NA_UNROLL = 4


def _na_kernel(q_ref, k_ref, v_ref, kc_ref, vc_ref, bias_ref, o_ref, kcb_scr, vcb_scr):
    rows = DEC_SEQ // GRID_W
    win = NA_ROWS * GRID_W
    scale = NA_HD ** -0.5
    dn_nt = (((1,), (1,)), ((), ()))
    dn_tn = (((0,), (0,)), ((), ()))

    kcb_scr[...] = kc_ref[...].astype(BF16)
    vcb_scr[...] = vc_ref[...].astype(BF16)
    first = lax.broadcasted_iota(jnp.int32, (GRID_W, LANES), 1) < NA_HD

    def scores(group):
        out = []
        for r in range(group * _NA_UNROLL, (group + 1) * _NA_UNROLL):
            rs = min(max(r - NA_ROWS // 2, 0), rows - NA_ROWS)
            qsl, wsl = pl.ds(r * GRID_W, GRID_W), pl.ds(rs * GRID_W, win)
            qm = _pair_queries(q_ref[qsl, :] * scale, first)
            bias = jnp.concatenate([bias_ref[NA_ROWS - 1 - (r - rs) + i] for i in range(NA_ROWS)], axis=0)
            s_win = lax.dot_general(k_ref[wsl, :], qm, dn_nt, preferred_element_type=F32) + bias
            s_ctx = lax.dot_general(kcb_scr[...], qm, dn_nt, preferred_element_type=F32)
            out.append((qsl, wsl, s_win, s_ctx))
        return out

    n_groups = rows // _NA_UNROLL
    nxt = scores(0)
    for group in range(n_groups):
        cur, nxt = nxt, (scores(group + 1) if group + 1 < n_groups else None)
        ms = [jnp.maximum(jnp.max(sw, axis=0, keepdims=True), jnp.max(sc, axis=0, keepdims=True))
              for _, _, sw, sc in cur]
        e_wins = [jnp.exp(sw - m) for (_, _, sw, _), m in zip(cur, ms)]
        e_ctxs = [jnp.exp(sc - m) for (_, _, _, sc), m in zip(cur, ms)]
        dens = [jnp.sum(ew, axis=0, keepdims=True) + jnp.sum(ec, axis=0, keepdims=True)
                for ew, ec in zip(e_wins, e_ctxs)]
        for (qsl, wsl, _, _), ew, ec, den in zip(cur, e_wins, e_ctxs, dens):
            o = (lax.dot_general(ew.astype(BF16), v_ref[wsl, :], dn_tn, preferred_element_type=F32)
                 + lax.dot_general(ec.astype(BF16), vcb_scr[...], dn_tn, preferred_element_type=F32))
            o = o / jnp.transpose(jnp.broadcast_to(den, (LANES, LANES)))
            o_ref[qsl, :] = jnp.where(first, o[:GRID_W], o[GRID_W:]).astype(o_ref.dtype)


def _na_bias_table(rpb):
    col = jnp.arange(GRID_W)
    cs = jnp.clip(col - NA_COLS // 2, 0, GRID_W - NA_COLS)
    col_ok = (col[None, :] >= cs[:, None]) & (col[None, :] < cs[:, None] + NA_COLS)
    dc = jnp.clip(col[None, :] - col[:, None] + NA_COLS - 1, 0, 2 * NA_COLS - 2)
    onehot = (dc.T[None, :, :] == jnp.arange(2 * NA_COLS - 1)[:, None, None]).astype(F32)
    t = jnp.einsum('hrd,dkq->hrkq', rpb.astype(F32), onehot, precision=lax.Precision.HIGHEST)
    t = jnp.where(col_ok.T[None, None], t, NEG_INF)
    t = t.reshape(NA_HEADS // 2, 2, 2 * NA_ROWS - 1, GRID_W, GRID_W)
    return jnp.concatenate([t[:, 0], t[:, 1]], axis=-1)


def _na_attention(proj, kctx, vctx, layer, rpb):
    blk = N_CTX // DEC_SEQ

    def col(cb):
        return pl.BlockSpec((None, DEC_SEQ, LANES), lambda b, p: (cb + p, blk + b, 0))

    ctx = pl.BlockSpec((None, None, PAST_LEN, LANES), lambda b, p: (b, layer, 0, p))
    return pl.pallas_call(
        _na_kernel, grid=(DEC_BATCH, NA_HEADS // 2),
        in_specs=[col(_QB), col(_KB), col(_VB), ctx, ctx,
                  pl.BlockSpec((None, 2 * NA_ROWS - 1, GRID_W, 2 * GRID_W), lambda b, p: (p, 0, 0, 0))],
        out_specs=pl.BlockSpec((None, DEC_SEQ, LANES), lambda b, p: (p, b, 0)),
        out_shape=jax.ShapeDtypeStruct((NA_HEADS // 2, N_LAT, LANES), BF16),
        scratch_shapes=[pltpu.VMEM((PAST_LEN, LANES), BF16), pltpu.VMEM((PAST_LEN, LANES), BF16)],
        compiler_params=_params(("parallel", "parallel")), name="na_attention",
    )(proj, proj, proj, kctx, vctx, _na_bias_table(rpb))


_LOGIT0 = N_EGROUPS
_R_E, _R_W, _R_RANK = 0, 2, 4


def _lane_min_where(mask, lane):
    return jnp.min(jnp.where(mask, lane, LANES), axis=-1, keepdims=True)


def _route_rows(lg, carry_ref, tri_ref):
    big = -3.0e38
    lane = lax.broadcasted_iota(jnp.int32, lg.shape, 1)
    is_g = lane < N_EGROUPS
    gmax = jnp.max(jnp.where(is_g, lg, big), axis=-1, keepdims=True)
    gsum = jnp.sum(jnp.where(is_g, jnp.exp(jnp.where(is_g, lg - gmax, 0.0)), 0.0), axis=-1, keepdims=True)
    pg_top = 1.0 / gsum
    g_idx = _lane_min_where(jnp.logical_and(is_g, lg == gmax), lane)
    in_g = jnp.logical_and(lane >= _LOGIT0, lax.shift_right_arithmetic(lane - _LOGIT0, 3) == g_idx)
    in_g = jnp.logical_and(in_g, lane < _LOGIT0 + N_EXPERTS)
    m1 = jnp.max(jnp.where(in_g, lg, big), axis=-1, keepdims=True)
    i1 = _lane_min_where(jnp.logical_and(in_g, lg == m1), lane)
    rest = jnp.logical_and(in_g, lane != i1)
    m2 = jnp.max(jnp.where(rest, lg, big), axis=-1, keepdims=True)
    i2 = _lane_min_where(jnp.logical_and(rest, lg == m2), lane)
    e2 = jnp.exp(m2 - m1)
    w1 = pg_top * (1.0 / (1.0 + e2))
    w2 = pg_top * (e2 / (1.0 + e2))
    hit1 = lane == i1
    hit2 = lane == i2
    picked = jnp.where(jnp.logical_or(hit1, hit2), 1.0, 0.0)
    before = jnp.dot(tri_ref[...], picked.astype(BF16), preferred_element_type=F32) + carry_ref[...]
    r1 = jnp.sum(jnp.where(hit1, before, 0.0), axis=-1, keepdims=True)
    r2 = jnp.sum(jnp.where(hit2, before, 0.0), axis=-1, keepdims=True)
    carry_ref[...] = carry_ref[...] + jnp.sum(picked, axis=0, keepdims=True)
    rec = jnp.zeros(lg.shape, F32)
    for ln, val in ((_R_E, (i1 - _LOGIT0).astype(F32)), (_R_E + 1, (i2 - _LOGIT0).astype(F32)),
                    (_R_W, w1), (_R_W + 1, w2), (_R_RANK, r1), (_R_RANK + 1, r2)):
        rec = jnp.where(lane == ln, val, rec)
    return rec


_PACK_W = D_MODEL // 2


def _pack_rows(hb):
    lo = lax.bitcast_convert_type(hb[:, :_PACK_W].astype(F32), jnp.int32)
    hi = lax.bitcast_convert_type(hb[:, _PACK_W:].astype(F32), jnp.int32)
    return jnp.bitwise_or(jnp.bitwise_and(hi, -65536), lax.shift_right_logical(lo, 16))


def _unpack_rows(w):
    lo = lax.bitcast_convert_type(lax.shift_left(w, 16), F32)
    hi = lax.bitcast_convert_type(jnp.bitwise_and(w, -65536), F32)
    return jnp.concatenate([lo, hi], axis=-1).astype(BF16)


def _moe_input(xnew, first, tail_in, tail_out, tail_scr):
    g2_ref, sc2_ref, sh2_ref, wr_ref, br_ref = tail_in
    x_out, h_out, rec_out, cnt_out = tail_out
    tri_scr, carry_scr = tail_scr

    @pl.when(first)
    def _():
        tm = tri_scr.shape[0]
        r = lax.broadcasted_iota(jnp.int32, (tm, tm), 0)
        c = lax.broadcasted_iota(jnp.int32, (tm, tm), 1)
        tri_scr[...] = jnp.where(c < r, 1.0, 0.0).astype(BF16)
        carry_scr[...] = jnp.zeros(carry_scr.shape, F32)

    x_out[...] = xnew
    h = _rms(xnew, g2_ref[...]) * (1.0 + sc2_ref[...]) + sh2_ref[...]
    hh = h.astype(BF16)
    h_out[...] = _pack_rows(hh)
    lg = jnp.dot(hh, wr_ref[...], preferred_element_type=F32) + br_ref[...]
    rec_out[...] = _route_rows(lg, carry_scr, tri_scr)
    cnt_out[...] = carry_scr[...]


def _even_out_kernel(oac_ref, obc_ref, oal_ref, obl_ref, x_ref, w_ref, gate_ref, *rest, ctx_tiles):
    tail_in, tail_out, (w_scr,), tail_scr = rest[:5], rest[5:9], rest[9:10], rest[10:]
    first = pl.program_id(0) == 0

    @pl.when(first)
    def _():
        w_scr[...] = w_ref[...].astype(BF16)

    is_ctx = pl.program_id(0) < ctx_tiles
    parts = [jnp.where(is_ctx, c_ref[hb], l_ref[hb])
             for c_ref, l_ref in ((oac_ref, oal_ref), (obc_ref, obl_ref)) for hb in range(DN_HEADS)]
    mix = jnp.concatenate(parts, axis=-1)
    out = jnp.dot(mix, w_scr[...], preferred_element_type=F32)
    _moe_input(x_ref[...] + gate_ref[...] * out, first, tail_in, tail_out, tail_scr)


def _tail_specs(tm):
    const = lambda shape: pl.BlockSpec(shape, lambda i: (0,) * len(shape))
    in_specs = [_mod_spec(2, tm), const((1, D_MODEL)), _mod_spec(4, tm), _mod_spec(3, tm),
                const((D_MODEL, LANES)), const((1, LANES))]
    out_specs = [pl.BlockSpec((tm, D_MODEL), lambda i: (i, 0)),
                 pl.BlockSpec((tm, _PACK_W), lambda i: (i, 0)),
                 pl.BlockSpec((tm, LANES), lambda i: (i, 0)),
                 const((1, LANES))]
    out_shape = [jax.ShapeDtypeStruct((N_TOK, D_MODEL), F32),
                 jax.ShapeDtypeStruct((N_TOK, _PACK_W), jnp.int32),
                 jax.ShapeDtypeStruct((N_TOK, LANES), F32),
                 jax.ShapeDtypeStruct((1, LANES), F32)]
    scratch = [pltpu.VMEM((tm, tm), BF16), pltpu.VMEM((1, LANES), F32)]
    return in_specs, out_specs, out_shape, scratch


def _router_weights(w_rg, b_rg, w_re, b_re):
    pad = LANES - N_EGROUPS - N_EXPERTS
    w = jnp.concatenate([w_rg, w_re, jnp.zeros((D_MODEL, pad), F32)], axis=1)
    b = jnp.concatenate([b_rg, b_re, jnp.zeros((pad,), F32)]).reshape(1, LANES)
    return w.astype(BF16), b


def _even_out(oa_ctx, ob_ctx, oa_lat, ob_lat, x, w_out, layer, mods, g2, router):
    tm = 512
    ctx_tiles = N_CTX // tm
    tail_in, out_specs, out_shape, tail_scr = _tail_specs(tm)
    ctxblk = pl.BlockSpec((DN_HEADS, tm, LANES), lambda i: (0, jnp.minimum(i, ctx_tiles - 1), 0))
    latblk = pl.BlockSpec((DN_HEADS, tm, LANES), lambda i: (0, jnp.maximum(i - ctx_tiles, 0), 0))
    return pl.pallas_call(
        functools.partial(_even_out_kernel, ctx_tiles=ctx_tiles), grid=(N_TOK // tm,),
        in_specs=[ctxblk, ctxblk, latblk, latblk, pl.BlockSpec((tm, D_MODEL), lambda i: (i, 0)),
                  pl.BlockSpec((None, D_MODEL, D_MODEL), lambda i: (layer, 0, 0))] + tail_in,
        out_specs=out_specs, out_shape=out_shape,
        scratch_shapes=[pltpu.VMEM((D_MODEL, D_MODEL), BF16)] + tail_scr,
        compiler_params=_params(("arbitrary",)), name="even_out",
    )(oa_ctx, ob_ctx, oa_lat, ob_lat, x, w_out, mods, g2.reshape(1, D_MODEL), mods, mods, *router)


def _gelu_tanh(x):
    c = 0.7978845608028654
    hx = 0.5 * x
    return hx + hx * jnp.tanh(x * (c + (c * 0.044715) * (x * x)))


def _sgu_kernel(x_ref, *rest, tm, n_pend):
    pend, rest = rest[:n_pend], rest[n_pend:]
    g1_ref, sh1_ref, sc1_ref, win_ref, lng_ref, lnb_ref, ws_ref, bst_ref, wout_ref, gate_ref = rest[:10]
    rest = rest[10:]
    tail_in, tail_out, (v_scr, m_scr), tail_scr = rest[:5], rest[5:9], rest[9:11], rest[11:]
    first = pl.program_id(0) == 0
    x = _with_pending(x_ref, pend)
    h = (_rms(x, g1_ref[...]) * (1.0 + sc1_ref[...]) + sh1_ref[...]).astype(BF16)

    def proj_u(g):
        return jnp.dot(h, win_ref[:, g * SG_GW:(g + 1) * SG_GW], preferred_element_type=F32)

    v = _gelu_tanh(jnp.dot(h, win_ref[:, SG_W:], preferred_element_type=F32))
    mu = jnp.mean(v, axis=-1, keepdims=True)
    vc = v - mu
    var = jnp.mean(vc * vc, axis=-1, keepdims=True)
    v_scr[...] = (vc * lax.rsqrt(var + EPS) * lng_ref[...] + lnb_ref[...]).astype(BF16)

    u_next = proj_u(0)
    for g in range(SG_GROUPS):
        cs = slice(g * SG_GW, (g + 1) * SG_GW)
        u_raw, u_next = u_next, (proj_u(g + 1) if g + 1 < SG_GROUPS else None)
        w_sp = ws_ref[g].astype(BF16)
        chunks = [slice(c * SG_CHUNK, (c + 1) * SG_CHUNK) for c in range(tm // SG_CHUNK)]
        sps = [jnp.dot(w_sp, v_scr[rs, cs], preferred_element_type=F32) for rs in chunks]
        u = _gelu_tanh(u_raw)
        for rs, sp in zip(chunks, sps):
            m_scr[rs, cs] = (u[rs] * (sp + bst_ref[:, g:g + 1])).astype(BF16)
    out = jnp.dot(m_scr[...], wout_ref[...], preferred_element_type=F32)
    _moe_input(x + gate_ref[...] * out, first, tail_in, tail_out, tail_scr)


def _sgu_layer(x, pend, mods, g1, w_in, ln_g, ln_b, w_s, b_s, w_out, g2, router):
    tm = 512
    pend = tuple(pend) if pend else ()
    tail_in, out_specs, out_shape, tail_scr = _tail_specs(tm)
    const = lambda shape: pl.BlockSpec(shape, lambda i: (0,) * len(shape))
    held = lambda shape: pl.BlockSpec(shape, lambda i: (0,) * len(shape), pipeline_mode=pl.Buffered(1))
    return pl.pallas_call(
        functools.partial(_sgu_kernel, tm=tm, n_pend=len(pend)), grid=(N_TOK // tm,),
        in_specs=[pl.BlockSpec((tm, D_MODEL), lambda i: (i, 0))] + (_pending_specs(tm) if pend else [])
        + [const((1, D_MODEL)), _mod_spec(0, tm), _mod_spec(1, tm),
                  held((D_MODEL, 2 * SG_W)), const((1, SG_W)), const((1, SG_W)),
                  const((SG_GROUPS, SG_CHUNK, SG_CHUNK)), const((SG_CHUNK, SG_GROUPS)),
                  held((SG_W, D_MODEL))] + tail_in,
        out_specs=out_specs, out_shape=out_shape,
        scratch_shapes=[pltpu.VMEM((tm, SG_W), BF16), pltpu.VMEM((tm, SG_W), BF16)] + tail_scr,
        compiler_params=_params(("arbitrary",)), name="sgu_layer",
    )(x, *pend, g1.reshape(1, D_MODEL), mods, mods, w_in.astype(BF16), ln_g.reshape(1, SG_W), ln_b.reshape(1, SG_W),
      w_s, b_s.T, w_out.astype(BF16), mods, g2.reshape(1, D_MODEL), mods, mods, *router)


def _plan(rec, cnt):
    e_idx = rec[:, _R_E:_R_E + 2].astype(jnp.int32)
    rank = rec[:, _R_RANK:_R_RANK + 2].astype(jnp.int32)
    counts = cnt[0, _LOGIT0:_LOGIT0 + N_EXPERTS].astype(jnp.int32)
    padded = (counts + MOE_BLK - 1) // MOE_BLK * MOE_BLK
    pad_end = jnp.cumsum(padded)
    pad_start = pad_end - padded
    hit = e_idx[:, :, None] == jnp.arange(N_EXPERTS, dtype=jnp.int32)[None, None, :]
    dest = jnp.sum(jnp.where(hit, pad_start[None, None, :], 0), axis=-1) + rank
    blk0 = jnp.arange(MOE_NBLK, dtype=jnp.int32) * MOE_BLK
    blk_e = jnp.minimum(jnp.sum((pad_end[None, :] <= blk0[:, None]).astype(jnp.int32), axis=-1),
                        N_EXPERTS - 1)
    n_used = (pad_end[-1] // MOE_BLK).astype(jnp.int32).reshape(1)
    owns = counts > 0
    slot_of = (jnp.cumsum(owns.astype(jnp.int32)) - 1) % _W_SLOTS
    ids = jnp.arange(N_EXPERTS, dtype=jnp.int32)
    later = jnp.logical_and(owns[None, :], ids[None, :] > ids[:, None])
    next_of = jnp.min(jnp.where(later, ids[None, :], N_EXPERTS), axis=-1)
    next2_of = jnp.concatenate([next_of, jnp.full((1,), N_EXPERTS, jnp.int32)])[next_of]
    ahead = jnp.stack([next_of, next2_of], axis=0)
    ahead = jnp.where(ahead == N_EXPERTS, -1, ahead)
    return dest, blk_e, n_used, slot_of[blk_e], ahead[:, blk_e].reshape(-1)


_W_PARTS = 4
_W_SLOTS = 3


def _expert_kernel(blk_e_ref, n_used_ref, slot_ref, next_ref, x_ref, wg_hbm, wu_hbm, wd_hbm, o_ref,
                   wg_buf, wu_buf, wd_buf, wg_scr, wu_scr, wd_scr, sems, *, layer):
    j = pl.program_id(0)
    e = blk_e_ref[j]
    slot = slot_ref[j]
    fresh = jnp.logical_or(j == 0, e != blk_e_ref[jnp.maximum(j - 1, 0)])
    live = j < n_used_ref[0]

    def copies(expert, s):
        out = []
        for m, (hbm, buf) in enumerate(((wg_hbm, wg_buf), (wu_hbm, wu_buf), (wd_hbm, wd_buf))):
            rows = buf.shape[1] // _W_PARTS
            for part in range(_W_PARTS):
                band = pl.ds(part * rows, rows)
                out.append(pltpu.make_async_copy(hbm.at[layer, expert, band], buf.at[s, band],
                                                 sems.at[s, m, part]))
        return out

    def start_if_any(expert, s):
        @pl.when(expert >= 0)
        def _():
            for cp in copies(expert, s):
                cp.start()

    @pl.when(j == 0)
    def _():
        for cp in copies(e, slot):
            cp.start()
        start_if_any(next_ref[j], lax.rem(slot + 1, _W_SLOTS))

    @pl.when(jnp.logical_and(fresh, live))
    def _():
        for cp in copies(e, slot):
            cp.wait()
        start_if_any(next_ref[MOE_NBLK + j], lax.rem(slot + 2, _W_SLOTS))

        wg_scr[...] = wg_buf[slot].astype(BF16)
        wu_scr[...] = wu_buf[slot].astype(BF16)
        wd_scr[...] = wd_buf[slot].astype(BF16)

    @pl.when(live)
    def _():
        x = _unpack_rows(x_ref[...])
        gt = jnp.dot(x, wg_scr[...], preferred_element_type=F32)
        up = jnp.dot(x, wu_scr[...], preferred_element_type=F32)
        hb = (_silu(gt) * up).astype(BF16)
        o_ref[...] = jnp.dot(hb, wd_scr[...], preferred_element_type=F32).astype(o_ref.dtype)

    @pl.when(jnp.logical_not(live))
    def _():
        o_ref[...] = jnp.zeros(o_ref.shape, o_ref.dtype)


def _experts(x_pad, blk_e, n_used, slot, nxt, w_gate, w_up, w_down, layer):
    hbm = pl.BlockSpec(memory_space=pl.ANY)
    grid_spec = pltpu.PrefetchScalarGridSpec(
        num_scalar_prefetch=4, grid=(MOE_NBLK,),
        in_specs=[pl.BlockSpec((MOE_BLK, _PACK_W), lambda j, be, nu, *_: (jnp.minimum(j, nu[0] - 1), 0)),
                  hbm, hbm, hbm],
        out_specs=pl.BlockSpec((MOE_BLK, D_MODEL), lambda j, *_: (j, 0)),
        scratch_shapes=[pltpu.VMEM((_W_SLOTS, D_MODEL, D_EXPERT), F32),
                        pltpu.VMEM((_W_SLOTS, D_MODEL, D_EXPERT), F32),
                        pltpu.VMEM((_W_SLOTS, D_EXPERT, D_MODEL), F32),
                        pltpu.VMEM((D_MODEL, D_EXPERT), BF16), pltpu.VMEM((D_MODEL, D_EXPERT), BF16),
                        pltpu.VMEM((D_EXPERT, D_MODEL), BF16),
                        pltpu.SemaphoreType.DMA((_W_SLOTS, 3, _W_PARTS))])
    return pl.pallas_call(
        functools.partial(_expert_kernel, layer=layer), grid_spec=grid_spec,
        out_shape=jax.ShapeDtypeStruct((MOE_NBLK * MOE_BLK, D_MODEL), BF16),
        compiler_params=_params(("arbitrary",)), name="experts",
    )(blk_e, n_used, slot, nxt, x_pad, w_gate, w_up, w_down)


def _final_kernel(x_ref, y2_ref, rec_ref, gate_ref, fg_ref, o_ref):
    o_ref[...] = _rms(_with_pending(x_ref, (y2_ref, rec_ref, gate_ref)), fg_ref[...])


def _final_norm(x, pend, final_g, row0, n_rows):
    tm = 512
    tile0 = row0 // tm
    return pl.pallas_call(
        _final_kernel, grid=(n_rows // tm,),
        in_specs=[pl.BlockSpec((tm, D_MODEL), lambda i: (i + tile0, 0))] + _pending_specs(tm, tile0)
        + [pl.BlockSpec((1, D_MODEL), lambda i: (0, 0))],
        out_specs=pl.BlockSpec((tm, D_MODEL), lambda i: (i, 0)),
        out_shape=jax.ShapeDtypeStruct((n_rows, D_MODEL), F32),
        compiler_params=_params(("parallel",)), name="final_norm",
    )(x, *pend, final_g.reshape(1, D_MODEL))


_SC_WORKERS = 32
_SC_CORES = 2
_SC_ROWS = 64


def _dispatch_rows(hp, dest):
    n, width = hp.shape
    per_w = n // _SC_WORKERS
    n_ch = per_w // _SC_ROWS
    idx = dest.T.reshape(2, _SC_WORKERS, n_ch, _SC_ROWS)
    mesh = plsc.VectorSubcoreMesh(core_axis_name="c", subcore_axis_name="s")

    @functools.partial(
        pl.kernel, mesh=mesh, out_type=jax.ShapeDtypeStruct((MOE_NBLK * MOE_BLK, width), hp.dtype),
        scratch_types=[pltpu.VMEM((n_ch, _SC_ROWS), jnp.int32), pltpu.VMEM((n_ch, _SC_ROWS), jnp.int32),
                       pltpu.VMEM((_SC_ROWS, width), hp.dtype)], name="dispatch_rows")
    def scatter(h_hbm, idx_hbm, out_hbm, i0_v, i1_v, rows_v):
        wid = lax.axis_index("s") * _SC_CORES + lax.axis_index("c")
        pltpu.sync_copy(idx_hbm.at[0, wid], i0_v)
        pltpu.sync_copy(idx_hbm.at[1, wid], i1_v)

        @pl.loop(0, n_ch)
        def _(g):
            pltpu.sync_copy(h_hbm.at[pl.ds(wid * per_w + g * _SC_ROWS, _SC_ROWS)], rows_v)
            pltpu.sync_copy(rows_v, out_hbm.at[i0_v.at[g]])
            pltpu.sync_copy(rows_v, out_hbm.at[i1_v.at[g]])

    return scatter(hp, idx)


def _moe(h, rec, cnt, mods, w_gate, w_up, w_down, layer):
    dest, blk_e, n_used, slot, nxt = _plan(rec, cnt)
    y_pad = _experts(_dispatch_rows(h, dest), blk_e, n_used, slot, nxt, w_gate, w_up, w_down, layer)
    order = dest.reshape(N_TOK // _PEND_TM, _PEND_TM, 2).transpose(0, 2, 1).reshape(-1)
    return y_pad[order], rec, mods


def kernel(x_prompt, x_sample, c, cache_k, cache_v, state_delta, c_ctx, ada_w, ada_b, norm1_g, norm2_g, final_g,
           ev_w_in, ev_w_out, ev_conv_w, ev_a_log, ev_dt_bias, ev_onorm_g, ev_rpb, od_w_in, od_ln_g, od_ln_b,
           od_w_s, od_b_s, od_w_out, moe_w_rg, moe_b_rg, moe_w_re, moe_b_re, moe_w_gate, moe_w_up, moe_w_down):
    x = (x_prompt.reshape(N_CTX, D_MODEL), x_sample.reshape(N_LAT, D_MODEL))
    cond = jnp.concatenate([c_ctx[None, :], c, jnp.zeros((N_COND - 1 - DEC_BATCH, D_MODEL), F32)], axis=0)
    mods_all = _ada_mods(cond, ada_w, ada_b)
    kctx_all = cache_k.reshape(DEC_BATCH, -1, PAST_LEN, NA_HEADS * NA_HD)
    vctx_all = cache_v.reshape(DEC_BATCH, -1, PAST_LEN, NA_HEADS * NA_HD)

    ks, vs, ss = [], [], []
    pend = None
    for l in range(DEPTH):
        mods = mods_all[l]
        router = _router_weights(moe_w_rg[l], moe_b_rg[l], moe_w_re[l], moe_b_re[l])
        if l % 2 == 0:
            e = l // 2
            proj, ab, kv, x = _even_proj(x, pend, mods, norm1_g[l], ev_w_in[e])
            dn = (proj, ab, ev_conv_w[e], ev_a_log[e], ev_dt_bias[e], ev_onorm_g[e])
            oa_ctx, s_fin = _delta_heads(*dn, SEQ, BATCH, 0, None)
            oa_lat, _ = _delta_heads(*dn, DEC_SEQ, DEC_BATCH, N_CTX // DEC_SEQ, (state_delta, e))
            ob_ctx = _ctx_attention(proj)
            ob_lat = _na_attention(proj, kctx_all, vctx_all, e, ev_rpb[e])
            x, h, rec, cnt = _even_out(oa_ctx, ob_ctx, oa_lat, ob_lat, x, ev_w_out, e, mods, norm2_g[l],
                                       router)
            na_w = NA_HEADS * NA_HD
            ks.append(kv[:N_CTX, :na_w].reshape(BATCH, SEQ, NA_HEADS, NA_HD))
            vs.append(kv[:N_CTX, na_w:].reshape(BATCH, SEQ, NA_HEADS, NA_HD))
            ss.append(s_fin)
        else:
            o = l // 2
            x, h, rec, cnt = _sgu_layer(x, pend, mods, norm1_g[l], od_w_in[o], od_ln_g[o], od_ln_b[o],
                                        od_w_s[o], od_b_s[o], od_w_out[o], norm2_g[l], router)
        pend = _moe(h, rec, cnt, mods, moe_w_gate, moe_w_up, moe_w_down, l)
    y_prompt = _final_norm(x, pend, final_g, 0, N_CTX).reshape(BATCH, SEQ, D_MODEL)
    y_sample = _final_norm(x, pend, final_g, N_CTX, N_LAT).reshape(DEC_BATCH, DEC_SEQ, D_MODEL)
    return (y_prompt, y_sample, jnp.stack(ks, axis=1), jnp.stack(vs, axis=1), jnp.stack(ss, axis=1))
```

```python
import functools

import jax
import jax.numpy as jnp
from jax import lax
from jax.experimental import pallas as pl
from jax.experimental.pallas import tpu as pltpu
from jax.experimental.pallas import tpu_sc as plsc

F32 = jnp.float32
BF16 = jnp.bfloat16

D_MODEL = 1024
BATCH = 16
SEQ = 256
DEPTH = 4
DEC_BATCH = 4
DEC_SEQ = 2048
PAST_LEN = 512
GRID_W = 64
EPS = 1e-6
NEG_INF = -1e30

DN_HEADS = 4
DN_DK = 128
DN_CHUNK = 64
NA_HEADS = 8
NA_HD = 64
NA_ROWS = 8
NA_COLS = 16
SG_CHUNK = 128
SG_GROUPS = 8
SG_W = 2 * D_MODEL
SG_GW = SG_W // SG_GROUPS
N_EGROUPS = 4
EXP_PER_GROUP = 8
N_EXPERTS = 32
D_EXPERT = 512

N_CTX = BATCH * SEQ
N_LAT = DEC_BATCH * DEC_SEQ
N_TOK = N_CTX + N_LAT
N_COND = 8
PROJ_W = 4096
LANES = 128
MOE_BLK = 256
MOE_NBLK = -(-(2 * N_TOK + N_EXPERTS * (MOE_BLK - 1)) // MOE_BLK)
VMEM_LIMIT = 56 * 1024 * 1024

_QA, _KA, _VA, _ZA, _QB, _KB, _VB = 0, 4, 8, 12, 16, 20, 24


def _params(sem):
    return pltpu.CompilerParams(dimension_semantics=sem, vmem_limit_bytes=VMEM_LIMIT)


def _bdot(a, b):
    return jnp.dot(a.astype(BF16), b.astype(BF16), preferred_element_type=F32)


def _bdot_nt(a, b):
    return lax.dot_general(a.astype(BF16), b.astype(BF16), (((1,), (1,)), ((), ())),
                           preferred_element_type=F32)


def _bdot_tn(a, b):
    return lax.dot_general(a.astype(BF16), b.astype(BF16), (((0,), (0,)), ((), ())),
                           preferred_element_type=F32)


def _split2(a):
    p0 = a.astype(BF16)
    return p0, (a - p0.astype(F32)).astype(BF16)


def _dot3(a, b):
    ah = a.astype(BF16)
    al = (a - ah.astype(F32)).astype(BF16)
    bh = b.astype(BF16)
    bl = (b - bh.astype(F32)).astype(BF16)
    return (jnp.dot(ah, bh, preferred_element_type=F32) + jnp.dot(ah, bl, preferred_element_type=F32)
            + jnp.dot(al, bh, preferred_element_type=F32))


def _mask_bf16(m01):
    return jnp.where(m01, 1.0, 0.0).astype(BF16)


def _xdot(m01, a):
    m = _mask_bf16(m01)
    p0, p1 = _split2(a)
    return jnp.dot(m, p0, preferred_element_type=F32) + jnp.dot(m, p1, preferred_element_type=F32)


def _xdot_r(a, m01):
    m = _mask_bf16(m01)
    p0, p1 = _split2(a)
    return jnp.dot(p0, m, preferred_element_type=F32) + jnp.dot(p1, m, preferred_element_type=F32)


def _sigmoid(x):
    return 0.5 * jnp.tanh(0.5 * x) + 0.5


def _silu(x):
    hx = 0.5 * x
    return hx + hx * jnp.tanh(hx)


def _rms(x, g):
    return x * lax.rsqrt(jnp.mean(x * x, axis=-1, keepdims=True) + EPS) * g


def _cond_index(row):
    return jnp.where(row < N_CTX, 0, 1 + (row - N_CTX) // DEC_SEQ)


def _mod_spec(k, tm, tile0=0):
    return pl.BlockSpec((None, None, 1, D_MODEL), lambda i, *_: (_cond_index((i + tile0) * tm), k, 0, 0))


def _ada_kernel(c_ref, w_ref, b_ref, o_ref):
    o_ref[...] = _bdot(_silu(c_ref[...]), w_ref[...]) + b_ref[...]


def _ada_mods(cond, ada_w, ada_b):
    tn = 1536
    out = pl.pallas_call(
        _ada_kernel, grid=(DEPTH, 6 * D_MODEL // tn),
        in_specs=[pl.BlockSpec((N_COND, D_MODEL), lambda l, j: (0, 0)),
                  pl.BlockSpec((None, D_MODEL, tn), lambda l, j: (l, 0, j)),
                  pl.BlockSpec((None, 1, tn), lambda l, j: (l, 0, j))],
        out_specs=pl.BlockSpec((None, N_COND, tn), lambda l, j: (l, 0, j)),
        out_shape=jax.ShapeDtypeStruct((DEPTH, N_COND, 6 * D_MODEL), F32),
        compiler_params=_params(("parallel", "parallel")), name="ada_mods",
    )(cond, ada_w, ada_b.reshape(DEPTH, 1, 6 * D_MODEL))
    return out.reshape(DEPTH, N_COND, 6, 1, D_MODEL)


_EV_TN = 512
_EV_W = 7 * DN_HEADS * LANES
_KV_COL0 = _KB * LANES


def _with_pending(x_ref, pend):
    if not pend:
        return x_ref[...]
    y2_ref, rec_ref, gate_ref = pend
    rec = rec_ref[...]
    tm = rec.shape[0]
    y = (rec[:, _R_W:_R_W + 1] * y2_ref[:tm, :].astype(F32)
         + rec[:, _R_W + 1:_R_W + 2] * y2_ref[tm:, :].astype(F32))
    return x_ref[...] + gate_ref[...] * y


_PEND_TM = 512


def _pending_specs(tm, tile0=0):
    assert tm == _PEND_TM
    return [pl.BlockSpec((2 * tm, D_MODEL), lambda i: (i + tile0, 0)),
            pl.BlockSpec((tm, LANES), lambda i: (i + tile0, 0)), _mod_spec(5, tm, tile0)]


def _even_proj_kernel(x_ref, *rest, n_pend, ctx_tiles):
    pend, (g_ref, sh_ref, sc_ref, w_ref, wab_ref, o_ref, ab_ref, kv_ref) = rest[:n_pend], rest[n_pend:n_pend + 8]
    if n_pend == 1:
        x = jnp.where(pl.program_id(0) < ctx_tiles, x_ref[...], pend[0][...])
    else:
        x = _with_pending(x_ref, pend)
    if pend:
        rest[n_pend + 8][...] = x
    h = (_rms(x, g_ref[...]) * (1.0 + sc_ref[...]) + sh_ref[...]).astype(BF16)
    ab_ref[...] = jnp.dot(h, wab_ref[...], preferred_element_type=F32)
    for j in range(_EV_W // _EV_TN):
        c0 = j * _EV_TN
        y = jnp.dot(h, w_ref[:, c0:c0 + _EV_TN], preferred_element_type=F32)
        for c in range(_EV_TN // LANES):
            o_ref[c0 // LANES + c] = y[:, c * LANES:(c + 1) * LANES].astype(BF16)
        if c0 >= _KV_COL0:
            kv_ref[:, c0 - _KV_COL0:c0 - _KV_COL0 + _EV_TN] = y


def _even_proj(x, pend, mods, g, w_in):
    tm = 512
    ctx_tiles = N_CTX // tm
    rows = pl.BlockSpec((tm, D_MODEL), lambda i: (i, 0))
    if isinstance(x, tuple):
        x, pend = x[0], (x[1],)
        x_specs = [pl.BlockSpec((tm, D_MODEL), lambda i: (jnp.minimum(i, ctx_tiles - 1), 0)),
                   pl.BlockSpec((tm, D_MODEL), lambda i: (jnp.maximum(i - ctx_tiles, 0), 0))]
    else:
        pend = tuple(pend) if pend else ()
        x_specs = [rows] + (_pending_specs(tm) if pend else [])
    n_ab = 4 * DN_HEADS
    ab0 = 4 * DN_HEADS * DN_DK
    w_main = jnp.concatenate([w_in[:, :ab0], w_in[:, ab0 + n_ab:]], axis=1).astype(BF16)
    w_ab = jnp.concatenate([w_in[:, ab0:ab0 + n_ab], jnp.zeros((D_MODEL, LANES - n_ab), F32)],
                           axis=1).astype(BF16)
    held = lambda shape: pl.BlockSpec(shape, lambda i: (0,) * len(shape), pipeline_mode=pl.Buffered(1))
    out_specs = [pl.BlockSpec((_EV_W // LANES, tm, LANES), lambda i: (0, i, 0)),
                 pl.BlockSpec((tm, LANES), lambda i: (i, 0)),
                 pl.BlockSpec((tm, 2 * NA_HEADS * NA_HD), lambda i: (i, 0))]
    out_shape = [jax.ShapeDtypeStruct((_EV_W // LANES, N_TOK, LANES), BF16),
                 jax.ShapeDtypeStruct((N_TOK, LANES), F32),
                 jax.ShapeDtypeStruct((N_TOK, 2 * NA_HEADS * NA_HD), F32)]
    if pend:
        out_specs.append(rows)
        out_shape.append(jax.ShapeDtypeStruct((N_TOK, D_MODEL), F32))
    res = pl.pallas_call(
        functools.partial(_even_proj_kernel, n_pend=len(pend), ctx_tiles=ctx_tiles), grid=(N_TOK // tm,),
        in_specs=x_specs
        + [pl.BlockSpec((1, D_MODEL), lambda i: (0, 0)), _mod_spec(0, tm), _mod_spec(1, tm),
           held((D_MODEL, _EV_W)), held((D_MODEL, LANES))],
        out_specs=out_specs, out_shape=out_shape,
        compiler_params=_params(("parallel",)), name="even_proj",
    )(x, *pend, g.reshape(1, D_MODEL), mods, mods, w_main, w_ab)
    return (*res[:3], res[3] if pend else x)


_CHUNK_SHIFT = DN_CHUNK.bit_length() - 1
_CUM_ROWS = 256
_DN_CHAINS = 16
_DN_SHORT = 256
_MQ_ROWS = DN_DK + DN_CHUNK


def _dn_kernel(*refs, T, HB, has_s0, want_state):
    it = iter(refs)
    q_ref, k_ref, v_ref, z_ref, ab_ref = (next(it) for _ in range(5))
    cwq_ref, cwk_ref, cwv_ref, alog_ref, dtb_ref, og_ref = (next(it) for _ in range(6))
    s0_ref = next(it) if has_s0 else None
    o_ref = next(it)
    sfin_ref = next(it) if want_state else None
    qc, kc, vc, gsc, bsc, osc, b_s, mq_s = (next(it) for _ in range(8))

    C = DN_CHUNK
    n = T // C
    h0 = pl.program_id(1) * HB

    row = lax.broadcasted_iota(jnp.int32, (T, 1), 0)

    def conv(x_ref, cw_ref, hh):
        x = x_ref[hh].astype(F32)
        cw = cw_ref[:, hh * LANES:(hh + 1) * LANES]
        xp = jnp.where(row == 0, 0.0, pltpu.roll(x, 1, 0))
        xn = jnp.where(row == T - 1, 0.0, pltpu.roll(x, T - 1, 0))
        return _silu(cw[0:1, :] * xp + cw[1:2, :] * x + cw[2:3, :] * xn)

    def l2n(x):
        return x * lax.rsqrt(jnp.sum(x * x, axis=-1, keepdims=True) + EPS)

    ab = ab_ref[...]
    lane = lax.broadcasted_iota(jnp.int32, (1, LANES), 1)
    dtb = jnp.zeros((1, LANES), F32)
    alog = jnp.zeros((1, LANES), F32)
    for d in range(2):
        for hq in range(DN_HEADS):
            dtb = jnp.where(lane == d * DN_HEADS + hq, dtb_ref[d, hq], dtb)
            alog = jnp.where(lane == d * DN_HEADS + hq, alog_ref[d, hq], alog)
    xs = ab + dtb
    g_all = -jnp.exp(alog) * (jnp.maximum(xs, 0.0) + jnp.log(1.0 + jnp.exp(-jnp.abs(xs))))
    beta_all = _sigmoid(ab)

    sel_r = lax.broadcasted_iota(jnp.int32, (LANES, LANES), 0)
    for hh in range(HB):
        qc[hh] = l2n(conv(q_ref, cwq_ref, hh)) * (DN_DK ** -0.5)
        kc[hh] = l2n(conv(k_ref, cwk_ref, hh))
        vc[hh] = conv(v_ref, cwv_ref, hh)
        hd = h0 + hh
        for d in range(2):
            gsc[hh, d] = _xdot_r(g_all, sel_r == d * DN_HEADS + hd)
            bsc[hh, d] = _xdot_r(beta_all, sel_r == 2 * DN_HEADS + d * DN_HEADS + hd)

    pr = lax.broadcasted_iota(jnp.int32, (_CUM_ROWS, _CUM_ROWS), 0)
    pc = lax.broadcasted_iota(jnp.int32, (_CUM_ROWS, _CUM_ROWS), 1)
    same = lax.shift_right_logical(pr, _CHUNK_SHIFT) == lax.shift_right_logical(pc, _CHUNK_SHIFT)
    cum_mask = (jnp.logical_and(same, pc <= pr), jnp.logical_and(same, pc >= pr))

    def cum_body(i, carry):
        sl = pl.ds(pl.multiple_of(i * _CUM_ROWS, _CUM_ROWS), _CUM_ROWS)
        for hh in range(HB):
            for d in range(2):
                gsc[hh, d, sl, :] = _xdot(cum_mask[d], gsc[hh, d, sl, :])
        return carry

    lax.fori_loop(0, T // _CUM_ROWS, cum_body, 0, unroll=True)

    ri = lax.broadcasted_iota(jnp.int32, (C, C), 0)
    ci = lax.broadcasted_iota(jnp.int32, (C, C), 1)
    eye = (ri == ci).astype(F32)

    def prepare(items):
        lows, decays = [], []
        kk, qk = {}, {}
        for hh, d, c, slot in items:
            if (hh, slot) not in kk:
                sl = pl.ds(pl.multiple_of(c * C, C), C)
                k = kc[hh, sl, :]
                kk[hh, slot] = _bdot_nt(k, k)
                qk[hh, slot] = _bdot_nt(qc[hh, sl, :], k)
        for hh, d, c, slot in items:
            sl = pl.ds(pl.multiple_of(c * C, C), C)
            gc = gsc[hh, d, sl, :]
            incl = (ci <= ri) if d == 0 else (ci >= ri)
            strict = (ci < ri) if d == 0 else (ci > ri)
            gr = jnp.transpose(gc)[0:1, :C]
            decay = jnp.where(incl, jnp.exp(jnp.where(incl, gc[:, :C] - gr, 0.0)), 0.0)
            lows.append(jnp.where(strict, bsc[hh, d, sl, :C] * kk[hh, slot] * decay, 0.0))
            decays.append(decay)
        ts = [eye - low for low in lows]
        ps = lows
        for step in range(_CHUNK_SHIFT - 1):
            ps = [_dot3(p, p) for p in ps]
            ts = [t + _dot3(t, p) for t, p in zip(ts, ps)]
        wus, kds, attns, qds = [], [], [], []
        for (hh, d, c, slot), t, decay in zip(items, ts, decays):
            sl = pl.ds(pl.multiple_of(c * C, C), C)
            k, gc, beta = kc[hh, sl, :], gsc[hh, d, sl, :], bsc[hh, d, sl, :]
            eg = jnp.exp(gc)
            uw = _bdot(t, jnp.concatenate([vc[hh, sl, :] * beta, k * beta * eg], axis=-1))
            last = gc[C - 1:C, :] if d == 0 else gc[0:1, :]
            wus.append(jnp.concatenate([uw[:, LANES:], uw[:, :LANES]], axis=-1).astype(BF16))
            kds.append((k * jnp.exp(last - gc)).astype(BF16))
            attns.append((qk[hh, slot] * decay).astype(BF16))
            qds.append(qc[hh, sl, :] * eg)
        kdwus = [lax.dot_general(kd, wu, (((0,), (0,)), ((), ())), preferred_element_type=F32)
                 for kd, wu in zip(kds, wus)]
        awus = [jnp.dot(attn, wu, preferred_element_type=F32) for attn, wu in zip(attns, wus)]
        for (hh, d, c, slot), kdwu, awu, qd in zip(items, kdwus, awus, qds):
            mq0 = pl.multiple_of(c * _MQ_ROWS, _MQ_ROWS)
            mq_s[hh, d, pl.ds(mq0, DN_DK), :] = kdwu[:, :LANES].astype(BF16)
            mq_s[hh, d, pl.ds(mq0 + DN_DK, C), :] = (qd - awu[:, :LANES]).astype(BF16)
            b_s[hh, d, pl.ds(pl.multiple_of(c * DN_DK, DN_DK), DN_DK), :] = kdwu[:, LANES:]
            osc[hh, d, pl.ds(pl.multiple_of(c * C, C), C), :] = awu[:, LANES:]

    n_prep = min(n, _DN_CHAINS // 2)
    h_prep = max(1, min(HB, _DN_CHAINS // (2 * n_prep)))

    def prep_body(i, carry):
        for hg in range(0, HB, h_prep):
            prepare([(hh, d, i * n_prep + j, j) for hh in range(hg, hg + h_prep) for j in range(n_prep)
                     for d in range(2)])
        return carry

    lax.fori_loop(0, n // n_prep, prep_body, 0)

    def body(i, carry):
        chains = [(hh, d, i if d == 0 else n - 1 - i) for hh in range(HB) for d in range(2)]
        mss = [jnp.dot(mq_s[hh, d, pl.ds(pl.multiple_of(c * _MQ_ROWS, _MQ_ROWS), _MQ_ROWS), :],
                       S.astype(BF16), preferred_element_type=F32)
               for (hh, d, c), S in zip(chains, carry)]
        new = []
        for (hh, d, c), S, ms in zip(chains, carry, mss):
            sl = pl.ds(pl.multiple_of(c * C, C), C)
            osc[hh, d, sl, :] = osc[hh, d, sl, :] + ms[DN_DK:]
            last = gsc[hh, d, pl.ds(c * C + (C - 1 if d == 0 else 0), 1), :]
            new.append(S * jnp.exp(last) - ms[:DN_DK]
                       + b_s[hh, d, pl.ds(pl.multiple_of(c * DN_DK, DN_DK), DN_DK), :])
        return tuple(new)

    if has_s0:
        init = tuple(s0_ref[d, hh] for hh in range(HB) for d in range(2))
    else:
        init = tuple(jnp.zeros((DN_DK, LANES), F32) for _ in range(2 * HB))
    fin = lax.fori_loop(0, n, body, init, unroll=4)
    for hh in range(HB):
        if want_state:
            sfin_ref[0, hh] = fin[2 * hh]
            sfin_ref[1, hh] = fin[2 * hh + 1]
        o = osc[hh, 0] + osc[hh, 1]
        o_ref[hh] = (_rms(o, og_ref[...]) * _silu(z_ref[hh].astype(F32))).astype(o_ref.dtype)


def _delta_heads(proj, ab, conv_w, a_log, dt_bias, onorm_g, T, n_seq, row0, s0):
    has_s0 = s0 is not None
    want_state = not has_s0
    hb = DN_HEADS if T <= _DN_SHORT else 2

    def col(cb):
        return pl.BlockSpec((hb, T, LANES), lambda s, h: (cb // hb + h, row0 + s, 0))

    def cw(cb):
        return pl.BlockSpec((3, hb * LANES), lambda s, h: (0, cb // hb + h))

    smem = pl.BlockSpec(memory_space=pltpu.SMEM)
    in_specs = [col(_QA), col(_KA), col(_VA), col(_ZA),
                pl.BlockSpec((T, LANES), lambda s, h: (row0 + s, 0)),
                cw(0), cw(4), cw(8), smem, smem,
                pl.BlockSpec((1, LANES), lambda s, h: (0, 0))]
    args = [proj, proj, proj, proj, ab, conv_w, conv_w, conv_w, a_log, dt_bias,
            onorm_g.reshape(1, LANES)]
    state_spec = pl.BlockSpec((None, 2, hb, DN_DK, LANES), lambda s, h: (s, 0, h, 0, 0))
    if has_s0:
        states, layer = s0
        in_specs.append(pl.BlockSpec((None, None, 2, hb, DN_DK, LANES), lambda s, h: (s, layer, 0, h, 0, 0)))
        args.append(states)
    out_shape = [jax.ShapeDtypeStruct((DN_HEADS, n_seq * T, LANES), BF16)]
    out_specs = [pl.BlockSpec((hb, T, LANES), lambda s, h: (h, s, 0))]
    if want_state:
        out_shape.append(jax.ShapeDtypeStruct((n_seq, 2, DN_HEADS, DN_DK, LANES), F32))
        out_specs.append(state_spec)
    res = pl.pallas_call(
        functools.partial(_dn_kernel, T=T, HB=hb, has_s0=has_s0, want_state=want_state),
        grid=(n_seq, DN_HEADS // hb), in_specs=in_specs, out_specs=out_specs, out_shape=out_shape,
        scratch_shapes=[pltpu.VMEM((hb, T, LANES), F32)] * 3
        + [pltpu.VMEM((hb, 2, T, LANES), F32)] * 3
        + [pltpu.VMEM((hb, 2, T // DN_CHUNK * DN_DK, LANES), F32),
           pltpu.VMEM((hb, 2, T // DN_CHUNK * _MQ_ROWS, LANES), BF16)],
        compiler_params=_params(("parallel", "parallel")), name="delta_heads_%d" % T,
    )(*args)
    return res if want_state else (res[0], None)


def _pair_queries(q, first):
    return jnp.concatenate([jnp.where(first, q, 0.0), jnp.where(first, 0.0, q)], axis=0).astype(BF16)


def _ctx_attn_kernel(q_ref, k_ref, v_ref, o_ref):
    first = lax.broadcasted_iota(jnp.int32, (SEQ, LANES), 1) < NA_HD
    pairs = range(NA_HEADS // 2)
    ss = [lax.dot_general(k_ref[p], _pair_queries(q_ref[p] * (NA_HD ** -0.5), first),
                          (((1,), (1,)), ((), ())), preferred_element_type=F32) for p in pairs]
    es = [jnp.exp(s - jnp.max(s, axis=0, keepdims=True)) for s in ss]
    dens = [jnp.sum(e, axis=0, keepdims=True) for e in es]
    os = [lax.dot_general(e.astype(BF16), v_ref[p], (((0,), (0,)), ((), ())), preferred_element_type=F32)
          for p, e in zip(pairs, es)]
    for p, o, den in zip(pairs, os, dens):
        o = jnp.where(first, o[:SEQ], o[SEQ:])
        den_t = jnp.transpose(jnp.broadcast_to(den, (LANES, 2 * SEQ)))
        o_ref[p] = (o / jnp.where(first, den_t[:SEQ], den_t[SEQ:])).astype(o_ref.dtype)


def _ctx_attention(proj):
    n_pairs = NA_HEADS // 2

    def col(cb):
        return pl.BlockSpec((n_pairs, SEQ, LANES), lambda s: (cb // n_pairs, s, 0))

    return pl.pallas_call(
        _ctx_attn_kernel, grid=(BATCH,),
        in_specs=[col(_QB), col(_KB), col(_VB)],
        out_specs=pl.BlockSpec((n_pairs, SEQ, LANES), lambda s: (0, s, 0)),
        out_shape=jax.ShapeDtypeStruct((n_pairs, N_CTX, LANES), BF16),
        compiler_params=_params(("parallel",)), name="ctx_attention",
    )(proj, proj, proj)


_NA_UNROLL = 4


def _na_kernel(q_ref, k_ref, v_ref, kc_ref, vc_ref, bias_ref, o_ref, kcb_scr, vcb_scr):
    rows = DEC_SEQ // GRID_W
    win = NA_ROWS * GRID_W
    scale = NA_HD ** -0.5
    dn_nt = (((1,), (1,)), ((), ()))
    dn_tn = (((0,), (0,)), ((), ()))

    kcb_scr[...] = kc_ref[...].astype(BF16)
    vcb_scr[...] = vc_ref[...].astype(BF16)
    first = lax.broadcasted_iota(jnp.int32, (GRID_W, LANES), 1) < NA_HD

    def scores(group):
        out = []
        for r in range(group * _NA_UNROLL, (group + 1) * _NA_UNROLL):
            rs = min(max(r - NA_ROWS // 2, 0), rows - NA_ROWS)
            qsl, wsl = pl.ds(r * GRID_W, GRID_W), pl.ds(rs * GRID_W, win)
            qm = _pair_queries(q_ref[qsl, :] * scale, first)
            bias = jnp.concatenate([bias_ref[NA_ROWS - 1 - (r - rs) + i] for i in range(NA_ROWS)], axis=0)
            s_win = lax.dot_general(k_ref[wsl, :], qm, dn_nt, preferred_element_type=F32) + bias
            s_ctx = lax.dot_general(kcb_scr[...], qm, dn_nt, preferred_element_type=F32)
            out.append((qsl, wsl, s_win, s_ctx))
        return out

    n_groups = rows // _NA_UNROLL
    nxt = scores(0)
    for group in range(n_groups):
        cur, nxt = nxt, (scores(group + 1) if group + 1 < n_groups else None)
        ms = [jnp.maximum(jnp.max(sw, axis=0, keepdims=True), jnp.max(sc, axis=0, keepdims=True))
              for _, _, sw, sc in cur]
        e_wins = [jnp.exp(sw - m) for (_, _, sw, _), m in zip(cur, ms)]
        e_ctxs = [jnp.exp(sc - m) for (_, _, _, sc), m in zip(cur, ms)]
        dens = [jnp.sum(ew, axis=0, keepdims=True) + jnp.sum(ec, axis=0, keepdims=True)
                for ew, ec in zip(e_wins, e_ctxs)]
        for (qsl, wsl, _, _), ew, ec, den in zip(cur, e_wins, e_ctxs, dens):
            o = (lax.dot_general(ew.astype(BF16), v_ref[wsl, :], dn_tn, preferred_element_type=F32)
                 + lax.dot_general(ec.astype(BF16), vcb_scr[...], dn_tn, preferred_element_type=F32))
            o = o / jnp.transpose(jnp.broadcast_to(den, (LANES, LANES)))
            o_ref[qsl, :] = jnp.where(first, o[:GRID_W], o[GRID_W:]).astype(o_ref.dtype)


def _na_bias_table(rpb):
    col = jnp.arange(GRID_W)
    cs = jnp.clip(col - NA_COLS // 2, 0, GRID_W - NA_COLS)
    col_ok = (col[None, :] >= cs[:, None]) & (col[None, :] < cs[:, None] + NA_COLS)
    dc = jnp.clip(col[None, :] - col[:, None] + NA_COLS - 1, 0, 2 * NA_COLS - 2)
    onehot = (dc.T[None, :, :] == jnp.arange(2 * NA_COLS - 1)[:, None, None]).astype(F32)
    t = jnp.einsum('hrd,dkq->hrkq', rpb.astype(F32), onehot, precision=lax.Precision.HIGHEST)
    t = jnp.where(col_ok.T[None, None], t, NEG_INF)
    t = t.reshape(NA_HEADS // 2, 2, 2 * NA_ROWS - 1, GRID_W, GRID_W)
    return jnp.concatenate([t[:, 0], t[:, 1]], axis=-1)


def _na_attention(proj, kctx, vctx, layer, rpb):
    blk = N_CTX // DEC_SEQ

    def col(cb):
        return pl.BlockSpec((None, DEC_SEQ, LANES), lambda b, p: (cb + p, blk + b, 0))

    ctx = pl.BlockSpec((None, None, PAST_LEN, LANES), lambda b, p: (b, layer, 0, p))
    return pl.pallas_call(
        _na_kernel, grid=(DEC_BATCH, NA_HEADS // 2),
        in_specs=[col(_QB), col(_KB), col(_VB), ctx, ctx,
                  pl.BlockSpec((None, 2 * NA_ROWS - 1, GRID_W, 2 * GRID_W), lambda b, p: (p, 0, 0, 0))],
        out_specs=pl.BlockSpec((None, DEC_SEQ, LANES), lambda b, p: (p, b, 0)),
        out_shape=jax.ShapeDtypeStruct((NA_HEADS // 2, N_LAT, LANES), BF16),
        scratch_shapes=[pltpu.VMEM((PAST_LEN, LANES), BF16), pltpu.VMEM((PAST_LEN, LANES), BF16)],
        compiler_params=_params(("parallel", "parallel")), name="na_attention",
    )(proj, proj, proj, kctx, vctx, _na_bias_table(rpb))


_LOGIT0 = N_EGROUPS
_R_E, _R_W, _R_RANK = 0, 2, 4


def _lane_min_where(mask, lane):
    return jnp.min(jnp.where(mask, lane, LANES), axis=-1, keepdims=True)


def _route_rows(lg, carry_ref, tri_ref):
    big = -3.0e38
    lane = lax.broadcasted_iota(jnp.int32, lg.shape, 1)
    is_g = lane < N_EGROUPS
    gmax = jnp.max(jnp.where(is_g, lg, big), axis=-1, keepdims=True)
    gsum = jnp.sum(jnp.where(is_g, jnp.exp(jnp.where(is_g, lg - gmax, 0.0)), 0.0), axis=-1, keepdims=True)
    pg_top = 1.0 / gsum
    g_idx = _lane_min_where(jnp.logical_and(is_g, lg == gmax), lane)
    in_g = jnp.logical_and(lane >= _LOGIT0, lax.shift_right_arithmetic(lane - _LOGIT0, 3) == g_idx)
    in_g = jnp.logical_and(in_g, lane < _LOGIT0 + N_EXPERTS)
    m1 = jnp.max(jnp.where(in_g, lg, big), axis=-1, keepdims=True)
    i1 = _lane_min_where(jnp.logical_and(in_g, lg == m1), lane)
    rest = jnp.logical_and(in_g, lane != i1)
    m2 = jnp.max(jnp.where(rest, lg, big), axis=-1, keepdims=True)
    i2 = _lane_min_where(jnp.logical_and(rest, lg == m2), lane)
    e2 = jnp.exp(m2 - m1)
    w1 = pg_top * (1.0 / (1.0 + e2))
    w2 = pg_top * (e2 / (1.0 + e2))
    hit1 = lane == i1
    hit2 = lane == i2
    picked = jnp.where(jnp.logical_or(hit1, hit2), 1.0, 0.0)
    before = jnp.dot(tri_ref[...], picked.astype(BF16), preferred_element_type=F32) + carry_ref[...]
    r1 = jnp.sum(jnp.where(hit1, before, 0.0), axis=-1, keepdims=True)
    r2 = jnp.sum(jnp.where(hit2, before, 0.0), axis=-1, keepdims=True)
    carry_ref[...] = carry_ref[...] + jnp.sum(picked, axis=0, keepdims=True)
    rec = jnp.zeros(lg.shape, F32)
    for ln, val in ((_R_E, (i1 - _LOGIT0).astype(F32)), (_R_E + 1, (i2 - _LOGIT0).astype(F32)),
                    (_R_W, w1), (_R_W + 1, w2), (_R_RANK, r1), (_R_RANK + 1, r2)):
        rec = jnp.where(lane == ln, val, rec)
    return rec


_PACK_W = D_MODEL // 2


def _pack_rows(hb):
    lo = lax.bitcast_convert_type(hb[:, :_PACK_W].astype(F32), jnp.int32)
    hi = lax.bitcast_convert_type(hb[:, _PACK_W:].astype(F32), jnp.int32)
    return jnp.bitwise_or(jnp.bitwise_and(hi, -65536), lax.shift_right_logical(lo, 16))


def _unpack_rows(w):
    lo = lax.bitcast_convert_type(lax.shift_left(w, 16), F32)
    hi = lax.bitcast_convert_type(jnp.bitwise_and(w, -65536), F32)
    return jnp.concatenate([lo, hi], axis=-1).astype(BF16)


def _moe_input(xnew, first, tail_in, tail_out, tail_scr):
    g2_ref, sc2_ref, sh2_ref, wr_ref, br_ref = tail_in
    x_out, h_out, rec_out, cnt_out = tail_out
    tri_scr, carry_scr = tail_scr

    @pl.when(first)
    def _():
        tm = tri_scr.shape[0]
        r = lax.broadcasted_iota(jnp.int32, (tm, tm), 0)
        c = lax.broadcasted_iota(jnp.int32, (tm, tm), 1)
        tri_scr[...] = jnp.where(c < r, 1.0, 0.0).astype(BF16)
        carry_scr[...] = jnp.zeros(carry_scr.shape, F32)

    x_out[...] = xnew
    h = _rms(xnew, g2_ref[...]) * (1.0 + sc2_ref[...]) + sh2_ref[...]
    hh = h.astype(BF16)
    h_out[...] = _pack_rows(hh)
    lg = jnp.dot(hh, wr_ref[...], preferred_element_type=F32) + br_ref[...]
    rec_out[...] = _route_rows(lg, carry_scr, tri_scr)
    cnt_out[...] = carry_scr[...]


def _even_out_kernel(oac_ref, obc_ref, oal_ref, obl_ref, x_ref, w_ref, gate_ref, *rest, ctx_tiles):
    tail_in, tail_out, (w_scr,), tail_scr = rest[:5], rest[5:9], rest[9:10], rest[10:]
    first = pl.program_id(0) == 0

    @pl.when(first)
    def _():
        w_scr[...] = w_ref[...].astype(BF16)

    is_ctx = pl.program_id(0) < ctx_tiles
    parts = [jnp.where(is_ctx, c_ref[hb], l_ref[hb])
             for c_ref, l_ref in ((oac_ref, oal_ref), (obc_ref, obl_ref)) for hb in range(DN_HEADS)]
    mix = jnp.concatenate(parts, axis=-1)
    out = jnp.dot(mix, w_scr[...], preferred_element_type=F32)
    _moe_input(x_ref[...] + gate_ref[...] * out, first, tail_in, tail_out, tail_scr)


def _tail_specs(tm):
    const = lambda shape: pl.BlockSpec(shape, lambda i: (0,) * len(shape))
    in_specs = [_mod_spec(2, tm), const((1, D_MODEL)), _mod_spec(4, tm), _mod_spec(3, tm),
                const((D_MODEL, LANES)), const((1, LANES))]
    out_specs = [pl.BlockSpec((tm, D_MODEL), lambda i: (i, 0)),
                 pl.BlockSpec((tm, _PACK_W), lambda i: (i, 0)),
                 pl.BlockSpec((tm, LANES), lambda i: (i, 0)),
                 const((1, LANES))]
    out_shape = [jax.ShapeDtypeStruct((N_TOK, D_MODEL), F32),
                 jax.ShapeDtypeStruct((N_TOK, _PACK_W), jnp.int32),
                 jax.ShapeDtypeStruct((N_TOK, LANES), F32),
                 jax.ShapeDtypeStruct((1, LANES), F32)]
    scratch = [pltpu.VMEM((tm, tm), BF16), pltpu.VMEM((1, LANES), F32)]
    return in_specs, out_specs, out_shape, scratch


def _router_weights(w_rg, b_rg, w_re, b_re):
    pad = LANES - N_EGROUPS - N_EXPERTS
    w = jnp.concatenate([w_rg, w_re, jnp.zeros((D_MODEL, pad), F32)], axis=1)
    b = jnp.concatenate([b_rg, b_re, jnp.zeros((pad,), F32)]).reshape(1, LANES)
    return w.astype(BF16), b


def _even_out(oa_ctx, ob_ctx, oa_lat, ob_lat, x, w_out, layer, mods, g2, router):
    tm = 512
    ctx_tiles = N_CTX // tm
    tail_in, out_specs, out_shape, tail_scr = _tail_specs(tm)
    ctxblk = pl.BlockSpec((DN_HEADS, tm, LANES), lambda i: (0, jnp.minimum(i, ctx_tiles - 1), 0))
    latblk = pl.BlockSpec((DN_HEADS, tm, LANES), lambda i: (0, jnp.maximum(i - ctx_tiles, 0), 0))
    return pl.pallas_call(
        functools.partial(_even_out_kernel, ctx_tiles=ctx_tiles), grid=(N_TOK // tm,),
        in_specs=[ctxblk, ctxblk, latblk, latblk, pl.BlockSpec((tm, D_MODEL), lambda i: (i, 0)),
                  pl.BlockSpec((None, D_MODEL, D_MODEL), lambda i: (layer, 0, 0))] + tail_in,
        out_specs=out_specs, out_shape=out_shape,
        scratch_shapes=[pltpu.VMEM((D_MODEL, D_MODEL), BF16)] + tail_scr,
        compiler_params=_params(("arbitrary",)), name="even_out",
    )(oa_ctx, ob_ctx, oa_lat, ob_lat, x, w_out, mods, g2.reshape(1, D_MODEL), mods, mods, *router)


def _gelu_tanh(x):
    c = 0.7978845608028654
    hx = 0.5 * x
    return hx + hx * jnp.tanh(x * (c + (c * 0.044715) * (x * x)))


def _sgu_kernel(x_ref, *rest, tm, n_pend):
    pend, rest = rest[:n_pend], rest[n_pend:]
    g1_ref, sh1_ref, sc1_ref, win_ref, lng_ref, lnb_ref, ws_ref, bst_ref, wout_ref, gate_ref = rest[:10]
    rest = rest[10:]
    tail_in, tail_out, (v_scr, m_scr), tail_scr = rest[:5], rest[5:9], rest[9:11], rest[11:]
    first = pl.program_id(0) == 0
    x = _with_pending(x_ref, pend)
    h = (_rms(x, g1_ref[...]) * (1.0 + sc1_ref[...]) + sh1_ref[...]).astype(BF16)

    def proj_u(g):
        return jnp.dot(h, win_ref[:, g * SG_GW:(g + 1) * SG_GW], preferred_element_type=F32)

    v = _gelu_tanh(jnp.dot(h, win_ref[:, SG_W:], preferred_element_type=F32))
    mu = jnp.mean(v, axis=-1, keepdims=True)
    vc = v - mu
    var = jnp.mean(vc * vc, axis=-1, keepdims=True)
    v_scr[...] = (vc * lax.rsqrt(var + EPS) * lng_ref[...] + lnb_ref[...]).astype(BF16)

    u_next = proj_u(0)
    for g in range(SG_GROUPS):
        cs = slice(g * SG_GW, (g + 1) * SG_GW)
        u_raw, u_next = u_next, (proj_u(g + 1) if g + 1 < SG_GROUPS else None)
        w_sp = ws_ref[g].astype(BF16)
        chunks = [slice(c * SG_CHUNK, (c + 1) * SG_CHUNK) for c in range(tm // SG_CHUNK)]
        sps = [jnp.dot(w_sp, v_scr[rs, cs], preferred_element_type=F32) for rs in chunks]
        u = _gelu_tanh(u_raw)
        for rs, sp in zip(chunks, sps):
            m_scr[rs, cs] = (u[rs] * (sp + bst_ref[:, g:g + 1])).astype(BF16)
    out = jnp.dot(m_scr[...], wout_ref[...], preferred_element_type=F32)
    _moe_input(x + gate_ref[...] * out, first, tail_in, tail_out, tail_scr)


def _sgu_layer(x, pend, mods, g1, w_in, ln_g, ln_b, w_s, b_s, w_out, g2, router):
    tm = 512
    pend = tuple(pend) if pend else ()
    tail_in, out_specs, out_shape, tail_scr = _tail_specs(tm)
    const = lambda shape: pl.BlockSpec(shape, lambda i: (0,) * len(shape))
    held = lambda shape: pl.BlockSpec(shape, lambda i: (0,) * len(shape), pipeline_mode=pl.Buffered(1))
    return pl.pallas_call(
        functools.partial(_sgu_kernel, tm=tm, n_pend=len(pend)), grid=(N_TOK // tm,),
        in_specs=[pl.BlockSpec((tm, D_MODEL), lambda i: (i, 0))] + (_pending_specs(tm) if pend else [])
        + [const((1, D_MODEL)), _mod_spec(0, tm), _mod_spec(1, tm),
                  held((D_MODEL, 2 * SG_W)), const((1, SG_W)), const((1, SG_W)),
                  const((SG_GROUPS, SG_CHUNK, SG_CHUNK)), const((SG_CHUNK, SG_GROUPS)),
                  held((SG_W, D_MODEL))] + tail_in,
        out_specs=out_specs, out_shape=out_shape,
        scratch_shapes=[pltpu.VMEM((tm, SG_W), BF16), pltpu.VMEM((tm, SG_W), BF16)] + tail_scr,
        compiler_params=_params(("arbitrary",)), name="sgu_layer",
    )(x, *pend, g1.reshape(1, D_MODEL), mods, mods, w_in.astype(BF16), ln_g.reshape(1, SG_W), ln_b.reshape(1, SG_W),
      w_s, b_s.T, w_out.astype(BF16), mods, g2.reshape(1, D_MODEL), mods, mods, *router)


def _plan(rec, cnt):
    e_idx = rec[:, _R_E:_R_E + 2].astype(jnp.int32)
    rank = rec[:, _R_RANK:_R_RANK + 2].astype(jnp.int32)
    counts = cnt[0, _LOGIT0:_LOGIT0 + N_EXPERTS].astype(jnp.int32)
    padded = (counts + MOE_BLK - 1) // MOE_BLK * MOE_BLK
    pad_end = jnp.cumsum(padded)
    pad_start = pad_end - padded
    hit = e_idx[:, :, None] == jnp.arange(N_EXPERTS, dtype=jnp.int32)[None, None, :]
    dest = jnp.sum(jnp.where(hit, pad_start[None, None, :], 0), axis=-1) + rank
    blk0 = jnp.arange(MOE_NBLK, dtype=jnp.int32) * MOE_BLK
    blk_e = jnp.minimum(jnp.sum((pad_end[None, :] <= blk0[:, None]).astype(jnp.int32), axis=-1),
                        N_EXPERTS - 1)
    n_used = (pad_end[-1] // MOE_BLK).astype(jnp.int32).reshape(1)
    owns = counts > 0
    slot_of = (jnp.cumsum(owns.astype(jnp.int32)) - 1) % _W_SLOTS
    ids = jnp.arange(N_EXPERTS, dtype=jnp.int32)
    later = jnp.logical_and(owns[None, :], ids[None, :] > ids[:, None])
    next_of = jnp.min(jnp.where(later, ids[None, :], N_EXPERTS), axis=-1)
    next2_of = jnp.concatenate([next_of, jnp.full((1,), N_EXPERTS, jnp.int32)])[next_of]
    ahead = jnp.stack([next_of, next2_of], axis=0)
    ahead = jnp.where(ahead == N_EXPERTS, -1, ahead)
    return dest, blk_e, n_used, slot_of[blk_e], ahead[:, blk_e].reshape(-1)


_W_PARTS = 4
_W_SLOTS = 3


def _expert_kernel(blk_e_ref, n_used_ref, slot_ref, next_ref, x_ref, wg_hbm, wu_hbm, wd_hbm, o_ref,
                   wg_buf, wu_buf, wd_buf, wg_scr, wu_scr, wd_scr, sems, *, layer):
    j = pl.program_id(0)
    e = blk_e_ref[j]
    slot = slot_ref[j]
    fresh = jnp.logical_or(j == 0, e != blk_e_ref[jnp.maximum(j - 1, 0)])
    live = j < n_used_ref[0]

    def copies(expert, s):
        out = []
        for m, (hbm, buf) in enumerate(((wg_hbm, wg_buf), (wu_hbm, wu_buf), (wd_hbm, wd_buf))):
            rows = buf.shape[1] // _W_PARTS
            for part in range(_W_PARTS):
                band = pl.ds(part * rows, rows)
                out.append(pltpu.make_async_copy(hbm.at[layer, expert, band], buf.at[s, band],
                                                 sems.at[s, m, part]))
        return out

    def start_if_any(expert, s):
        @pl.when(expert >= 0)
        def _():
            for cp in copies(expert, s):
                cp.start()

    @pl.when(j == 0)
    def _():
        for cp in copies(e, slot):
            cp.start()
        start_if_any(next_ref[j], lax.rem(slot + 1, _W_SLOTS))

    @pl.when(jnp.logical_and(fresh, live))
    def _():
        for cp in copies(e, slot):
            cp.wait()
        start_if_any(next_ref[MOE_NBLK + j], lax.rem(slot + 2, _W_SLOTS))

        wg_scr[...] = wg_buf[slot].astype(BF16)
        wu_scr[...] = wu_buf[slot].astype(BF16)
        wd_scr[...] = wd_buf[slot].astype(BF16)

    @pl.when(live)
    def _():
        x = _unpack_rows(x_ref[...])
        gt = jnp.dot(x, wg_scr[...], preferred_element_type=F32)
        up = jnp.dot(x, wu_scr[...], preferred_element_type=F32)
        hb = (_silu(gt) * up).astype(BF16)
        o_ref[...] = jnp.dot(hb, wd_scr[...], preferred_element_type=F32).astype(o_ref.dtype)

    @pl.when(jnp.logical_not(live))
    def _():
        o_ref[...] = jnp.zeros(o_ref.shape, o_ref.dtype)


def _experts(x_pad, blk_e, n_used, slot, nxt, w_gate, w_up, w_down, layer):
    hbm = pl.BlockSpec(memory_space=pl.ANY)
    grid_spec = pltpu.PrefetchScalarGridSpec(
        num_scalar_prefetch=4, grid=(MOE_NBLK,),
        in_specs=[pl.BlockSpec((MOE_BLK, _PACK_W), lambda j, be, nu, *_: (jnp.minimum(j, nu[0] - 1), 0)),
                  hbm, hbm, hbm],
        out_specs=pl.BlockSpec((MOE_BLK, D_MODEL), lambda j, *_: (j, 0)),
        scratch_shapes=[pltpu.VMEM((_W_SLOTS, D_MODEL, D_EXPERT), F32),
                        pltpu.VMEM((_W_SLOTS, D_MODEL, D_EXPERT), F32),
                        pltpu.VMEM((_W_SLOTS, D_EXPERT, D_MODEL), F32),
                        pltpu.VMEM((D_MODEL, D_EXPERT), BF16), pltpu.VMEM((D_MODEL, D_EXPERT), BF16),
                        pltpu.VMEM((D_EXPERT, D_MODEL), BF16),
                        pltpu.SemaphoreType.DMA((_W_SLOTS, 3, _W_PARTS))])
    return pl.pallas_call(
        functools.partial(_expert_kernel, layer=layer), grid_spec=grid_spec,
        out_shape=jax.ShapeDtypeStruct((MOE_NBLK * MOE_BLK, D_MODEL), BF16),
        compiler_params=_params(("arbitrary",)), name="experts",
    )(blk_e, n_used, slot, nxt, x_pad, w_gate, w_up, w_down)


def _final_kernel(x_ref, y2_ref, rec_ref, gate_ref, fg_ref, o_ref):
    o_ref[...] = _rms(_with_pending(x_ref, (y2_ref, rec_ref, gate_ref)), fg_ref[...])


def _final_norm(x, pend, final_g, row0, n_rows):
    tm = 512
    tile0 = row0 // tm
    return pl.pallas_call(
        _final_kernel, grid=(n_rows // tm,),
        in_specs=[pl.BlockSpec((tm, D_MODEL), lambda i: (i + tile0, 0))] + _pending_specs(tm, tile0)
        + [pl.BlockSpec((1, D_MODEL), lambda i: (0, 0))],
        out_specs=pl.BlockSpec((tm, D_MODEL), lambda i: (i, 0)),
        out_shape=jax.ShapeDtypeStruct((n_rows, D_MODEL), F32),
        compiler_params=_params(("parallel",)), name="final_norm",
    )(x, *pend, final_g.reshape(1, D_MODEL))


_SC_WORKERS = 32
_SC_CORES = 2
_SC_ROWS = 64


def _dispatch_rows(hp, dest):
    n, width = hp.shape
    per_w = n // _SC_WORKERS
    n_ch = per_w // _SC_ROWS
    idx = dest.T.reshape(2, _SC_WORKERS, n_ch, _SC_ROWS)
    mesh = plsc.VectorSubcoreMesh(core_axis_name="c", subcore_axis_name="s")

    @functools.partial(
        pl.kernel, mesh=mesh, out_type=jax.ShapeDtypeStruct((MOE_NBLK * MOE_BLK, width), hp.dtype),
        scratch_types=[pltpu.VMEM((n_ch, _SC_ROWS), jnp.int32), pltpu.VMEM((n_ch, _SC_ROWS), jnp.int32),
                       pltpu.VMEM((_SC_ROWS, width), hp.dtype)], name="dispatch_rows")
    def scatter(h_hbm, idx_hbm, out_hbm, i0_v, i1_v, rows_v):
        wid = lax.axis_index("s") * _SC_CORES + lax.axis_index("c")
        pltpu.sync_copy(idx_hbm.at[0, wid], i0_v)
        pltpu.sync_copy(idx_hbm.at[1, wid], i1_v)

        @pl.loop(0, n_ch)
        def _(g):
            pltpu.sync_copy(h_hbm.at[pl.ds(wid * per_w + g * _SC_ROWS, _SC_ROWS)], rows_v)
            pltpu.sync_copy(rows_v, out_hbm.at[i0_v.at[g]])
            pltpu.sync_copy(rows_v, out_hbm.at[i1_v.at[g]])

    return scatter(hp, idx)


def _moe(h, rec, cnt, mods, w_gate, w_up, w_down, layer):
    dest, blk_e, n_used, slot, nxt = _plan(rec, cnt)
    y_pad = _experts(_dispatch_rows(h, dest), blk_e, n_used, slot, nxt, w_gate, w_up, w_down, layer)
    order = dest.reshape(N_TOK // _PEND_TM, _PEND_TM, 2).transpose(0, 2, 1).reshape(-1)
    return y_pad[order], rec, mods


def kernel(x_prompt, x_sample, c, cache_k, cache_v, state_delta, c_ctx, ada_w, ada_b, norm1_g, norm2_g, final_g,
           ev_w_in, ev_w_out, ev_conv_w, ev_a_log, ev_dt_bias, ev_onorm_g, ev_rpb, od_w_in, od_ln_g, od_ln_b,
           od_w_s, od_b_s, od_w_out, moe_w_rg, moe_b_rg, moe_w_re, moe_b_re, moe_w_gate, moe_w_up, moe_w_down):
    x = (x_prompt.reshape(N_CTX, D_MODEL), x_sample.reshape(N_LAT, D_MODEL))
    cond = jnp.concatenate([c_ctx[None, :], c, jnp.zeros((N_COND - 1 - DEC_BATCH, D_MODEL), F32)], axis=0)
    mods_all = _ada_mods(cond, ada_w, ada_b)
    kctx_all = cache_k.reshape(DEC_BATCH, -1, PAST_LEN, NA_HEADS * NA_HD)
    vctx_all = cache_v.reshape(DEC_BATCH, -1, PAST_LEN, NA_HEADS * NA_HD)

    ks, vs, ss = [], [], []
    pend = None
    for l in range(DEPTH):
        mods = mods_all[l]
        router = _router_weights(moe_w_rg[l], moe_b_rg[l], moe_w_re[l], moe_b_re[l])
        if l % 2 == 0:
            e = l // 2
            proj, ab, kv, x = _even_proj(x, pend, mods, norm1_g[l], ev_w_in[e])
            dn = (proj, ab, ev_conv_w[e], ev_a_log[e], ev_dt_bias[e], ev_onorm_g[e])
            oa_ctx, s_fin = _delta_heads(*dn, SEQ, BATCH, 0, None)
            oa_lat, _ = _delta_heads(*dn, DEC_SEQ, DEC_BATCH, N_CTX // DEC_SEQ, (state_delta, e))
            ob_ctx = _ctx_attention(proj)
            ob_lat = _na_attention(proj, kctx_all, vctx_all, e, ev_rpb[e])
            x, h, rec, cnt = _even_out(oa_ctx, ob_ctx, oa_lat, ob_lat, x, ev_w_out, e, mods, norm2_g[l],
                                       router)
            na_w = NA_HEADS * NA_HD
            ks.append(kv[:N_CTX, :na_w].reshape(BATCH, SEQ, NA_HEADS, NA_HD))
            vs.append(kv[:N_CTX, na_w:].reshape(BATCH, SEQ, NA_HEADS, NA_HD))
            ss.append(s_fin)
        else:
            o = l // 2
            x, h, rec, cnt = _sgu_layer(x, pend, mods, norm1_g[l], od_w_in[o], od_ln_g[o], od_ln_b[o],
                                        od_w_s[o], od_b_s[o], od_w_out[o], norm2_g[l], router)
        pend = _moe(h, rec, cnt, mods, moe_w_gate, moe_w_up, moe_w_down, l)
    y_prompt = _final_norm(x, pend, final_g, 0, N_CTX).reshape(BATCH, SEQ, D_MODEL)
    y_sample = _final_norm(x, pend, final_g, N_CTX, N_LAT).reshape(DEC_BATCH, DEC_SEQ, D_MODEL)
    return (y_prompt, y_sample, jnp.stack(ks, axis=1), jnp.stack(vs, axis=1), jnp.stack(ss, axis=1))
```
